```python
import jax, jax.numpy as jnp
from jax import lax
import numpy as np

D_MODEL = 1024
BATCH = 8
SEQ = 2048
DEPTH = 1
DEC_BATCH = 128
DEC_SEQ = 1
PAST_LEN = 16384
PAGE_SIZE = 128

N_HEADS = 8
N_KV_HEADS = 2
HEAD_DIM = 64
Q_GROUP = N_HEADS // N_KV_HEADS
ATTN_DIM = N_HEADS * HEAD_DIM
KV_DIM = N_KV_HEADS * HEAD_DIM
WINDOW = 128
ATTN_BLOCK = WINDOW
ROPE_THETA = 10000.0
ATTN_SCALE = HEAD_DIM ** -0.5
CONV_CH = D_MODEL - ATTN_DIM
CONV_WIDTH = 31
MIX_DIM = ATTN_DIM + CONV_CH
IN_DIM = ATTN_DIM + 2 * KV_DIM + 2 * CONV_CH
N_GROUPS = 4
EXPERTS_PER_GROUP = 8
N_EXPERTS = N_GROUPS * EXPERTS_PER_GROUP
TOP_K_IN_GROUP = 2
D_FF_EXPERT = 256
DISPATCH_BLOCK = 128
LN_EPS = 1e-5
NEG_INF = -1e30
ALPHA = (2.0 * DEPTH) ** 0.25
BETA = (8.0 * DEPTH) ** -0.25

kernel_name = "hymba_swa_sink_conformer_hiermoe_step"


def layer_norm(x, g, b):
    xf = x.astype(jnp.float32)
    mu = jnp.mean(xf, axis=-1, keepdims=True)
    var = jnp.mean(jnp.square(xf - mu), axis=-1, keepdims=True)
    y = (xf - mu) * lax.rsqrt(var + LN_EPS) * g.astype(jnp.float32) + b.astype(jnp.float32)
    return y.astype(x.dtype)


def rope(x, pos):
    half = HEAD_DIM // 2
    inv = ROPE_THETA ** (-jnp.arange(half, dtype=jnp.float32) / half)
    ang = pos.astype(jnp.float32)[:, None] * inv
    cos = jnp.cos(ang)[:, None, :]
    sin = jnp.sin(ang)[:, None, :]
    x1 = x[..., :half].astype(jnp.float32)
    x2 = x[..., half:].astype(jnp.float32)
    return jnp.concatenate([x1 * cos - x2 * sin, x2 * cos + x1 * sin], axis=-1).astype(x.dtype)


def window_mask(qpos, kpos):
    d = qpos[..., :, None] - kpos[..., None, :]
    return (d >= 0) & (d < WINDOW) & (kpos[..., None, :] >= 0)


def attend(q, k, v, mask, sinks):
    s = jnp.einsum('...qkgd,...skd->...kgqs', q, k).astype(jnp.float32) * ATTN_SCALE
    s = jnp.where(mask, s, NEG_INF)
    sink = jnp.broadcast_to(sinks.astype(jnp.float32).reshape(N_KV_HEADS, Q_GROUP, 1, 1), s.shape[:-1] + (1,))
    p = jax.nn.softmax(jnp.concatenate([s, sink], axis=-1), axis=-1)[..., :-1]
    return jnp.einsum('...kgqs,...skd->...qkgd', p.astype(v.dtype), v)


def banded_window_attention(q, k, v, sinks):
    b, t = q.shape[0], q.shape[1]
    nblk = t // ATTN_BLOCK
    qb = q.reshape(b, nblk, ATTN_BLOCK, N_KV_HEADS, Q_GROUP, HEAD_DIM)

    def band(z):
        zp = jnp.pad(z, ((0, 0), (ATTN_BLOCK, 0), (0, 0), (0, 0)))
        zp = zp.reshape(b, nblk + 1, ATTN_BLOCK, N_KV_HEADS, HEAD_DIM)
        return jnp.concatenate([zp[:, :-1], zp[:, 1:]], axis=2)

    qpos = jnp.arange(t).reshape(nblk, ATTN_BLOCK)
    kp_all = jnp.arange(-ATTN_BLOCK, t).reshape(nblk + 1, ATTN_BLOCK)
    kpos = jnp.concatenate([kp_all[:-1], kp_all[1:]], axis=-1)
    mask = window_mask(qpos, kpos)
    out = attend(qb, band(k), band(v), mask[:, None, None], sinks)
    return out.reshape(b, t, ATTN_DIM)


def cached_window_attention(q, keys, vals, pos, sinks):
    b, t = q.shape[0], q.shape[1]
    kpos = pos[0] - WINDOW + jnp.arange(WINDOW + t)
    mask = window_mask(pos, kpos)
    out = attend(q, keys, vals, mask[None, None, None], sinks)
    return out.reshape(b, t, ATTN_DIM)


def depthwise_causal_conv(u_ext, w, bias):
    y = lax.conv_general_dilated(u_ext, w[:, None, :].astype(u_ext.dtype), window_strides=(1,),
                                 padding='VALID', dimension_numbers=('NWC', 'WIO', 'NWC'),
                                 feature_group_count=CONV_CH)
    return y + bias


def hier_moe(x2d, w_rg, w_re, w1, w3, w2):
    n = x2d.shape[0]
    lg = jnp.einsum('nd,dg->ng', x2d, w_rg).astype(jnp.float32)
    p_group = jax.nn.softmax(lg, axis=-1)
    g_idx = jnp.argmax(lg, axis=-1)
    p_g = jnp.take_along_axis(p_group, g_idx[:, None], axis=-1)[:, 0]
    le = jnp.einsum('nd,dge->nge', x2d, w_re).astype(jnp.float32)
    le_sel = jnp.take_along_axis(le, g_idx[:, None, None], axis=1)[:, 0]
    top_v, top_i = lax.top_k(le_sel, TOP_K_IN_GROUP)
    gate = p_g[:, None] * jax.nn.softmax(top_v, axis=-1)
    expert = g_idx[:, None] * EXPERTS_PER_GROUP + top_i

    nk = n * TOP_K_IN_GROUP
    e_flat = expert.reshape(-1)
    tok = jnp.arange(nk) // TOP_K_IN_GROUP
    counts = jnp.zeros((N_EXPERTS,), jnp.int32).at[e_flat].add(1)
    padded = (counts + DISPATCH_BLOCK - 1) // DISPATCH_BLOCK * DISPATCH_BLOCK
    pad_end = jnp.cumsum(padded)
    pad_start = pad_end - padded
    start = jnp.cumsum(counts) - counts
    order = jnp.argsort(e_flat, stable=True)
    e_sorted = e_flat[order]
    dest_sorted = pad_start[e_sorted] + (jnp.arange(nk) - start[e_sorted])
    dest = jnp.zeros((nk,), jnp.int32).at[order].set(dest_sorted.astype(jnp.int32))
    n_blocks = (nk + DISPATCH_BLOCK - 1) // DISPATCH_BLOCK + N_EXPERTS
    xbuf = jnp.zeros((n_blocks * DISPATCH_BLOCK, x2d.shape[1]), x2d.dtype).at[dest].set(x2d[tok])
    block_expert = jnp.clip(jnp.searchsorted(pad_end, jnp.arange(n_blocks) * DISPATCH_BLOCK, side='right'),
                            0, N_EXPERTS - 1)

    def run_block(args):
        xb, e = args
        h = jax.nn.silu(xb @ w1[e]) * (xb @ w3[e])
        return h @ w2[e]

    ybuf = lax.map(run_block, (xbuf.reshape(n_blocks, DISPATCH_BLOCK, -1), block_expert))
    y = ybuf.reshape(n_blocks * DISPATCH_BLOCK, -1)[dest].reshape(n, TOP_K_IN_GROUP, -1)
    return jnp.einsum('nk,nkd->nd', gate.astype(x2d.dtype), y)


def decoder_layer(x, pos, prev_k, prev_v, conv_prefix, w_in, b_in, sinks, conv_w, conv_b,
                  ln_conv_g, ln_conv_b, w_out, b_out, ln1_g, ln1_b, w_rg, w_re, w1, w3, w2,
                  ln2_g, ln2_b):
    nb, t, _ = x.shape
    proj = jnp.einsum('btd,de->bte', x, w_in) + b_in
    q, k, v, ca, cg = jnp.split(proj, [ATTN_DIM, ATTN_DIM + KV_DIM, ATTN_DIM + 2 * KV_DIM,
                                       ATTN_DIM + 2 * KV_DIM + CONV_CH], axis=-1)
    q = rope(q.reshape(nb, t, N_HEADS, HEAD_DIM), pos).reshape(nb, t, N_KV_HEADS, Q_GROUP, HEAD_DIM)
    k = rope(k.reshape(nb, t, N_KV_HEADS, HEAD_DIM), pos)
    v = v.reshape(nb, t, N_KV_HEADS, HEAD_DIM)
    if prev_k is None:
        attn = banded_window_attention(q, k, v, sinks)
        keys, vals = k, v
    else:
        keys = jnp.concatenate([prev_k, k], axis=1)
        vals = jnp.concatenate([prev_v, v], axis=1)
        attn = cached_window_attention(q, keys, vals, pos, sinks)
    new_k = keys[:, -WINDOW:]
    new_v = vals[:, -WINDOW:]

    u = ca * jax.nn.sigmoid(cg)
    u_ext = jnp.concatenate([conv_prefix, u], axis=1)
    conv = jax.nn.silu(layer_norm(depthwise_causal_conv(u_ext, conv_w, conv_b), ln_conv_g, ln_conv_b))
    new_conv = u_ext[:, -(CONV_WIDTH - 1):]

    mix = jnp.einsum('bte,ed->btd', jnp.concatenate([attn, conv], axis=-1), w_out) + b_out
    h = layer_norm(ALPHA * x + mix, ln1_g, ln1_b)
    f = hier_moe(h.reshape(nb * t, D_MODEL), w_rg, w_re, w1, w3, w2).reshape(nb, t, D_MODEL)
    y = layer_norm(ALPHA * h + f, ln2_g, ln2_b)
    return y, new_k, new_v, new_conv


def setup_inputs(seed: int = 0) -> dict:
    key = jax.random.key(seed)
    ks = jax.random.split(key, 24)

    def nrm(k, shape, scale):
        return jax.random.normal(k, shape, jnp.float32) * scale

    col_scale = jnp.ones((IN_DIM,), jnp.float32).at[ATTN_DIM + KV_DIM:ATTN_DIM + 2 * KV_DIM].set(BETA)
    return {
        "x_prompt": nrm(ks[0], (BATCH, SEQ, D_MODEL), 1.0),
        "x_sample": nrm(ks[1], (DEC_BATCH, DEC_SEQ, D_MODEL), 1.0),
        "cache_k": nrm(ks[2], (DEPTH, DEC_BATCH, WINDOW, N_KV_HEADS, HEAD_DIM), 1.0),
        "cache_v": nrm(ks[3], (DEPTH, DEC_BATCH, WINDOW, N_KV_HEADS, HEAD_DIM), 0.5),
        "state_conv": nrm(ks[4], (DEPTH, DEC_BATCH, CONV_WIDTH - 1, CONV_CH), 0.5),
        "w_in": nrm(ks[5], (DEPTH, D_MODEL, IN_DIM), D_MODEL ** -0.5) * col_scale,
        "b_in": nrm(ks[6], (DEPTH, IN_DIM), 0.02),
        "sinks": nrm(ks[7], (DEPTH, N_HEADS), 1.0),
        "conv_w": nrm(ks[8], (DEPTH, CONV_WIDTH, CONV_CH), CONV_WIDTH ** -0.5),
        "conv_b": nrm(ks[9], (DEPTH, CONV_CH), 0.02),
        "ln_conv_g": 1.0 + nrm(ks[10], (DEPTH, CONV_CH), 0.05),
        "ln_conv_b": nrm(ks[11], (DEPTH, CONV_CH), 0.02),
        "w_out": nrm(ks[12], (DEPTH, MIX_DIM, D_MODEL), MIX_DIM ** -0.5) * BETA,
        "b_out": nrm(ks[13], (DEPTH, D_MODEL), 0.02),
        "ln1_g": 1.0 + nrm(ks[14], (DEPTH, D_MODEL), 0.05),
        "ln1_b": nrm(ks[15], (DEPTH, D_MODEL), 0.02),
        "w_router_group": nrm(ks[16], (DEPTH, D_MODEL, N_GROUPS), D_MODEL ** -0.5),
        "w_router_expert": nrm(ks[17], (DEPTH, D_MODEL, N_GROUPS, EXPERTS_PER_GROUP), D_MODEL ** -0.5),
        "w1": nrm(ks[18], (DEPTH, N_EXPERTS, D_MODEL, D_FF_EXPERT), D_MODEL ** -0.5),
        "w3": nrm(ks[19], (DEPTH, N_EXPERTS, D_MODEL, D_FF_EXPERT), D_MODEL ** -0.5),
        "w2": nrm(ks[20], (DEPTH, N_EXPERTS, D_FF_EXPERT, D_MODEL), D_FF_EXPERT ** -0.5) * BETA,
        "ln2_g": 1.0 + nrm(ks[21], (DEPTH, D_MODEL), 0.05),
        "ln2_b": nrm(ks[22], (DEPTH, D_MODEL), 0.02),
    }


def reference(x_prompt, x_sample, cache_k, cache_v, state_conv, w_in, b_in, sinks, conv_w, conv_b,
              ln_conv_g, ln_conv_b, w_out, b_out, ln1_g, ln1_b, w_router_group, w_router_expert,
              w1, w3, w2, ln2_g, ln2_b):
    pos_p = jnp.arange(x_prompt.shape[1])
    pos_s = PAST_LEN + jnp.arange(x_sample.shape[1])
    zero_prefix = jnp.zeros((x_prompt.shape[0], CONV_WIDTH - 1, CONV_CH), x_prompt.dtype)
    yp, ys = x_prompt, x_sample
    kp_l, vp_l, cp_l, ks_l, vs_l, cs_l = [], [], [], [], [], []
    for l in range(DEPTH):
        params = (w_in[l], b_in[l], sinks[l], conv_w[l], conv_b[l], ln_conv_g[l], ln_conv_b[l],
                  w_out[l], b_out[l], ln1_g[l], ln1_b[l], w_router_group[l], w_router_expert[l],
                  w1[l], w3[l], w2[l], ln2_g[l], ln2_b[l])
        yp, kp, vp, cp = decoder_layer(yp, pos_p, None, None, zero_prefix, *params)
        ys, k_s, v_s, c_s = decoder_layer(ys, pos_s, cache_k[l], cache_v[l], state_conv[l], *params)
        kp_l.append(kp); vp_l.append(vp); cp_l.append(cp)
        ks_l.append(k_s); vs_l.append(v_s); cs_l.append(c_s)
    new_k_prompt = jnp.stack(kp_l)
    new_v_prompt = jnp.stack(vp_l)
    new_conv_prompt = jnp.stack(cp_l)
    new_k_sample = jnp.stack(ks_l)
    new_v_sample = jnp.stack(vs_l)
    new_conv_sample = jnp.stack(cs_l)
    return (yp, ys, new_k_prompt, new_v_prompt, new_conv_prompt, new_k_sample, new_v_sample, new_conv_sample)
```

```python
import functools

import jax
import jax.numpy as jnp
from jax import lax
from jax.experimental import pallas as pl
from jax.experimental.pallas import tpu as pltpu

F32 = jnp.float32
BF16 = jnp.bfloat16
U32 = jnp.uint32

D_MODEL = 1024
N_HEADS = 8
N_KV_HEADS = 2
HEAD_DIM = 64
Q_GROUP = N_HEADS // N_KV_HEADS
ATTN_DIM = N_HEADS * HEAD_DIM
KV_DIM = N_KV_HEADS * HEAD_DIM
WINDOW = 128
ROPE_THETA = 10000.0
ATTN_SCALE = HEAD_DIM ** -0.5
CONV_CH = D_MODEL - ATTN_DIM
CONV_WIDTH = 31
IN_DIM = ATTN_DIM + 2 * KV_DIM + 2 * CONV_CH
N_GROUPS = 4
EXPERTS_PER_GROUP = 8
N_EXPERTS = N_GROUPS * EXPERTS_PER_GROUP
D_FF_EXPERT = 256
LN_EPS = 1e-5
NEG_INF = -1e30
DEPTH = 1
ALPHA = (2.0 * DEPTH) ** 0.25

LANES = 128
SUBLANES = 8
HALF = D_MODEL // 2
ELANE0 = N_GROUPS
PROMPT_SCOPE = 256
AREA_ROWS = 768
AREA_CHUNKS = AREA_ROWS // SUBLANES
BLOCK_CHUNKS = 32
BLOCK_ROWS = BLOCK_CHUNKS * SUBLANES
TILE_Q = 512
SAMPLE_TB = 32
SAMPLE_GRP = 8
VMEM_LIMIT = 56 * 1024 * 1024


def _layer_norm(x, g, b):
    mu = jnp.mean(x, axis=-1, keepdims=True)
    xc = x - mu
    var = jnp.mean(xc * xc, axis=-1, keepdims=True)
    return xc * lax.rsqrt(var + LN_EPS) * g + b


def _swap_halves(x):
    n = x.shape[-1]
    lane = lax.broadcasted_iota(jnp.int32, x.shape, x.ndim - 1)
    first = (lane & (HEAD_DIM // 2)) == 0
    return jnp.where(first, pltpu.roll(x, n - HEAD_DIM // 2, x.ndim - 1), pltpu.roll(x, HEAD_DIM // 2, x.ndim - 1))


def _rope(x, cos, sin_signed):
    reps = x.shape[-1] // LANES
    if reps > 1:
        cos = jnp.concatenate([cos] * reps, axis=-1)
        sin_signed = jnp.concatenate([sin_signed] * reps, axis=-1)
    return x * cos + _swap_halves(x) * sin_signed


def _pack_rows(v):
    lo = lax.bitcast_convert_type(v[:, :HALF], U32)
    hi = lax.bitcast_convert_type(v[:, HALF:], U32)
    return (lo >> 16) | (hi & jnp.uint32(0xFFFF0000))


def _unpack_rows(w):
    lo = lax.bitcast_convert_type(w << 16, F32).astype(BF16)
    hi = lax.bitcast_convert_type(w & jnp.uint32(0xFFFF0000), F32).astype(BF16)
    return lo, hi


def _route_and_sort(h, w_r, scope):
    n = h.shape[0]
    hb = h.astype(BF16)
    logits = jnp.dot(hb, w_r, preferred_element_type=F32)
    lane = lax.broadcasted_iota(jnp.int32, (n, LANES), 1)
    lane_f = lane.astype(F32)
    is_g = lane < N_GROUPS
    lg = jnp.where(is_g, logits, -jnp.inf)
    gmax = jnp.max(lg, axis=-1, keepdims=True)
    gidx = jnp.min(jnp.where(lg == gmax, lane_f, float(LANES)), axis=-1, keepdims=True)
    p_g = 1.0 / jnp.sum(jnp.where(is_g, jnp.exp(logits - gmax), 0.0), axis=-1, keepdims=True)
    egrp = ((lane - ELANE0) >> 3).astype(F32)
    emask = (lane >= ELANE0) & (lane < ELANE0 + N_EXPERTS) & (egrp == gidx)
    sel = jnp.where(emask, logits, -jnp.inf)
    v1 = jnp.max(sel, axis=-1, keepdims=True)
    i1 = jnp.min(jnp.where(sel == v1, lane_f, float(LANES)), axis=-1, keepdims=True)
    sel2 = jnp.where(lane_f == i1, -jnp.inf, sel)
    v2 = jnp.max(sel2, axis=-1, keepdims=True)
    i2 = jnp.min(jnp.where(sel2 == v2, lane_f, float(LANES)), axis=-1, keepdims=True)
    t = jnp.exp(v2 - v1)
    gate1 = p_g / (1.0 + t)
    gate2 = p_g * t / (1.0 + t)
    pick1 = lane_f == i1
    pick2 = lane_f == i2
    member = jnp.where(pick1 | pick2, 1.0, 0.0)

    r_i = lax.broadcasted_iota(jnp.int32, (scope, scope), 0)
    c_i = lax.broadcasted_iota(jnp.int32, (scope, scope), 1)
    lower = jnp.where(c_i < r_i, 1.0, 0.0).astype(BF16)
    e_r = lax.broadcasted_iota(jnp.int32, (LANES, LANES), 0)
    e_c = lax.broadcasted_iota(jnp.int32, (LANES, LANES), 1)
    upper = jnp.where(e_r < e_c, 1.0, 0.0).astype(BF16)
    s_iota = lax.broadcasted_iota(jnp.int32, (scope, AREA_ROWS), 1).astype(F32)

    slots1, slots2, areas, counts = [], [], [], []
    for sc in range(n // scope):
        rows = slice(sc * scope, (sc + 1) * scope)
        m_sc = member[rows]
        before = jnp.dot(lower, m_sc.astype(BF16), preferred_element_type=F32)
        cnt = jnp.sum(m_sc, axis=0, keepdims=True)
        c8 = jnp.floor((cnt + (SUBLANES - 1)) * (1.0 / SUBLANES))
        c8b = jnp.broadcast_to(c8, (SUBLANES, LANES)).astype(BF16)
        off8 = jnp.dot(c8b, upper, preferred_element_type=F32)[0:1]
        slot_all = before + off8 * float(SUBLANES)
        s1 = jnp.sum(jnp.where(pick1[rows], slot_all, 0.0), axis=-1, keepdims=True)
        s2 = jnp.sum(jnp.where(pick2[rows], slot_all, 0.0), axis=-1, keepdims=True)
        perm_t = jnp.where((s_iota == s1) | (s_iota == s2), 1.0, 0.0).astype(BF16)
        sorted_rows = lax.dot_general(perm_t, hb[rows], (((0,), (0,)), ((), ())),
                                      preferred_element_type=F32)
        areas.append(_pack_rows(sorted_rows))
        counts.append(cnt)
        slots1.append(s1)
        slots2.append(s2)
    slot1 = jnp.concatenate(slots1, axis=0) if len(slots1) > 1 else slots1[0]
    slot2 = jnp.concatenate(slots2, axis=0) if len(slots2) > 1 else slots2[0]
    route = jnp.where(lane == 0, slot1, jnp.where(lane == 1, slot2, jnp.where(lane == 2, gate1,
                      jnp.where(lane == 3, gate2, 0.0))))
    return route, areas, counts


def _softmax_rows_with_sink(s, sink, extra=None):
    m = jnp.maximum(jnp.max(s, axis=-1, keepdims=True), sink)
    if extra is not None:
        m = jnp.maximum(m, extra)
    e = jnp.exp(s - m)
    den = jnp.sum(e, axis=-1, keepdims=True) + jnp.exp(sink - m)
    if extra is not None:
        ee = jnp.exp(extra - m)
        den = den + ee
        return e / den, ee / den
    return e / den, None


def _mix_tail(x, attn, conv_pre, lcg, lcb, w_out_ref, b_out, l1g, l1b):
    conv = _layer_norm(conv_pre, lcg, lcb)
    conv = conv * jax.nn.sigmoid(conv)
    mix = (jnp.dot(attn.astype(BF16), w_out_ref[0:ATTN_DIM, :], preferred_element_type=F32)
           + jnp.dot(conv.astype(BF16), w_out_ref[ATTN_DIM:, :], preferred_element_type=F32) + b_out)
    return _layer_norm(ALPHA * x + mix, l1g, l1b)


def _mix_prompt_kernel(sinks_ref, x_ref, cos_ref, sin_ref, xs_s_ref, *rest, tiles):
    xs_ref = rest[12]
    cnt_ref = rest[14]
    step = pl.program_id(0)
    n_real = pl.num_programs(0) - 1

    @pl.when(step < n_real)
    def _():
        _mix_prompt_tile(sinks_ref, x_ref, cos_ref, sin_ref, *rest, i=step % tiles, last=tiles - 1)

    @pl.when(step == n_real)
    def _():
        xs_ref[0:AREA_ROWS, :] = xs_s_ref[...]
        xs_ref[AREA_ROWS:, :] = jnp.zeros((xs_ref.shape[0] - AREA_ROWS, HALF), U32)
        cnt_ref[...] = jnp.zeros(cnt_ref.shape, F32)


def _mix_prompt_tile(sinks_ref, x_ref, cos_ref, sin_ref, w_in_ref, b_in_ref, conv_w_ref, conv_b_ref,
                     lcg_ref, lcb_ref, w_out_ref, b_out_ref, l1g_ref, l1b_ref, w_r_ref,
                     h_ref, xs_ref, route_ref, cnt_ref, nk_ref, nv_ref, nc_ref,
                     kprev, vprev, u_scr, *, i, last):
    tq = x_ref.shape[1]
    pre = 32

    @pl.when(i == 0)
    def _():
        kprev[...] = jnp.zeros_like(kprev)
        vprev[...] = jnp.zeros_like(vprev)
        u_scr[0:pre, :] = jnp.zeros((pre, CONV_CH), F32)

    x = x_ref[0]
    proj = jnp.dot(x.astype(BF16), w_in_ref[...], preferred_element_type=F32) + b_in_ref[...]
    cos = cos_ref[...]
    sin = sin_ref[...]
    q = _rope(proj[:, :ATTN_DIM], cos, sin) * ATTN_SCALE
    k = _rope(proj[:, ATTN_DIM:ATTN_DIM + KV_DIM], cos, sin)
    v = proj[:, ATTN_DIM + KV_DIM:ATTN_DIM + 2 * KV_DIM]
    ca = proj[:, ATTN_DIM + 2 * KV_DIM:ATTN_DIM + 2 * KV_DIM + CONV_CH]
    cg = proj[:, ATTN_DIM + 2 * KV_DIM + CONV_CH:]

    qb = q.astype(BF16)
    kb = jnp.concatenate([kprev[...], k], axis=0).astype(BF16)
    vb = jnp.concatenate([vprev[...], v], axis=0).astype(BF16)
    row = lax.broadcasted_iota(jnp.int32, (WINDOW, 2 * WINDOW), 0)
    col = lax.broadcasted_iota(jnp.int32, (WINDOW, 2 * WINDOW), 1)
    dist = row + WINDOW - col
    band = (dist >= 0) & (dist < WINDOW)
    first_col = jnp.where(i > 0, 0, WINDOW)
    attn_blocks = []
    for j in range(tq // WINDOW):
        msk = (band & (col >= first_col)) if j == 0 else band
        heads = [None] * N_HEADS
        for c in range(N_KV_HEADS):
            kc = kb[j * WINDOW:(j + 2) * WINDOW, c * HEAD_DIM:(c + 1) * HEAD_DIM]
            vc = vb[j * WINDOW:(j + 2) * WINDOW, c * HEAD_DIM:(c + 1) * HEAD_DIM]
            qc = jnp.concatenate(
                [qb[j * WINDOW:(j + 1) * WINDOW, (c * Q_GROUP + g) * HEAD_DIM:(c * Q_GROUP + g + 1) * HEAD_DIM]
                 for g in range(Q_GROUP)], axis=0)
            s = lax.dot_general(qc, kc, (((1,), (1,)), ((), ())), preferred_element_type=F32)
            probs = []
            for g in range(Q_GROUP):
                sg = jnp.where(msk, s[g * WINDOW:(g + 1) * WINDOW], NEG_INF)
                p, _ = _softmax_rows_with_sink(sg, sinks_ref[c * Q_GROUP + g])
                probs.append(p.astype(BF16))
            o = jnp.dot(jnp.concatenate(probs, axis=0), vc, preferred_element_type=F32)
            for g in range(Q_GROUP):
                heads[c * Q_GROUP + g] = o[g * WINDOW:(g + 1) * WINDOW]
        attn_blocks.append(jnp.concatenate(heads, axis=1))
    attn = jnp.concatenate(attn_blocks, axis=0)
    kprev[...] = k[tq - WINDOW:, :]
    vprev[...] = v[tq - WINDOW:, :]

    u = ca * jax.nn.sigmoid(cg)
    u_scr[pre:pre + tq, :] = u
    acc = jnp.zeros((tq, CONV_CH), F32)
    for j in range(CONV_WIDTH):
        start = pre - (CONV_WIDTH - 1) + j
        acc = acc + u_scr[start:start + tq, :] * conv_w_ref[j:j + 1, :]
    conv_pre = acc + conv_b_ref[...]
    tail = u_scr[tq:tq + pre, :]
    u_scr[0:pre, :] = tail

    @pl.when(i == last)
    def _():
        nk_ref[0] = k[tq - WINDOW:, :]
        nv_ref[0] = v[tq - WINDOW:, :]
        nc_ref[0] = u_scr[pre + tq - (CONV_WIDTH - 1):pre + tq, :]

    h = _mix_tail(x, attn, conv_pre, lcg_ref[...], lcb_ref[...], w_out_ref, b_out_ref[...],
                  l1g_ref[...], l1b_ref[...])
    h_ref[0] = h
    route, areas, counts = _route_and_sort(h, w_r_ref[...], PROMPT_SCOPE)
    route_ref[...] = route
    for sc, (area, cnt) in enumerate(zip(areas, counts)):
        xs_ref[sc * AREA_ROWS:(sc + 1) * AREA_ROWS, :] = area
        cnt_ref[sc] = jnp.broadcast_to(cnt, (SUBLANES, LANES))


def _mix_sample_kernel(sinks_ref, x_ref, cos_ref, sin_ref, ck_ref, cv_ref, st_ref, w_in_ref, b_in_ref,
                       conv_w_ref, conv_b_ref, lcg_ref, lcb_ref, w_out_ref, b_out_ref, l1g_ref, l1b_ref,
                       w_r_ref,
                       h_ref, nk_ref, nv_ref, nc_ref, xs_ref, route_ref, cnt_ref,
                       h_scr):
    tb = x_ref.shape[0]
    n_tok = h_scr.shape[0]
    step = pl.program_id(0)
    x = x_ref[...]
    proj = jnp.dot(x.astype(BF16), w_in_ref[...], preferred_element_type=F32) + b_in_ref[...]
    cos = cos_ref[...]
    sin = sin_ref[...]
    q = _rope(proj[:, :ATTN_DIM], cos, sin) * ATTN_SCALE
    k_new = _rope(proj[:, ATTN_DIM:ATTN_DIM + KV_DIM], cos, sin)
    v_new = proj[:, ATTN_DIM + KV_DIM:ATTN_DIM + 2 * KV_DIM]
    ca = proj[:, ATTN_DIM + 2 * KV_DIM:ATTN_DIM + 2 * KV_DIM + CONV_CH]
    cg = proj[:, ATTN_DIM + 2 * KV_DIM + CONV_CH:]

    qb = q.astype(BF16)
    knb = k_new.astype(BF16).astype(F32)
    vnb = v_new.astype(BF16).astype(F32)
    gq = SAMPLE_GRP * Q_GROUP
    nkeys = SAMPLE_GRP * WINDOW
    r_i = lax.broadcasted_iota(jnp.int32, (gq, nkeys), 0)
    c_i = lax.broadcasted_iota(jnp.int32, (gq, nkeys), 1)
    msk = ((c_i >> 7) == (r_i & (SAMPLE_GRP - 1))) & ((c_i & (WINDOW - 1)) >= 1)
    r_col = lax.broadcasted_iota(jnp.int32, (gq, 1), 0) >> 3
    attn_groups = []
    for grp in range(tb // SAMPLE_GRP):
        r0 = grp * SAMPLE_GRP
        kc_all = ck_ref[r0:r0 + SAMPLE_GRP].reshape(nkeys, KV_DIM).astype(BF16)
        vc_all = cv_ref[r0:r0 + SAMPLE_GRP].reshape(nkeys, KV_DIM).astype(BF16)
        heads = [None] * N_HEADS
        for c in range(N_KV_HEADS):
            lanes = slice(c * HEAD_DIM, (c + 1) * HEAD_DIM)
            qc = jnp.concatenate(
                [qb[r0:r0 + SAMPLE_GRP, (c * Q_GROUP + g) * HEAD_DIM:(c * Q_GROUP + g + 1) * HEAD_DIM]
                 for g in range(Q_GROUP)], axis=0)
            s = lax.dot_general(qc, kc_all[:, lanes], (((1,), (1,)), ((), ())), preferred_element_type=F32)
            s = jnp.where(msk, s, NEG_INF)
            kn = jnp.concatenate([knb[r0:r0 + SAMPLE_GRP, lanes]] * Q_GROUP, axis=0)
            vn = jnp.concatenate([vnb[r0:r0 + SAMPLE_GRP, lanes]] * Q_GROUP, axis=0)
            s_new = jnp.sum(qc.astype(F32) * kn, axis=-1, keepdims=True)
            sink = jnp.zeros((gq, 1), F32)
            for g in range(Q_GROUP):
                sink = jnp.where(r_col == g, sinks_ref[c * Q_GROUP + g], sink)
            p, p_new = _softmax_rows_with_sink(s, sink, s_new)
            o = (jnp.dot(p.astype(BF16), vc_all[:, lanes], preferred_element_type=F32)
                 + p_new.astype(BF16).astype(F32) * vn)
            for g in range(Q_GROUP):
                heads[c * Q_GROUP + g] = o[g * SAMPLE_GRP:(g + 1) * SAMPLE_GRP]
        attn_groups.append(jnp.concatenate(heads, axis=1))
    attn = jnp.concatenate(attn_groups, axis=0)

    u = ca * jax.nn.sigmoid(cg)
    st = st_ref[...]
    acc = jnp.sum(st * conv_w_ref[0:CONV_WIDTH - 1, :][None], axis=1)
    conv_pre = acc + u * conv_w_ref[CONV_WIDTH - 1:CONV_WIDTH, :] + conv_b_ref[...]

    nk_ref[:, 0:WINDOW - 1, :] = ck_ref[:, 1:WINDOW, :]
    nv_ref[:, 0:WINDOW - 1, :] = cv_ref[:, 1:WINDOW, :]
    nc_ref[:, 0:CONV_WIDTH - 2, :] = st_ref[:, 1:CONV_WIDTH - 1, :]
    for b in range(tb):
        nk_ref[b, WINDOW - 1:WINDOW, :] = k_new[b:b + 1, :]
        nv_ref[b, WINDOW - 1:WINDOW, :] = v_new[b:b + 1, :]
        nc_ref[b, CONV_WIDTH - 2:CONV_WIDTH - 1, :] = u[b:b + 1, :]

    h = _mix_tail(x, attn, conv_pre, lcg_ref[...], lcb_ref[...], w_out_ref, b_out_ref[...],
                  l1g_ref[...], l1b_ref[...])
    h_ref[...] = h
    h_scr[pl.ds(pl.multiple_of(step * tb, tb), tb), :] = h

    @pl.when(step == pl.num_programs(0) - 1)
    def _():
        route, areas, counts = _route_and_sort(h_scr[...], w_r_ref[...], n_tok)
        route_ref[...] = route
        xs_ref[...] = areas[0]
        cnt_ref[0] = jnp.broadcast_to(counts[0], (SUBLANES, LANES))


def _moe_ffn_kernel(src_ref, bexp_ref, bnv_ref, nblk_ref,
                    xs_hbm, w1_ref, w3_ref, w2_ref,
                    ys_hbm,
                    xbuf, ybuf, w1b, w3b, w2b, sem_in, sem_out):
    j = pl.program_id(0)
    nblk = nblk_ref[0]
    slot = j % 2

    def chunk_rows(blk, c):
        return pl.ds(pl.multiple_of(src_ref[blk * BLOCK_CHUNKS + c] * SUBLANES, SUBLANES), SUBLANES)

    def in_copy(blk, sl, c):
        return pltpu.make_async_copy(xs_hbm.at[chunk_rows(blk, c)],
                                     xbuf.at[sl, pl.ds(c * SUBLANES, SUBLANES)], sem_in.at[sl])

    def out_copy(blk, sl, c):
        return pltpu.make_async_copy(ybuf.at[sl, pl.ds(pl.multiple_of(c * SUBLANES, SUBLANES), SUBLANES)],
                                     ys_hbm.at[chunk_rows(blk, c)], sem_out.at[sl])

    def start_gather(blk, sl):
        for c in range(BLOCK_CHUNKS):
            in_copy(blk, sl, c).start()

    def wait_gather(blk, sl):
        for c in range(BLOCK_CHUNKS):
            in_copy(blk, sl, c).wait()

    def start_scatter(blk, sl):
        def body(c, carry):
            out_copy(blk, sl, c).start()
            return carry
        lax.fori_loop(0, bnv_ref[blk], body, 0)

    def wait_scatter(blk, sl):
        def body(c, carry):
            out_copy(blk, sl, c).wait()
            return carry
        lax.fori_loop(0, bnv_ref[blk], body, 0)

    is_expert = bexp_ref[j] < N_EXPERTS

    @pl.when((j == 0) & (j < nblk) & is_expert)
    def _():
        start_gather(0, 0)

    @pl.when((j + 1 < nblk) & (bexp_ref[jnp.minimum(j + 1, pl.num_programs(0) - 1)] < N_EXPERTS))
    def _():
        start_gather(j + 1, 1 - slot)

    @pl.when(j < nblk)
    def _():
        @pl.when(j >= 2)
        def _():
            wait_scatter(j - 2, slot)

        @pl.when(is_expert)
        def _():
            prev_e = bexp_ref[jnp.maximum(j - 1, 0)]

            @pl.when((j == 0) | (prev_e != bexp_ref[j]))
            def _():
                w1b[...] = w1_ref[0].astype(BF16)
                w3b[...] = w3_ref[0].astype(BF16)
                w2b[...] = w2_ref[0].astype(BF16)

            wait_gather(j, slot)
            x_lo, x_hi = _unpack_rows(xbuf[slot])
            a1 = (jnp.dot(x_lo, w1b[0:HALF, :], preferred_element_type=F32)
                  + jnp.dot(x_hi, w1b[HALF:, :], preferred_element_type=F32))
            a3 = (jnp.dot(x_lo, w3b[0:HALF, :], preferred_element_type=F32)
                  + jnp.dot(x_hi, w3b[HALF:, :], preferred_element_type=F32))
            hmid = (a1 * jax.nn.sigmoid(a1) * a3).astype(BF16)
            y = jnp.dot(hmid, w2b[...], preferred_element_type=F32)
            ybuf[slot] = _pack_rows(y.astype(BF16).astype(F32))

        @pl.when(jnp.logical_not(is_expert))
        def _():
            ybuf[slot] = jnp.zeros((BLOCK_ROWS, HALF), U32)

        start_scatter(j, slot)

        @pl.when(j == nblk - 1)
        def _():
            @pl.when(j >= 1)
            def _():
                wait_scatter(j - 1, 1 - slot)
            wait_scatter(j, slot)


def _moe_combine_kernel(h_ref, route_ref, ys_ref, l2g_ref, l2b_ref, y_ref):
    n = h_ref.shape[0]
    scope = n // (ys_ref.shape[0] // AREA_ROWS)
    s_iota = lax.broadcasted_iota(jnp.int32, (scope, AREA_ROWS), 1).astype(F32)
    for sc in range(n // scope):
        rows = slice(sc * scope, (sc + 1) * scope)
        route = route_ref[rows, :]
        slot1 = route[:, 0:1]
        slot2 = route[:, 1:2]
        gate1 = route[:, 2:3]
        gate2 = route[:, 3:4]
        gmat = (jnp.where(s_iota == slot1, gate1, 0.0) + jnp.where(s_iota == slot2, gate2, 0.0)).astype(BF16)
        y_lo, y_hi = _unpack_rows(ys_ref[sc * AREA_ROWS:(sc + 1) * AREA_ROWS, :])
        f = jnp.concatenate([jnp.dot(gmat, y_lo, preferred_element_type=F32),
                             jnp.dot(gmat, y_hi, preferred_element_type=F32)], axis=1)
        y_ref[rows, :] = _layer_norm(ALPHA * h_ref[rows, :] + f, l2g_ref[...], l2b_ref[...])


def _excl_cumsum(a, axis):
    return jnp.cumsum(a, axis=axis) - a


def _plan_blocks(cnt, q_max, n_blk):
    n_areas = cnt.shape[0]
    c8 = (cnt + (SUBLANES - 1)) // SUBLANES
    used8 = jnp.sum(c8, axis=1)
    base8 = jnp.arange(n_areas, dtype=jnp.int32) * AREA_CHUNKS
    src_start = base8[:, None] + _excl_cumsum(c8, 1)
    len_runs = jnp.concatenate([c8.T, (AREA_CHUNKS - used8)[None]], axis=0)
    src_runs = jnp.concatenate([src_start.T, (base8 + used8)[None]], axis=0)
    tot = jnp.sum(len_runs, axis=1)
    ptot = jnp.concatenate([(tot[:N_EXPERTS] + BLOCK_CHUNKS - 1) // BLOCK_CHUNKS * BLOCK_CHUNKS, tot[N_EXPERTS:]])
    pstart = _excl_cumsum(ptot, 0)
    dst_runs = pstart[:, None] + _excl_cumsum(len_runs, 1)
    off = (src_runs - dst_runs).reshape(-1)
    diff = off - jnp.concatenate([jnp.zeros((1,), jnp.int32), off[:-1]])
    steps = jnp.zeros((q_max + 1,), jnp.int32).at[dst_runs.reshape(-1)].add(diff, mode="drop")
    q = jnp.arange(q_max, dtype=jnp.int32)
    src8 = q + jnp.cumsum(steps)[:q_max]
    blk_end = jnp.cumsum((ptot + BLOCK_CHUNKS - 1) // BLOCK_CHUNKS)
    blk = jnp.arange(n_blk, dtype=jnp.int32)
    bexp = jnp.sum((blk[:, None] >= blk_end[None, :]).astype(jnp.int32), axis=1)
    bexp_c = jnp.minimum(bexp, N_EXPERTS)
    valid_end = jnp.where(bexp <= N_EXPERTS, (pstart + tot)[bexp_c], 0)
    bnv = jnp.clip(valid_end - blk * BLOCK_CHUNKS, 0, BLOCK_CHUNKS)
    zero_chunk = AREA_CHUNKS - 1
    src8 = jnp.where(q < valid_end[q // BLOCK_CHUNKS], src8, zero_chunk)
    return src8.astype(jnp.int32), bexp_c.astype(jnp.int32), bnv.astype(jnp.int32), blk_end[-1:].astype(jnp.int32)


def _rope_tables(pos):
    half = HEAD_DIM // 2
    inv = ROPE_THETA ** (-jnp.arange(half, dtype=F32) / half)
    ang = pos.astype(F32)[:, None] * inv
    cos = jnp.cos(ang)
    sin = jnp.sin(ang)
    reps = LANES // HEAD_DIM
    cos_t = jnp.concatenate([cos, cos] * reps, axis=1)
    sin_t = jnp.concatenate([-sin, sin] * reps, axis=1)
    return cos_t, sin_t


def _full(shape):
    nd = len(shape)
    return pl.BlockSpec(shape, lambda *_: (0,) * nd)


def kernel(x_prompt, x_sample, cache_k, cache_v, state_conv, w_in, b_in, sinks, conv_w, conv_b, ln_conv_g,
           ln_conv_b, w_out, b_out, ln1_g, ln1_b, w_router_group, w_router_expert, w1, w3, w2, ln2_g, ln2_b):
    assert w_in.shape[0] == DEPTH
    bsz, seq, _ = x_prompt.shape
    dec_b, dec_t, _ = x_sample.shape
    assert dec_t == 1 and seq % TILE_Q == 0 and dec_b % SAMPLE_TB == 0 and 2 * dec_b + 7 * N_EXPERTS <= AREA_ROWS
    n_prompt = bsz * seq
    tiles = seq // TILE_Q
    spt = TILE_Q // PROMPT_SCOPE
    p_areas = n_prompt // PROMPT_SCOPE
    n_areas = p_areas + spt
    q_max = (n_areas * AREA_CHUNKS + N_EXPERTS * (BLOCK_CHUNKS - 1) + BLOCK_CHUNKS - 1) // BLOCK_CHUNKS * BLOCK_CHUNKS
    n_blk = q_max // BLOCK_CHUNKS

    row = lambda a: a.reshape(1, -1)
    w_in_b = w_in[0].astype(BF16)
    w_out_b = w_out[0].astype(BF16)
    w_r = jnp.concatenate([w_router_group[0], w_router_expert[0].reshape(D_MODEL, N_EXPERTS),
                           jnp.zeros((D_MODEL, LANES - ELANE0 - N_EXPERTS), F32)], axis=1).astype(BF16)
    shared = (w_in_b, row(b_in[0]), conv_w[0], row(conv_b[0]), row(ln_conv_g[0]), row(ln_conv_b[0]),
              w_out_b, row(b_out[0]), row(ln1_g[0]), row(ln1_b[0]), w_r)
    shared_specs = [_full(a.shape) for a in shared]
    smem = pl.BlockSpec(memory_space=pltpu.SMEM)
    cos_p, sin_p = _rope_tables(jnp.arange(seq))
    cos_s, sin_s = _rope_tables(16384 + jnp.arange(dec_t))

    cparams = lambda sem: pltpu.CompilerParams(dimension_semantics=sem, vmem_limit_bytes=VMEM_LIMIT)

    ck = cache_k[0].reshape(dec_b, WINDOW, KV_DIM)
    cv = cache_v[0].reshape(dec_b, WINDOW, KV_DIM)
    st = state_conv[0]
    tb = SAMPLE_TB
    h_s, nk_s, nv_s, nc_s, xs_s, route_s, cnt_s = pl.pallas_call(
        _mix_sample_kernel,
        grid=(dec_b // tb,),
        in_specs=[smem,
                  pl.BlockSpec((tb, D_MODEL), lambda t: (t, 0)),
                  _full(cos_s.shape), _full(sin_s.shape),
                  pl.BlockSpec((tb, WINDOW, KV_DIM), lambda t: (t, 0, 0)),
                  pl.BlockSpec((tb, WINDOW, KV_DIM), lambda t: (t, 0, 0)),
                  pl.BlockSpec((tb, CONV_WIDTH - 1, CONV_CH), lambda t: (t, 0, 0))] + shared_specs,
        out_specs=[pl.BlockSpec((tb, D_MODEL), lambda t: (t, 0)),
                   pl.BlockSpec((tb, WINDOW, KV_DIM), lambda t: (t, 0, 0)),
                   pl.BlockSpec((tb, WINDOW, KV_DIM), lambda t: (t, 0, 0)),
                   pl.BlockSpec((tb, CONV_WIDTH - 1, CONV_CH), lambda t: (t, 0, 0)),
                   _full((AREA_ROWS, HALF)),
                   _full((dec_b, LANES)),
                   _full((1, SUBLANES, LANES))],
        out_shape=[jax.ShapeDtypeStruct((dec_b, D_MODEL), F32),
                   jax.ShapeDtypeStruct((dec_b, WINDOW, KV_DIM), F32),
                   jax.ShapeDtypeStruct((dec_b, WINDOW, KV_DIM), F32),
                   jax.ShapeDtypeStruct((dec_b, CONV_WIDTH - 1, CONV_CH), F32),
                   jax.ShapeDtypeStruct((AREA_ROWS, HALF), U32),
                   jax.ShapeDtypeStruct((dec_b, LANES), F32),
                   jax.ShapeDtypeStruct((1, SUBLANES, LANES), F32)],
        scratch_shapes=[pltpu.VMEM((dec_b, D_MODEL), F32)],
        compiler_params=cparams(("arbitrary",)),
        name="mix_sample",
    )(sinks[0], x_sample.reshape(dec_b, D_MODEL), cos_s, sin_s, ck, cv, st, *shared)

    n_real = bsz * tiles
    real = lambda t: jnp.minimum(t, n_real - 1)
    h_p, xs, route_p, cnt_p, nk_p, nv_p, nc_p = pl.pallas_call(
        functools.partial(_mix_prompt_kernel, tiles=tiles),
        grid=(n_real + 1,),
        in_specs=[smem,
                  pl.BlockSpec((1, TILE_Q, D_MODEL), lambda t: (real(t) // tiles, real(t) % tiles, 0)),
                  pl.BlockSpec((TILE_Q, LANES), lambda t: (real(t) % tiles, 0)),
                  pl.BlockSpec((TILE_Q, LANES), lambda t: (real(t) % tiles, 0)),
                  _full((AREA_ROWS, HALF))] + shared_specs,
        out_specs=[pl.BlockSpec((1, TILE_Q, D_MODEL), lambda t: (real(t) // tiles, real(t) % tiles, 0)),
                   pl.BlockSpec((spt * AREA_ROWS, HALF), lambda t: (t, 0)),
                   pl.BlockSpec((TILE_Q, LANES), lambda t: (real(t), 0)),
                   pl.BlockSpec((spt, SUBLANES, LANES), lambda t: (t, 0, 0)),
                   pl.BlockSpec((1, WINDOW, KV_DIM), lambda t: (real(t) // tiles, 0, 0)),
                   pl.BlockSpec((1, WINDOW, KV_DIM), lambda t: (real(t) // tiles, 0, 0)),
                   pl.BlockSpec((1, CONV_WIDTH - 1, CONV_CH), lambda t: (real(t) // tiles, 0, 0))],
        out_shape=[jax.ShapeDtypeStruct((bsz, seq, D_MODEL), F32),
                   jax.ShapeDtypeStruct((n_areas * AREA_ROWS, HALF), U32),
                   jax.ShapeDtypeStruct((n_prompt, LANES), F32),
                   jax.ShapeDtypeStruct((n_areas, SUBLANES, LANES), F32),
                   jax.ShapeDtypeStruct((bsz, WINDOW, KV_DIM), F32),
                   jax.ShapeDtypeStruct((bsz, WINDOW, KV_DIM), F32),
                   jax.ShapeDtypeStruct((bsz, CONV_WIDTH - 1, CONV_CH), F32)],
        scratch_shapes=[pltpu.VMEM((WINDOW, KV_DIM), F32), pltpu.VMEM((WINDOW, KV_DIM), F32),
                        pltpu.VMEM((TILE_Q + 64, CONV_CH), F32)],
        compiler_params=cparams(("arbitrary",)),
        name="mix_prompt",
    )(sinks[0], x_prompt, cos_p, sin_p, xs_s, *shared)

    cnt = jnp.concatenate([cnt_p[:p_areas, 0], cnt_s[:, 0], cnt_p[p_areas + 1:, 0]], axis=0)
    cnt = cnt[:, ELANE0:ELANE0 + N_EXPERTS].astype(jnp.int32)
    src8, bexp, bnv, nblk = _plan_blocks(cnt, q_max, n_blk)
    w_idx = lambda j, src, be, nv, nb: (jnp.minimum(be[j], N_EXPERTS - 1), 0, 0)
    ys = pl.pallas_call(
        _moe_ffn_kernel,
        grid_spec=pltpu.PrefetchScalarGridSpec(
            num_scalar_prefetch=4,
            grid=(n_blk,),
            in_specs=[pl.BlockSpec(memory_space=pl.ANY),
                      pl.BlockSpec((1, D_MODEL, D_FF_EXPERT), w_idx),
                      pl.BlockSpec((1, D_MODEL, D_FF_EXPERT), w_idx),
                      pl.BlockSpec((1, D_FF_EXPERT, D_MODEL), w_idx)],
            out_specs=pl.BlockSpec(memory_space=pl.ANY),
            scratch_shapes=[pltpu.VMEM((2, BLOCK_ROWS, HALF), U32), pltpu.VMEM((2, BLOCK_ROWS, HALF), U32),
                            pltpu.VMEM((D_MODEL, D_FF_EXPERT), BF16), pltpu.VMEM((D_MODEL, D_FF_EXPERT), BF16),
                            pltpu.VMEM((D_FF_EXPERT, D_MODEL), BF16),
                            pltpu.SemaphoreType.DMA((2,)), pltpu.SemaphoreType.DMA((2,))]),
        out_shape=jax.ShapeDtypeStruct((n_areas * AREA_ROWS, HALF), U32),
        compiler_params=cparams(("arbitrary",)),
        name="moe_ffn",
    )(src8, bexp, bnv, nblk, xs, w1[0], w3[0], w2[0])

    def combine(h2d, route, first_area, scope, tile):
        n = h2d.shape[0]
        apt = tile // scope
        return pl.pallas_call(
            _moe_combine_kernel,
            grid=(n // tile,),
            in_specs=[pl.BlockSpec((tile, D_MODEL), lambda t: (t, 0)),
                      pl.BlockSpec((tile, LANES), lambda t: (t, 0)),
                      pl.BlockSpec((apt * AREA_ROWS, HALF), lambda t: (first_area // apt + t, 0)),
                      _full((1, D_MODEL)), _full((1, D_MODEL))],
            out_specs=pl.BlockSpec((tile, D_MODEL), lambda t: (t, 0)),
            out_shape=jax.ShapeDtypeStruct((n, D_MODEL), F32),
            compiler_params=cparams(("arbitrary",)),
            name="moe_combine",
        )(h2d, route, ys, row(ln2_g[0]), row(ln2_b[0]))

    y_p = combine(h_p.reshape(n_prompt, D_MODEL), route_p, 0, PROMPT_SCOPE, TILE_Q)
    y_s = combine(h_s, route_s, p_areas, dec_b, dec_b)

    kv_shape = lambda n: (DEPTH, n, WINDOW, N_KV_HEADS, HEAD_DIM)
    return (y_p.reshape(bsz, seq, D_MODEL), y_s.reshape(dec_b, dec_t, D_MODEL),
            nk_p.reshape(kv_shape(bsz)), nv_p.reshape(kv_shape(bsz)), nc_p[None],
            nk_s.reshape(kv_shape(dec_b)), nv_s.reshape(kv_shape(dec_b)), nc_s[None])
```

```python
import functools

import jax
import jax.numpy as jnp
from jax import lax
from jax.experimental import pallas as pl
from jax.experimental.pallas import tpu as pltpu

F32 = jnp.float32
BF16 = jnp.bfloat16
U32 = jnp.uint32

D_MODEL = 1024
N_HEADS = 8
N_KV_HEADS = 2
HEAD_DIM = 64
Q_GROUP = N_HEADS // N_KV_HEADS
ATTN_DIM = N_HEADS * HEAD_DIM
KV_DIM = N_KV_HEADS * HEAD_DIM
WINDOW = 128
ROPE_THETA = 10000.0
ATTN_SCALE = HEAD_DIM ** -0.5
CONV_CH = D_MODEL - ATTN_DIM
CONV_WIDTH = 31
IN_DIM = ATTN_DIM + 2 * KV_DIM + 2 * CONV_CH
N_GROUPS = 4
EXPERTS_PER_GROUP = 8
N_EXPERTS = N_GROUPS * EXPERTS_PER_GROUP
D_FF_EXPERT = 256
LN_EPS = 1e-5
NEG_INF = -1e30
DEPTH = 1
ALPHA = (2.0 * DEPTH) ** 0.25

LANES = 128
SUBLANES = 8
HALF = D_MODEL // 2
ELANE0 = N_GROUPS
PROMPT_SCOPE = 256
AREA_ROWS = 768
AREA_CHUNKS = AREA_ROWS // SUBLANES
BLOCK_CHUNKS = 32
BLOCK_ROWS = BLOCK_CHUNKS * SUBLANES
TILE_Q = 512
SAMPLE_TB = 32
SAMPLE_GRP = 8
VMEM_LIMIT = 56 * 1024 * 1024


def _layer_norm(x, g, b):
    mu = jnp.mean(x, axis=-1, keepdims=True)
    xc = x - mu
    var = jnp.mean(xc * xc, axis=-1, keepdims=True)
    return xc * lax.rsqrt(var + LN_EPS) * g + b


def _swap_halves(x):
    n = x.shape[-1]
    lane = lax.broadcasted_iota(jnp.int32, x.shape, x.ndim - 1)
    first = (lane & (HEAD_DIM // 2)) == 0
    return jnp.where(first, pltpu.roll(x, n - HEAD_DIM // 2, x.ndim - 1), pltpu.roll(x, HEAD_DIM // 2, x.ndim - 1))


def _rope(x, cos, sin_signed):
    reps = x.shape[-1] // LANES
    if reps > 1:
        cos = jnp.concatenate([cos] * reps, axis=-1)
        sin_signed = jnp.concatenate([sin_signed] * reps, axis=-1)
    return x * cos + _swap_halves(x) * sin_signed


def _pack_rows(v):
    lo = lax.bitcast_convert_type(v[:, :HALF], U32)
    hi = lax.bitcast_convert_type(v[:, HALF:], U32)
    return (lo >> 16) | (hi & jnp.uint32(0xFFFF0000))


def _unpack_rows(w):
    lo = lax.bitcast_convert_type(w << 16, F32).astype(BF16)
    hi = lax.bitcast_convert_type(w & jnp.uint32(0xFFFF0000), F32).astype(BF16)
    return lo, hi


def _route_and_sort(h, w_r, scope):
    n = h.shape[0]
    hb = h.astype(BF16)
    logits = jnp.dot(hb, w_r, preferred_element_type=F32)
    lane = lax.broadcasted_iota(jnp.int32, (n, LANES), 1)
    lane_f = lane.astype(F32)
    is_g = lane < N_GROUPS
    lg = jnp.where(is_g, logits, -jnp.inf)
    gmax = jnp.max(lg, axis=-1, keepdims=True)
    gidx = jnp.min(jnp.where(lg == gmax, lane_f, float(LANES)), axis=-1, keepdims=True)
    p_g = 1.0 / jnp.sum(jnp.where(is_g, jnp.exp(logits - gmax), 0.0), axis=-1, keepdims=True)
    egrp = ((lane - ELANE0) >> 3).astype(F32)
    emask = (lane >= ELANE0) & (lane < ELANE0 + N_EXPERTS) & (egrp == gidx)
    sel = jnp.where(emask, logits, -jnp.inf)
    v1 = jnp.max(sel, axis=-1, keepdims=True)
    i1 = jnp.min(jnp.where(sel == v1, lane_f, float(LANES)), axis=-1, keepdims=True)
    sel2 = jnp.where(lane_f == i1, -jnp.inf, sel)
    v2 = jnp.max(sel2, axis=-1, keepdims=True)
    i2 = jnp.min(jnp.where(sel2 == v2, lane_f, float(LANES)), axis=-1, keepdims=True)
    t = jnp.exp(v2 - v1)
    gate1 = p_g / (1.0 + t)
    gate2 = p_g * t / (1.0 + t)
    pick1 = lane_f == i1
    pick2 = lane_f == i2
    member = jnp.where(pick1 | pick2, 1.0, 0.0)

    r_i = lax.broadcasted_iota(jnp.int32, (scope, scope), 0)
    c_i = lax.broadcasted_iota(jnp.int32, (scope, scope), 1)
    lower = jnp.where(c_i < r_i, 1.0, 0.0).astype(BF16)
    e_r = lax.broadcasted_iota(jnp.int32, (LANES, LANES), 0)
    e_c = lax.broadcasted_iota(jnp.int32, (LANES, LANES), 1)
    upper = jnp.where(e_r < e_c, 1.0, 0.0).astype(BF16)
    s_iota = lax.broadcasted_iota(jnp.int32, (scope, AREA_ROWS), 1).astype(F32)

    slots1, slots2, areas, counts = [], [], [], []
    for sc in range(n // scope):
        rows = slice(sc * scope, (sc + 1) * scope)
        m_sc = member[rows]
        before = jnp.dot(lower, m_sc.astype(BF16), preferred_element_type=F32)
        cnt = jnp.sum(m_sc, axis=0, keepdims=True)
        c8 = jnp.floor((cnt + (SUBLANES - 1)) * (1.0 / SUBLANES))
        c8b = jnp.broadcast_to(c8, (SUBLANES, LANES)).astype(BF16)
        off8 = jnp.dot(c8b, upper, preferred_element_type=F32)[0:1]
        slot_all = before + off8 * float(SUBLANES)
        s1 = jnp.sum(jnp.where(pick1[rows], slot_all, 0.0), axis=-1, keepdims=True)
        s2 = jnp.sum(jnp.where(pick2[rows], slot_all, 0.0), axis=-1, keepdims=True)
        perm_t = jnp.where((s_iota == s1) | (s_iota == s2), 1.0, 0.0).astype(BF16)
        sorted_rows = lax.dot_general(perm_t, hb[rows], (((0,), (0,)), ((), ())),
                                      preferred_element_type=F32)
        areas.append(_pack_rows(sorted_rows))
        counts.append(cnt)
        slots1.append(s1)
        slots2.append(s2)
    slot1 = jnp.concatenate(slots1, axis=0) if len(slots1) > 1 else slots1[0]
    slot2 = jnp.concatenate(slots2, axis=0) if len(slots2) > 1 else slots2[0]
    route = jnp.where(lane == 0, slot1, jnp.where(lane == 1, slot2, jnp.where(lane == 2, gate1,
                      jnp.where(lane == 3, gate2, 0.0))))
    return route, areas, counts


def _softmax_rows_with_sink(s, sink, extra=None):
    m = jnp.maximum(jnp.max(s, axis=-1, keepdims=True), sink)
    if extra is not None:
        m = jnp.maximum(m, extra)
    e = jnp.exp(s - m)
    den = jnp.sum(e, axis=-1, keepdims=True) + jnp.exp(sink - m)
    if extra is not None:
        ee = jnp.exp(extra - m)
        den = den + ee
        return e / den, ee / den
    return e / den, None


def _mix_tail(x, attn, conv_pre, lcg, lcb, w_out_ref, b_out, l1g, l1b):
    conv = _layer_norm(conv_pre, lcg, lcb)
    conv = conv * jax.nn.sigmoid(conv)
    mix = (jnp.dot(attn.astype(BF16), w_out_ref[0:ATTN_DIM, :], preferred_element_type=F32)
           + jnp.dot(conv.astype(BF16), w_out_ref[ATTN_DIM:, :], preferred_element_type=F32) + b_out)
    return _layer_norm(ALPHA * x + mix, l1g, l1b)


def _mix_prompt_kernel(sinks_ref, x_ref, cos_ref, sin_ref, xs_s_ref, *rest, tiles):
    xs_ref = rest[12]
    cnt_ref = rest[14]
    step = pl.program_id(0)
    n_real = pl.num_programs(0) - 1

    @pl.when(step < n_real)
    def _():
        _mix_prompt_tile(sinks_ref, x_ref, cos_ref, sin_ref, *rest, i=step % tiles, last=tiles - 1)

    @pl.when(step == n_real)
    def _():
        xs_ref[0:AREA_ROWS, :] = xs_s_ref[...]
        xs_ref[AREA_ROWS:, :] = jnp.zeros((xs_ref.shape[0] - AREA_ROWS, HALF), U32)
        cnt_ref[...] = jnp.zeros(cnt_ref.shape, F32)


def _mix_prompt_tile(sinks_ref, x_ref, cos_ref, sin_ref, w_in_ref, b_in_ref, conv_w_ref, conv_b_ref,
                     lcg_ref, lcb_ref, w_out_ref, b_out_ref, l1g_ref, l1b_ref, w_r_ref,
                     h_ref, xs_ref, route_ref, cnt_ref, nk_ref, nv_ref, nc_ref,
                     kprev, vprev, u_scr, *, i, last):
    tq = x_ref.shape[1]
    pre = 32

    @pl.when(i == 0)
    def _():
        kprev[...] = jnp.zeros_like(kprev)
        vprev[...] = jnp.zeros_like(vprev)
        u_scr[0:pre, :] = jnp.zeros((pre, CONV_CH), F32)

    x = x_ref[0]
    proj = jnp.dot(x.astype(BF16), w_in_ref[...], preferred_element_type=F32) + b_in_ref[...]
    cos = cos_ref[...]
    sin = sin_ref[...]
    q = _rope(proj[:, :ATTN_DIM], cos, sin) * ATTN_SCALE
    k = _rope(proj[:, ATTN_DIM:ATTN_DIM + KV_DIM], cos, sin)
    v = proj[:, ATTN_DIM + KV_DIM:ATTN_DIM + 2 * KV_DIM]
    ca = proj[:, ATTN_DIM + 2 * KV_DIM:ATTN_DIM + 2 * KV_DIM + CONV_CH]
    cg = proj[:, ATTN_DIM + 2 * KV_DIM + CONV_CH:]

    qb = q.astype(BF16)
    kb = jnp.concatenate([kprev[...], k], axis=0).astype(BF16)
    vb = jnp.concatenate([vprev[...], v], axis=0).astype(BF16)
    row = lax.broadcasted_iota(jnp.int32, (WINDOW, 2 * WINDOW), 0)
    col = lax.broadcasted_iota(jnp.int32, (WINDOW, 2 * WINDOW), 1)
    dist = row + WINDOW - col
    band = (dist >= 0) & (dist < WINDOW)
    first_col = jnp.where(i > 0, 0, WINDOW)
    attn_blocks = []
    for j in range(tq // WINDOW):
        msk = (band & (col >= first_col)) if j == 0 else band
        heads = [None] * N_HEADS
        for c in range(N_KV_HEADS):
            kc = kb[j * WINDOW:(j + 2) * WINDOW, c * HEAD_DIM:(c + 1) * HEAD_DIM]
            vc = vb[j * WINDOW:(j + 2) * WINDOW, c * HEAD_DIM:(c + 1) * HEAD_DIM]
            qc = jnp.concatenate(
                [qb[j * WINDOW:(j + 1) * WINDOW, (c * Q_GROUP + g) * HEAD_DIM:(c * Q_GROUP + g + 1) * HEAD_DIM]
                 for g in range(Q_GROUP)], axis=0)
            s = lax.dot_general(qc, kc, (((1,), (1,)), ((), ())), preferred_element_type=F32)
            probs = []
            for g in range(Q_GROUP):
                sg = jnp.where(msk, s[g * WINDOW:(g + 1) * WINDOW], NEG_INF)
                p, _ = _softmax_rows_with_sink(sg, sinks_ref[c * Q_GROUP + g])
                probs.append(p.astype(BF16))
            o = jnp.dot(jnp.concatenate(probs, axis=0), vc, preferred_element_type=F32)
            for g in range(Q_GROUP):
                heads[c * Q_GROUP + g] = o[g * WINDOW:(g + 1) * WINDOW]
        attn_blocks.append(jnp.concatenate(heads, axis=1))
    attn = jnp.concatenate(attn_blocks, axis=0)
    kprev[...] = k[tq - WINDOW:, :]
    vprev[...] = v[tq - WINDOW:, :]

    u = ca * jax.nn.sigmoid(cg)
    u_scr[pre:pre + tq, :] = u
    acc = jnp.zeros((tq, CONV_CH), F32)
    for j in range(CONV_WIDTH):
        start = pre - (CONV_WIDTH - 1) + j
        acc = acc + u_scr[start:start + tq, :] * conv_w_ref[j:j + 1, :]
    conv_pre = acc + conv_b_ref[...]
    tail = u_scr[tq:tq + pre, :]
    u_scr[0:pre, :] = tail

    @pl.when(i == last)
    def _():
        nk_ref[0] = k[tq - WINDOW:, :]
        nv_ref[0] = v[tq - WINDOW:, :]
        nc_ref[0] = u_scr[pre + tq - (CONV_WIDTH - 1):pre + tq, :]

    h = _mix_tail(x, attn, conv_pre, lcg_ref[...], lcb_ref[...], w_out_ref, b_out_ref[...],
                  l1g_ref[...], l1b_ref[...])
    h_ref[0] = h
    route, areas, counts = _route_and_sort(h, w_r_ref[...], PROMPT_SCOPE)
    route_ref[...] = route
    for sc, (area, cnt) in enumerate(zip(areas, counts)):
        xs_ref[sc * AREA_ROWS:(sc + 1) * AREA_ROWS, :] = area
        cnt_ref[sc] = jnp.broadcast_to(cnt, (SUBLANES, LANES))


def _mix_sample_kernel(sinks_ref, x_ref, cos_ref, sin_ref, ck_ref, cv_ref, st_ref, w_in_ref, b_in_ref,
                       conv_w_ref, conv_b_ref, lcg_ref, lcb_ref, w_out_ref, b_out_ref, l1g_ref, l1b_ref,
                       w_r_ref,
                       h_ref, nk_ref, nv_ref, nc_ref, xs_ref, route_ref, cnt_ref,
                       h_scr):
    tb = x_ref.shape[0]
    n_tok = h_scr.shape[0]
    step = pl.program_id(0)
    x = x_ref[...]
    proj = jnp.dot(x.astype(BF16), w_in_ref[...], preferred_element_type=F32) + b_in_ref[...]
    cos = cos_ref[...]
    sin = sin_ref[...]
    q = _rope(proj[:, :ATTN_DIM], cos, sin) * ATTN_SCALE
    k_new = _rope(proj[:, ATTN_DIM:ATTN_DIM + KV_DIM], cos, sin)
    v_new = proj[:, ATTN_DIM + KV_DIM:ATTN_DIM + 2 * KV_DIM]
    ca = proj[:, ATTN_DIM + 2 * KV_DIM:ATTN_DIM + 2 * KV_DIM + CONV_CH]
    cg = proj[:, ATTN_DIM + 2 * KV_DIM + CONV_CH:]

    qb = q.astype(BF16)
    knb = k_new.astype(BF16).astype(F32)
    vnb = v_new.astype(BF16).astype(F32)
    gq = SAMPLE_GRP * Q_GROUP
    nkeys = SAMPLE_GRP * WINDOW
    r_i = lax.broadcasted_iota(jnp.int32, (gq, nkeys), 0)
    c_i = lax.broadcasted_iota(jnp.int32, (gq, nkeys), 1)
    msk = ((c_i >> 7) == (r_i & (SAMPLE_GRP - 1))) & ((c_i & (WINDOW - 1)) >= 1)
    r_col = lax.broadcasted_iota(jnp.int32, (gq, 1), 0) >> 3
    attn_groups = []
    for grp in range(tb // SAMPLE_GRP):
        r0 = grp * SAMPLE_GRP
        kc_all = ck_ref[r0:r0 + SAMPLE_GRP].reshape(nkeys, KV_DIM).astype(BF16)
        vc_all = cv_ref[r0:r0 + SAMPLE_GRP].reshape(nkeys, KV_DIM).astype(BF16)
        heads = [None] * N_HEADS
        for c in range(N_KV_HEADS):
            lanes = slice(c * HEAD_DIM, (c + 1) * HEAD_DIM)
            qc = jnp.concatenate(
                [qb[r0:r0 + SAMPLE_GRP, (c * Q_GROUP + g) * HEAD_DIM:(c * Q_GROUP + g + 1) * HEAD_DIM]
                 for g in range(Q_GROUP)], axis=0)
            s = lax.dot_general(qc, kc_all[:, lanes], (((1,), (1,)), ((), ())), preferred_element_type=F32)
            s = jnp.where(msk, s, NEG_INF)
            kn = jnp.concatenate([knb[r0:r0 + SAMPLE_GRP, lanes]] * Q_GROUP, axis=0)
            vn = jnp.concatenate([vnb[r0:r0 + SAMPLE_GRP, lanes]] * Q_GROUP, axis=0)
            s_new = jnp.sum(qc.astype(F32) * kn, axis=-1, keepdims=True)
            sink = jnp.zeros((gq, 1), F32)
            for g in range(Q_GROUP):
                sink = jnp.where(r_col == g, sinks_ref[c * Q_GROUP + g], sink)
            p, p_new = _softmax_rows_with_sink(s, sink, s_new)
            o = (jnp.dot(p.astype(BF16), vc_all[:, lanes], preferred_element_type=F32)
                 + p_new.astype(BF16).astype(F32) * vn)
            for g in range(Q_GROUP):
                heads[c * Q_GROUP + g] = o[g * SAMPLE_GRP:(g + 1) * SAMPLE_GRP]
        attn_groups.append(jnp.concatenate(heads, axis=1))
    attn = jnp.concatenate(attn_groups, axis=0)

    u = ca * jax.nn.sigmoid(cg)
    st = st_ref[...]
    acc = jnp.sum(st * conv_w_ref[0:CONV_WIDTH - 1, :][None], axis=1)
    conv_pre = acc + u * conv_w_ref[CONV_WIDTH - 1:CONV_WIDTH, :] + conv_b_ref[...]

    nk_ref[:, 0:WINDOW - 1, :] = ck_ref[:, 1:WINDOW, :]
    nv_ref[:, 0:WINDOW - 1, :] = cv_ref[:, 1:WINDOW, :]
    nc_ref[:, 0:CONV_WIDTH - 2, :] = st_ref[:, 1:CONV_WIDTH - 1, :]
    for b in range(tb):
        nk_ref[b, WINDOW - 1:WINDOW, :] = k_new[b:b + 1, :]
        nv_ref[b, WINDOW - 1:WINDOW, :] = v_new[b:b + 1, :]
        nc_ref[b, CONV_WIDTH - 2:CONV_WIDTH - 1, :] = u[b:b + 1, :]

    h = _mix_tail(x, attn, conv_pre, lcg_ref[...], lcb_ref[...], w_out_ref, b_out_ref[...],
                  l1g_ref[...], l1b_ref[...])
    h_ref[...] = h
    h_scr[pl.ds(pl.multiple_of(step * tb, tb), tb), :] = h

    @pl.when(step == pl.num_programs(0) - 1)
    def _():
        route, areas, counts = _route_and_sort(h_scr[...], w_r_ref[...], n_tok)
        route_ref[...] = route
        xs_ref[...] = areas[0]
        cnt_ref[0] = jnp.broadcast_to(counts[0], (SUBLANES, LANES))


ZERO_BITS = (64, 32, 16, 8, 4, 2, 1)
ZERO_ROWS = ZERO_BITS[0] * SUBLANES
DUMP_CHUNKS = 2 * BLOCK_CHUNKS


def _moe_ffn_kernel(src_ref, dst_ref, bstart_ref, used_ref,
                    xs_hbm, w1_ref, w3_ref, w2_ref,
                    ys_hbm,
                    xbuf, ybuf, zbuf, w1b, w3b, w2b, sem_in, sem_out, sem_zero, sem_dump):
    e = pl.program_id(0)
    n_areas = used_ref.shape[0]
    total = bstart_ref[N_EXPERTS]
    dump_row0 = n_areas * AREA_ROWS

    def rows_at(ref, blk, c):
        return pl.ds(pl.multiple_of(ref[blk * BLOCK_CHUNKS + c] * SUBLANES, SUBLANES), SUBLANES)

    def in_copy(blk, sl, c):
        return pltpu.make_async_copy(xs_hbm.at[rows_at(src_ref, blk, c)],
                                     xbuf.at[sl, pl.ds(c * SUBLANES, SUBLANES)], sem_in.at[sl])

    def out_copy(blk, sl, c):
        return pltpu.make_async_copy(ybuf.at[sl, pl.ds(c * SUBLANES, SUBLANES)],
                                     ys_hbm.at[rows_at(dst_ref, blk, c)], sem_out.at[sl])

    def start_gather(blk, sl):
        for c in range(BLOCK_CHUNKS):
            in_copy(blk, sl, c).start()

    def wait_gather(blk, sl):
        for c in range(BLOCK_CHUNKS):
            in_copy(blk, sl, c).wait()

    def start_scatter(blk, sl):
        for c in range(BLOCK_CHUNKS):
            out_copy(blk, sl, c).start()

    def wait_scatter(blk, sl):
        for c in range(BLOCK_CHUNKS):
            out_copy(blk, sl, c).wait()

    def for_each_tail_piece(fn):
        def area_body(s, carry):
            used = used_ref[s]
            tail = AREA_CHUNKS - used
            row = (s * AREA_CHUNKS + used) * SUBLANES
            for bit in ZERO_BITS:
                take = (tail & bit) != 0

                @pl.when(take)
                def _(row=row, bit=bit):
                    fn(pltpu.make_async_copy(zbuf.at[pl.ds(0, bit * SUBLANES)],
                                             ys_hbm.at[pl.ds(pl.multiple_of(row, SUBLANES), bit * SUBLANES)],
                                             sem_zero))
                row = row + jnp.where(take, bit * SUBLANES, 0)
            return carry
        lax.fori_loop(0, n_areas, area_body, 0)

    @pl.when(e == 0)
    def _():
        zbuf[...] = jnp.zeros(zbuf.shape, U32)
        dump = pltpu.make_async_copy(zbuf.at[pl.ds(0, DUMP_CHUNKS * SUBLANES)],
                                     ys_hbm.at[pl.ds(dump_row0, DUMP_CHUNKS * SUBLANES)], sem_dump)
        dump.start()
        dump.wait()
        for_each_tail_piece(lambda cp: cp.start())

        @pl.when(total > 0)
        def _():
            start_gather(0, 0)

    w1b[...] = w1_ref[0].astype(BF16)
    w3b[...] = w3_ref[0].astype(BF16)
    w2b[...] = w2_ref[0].astype(BF16)

    def block_body(blk, carry):
        slot = blk % 2
        wait_gather(blk, slot)

        @pl.when(blk + 1 < total)
        def _():
            start_gather(blk + 1, 1 - slot)

        x_lo, x_hi = _unpack_rows(xbuf[slot])
        a1 = (jnp.dot(x_lo, w1b[0:HALF, :], preferred_element_type=F32)
              + jnp.dot(x_hi, w1b[HALF:, :], preferred_element_type=F32))
        a3 = (jnp.dot(x_lo, w3b[0:HALF, :], preferred_element_type=F32)
              + jnp.dot(x_hi, w3b[HALF:, :], preferred_element_type=F32))
        hmid = (a1 * jax.nn.sigmoid(a1) * a3).astype(BF16)
        y = jnp.dot(hmid, w2b[...], preferred_element_type=F32)

        @pl.when(blk >= 2)
        def _():
            wait_scatter(blk - 2, slot)

        ybuf[slot] = _pack_rows(y.astype(BF16).astype(F32))
        start_scatter(blk, slot)
        return carry

    lax.fori_loop(bstart_ref[e], bstart_ref[e + 1], block_body, 0)

    @pl.when(e == N_EXPERTS - 1)
    def _():
        @pl.when(total >= 2)
        def _():
            wait_scatter(total - 2, total % 2)

        @pl.when(total >= 1)
        def _():
            wait_scatter(total - 1, (total - 1) % 2)

        for_each_tail_piece(lambda cp: cp.wait())


def _moe_combine_kernel(h_ref, route_ref, ys_ref, l2g_ref, l2b_ref, y_ref):
    n = h_ref.shape[0]
    scope = n // (ys_ref.shape[0] // AREA_ROWS)
    s_iota = lax.broadcasted_iota(jnp.int32, (scope, AREA_ROWS), 1).astype(F32)
    for sc in range(n // scope):
        rows = slice(sc * scope, (sc + 1) * scope)
        route = route_ref[rows, :]
        slot1 = route[:, 0:1]
        slot2 = route[:, 1:2]
        gate1 = route[:, 2:3]
        gate2 = route[:, 3:4]
        gmat = (jnp.where(s_iota == slot1, gate1, 0.0) + jnp.where(s_iota == slot2, gate2, 0.0)).astype(BF16)
        y_lo, y_hi = _unpack_rows(ys_ref[sc * AREA_ROWS:(sc + 1) * AREA_ROWS, :])
        f = jnp.concatenate([jnp.dot(gmat, y_lo, preferred_element_type=F32),
                             jnp.dot(gmat, y_hi, preferred_element_type=F32)], axis=1)
        y_ref[rows, :] = _layer_norm(ALPHA * h_ref[rows, :] + f, l2g_ref[...], l2b_ref[...])


def _excl_cumsum(a, axis):
    return jnp.cumsum(a, axis=axis) - a


def _plan_blocks(cnt, q_max):
    n_areas = cnt.shape[0]
    c8 = (cnt + (SUBLANES - 1)) // SUBLANES
    used8 = jnp.sum(c8, axis=1)
    base8 = jnp.arange(n_areas, dtype=jnp.int32) * AREA_CHUNKS
    src_runs = (base8[:, None] + _excl_cumsum(c8, 1)).T
    len_runs = c8.T
    tot = jnp.sum(len_runs, axis=1)
    ptot = (tot + BLOCK_CHUNKS - 1) // BLOCK_CHUNKS * BLOCK_CHUNKS
    pstart = _excl_cumsum(ptot, 0)
    dst_runs = pstart[:, None] + _excl_cumsum(len_runs, 1)
    off = (src_runs - dst_runs).reshape(-1)
    diff = off - jnp.concatenate([jnp.zeros((1,), jnp.int32), off[:-1]])
    q = jnp.arange(q_max, dtype=jnp.int32)
    src = q + jnp.sum(jnp.where(dst_runs.reshape(1, -1) <= q[:, None], diff[None, :], 0), axis=1)
    valid = jnp.any((pstart[None, :] <= q[:, None]) & (q[:, None] < (pstart + tot)[None, :]), axis=1)
    zero_chunk = AREA_CHUNKS - 1
    dump = n_areas * AREA_CHUNKS + ((q // BLOCK_CHUNKS) % 2) * BLOCK_CHUNKS + q % BLOCK_CHUNKS
    src_tab = jnp.where(valid, src, zero_chunk).astype(jnp.int32)
    dst_tab = jnp.where(valid, src, dump).astype(jnp.int32)
    bstart = jnp.concatenate([pstart, jnp.sum(ptot, keepdims=True)]) // BLOCK_CHUNKS
    return src_tab, dst_tab, bstart.astype(jnp.int32), used8.astype(jnp.int32)


def _rope_tables(pos):
    half = HEAD_DIM // 2
    inv = ROPE_THETA ** (-jnp.arange(half, dtype=F32) / half)
    ang = pos.astype(F32)[:, None] * inv
    cos = jnp.cos(ang)
    sin = jnp.sin(ang)
    reps = LANES // HEAD_DIM
    cos_t = jnp.concatenate([cos, cos] * reps, axis=1)
    sin_t = jnp.concatenate([-sin, sin] * reps, axis=1)
    return cos_t, sin_t


def _full(shape):
    nd = len(shape)
    return pl.BlockSpec(shape, lambda *_: (0,) * nd)


def kernel(x_prompt, x_sample, cache_k, cache_v, state_conv, w_in, b_in, sinks, conv_w, conv_b, ln_conv_g,
           ln_conv_b, w_out, b_out, ln1_g, ln1_b, w_router_group, w_router_expert, w1, w3, w2, ln2_g, ln2_b):
    assert w_in.shape[0] == DEPTH
    bsz, seq, _ = x_prompt.shape
    dec_b, dec_t, _ = x_sample.shape
    assert dec_t == 1 and seq % TILE_Q == 0 and dec_b % SAMPLE_TB == 0 and 2 * dec_b + 7 * N_EXPERTS <= AREA_ROWS
    n_prompt = bsz * seq
    tiles = seq // TILE_Q
    spt = TILE_Q // PROMPT_SCOPE
    p_areas = n_prompt // PROMPT_SCOPE
    n_areas = p_areas + spt
    q_max = (n_areas * AREA_CHUNKS + N_EXPERTS * (BLOCK_CHUNKS - 1) + BLOCK_CHUNKS - 1) // BLOCK_CHUNKS * BLOCK_CHUNKS

    row = lambda a: a.reshape(1, -1)
    w_in_b = w_in[0].astype(BF16)
    w_out_b = w_out[0].astype(BF16)
    w_r = jnp.concatenate([w_router_group[0], w_router_expert[0].reshape(D_MODEL, N_EXPERTS),
                           jnp.zeros((D_MODEL, LANES - ELANE0 - N_EXPERTS), F32)], axis=1).astype(BF16)
    shared = (w_in_b, row(b_in[0]), conv_w[0], row(conv_b[0]), row(ln_conv_g[0]), row(ln_conv_b[0]),
              w_out_b, row(b_out[0]), row(ln1_g[0]), row(ln1_b[0]), w_r)
    shared_specs = [_full(a.shape) for a in shared]
    smem = pl.BlockSpec(memory_space=pltpu.SMEM)
    cos_p, sin_p = _rope_tables(jnp.arange(seq))
    cos_s, sin_s = _rope_tables(16384 + jnp.arange(dec_t))

    cparams = lambda sem: pltpu.CompilerParams(dimension_semantics=sem, vmem_limit_bytes=VMEM_LIMIT)

    ck = cache_k[0].reshape(dec_b, WINDOW, KV_DIM)
    cv = cache_v[0].reshape(dec_b, WINDOW, KV_DIM)
    st = state_conv[0]
    tb = SAMPLE_TB
    h_s, nk_s, nv_s, nc_s, xs_s, route_s, cnt_s = pl.pallas_call(
        _mix_sample_kernel,
        grid=(dec_b // tb,),
        in_specs=[smem,
                  pl.BlockSpec((tb, D_MODEL), lambda t: (t, 0)),
                  _full(cos_s.shape), _full(sin_s.shape),
                  pl.BlockSpec((tb, WINDOW, KV_DIM), lambda t: (t, 0, 0)),
                  pl.BlockSpec((tb, WINDOW, KV_DIM), lambda t: (t, 0, 0)),
                  pl.BlockSpec((tb, CONV_WIDTH - 1, CONV_CH), lambda t: (t, 0, 0))] + shared_specs,
        out_specs=[pl.BlockSpec((tb, D_MODEL), lambda t: (t, 0)),
                   pl.BlockSpec((tb, WINDOW, KV_DIM), lambda t: (t, 0, 0)),
                   pl.BlockSpec((tb, WINDOW, KV_DIM), lambda t: (t, 0, 0)),
                   pl.BlockSpec((tb, CONV_WIDTH - 1, CONV_CH), lambda t: (t, 0, 0)),
                   _full((AREA_ROWS, HALF)),
                   _full((dec_b, LANES)),
                   _full((1, SUBLANES, LANES))],
        out_shape=[jax.ShapeDtypeStruct((dec_b, D_MODEL), F32),
                   jax.ShapeDtypeStruct((dec_b, WINDOW, KV_DIM), F32),
                   jax.ShapeDtypeStruct((dec_b, WINDOW, KV_DIM), F32),
                   jax.ShapeDtypeStruct((dec_b, CONV_WIDTH - 1, CONV_CH), F32),
                   jax.ShapeDtypeStruct((AREA_ROWS, HALF), U32),
                   jax.ShapeDtypeStruct((dec_b, LANES), F32),
                   jax.ShapeDtypeStruct((1, SUBLANES, LANES), F32)],
        scratch_shapes=[pltpu.VMEM((dec_b, D_MODEL), F32)],
        compiler_params=cparams(("arbitrary",)),
        name="mix_sample",
    )(sinks[0], x_sample.reshape(dec_b, D_MODEL), cos_s, sin_s, ck, cv, st, *shared)

    n_real = bsz * tiles
    real = lambda t: jnp.minimum(t, n_real - 1)
    h_p, xs, route_p, cnt_p, nk_p, nv_p, nc_p = pl.pallas_call(
        functools.partial(_mix_prompt_kernel, tiles=tiles),
        grid=(n_real + 1,),
        in_specs=[smem,
                  pl.BlockSpec((1, TILE_Q, D_MODEL), lambda t: (real(t) // tiles, real(t) % tiles, 0)),
                  pl.BlockSpec((TILE_Q, LANES), lambda t: (real(t) % tiles, 0)),
                  pl.BlockSpec((TILE_Q, LANES), lambda t: (real(t) % tiles, 0)),
                  _full((AREA_ROWS, HALF))] + shared_specs,
        out_specs=[pl.BlockSpec((1, TILE_Q, D_MODEL), lambda t: (real(t) // tiles, real(t) % tiles, 0)),
                   pl.BlockSpec((spt * AREA_ROWS, HALF), lambda t: (t, 0)),
                   pl.BlockSpec((TILE_Q, LANES), lambda t: (real(t), 0)),
                   pl.BlockSpec((spt, SUBLANES, LANES), lambda t: (t, 0, 0)),
                   pl.BlockSpec((1, WINDOW, KV_DIM), lambda t: (real(t) // tiles, 0, 0)),
                   pl.BlockSpec((1, WINDOW, KV_DIM), lambda t: (real(t) // tiles, 0, 0)),
                   pl.BlockSpec((1, CONV_WIDTH - 1, CONV_CH), lambda t: (real(t) // tiles, 0, 0))],
        out_shape=[jax.ShapeDtypeStruct((bsz, seq, D_MODEL), F32),
                   jax.ShapeDtypeStruct((n_areas * AREA_ROWS, HALF), U32),
                   jax.ShapeDtypeStruct((n_prompt, LANES), F32),
                   jax.ShapeDtypeStruct((n_areas, SUBLANES, LANES), F32),
                   jax.ShapeDtypeStruct((bsz, WINDOW, KV_DIM), F32),
                   jax.ShapeDtypeStruct((bsz, WINDOW, KV_DIM), F32),
                   jax.ShapeDtypeStruct((bsz, CONV_WIDTH - 1, CONV_CH), F32)],
        scratch_shapes=[pltpu.VMEM((WINDOW, KV_DIM), F32), pltpu.VMEM((WINDOW, KV_DIM), F32),
                        pltpu.VMEM((TILE_Q + 64, CONV_CH), F32)],
        compiler_params=cparams(("arbitrary",)),
        name="mix_prompt",
    )(sinks[0], x_prompt, cos_p, sin_p, xs_s, *shared)

    cnt = jnp.concatenate([cnt_p[:p_areas, 0], cnt_s[:, 0], cnt_p[p_areas + 1:, 0]], axis=0)
    cnt = cnt[:, ELANE0:ELANE0 + N_EXPERTS].astype(jnp.int32)
    src_tab, dst_tab, bstart, used8 = _plan_blocks(cnt, q_max)
    w_idx = lambda e, *_: (e, 0, 0)
    ys = pl.pallas_call(
        _moe_ffn_kernel,
        grid_spec=pltpu.PrefetchScalarGridSpec(
            num_scalar_prefetch=4,
            grid=(N_EXPERTS,),
            in_specs=[pl.BlockSpec(memory_space=pl.ANY),
                      pl.BlockSpec((1, D_MODEL, D_FF_EXPERT), w_idx),
                      pl.BlockSpec((1, D_MODEL, D_FF_EXPERT), w_idx),
                      pl.BlockSpec((1, D_FF_EXPERT, D_MODEL), w_idx)],
            out_specs=pl.BlockSpec(memory_space=pl.ANY),
            scratch_shapes=[pltpu.VMEM((2, BLOCK_ROWS, HALF), U32), pltpu.VMEM((2, BLOCK_ROWS, HALF), U32),
                            pltpu.VMEM((ZERO_ROWS, HALF), U32),
                            pltpu.VMEM((D_MODEL, D_FF_EXPERT), BF16), pltpu.VMEM((D_MODEL, D_FF_EXPERT), BF16),
                            pltpu.VMEM((D_FF_EXPERT, D_MODEL), BF16),
                            pltpu.SemaphoreType.DMA((2,)), pltpu.SemaphoreType.DMA((2,)),
                            pltpu.SemaphoreType.DMA(()), pltpu.SemaphoreType.DMA(())]),
        out_shape=jax.ShapeDtypeStruct((n_areas * AREA_ROWS + DUMP_CHUNKS * SUBLANES, HALF), U32),
        compiler_params=cparams(("arbitrary",)),
        name="moe_ffn",
    )(src_tab, dst_tab, bstart, used8, xs, w1[0], w3[0], w2[0])

    def combine(h2d, route, first_area, scope, tile):
        n = h2d.shape[0]
        apt = tile // scope
        return pl.pallas_call(
            _moe_combine_kernel,
            grid=(n // tile,),
            in_specs=[pl.BlockSpec((tile, D_MODEL), lambda t: (t, 0)),
                      pl.BlockSpec((tile, LANES), lambda t: (t, 0)),
                      pl.BlockSpec((apt * AREA_ROWS, HALF), lambda t: (first_area // apt + t, 0)),
                      _full((1, D_MODEL)), _full((1, D_MODEL))],
            out_specs=pl.BlockSpec((tile, D_MODEL), lambda t: (t, 0)),
            out_shape=jax.ShapeDtypeStruct((n, D_MODEL), F32),
            compiler_params=cparams(("arbitrary",)),
            name="moe_combine",
        )(h2d, route, ys, row(ln2_g[0]), row(ln2_b[0]))

    y_p = combine(h_p.reshape(n_prompt, D_MODEL), route_p, 0, PROMPT_SCOPE, TILE_Q)
    y_s = combine(h_s, route_s, p_areas, dec_b, dec_b)

    kv_shape = lambda n: (DEPTH, n, WINDOW, N_KV_HEADS, HEAD_DIM)
    return (y_p.reshape(bsz, seq, D_MODEL), y_s.reshape(dec_b, dec_t, D_MODEL),
            nk_p.reshape(kv_shape(bsz)), nv_p.reshape(kv_shape(bsz)), nc_p[None],
            nk_s.reshape(kv_shape(dec_b)), nv_s.reshape(kv_shape(dec_b)), nc_s[None])
```

```python
import functools

import jax
import jax.numpy as jnp
from jax import lax
from jax.experimental import pallas as pl
from jax.experimental.pallas import tpu as pltpu

F32 = jnp.float32
BF16 = jnp.bfloat16
U32 = jnp.uint32

D_MODEL = 1024
N_HEADS = 8
N_KV_HEADS = 2
HEAD_DIM = 64
Q_GROUP = N_HEADS // N_KV_HEADS
ATTN_DIM = N_HEADS * HEAD_DIM
KV_DIM = N_KV_HEADS * HEAD_DIM
WINDOW = 128
ROPE_THETA = 10000.0
ATTN_SCALE = HEAD_DIM ** -0.5
CONV_CH = D_MODEL - ATTN_DIM
CONV_WIDTH = 31
IN_DIM = ATTN_DIM + 2 * KV_DIM + 2 * CONV_CH
N_GROUPS = 4
EXPERTS_PER_GROUP = 8
N_EXPERTS = N_GROUPS * EXPERTS_PER_GROUP
D_FF_EXPERT = 256
LN_EPS = 1e-5
NEG_INF = -1e30
DEPTH = 1
ALPHA = (2.0 * DEPTH) ** 0.25

LANES = 128
SUBLANES = 8
HALF = D_MODEL // 2
ELANE0 = N_GROUPS
PROMPT_SCOPE = 256
AREA_ROWS = 768
AREA_CHUNKS = AREA_ROWS // SUBLANES
BLOCK_CHUNKS = 32
BLOCK_ROWS = BLOCK_CHUNKS * SUBLANES
TILE_Q = 512
SAMPLE_TB = 32
SAMPLE_GRP = 8
VMEM_LIMIT = 56 * 1024 * 1024


def _layer_norm(x, g, b):
    mu = jnp.mean(x, axis=-1, keepdims=True)
    xc = x - mu
    var = jnp.mean(xc * xc, axis=-1, keepdims=True)
    return xc * lax.rsqrt(var + LN_EPS) * g + b


def _rope(x, x_swapped, cos, sin_signed):
    pieces = [x[:, g * LANES:(g + 1) * LANES] * cos + x_swapped[:, g * LANES:(g + 1) * LANES] * sin_signed
              for g in range(x.shape[-1] // LANES)]
    return pieces[0] if len(pieces) == 1 else jnp.concatenate(pieces, axis=-1)


def _in_proj_glu(xb, w_in_ref, b_in_ref):
    lo, hi = ATTN_DIM + 2 * KV_DIM, IN_DIM
    cacg = jnp.dot(xb, w_in_ref[:, lo:hi], preferred_element_type=F32) + b_in_ref[:, lo:hi]
    return cacg[:, :CONV_CH] * jax.nn.sigmoid(cacg[:, CONV_CH:])


def _in_proj_qkv(xb, w_in_ref, b_in_ref, cos, sin_signed):
    o_v, o_e = ATTN_DIM + KV_DIM, ATTN_DIM + 2 * KV_DIM
    qkv = jnp.dot(xb, w_in_ref[:, 0:o_e], preferred_element_type=F32) + b_in_ref[:, 0:o_e]
    sw = jnp.dot(xb, w_in_ref[:, IN_DIM:], preferred_element_type=F32) + b_in_ref[:, IN_DIM:]
    q = _rope(qkv[:, :ATTN_DIM], sw[:, :ATTN_DIM], cos, sin_signed)
    k = _rope(qkv[:, ATTN_DIM:o_v], sw[:, ATTN_DIM:], cos, sin_signed)
    return q, k, qkv[:, o_v:]


def _pack_rows(v):
    lo = lax.bitcast_convert_type(v[:, :HALF], U32)
    hi = lax.bitcast_convert_type(v[:, HALF:], U32)
    return (lo >> 16) | (hi & jnp.uint32(0xFFFF0000))


def _unpack_rows(w):
    lo = lax.bitcast_convert_type(w << 16, F32).astype(BF16)
    hi = lax.bitcast_convert_type(w & jnp.uint32(0xFFFF0000), F32).astype(BF16)
    return lo, hi


def _route_and_sort(h, w_r, scope):
    n = h.shape[0]
    hb = h.astype(BF16)
    logits = jnp.dot(hb, w_r, preferred_element_type=F32)
    lane = lax.broadcasted_iota(jnp.int32, (n, LANES), 1)
    lane_f = lane.astype(F32)
    is_g = lane < N_GROUPS
    lg = jnp.where(is_g, logits, -jnp.inf)
    gmax = jnp.max(lg, axis=-1, keepdims=True)
    gidx = jnp.min(jnp.where(lg == gmax, lane_f, float(LANES)), axis=-1, keepdims=True)
    p_g = 1.0 / jnp.sum(jnp.where(is_g, jnp.exp(logits - gmax), 0.0), axis=-1, keepdims=True)
    egrp = ((lane - ELANE0) >> 3).astype(F32)
    emask = (lane >= ELANE0) & (lane < ELANE0 + N_EXPERTS) & (egrp == gidx)
    sel = jnp.where(emask, logits, -jnp.inf)
    v1 = jnp.max(sel, axis=-1, keepdims=True)
    i1 = jnp.min(jnp.where(sel == v1, lane_f, float(LANES)), axis=-1, keepdims=True)
    sel2 = jnp.where(lane_f == i1, -jnp.inf, sel)
    v2 = jnp.max(sel2, axis=-1, keepdims=True)
    i2 = jnp.min(jnp.where(sel2 == v2, lane_f, float(LANES)), axis=-1, keepdims=True)
    t = jnp.exp(v2 - v1)
    gate1 = p_g / (1.0 + t)
    gate2 = p_g * t / (1.0 + t)
    pick1 = lane_f == i1
    pick2 = lane_f == i2
    member = jnp.where(pick1 | pick2, 1.0, 0.0)

    r_i = lax.broadcasted_iota(jnp.int32, (scope, scope), 0)
    c_i = lax.broadcasted_iota(jnp.int32, (scope, scope), 1)
    lower = jnp.where(c_i < r_i, 1.0, 0.0).astype(BF16)
    e_r = lax.broadcasted_iota(jnp.int32, (LANES, LANES), 0)
    e_c = lax.broadcasted_iota(jnp.int32, (LANES, LANES), 1)
    upper = jnp.where(e_r < e_c, 1.0, 0.0).astype(BF16)
    s_iota = lax.broadcasted_iota(jnp.int32, (scope, AREA_ROWS), 1).astype(F32)

    slots1, slots2, areas, counts = [], [], [], []
    for sc in range(n // scope):
        rows = slice(sc * scope, (sc + 1) * scope)
        m_sc = member[rows]
        before = jnp.dot(lower, m_sc.astype(BF16), preferred_element_type=F32)
        cnt = jnp.sum(m_sc, axis=0, keepdims=True)
        c8 = jnp.floor((cnt + (SUBLANES - 1)) * (1.0 / SUBLANES))
        c8b = jnp.broadcast_to(c8, (SUBLANES, LANES)).astype(BF16)
        off8 = jnp.dot(c8b, upper, preferred_element_type=F32)[0:1]
        slot_all = before + off8 * float(SUBLANES)
        s1 = jnp.sum(jnp.where(pick1[rows], slot_all, 0.0), axis=-1, keepdims=True)
        s2 = jnp.sum(jnp.where(pick2[rows], slot_all, 0.0), axis=-1, keepdims=True)
        perm_t = jnp.where((s_iota == s1) | (s_iota == s2), 1.0, 0.0).astype(BF16)
        sorted_rows = lax.dot_general(perm_t, hb[rows], (((0,), (0,)), ((), ())),
                                      preferred_element_type=F32)
        areas.append(_pack_rows(sorted_rows))
        counts.append(cnt)
        slots1.append(s1)
        slots2.append(s2)
    slot1 = jnp.concatenate(slots1, axis=0) if len(slots1) > 1 else slots1[0]
    slot2 = jnp.concatenate(slots2, axis=0) if len(slots2) > 1 else slots2[0]
    route = jnp.where(lane == 0, slot1, jnp.where(lane == 1, slot2, jnp.where(lane == 2, gate1,
                      jnp.where(lane == 3, gate2, 0.0))))
    return route, areas, counts


def _softmax_rows_with_sink(s, sink, extra=None):
    m = jnp.maximum(jnp.max(s, axis=-1, keepdims=True), sink)
    if extra is not None:
        m = jnp.maximum(m, extra)
    e = jnp.exp(s - m)
    den = jnp.sum(e, axis=-1, keepdims=True) + jnp.exp(sink - m)
    if extra is not None:
        ee = jnp.exp(extra - m)
        den = den + ee
        return e / den, ee / den
    return e / den, None


def _mix_tail(x, attn, conv_pre, lcg, lcb, w_out_ref, b_out, l1g, l1b):
    conv = _layer_norm(conv_pre, lcg, lcb)
    conv = conv * jax.nn.sigmoid(conv)
    mix = (jnp.dot(attn.astype(BF16), w_out_ref[0:ATTN_DIM, :], preferred_element_type=F32)
           + jnp.dot(conv.astype(BF16), w_out_ref[ATTN_DIM:, :], preferred_element_type=F32) + b_out)
    return _layer_norm(ALPHA * x + mix, l1g, l1b)


def _mix_prompt_kernel(sinks_ref, x_ref, cos_ref, sin_ref, xs_s_ref, *rest, tiles):
    xs_ref = rest[12]
    cnt_ref = rest[14]
    step = pl.program_id(0)
    n_real = pl.num_programs(0) - 1

    @pl.when(step < n_real)
    def _():
        _mix_prompt_tile(sinks_ref, x_ref, cos_ref, sin_ref, *rest, i=step % tiles)

    @pl.when(step == n_real)
    def _():
        xs_ref[0:AREA_ROWS, :] = xs_s_ref[...]
        xs_ref[AREA_ROWS:, :] = jnp.zeros((xs_ref.shape[0] - AREA_ROWS, HALF), U32)
        cnt_ref[...] = jnp.zeros(cnt_ref.shape, F32)


def _mix_prompt_tile(sinks_ref, x_ref, cos_ref, sin_ref, w_in_ref, b_in_ref, conv_w_ref, conv_b_ref,
                     lcg_ref, lcb_ref, w_out_ref, b_out_ref, l1g_ref, l1b_ref, w_r_ref,
                     h_ref, xs_ref, route_ref, cnt_ref, nk_ref, nv_ref, nc_ref,
                     kprev, vprev, u_scr, u_shift, *, i):
    tq = x_ref.shape[1]
    pre = 32

    @pl.when(i == 0)
    def _():
        kprev[...] = jnp.zeros_like(kprev)
        vprev[...] = jnp.zeros_like(vprev)
        u_scr[0:pre, :] = jnp.zeros((pre, CONV_CH), F32)

    x = x_ref[0]
    xb = x.astype(BF16)

    u_scr[pre:pre + tq, :] = _in_proj_glu(xb, w_in_ref, b_in_ref)
    shift_rows = u_shift.shape[1]
    for r in range(1, SUBLANES):
        u_shift[r - 1] = u_scr[r:r + shift_rows, :]
    acc = jnp.zeros((tq, CONV_CH), F32)
    for j in range(CONV_WIDTH):
        a, r = divmod(pre - (CONV_WIDTH - 1) + j, SUBLANES)
        src = u_scr[a * SUBLANES:a * SUBLANES + tq, :] if r == 0 else u_shift[r - 1, a * SUBLANES:a * SUBLANES + tq, :]
        acc = acc + src * conv_w_ref[j:j + 1, :]
    conv_pre = acc + conv_b_ref[...]

    nc_ref[0] = u_scr[pre + tq - (CONV_WIDTH - 1):pre + tq, :]
    tail = u_scr[tq:tq + pre, :]
    u_scr[0:pre, :] = tail

    q, k, v = _in_proj_qkv(xb, w_in_ref, b_in_ref, cos_ref[...], sin_ref[...])
    qb = q.astype(BF16)
    kb = jnp.concatenate([kprev[...], k], axis=0).astype(BF16)
    vb = jnp.concatenate([vprev[...], v], axis=0).astype(BF16)
    row = lax.broadcasted_iota(jnp.int32, (WINDOW, 2 * WINDOW), 0)
    col = lax.broadcasted_iota(jnp.int32, (WINDOW, 2 * WINDOW), 1)
    dist = row + WINDOW - col
    band = (dist >= 0) & (dist < WINDOW)
    first_col = jnp.where(i > 0, 0, WINDOW)
    attn_blocks = []
    for j in range(tq // WINDOW):
        msk = (band & (col >= first_col)) if j == 0 else band
        heads = [None] * N_HEADS
        for c in range(N_KV_HEADS):
            kc = kb[j * WINDOW:(j + 2) * WINDOW, c * HEAD_DIM:(c + 1) * HEAD_DIM]
            vc = vb[j * WINDOW:(j + 2) * WINDOW, c * HEAD_DIM:(c + 1) * HEAD_DIM]
            qc = jnp.concatenate(
                [qb[j * WINDOW:(j + 1) * WINDOW, (c * Q_GROUP + g) * HEAD_DIM:(c * Q_GROUP + g + 1) * HEAD_DIM]
                 for g in range(Q_GROUP)], axis=0)
            s = lax.dot_general(qc, kc, (((1,), (1,)), ((), ())), preferred_element_type=F32)
            probs = []
            for g in range(Q_GROUP):
                sg = jnp.where(msk, s[g * WINDOW:(g + 1) * WINDOW], NEG_INF)
                p, _ = _softmax_rows_with_sink(sg, sinks_ref[c * Q_GROUP + g])
                probs.append(p.astype(BF16))
            o = jnp.dot(jnp.concatenate(probs, axis=0), vc, preferred_element_type=F32)
            for g in range(Q_GROUP):
                heads[c * Q_GROUP + g] = o[g * WINDOW:(g + 1) * WINDOW]
        attn_blocks.append(jnp.concatenate(heads, axis=1))
    attn = jnp.concatenate(attn_blocks, axis=0)
    kprev[...] = k[tq - WINDOW:, :]
    vprev[...] = v[tq - WINDOW:, :]
    nk_ref[0] = k[tq - WINDOW:, :]
    nv_ref[0] = v[tq - WINDOW:, :]

    h = _mix_tail(x, attn, conv_pre, lcg_ref[...], lcb_ref[...], w_out_ref, b_out_ref[...],
                  l1g_ref[...], l1b_ref[...])
    h_ref[0] = h
    route, areas, counts = _route_and_sort(h, w_r_ref[...], PROMPT_SCOPE)
    route_ref[...] = route
    for sc, (area, cnt) in enumerate(zip(areas, counts)):
        xs_ref[sc * AREA_ROWS:(sc + 1) * AREA_ROWS, :] = area
        cnt_ref[sc] = jnp.broadcast_to(cnt, (SUBLANES, LANES))


def _mix_sample_kernel(sinks_ref, x_ref, cos_ref, sin_ref, ck_ref, cv_ref, st_ref, w_in_ref, b_in_ref,
                       conv_w_ref, conv_b_ref, lcg_ref, lcb_ref, w_out_ref, b_out_ref, l1g_ref, l1b_ref,
                       w_r_ref,
                       h_ref, nk_ref, nv_ref, nc_ref, xs_ref, route_ref, cnt_ref,
                       h_scr):
    tb = x_ref.shape[0]
    n_tok = h_scr.shape[0]
    step = pl.program_id(0)
    x = x_ref[...]
    xb = x.astype(BF16)
    u = _in_proj_glu(xb, w_in_ref, b_in_ref)
    q, k_new, v_new = _in_proj_qkv(xb, w_in_ref, b_in_ref, cos_ref[...], sin_ref[...])

    qb = q.astype(BF16)
    knb = k_new.astype(BF16).astype(F32)
    vnb = v_new.astype(BF16).astype(F32)
    gq = SAMPLE_GRP * Q_GROUP
    nkeys = SAMPLE_GRP * WINDOW
    r_i = lax.broadcasted_iota(jnp.int32, (gq, nkeys), 0)
    c_i = lax.broadcasted_iota(jnp.int32, (gq, nkeys), 1)
    msk = ((c_i >> 7) == (r_i & (SAMPLE_GRP - 1))) & ((c_i & (WINDOW - 1)) >= 1)
    r_col = lax.broadcasted_iota(jnp.int32, (gq, 1), 0) >> 3
    attn_groups = []
    for grp in range(tb // SAMPLE_GRP):
        r0 = grp * SAMPLE_GRP
        kc_all = ck_ref[r0:r0 + SAMPLE_GRP].reshape(nkeys, KV_DIM).astype(BF16)
        vc_all = cv_ref[r0:r0 + SAMPLE_GRP].reshape(nkeys, KV_DIM).astype(BF16)
        heads = [None] * N_HEADS
        for c in range(N_KV_HEADS):
            lanes = slice(c * HEAD_DIM, (c + 1) * HEAD_DIM)
            qc = jnp.concatenate(
                [qb[r0:r0 + SAMPLE_GRP, (c * Q_GROUP + g) * HEAD_DIM:(c * Q_GROUP + g + 1) * HEAD_DIM]
                 for g in range(Q_GROUP)], axis=0)
            s = lax.dot_general(qc, kc_all[:, lanes], (((1,), (1,)), ((), ())), preferred_element_type=F32)
            s = jnp.where(msk, s, NEG_INF)
            kn = jnp.concatenate([knb[r0:r0 + SAMPLE_GRP, lanes]] * Q_GROUP, axis=0)
            vn = jnp.concatenate([vnb[r0:r0 + SAMPLE_GRP, lanes]] * Q_GROUP, axis=0)
            s_new = jnp.sum(qc.astype(F32) * kn, axis=-1, keepdims=True)
            sink = jnp.zeros((gq, 1), F32)
            for g in range(Q_GROUP):
                sink = jnp.where(r_col == g, sinks_ref[c * Q_GROUP + g], sink)
            p, p_new = _softmax_rows_with_sink(s, sink, s_new)
            o = (jnp.dot(p.astype(BF16), vc_all[:, lanes], preferred_element_type=F32)
                 + p_new.astype(BF16).astype(F32) * vn)
            for g in range(Q_GROUP):
                heads[c * Q_GROUP + g] = o[g * SAMPLE_GRP:(g + 1) * SAMPLE_GRP]
        attn_groups.append(jnp.concatenate(heads, axis=1))
    attn = jnp.concatenate(attn_groups, axis=0)

    st = st_ref[...]
    acc = jnp.sum(st * conv_w_ref[0:CONV_WIDTH - 1, :][None], axis=1)
    conv_pre = acc + u * conv_w_ref[CONV_WIDTH - 1:CONV_WIDTH, :] + conv_b_ref[...]

    nk_ref[:, 0:WINDOW - 1, :] = ck_ref[:, 1:WINDOW, :]
    nv_ref[:, 0:WINDOW - 1, :] = cv_ref[:, 1:WINDOW, :]
    nc_ref[:, 0:CONV_WIDTH - 2, :] = st_ref[:, 1:CONV_WIDTH - 1, :]
    for b in range(tb):
        nk_ref[b, WINDOW - 1:WINDOW, :] = k_new[b:b + 1, :]
        nv_ref[b, WINDOW - 1:WINDOW, :] = v_new[b:b + 1, :]
        nc_ref[b, CONV_WIDTH - 2:CONV_WIDTH - 1, :] = u[b:b + 1, :]

    h = _mix_tail(x, attn, conv_pre, lcg_ref[...], lcb_ref[...], w_out_ref, b_out_ref[...],
                  l1g_ref[...], l1b_ref[...])
    h_ref[...] = h
    h_scr[pl.ds(pl.multiple_of(step * tb, tb), tb), :] = h

    @pl.when(step == pl.num_programs(0) - 1)
    def _():
        route, areas, counts = _route_and_sort(h_scr[...], w_r_ref[...], n_tok)
        route_ref[...] = route
        xs_ref[...] = areas[0]
        cnt_ref[0] = jnp.broadcast_to(counts[0], (SUBLANES, LANES))


ZERO_BITS = (64, 32, 16, 8, 4, 2, 1)
ZERO_ROWS = ZERO_BITS[0] * SUBLANES
DUMP_CHUNKS = 2 * BLOCK_CHUNKS


GATHER_BUFS = 4
SCATTER_BUFS = 2


def _moe_ffn_kernel(src_ref, dst_ref, bstart_ref, used_ref,
                    xs_hbm, w1_ref, w3_ref, w2_ref,
                    ys_hbm,
                    xbuf, ybuf, hbuf, zbuf, w1b, w3b, w2b, sem_in, sem_out, sem_zero, sem_dump):
    e = pl.program_id(0)
    n_areas = used_ref.shape[0]
    total = bstart_ref[N_EXPERTS]
    dump_row0 = n_areas * AREA_ROWS
    lookahead = GATHER_BUFS - 1
    any_rows = pl.ds(0, SUBLANES)

    def chunk_at(ref, idx):
        return pl.ds(pl.multiple_of(ref[idx] * SUBLANES, SUBLANES), SUBLANES)

    def in_copy(rows, sl, c):
        return pltpu.make_async_copy(xs_hbm.at[rows], xbuf.at[sl, pl.ds(c * SUBLANES, SUBLANES)], sem_in.at[sl])

    def out_copy(rows, sl, c):
        return pltpu.make_async_copy(ybuf.at[sl, pl.ds(c * SUBLANES, SUBLANES)], ys_hbm.at[rows], sem_out.at[sl])

    def start_gather(blk):
        for c in range(BLOCK_CHUNKS):
            in_copy(chunk_at(src_ref, blk * BLOCK_CHUNKS + c), blk % GATHER_BUFS, c).start()

    def wait_gather(blk):
        for c in range(BLOCK_CHUNKS):
            in_copy(any_rows, blk % GATHER_BUFS, c).wait()

    def start_scatter(blk):
        for c in range(BLOCK_CHUNKS):
            out_copy(chunk_at(dst_ref, blk * BLOCK_CHUNKS + c), blk % SCATTER_BUFS, c).start()

    def wait_scatter(sl):
        for c in range(BLOCK_CHUNKS):
            out_copy(any_rows, sl, c).wait()

    def up_gate(blk):
        x_lo, x_hi = _unpack_rows(xbuf[blk % GATHER_BUFS])
        a1 = (jnp.dot(x_lo, w1b[0:HALF, :], preferred_element_type=F32)
              + jnp.dot(x_hi, w1b[HALF:, :], preferred_element_type=F32))
        a3 = (jnp.dot(x_lo, w3b[0:HALF, :], preferred_element_type=F32)
              + jnp.dot(x_hi, w3b[HALF:, :], preferred_element_type=F32))
        hbuf[blk % SCATTER_BUFS] = (a1 * jax.nn.sigmoid(a1) * a3).astype(BF16)

    def down(blk):
        y = jnp.dot(hbuf[blk % SCATTER_BUFS], w2b[...], preferred_element_type=F32)
        ybuf[blk % SCATTER_BUFS] = _pack_rows(y.astype(BF16).astype(F32))

    def for_each_tail_piece(fn):
        def area_body(s, carry):
            used = used_ref[s]
            tail = AREA_CHUNKS - used
            row = (s * AREA_CHUNKS + used) * SUBLANES
            for bit in ZERO_BITS:
                take = (tail & bit) != 0

                @pl.when(take)
                def _(row=row, bit=bit):
                    fn(pltpu.make_async_copy(zbuf.at[pl.ds(0, bit * SUBLANES)],
                                             ys_hbm.at[pl.ds(pl.multiple_of(row, SUBLANES), bit * SUBLANES)],
                                             sem_zero))
                row = row + jnp.where(take, bit * SUBLANES, 0)
            return carry
        lax.fori_loop(0, n_areas, area_body, 0)

    @pl.when(e == 0)
    def _():
        zbuf[...] = jnp.zeros(zbuf.shape, U32)
        ybuf[...] = jnp.zeros(ybuf.shape, U32)
        dump = pltpu.make_async_copy(zbuf.at[pl.ds(0, DUMP_CHUNKS * SUBLANES)],
                                     ys_hbm.at[pl.ds(dump_row0, DUMP_CHUNKS * SUBLANES)], sem_dump)
        dump.start()
        dump.wait()
        for_each_tail_piece(lambda cp: cp.start())
        for sl in range(SCATTER_BUFS):
            for c in range(BLOCK_CHUNKS):
                out_copy(pl.ds(dump_row0 + (sl * BLOCK_CHUNKS + c) * SUBLANES, SUBLANES), sl, c).start()
        for k in range(lookahead):
            start_gather(k)

    w1b[...] = w1_ref[0].astype(BF16)
    w3b[...] = w3_ref[0].astype(BF16)
    w2b[...] = w2_ref[0].astype(BF16)
    b0 = bstart_ref[e]
    b1 = bstart_ref[e + 1]

    @pl.when(b1 > b0)
    def _():
        wait_gather(b0)
        up_gate(b0)
        start_gather(b0 + lookahead)

    def block_body(blk, carry):
        wait_gather(blk)
        wait_scatter((blk - 1) % SCATTER_BUFS)
        down(blk - 1)
        up_gate(blk)
        start_scatter(blk - 1)
        start_gather(blk + lookahead)
        return carry

    lax.fori_loop(b0 + 1, b1, block_body, 0)

    @pl.when(b1 > b0)
    def _():
        wait_scatter((b1 - 1) % SCATTER_BUFS)
        down(b1 - 1)
        start_scatter(b1 - 1)

    @pl.when(e == N_EXPERTS - 1)
    def _():
        for sl in range(SCATTER_BUFS):
            wait_scatter(sl)
        for k in range(lookahead):
            wait_gather(total + k)
        for_each_tail_piece(lambda cp: cp.wait())


def _moe_combine_kernel(h_ref, route_ref, ys_ref, l2g_ref, l2b_ref, y_ref):
    n = h_ref.shape[0]
    scope = n // (ys_ref.shape[0] // AREA_ROWS)
    s_iota = lax.broadcasted_iota(jnp.int32, (scope, AREA_ROWS), 1).astype(F32)
    for sc in range(n // scope):
        rows = slice(sc * scope, (sc + 1) * scope)
        route = route_ref[rows, :]
        slot1 = route[:, 0:1]
        slot2 = route[:, 1:2]
        gate1 = route[:, 2:3]
        gate2 = route[:, 3:4]
        gmat = (jnp.where(s_iota == slot1, gate1, 0.0) + jnp.where(s_iota == slot2, gate2, 0.0)).astype(BF16)
        y_lo, y_hi = _unpack_rows(ys_ref[sc * AREA_ROWS:(sc + 1) * AREA_ROWS, :])
        f = jnp.concatenate([jnp.dot(gmat, y_lo, preferred_element_type=F32),
                             jnp.dot(gmat, y_hi, preferred_element_type=F32)], axis=1)
        y_ref[rows, :] = _layer_norm(ALPHA * h_ref[rows, :] + f, l2g_ref[...], l2b_ref[...])


def _excl_cumsum(a, axis):
    return jnp.cumsum(a, axis=axis) - a


def _plan_blocks(cnt, q_max):
    n_areas = cnt.shape[0]
    c8 = (cnt + (SUBLANES - 1)) // SUBLANES
    used8 = jnp.sum(c8, axis=1)
    base8 = jnp.arange(n_areas, dtype=jnp.int32) * AREA_CHUNKS
    src_runs = (base8[:, None] + _excl_cumsum(c8, 1)).T
    len_runs = c8.T
    tot = jnp.sum(len_runs, axis=1)
    ptot = (tot + BLOCK_CHUNKS - 1) // BLOCK_CHUNKS * BLOCK_CHUNKS
    pstart = _excl_cumsum(ptot, 0)
    dst_runs = pstart[:, None] + _excl_cumsum(len_runs, 1)
    off = (src_runs - dst_runs).reshape(-1)
    diff = off - jnp.concatenate([jnp.zeros((1,), jnp.int32), off[:-1]])
    q = jnp.arange(q_max, dtype=jnp.int32)
    src = q + jnp.sum(jnp.where(dst_runs.reshape(1, -1) <= q[:, None], diff[None, :], 0), axis=1)
    valid = jnp.any((pstart[None, :] <= q[:, None]) & (q[:, None] < (pstart + tot)[None, :]), axis=1)
    zero_chunk = AREA_CHUNKS - 1
    dump = n_areas * AREA_CHUNKS + ((q // BLOCK_CHUNKS) % 2) * BLOCK_CHUNKS + q % BLOCK_CHUNKS
    ahead = jnp.full(((GATHER_BUFS - 1) * BLOCK_CHUNKS,), zero_chunk, jnp.int32)
    src_tab = jnp.concatenate([jnp.where(valid, src, zero_chunk).astype(jnp.int32), ahead])
    dst_tab = jnp.where(valid, src, dump).astype(jnp.int32)
    bstart = jnp.concatenate([pstart, jnp.sum(ptot, keepdims=True)]) // BLOCK_CHUNKS
    return src_tab, dst_tab, bstart.astype(jnp.int32), used8.astype(jnp.int32)


def _rope_tables(pos):
    half = HEAD_DIM // 2
    inv = ROPE_THETA ** (-jnp.arange(half, dtype=F32) / half)
    ang = pos.astype(F32)[:, None] * inv
    cos = jnp.cos(ang)
    sin = jnp.sin(ang)
    reps = LANES // HEAD_DIM
    cos_t = jnp.concatenate([cos, cos] * reps, axis=1)
    sin_t = jnp.concatenate([-sin, sin] * reps, axis=1)
    return cos_t, sin_t


def _full(shape):
    nd = len(shape)
    return pl.BlockSpec(shape, lambda *_: (0,) * nd)


def kernel(x_prompt, x_sample, cache_k, cache_v, state_conv, w_in, b_in, sinks, conv_w, conv_b, ln_conv_g,
           ln_conv_b, w_out, b_out, ln1_g, ln1_b, w_router_group, w_router_expert, w1, w3, w2, ln2_g, ln2_b):
    assert w_in.shape[0] == DEPTH
    bsz, seq, _ = x_prompt.shape
    dec_b, dec_t, _ = x_sample.shape
    assert dec_t == 1 and seq % TILE_Q == 0 and dec_b % SAMPLE_TB == 0 and 2 * dec_b + 7 * N_EXPERTS <= AREA_ROWS
    n_prompt = bsz * seq
    tiles = seq // TILE_Q
    spt = TILE_Q // PROMPT_SCOPE
    p_areas = n_prompt // PROMPT_SCOPE
    n_areas = p_areas + spt
    q_max = (n_areas * AREA_CHUNKS + N_EXPERTS * (BLOCK_CHUNKS - 1) + BLOCK_CHUNKS - 1) // BLOCK_CHUNKS * BLOCK_CHUNKS

    row = lambda a: a.reshape(1, -1)
    qk = ATTN_DIM + KV_DIM
    col = jnp.arange(qk)
    swap = (col // HEAD_DIM) * HEAD_DIM + (col % HEAD_DIM + HEAD_DIM // 2) % HEAD_DIM
    col_scale = jnp.where(jnp.arange(IN_DIM) < ATTN_DIM, ATTN_SCALE, 1.0).astype(F32)
    w_main = w_in[0] * col_scale
    b_main = b_in[0] * col_scale
    w_in_b = jnp.concatenate([w_main, w_main[:, :qk][:, swap]], axis=1).astype(BF16)
    b_in_x = jnp.concatenate([b_main, b_main[:qk][swap]])
    w_out_b = w_out[0].astype(BF16)
    w_r = jnp.concatenate([w_router_group[0], w_router_expert[0].reshape(D_MODEL, N_EXPERTS),
                           jnp.zeros((D_MODEL, LANES - ELANE0 - N_EXPERTS), F32)], axis=1).astype(BF16)
    shared = (w_in_b, row(b_in_x), conv_w[0], row(conv_b[0]), row(ln_conv_g[0]), row(ln_conv_b[0]),
              w_out_b, row(b_out[0]), row(ln1_g[0]), row(ln1_b[0]), w_r)
    shared_specs = [_full(a.shape) for a in shared]
    smem = pl.BlockSpec(memory_space=pltpu.SMEM)
    cos_p, sin_p = _rope_tables(jnp.arange(seq))
    cos_s, sin_s = _rope_tables(16384 + jnp.arange(dec_t))

    cparams = lambda sem: pltpu.CompilerParams(dimension_semantics=sem, vmem_limit_bytes=VMEM_LIMIT)

    ck = cache_k[0].reshape(dec_b, WINDOW, KV_DIM)
    cv = cache_v[0].reshape(dec_b, WINDOW, KV_DIM)
    st = state_conv[0]
    tb = SAMPLE_TB
    h_s, nk_s, nv_s, nc_s, xs_s, route_s, cnt_s = pl.pallas_call(
        _mix_sample_kernel,
        grid=(dec_b // tb,),
        in_specs=[smem,
                  pl.BlockSpec((tb, D_MODEL), lambda t: (t, 0)),
                  _full(cos_s.shape), _full(sin_s.shape),
                  pl.BlockSpec((tb, WINDOW, KV_DIM), lambda t: (t, 0, 0)),
                  pl.BlockSpec((tb, WINDOW, KV_DIM), lambda t: (t, 0, 0)),
                  pl.BlockSpec((tb, CONV_WIDTH - 1, CONV_CH), lambda t: (t, 0, 0))] + shared_specs,
        out_specs=[pl.BlockSpec((tb, D_MODEL), lambda t: (t, 0)),
                   pl.BlockSpec((tb, WINDOW, KV_DIM), lambda t: (t, 0, 0)),
                   pl.BlockSpec((tb, WINDOW, KV_DIM), lambda t: (t, 0, 0)),
                   pl.BlockSpec((tb, CONV_WIDTH - 1, CONV_CH), lambda t: (t, 0, 0)),
                   _full((AREA_ROWS, HALF)),
                   _full((dec_b, LANES)),
                   _full((1, SUBLANES, LANES))],
        out_shape=[jax.ShapeDtypeStruct((dec_b, D_MODEL), F32),
                   jax.ShapeDtypeStruct((dec_b, WINDOW, KV_DIM), F32),
                   jax.ShapeDtypeStruct((dec_b, WINDOW, KV_DIM), F32),
                   jax.ShapeDtypeStruct((dec_b, CONV_WIDTH - 1, CONV_CH), F32),
                   jax.ShapeDtypeStruct((AREA_ROWS, HALF), U32),
                   jax.ShapeDtypeStruct((dec_b, LANES), F32),
                   jax.ShapeDtypeStruct((1, SUBLANES, LANES), F32)],
        scratch_shapes=[pltpu.VMEM((dec_b, D_MODEL), F32)],
        compiler_params=cparams(("arbitrary",)),
        name="mix_sample",
    )(sinks[0], x_sample.reshape(dec_b, D_MODEL), cos_s, sin_s, ck, cv, st, *shared)

    n_real = bsz * tiles
    real = lambda t: jnp.minimum(t, n_real - 1)
    h_p, xs, route_p, cnt_p, nk_p, nv_p, nc_p = pl.pallas_call(
        functools.partial(_mix_prompt_kernel, tiles=tiles),
        grid=(n_real + 1,),
        in_specs=[smem,
                  pl.BlockSpec((1, TILE_Q, D_MODEL), lambda t: (real(t) // tiles, real(t) % tiles, 0)),
                  pl.BlockSpec((TILE_Q, LANES), lambda t: (real(t) % tiles, 0)),
                  pl.BlockSpec((TILE_Q, LANES), lambda t: (real(t) % tiles, 0)),
                  _full((AREA_ROWS, HALF))] + shared_specs,
        out_specs=[pl.BlockSpec((1, TILE_Q, D_MODEL), lambda t: (real(t) // tiles, real(t) % tiles, 0)),
                   pl.BlockSpec((spt * AREA_ROWS, HALF), lambda t: (t, 0)),
                   pl.BlockSpec((TILE_Q, LANES), lambda t: (real(t), 0)),
                   pl.BlockSpec((spt, SUBLANES, LANES), lambda t: (t, 0, 0)),
                   pl.BlockSpec((1, WINDOW, KV_DIM), lambda t: (real(t) // tiles, 0, 0)),
                   pl.BlockSpec((1, WINDOW, KV_DIM), lambda t: (real(t) // tiles, 0, 0)),
                   pl.BlockSpec((1, CONV_WIDTH - 1, CONV_CH), lambda t: (real(t) // tiles, 0, 0))],
        out_shape=[jax.ShapeDtypeStruct((bsz, seq, D_MODEL), F32),
                   jax.ShapeDtypeStruct((n_areas * AREA_ROWS, HALF), U32),
                   jax.ShapeDtypeStruct((n_prompt, LANES), F32),
                   jax.ShapeDtypeStruct((n_areas, SUBLANES, LANES), F32),
                   jax.ShapeDtypeStruct((bsz, WINDOW, KV_DIM), F32),
                   jax.ShapeDtypeStruct((bsz, WINDOW, KV_DIM), F32),
                   jax.ShapeDtypeStruct((bsz, CONV_WIDTH - 1, CONV_CH), F32)],
        scratch_shapes=[pltpu.VMEM((WINDOW, KV_DIM), F32), pltpu.VMEM((WINDOW, KV_DIM), F32),
                        pltpu.VMEM((TILE_Q + 32, CONV_CH), F32),
                        pltpu.VMEM((SUBLANES - 1, TILE_Q + 24, CONV_CH), F32)],
        compiler_params=cparams(("arbitrary",)),
        name="mix_prompt",
    )(sinks[0], x_prompt, cos_p, sin_p, xs_s, *shared)

    cnt = jnp.concatenate([cnt_p[:p_areas, 0], cnt_s[:, 0], cnt_p[p_areas + 1:, 0]], axis=0)
    cnt = cnt[:, ELANE0:ELANE0 + N_EXPERTS].astype(jnp.int32)
    src_tab, dst_tab, bstart, used8 = _plan_blocks(cnt, q_max)
    w_idx = lambda e, *_: (e, 0, 0)
    ys = pl.pallas_call(
        _moe_ffn_kernel,
        grid_spec=pltpu.PrefetchScalarGridSpec(
            num_scalar_prefetch=4,
            grid=(N_EXPERTS,),
            in_specs=[pl.BlockSpec(memory_space=pl.ANY),
                      pl.BlockSpec((1, D_MODEL, D_FF_EXPERT), w_idx),
                      pl.BlockSpec((1, D_MODEL, D_FF_EXPERT), w_idx),
                      pl.BlockSpec((1, D_FF_EXPERT, D_MODEL), w_idx)],
            out_specs=pl.BlockSpec(memory_space=pl.ANY),
            scratch_shapes=[pltpu.VMEM((GATHER_BUFS, BLOCK_ROWS, HALF), U32),
                            pltpu.VMEM((SCATTER_BUFS, BLOCK_ROWS, HALF), U32),
                            pltpu.VMEM((SCATTER_BUFS, BLOCK_ROWS, D_FF_EXPERT), BF16),
                            pltpu.VMEM((ZERO_ROWS, HALF), U32),
                            pltpu.VMEM((D_MODEL, D_FF_EXPERT), BF16), pltpu.VMEM((D_MODEL, D_FF_EXPERT), BF16),
                            pltpu.VMEM((D_FF_EXPERT, D_MODEL), BF16),
                            pltpu.SemaphoreType.DMA((GATHER_BUFS,)), pltpu.SemaphoreType.DMA((SCATTER_BUFS,)),
                            pltpu.SemaphoreType.DMA(()), pltpu.SemaphoreType.DMA(())]),
        out_shape=jax.ShapeDtypeStruct((n_areas * AREA_ROWS + DUMP_CHUNKS * SUBLANES, HALF), U32),
        compiler_params=cparams(("arbitrary",)),
        name="moe_ffn",
    )(src_tab, dst_tab, bstart, used8, xs, w1[0], w3[0], w2[0])

    def combine(h2d, route, first_area, scope, tile):
        n = h2d.shape[0]
        apt = tile // scope
        return pl.pallas_call(
            _moe_combine_kernel,
            grid=(n // tile,),
            in_specs=[pl.BlockSpec((tile, D_MODEL), lambda t: (t, 0)),
                      pl.BlockSpec((tile, LANES), lambda t: (t, 0)),
                      pl.BlockSpec((apt * AREA_ROWS, HALF), lambda t: (first_area // apt + t, 0)),
                      _full((1, D_MODEL)), _full((1, D_MODEL))],
            out_specs=pl.BlockSpec((tile, D_MODEL), lambda t: (t, 0)),
            out_shape=jax.ShapeDtypeStruct((n, D_MODEL), F32),
            compiler_params=cparams(("arbitrary",)),
            name="moe_combine",
        )(h2d, route, ys, row(ln2_g[0]), row(ln2_b[0]))

    y_p = combine(h_p.reshape(n_prompt, D_MODEL), route_p, 0, PROMPT_SCOPE, TILE_Q)
    y_s = combine(h_s, route_s, p_areas, dec_b, dec_b)

    kv_shape = lambda n: (DEPTH, n, WINDOW, N_KV_HEADS, HEAD_DIM)
    return (y_p.reshape(bsz, seq, D_MODEL), y_s.reshape(dec_b, dec_t, D_MODEL),
            nk_p.reshape(kv_shape(bsz)), nv_p.reshape(kv_shape(bsz)), nc_p[None],
            nk_s.reshape(kv_shape(dec_b)), nv_s.reshape(kv_shape(dec_b)), nc_s[None])
```

```python
import functools

import jax
import jax.numpy as jnp
from jax import lax
from jax.experimental import pallas as pl
from jax.experimental.pallas import tpu as pltpu

F32 = jnp.float32
BF16 = jnp.bfloat16
U32 = jnp.uint32

D_MODEL = 1024
N_HEADS = 8
N_KV_HEADS = 2
HEAD_DIM = 64
Q_GROUP = N_HEADS // N_KV_HEADS
ATTN_DIM = N_HEADS * HEAD_DIM
KV_DIM = N_KV_HEADS * HEAD_DIM
WINDOW = 128
ROPE_THETA = 10000.0
ATTN_SCALE = HEAD_DIM ** -0.5
CONV_CH = D_MODEL - ATTN_DIM
CONV_WIDTH = 31
IN_DIM = ATTN_DIM + 2 * KV_DIM + 2 * CONV_CH
N_GROUPS = 4
EXPERTS_PER_GROUP = 8
N_EXPERTS = N_GROUPS * EXPERTS_PER_GROUP
D_FF_EXPERT = 256
LN_EPS = 1e-5
NEG_INF = -1e30
DEPTH = 1
ALPHA = (2.0 * DEPTH) ** 0.25

LANES = 128
SUBLANES = 8
HALF = D_MODEL // 2
ELANE0 = N_GROUPS
PROMPT_SCOPE = 256
AREA_ROWS = 768
AREA_CHUNKS = AREA_ROWS // SUBLANES
BLOCK_CHUNKS = 32
BLOCK_ROWS = BLOCK_CHUNKS * SUBLANES
TILE_Q = 512
SAMPLE_TB = 32
SAMPLE_GRP = 8
VMEM_LIMIT = 56 * 1024 * 1024


def _layer_norm(x, g, b):
    mu = jnp.mean(x, axis=-1, keepdims=True)
    xc = x - mu
    var = jnp.mean(xc * xc, axis=-1, keepdims=True)
    return xc * lax.rsqrt(var + LN_EPS) * g + b


def _rope(x, x_swapped, cos, sin_signed):
    pieces = [x[:, g * LANES:(g + 1) * LANES] * cos + x_swapped[:, g * LANES:(g + 1) * LANES] * sin_signed
              for g in range(x.shape[-1] // LANES)]
    return pieces[0] if len(pieces) == 1 else jnp.concatenate(pieces, axis=-1)


def _in_proj_glu(xb, w_in_ref, b_in_ref):
    lo, hi = ATTN_DIM + 2 * KV_DIM, IN_DIM
    cacg = jnp.dot(xb, w_in_ref[:, lo:hi], preferred_element_type=F32) + b_in_ref[:, lo:hi]
    return cacg[:, :CONV_CH] * jax.nn.sigmoid(cacg[:, CONV_CH:])


def _in_proj_qkv(xb, w_in_ref, b_in_ref, cos, sin_signed):
    o_v, o_e = ATTN_DIM + KV_DIM, ATTN_DIM + 2 * KV_DIM
    qkv = jnp.dot(xb, w_in_ref[:, 0:o_e], preferred_element_type=F32) + b_in_ref[:, 0:o_e]
    sw = jnp.dot(xb, w_in_ref[:, IN_DIM:], preferred_element_type=F32) + b_in_ref[:, IN_DIM:]
    q = _rope(qkv[:, :ATTN_DIM], sw[:, :ATTN_DIM], cos, sin_signed)
    k = _rope(qkv[:, ATTN_DIM:o_v], sw[:, ATTN_DIM:], cos, sin_signed)
    return q, k, qkv[:, o_v:]


def _pack_rows(v):
    lo = lax.bitcast_convert_type(v[:, :HALF], U32)
    hi = lax.bitcast_convert_type(v[:, HALF:], U32)
    return (lo >> 16) | (hi & jnp.uint32(0xFFFF0000))


def _unpack_rows(w):
    lo = lax.bitcast_convert_type(w << 16, F32).astype(BF16)
    hi = lax.bitcast_convert_type(w & jnp.uint32(0xFFFF0000), F32).astype(BF16)
    return lo, hi


def _route_and_sort(h, w_r, scope):
    n = h.shape[0]
    hb = h.astype(BF16)
    logits = jnp.dot(hb, w_r, preferred_element_type=F32)
    lane = lax.broadcasted_iota(jnp.int32, (n, LANES), 1)
    lane_f = lane.astype(F32)
    is_g = lane < N_GROUPS
    lg = jnp.where(is_g, logits, -jnp.inf)
    gmax = jnp.max(lg, axis=-1, keepdims=True)
    gidx = jnp.min(jnp.where(lg == gmax, lane_f, float(LANES)), axis=-1, keepdims=True)
    p_g = 1.0 / jnp.sum(jnp.where(is_g, jnp.exp(logits - gmax), 0.0), axis=-1, keepdims=True)
    egrp = ((lane - ELANE0) >> 3).astype(F32)
    emask = (lane >= ELANE0) & (lane < ELANE0 + N_EXPERTS) & (egrp == gidx)
    sel = jnp.where(emask, logits, -jnp.inf)
    v1 = jnp.max(sel, axis=-1, keepdims=True)
    i1 = jnp.min(jnp.where(sel == v1, lane_f, float(LANES)), axis=-1, keepdims=True)
    sel2 = jnp.where(lane_f == i1, -jnp.inf, sel)
    v2 = jnp.max(sel2, axis=-1, keepdims=True)
    i2 = jnp.min(jnp.where(sel2 == v2, lane_f, float(LANES)), axis=-1, keepdims=True)
    t = jnp.exp(v2 - v1)
    gate1 = p_g / (1.0 + t)
    gate2 = p_g * t / (1.0 + t)
    pick1 = lane_f == i1
    pick2 = lane_f == i2
    member = jnp.where(pick1 | pick2, 1.0, 0.0)

    r_i = lax.broadcasted_iota(jnp.int32, (scope, scope), 0)
    c_i = lax.broadcasted_iota(jnp.int32, (scope, scope), 1)
    lower = jnp.where(c_i < r_i, 1.0, 0.0).astype(BF16)
    e_r = lax.broadcasted_iota(jnp.int32, (LANES, LANES), 0)
    e_c = lax.broadcasted_iota(jnp.int32, (LANES, LANES), 1)
    upper = jnp.where(e_r < e_c, 1.0, 0.0).astype(BF16)
    s_iota = lax.broadcasted_iota(jnp.int32, (scope, AREA_ROWS), 1).astype(F32)

    slots1, slots2, areas, counts = [], [], [], []
    for sc in range(n // scope):
        rows = slice(sc * scope, (sc + 1) * scope)
        m_sc = member[rows]
        before = jnp.dot(lower, m_sc.astype(BF16), preferred_element_type=F32)
        cnt = jnp.sum(m_sc, axis=0, keepdims=True)
        c8 = jnp.floor((cnt + (SUBLANES - 1)) * (1.0 / SUBLANES))
        c8b = jnp.broadcast_to(c8, (SUBLANES, LANES)).astype(BF16)
        off8 = jnp.dot(c8b, upper, preferred_element_type=F32)[0:1]
        slot_all = before + off8 * float(SUBLANES)
        s1 = jnp.sum(jnp.where(pick1[rows], slot_all, 0.0), axis=-1, keepdims=True)
        s2 = jnp.sum(jnp.where(pick2[rows], slot_all, 0.0), axis=-1, keepdims=True)
        perm_t = jnp.where((s_iota == s1) | (s_iota == s2), 1.0, 0.0).astype(BF16)
        sorted_rows = lax.dot_general(perm_t, hb[rows], (((0,), (0,)), ((), ())),
                                      preferred_element_type=F32)
        areas.append(_pack_rows(sorted_rows))
        counts.append(cnt)
        slots1.append(s1)
        slots2.append(s2)
    slot1 = jnp.concatenate(slots1, axis=0) if len(slots1) > 1 else slots1[0]
    slot2 = jnp.concatenate(slots2, axis=0) if len(slots2) > 1 else slots2[0]
    route = jnp.where(lane == 0, slot1, jnp.where(lane == 1, slot2, jnp.where(lane == 2, gate1,
                      jnp.where(lane == 3, gate2, 0.0))))
    return route, areas, counts


def _softmax_rows_with_sink(s, sink, extra=None):
    m = jnp.maximum(jnp.max(s, axis=-1, keepdims=True), sink)
    if extra is not None:
        m = jnp.maximum(m, extra)
    e = jnp.exp(s - m)
    den = jnp.sum(e, axis=-1, keepdims=True) + jnp.exp(sink - m)
    if extra is not None:
        ee = jnp.exp(extra - m)
        den = den + ee
        return e / den, ee / den
    return e / den, None


def _mix_tail(x, attn, conv_pre, lcg, lcb, w_out_ref, b_out, l1g, l1b):
    conv = _layer_norm(conv_pre, lcg, lcb)
    conv = conv * jax.nn.sigmoid(conv)
    mix = (jnp.dot(attn.astype(BF16), w_out_ref[0:ATTN_DIM, :], preferred_element_type=F32)
           + jnp.dot(conv.astype(BF16), w_out_ref[ATTN_DIM:, :], preferred_element_type=F32) + b_out)
    return _layer_norm(ALPHA * x + mix, l1g, l1b)


def _mix_prompt_kernel(sinks_ref, x_ref, cos_ref, sin_ref, xs_s_ref, *rest, tiles):
    xs_ref = rest[12]
    cnt_ref = rest[14]
    step = pl.program_id(0)
    n_real = pl.num_programs(0) - 1

    @pl.when(step < n_real)
    def _():
        _mix_prompt_tile(sinks_ref, x_ref, cos_ref, sin_ref, *rest, i=step % tiles)

    @pl.when(step == n_real)
    def _():
        xs_ref[0:AREA_ROWS, :] = xs_s_ref[...]
        xs_ref[AREA_ROWS:, :] = jnp.zeros((xs_ref.shape[0] - AREA_ROWS, HALF), U32)
        cnt_ref[...] = jnp.zeros(cnt_ref.shape, F32)


def _mix_prompt_tile(sinks_ref, x_ref, cos_ref, sin_ref, w_in_ref, b_in_ref, conv_w_ref, conv_b_ref,
                     lcg_ref, lcb_ref, w_out_ref, b_out_ref, l1g_ref, l1b_ref, w_r_ref,
                     h_ref, xs_ref, route_ref, cnt_ref, nk_ref, nv_ref, nc_ref,
                     kprev, vprev, u_scr, u_shift, *, i):
    tq = x_ref.shape[1]
    pre = 32

    @pl.when(i == 0)
    def _():
        kprev[...] = jnp.zeros_like(kprev)
        vprev[...] = jnp.zeros_like(vprev)
        u_scr[0:pre, :] = jnp.zeros((pre, CONV_CH), F32)

    x = x_ref[0]
    xb = x.astype(BF16)

    u_scr[pre:pre + tq, :] = _in_proj_glu(xb, w_in_ref, b_in_ref)
    shift_rows = u_shift.shape[1]
    for r in range(1, SUBLANES):
        u_shift[r - 1] = u_scr[r:r + shift_rows, :]
    acc = jnp.zeros((tq, CONV_CH), F32)
    for j in range(CONV_WIDTH):
        a, r = divmod(pre - (CONV_WIDTH - 1) + j, SUBLANES)
        src = u_scr[a * SUBLANES:a * SUBLANES + tq, :] if r == 0 else u_shift[r - 1, a * SUBLANES:a * SUBLANES + tq, :]
        acc = acc + src * conv_w_ref[j:j + 1, :]
    conv_pre = acc + conv_b_ref[...]

    nc_ref[0] = u_scr[pre + tq - (CONV_WIDTH - 1):pre + tq, :]
    tail = u_scr[tq:tq + pre, :]
    u_scr[0:pre, :] = tail

    q, k, v = _in_proj_qkv(xb, w_in_ref, b_in_ref, cos_ref[...], sin_ref[...])
    qb = q.astype(BF16)
    kb = jnp.concatenate([kprev[...], k], axis=0).astype(BF16)
    vb = jnp.concatenate([vprev[...], v], axis=0).astype(BF16)
    row = lax.broadcasted_iota(jnp.int32, (WINDOW, 2 * WINDOW), 0)
    col = lax.broadcasted_iota(jnp.int32, (WINDOW, 2 * WINDOW), 1)
    dist = row + WINDOW - col
    band = (dist >= 0) & (dist < WINDOW)
    first_col = jnp.where(i > 0, 0, WINDOW)
    attn_blocks = []
    for j in range(tq // WINDOW):
        msk = (band & (col >= first_col)) if j == 0 else band
        heads = [None] * N_HEADS
        for c in range(N_KV_HEADS):
            kc = kb[j * WINDOW:(j + 2) * WINDOW, c * HEAD_DIM:(c + 1) * HEAD_DIM]
            vc = vb[j * WINDOW:(j + 2) * WINDOW, c * HEAD_DIM:(c + 1) * HEAD_DIM]
            qc = jnp.concatenate(
                [qb[j * WINDOW:(j + 1) * WINDOW, (c * Q_GROUP + g) * HEAD_DIM:(c * Q_GROUP + g + 1) * HEAD_DIM]
                 for g in range(Q_GROUP)], axis=0)
            s = lax.dot_general(qc, kc, (((1,), (1,)), ((), ())), preferred_element_type=F32)
            probs = []
            for g in range(Q_GROUP):
                sg = jnp.where(msk, s[g * WINDOW:(g + 1) * WINDOW], NEG_INF)
                p, _ = _softmax_rows_with_sink(sg, sinks_ref[c * Q_GROUP + g])
                probs.append(p.astype(BF16))
            o = jnp.dot(jnp.concatenate(probs, axis=0), vc, preferred_element_type=F32)
            for g in range(Q_GROUP):
                heads[c * Q_GROUP + g] = o[g * WINDOW:(g + 1) * WINDOW]
        attn_blocks.append(jnp.concatenate(heads, axis=1))
    attn = jnp.concatenate(attn_blocks, axis=0)
    kprev[...] = k[tq - WINDOW:, :]
    vprev[...] = v[tq - WINDOW:, :]
    nk_ref[0] = k[tq - WINDOW:, :]
    nv_ref[0] = v[tq - WINDOW:, :]

    h = _mix_tail(x, attn, conv_pre, lcg_ref[...], lcb_ref[...], w_out_ref, b_out_ref[...],
                  l1g_ref[...], l1b_ref[...])
    h_ref[0] = h
    route, areas, counts = _route_and_sort(h, w_r_ref[...], PROMPT_SCOPE)
    route_ref[...] = route
    for sc, (area, cnt) in enumerate(zip(areas, counts)):
        xs_ref[sc * AREA_ROWS:(sc + 1) * AREA_ROWS, :] = area
        cnt_ref[sc] = jnp.broadcast_to(cnt, (SUBLANES, LANES))


def _mix_sample_kernel(sinks_ref, x_ref, cos_ref, sin_ref, ck_ref, cv_ref, st_ref, w_in_ref, b_in_ref,
                       conv_w_ref, conv_b_ref, lcg_ref, lcb_ref, w_out_ref, b_out_ref, l1g_ref, l1b_ref,
                       w_r_ref,
                       h_ref, nk_ref, nv_ref, nc_ref, xs_ref, route_ref, cnt_ref,
                       h_scr):
    tb = x_ref.shape[0]
    n_tok = h_scr.shape[0]
    step = pl.program_id(0)
    x = x_ref[...]
    xb = x.astype(BF16)
    u = _in_proj_glu(xb, w_in_ref, b_in_ref)
    q, k_new, v_new = _in_proj_qkv(xb, w_in_ref, b_in_ref, cos_ref[...], sin_ref[...])

    qb = q.astype(BF16)
    knb = k_new.astype(BF16).astype(F32)
    vnb = v_new.astype(BF16).astype(F32)
    gq = SAMPLE_GRP * Q_GROUP
    nkeys = SAMPLE_GRP * WINDOW
    r_i = lax.broadcasted_iota(jnp.int32, (gq, nkeys), 0)
    c_i = lax.broadcasted_iota(jnp.int32, (gq, nkeys), 1)
    msk = ((c_i >> 7) == (r_i & (SAMPLE_GRP - 1))) & ((c_i & (WINDOW - 1)) >= 1)
    r_col = lax.broadcasted_iota(jnp.int32, (gq, 1), 0) >> 3
    attn_groups = []
    for grp in range(tb // SAMPLE_GRP):
        r0 = grp * SAMPLE_GRP
        kc_all = ck_ref[r0:r0 + SAMPLE_GRP].reshape(nkeys, KV_DIM).astype(BF16)
        vc_all = cv_ref[r0:r0 + SAMPLE_GRP].reshape(nkeys, KV_DIM).astype(BF16)
        heads = [None] * N_HEADS
        for c in range(N_KV_HEADS):
            lanes = slice(c * HEAD_DIM, (c + 1) * HEAD_DIM)
            qc = jnp.concatenate(
                [qb[r0:r0 + SAMPLE_GRP, (c * Q_GROUP + g) * HEAD_DIM:(c * Q_GROUP + g + 1) * HEAD_DIM]
                 for g in range(Q_GROUP)], axis=0)
            s = lax.dot_general(qc, kc_all[:, lanes], (((1,), (1,)), ((), ())), preferred_element_type=F32)
            s = jnp.where(msk, s, NEG_INF)
            kn = jnp.concatenate([knb[r0:r0 + SAMPLE_GRP, lanes]] * Q_GROUP, axis=0)
            vn = jnp.concatenate([vnb[r0:r0 + SAMPLE_GRP, lanes]] * Q_GROUP, axis=0)
            s_new = jnp.sum(qc.astype(F32) * kn, axis=-1, keepdims=True)
            sink = jnp.zeros((gq, 1), F32)
            for g in range(Q_GROUP):
                sink = jnp.where(r_col == g, sinks_ref[c * Q_GROUP + g], sink)
            p, p_new = _softmax_rows_with_sink(s, sink, s_new)
            o = (jnp.dot(p.astype(BF16), vc_all[:, lanes], preferred_element_type=F32)
                 + p_new.astype(BF16).astype(F32) * vn)
            for g in range(Q_GROUP):
                heads[c * Q_GROUP + g] = o[g * SAMPLE_GRP:(g + 1) * SAMPLE_GRP]
        attn_groups.append(jnp.concatenate(heads, axis=1))
    attn = jnp.concatenate(attn_groups, axis=0)

    st = st_ref[...]
    acc = jnp.sum(st * conv_w_ref[0:CONV_WIDTH - 1, :][None], axis=1)
    conv_pre = acc + u * conv_w_ref[CONV_WIDTH - 1:CONV_WIDTH, :] + conv_b_ref[...]

    nk_ref[:, 0:WINDOW - 1, :] = ck_ref[:, 1:WINDOW, :]
    nv_ref[:, 0:WINDOW - 1, :] = cv_ref[:, 1:WINDOW, :]
    nc_ref[:, 0:CONV_WIDTH - 2, :] = st_ref[:, 1:CONV_WIDTH - 1, :]
    for b in range(tb):
        nk_ref[b, WINDOW - 1:WINDOW, :] = k_new[b:b + 1, :]
        nv_ref[b, WINDOW - 1:WINDOW, :] = v_new[b:b + 1, :]
        nc_ref[b, CONV_WIDTH - 2:CONV_WIDTH - 1, :] = u[b:b + 1, :]

    h = _mix_tail(x, attn, conv_pre, lcg_ref[...], lcb_ref[...], w_out_ref, b_out_ref[...],
                  l1g_ref[...], l1b_ref[...])
    h_ref[...] = h
    h_scr[pl.ds(pl.multiple_of(step * tb, tb), tb), :] = h

    @pl.when(step == pl.num_programs(0) - 1)
    def _():
        route, areas, counts = _route_and_sort(h_scr[...], w_r_ref[...], n_tok)
        route_ref[...] = route
        xs_ref[...] = areas[0]
        cnt_ref[0] = jnp.broadcast_to(counts[0], (SUBLANES, LANES))


GATHER_BUFS = 4
SCATTER_BUFS = 4
SCATTER_LAG = 2
ZERO_BITS = (64, 32, 16, 8, 4, 2, 1)
DUMP_CHUNKS = SCATTER_BUFS * BLOCK_CHUNKS
ZERO_ROWS = max(ZERO_BITS[0], DUMP_CHUNKS) * SUBLANES


def _moe_ffn_kernel(src_ref, dst_ref, bstart_ref, used_ref,
                    xs_hbm, w1_ref, w3_ref, w2_ref,
                    ys_hbm,
                    xbuf, ybuf, hbuf, zbuf, w1b, w3b, w2b, sem_in, sem_out, sem_zero, sem_dump):
    e = pl.program_id(0)
    n_areas = used_ref.shape[0]
    total = bstart_ref[N_EXPERTS]
    dump_row0 = n_areas * AREA_ROWS
    lookahead = GATHER_BUFS - 1
    any_rows = pl.ds(0, SUBLANES)

    def chunk_at(ref, idx):
        return pl.ds(pl.multiple_of(ref[idx] * SUBLANES, SUBLANES), SUBLANES)

    def in_copy(rows, sl, c):
        return pltpu.make_async_copy(xs_hbm.at[rows], xbuf.at[sl, pl.ds(c * SUBLANES, SUBLANES)], sem_in.at[sl])

    def out_copy(rows, sl, c):
        return pltpu.make_async_copy(ybuf.at[sl, pl.ds(c * SUBLANES, SUBLANES)], ys_hbm.at[rows], sem_out.at[sl])

    def start_gather(blk):
        for c in range(BLOCK_CHUNKS):
            in_copy(chunk_at(src_ref, blk * BLOCK_CHUNKS + c), blk % GATHER_BUFS, c).start()

    def wait_gather(blk):
        for c in range(BLOCK_CHUNKS):
            in_copy(any_rows, blk % GATHER_BUFS, c).wait()

    def start_scatter(blk):
        for c in range(BLOCK_CHUNKS):
            out_copy(chunk_at(dst_ref, (blk + SCATTER_LAG) * BLOCK_CHUNKS + c), blk % SCATTER_BUFS, c).start()

    def wait_scatter(sl):
        for c in range(BLOCK_CHUNKS):
            out_copy(any_rows, sl, c).wait()

    def load_rows(blk):
        return _unpack_rows(xbuf[blk % GATHER_BUFS])

    def issue_dmas(blk):
        start_scatter(blk - SCATTER_LAG)
        start_gather(blk + lookahead)

    def up_gate(blk, rows):
        x_lo, x_hi = rows
        a1 = (jnp.dot(x_lo, w1b[0:HALF, :], preferred_element_type=F32)
              + jnp.dot(x_hi, w1b[HALF:, :], preferred_element_type=F32))
        a3 = (jnp.dot(x_lo, w3b[0:HALF, :], preferred_element_type=F32)
              + jnp.dot(x_hi, w3b[HALF:, :], preferred_element_type=F32))
        hbuf[blk % SCATTER_BUFS] = (a1 * jax.nn.sigmoid(a1) * a3).astype(BF16)

    def down(blk):
        y = jnp.dot(hbuf[blk % SCATTER_BUFS], w2b[...], preferred_element_type=F32)
        ybuf[blk % SCATTER_BUFS] = _pack_rows(y.astype(BF16).astype(F32))

    def for_each_tail_piece(fn):
        def area_body(s, carry):
            used = used_ref[s]
            tail = AREA_CHUNKS - used
            row = (s * AREA_CHUNKS + used) * SUBLANES
            for bit in ZERO_BITS:
                take = (tail & bit) != 0

                @pl.when(take)
                def _(row=row, bit=bit):
                    fn(pltpu.make_async_copy(zbuf.at[pl.ds(0, bit * SUBLANES)],
                                             ys_hbm.at[pl.ds(pl.multiple_of(row, SUBLANES), bit * SUBLANES)],
                                             sem_zero))
                row = row + jnp.where(take, bit * SUBLANES, 0)
            return carry
        lax.fori_loop(0, n_areas, area_body, 0)

    @pl.when(e == 0)
    def _():
        zbuf[...] = jnp.zeros(zbuf.shape, U32)
        ybuf[...] = jnp.zeros(ybuf.shape, U32)
        dump = pltpu.make_async_copy(zbuf.at[pl.ds(0, DUMP_CHUNKS * SUBLANES)],
                                     ys_hbm.at[pl.ds(dump_row0, DUMP_CHUNKS * SUBLANES)], sem_dump)
        dump.start()
        dump.wait()
        for_each_tail_piece(lambda cp: cp.start())
        for sl in range(SCATTER_BUFS - SCATTER_LAG):
            for c in range(BLOCK_CHUNKS):
                out_copy(pl.ds(dump_row0 + (sl * BLOCK_CHUNKS + c) * SUBLANES, SUBLANES), sl, c).start()
        for k in range(lookahead):
            start_gather(k)

    w1b[...] = w1_ref[0].astype(BF16)
    w3b[...] = w3_ref[0].astype(BF16)
    w2b[...] = w2_ref[0].astype(BF16)
    b0 = bstart_ref[e]
    b1 = bstart_ref[e + 1]

    @pl.when(b1 > b0)
    def _():
        wait_gather(b0)
        rows = load_rows(b0)
        issue_dmas(b0)
        up_gate(b0, rows)

    def block_body(blk, carry):
        wait_gather(blk)
        wait_scatter((blk - 1) % SCATTER_BUFS)
        rows = load_rows(blk)
        issue_dmas(blk)
        down(blk - 1)
        up_gate(blk, rows)
        return carry

    lax.fori_loop(b0 + 1, b1, block_body, 0)

    @pl.when(b1 > b0)
    def _():
        wait_scatter((b1 - 1) % SCATTER_BUFS)
        down(b1 - 1)

    @pl.when(e == N_EXPERTS - 1)
    def _():
        for k in range(SCATTER_LAG):
            start_scatter(total - SCATTER_LAG + k)
        for sl in range(SCATTER_BUFS):
            wait_scatter(sl)
        for k in range(lookahead):
            wait_gather(total + k)
        for_each_tail_piece(lambda cp: cp.wait())


def _moe_combine_kernel(h_ref, route_ref, ys_ref, l2g_ref, l2b_ref, y_ref):
    n = h_ref.shape[0]
    scope = n // (ys_ref.shape[0] // AREA_ROWS)
    s_iota = lax.broadcasted_iota(jnp.int32, (scope, AREA_ROWS), 1).astype(F32)
    for sc in range(n // scope):
        rows = slice(sc * scope, (sc + 1) * scope)
        route = route_ref[rows, :]
        slot1 = route[:, 0:1]
        slot2 = route[:, 1:2]
        gate1 = route[:, 2:3]
        gate2 = route[:, 3:4]
        gmat = (jnp.where(s_iota == slot1, gate1, 0.0) + jnp.where(s_iota == slot2, gate2, 0.0)).astype(BF16)
        y_lo, y_hi = _unpack_rows(ys_ref[sc * AREA_ROWS:(sc + 1) * AREA_ROWS, :])
        f = jnp.concatenate([jnp.dot(gmat, y_lo, preferred_element_type=F32),
                             jnp.dot(gmat, y_hi, preferred_element_type=F32)], axis=1)
        y_ref[rows, :] = _layer_norm(ALPHA * h_ref[rows, :] + f, l2g_ref[...], l2b_ref[...])


def _excl_cumsum(a, axis):
    return jnp.cumsum(a, axis=axis) - a


def _plan_blocks(cnt, q_max):
    n_areas = cnt.shape[0]
    c8 = (cnt + (SUBLANES - 1)) // SUBLANES
    used8 = jnp.sum(c8, axis=1)
    base8 = jnp.arange(n_areas, dtype=jnp.int32) * AREA_CHUNKS
    src_runs = (base8[:, None] + _excl_cumsum(c8, 1)).T
    len_runs = c8.T
    tot = jnp.sum(len_runs, axis=1)
    ptot = (tot + BLOCK_CHUNKS - 1) // BLOCK_CHUNKS * BLOCK_CHUNKS
    pstart = _excl_cumsum(ptot, 0)
    dst_runs = pstart[:, None] + _excl_cumsum(len_runs, 1)
    off = (src_runs - dst_runs).reshape(-1)
    diff = off - jnp.concatenate([jnp.zeros((1,), jnp.int32), off[:-1]])
    q = jnp.arange(q_max, dtype=jnp.int32)
    src = q + jnp.sum(jnp.where(dst_runs.reshape(1, -1) <= q[:, None], diff[None, :], 0), axis=1)
    valid = jnp.any((pstart[None, :] <= q[:, None]) & (q[:, None] < (pstart + tot)[None, :]), axis=1)
    zero_chunk = AREA_CHUNKS - 1
    ahead = jnp.full(((GATHER_BUFS - 1) * BLOCK_CHUNKS,), zero_chunk, jnp.int32)
    src_tab = jnp.concatenate([jnp.where(valid, src, zero_chunk).astype(jnp.int32), ahead])
    qv = jnp.arange(-SCATTER_LAG * BLOCK_CHUNKS, q_max, dtype=jnp.int32)
    dump = n_areas * AREA_CHUNKS + ((qv // BLOCK_CHUNKS) % SCATTER_BUFS) * BLOCK_CHUNKS + qv % BLOCK_CHUNKS
    lead = jnp.zeros((SCATTER_LAG * BLOCK_CHUNKS,), jnp.int32)
    dst_tab = jnp.where(jnp.concatenate([lead > 0, valid]), jnp.concatenate([lead, src]), dump).astype(jnp.int32)
    bstart = jnp.concatenate([pstart, jnp.sum(ptot, keepdims=True)]) // BLOCK_CHUNKS
    return src_tab, dst_tab, bstart.astype(jnp.int32), used8.astype(jnp.int32)


def _rope_tables(pos):
    half = HEAD_DIM // 2
    inv = ROPE_THETA ** (-jnp.arange(half, dtype=F32) / half)
    ang = pos.astype(F32)[:, None] * inv
    cos = jnp.cos(ang)
    sin = jnp.sin(ang)
    reps = LANES // HEAD_DIM
    cos_t = jnp.concatenate([cos, cos] * reps, axis=1)
    sin_t = jnp.concatenate([-sin, sin] * reps, axis=1)
    return cos_t, sin_t


def _full(shape):
    nd = len(shape)
    return pl.BlockSpec(shape, lambda *_: (0,) * nd)


def kernel(x_prompt, x_sample, cache_k, cache_v, state_conv, w_in, b_in, sinks, conv_w, conv_b, ln_conv_g,
           ln_conv_b, w_out, b_out, ln1_g, ln1_b, w_router_group, w_router_expert, w1, w3, w2, ln2_g, ln2_b):
    assert w_in.shape[0] == DEPTH
    bsz, seq, _ = x_prompt.shape
    dec_b, dec_t, _ = x_sample.shape
    assert dec_t == 1 and seq % TILE_Q == 0 and dec_b % SAMPLE_TB == 0 and 2 * dec_b + 7 * N_EXPERTS <= AREA_ROWS
    n_prompt = bsz * seq
    tiles = seq // TILE_Q
    spt = TILE_Q // PROMPT_SCOPE
    p_areas = n_prompt // PROMPT_SCOPE
    n_areas = p_areas + spt
    q_max = (n_areas * AREA_CHUNKS + N_EXPERTS * (BLOCK_CHUNKS - 1) + BLOCK_CHUNKS - 1) // BLOCK_CHUNKS * BLOCK_CHUNKS

    row = lambda a: a.reshape(1, -1)
    qk = ATTN_DIM + KV_DIM
    col = jnp.arange(qk)
    swap = (col // HEAD_DIM) * HEAD_DIM + (col % HEAD_DIM + HEAD_DIM // 2) % HEAD_DIM
    col_scale = jnp.where(jnp.arange(IN_DIM) < ATTN_DIM, ATTN_SCALE, 1.0).astype(F32)
    w_main = w_in[0] * col_scale
    b_main = b_in[0] * col_scale
    w_in_b = jnp.concatenate([w_main, w_main[:, :qk][:, swap]], axis=1).astype(BF16)
    b_in_x = jnp.concatenate([b_main, b_main[:qk][swap]])
    w_out_b = w_out[0].astype(BF16)
    w_r = jnp.concatenate([w_router_group[0], w_router_expert[0].reshape(D_MODEL, N_EXPERTS),
                           jnp.zeros((D_MODEL, LANES - ELANE0 - N_EXPERTS), F32)], axis=1).astype(BF16)
    shared = (w_in_b, row(b_in_x), conv_w[0], row(conv_b[0]), row(ln_conv_g[0]), row(ln_conv_b[0]),
              w_out_b, row(b_out[0]), row(ln1_g[0]), row(ln1_b[0]), w_r)
    shared_specs = [_full(a.shape) for a in shared]
    smem = pl.BlockSpec(memory_space=pltpu.SMEM)
    cos_p, sin_p = _rope_tables(jnp.arange(seq))
    cos_s, sin_s = _rope_tables(16384 + jnp.arange(dec_t))

    cparams = lambda sem: pltpu.CompilerParams(dimension_semantics=sem, vmem_limit_bytes=VMEM_LIMIT)

    ck = cache_k[0].reshape(dec_b, WINDOW, KV_DIM)
    cv = cache_v[0].reshape(dec_b, WINDOW, KV_DIM)
    st = state_conv[0]
    tb = SAMPLE_TB
    h_s, nk_s, nv_s, nc_s, xs_s, route_s, cnt_s = pl.pallas_call(
        _mix_sample_kernel,
        grid=(dec_b // tb,),
        in_specs=[smem,
                  pl.BlockSpec((tb, D_MODEL), lambda t: (t, 0)),
                  _full(cos_s.shape), _full(sin_s.shape),
                  pl.BlockSpec((tb, WINDOW, KV_DIM), lambda t: (t, 0, 0)),
                  pl.BlockSpec((tb, WINDOW, KV_DIM), lambda t: (t, 0, 0)),
                  pl.BlockSpec((tb, CONV_WIDTH - 1, CONV_CH), lambda t: (t, 0, 0))] + shared_specs,
        out_specs=[pl.BlockSpec((tb, D_MODEL), lambda t: (t, 0)),
                   pl.BlockSpec((tb, WINDOW, KV_DIM), lambda t: (t, 0, 0)),
                   pl.BlockSpec((tb, WINDOW, KV_DIM), lambda t: (t, 0, 0)),
                   pl.BlockSpec((tb, CONV_WIDTH - 1, CONV_CH), lambda t: (t, 0, 0)),
                   _full((AREA_ROWS, HALF)),
                   _full((dec_b, LANES)),
                   _full((1, SUBLANES, LANES))],
        out_shape=[jax.ShapeDtypeStruct((dec_b, D_MODEL), F32),
                   jax.ShapeDtypeStruct((dec_b, WINDOW, KV_DIM), F32),
                   jax.ShapeDtypeStruct((dec_b, WINDOW, KV_DIM), F32),
                   jax.ShapeDtypeStruct((dec_b, CONV_WIDTH - 1, CONV_CH), F32),
                   jax.ShapeDtypeStruct((AREA_ROWS, HALF), U32),
                   jax.ShapeDtypeStruct((dec_b, LANES), F32),
                   jax.ShapeDtypeStruct((1, SUBLANES, LANES), F32)],
        scratch_shapes=[pltpu.VMEM((dec_b, D_MODEL), F32)],
        compiler_params=cparams(("arbitrary",)),
        name="mix_sample",
    )(sinks[0], x_sample.reshape(dec_b, D_MODEL), cos_s, sin_s, ck, cv, st, *shared)

    n_real = bsz * tiles
    real = lambda t: jnp.minimum(t, n_real - 1)
    h_p, xs, route_p, cnt_p, nk_p, nv_p, nc_p = pl.pallas_call(
        functools.partial(_mix_prompt_kernel, tiles=tiles),
        grid=(n_real + 1,),
        in_specs=[smem,
                  pl.BlockSpec((1, TILE_Q, D_MODEL), lambda t: (real(t) // tiles, real(t) % tiles, 0)),
                  pl.BlockSpec((TILE_Q, LANES), lambda t: (real(t) % tiles, 0)),
                  pl.BlockSpec((TILE_Q, LANES), lambda t: (real(t) % tiles, 0)),
                  _full((AREA_ROWS, HALF))] + shared_specs,
        out_specs=[pl.BlockSpec((1, TILE_Q, D_MODEL), lambda t: (real(t) // tiles, real(t) % tiles, 0)),
                   pl.BlockSpec((spt * AREA_ROWS, HALF), lambda t: (t, 0)),
                   pl.BlockSpec((TILE_Q, LANES), lambda t: (real(t), 0)),
                   pl.BlockSpec((spt, SUBLANES, LANES), lambda t: (t, 0, 0)),
                   pl.BlockSpec((1, WINDOW, KV_DIM), lambda t: (real(t) // tiles, 0, 0)),
                   pl.BlockSpec((1, WINDOW, KV_DIM), lambda t: (real(t) // tiles, 0, 0)),
                   pl.BlockSpec((1, CONV_WIDTH - 1, CONV_CH), lambda t: (real(t) // tiles, 0, 0))],
        out_shape=[jax.ShapeDtypeStruct((bsz, seq, D_MODEL), F32),
                   jax.ShapeDtypeStruct((n_areas * AREA_ROWS, HALF), U32),
                   jax.ShapeDtypeStruct((n_prompt, LANES), F32),
                   jax.ShapeDtypeStruct((n_areas, SUBLANES, LANES), F32),
                   jax.ShapeDtypeStruct((bsz, WINDOW, KV_DIM), F32),
                   jax.ShapeDtypeStruct((bsz, WINDOW, KV_DIM), F32),
                   jax.ShapeDtypeStruct((bsz, CONV_WIDTH - 1, CONV_CH), F32)],
        scratch_shapes=[pltpu.VMEM((WINDOW, KV_DIM), F32), pltpu.VMEM((WINDOW, KV_DIM), F32),
                        pltpu.VMEM((TILE_Q + 32, CONV_CH), F32),
                        pltpu.VMEM((SUBLANES - 1, TILE_Q + 24, CONV_CH), F32)],
        compiler_params=cparams(("arbitrary",)),
        name="mix_prompt",
    )(sinks[0], x_prompt, cos_p, sin_p, xs_s, *shared)

    cnt = jnp.concatenate([cnt_p[:p_areas, 0], cnt_s[:, 0], cnt_p[p_areas + 1:, 0]], axis=0)
    cnt = cnt[:, ELANE0:ELANE0 + N_EXPERTS].astype(jnp.int32)
    src_tab, dst_tab, bstart, used8 = _plan_blocks(cnt, q_max)
    w_idx = lambda e, *_: (e, 0, 0)
    ys = pl.pallas_call(
        _moe_ffn_kernel,
        grid_spec=pltpu.PrefetchScalarGridSpec(
            num_scalar_prefetch=4,
            grid=(N_EXPERTS,),
            in_specs=[pl.BlockSpec(memory_space=pl.ANY),
                      pl.BlockSpec((1, D_MODEL, D_FF_EXPERT), w_idx),
                      pl.BlockSpec((1, D_MODEL, D_FF_EXPERT), w_idx),
                      pl.BlockSpec((1, D_FF_EXPERT, D_MODEL), w_idx)],
            out_specs=pl.BlockSpec(memory_space=pl.ANY),
            scratch_shapes=[pltpu.VMEM((GATHER_BUFS, BLOCK_ROWS, HALF), U32),
                            pltpu.VMEM((SCATTER_BUFS, BLOCK_ROWS, HALF), U32),
                            pltpu.VMEM((SCATTER_BUFS, BLOCK_ROWS, D_FF_EXPERT), BF16),
                            pltpu.VMEM((ZERO_ROWS, HALF), U32),
                            pltpu.VMEM((D_MODEL, D_FF_EXPERT), BF16), pltpu.VMEM((D_MODEL, D_FF_EXPERT), BF16),
                            pltpu.VMEM((D_FF_EXPERT, D_MODEL), BF16),
                            pltpu.SemaphoreType.DMA((GATHER_BUFS,)), pltpu.SemaphoreType.DMA((SCATTER_BUFS,)),
                            pltpu.SemaphoreType.DMA(()), pltpu.SemaphoreType.DMA(())]),
        out_shape=jax.ShapeDtypeStruct((n_areas * AREA_ROWS + DUMP_CHUNKS * SUBLANES, HALF), U32),
        compiler_params=cparams(("arbitrary",)),
        name="moe_ffn",
    )(src_tab, dst_tab, bstart, used8, xs, w1[0], w3[0], w2[0])

    def combine(h2d, route, first_area, scope, tile):
        n = h2d.shape[0]
        apt = tile // scope
        return pl.pallas_call(
            _moe_combine_kernel,
            grid=(n // tile,),
            in_specs=[pl.BlockSpec((tile, D_MODEL), lambda t: (t, 0)),
                      pl.BlockSpec((tile, LANES), lambda t: (t, 0)),
                      pl.BlockSpec((apt * AREA_ROWS, HALF), lambda t: (first_area // apt + t, 0)),
                      _full((1, D_MODEL)), _full((1, D_MODEL))],
            out_specs=pl.BlockSpec((tile, D_MODEL), lambda t: (t, 0)),
            out_shape=jax.ShapeDtypeStruct((n, D_MODEL), F32),
            compiler_params=cparams(("arbitrary",)),
            name="moe_combine",
        )(h2d, route, ys, row(ln2_g[0]), row(ln2_b[0]))

    y_p = combine(h_p.reshape(n_prompt, D_MODEL), route_p, 0, PROMPT_SCOPE, TILE_Q)
    y_s = combine(h_s, route_s, p_areas, dec_b, dec_b)

    kv_shape = lambda n: (DEPTH, n, WINDOW, N_KV_HEADS, HEAD_DIM)
    return (y_p.reshape(bsz, seq, D_MODEL), y_s.reshape(dec_b, dec_t, D_MODEL),
            nk_p.reshape(kv_shape(bsz)), nv_p.reshape(kv_shape(bsz)), nc_p[None],
            nk_s.reshape(kv_shape(dec_b)), nv_s.reshape(kv_shape(dec_b)), nc_s[None])
```

```python
import functools

import jax
import jax.numpy as jnp
from jax import lax
from jax.experimental import pallas as pl
from jax.experimental.pallas import tpu as pltpu

F32 = jnp.float32
BF16 = jnp.bfloat16
U32 = jnp.uint32

D_MODEL = 1024
N_HEADS = 8
N_KV_HEADS = 2
HEAD_DIM = 64
Q_GROUP = N_HEADS // N_KV_HEADS
ATTN_DIM = N_HEADS * HEAD_DIM
KV_DIM = N_KV_HEADS * HEAD_DIM
WINDOW = 128
ROPE_THETA = 10000.0
ATTN_SCALE = HEAD_DIM ** -0.5
CONV_CH = D_MODEL - ATTN_DIM
CONV_WIDTH = 31
IN_DIM = ATTN_DIM + 2 * KV_DIM + 2 * CONV_CH
N_GROUPS = 4
EXPERTS_PER_GROUP = 8
N_EXPERTS = N_GROUPS * EXPERTS_PER_GROUP
D_FF_EXPERT = 256
LN_EPS = 1e-5
NEG_INF = -1e30
DEPTH = 1
ALPHA = (2.0 * DEPTH) ** 0.25

LANES = 128
SUBLANES = 8
HALF = D_MODEL // 2
ELANE0 = N_GROUPS
PROMPT_SCOPE = 256
AREA_ROWS = 768
AREA_CHUNKS = AREA_ROWS // SUBLANES
BLOCK_CHUNKS = 32
BLOCK_ROWS = BLOCK_CHUNKS * SUBLANES
TILE_Q = 512
SAMPLE_TB = 32
SAMPLE_GRP = 8
VMEM_LIMIT = 56 * 1024 * 1024


def _layer_norm(x, g, b):
    mu = jnp.mean(x, axis=-1, keepdims=True)
    xc = x - mu
    var = jnp.mean(xc * xc, axis=-1, keepdims=True)
    return xc * lax.rsqrt(var + LN_EPS) * g + b


def _rope(x, x_swapped, cos, sin_signed):
    pieces = [x[:, g * LANES:(g + 1) * LANES] * cos + x_swapped[:, g * LANES:(g + 1) * LANES] * sin_signed
              for g in range(x.shape[-1] // LANES)]
    return pieces[0] if len(pieces) == 1 else jnp.concatenate(pieces, axis=-1)


def _in_proj_glu(xb, w_in_ref, b_in_ref):
    lo, hi = ATTN_DIM + 2 * KV_DIM, IN_DIM
    cacg = jnp.dot(xb, w_in_ref[:, lo:hi], preferred_element_type=F32) + b_in_ref[:, lo:hi]
    return cacg[:, :CONV_CH] * jax.nn.sigmoid(cacg[:, CONV_CH:])


def _in_proj_qkv(xb, w_in_ref, b_in_ref, cos, sin_signed):
    o_v, o_e = ATTN_DIM + KV_DIM, ATTN_DIM + 2 * KV_DIM
    qkv = jnp.dot(xb, w_in_ref[:, 0:o_e], preferred_element_type=F32) + b_in_ref[:, 0:o_e]
    sw = jnp.dot(xb, w_in_ref[:, IN_DIM:], preferred_element_type=F32) + b_in_ref[:, IN_DIM:]
    q = _rope(qkv[:, :ATTN_DIM], sw[:, :ATTN_DIM], cos, sin_signed)
    k = _rope(qkv[:, ATTN_DIM:o_v], sw[:, ATTN_DIM:], cos, sin_signed)
    return q, k, qkv[:, o_v:]


def _pack_rows(v):
    lo = lax.bitcast_convert_type(v[:, :HALF], U32)
    hi = lax.bitcast_convert_type(v[:, HALF:], U32)
    return (lo >> 16) | (hi & jnp.uint32(0xFFFF0000))


def _unpack_rows(w):
    lo = lax.bitcast_convert_type(w << 16, F32).astype(BF16)
    hi = lax.bitcast_convert_type(w & jnp.uint32(0xFFFF0000), F32).astype(BF16)
    return lo, hi


def _route_and_sort(h, w_r, scope):
    n = h.shape[0]
    hb = h.astype(BF16)
    logits = jnp.dot(hb, w_r, preferred_element_type=F32)
    lane = lax.broadcasted_iota(jnp.int32, (n, LANES), 1)
    lane_f = lane.astype(F32)
    is_g = lane < N_GROUPS
    lg = jnp.where(is_g, logits, -jnp.inf)
    gmax = jnp.max(lg, axis=-1, keepdims=True)
    gidx = jnp.min(jnp.where(lg == gmax, lane_f, float(LANES)), axis=-1, keepdims=True)
    p_g = 1.0 / jnp.sum(jnp.where(is_g, jnp.exp(logits - gmax), 0.0), axis=-1, keepdims=True)
    egrp = ((lane - ELANE0) >> 3).astype(F32)
    emask = (lane >= ELANE0) & (lane < ELANE0 + N_EXPERTS) & (egrp == gidx)
    sel = jnp.where(emask, logits, -jnp.inf)
    v1 = jnp.max(sel, axis=-1, keepdims=True)
    i1 = jnp.min(jnp.where(sel == v1, lane_f, float(LANES)), axis=-1, keepdims=True)
    sel2 = jnp.where(lane_f == i1, -jnp.inf, sel)
    v2 = jnp.max(sel2, axis=-1, keepdims=True)
    i2 = jnp.min(jnp.where(sel2 == v2, lane_f, float(LANES)), axis=-1, keepdims=True)
    t = jnp.exp(v2 - v1)
    gate1 = p_g / (1.0 + t)
    gate2 = p_g * t / (1.0 + t)
    pick1 = lane_f == i1
    pick2 = lane_f == i2
    member = jnp.where(pick1 | pick2, 1.0, 0.0)

    r_i = lax.broadcasted_iota(jnp.int32, (scope, scope), 0)
    c_i = lax.broadcasted_iota(jnp.int32, (scope, scope), 1)
    lower = jnp.where(c_i < r_i, 1.0, 0.0).astype(BF16)
    e_r = lax.broadcasted_iota(jnp.int32, (LANES, LANES), 0)
    e_c = lax.broadcasted_iota(jnp.int32, (LANES, LANES), 1)
    upper = jnp.where(e_r < e_c, 1.0, 0.0).astype(BF16)
    s_iota = lax.broadcasted_iota(jnp.int32, (scope, AREA_ROWS), 1).astype(F32)

    slots1, slots2, areas, counts = [], [], [], []
    for sc in range(n // scope):
        rows = slice(sc * scope, (sc + 1) * scope)
        m_sc = member[rows]
        before = jnp.dot(lower, m_sc.astype(BF16), preferred_element_type=F32)
        cnt = jnp.sum(m_sc, axis=0, keepdims=True)
        c8 = jnp.floor((cnt + (SUBLANES - 1)) * (1.0 / SUBLANES))
        c8b = jnp.broadcast_to(c8, (SUBLANES, LANES)).astype(BF16)
        off8 = jnp.dot(c8b, upper, preferred_element_type=F32)[0:1]
        slot_all = before + off8 * float(SUBLANES)
        s1 = jnp.sum(jnp.where(pick1[rows], slot_all, 0.0), axis=-1, keepdims=True)
        s2 = jnp.sum(jnp.where(pick2[rows], slot_all, 0.0), axis=-1, keepdims=True)
        perm_t = jnp.where((s_iota == s1) | (s_iota == s2), 1.0, 0.0).astype(BF16)
        sorted_rows = lax.dot_general(perm_t, hb[rows], (((0,), (0,)), ((), ())),
                                      preferred_element_type=F32)
        areas.append(_pack_rows(sorted_rows))
        counts.append(cnt)
        slots1.append(s1)
        slots2.append(s2)
    slot1 = jnp.concatenate(slots1, axis=0) if len(slots1) > 1 else slots1[0]
    slot2 = jnp.concatenate(slots2, axis=0) if len(slots2) > 1 else slots2[0]
    route = jnp.where(lane == 0, slot1, jnp.where(lane == 1, slot2, jnp.where(lane == 2, gate1,
                      jnp.where(lane == 3, gate2, 0.0))))
    return route, areas, counts


def _softmax_rows_with_sink(s, sink, extra=None):
    m = jnp.maximum(jnp.max(s, axis=-1, keepdims=True), sink)
    if extra is not None:
        m = jnp.maximum(m, extra)
    e = jnp.exp(s - m)
    den = jnp.sum(e, axis=-1, keepdims=True) + jnp.exp(sink - m)
    if extra is not None:
        ee = jnp.exp(extra - m)
        den = den + ee
        return e / den, ee / den
    return e / den, None


def _mix_tail(x, attn, conv_pre, lcg, lcb, w_out_ref, b_out, l1g, l1b):
    conv = _layer_norm(conv_pre, lcg, lcb)
    conv = conv * jax.nn.sigmoid(conv)
    mix = (jnp.dot(attn.astype(BF16), w_out_ref[0:ATTN_DIM, :], preferred_element_type=F32)
           + jnp.dot(conv.astype(BF16), w_out_ref[ATTN_DIM:, :], preferred_element_type=F32) + b_out)
    return _layer_norm(ALPHA * x + mix, l1g, l1b)


def _mix_prompt_kernel(sinks_ref, x_ref, cos_ref, sin_ref, xs_s_ref, *rest, tiles):
    xs_ref = rest[12]
    cnt_ref = rest[14]
    step = pl.program_id(0)
    n_real = pl.num_programs(0) - 1

    @pl.when(step < n_real)
    def _():
        _mix_prompt_tile(sinks_ref, x_ref, cos_ref, sin_ref, *rest, i=step % tiles)

    @pl.when(step == n_real)
    def _():
        xs_ref[0:AREA_ROWS, :] = xs_s_ref[...]
        xs_ref[AREA_ROWS:, :] = jnp.zeros((xs_ref.shape[0] - AREA_ROWS, HALF), U32)
        cnt_ref[...] = jnp.zeros(cnt_ref.shape, F32)


def _mix_prompt_tile(sinks_ref, x_ref, cos_ref, sin_ref, w_in_ref, b_in_ref, conv_w_ref, conv_b_ref,
                     lcg_ref, lcb_ref, w_out_ref, b_out_ref, l1g_ref, l1b_ref, w_r_ref,
                     h_ref, xs_ref, route_ref, cnt_ref, nk_ref, nv_ref, nc_ref,
                     kprev, vprev, u_scr, u_shift, *, i):
    tq = x_ref.shape[1]
    pre = 32

    @pl.when(i == 0)
    def _():
        kprev[...] = jnp.zeros_like(kprev)
        vprev[...] = jnp.zeros_like(vprev)
        u_scr[0:pre, :] = jnp.zeros((pre, CONV_CH), F32)

    x = x_ref[0]
    xb = x.astype(BF16)

    u_scr[pre:pre + tq, :] = _in_proj_glu(xb, w_in_ref, b_in_ref)
    shift_rows = u_shift.shape[1]
    for r in range(1, SUBLANES):
        u_shift[r - 1] = u_scr[r:r + shift_rows, :]
    acc = jnp.zeros((tq, CONV_CH), F32)
    for j in range(CONV_WIDTH):
        a, r = divmod(pre - (CONV_WIDTH - 1) + j, SUBLANES)
        src = u_scr[a * SUBLANES:a * SUBLANES + tq, :] if r == 0 else u_shift[r - 1, a * SUBLANES:a * SUBLANES + tq, :]
        acc = acc + src * conv_w_ref[j:j + 1, :]
    conv_pre = acc + conv_b_ref[...]

    nc_ref[0] = u_scr[pre + tq - (CONV_WIDTH - 1):pre + tq, :]
    tail = u_scr[tq:tq + pre, :]
    u_scr[0:pre, :] = tail

    q, k, v = _in_proj_qkv(xb, w_in_ref, b_in_ref, cos_ref[...], sin_ref[...])
    qb = q.astype(BF16)
    kb = jnp.concatenate([kprev[...], k], axis=0).astype(BF16)
    vb = jnp.concatenate([vprev[...], v], axis=0).astype(BF16)
    row = lax.broadcasted_iota(jnp.int32, (WINDOW, 2 * WINDOW), 0)
    col = lax.broadcasted_iota(jnp.int32, (WINDOW, 2 * WINDOW), 1)
    dist = row + WINDOW - col
    band = (dist >= 0) & (dist < WINDOW)
    first_col = jnp.where(i > 0, 0, WINDOW)
    attn_blocks = []
    for j in range(tq // WINDOW):
        msk = (band & (col >= first_col)) if j == 0 else band
        heads = [None] * N_HEADS
        for c in range(N_KV_HEADS):
            kc = kb[j * WINDOW:(j + 2) * WINDOW, c * HEAD_DIM:(c + 1) * HEAD_DIM]
            vc = vb[j * WINDOW:(j + 2) * WINDOW, c * HEAD_DIM:(c + 1) * HEAD_DIM]
            qc = jnp.concatenate(
                [qb[j * WINDOW:(j + 1) * WINDOW, (c * Q_GROUP + g) * HEAD_DIM:(c * Q_GROUP + g + 1) * HEAD_DIM]
                 for g in range(Q_GROUP)], axis=0)
            s = lax.dot_general(qc, kc, (((1,), (1,)), ((), ())), preferred_element_type=F32)
            probs = []
            for g in range(Q_GROUP):
                sg = jnp.where(msk, s[g * WINDOW:(g + 1) * WINDOW], NEG_INF)
                p, _ = _softmax_rows_with_sink(sg, sinks_ref[c * Q_GROUP + g])
                probs.append(p.astype(BF16))
            o = jnp.dot(jnp.concatenate(probs, axis=0), vc, preferred_element_type=F32)
            for g in range(Q_GROUP):
                heads[c * Q_GROUP + g] = o[g * WINDOW:(g + 1) * WINDOW]
        attn_blocks.append(jnp.concatenate(heads, axis=1))
    attn = jnp.concatenate(attn_blocks, axis=0)
    kprev[...] = k[tq - WINDOW:, :]
    vprev[...] = v[tq - WINDOW:, :]
    nk_ref[0] = k[tq - WINDOW:, :]
    nv_ref[0] = v[tq - WINDOW:, :]

    for sc in range(tq // PROMPT_SCOPE):
        rows = slice(sc * PROMPT_SCOPE, (sc + 1) * PROMPT_SCOPE)
        h = _mix_tail(x[rows], attn[rows], conv_pre[rows], lcg_ref[...], lcb_ref[...], w_out_ref, b_out_ref[...],
                      l1g_ref[...], l1b_ref[...])
        h_ref[0, rows, :] = h
        route, areas, counts = _route_and_sort(h, w_r_ref[...], PROMPT_SCOPE)
        route_ref[rows, :] = route
        xs_ref[sc * AREA_ROWS:(sc + 1) * AREA_ROWS, :] = areas[0]
        cnt_ref[sc] = jnp.broadcast_to(counts[0], (SUBLANES, LANES))


def _mix_sample_kernel(sinks_ref, x_ref, cos_ref, sin_ref, ck_ref, cv_ref, st_ref, w_in_ref, b_in_ref,
                       conv_w_ref, conv_b_ref, lcg_ref, lcb_ref, w_out_ref, b_out_ref, l1g_ref, l1b_ref,
                       w_r_ref,
                       h_ref, nk_ref, nv_ref, nc_ref, xs_ref, route_ref, cnt_ref,
                       h_scr):
    tb = x_ref.shape[0]
    n_tok = h_scr.shape[0]
    step = pl.program_id(0)
    x = x_ref[...]
    xb = x.astype(BF16)
    u = _in_proj_glu(xb, w_in_ref, b_in_ref)
    q, k_new, v_new = _in_proj_qkv(xb, w_in_ref, b_in_ref, cos_ref[...], sin_ref[...])

    qb = q.astype(BF16)
    knb = k_new.astype(BF16).astype(F32)
    vnb = v_new.astype(BF16).astype(F32)
    gq = SAMPLE_GRP * Q_GROUP
    nkeys = SAMPLE_GRP * WINDOW
    r_i = lax.broadcasted_iota(jnp.int32, (gq, nkeys), 0)
    c_i = lax.broadcasted_iota(jnp.int32, (gq, nkeys), 1)
    msk = ((c_i >> 7) == (r_i & (SAMPLE_GRP - 1))) & ((c_i & (WINDOW - 1)) >= 1)
    r_col = lax.broadcasted_iota(jnp.int32, (gq, 1), 0) >> 3
    attn_groups = []
    for grp in range(tb // SAMPLE_GRP):
        r0 = grp * SAMPLE_GRP
        kc_all = ck_ref[r0:r0 + SAMPLE_GRP].reshape(nkeys, KV_DIM).astype(BF16)
        vc_all = cv_ref[r0:r0 + SAMPLE_GRP].reshape(nkeys, KV_DIM).astype(BF16)
        heads = [None] * N_HEADS
        for c in range(N_KV_HEADS):
            lanes = slice(c * HEAD_DIM, (c + 1) * HEAD_DIM)
            qc = jnp.concatenate(
                [qb[r0:r0 + SAMPLE_GRP, (c * Q_GROUP + g) * HEAD_DIM:(c * Q_GROUP + g + 1) * HEAD_DIM]
                 for g in range(Q_GROUP)], axis=0)
            s = lax.dot_general(qc, kc_all[:, lanes], (((1,), (1,)), ((), ())), preferred_element_type=F32)
            s = jnp.where(msk, s, NEG_INF)
            kn = jnp.concatenate([knb[r0:r0 + SAMPLE_GRP, lanes]] * Q_GROUP, axis=0)
            vn = jnp.concatenate([vnb[r0:r0 + SAMPLE_GRP, lanes]] * Q_GROUP, axis=0)
            s_new = jnp.sum(qc.astype(F32) * kn, axis=-1, keepdims=True)
            sink = jnp.zeros((gq, 1), F32)
            for g in range(Q_GROUP):
                sink = jnp.where(r_col == g, sinks_ref[c * Q_GROUP + g], sink)
            p, p_new = _softmax_rows_with_sink(s, sink, s_new)
            o = (jnp.dot(p.astype(BF16), vc_all[:, lanes], preferred_element_type=F32)
                 + p_new.astype(BF16).astype(F32) * vn)
            for g in range(Q_GROUP):
                heads[c * Q_GROUP + g] = o[g * SAMPLE_GRP:(g + 1) * SAMPLE_GRP]
        attn_groups.append(jnp.concatenate(heads, axis=1))
    attn = jnp.concatenate(attn_groups, axis=0)

    st = st_ref[...]
    acc = jnp.sum(st * conv_w_ref[0:CONV_WIDTH - 1, :][None], axis=1)
    conv_pre = acc + u * conv_w_ref[CONV_WIDTH - 1:CONV_WIDTH, :] + conv_b_ref[...]

    nk_ref[:, 0:WINDOW - 1, :] = ck_ref[:, 1:WINDOW, :]
    nv_ref[:, 0:WINDOW - 1, :] = cv_ref[:, 1:WINDOW, :]
    nc_ref[:, 0:CONV_WIDTH - 2, :] = st_ref[:, 1:CONV_WIDTH - 1, :]
    for b in range(tb):
        nk_ref[b, WINDOW - 1:WINDOW, :] = k_new[b:b + 1, :]
        nv_ref[b, WINDOW - 1:WINDOW, :] = v_new[b:b + 1, :]
        nc_ref[b, CONV_WIDTH - 2:CONV_WIDTH - 1, :] = u[b:b + 1, :]

    h = _mix_tail(x, attn, conv_pre, lcg_ref[...], lcb_ref[...], w_out_ref, b_out_ref[...],
                  l1g_ref[...], l1b_ref[...])
    h_ref[...] = h
    h_scr[pl.ds(pl.multiple_of(step * tb, tb), tb), :] = h

    @pl.when(step == pl.num_programs(0) - 1)
    def _():
        route, areas, counts = _route_and_sort(h_scr[...], w_r_ref[...], n_tok)
        route_ref[...] = route
        xs_ref[...] = areas[0]
        cnt_ref[0] = jnp.broadcast_to(counts[0], (SUBLANES, LANES))


GATHER_BUFS = 8
SCATTER_BUFS = 8
SCATTER_LAG = 2
ZERO_BITS = (64, 32, 16, 8, 4, 2, 1)
DUMP_CHUNKS = SCATTER_BUFS * BLOCK_CHUNKS
ZERO_ROWS = max(ZERO_BITS[0], DUMP_CHUNKS) * SUBLANES


def _moe_ffn_kernel(src_ref, dst_ref, bstart_ref, used_ref,
                    xs_hbm, w1_ref, w3_ref, w2_ref,
                    ys_hbm,
                    xbuf, ybuf, hbuf, zbuf, w1b, w3b, w2b, sem_in, sem_out, sem_zero, sem_dump):
    e = pl.program_id(0)
    n_areas = used_ref.shape[0]
    total = bstart_ref[N_EXPERTS]
    dump_row0 = n_areas * AREA_ROWS
    lookahead = GATHER_BUFS - 1
    any_rows = pl.ds(0, SUBLANES)

    def chunk_at(ref, idx):
        return pl.ds(pl.multiple_of(ref[idx] * SUBLANES, SUBLANES), SUBLANES)

    def in_copy(rows, sl, c):
        return pltpu.make_async_copy(xs_hbm.at[rows], xbuf.at[sl, pl.ds(c * SUBLANES, SUBLANES)], sem_in.at[sl])

    def out_copy(rows, sl, c):
        return pltpu.make_async_copy(ybuf.at[sl, pl.ds(c * SUBLANES, SUBLANES)], ys_hbm.at[rows], sem_out.at[sl])

    def start_gather(blk):
        for c in range(BLOCK_CHUNKS):
            in_copy(chunk_at(src_ref, blk * BLOCK_CHUNKS + c), blk % GATHER_BUFS, c).start()

    def wait_gather(blk):
        for c in range(BLOCK_CHUNKS):
            in_copy(any_rows, blk % GATHER_BUFS, c).wait()

    def start_scatter(blk):
        for c in range(BLOCK_CHUNKS):
            out_copy(chunk_at(dst_ref, (blk + SCATTER_LAG) * BLOCK_CHUNKS + c), blk % SCATTER_BUFS, c).start()

    def wait_scatter(sl):
        for c in range(BLOCK_CHUNKS):
            out_copy(any_rows, sl, c).wait()

    def load_rows(blk):
        return _unpack_rows(xbuf[blk % GATHER_BUFS])

    def issue_dmas(blk):
        start_scatter(blk - SCATTER_LAG)
        start_gather(blk + lookahead)

    def up_gate(blk, rows):
        x_lo, x_hi = rows
        a1 = (jnp.dot(x_lo, w1b[0:HALF, :], preferred_element_type=F32)
              + jnp.dot(x_hi, w1b[HALF:, :], preferred_element_type=F32))
        a3 = (jnp.dot(x_lo, w3b[0:HALF, :], preferred_element_type=F32)
              + jnp.dot(x_hi, w3b[HALF:, :], preferred_element_type=F32))
        hbuf[blk % SCATTER_BUFS] = (a1 * jax.nn.sigmoid(a1) * a3).astype(BF16)

    def down(blk):
        y = jnp.dot(hbuf[blk % SCATTER_BUFS], w2b[...], preferred_element_type=F32)
        ybuf[blk % SCATTER_BUFS] = _pack_rows(y.astype(BF16).astype(F32))

    def for_each_tail_piece(fn):
        def area_body(s, carry):
            used = used_ref[s]
            tail = AREA_CHUNKS - used
            row = (s * AREA_CHUNKS + used) * SUBLANES
            for bit in ZERO_BITS:
                take = (tail & bit) != 0

                @pl.when(take)
                def _(row=row, bit=bit):
                    fn(pltpu.make_async_copy(zbuf.at[pl.ds(0, bit * SUBLANES)],
                                             ys_hbm.at[pl.ds(pl.multiple_of(row, SUBLANES), bit * SUBLANES)],
                                             sem_zero))
                row = row + jnp.where(take, bit * SUBLANES, 0)
            return carry
        lax.fori_loop(0, n_areas, area_body, 0)

    @pl.when(e == 0)
    def _():
        zbuf[...] = jnp.zeros(zbuf.shape, U32)
        ybuf[...] = jnp.zeros(ybuf.shape, U32)
        dump = pltpu.make_async_copy(zbuf.at[pl.ds(0, DUMP_CHUNKS * SUBLANES)],
                                     ys_hbm.at[pl.ds(dump_row0, DUMP_CHUNKS * SUBLANES)], sem_dump)
        dump.start()
        dump.wait()
        for_each_tail_piece(lambda cp: cp.start())
        for sl in range(SCATTER_BUFS - SCATTER_LAG):
            for c in range(BLOCK_CHUNKS):
                out_copy(pl.ds(dump_row0 + (sl * BLOCK_CHUNKS + c) * SUBLANES, SUBLANES), sl, c).start()
        for k in range(lookahead):
            start_gather(k)

    w1b[...] = w1_ref[0].astype(BF16)
    w3b[...] = w3_ref[0].astype(BF16)
    w2b[...] = w2_ref[0].astype(BF16)
    b0 = bstart_ref[e]
    b1 = bstart_ref[e + 1]

    @pl.when(b1 > b0)
    def _():
        wait_gather(b0)
        rows = load_rows(b0)
        issue_dmas(b0)
        up_gate(b0, rows)

    def block_body(blk, carry):
        wait_gather(blk)
        wait_scatter((blk - 1) % SCATTER_BUFS)
        rows = load_rows(blk)
        issue_dmas(blk)
        down(blk - 1)
        up_gate(blk, rows)
        return carry

    lax.fori_loop(b0 + 1, b1, block_body, 0)

    @pl.when(b1 > b0)
    def _():
        wait_scatter((b1 - 1) % SCATTER_BUFS)
        down(b1 - 1)

    @pl.when(e == N_EXPERTS - 1)
    def _():
        for k in range(SCATTER_LAG):
            start_scatter(total - SCATTER_LAG + k)
        for sl in range(SCATTER_BUFS):
            wait_scatter(sl)
        for k in range(lookahead):
            wait_gather(total + k)
        for_each_tail_piece(lambda cp: cp.wait())


def _moe_combine_kernel(h_ref, route_ref, ys_ref, l2g_ref, l2b_ref, y_ref):
    n = h_ref.shape[0]
    scope = n // (ys_ref.shape[0] // AREA_ROWS)
    s_iota = lax.broadcasted_iota(jnp.int32, (scope, AREA_ROWS), 1).astype(F32)
    for sc in range(n // scope):
        rows = slice(sc * scope, (sc + 1) * scope)
        route = route_ref[rows, :]
        slot1 = route[:, 0:1]
        slot2 = route[:, 1:2]
        gate1 = route[:, 2:3]
        gate2 = route[:, 3:4]
        gmat = (jnp.where(s_iota == slot1, gate1, 0.0) + jnp.where(s_iota == slot2, gate2, 0.0)).astype(BF16)
        y_lo, y_hi = _unpack_rows(ys_ref[sc * AREA_ROWS:(sc + 1) * AREA_ROWS, :])
        f = jnp.concatenate([jnp.dot(gmat, y_lo, preferred_element_type=F32),
                             jnp.dot(gmat, y_hi, preferred_element_type=F32)], axis=1)
        y_ref[rows, :] = _layer_norm(ALPHA * h_ref[rows, :] + f, l2g_ref[...], l2b_ref[...])


def _excl_cumsum(a, axis):
    return jnp.cumsum(a, axis=axis) - a


def _plan_blocks(cnt, q_max):
    n_areas = cnt.shape[0]
    c8 = (cnt + (SUBLANES - 1)) // SUBLANES
    used8 = jnp.sum(c8, axis=1)
    base8 = jnp.arange(n_areas, dtype=jnp.int32) * AREA_CHUNKS
    src_runs = (base8[:, None] + _excl_cumsum(c8, 1)).T
    len_runs = c8.T
    tot = jnp.sum(len_runs, axis=1)
    ptot = (tot + BLOCK_CHUNKS - 1) // BLOCK_CHUNKS * BLOCK_CHUNKS
    pstart = _excl_cumsum(ptot, 0)
    dst_runs = pstart[:, None] + _excl_cumsum(len_runs, 1)
    off = (src_runs - dst_runs).reshape(-1)
    diff = off - jnp.concatenate([jnp.zeros((1,), jnp.int32), off[:-1]])
    q = jnp.arange(q_max, dtype=jnp.int32)
    src = q + jnp.sum(jnp.where(dst_runs.reshape(1, -1) <= q[:, None], diff[None, :], 0), axis=1)
    valid = jnp.any((pstart[None, :] <= q[:, None]) & (q[:, None] < (pstart + tot)[None, :]), axis=1)
    zero_chunk = AREA_CHUNKS - 1
    ahead = jnp.full(((GATHER_BUFS - 1) * BLOCK_CHUNKS,), zero_chunk, jnp.int32)
    src_tab = jnp.concatenate([jnp.where(valid, src, zero_chunk).astype(jnp.int32), ahead])
    qv = jnp.arange(-SCATTER_LAG * BLOCK_CHUNKS, q_max, dtype=jnp.int32)
    dump = n_areas * AREA_CHUNKS + ((qv // BLOCK_CHUNKS) % SCATTER_BUFS) * BLOCK_CHUNKS + qv % BLOCK_CHUNKS
    lead = jnp.zeros((SCATTER_LAG * BLOCK_CHUNKS,), jnp.int32)
    dst_tab = jnp.where(jnp.concatenate([lead > 0, valid]), jnp.concatenate([lead, src]), dump).astype(jnp.int32)
    bstart = jnp.concatenate([pstart, jnp.sum(ptot, keepdims=True)]) // BLOCK_CHUNKS
    return src_tab, dst_tab, bstart.astype(jnp.int32), used8.astype(jnp.int32)


def _rope_tables(pos):
    half = HEAD_DIM // 2
    inv = ROPE_THETA ** (-jnp.arange(half, dtype=F32) / half)
    ang = pos.astype(F32)[:, None] * inv
    cos = jnp.cos(ang)
    sin = jnp.sin(ang)
    reps = LANES // HEAD_DIM
    cos_t = jnp.concatenate([cos, cos] * reps, axis=1)
    sin_t = jnp.concatenate([-sin, sin] * reps, axis=1)
    return cos_t, sin_t


def _full(shape):
    nd = len(shape)
    return pl.BlockSpec(shape, lambda *_: (0,) * nd)


def kernel(x_prompt, x_sample, cache_k, cache_v, state_conv, w_in, b_in, sinks, conv_w, conv_b, ln_conv_g,
           ln_conv_b, w_out, b_out, ln1_g, ln1_b, w_router_group, w_router_expert, w1, w3, w2, ln2_g, ln2_b):
    assert w_in.shape[0] == DEPTH
    bsz, seq, _ = x_prompt.shape
    dec_b, dec_t, _ = x_sample.shape
    assert dec_t == 1 and seq % TILE_Q == 0 and dec_b % SAMPLE_TB == 0 and 2 * dec_b + 7 * N_EXPERTS <= AREA_ROWS
    n_prompt = bsz * seq
    tiles = seq // TILE_Q
    spt = TILE_Q // PROMPT_SCOPE
    p_areas = n_prompt // PROMPT_SCOPE
    n_areas = p_areas + spt
    q_max = (n_areas * AREA_CHUNKS + N_EXPERTS * (BLOCK_CHUNKS - 1) + BLOCK_CHUNKS - 1) // BLOCK_CHUNKS * BLOCK_CHUNKS

    row = lambda a: a.reshape(1, -1)
    qk = ATTN_DIM + KV_DIM
    col = jnp.arange(qk)
    swap = (col // HEAD_DIM) * HEAD_DIM + (col % HEAD_DIM + HEAD_DIM // 2) % HEAD_DIM
    col_scale = jnp.where(jnp.arange(IN_DIM) < ATTN_DIM, ATTN_SCALE, 1.0).astype(F32)
    w_main = w_in[0] * col_scale
    b_main = b_in[0] * col_scale
    w_in_b = jnp.concatenate([w_main, w_main[:, :qk][:, swap]], axis=1).astype(BF16)
    b_in_x = jnp.concatenate([b_main, b_main[:qk][swap]])
    w_out_b = w_out[0].astype(BF16)
    w_r = jnp.concatenate([w_router_group[0], w_router_expert[0].reshape(D_MODEL, N_EXPERTS),
                           jnp.zeros((D_MODEL, LANES - ELANE0 - N_EXPERTS), F32)], axis=1).astype(BF16)
    shared = (w_in_b, row(b_in_x), conv_w[0], row(conv_b[0]), row(ln_conv_g[0]), row(ln_conv_b[0]),
              w_out_b, row(b_out[0]), row(ln1_g[0]), row(ln1_b[0]), w_r)
    shared_specs = [_full(a.shape) for a in shared]
    smem = pl.BlockSpec(memory_space=pltpu.SMEM)
    cos_p, sin_p = _rope_tables(jnp.arange(seq))
    cos_s, sin_s = _rope_tables(16384 + jnp.arange(dec_t))

    cparams = lambda sem: pltpu.CompilerParams(dimension_semantics=sem, vmem_limit_bytes=VMEM_LIMIT)

    ck = cache_k[0].reshape(dec_b, WINDOW, KV_DIM)
    cv = cache_v[0].reshape(dec_b, WINDOW, KV_DIM)
    st = state_conv[0]
    tb = SAMPLE_TB
    h_s, nk_s, nv_s, nc_s, xs_s, route_s, cnt_s = pl.pallas_call(
        _mix_sample_kernel,
        grid=(dec_b // tb,),
        in_specs=[smem,
                  pl.BlockSpec((tb, D_MODEL), lambda t: (t, 0)),
                  _full(cos_s.shape), _full(sin_s.shape),
                  pl.BlockSpec((tb, WINDOW, KV_DIM), lambda t: (t, 0, 0)),
                  pl.BlockSpec((tb, WINDOW, KV_DIM), lambda t: (t, 0, 0)),
                  pl.BlockSpec((tb, CONV_WIDTH - 1, CONV_CH), lambda t: (t, 0, 0))] + shared_specs,
        out_specs=[pl.BlockSpec((tb, D_MODEL), lambda t: (t, 0)),
                   pl.BlockSpec((tb, WINDOW, KV_DIM), lambda t: (t, 0, 0)),
                   pl.BlockSpec((tb, WINDOW, KV_DIM), lambda t: (t, 0, 0)),
                   pl.BlockSpec((tb, CONV_WIDTH - 1, CONV_CH), lambda t: (t, 0, 0)),
                   _full((AREA_ROWS, HALF)),
                   _full((dec_b, LANES)),
                   _full((1, SUBLANES, LANES))],
        out_shape=[jax.ShapeDtypeStruct((dec_b, D_MODEL), F32),
                   jax.ShapeDtypeStruct((dec_b, WINDOW, KV_DIM), F32),
                   jax.ShapeDtypeStruct((dec_b, WINDOW, KV_DIM), F32),
                   jax.ShapeDtypeStruct((dec_b, CONV_WIDTH - 1, CONV_CH), F32),
                   jax.ShapeDtypeStruct((AREA_ROWS, HALF), U32),
                   jax.ShapeDtypeStruct((dec_b, LANES), F32),
                   jax.ShapeDtypeStruct((1, SUBLANES, LANES), F32)],
        scratch_shapes=[pltpu.VMEM((dec_b, D_MODEL), F32)],
        compiler_params=cparams(("arbitrary",)),
        name="mix_sample",
    )(sinks[0], x_sample.reshape(dec_b, D_MODEL), cos_s, sin_s, ck, cv, st, *shared)

    n_real = bsz * tiles
    real = lambda t: jnp.minimum(t, n_real - 1)
    h_p, xs, route_p, cnt_p, nk_p, nv_p, nc_p = pl.pallas_call(
        functools.partial(_mix_prompt_kernel, tiles=tiles),
        grid=(n_real + 1,),
        in_specs=[smem,
                  pl.BlockSpec((1, TILE_Q, D_MODEL), lambda t: (real(t) // tiles, real(t) % tiles, 0)),
                  pl.BlockSpec((TILE_Q, LANES), lambda t: (real(t) % tiles, 0)),
                  pl.BlockSpec((TILE_Q, LANES), lambda t: (real(t) % tiles, 0)),
                  _full((AREA_ROWS, HALF))] + shared_specs,
        out_specs=[pl.BlockSpec((1, TILE_Q, D_MODEL), lambda t: (real(t) // tiles, real(t) % tiles, 0)),
                   pl.BlockSpec((spt * AREA_ROWS, HALF), lambda t: (t, 0)),
                   pl.BlockSpec((TILE_Q, LANES), lambda t: (real(t), 0)),
                   pl.BlockSpec((spt, SUBLANES, LANES), lambda t: (t, 0, 0)),
                   pl.BlockSpec((1, WINDOW, KV_DIM), lambda t: (real(t) // tiles, 0, 0)),
                   pl.BlockSpec((1, WINDOW, KV_DIM), lambda t: (real(t) // tiles, 0, 0)),
                   pl.BlockSpec((1, CONV_WIDTH - 1, CONV_CH), lambda t: (real(t) // tiles, 0, 0))],
        out_shape=[jax.ShapeDtypeStruct((bsz, seq, D_MODEL), F32),
                   jax.ShapeDtypeStruct((n_areas * AREA_ROWS, HALF), U32),
                   jax.ShapeDtypeStruct((n_prompt, LANES), F32),
                   jax.ShapeDtypeStruct((n_areas, SUBLANES, LANES), F32),
                   jax.ShapeDtypeStruct((bsz, WINDOW, KV_DIM), F32),
                   jax.ShapeDtypeStruct((bsz, WINDOW, KV_DIM), F32),
                   jax.ShapeDtypeStruct((bsz, CONV_WIDTH - 1, CONV_CH), F32)],
        scratch_shapes=[pltpu.VMEM((WINDOW, KV_DIM), F32), pltpu.VMEM((WINDOW, KV_DIM), F32),
                        pltpu.VMEM((TILE_Q + 32, CONV_CH), F32),
                        pltpu.VMEM((SUBLANES - 1, TILE_Q + 24, CONV_CH), F32)],
        compiler_params=cparams(("arbitrary",)),
        name="mix_prompt",
    )(sinks[0], x_prompt, cos_p, sin_p, xs_s, *shared)

    cnt = jnp.concatenate([cnt_p[:p_areas, 0], cnt_s[:, 0], cnt_p[p_areas + 1:, 0]], axis=0)
    cnt = cnt[:, ELANE0:ELANE0 + N_EXPERTS].astype(jnp.int32)
    src_tab, dst_tab, bstart, used8 = _plan_blocks(cnt, q_max)
    w_idx = lambda e, *_: (e, 0, 0)
    ys = pl.pallas_call(
        _moe_ffn_kernel,
        grid_spec=pltpu.PrefetchScalarGridSpec(
            num_scalar_prefetch=4,
            grid=(N_EXPERTS,),
            in_specs=[pl.BlockSpec(memory_space=pl.ANY),
                      pl.BlockSpec((1, D_MODEL, D_FF_EXPERT), w_idx),
                      pl.BlockSpec((1, D_MODEL, D_FF_EXPERT), w_idx),
                      pl.BlockSpec((1, D_FF_EXPERT, D_MODEL), w_idx)],
            out_specs=pl.BlockSpec(memory_space=pl.ANY),
            scratch_shapes=[pltpu.VMEM((GATHER_BUFS, BLOCK_ROWS, HALF), U32),
                            pltpu.VMEM((SCATTER_BUFS, BLOCK_ROWS, HALF), U32),
                            pltpu.VMEM((SCATTER_BUFS, BLOCK_ROWS, D_FF_EXPERT), BF16),
                            pltpu.VMEM((ZERO_ROWS, HALF), U32),
                            pltpu.VMEM((D_MODEL, D_FF_EXPERT), BF16), pltpu.VMEM((D_MODEL, D_FF_EXPERT), BF16),
                            pltpu.VMEM((D_FF_EXPERT, D_MODEL), BF16),
                            pltpu.SemaphoreType.DMA((GATHER_BUFS,)), pltpu.SemaphoreType.DMA((SCATTER_BUFS,)),
                            pltpu.SemaphoreType.DMA(()), pltpu.SemaphoreType.DMA(())]),
        out_shape=jax.ShapeDtypeStruct((n_areas * AREA_ROWS + DUMP_CHUNKS * SUBLANES, HALF), U32),
        compiler_params=cparams(("arbitrary",)),
        name="moe_ffn",
    )(src_tab, dst_tab, bstart, used8, xs, w1[0], w3[0], w2[0])

    def combine(h2d, route, first_area, scope, tile):
        n = h2d.shape[0]
        apt = tile // scope
        return pl.pallas_call(
            _moe_combine_kernel,
            grid=(n // tile,),
            in_specs=[pl.BlockSpec((tile, D_MODEL), lambda t: (t, 0)),
                      pl.BlockSpec((tile, LANES), lambda t: (t, 0)),
                      pl.BlockSpec((apt * AREA_ROWS, HALF), lambda t: (first_area // apt + t, 0)),
                      _full((1, D_MODEL)), _full((1, D_MODEL))],
            out_specs=pl.BlockSpec((tile, D_MODEL), lambda t: (t, 0)),
            out_shape=jax.ShapeDtypeStruct((n, D_MODEL), F32),
            compiler_params=cparams(("arbitrary",)),
            name="moe_combine",
        )(h2d, route, ys, row(ln2_g[0]), row(ln2_b[0]))

    y_p = combine(h_p.reshape(n_prompt, D_MODEL), route_p, 0, PROMPT_SCOPE, TILE_Q)
    y_s = combine(h_s, route_s, p_areas, dec_b, dec_b)

    kv_shape = lambda n: (DEPTH, n, WINDOW, N_KV_HEADS, HEAD_DIM)
    return (y_p.reshape(bsz, seq, D_MODEL), y_s.reshape(dec_b, dec_t, D_MODEL),
            nk_p.reshape(kv_shape(bsz)), nv_p.reshape(kv_shape(bsz)), nc_p[None],
            nk_s.reshape(kv_shape(dec_b)), nv_s.reshape(kv_shape(dec_b)), nc_s[None])
```

```python
import functools

import jax
import jax.numpy as jnp
from jax import lax
from jax.experimental import pallas as pl
from jax.experimental.pallas import tpu as pltpu

F32 = jnp.float32
BF16 = jnp.bfloat16
U32 = jnp.uint32

D_MODEL = 1024
N_HEADS = 8
N_KV_HEADS = 2
HEAD_DIM = 64
Q_GROUP = N_HEADS // N_KV_HEADS
ATTN_DIM = N_HEADS * HEAD_DIM
KV_DIM = N_KV_HEADS * HEAD_DIM
WINDOW = 128
ROPE_THETA = 10000.0
ATTN_SCALE = HEAD_DIM ** -0.5
CONV_CH = D_MODEL - ATTN_DIM
CONV_WIDTH = 31
MIX_DIM = ATTN_DIM + CONV_CH
IN_DIM = ATTN_DIM + 2 * KV_DIM + 2 * CONV_CH
N_GROUPS = 4
EXPERTS_PER_GROUP = 8
N_EXPERTS = N_GROUPS * EXPERTS_PER_GROUP
D_FF_EXPERT = 256
LN_EPS = 1e-5
NEG_INF = -1e30
DEPTH = 1
ALPHA = (2.0 * DEPTH) ** 0.25

LANES = 128
SUBLANES = 8
HALF = D_MODEL // 2
ELANE0 = N_GROUPS
PROMPT_SCOPE = 256
AREA_ROWS = 768
AREA_CHUNKS = AREA_ROWS // SUBLANES
BLOCK_CHUNKS = 32
BLOCK_ROWS = BLOCK_CHUNKS * SUBLANES
TILE_Q = 512
SAMPLE_TB = 32
SAMPLE_GRP = 8
VMEM_LIMIT = 56 * 1024 * 1024


def _layer_norm(x, g, b):
    mu = jnp.mean(x, axis=-1, keepdims=True)
    xc = x - mu
    var = jnp.mean(xc * xc, axis=-1, keepdims=True)
    return xc * lax.rsqrt(var + LN_EPS) * g + b


def _rope(x, x_swapped, cos, sin_signed):
    pieces = [x[:, g * LANES:(g + 1) * LANES] * cos + x_swapped[:, g * LANES:(g + 1) * LANES] * sin_signed
              for g in range(x.shape[-1] // LANES)]
    return pieces[0] if len(pieces) == 1 else jnp.concatenate(pieces, axis=-1)


def _in_proj_glu(xb, w_in_ref, b_in_ref):
    lo, hi = ATTN_DIM + 2 * KV_DIM, IN_DIM
    cacg = jnp.dot(xb, w_in_ref[:, lo:hi], preferred_element_type=F32) + b_in_ref[:, lo:hi]
    return cacg[:, :CONV_CH] * jax.nn.sigmoid(cacg[:, CONV_CH:])


def _in_proj_qkv(xb, w_in_ref, b_in_ref, cos, sin_signed):
    o_v, o_e = ATTN_DIM + KV_DIM, ATTN_DIM + 2 * KV_DIM
    qkv = jnp.dot(xb, w_in_ref[:, 0:o_e], preferred_element_type=F32) + b_in_ref[:, 0:o_e]
    sw = jnp.dot(xb, w_in_ref[:, IN_DIM:], preferred_element_type=F32) + b_in_ref[:, IN_DIM:]
    q = _rope(qkv[:, :ATTN_DIM], sw[:, :ATTN_DIM], cos, sin_signed)
    k = _rope(qkv[:, ATTN_DIM:o_v], sw[:, ATTN_DIM:], cos, sin_signed)
    return q, k, qkv[:, o_v:]


def _pack_rows(v):
    lo = lax.bitcast_convert_type(v[:, :HALF], U32)
    hi = lax.bitcast_convert_type(v[:, HALF:], U32)
    return (lo >> 16) | (hi & jnp.uint32(0xFFFF0000))


def _unpack_rows(w):
    lo = lax.bitcast_convert_type(w << 16, F32).astype(BF16)
    hi = lax.bitcast_convert_type(w & jnp.uint32(0xFFFF0000), F32).astype(BF16)
    return lo, hi


def _drain(stages):
    try:
        while True:
            next(stages)
    except StopIteration as stop:
        return stop.value


def _interleave(main, side, main_per_side):
    live = {"main": True, "side": True}

    def advance(name, stages):
        if live[name]:
            try:
                next(stages)
            except StopIteration:
                live[name] = False

    while live["main"] or live["side"]:
        for _ in range(main_per_side):
            advance("main", main)
        advance("side", side)


def _route_and_sort(h, w_r, scope):
    return _drain(_route_and_sort_stages(h, w_r, scope))


def _route_and_sort_stages(h, w_r, scope):
    n = h.shape[0]
    hb = h.astype(BF16)
    logits = jnp.dot(hb, w_r, preferred_element_type=F32)
    lane = lax.broadcasted_iota(jnp.int32, (n, LANES), 1)
    lane_f = lane.astype(F32)
    is_g = lane < N_GROUPS
    lg = jnp.where(is_g, logits, -jnp.inf)
    gmax = jnp.max(lg, axis=-1, keepdims=True)
    gidx = jnp.min(jnp.where(lg == gmax, lane_f, float(LANES)), axis=-1, keepdims=True)
    p_g = 1.0 / jnp.sum(jnp.where(is_g, jnp.exp(logits - gmax), 0.0), axis=-1, keepdims=True)
    egrp = ((lane - ELANE0) >> 3).astype(F32)
    emask = (lane >= ELANE0) & (lane < ELANE0 + N_EXPERTS) & (egrp == gidx)
    sel = jnp.where(emask, logits, -jnp.inf)
    v1 = jnp.max(sel, axis=-1, keepdims=True)
    i1 = jnp.min(jnp.where(sel == v1, lane_f, float(LANES)), axis=-1, keepdims=True)
    sel2 = jnp.where(lane_f == i1, -jnp.inf, sel)
    v2 = jnp.max(sel2, axis=-1, keepdims=True)
    i2 = jnp.min(jnp.where(sel2 == v2, lane_f, float(LANES)), axis=-1, keepdims=True)
    t = jnp.exp(v2 - v1)
    gate1 = p_g / (1.0 + t)
    gate2 = p_g * t / (1.0 + t)
    pick1 = lane_f == i1
    pick2 = lane_f == i2
    member = jnp.where(pick1 | pick2, 1.0, 0.0)
    yield

    r_i = lax.broadcasted_iota(jnp.int32, (scope, scope), 0)
    c_i = lax.broadcasted_iota(jnp.int32, (scope, scope), 1)
    lower = jnp.where(c_i < r_i, 1.0, 0.0).astype(BF16)
    e_r = lax.broadcasted_iota(jnp.int32, (LANES, LANES), 0)
    e_c = lax.broadcasted_iota(jnp.int32, (LANES, LANES), 1)
    upper = jnp.where(e_r < e_c, 1.0, 0.0).astype(BF16)
    s_iota = lax.broadcasted_iota(jnp.int32, (scope, AREA_ROWS), 1).astype(F32)

    slots1, slots2, areas, counts = [], [], [], []
    for sc in range(n // scope):
        rows = slice(sc * scope, (sc + 1) * scope)
        m_sc = member[rows]
        before = jnp.dot(lower, m_sc.astype(BF16), preferred_element_type=F32)
        cnt = jnp.sum(m_sc, axis=0, keepdims=True)
        c8 = jnp.floor((cnt + (SUBLANES - 1)) * (1.0 / SUBLANES))
        c8b = jnp.broadcast_to(c8, (SUBLANES, LANES)).astype(BF16)
        off8 = jnp.dot(c8b, upper, preferred_element_type=F32)[0:1]
        slot_all = before + off8 * float(SUBLANES)
        s1 = jnp.sum(jnp.where(pick1[rows], slot_all, 0.0), axis=-1, keepdims=True)
        s2 = jnp.sum(jnp.where(pick2[rows], slot_all, 0.0), axis=-1, keepdims=True)
        perm_t = jnp.where((s_iota == s1) | (s_iota == s2), 1.0, 0.0).astype(BF16)
        yield
        sorted_rows = lax.dot_general(perm_t, hb[rows], (((0,), (0,)), ((), ())),
                                      preferred_element_type=F32)
        areas.append(_pack_rows(sorted_rows))
        counts.append(cnt)
        slots1.append(s1)
        slots2.append(s2)
    slot1 = jnp.concatenate(slots1, axis=0) if len(slots1) > 1 else slots1[0]
    slot2 = jnp.concatenate(slots2, axis=0) if len(slots2) > 1 else slots2[0]
    route = jnp.where(lane == 0, slot1, jnp.where(lane == 1, slot2, jnp.where(lane == 2, gate1,
                      jnp.where(lane == 3, gate2, 0.0))))
    return route, areas, counts


def _softmax_rows_with_sink(s, sink, extra=None):
    m = jnp.maximum(jnp.max(s, axis=-1, keepdims=True), sink)
    if extra is not None:
        m = jnp.maximum(m, extra)
    e = jnp.exp(s - m)
    den = jnp.sum(e, axis=-1, keepdims=True) + jnp.exp(sink - m)
    if extra is not None:
        ee = jnp.exp(extra - m)
        den = den + ee
        return e / den, ee / den
    return e / den, None


def _mix_tail(x, attn, conv_pre, lcg, lcb, w_out_ref, b_out, l1g, l1b):
    conv = _layer_norm(conv_pre, lcg, lcb)
    conv = conv * jax.nn.sigmoid(conv)
    mix = (jnp.dot(attn.astype(BF16), w_out_ref[0:ATTN_DIM, :], preferred_element_type=F32)
           + jnp.dot(conv.astype(BF16), w_out_ref[ATTN_DIM:, :], preferred_element_type=F32) + b_out)
    return _layer_norm(ALPHA * x + mix, l1g, l1b)


def _mix_prompt_kernel(sinks_ref, x_ref, xp_ref, cos_ref, sin_ref, xs_s_ref,
                       w_in_ref, b_in_ref, conv_w_ref, conv_b_ref, lcg_ref, lcb_ref,
                       w_out_ref, b_out_ref, l1g_ref, l1b_ref, w_r_ref,
                       h_ref, xs_ref, route_ref, cnt_ref, nk_ref, nv_ref, nc_ref,
                       kprev, vprev, u_scr, u_shift, mix_scr, *, tiles):
    t = pl.program_id(0)
    n_real = pl.num_programs(0) - 2

    def front():
        return _mix_front(sinks_ref, x_ref, cos_ref, sin_ref, w_in_ref, b_in_ref, conv_w_ref, conv_b_ref,
                          lcg_ref, lcb_ref, nk_ref, nv_ref, nc_ref, kprev, vprev, u_scr, u_shift,
                          mix_scr.at[t % 2], i=t % tiles)

    def back():
        return _mix_back(xp_ref, mix_scr.at[(t + 1) % 2], w_out_ref, b_out_ref, l1g_ref, l1b_ref, w_r_ref,
                         h_ref, xs_ref, route_ref, cnt_ref)

    @pl.when(t == 0)
    def _():
        kprev[...] = jnp.zeros_like(kprev)
        vprev[...] = jnp.zeros_like(vprev)
        u_scr[...] = jnp.zeros_like(u_scr)
        _drain(front())

    @pl.when((t > 0) & (t < n_real))
    def _():
        _interleave(front(), back(), FRONT_STAGES_PER_BACK_STAGE)

    @pl.when(t == n_real)
    def _():
        _drain(back())

    @pl.when(t == n_real + 1)
    def _():
        xs_ref[0:AREA_ROWS, :] = xs_s_ref[...]
        xs_ref[AREA_ROWS:, :] = jnp.zeros((xs_ref.shape[0] - AREA_ROWS, HALF), U32)
        cnt_ref[...] = jnp.zeros(cnt_ref.shape, F32)


def _mix_back(xp_ref, mix_ref, w_out_ref, b_out_ref, l1g_ref, l1b_ref, w_r_ref, h_ref, xs_ref, route_ref, cnt_ref):
    for sc in range(mix_ref.shape[0] // PROMPT_SCOPE):
        rows = slice(sc * PROMPT_SCOPE, (sc + 1) * PROMPT_SCOPE)
        mix = jnp.dot(mix_ref[rows, :], w_out_ref[...], preferred_element_type=F32) + b_out_ref[...]
        h = _layer_norm(ALPHA * xp_ref[0, rows, :] + mix, l1g_ref[...], l1b_ref[...])
        h_ref[0, rows, :] = h
        yield
        route, areas, counts = yield from _route_and_sort_stages(h, w_r_ref[...], PROMPT_SCOPE)
        route_ref[rows, :] = route
        xs_ref[sc * AREA_ROWS:(sc + 1) * AREA_ROWS, :] = areas[0]
        cnt_ref[sc] = jnp.broadcast_to(counts[0], (SUBLANES, LANES))
        yield


FRONT_STAGES_PER_BACK_STAGE = 3


def _mix_front(sinks_ref, x_ref, cos_ref, sin_ref, w_in_ref, b_in_ref, conv_w_ref, conv_b_ref, lcg_ref, lcb_ref,
               nk_ref, nv_ref, nc_ref, kprev, vprev, u_scr, u_shift, mix_ref, *, i):
    tq = x_ref.shape[1]
    pre = 32
    carried = i > 0
    u_scr[0:pre, :] = jnp.where(carried, u_scr[0:pre, :], 0.0)
    xb = x_ref[0].astype(BF16)

    u_scr[pre:pre + tq, :] = _in_proj_glu(xb, w_in_ref, b_in_ref)
    yield
    shift_rows = u_shift.shape[1]
    for r in range(1, SUBLANES):
        u_shift[r - 1] = u_scr[r:r + shift_rows, :]
    yield
    acc = jnp.zeros((tq, CONV_CH), F32)
    for j in range(CONV_WIDTH):
        a, r = divmod(pre - (CONV_WIDTH - 1) + j, SUBLANES)
        src = u_scr[a * SUBLANES:a * SUBLANES + tq, :] if r == 0 else u_shift[r - 1, a * SUBLANES:a * SUBLANES + tq, :]
        acc = acc + src * conv_w_ref[j:j + 1, :]
        if j % 4 == 3:
            yield
    conv = _layer_norm(acc + conv_b_ref[...], lcg_ref[...], lcb_ref[...])
    mix_ref[:, ATTN_DIM:] = (conv * jax.nn.sigmoid(conv)).astype(BF16)

    nc_ref[0] = u_scr[pre + tq - (CONV_WIDTH - 1):pre + tq, :]
    tail = u_scr[tq:tq + pre, :]
    u_scr[0:pre, :] = tail
    yield

    q, k, v = _in_proj_qkv(xb, w_in_ref, b_in_ref, cos_ref[...], sin_ref[...])
    yield
    qb = q.astype(BF16)
    kb = jnp.concatenate([jnp.where(carried, kprev[...], 0.0), k], axis=0).astype(BF16)
    vb = jnp.concatenate([jnp.where(carried, vprev[...], 0.0), v], axis=0).astype(BF16)
    row = lax.broadcasted_iota(jnp.int32, (WINDOW, 2 * WINDOW), 0)
    col = lax.broadcasted_iota(jnp.int32, (WINDOW, 2 * WINDOW), 1)
    dist = row + WINDOW - col
    band = (dist >= 0) & (dist < WINDOW)
    first_col = jnp.where(i > 0, 0, WINDOW)
    for j in range(tq // WINDOW):
        msk = (band & (col >= first_col)) if j == 0 else band
        heads = [None] * N_HEADS
        for c in range(N_KV_HEADS):
            kc = kb[j * WINDOW:(j + 2) * WINDOW, c * HEAD_DIM:(c + 1) * HEAD_DIM]
            vc = vb[j * WINDOW:(j + 2) * WINDOW, c * HEAD_DIM:(c + 1) * HEAD_DIM]
            qc = jnp.concatenate(
                [qb[j * WINDOW:(j + 1) * WINDOW, (c * Q_GROUP + g) * HEAD_DIM:(c * Q_GROUP + g + 1) * HEAD_DIM]
                 for g in range(Q_GROUP)], axis=0)
            s = lax.dot_general(qc, kc, (((1,), (1,)), ((), ())), preferred_element_type=F32)
            probs = []
            for g in range(Q_GROUP):
                sg = jnp.where(msk, s[g * WINDOW:(g + 1) * WINDOW], NEG_INF)
                p, _ = _softmax_rows_with_sink(sg, sinks_ref[c * Q_GROUP + g])
                probs.append(p.astype(BF16))
            o = jnp.dot(jnp.concatenate(probs, axis=0), vc, preferred_element_type=F32)
            for g in range(Q_GROUP):
                heads[c * Q_GROUP + g] = o[g * WINDOW:(g + 1) * WINDOW]
            yield
        mix_ref[j * WINDOW:(j + 1) * WINDOW, 0:ATTN_DIM] = jnp.concatenate(heads, axis=1).astype(BF16)
    kprev[...] = k[tq - WINDOW:, :]
    vprev[...] = v[tq - WINDOW:, :]
    nk_ref[0] = k[tq - WINDOW:, :]
    nv_ref[0] = v[tq - WINDOW:, :]
    yield


def _mix_sample_kernel(sinks_ref, x_ref, cos_ref, sin_ref, ck_ref, cv_ref, st_ref, w_in_ref, b_in_ref,
                       conv_w_ref, conv_b_ref, lcg_ref, lcb_ref, w_out_ref, b_out_ref, l1g_ref, l1b_ref,
                       w_r_ref,
                       h_ref, nk_ref, nv_ref, nc_ref, xs_ref, route_ref, cnt_ref,
                       h_scr):
    tb = x_ref.shape[0]
    n_tok = h_scr.shape[0]
    step = pl.program_id(0)
    x = x_ref[...]
    xb = x.astype(BF16)
    u = _in_proj_glu(xb, w_in_ref, b_in_ref)
    q, k_new, v_new = _in_proj_qkv(xb, w_in_ref, b_in_ref, cos_ref[...], sin_ref[...])

    qb = q.astype(BF16)
    knb = k_new.astype(BF16).astype(F32)
    vnb = v_new.astype(BF16).astype(F32)
    gq = SAMPLE_GRP * Q_GROUP
    nkeys = SAMPLE_GRP * WINDOW
    r_i = lax.broadcasted_iota(jnp.int32, (gq, nkeys), 0)
    c_i = lax.broadcasted_iota(jnp.int32, (gq, nkeys), 1)
    msk = ((c_i >> 7) == (r_i & (SAMPLE_GRP - 1))) & ((c_i & (WINDOW - 1)) >= 1)
    r_col = lax.broadcasted_iota(jnp.int32, (gq, 1), 0) >> 3
    attn_groups = []
    for grp in range(tb // SAMPLE_GRP):
        r0 = grp * SAMPLE_GRP
        kc_all = ck_ref[r0:r0 + SAMPLE_GRP].reshape(nkeys, KV_DIM).astype(BF16)
        vc_all = cv_ref[r0:r0 + SAMPLE_GRP].reshape(nkeys, KV_DIM).astype(BF16)
        heads = [None] * N_HEADS
        for c in range(N_KV_HEADS):
            lanes = slice(c * HEAD_DIM, (c + 1) * HEAD_DIM)
            qc = jnp.concatenate(
                [qb[r0:r0 + SAMPLE_GRP, (c * Q_GROUP + g) * HEAD_DIM:(c * Q_GROUP + g + 1) * HEAD_DIM]
                 for g in range(Q_GROUP)], axis=0)
            s = lax.dot_general(qc, kc_all[:, lanes], (((1,), (1,)), ((), ())), preferred_element_type=F32)
            s = jnp.where(msk, s, NEG_INF)
            kn = jnp.concatenate([knb[r0:r0 + SAMPLE_GRP, lanes]] * Q_GROUP, axis=0)
            vn = jnp.concatenate([vnb[r0:r0 + SAMPLE_GRP, lanes]] * Q_GROUP, axis=0)
            s_new = jnp.sum(qc.astype(F32) * kn, axis=-1, keepdims=True)
            sink = jnp.zeros((gq, 1), F32)
            for g in range(Q_GROUP):
                sink = jnp.where(r_col == g, sinks_ref[c * Q_GROUP + g], sink)
            p, p_new = _softmax_rows_with_sink(s, sink, s_new)
            o = (jnp.dot(p.astype(BF16), vc_all[:, lanes], preferred_element_type=F32)
                 + p_new.astype(BF16).astype(F32) * vn)
            for g in range(Q_GROUP):
                heads[c * Q_GROUP + g] = o[g * SAMPLE_GRP:(g + 1) * SAMPLE_GRP]
        attn_groups.append(jnp.concatenate(heads, axis=1))
    attn = jnp.concatenate(attn_groups, axis=0)

    st = st_ref[...]
    acc = jnp.sum(st * conv_w_ref[0:CONV_WIDTH - 1, :][None], axis=1)
    conv_pre = acc + u * conv_w_ref[CONV_WIDTH - 1:CONV_WIDTH, :] + conv_b_ref[...]

    nk_ref[:, 0:WINDOW - 1, :] = ck_ref[:, 1:WINDOW, :]
    nv_ref[:, 0:WINDOW - 1, :] = cv_ref[:, 1:WINDOW, :]
    nc_ref[:, 0:CONV_WIDTH - 2, :] = st_ref[:, 1:CONV_WIDTH - 1, :]
    for b in range(tb):
        nk_ref[b, WINDOW - 1:WINDOW, :] = k_new[b:b + 1, :]
        nv_ref[b, WINDOW - 1:WINDOW, :] = v_new[b:b + 1, :]
        nc_ref[b, CONV_WIDTH - 2:CONV_WIDTH - 1, :] = u[b:b + 1, :]

    h = _mix_tail(x, attn, conv_pre, lcg_ref[...], lcb_ref[...], w_out_ref, b_out_ref[...],
                  l1g_ref[...], l1b_ref[...])
    h_ref[...] = h
    h_scr[pl.ds(pl.multiple_of(step * tb, tb), tb), :] = h

    @pl.when(step == pl.num_programs(0) - 1)
    def _():
        route, areas, counts = _route_and_sort(h_scr[...], w_r_ref[...], n_tok)
        route_ref[...] = route
        xs_ref[...] = areas[0]
        cnt_ref[0] = jnp.broadcast_to(counts[0], (SUBLANES, LANES))


GATHER_BUFS = 8
SCATTER_BUFS = 8
SCATTER_LAG = 2
ZERO_BITS = (64, 32, 16, 8, 4, 2, 1)
DUMP_CHUNKS = SCATTER_BUFS * BLOCK_CHUNKS
ZERO_ROWS = max(ZERO_BITS[0], DUMP_CHUNKS) * SUBLANES


def _moe_ffn_kernel(src_ref, dst_ref, bstart_ref, used_ref,
                    xs_hbm, w1_ref, w3_ref, w2_ref,
                    ys_hbm,
                    xbuf, ybuf, hbuf, zbuf, w1b, w3b, w2b, sem_in, sem_out, sem_zero, sem_dump):
    e = pl.program_id(0)
    n_areas = used_ref.shape[0]
    total = bstart_ref[N_EXPERTS]
    dump_row0 = n_areas * AREA_ROWS
    lookahead = GATHER_BUFS - 1
    any_rows = pl.ds(0, SUBLANES)

    def chunk_at(ref, idx):
        return pl.ds(pl.multiple_of(ref[idx] * SUBLANES, SUBLANES), SUBLANES)

    def in_copy(rows, sl, c):
        return pltpu.make_async_copy(xs_hbm.at[rows], xbuf.at[sl, pl.ds(c * SUBLANES, SUBLANES)], sem_in.at[sl])

    def out_copy(rows, sl, c):
        return pltpu.make_async_copy(ybuf.at[sl, pl.ds(c * SUBLANES, SUBLANES)], ys_hbm.at[rows], sem_out.at[sl])

    def start_gather(blk):
        for c in range(BLOCK_CHUNKS):
            in_copy(chunk_at(src_ref, blk * BLOCK_CHUNKS + c), blk % GATHER_BUFS, c).start()

    def wait_gather(blk):
        for c in range(BLOCK_CHUNKS):
            in_copy(any_rows, blk % GATHER_BUFS, c).wait()

    def start_scatter(blk):
        for c in range(BLOCK_CHUNKS):
            out_copy(chunk_at(dst_ref, (blk + SCATTER_LAG) * BLOCK_CHUNKS + c), blk % SCATTER_BUFS, c).start()

    def wait_scatter(sl):
        for c in range(BLOCK_CHUNKS):
            out_copy(any_rows, sl, c).wait()

    def load_rows(blk):
        return _unpack_rows(xbuf[blk % GATHER_BUFS])

    def issue_dmas(blk):
        start_scatter(blk - SCATTER_LAG)
        start_gather(blk + lookahead)

    def up_gate(blk, rows):
        x_lo, x_hi = rows
        a1 = (jnp.dot(x_lo, w1b[0:HALF, :], preferred_element_type=F32)
              + jnp.dot(x_hi, w1b[HALF:, :], preferred_element_type=F32))
        a3 = (jnp.dot(x_lo, w3b[0:HALF, :], preferred_element_type=F32)
              + jnp.dot(x_hi, w3b[HALF:, :], preferred_element_type=F32))
        hbuf[blk % SCATTER_BUFS] = (a1 * jax.nn.sigmoid(a1) * a3).astype(BF16)

    def down(blk):
        y = jnp.dot(hbuf[blk % SCATTER_BUFS], w2b[...], preferred_element_type=F32)
        ybuf[blk % SCATTER_BUFS] = _pack_rows(y.astype(BF16).astype(F32))

    def for_each_tail_piece(fn):
        def area_body(s, carry):
            used = used_ref[s]
            tail = AREA_CHUNKS - used
            row = (s * AREA_CHUNKS + used) * SUBLANES
            for bit in ZERO_BITS:
                take = (tail & bit) != 0

                @pl.when(take)
                def _(row=row, bit=bit):
                    fn(pltpu.make_async_copy(zbuf.at[pl.ds(0, bit * SUBLANES)],
                                             ys_hbm.at[pl.ds(pl.multiple_of(row, SUBLANES), bit * SUBLANES)],
                                             sem_zero))
                row = row + jnp.where(take, bit * SUBLANES, 0)
            return carry
        lax.fori_loop(0, n_areas, area_body, 0)

    @pl.when(e == 0)
    def _():
        zbuf[...] = jnp.zeros(zbuf.shape, U32)
        ybuf[...] = jnp.zeros(ybuf.shape, U32)
        dump = pltpu.make_async_copy(zbuf.at[pl.ds(0, DUMP_CHUNKS * SUBLANES)],
                                     ys_hbm.at[pl.ds(dump_row0, DUMP_CHUNKS * SUBLANES)], sem_dump)
        dump.start()
        dump.wait()
        for_each_tail_piece(lambda cp: cp.start())
        for sl in range(SCATTER_BUFS - SCATTER_LAG):
            for c in range(BLOCK_CHUNKS):
                out_copy(pl.ds(dump_row0 + (sl * BLOCK_CHUNKS + c) * SUBLANES, SUBLANES), sl, c).start()
        for k in range(lookahead):
            start_gather(k)

    w1b[...] = w1_ref[0].astype(BF16)
    w3b[...] = w3_ref[0].astype(BF16)
    w2b[...] = w2_ref[0].astype(BF16)
    b0 = bstart_ref[e]
    b1 = bstart_ref[e + 1]

    @pl.when(b1 > b0)
    def _():
        wait_gather(b0)
        rows = load_rows(b0)
        issue_dmas(b0)
        up_gate(b0, rows)

    def block_body(blk, carry):
        wait_gather(blk)
        wait_scatter((blk - 1) % SCATTER_BUFS)
        rows = load_rows(blk)
        issue_dmas(blk)
        down(blk - 1)
        up_gate(blk, rows)
        return carry

    lax.fori_loop(b0 + 1, b1, block_body, 0)

    @pl.when(b1 > b0)
    def _():
        wait_scatter((b1 - 1) % SCATTER_BUFS)
        down(b1 - 1)

    @pl.when(e == N_EXPERTS - 1)
    def _():
        for k in range(SCATTER_LAG):
            start_scatter(total - SCATTER_LAG + k)
        for sl in range(SCATTER_BUFS):
            wait_scatter(sl)
        for k in range(lookahead):
            wait_gather(total + k)
        for_each_tail_piece(lambda cp: cp.wait())


def _moe_combine_kernel(h_ref, route_ref, ys_ref, l2g_ref, l2b_ref, y_ref):
    n = h_ref.shape[0]
    scope = n // (ys_ref.shape[0] // AREA_ROWS)
    s_iota = lax.broadcasted_iota(jnp.int32, (scope, AREA_ROWS), 1).astype(F32)
    for sc in range(n // scope):
        rows = slice(sc * scope, (sc + 1) * scope)
        route = route_ref[rows, :]
        slot1 = route[:, 0:1]
        slot2 = route[:, 1:2]
        gate1 = route[:, 2:3]
        gate2 = route[:, 3:4]
        gmat = (jnp.where(s_iota == slot1, gate1, 0.0) + jnp.where(s_iota == slot2, gate2, 0.0)).astype(BF16)
        y_lo, y_hi = _unpack_rows(ys_ref[sc * AREA_ROWS:(sc + 1) * AREA_ROWS, :])
        f = jnp.concatenate([jnp.dot(gmat, y_lo, preferred_element_type=F32),
                             jnp.dot(gmat, y_hi, preferred_element_type=F32)], axis=1)
        y_ref[rows, :] = _layer_norm(ALPHA * h_ref[rows, :] + f, l2g_ref[...], l2b_ref[...])


def _excl_cumsum(a, axis):
    return jnp.cumsum(a, axis=axis) - a


def _plan_blocks(cnt, q_max):
    n_areas = cnt.shape[0]
    c8 = (cnt + (SUBLANES - 1)) // SUBLANES
    used8 = jnp.sum(c8, axis=1)
    base8 = jnp.arange(n_areas, dtype=jnp.int32) * AREA_CHUNKS
    src_runs = (base8[:, None] + _excl_cumsum(c8, 1)).T
    len_runs = c8.T
    tot = jnp.sum(len_runs, axis=1)
    ptot = (tot + BLOCK_CHUNKS - 1) // BLOCK_CHUNKS * BLOCK_CHUNKS
    pstart = _excl_cumsum(ptot, 0)
    dst_runs = pstart[:, None] + _excl_cumsum(len_runs, 1)
    off = (src_runs - dst_runs).reshape(-1)
    diff = off - jnp.concatenate([jnp.zeros((1,), jnp.int32), off[:-1]])
    q = jnp.arange(q_max, dtype=jnp.int32)
    src = q + jnp.sum(jnp.where(dst_runs.reshape(1, -1) <= q[:, None], diff[None, :], 0), axis=1)
    valid = jnp.any((pstart[None, :] <= q[:, None]) & (q[:, None] < (pstart + tot)[None, :]), axis=1)
    zero_chunk = AREA_CHUNKS - 1
    ahead = jnp.full(((GATHER_BUFS - 1) * BLOCK_CHUNKS,), zero_chunk, jnp.int32)
    src_tab = jnp.concatenate([jnp.where(valid, src, zero_chunk).astype(jnp.int32), ahead])
    qv = jnp.arange(-SCATTER_LAG * BLOCK_CHUNKS, q_max, dtype=jnp.int32)
    dump = n_areas * AREA_CHUNKS + ((qv // BLOCK_CHUNKS) % SCATTER_BUFS) * BLOCK_CHUNKS + qv % BLOCK_CHUNKS
    lead = jnp.zeros((SCATTER_LAG * BLOCK_CHUNKS,), jnp.int32)
    dst_tab = jnp.where(jnp.concatenate([lead > 0, valid]), jnp.concatenate([lead, src]), dump).astype(jnp.int32)
    bstart = jnp.concatenate([pstart, jnp.sum(ptot, keepdims=True)]) // BLOCK_CHUNKS
    return src_tab, dst_tab, bstart.astype(jnp.int32), used8.astype(jnp.int32)


def _rope_tables(pos):
    half = HEAD_DIM // 2
    inv = ROPE_THETA ** (-jnp.arange(half, dtype=F32) / half)
    ang = pos.astype(F32)[:, None] * inv
    cos = jnp.cos(ang)
    sin = jnp.sin(ang)
    reps = LANES // HEAD_DIM
    cos_t = jnp.concatenate([cos, cos] * reps, axis=1)
    sin_t = jnp.concatenate([-sin, sin] * reps, axis=1)
    return cos_t, sin_t


def _full(shape):
    nd = len(shape)
    return pl.BlockSpec(shape, lambda *_: (0,) * nd)


def kernel(x_prompt, x_sample, cache_k, cache_v, state_conv, w_in, b_in, sinks, conv_w, conv_b, ln_conv_g,
           ln_conv_b, w_out, b_out, ln1_g, ln1_b, w_router_group, w_router_expert, w1, w3, w2, ln2_g, ln2_b):
    assert w_in.shape[0] == DEPTH
    bsz, seq, _ = x_prompt.shape
    dec_b, dec_t, _ = x_sample.shape
    assert dec_t == 1 and seq % TILE_Q == 0 and dec_b % SAMPLE_TB == 0 and 2 * dec_b + 7 * N_EXPERTS <= AREA_ROWS
    n_prompt = bsz * seq
    tiles = seq // TILE_Q
    spt = TILE_Q // PROMPT_SCOPE
    p_areas = n_prompt // PROMPT_SCOPE
    n_areas = p_areas + spt
    q_max = (n_areas * AREA_CHUNKS + N_EXPERTS * (BLOCK_CHUNKS - 1) + BLOCK_CHUNKS - 1) // BLOCK_CHUNKS * BLOCK_CHUNKS

    row = lambda a: a.reshape(1, -1)
    qk = ATTN_DIM + KV_DIM
    col = jnp.arange(qk)
    swap = (col // HEAD_DIM) * HEAD_DIM + (col % HEAD_DIM + HEAD_DIM // 2) % HEAD_DIM
    col_scale = jnp.where(jnp.arange(IN_DIM) < ATTN_DIM, ATTN_SCALE, 1.0).astype(F32)
    w_main = w_in[0] * col_scale
    b_main = b_in[0] * col_scale
    w_in_b = jnp.concatenate([w_main, w_main[:, :qk][:, swap]], axis=1).astype(BF16)
    b_in_x = jnp.concatenate([b_main, b_main[:qk][swap]])
    w_out_b = w_out[0].astype(BF16)
    w_r = jnp.concatenate([w_router_group[0], w_router_expert[0].reshape(D_MODEL, N_EXPERTS),
                           jnp.zeros((D_MODEL, LANES - ELANE0 - N_EXPERTS), F32)], axis=1).astype(BF16)
    shared = (w_in_b, row(b_in_x), conv_w[0], row(conv_b[0]), row(ln_conv_g[0]), row(ln_conv_b[0]),
              w_out_b, row(b_out[0]), row(ln1_g[0]), row(ln1_b[0]), w_r)
    shared_specs = [_full(a.shape) for a in shared]
    smem = pl.BlockSpec(memory_space=pltpu.SMEM)
    cos_p, sin_p = _rope_tables(jnp.arange(seq))
    cos_s, sin_s = _rope_tables(16384 + jnp.arange(dec_t))

    cparams = lambda sem: pltpu.CompilerParams(dimension_semantics=sem, vmem_limit_bytes=VMEM_LIMIT)

    ck = cache_k[0].reshape(dec_b, WINDOW, KV_DIM)
    cv = cache_v[0].reshape(dec_b, WINDOW, KV_DIM)
    st = state_conv[0]
    tb = SAMPLE_TB
    h_s, nk_s, nv_s, nc_s, xs_s, route_s, cnt_s = pl.pallas_call(
        _mix_sample_kernel,
        grid=(dec_b // tb,),
        in_specs=[smem,
                  pl.BlockSpec((tb, D_MODEL), lambda t: (t, 0)),
                  _full(cos_s.shape), _full(sin_s.shape),
                  pl.BlockSpec((tb, WINDOW, KV_DIM), lambda t: (t, 0, 0)),
                  pl.BlockSpec((tb, WINDOW, KV_DIM), lambda t: (t, 0, 0)),
                  pl.BlockSpec((tb, CONV_WIDTH - 1, CONV_CH), lambda t: (t, 0, 0))] + shared_specs,
        out_specs=[pl.BlockSpec((tb, D_MODEL), lambda t: (t, 0)),
                   pl.BlockSpec((tb, WINDOW, KV_DIM), lambda t: (t, 0, 0)),
                   pl.BlockSpec((tb, WINDOW, KV_DIM), lambda t: (t, 0, 0)),
                   pl.BlockSpec((tb, CONV_WIDTH - 1, CONV_CH), lambda t: (t, 0, 0)),
                   _full((AREA_ROWS, HALF)),
                   _full((dec_b, LANES)),
                   _full((1, SUBLANES, LANES))],
        out_shape=[jax.ShapeDtypeStruct((dec_b, D_MODEL), F32),
                   jax.ShapeDtypeStruct((dec_b, WINDOW, KV_DIM), F32),
                   jax.ShapeDtypeStruct((dec_b, WINDOW, KV_DIM), F32),
                   jax.ShapeDtypeStruct((dec_b, CONV_WIDTH - 1, CONV_CH), F32),
                   jax.ShapeDtypeStruct((AREA_ROWS, HALF), U32),
                   jax.ShapeDtypeStruct((dec_b, LANES), F32),
                   jax.ShapeDtypeStruct((1, SUBLANES, LANES), F32)],
        scratch_shapes=[pltpu.VMEM((dec_b, D_MODEL), F32)],
        compiler_params=cparams(("arbitrary",)),
        name="mix_sample",
    )(sinks[0], x_sample.reshape(dec_b, D_MODEL), cos_s, sin_s, ck, cv, st, *shared)

    n_real = bsz * tiles
    real = lambda t: jnp.clip(t, 0, n_real - 1)
    prev = lambda t: jnp.clip(t - 1, 0, n_real - 1)
    h_p, xs, route_p, cnt_p, nk_p, nv_p, nc_p = pl.pallas_call(
        functools.partial(_mix_prompt_kernel, tiles=tiles),
        grid=(n_real + 2,),
        in_specs=[smem,
                  pl.BlockSpec((1, TILE_Q, D_MODEL), lambda t: (real(t) // tiles, real(t) % tiles, 0)),
                  pl.BlockSpec((1, TILE_Q, D_MODEL), lambda t: (prev(t) // tiles, prev(t) % tiles, 0)),
                  pl.BlockSpec((TILE_Q, LANES), lambda t: (real(t) % tiles, 0)),
                  pl.BlockSpec((TILE_Q, LANES), lambda t: (real(t) % tiles, 0)),
                  _full((AREA_ROWS, HALF))] + shared_specs,
        out_specs=[pl.BlockSpec((1, TILE_Q, D_MODEL), lambda t: (prev(t) // tiles, prev(t) % tiles, 0)),
                   pl.BlockSpec((spt * AREA_ROWS, HALF), lambda t: (jnp.maximum(t - 1, 0), 0)),
                   pl.BlockSpec((TILE_Q, LANES), lambda t: (prev(t), 0)),
                   pl.BlockSpec((spt, SUBLANES, LANES), lambda t: (jnp.maximum(t - 1, 0), 0, 0)),
                   pl.BlockSpec((1, WINDOW, KV_DIM), lambda t: (real(t) // tiles, 0, 0)),
                   pl.BlockSpec((1, WINDOW, KV_DIM), lambda t: (real(t) // tiles, 0, 0)),
                   pl.BlockSpec((1, CONV_WIDTH - 1, CONV_CH), lambda t: (real(t) // tiles, 0, 0))],
        out_shape=[jax.ShapeDtypeStruct((bsz, seq, D_MODEL), F32),
                   jax.ShapeDtypeStruct((n_areas * AREA_ROWS, HALF), U32),
                   jax.ShapeDtypeStruct((n_prompt, LANES), F32),
                   jax.ShapeDtypeStruct((n_areas, SUBLANES, LANES), F32),
                   jax.ShapeDtypeStruct((bsz, WINDOW, KV_DIM), F32),
                   jax.ShapeDtypeStruct((bsz, WINDOW, KV_DIM), F32),
                   jax.ShapeDtypeStruct((bsz, CONV_WIDTH - 1, CONV_CH), F32)],
        scratch_shapes=[pltpu.VMEM((WINDOW, KV_DIM), F32), pltpu.VMEM((WINDOW, KV_DIM), F32),
                        pltpu.VMEM((TILE_Q + 32, CONV_CH), F32),
                        pltpu.VMEM((SUBLANES - 1, TILE_Q + 24, CONV_CH), F32),
                        pltpu.VMEM((2, TILE_Q, MIX_DIM), BF16)],
        compiler_params=cparams(("arbitrary",)),
        name="mix_prompt",
    )(sinks[0], x_prompt, x_prompt, cos_p, sin_p, xs_s, *shared)

    cnt = jnp.concatenate([cnt_p[:p_areas, 0], cnt_s[:, 0], cnt_p[p_areas + 1:, 0]], axis=0)
    cnt = cnt[:, ELANE0:ELANE0 + N_EXPERTS].astype(jnp.int32)
    src_tab, dst_tab, bstart, used8 = _plan_blocks(cnt, q_max)
    w_idx = lambda e, *_: (e, 0, 0)
    ys = pl.pallas_call(
        _moe_ffn_kernel,
        grid_spec=pltpu.PrefetchScalarGridSpec(
            num_scalar_prefetch=4,
            grid=(N_EXPERTS,),
            in_specs=[pl.BlockSpec(memory_space=pl.ANY),
                      pl.BlockSpec((1, D_MODEL, D_FF_EXPERT), w_idx),
                      pl.BlockSpec((1, D_MODEL, D_FF_EXPERT), w_idx),
                      pl.BlockSpec((1, D_FF_EXPERT, D_MODEL), w_idx)],
            out_specs=pl.BlockSpec(memory_space=pl.ANY),
            scratch_shapes=[pltpu.VMEM((GATHER_BUFS, BLOCK_ROWS, HALF), U32),
                            pltpu.VMEM((SCATTER_BUFS, BLOCK_ROWS, HALF), U32),
                            pltpu.VMEM((SCATTER_BUFS, BLOCK_ROWS, D_FF_EXPERT), BF16),
                            pltpu.VMEM((ZERO_ROWS, HALF), U32),
                            pltpu.VMEM((D_MODEL, D_FF_EXPERT), BF16), pltpu.VMEM((D_MODEL, D_FF_EXPERT), BF16),
                            pltpu.VMEM((D_FF_EXPERT, D_MODEL), BF16),
                            pltpu.SemaphoreType.DMA((GATHER_BUFS,)), pltpu.SemaphoreType.DMA((SCATTER_BUFS,)),
                            pltpu.SemaphoreType.DMA(()), pltpu.SemaphoreType.DMA(())]),
        out_shape=jax.ShapeDtypeStruct((n_areas * AREA_ROWS + DUMP_CHUNKS * SUBLANES, HALF), U32),
        compiler_params=cparams(("arbitrary",)),
        name="moe_ffn",
    )(src_tab, dst_tab, bstart, used8, xs, w1[0], w3[0], w2[0])

    def combine(h2d, route, first_area, scope, tile):
        n = h2d.shape[0]
        apt = tile // scope
        return pl.pallas_call(
            _moe_combine_kernel,
            grid=(n // tile,),
            in_specs=[pl.BlockSpec((tile, D_MODEL), lambda t: (t, 0)),
                      pl.BlockSpec((tile, LANES), lambda t: (t, 0)),
                      pl.BlockSpec((apt * AREA_ROWS, HALF), lambda t: (first_area // apt + t, 0)),
                      _full((1, D_MODEL)), _full((1, D_MODEL))],
            out_specs=pl.BlockSpec((tile, D_MODEL), lambda t: (t, 0)),
            out_shape=jax.ShapeDtypeStruct((n, D_MODEL), F32),
            compiler_params=cparams(("arbitrary",)),
            name="moe_combine",
        )(h2d, route, ys, row(ln2_g[0]), row(ln2_b[0]))

    y_p = combine(h_p.reshape(n_prompt, D_MODEL), route_p, 0, PROMPT_SCOPE, TILE_Q)
    y_s = combine(h_s, route_s, p_areas, dec_b, dec_b)

    kv_shape = lambda n: (DEPTH, n, WINDOW, N_KV_HEADS, HEAD_DIM)
    return (y_p.reshape(bsz, seq, D_MODEL), y_s.reshape(dec_b, dec_t, D_MODEL),
            nk_p.reshape(kv_shape(bsz)), nv_p.reshape(kv_shape(bsz)), nc_p[None],
            nk_s.reshape(kv_shape(dec_b)), nv_s.reshape(kv_shape(dec_b)), nc_s[None])
```

```python
import functools

import jax
import jax.numpy as jnp
from jax import lax
from jax.experimental import pallas as pl
from jax.experimental.pallas import tpu as pltpu

F32 = jnp.float32
BF16 = jnp.bfloat16
U32 = jnp.uint32

D_MODEL = 1024
N_HEADS = 8
N_KV_HEADS = 2
HEAD_DIM = 64
Q_GROUP = N_HEADS // N_KV_HEADS
ATTN_DIM = N_HEADS * HEAD_DIM
KV_DIM = N_KV_HEADS * HEAD_DIM
WINDOW = 128
ROPE_THETA = 10000.0
ATTN_SCALE = HEAD_DIM ** -0.5
CONV_CH = D_MODEL - ATTN_DIM
CONV_WIDTH = 31
IN_DIM = ATTN_DIM + 2 * KV_DIM + 2 * CONV_CH
N_GROUPS = 4
EXPERTS_PER_GROUP = 8
N_EXPERTS = N_GROUPS * EXPERTS_PER_GROUP
D_FF_EXPERT = 256
LN_EPS = 1e-5
NEG_INF = -1e30
DEPTH = 1
ALPHA = (2.0 * DEPTH) ** 0.25
PAST_LEN = 16384
LOG2E = 1.4426950408889634

LANES = 128
SUBLANES = 8
HALF = D_MODEL // 2
ELANE0 = N_GROUPS
PROMPT_SCOPE = 256
AREA_ROWS = 768
AREA_CHUNKS = AREA_ROWS // SUBLANES
BLOCK_CHUNKS = 32
BLOCK_ROWS = BLOCK_CHUNKS * SUBLANES
TILE_Q = 512
CONV_PRE = 32
CONV_SHIFT_EXTRA = 24
SAMPLE_TB = 32
SAMPLE_GRP = 8
VMEM_LIMIT = 56 * 1024 * 1024


def _layer_norm(x, g, b):
    mu = jnp.mean(x, axis=-1, keepdims=True)
    xc = x - mu
    var = jnp.mean(xc * xc, axis=-1, keepdims=True)
    return xc * lax.rsqrt(var + LN_EPS) * g + b


def _rope(x, x_swapped, cos, sin_signed):
    pieces = [x[:, g * LANES:(g + 1) * LANES] * cos + x_swapped[:, g * LANES:(g + 1) * LANES] * sin_signed
              for g in range(x.shape[-1] // LANES)]
    return pieces[0] if len(pieces) == 1 else jnp.concatenate(pieces, axis=-1)


def _in_proj_glu(xb, w_in_ref, b_in_ref):
    lo, hi = ATTN_DIM + 2 * KV_DIM, IN_DIM
    cacg = jnp.dot(xb, w_in_ref[:, lo:hi], preferred_element_type=F32) + b_in_ref[:, lo:hi]
    return cacg[:, :CONV_CH] * jax.nn.sigmoid(cacg[:, CONV_CH:])


def _in_proj_qkv(xb, w_in_ref, b_in_ref, rope):
    o_v, o_e = ATTN_DIM + KV_DIM, ATTN_DIM + 2 * KV_DIM
    qkv = jnp.dot(xb, w_in_ref[:, 0:o_e], preferred_element_type=F32) + b_in_ref[:, 0:o_e]
    sw = jnp.dot(xb, w_in_ref[:, IN_DIM:], preferred_element_type=F32) + b_in_ref[:, IN_DIM:]
    k = _rope(qkv[:, ATTN_DIM:o_v], sw[:, ATTN_DIM:], rope[:, 0:LANES], rope[:, LANES:2 * LANES])
    q = _rope(qkv[:, :ATTN_DIM], sw[:, :ATTN_DIM], rope[:, 2 * LANES:3 * LANES], rope[:, 3 * LANES:])
    return q, k, qkv[:, o_v:]


def _pack_rows(v):
    lo = lax.bitcast_convert_type(v[:, :HALF], U32)
    hi = lax.bitcast_convert_type(v[:, HALF:], U32)
    return (lo >> 16) | (hi & jnp.uint32(0xFFFF0000))


def _unpack_rows(w):
    lo = lax.bitcast_convert_type(w << 16, F32).astype(BF16)
    hi = lax.bitcast_convert_type(w & jnp.uint32(0xFFFF0000), F32).astype(BF16)
    return lo, hi


def _route_and_sort(h, w_r, scope):
    n = h.shape[0]
    hb = h.astype(BF16)
    logits = jnp.dot(hb, w_r, preferred_element_type=F32)
    lane = lax.broadcasted_iota(jnp.int32, (n, LANES), 1)
    lane_f = lane.astype(F32)
    is_g = lane < N_GROUPS
    lg = jnp.where(is_g, logits, -jnp.inf)
    gmax = jnp.max(lg, axis=-1, keepdims=True)
    gidx = jnp.min(jnp.where(lg == gmax, lane_f, float(LANES)), axis=-1, keepdims=True)
    p_g = 1.0 / jnp.sum(jnp.where(is_g, jnp.exp(logits - gmax), 0.0), axis=-1, keepdims=True)
    egrp = ((lane - ELANE0) >> 3).astype(F32)
    emask = (lane >= ELANE0) & (lane < ELANE0 + N_EXPERTS) & (egrp == gidx)
    sel = jnp.where(emask, logits, -jnp.inf)
    v1 = jnp.max(sel, axis=-1, keepdims=True)
    i1 = jnp.min(jnp.where(sel == v1, lane_f, float(LANES)), axis=-1, keepdims=True)
    sel2 = jnp.where(lane_f == i1, -jnp.inf, sel)
    v2 = jnp.max(sel2, axis=-1, keepdims=True)
    i2 = jnp.min(jnp.where(sel2 == v2, lane_f, float(LANES)), axis=-1, keepdims=True)
    t = jnp.exp(v2 - v1)
    gate1 = p_g / (1.0 + t)
    gate2 = p_g * t / (1.0 + t)
    pick1 = lane_f == i1
    pick2 = lane_f == i2
    member = jnp.where(pick1 | pick2, 1.0, 0.0)

    r_i = lax.broadcasted_iota(jnp.int32, (scope, scope), 0)
    c_i = lax.broadcasted_iota(jnp.int32, (scope, scope), 1)
    lower = jnp.where(c_i < r_i, 1.0, 0.0).astype(BF16)
    e_r = lax.broadcasted_iota(jnp.int32, (LANES, LANES), 0)
    e_c = lax.broadcasted_iota(jnp.int32, (LANES, LANES), 1)
    upper = jnp.where(e_r < e_c, 1.0, 0.0).astype(BF16)
    s_iota = lax.broadcasted_iota(jnp.int32, (scope, AREA_ROWS), 1).astype(F32)

    slots1, slots2, areas, counts = [], [], [], []
    for sc in range(n // scope):
        rows = slice(sc * scope, (sc + 1) * scope)
        m_sc = member[rows]
        before = jnp.dot(lower, m_sc.astype(BF16), preferred_element_type=F32)
        cnt = jnp.sum(m_sc, axis=0, keepdims=True)
        c8 = jnp.floor((cnt + (SUBLANES - 1)) * (1.0 / SUBLANES))
        c8b = jnp.broadcast_to(c8, (SUBLANES, LANES)).astype(BF16)
        off8 = jnp.dot(c8b, upper, preferred_element_type=F32)[0:1]
        slot_all = before + off8 * float(SUBLANES)
        s1 = jnp.sum(jnp.where(pick1[rows], slot_all, 0.0), axis=-1, keepdims=True)
        s2 = jnp.sum(jnp.where(pick2[rows], slot_all, 0.0), axis=-1, keepdims=True)
        perm_t = jnp.where((s_iota == s1) | (s_iota == s2), 1.0, 0.0).astype(BF16)
        sorted_rows = lax.dot_general(perm_t, hb[rows], (((0,), (0,)), ((), ())),
                                      preferred_element_type=F32)
        areas.append(_pack_rows(sorted_rows))
        counts.append(cnt)
        slots1.append(s1)
        slots2.append(s2)
    slot1 = jnp.concatenate(slots1, axis=0) if len(slots1) > 1 else slots1[0]
    slot2 = jnp.concatenate(slots2, axis=0) if len(slots2) > 1 else slots2[0]
    route = jnp.where(lane == 0, slot1, jnp.where(lane == 1, slot2, jnp.where(lane == 2, gate1,
                      jnp.where(lane == 3, gate2, 0.0))))
    return route, areas, counts


def _softmax_rows_with_sink(s, sink, extra=None):
    m = jnp.maximum(jnp.max(s, axis=-1, keepdims=True), sink)
    if extra is not None:
        m = jnp.maximum(m, extra)
    e = jnp.exp2(s - m)
    den = jnp.sum(e, axis=-1, keepdims=True) + jnp.exp2(sink - m)
    if extra is not None:
        ee = jnp.exp2(extra - m)
        den = den + ee
        return e / den, ee / den
    return e / den, None


def _mix_tail(x, attn, conv_pre, lcg, lcb, w_out_ref, b_out, l1g, l1b):
    conv = _layer_norm(conv_pre, lcg, lcb)
    conv = conv * jax.nn.sigmoid(conv)
    mix = (jnp.dot(attn.astype(BF16), w_out_ref[0:ATTN_DIM, :], preferred_element_type=F32)
           + jnp.dot(conv.astype(BF16), w_out_ref[ATTN_DIM:, :], preferred_element_type=F32) + b_out)
    return _layer_norm(ALPHA * x + mix, l1g, l1b)


def _mix_prompt_kernel(sinks_ref, x_ref, rope_ref, xs_s_ref, *rest, tiles):
    xs_ref = rest[12]
    cnt_ref = rest[14]
    step = pl.program_id(0)
    n_real = pl.num_programs(0) - 1

    @pl.when(step < n_real)
    def _():
        _mix_prompt_tile(sinks_ref, x_ref, rope_ref, *rest, i=step % tiles)

    @pl.when(step == n_real)
    def _():
        xs_ref[0:AREA_ROWS, :] = xs_s_ref[...]
        xs_ref[AREA_ROWS:, :] = jnp.zeros((xs_ref.shape[0] - AREA_ROWS, HALF), U32)
        cnt_ref[...] = jnp.zeros(cnt_ref.shape, F32)


def _mix_prompt_tile(sinks_ref, x_ref, rope_ref, w_in_ref, b_in_ref, conv_w_ref, conv_b_ref,
                     lcg_ref, lcb_ref, w_out_ref, b_out_ref, l1g_ref, l1b_ref, w_r_ref,
                     h_ref, xs_ref, route_ref, cnt_ref, nk_ref, nv_ref, nc_ref,
                     kprev, vprev, u_scr, u_shift, *, i):
    tq = x_ref.shape[1]
    pre = CONV_PRE
    n_sc = tq // PROMPT_SCOPE

    @pl.when(i == 0)
    def _():
        kprev[...] = jnp.zeros_like(kprev)
        vprev[...] = jnp.zeros_like(vprev)
        u_scr[0:pre, :] = jnp.zeros((pre, CONV_CH), F32)

    x = x_ref[0]
    xb = x.astype(BF16)

    for sc in range(n_sc):
        rows = slice(sc * PROMPT_SCOPE, (sc + 1) * PROMPT_SCOPE)
        u_scr[pre + rows.start:pre + rows.stop, :] = _in_proj_glu(xb[rows], w_in_ref, b_in_ref)
    conv_pre = []
    for sc in range(n_sc):
        base = sc * PROMPT_SCOPE
        lo = 0 if sc == 0 else base + CONV_SHIFT_EXTRA
        hi = base + PROMPT_SCOPE + CONV_SHIFT_EXTRA
        for r in range(1, SUBLANES):
            u_shift[r - 1, lo:hi, :] = u_scr[lo + r:hi + r, :]
        acc = jnp.zeros((PROMPT_SCOPE, CONV_CH), F32)
        for j in range(CONV_WIDTH):
            a, r = divmod(pre - (CONV_WIDTH - 1) + j, SUBLANES)
            start = base + a * SUBLANES
            src = (u_scr[start:start + PROMPT_SCOPE, :] if r == 0
                   else u_shift[r - 1, start:start + PROMPT_SCOPE, :])
            acc = acc + src * conv_w_ref[j:j + 1, :]
        conv_pre.append(acc + conv_b_ref[...])

    nc_ref[0] = u_scr[pre + tq - (CONV_WIDTH - 1):pre + tq, :]
    tail = u_scr[tq:tq + pre, :]
    u_scr[0:pre, :] = tail

    q, k, v = _in_proj_qkv(xb, w_in_ref, b_in_ref, rope_ref[...])
    qb = q.astype(BF16)
    kb = jnp.concatenate([kprev[...], k], axis=0).astype(BF16)
    vb = jnp.concatenate([vprev[...], v], axis=0).astype(BF16)
    row = lax.broadcasted_iota(jnp.int32, (WINDOW, 2 * WINDOW), 0)
    col = lax.broadcasted_iota(jnp.int32, (WINDOW, 2 * WINDOW), 1)
    dist = row + WINDOW - col
    band = (dist >= 0) & (dist < WINDOW)
    first_col = jnp.where(i > 0, 0, WINDOW)
    attn_blocks = []
    for j in range(tq // WINDOW):
        msk = (band & (col >= first_col)) if j == 0 else band
        heads = [None] * N_HEADS
        for c in range(N_KV_HEADS):
            kc = kb[j * WINDOW:(j + 2) * WINDOW, c * HEAD_DIM:(c + 1) * HEAD_DIM]
            vc = vb[j * WINDOW:(j + 2) * WINDOW, c * HEAD_DIM:(c + 1) * HEAD_DIM]
            qc = jnp.concatenate(
                [qb[j * WINDOW:(j + 1) * WINDOW, (c * Q_GROUP + g) * HEAD_DIM:(c * Q_GROUP + g + 1) * HEAD_DIM]
                 for g in range(Q_GROUP)], axis=0)
            s = lax.dot_general(qc, kc, (((1,), (1,)), ((), ())), preferred_element_type=F32)
            probs = []
            for g in range(Q_GROUP):
                sg = jnp.where(msk, s[g * WINDOW:(g + 1) * WINDOW], NEG_INF)
                p, _ = _softmax_rows_with_sink(sg, sinks_ref[c * Q_GROUP + g] * LOG2E)
                probs.append(p.astype(BF16))
            o = jnp.dot(jnp.concatenate(probs, axis=0), vc, preferred_element_type=F32)
            for g in range(Q_GROUP):
                heads[c * Q_GROUP + g] = o[g * WINDOW:(g + 1) * WINDOW]
        attn_blocks.append(jnp.concatenate(heads, axis=1))
    attn = jnp.concatenate(attn_blocks, axis=0)
    kprev[...] = k[tq - WINDOW:, :]
    vprev[...] = v[tq - WINDOW:, :]
    nk_ref[0] = k[tq - WINDOW:, :]
    nv_ref[0] = v[tq - WINDOW:, :]

    for sc in range(n_sc):
        rows = slice(sc * PROMPT_SCOPE, (sc + 1) * PROMPT_SCOPE)
        h = _mix_tail(x[rows], attn[rows], conv_pre[sc], lcg_ref[...], lcb_ref[...], w_out_ref, b_out_ref[...],
                      l1g_ref[...], l1b_ref[...])
        h_ref[0, rows, :] = h
        route, areas, counts = _route_and_sort(h, w_r_ref[...], PROMPT_SCOPE)
        route_ref[rows, :] = route
        xs_ref[sc * AREA_ROWS:(sc + 1) * AREA_ROWS, :] = areas[0]
        cnt_ref[sc] = jnp.broadcast_to(counts[0], (SUBLANES, LANES))


def _mix_sample_kernel(sinks_ref, x_ref, rope_ref, ck_ref, cv_ref, st_ref, w_in_ref, b_in_ref,
                       conv_w_ref, conv_b_ref, lcg_ref, lcb_ref, w_out_ref, b_out_ref, l1g_ref, l1b_ref,
                       w_r_ref,
                       h_ref, nk_ref, nv_ref, nc_ref, xs_ref, route_ref, cnt_ref,
                       h_scr):
    tb = x_ref.shape[0]
    n_tok = h_scr.shape[0]
    step = pl.program_id(0)
    x = x_ref[...]
    xb = x.astype(BF16)
    u = _in_proj_glu(xb, w_in_ref, b_in_ref)
    q, k_new, v_new = _in_proj_qkv(xb, w_in_ref, b_in_ref, rope_ref[...])

    qb = q.astype(BF16)
    knb = k_new.astype(BF16).astype(F32)
    vnb = v_new.astype(BF16).astype(F32)
    gq = SAMPLE_GRP * Q_GROUP
    nkeys = SAMPLE_GRP * WINDOW
    r_i = lax.broadcasted_iota(jnp.int32, (gq, nkeys), 0)
    c_i = lax.broadcasted_iota(jnp.int32, (gq, nkeys), 1)
    msk = ((c_i >> 7) == (r_i & (SAMPLE_GRP - 1))) & ((c_i & (WINDOW - 1)) >= 1)
    r_col = lax.broadcasted_iota(jnp.int32, (gq, 1), 0) >> 3
    attn_groups = []
    for grp in range(tb // SAMPLE_GRP):
        r0 = grp * SAMPLE_GRP
        kc_all = ck_ref[r0:r0 + SAMPLE_GRP].reshape(nkeys, KV_DIM).astype(BF16)
        vc_all = cv_ref[r0:r0 + SAMPLE_GRP].reshape(nkeys, KV_DIM).astype(BF16)
        heads = [None] * N_HEADS
        for c in range(N_KV_HEADS):
            lanes = slice(c * HEAD_DIM, (c + 1) * HEAD_DIM)
            qc = jnp.concatenate(
                [qb[r0:r0 + SAMPLE_GRP, (c * Q_GROUP + g) * HEAD_DIM:(c * Q_GROUP + g + 1) * HEAD_DIM]
                 for g in range(Q_GROUP)], axis=0)
            s = lax.dot_general(qc, kc_all[:, lanes], (((1,), (1,)), ((), ())), preferred_element_type=F32)
            s = jnp.where(msk, s, NEG_INF)
            kn = jnp.concatenate([knb[r0:r0 + SAMPLE_GRP, lanes]] * Q_GROUP, axis=0)
            vn = jnp.concatenate([vnb[r0:r0 + SAMPLE_GRP, lanes]] * Q_GROUP, axis=0)
            s_new = jnp.sum(qc.astype(F32) * kn, axis=-1, keepdims=True)
            sink = jnp.zeros((gq, 1), F32)
            for g in range(Q_GROUP):
                sink = jnp.where(r_col == g, sinks_ref[c * Q_GROUP + g] * LOG2E, sink)
            p, p_new = _softmax_rows_with_sink(s, sink, s_new)
            o = (jnp.dot(p.astype(BF16), vc_all[:, lanes], preferred_element_type=F32)
                 + p_new.astype(BF16).astype(F32) * vn)
            for g in range(Q_GROUP):
                heads[c * Q_GROUP + g] = o[g * SAMPLE_GRP:(g + 1) * SAMPLE_GRP]
        attn_groups.append(jnp.concatenate(heads, axis=1))
    attn = jnp.concatenate(attn_groups, axis=0)

    st = st_ref[...]
    acc = jnp.sum(st * conv_w_ref[0:CONV_WIDTH - 1, :][None], axis=1)
    conv_pre = acc + u * conv_w_ref[CONV_WIDTH - 1:CONV_WIDTH, :] + conv_b_ref[...]

    nk_ref[:, 0:WINDOW - 1, :] = ck_ref[:, 1:WINDOW, :]
    nv_ref[:, 0:WINDOW - 1, :] = cv_ref[:, 1:WINDOW, :]
    nc_ref[:, 0:CONV_WIDTH - 2, :] = st_ref[:, 1:CONV_WIDTH - 1, :]
    for b in range(tb):
        nk_ref[b, WINDOW - 1:WINDOW, :] = k_new[b:b + 1, :]
        nv_ref[b, WINDOW - 1:WINDOW, :] = v_new[b:b + 1, :]
        nc_ref[b, CONV_WIDTH - 2:CONV_WIDTH - 1, :] = u[b:b + 1, :]

    h = _mix_tail(x, attn, conv_pre, lcg_ref[...], lcb_ref[...], w_out_ref, b_out_ref[...],
                  l1g_ref[...], l1b_ref[...])
    h_ref[...] = h
    h_scr[pl.ds(pl.multiple_of(step * tb, tb), tb), :] = h

    @pl.when(step == pl.num_programs(0) - 1)
    def _():
        route, areas, counts = _route_and_sort(h_scr[...], w_r_ref[...], n_tok)
        route_ref[...] = route
        xs_ref[...] = areas[0]
        cnt_ref[0] = jnp.broadcast_to(counts[0], (SUBLANES, LANES))


GATHER_BUFS = 8
SCATTER_BUFS = 8
SCATTER_LAG = 2
ZERO_BITS = (64, 32, 16, 8, 4, 2, 1)
DUMP_CHUNKS = SCATTER_BUFS * BLOCK_CHUNKS
ZERO_ROWS = max(ZERO_BITS[0], DUMP_CHUNKS) * SUBLANES


def _moe_ffn_kernel(src_ref, dst_ref, bstart_ref, used_ref,
                    xs_hbm, w1_ref, w3_ref, w2_ref,
                    ys_hbm,
                    xbuf, ybuf, hbuf, zbuf, w1b, w3b, w2b, sem_in, sem_out, sem_zero, sem_dump):
    e = pl.program_id(0)
    n_areas = used_ref.shape[0]
    total = bstart_ref[N_EXPERTS]
    dump_row0 = n_areas * AREA_ROWS
    lookahead = GATHER_BUFS - 1
    any_rows = pl.ds(0, SUBLANES)

    def chunk_at(ref, idx):
        return pl.ds(pl.multiple_of(ref[idx] * SUBLANES, SUBLANES), SUBLANES)

    def in_copy(rows, sl, c):
        return pltpu.make_async_copy(xs_hbm.at[rows], xbuf.at[sl, pl.ds(c * SUBLANES, SUBLANES)], sem_in.at[sl])

    def out_copy(rows, sl, c):
        return pltpu.make_async_copy(ybuf.at[sl, pl.ds(c * SUBLANES, SUBLANES)], ys_hbm.at[rows], sem_out.at[sl])

    def start_gather(blk):
        for c in range(BLOCK_CHUNKS):
            in_copy(chunk_at(src_ref, blk * BLOCK_CHUNKS + c), blk % GATHER_BUFS, c).start()

    def wait_gather(blk):
        for c in range(BLOCK_CHUNKS):
            in_copy(any_rows, blk % GATHER_BUFS, c).wait()

    def start_scatter(blk):
        for c in range(BLOCK_CHUNKS):
            out_copy(chunk_at(dst_ref, (blk + SCATTER_LAG) * BLOCK_CHUNKS + c), blk % SCATTER_BUFS, c).start()

    def wait_scatter(sl):
        for c in range(BLOCK_CHUNKS):
            out_copy(any_rows, sl, c).wait()

    def load_rows(blk):
        return _unpack_rows(xbuf[blk % GATHER_BUFS])

    def issue_dmas(blk):
        start_scatter(blk - SCATTER_LAG)
        start_gather(blk + lookahead)

    def up_gate(blk, rows):
        x_lo, x_hi = rows
        a1 = (jnp.dot(x_lo, w1b[0:HALF, :], preferred_element_type=F32)
              + jnp.dot(x_hi, w1b[HALF:, :], preferred_element_type=F32))
        a3 = (jnp.dot(x_lo, w3b[0:HALF, :], preferred_element_type=F32)
              + jnp.dot(x_hi, w3b[HALF:, :], preferred_element_type=F32))
        hbuf[blk % SCATTER_BUFS] = (a1 * jax.nn.sigmoid(a1) * a3).astype(BF16)

    def down(blk):
        y = jnp.dot(hbuf[blk % SCATTER_BUFS], w2b[...], preferred_element_type=F32)
        ybuf[blk % SCATTER_BUFS] = _pack_rows(y.astype(BF16).astype(F32))

    def for_each_tail_piece(fn):
        def area_body(s, carry):
            used = used_ref[s]
            tail = AREA_CHUNKS - used
            row = (s * AREA_CHUNKS + used) * SUBLANES
            for bit in ZERO_BITS:
                take = (tail & bit) != 0

                @pl.when(take)
                def _(row=row, bit=bit):
                    fn(pltpu.make_async_copy(zbuf.at[pl.ds(0, bit * SUBLANES)],
                                             ys_hbm.at[pl.ds(pl.multiple_of(row, SUBLANES), bit * SUBLANES)],
                                             sem_zero))
                row = row + jnp.where(take, bit * SUBLANES, 0)
            return carry
        lax.fori_loop(0, n_areas, area_body, 0)

    @pl.when(e == 0)
    def _():
        zbuf[...] = jnp.zeros(zbuf.shape, U32)
        ybuf[...] = jnp.zeros(ybuf.shape, U32)
        dump = pltpu.make_async_copy(zbuf.at[pl.ds(0, DUMP_CHUNKS * SUBLANES)],
                                     ys_hbm.at[pl.ds(dump_row0, DUMP_CHUNKS * SUBLANES)], sem_dump)
        dump.start()
        dump.wait()
        for_each_tail_piece(lambda cp: cp.start())
        for sl in range(SCATTER_BUFS - SCATTER_LAG):
            for c in range(BLOCK_CHUNKS):
                out_copy(pl.ds(dump_row0 + (sl * BLOCK_CHUNKS + c) * SUBLANES, SUBLANES), sl, c).start()
        for k in range(lookahead):
            start_gather(k)

    w1b[...] = w1_ref[0].astype(BF16)
    w3b[...] = w3_ref[0].astype(BF16)
    w2b[...] = w2_ref[0].astype(BF16)
    b0 = bstart_ref[e]
    b1 = bstart_ref[e + 1]

    @pl.when(b1 > b0)
    def _():
        wait_gather(b0)
        rows = load_rows(b0)
        issue_dmas(b0)
        up_gate(b0, rows)

    def block_body(blk, carry):
        wait_gather(blk)
        wait_scatter((blk - 1) % SCATTER_BUFS)
        rows = load_rows(blk)
        issue_dmas(blk)
        down(blk - 1)
        up_gate(blk, rows)
        return carry

    lax.fori_loop(b0 + 1, b1, block_body, 0)

    @pl.when(b1 > b0)
    def _():
        wait_scatter((b1 - 1) % SCATTER_BUFS)
        down(b1 - 1)

    @pl.when(e == N_EXPERTS - 1)
    def _():
        for k in range(SCATTER_LAG):
            start_scatter(total - SCATTER_LAG + k)
        for sl in range(SCATTER_BUFS):
            wait_scatter(sl)
        for k in range(lookahead):
            wait_gather(total + k)
        for_each_tail_piece(lambda cp: cp.wait())


def _moe_combine_kernel(h_ref, route_ref, ys_ref, l2g_ref, l2b_ref, y_ref):
    n = h_ref.shape[0]
    scope = n // (ys_ref.shape[0] // AREA_ROWS)
    s_iota = lax.broadcasted_iota(jnp.int32, (scope, AREA_ROWS), 1).astype(F32)
    for sc in range(n // scope):
        rows = slice(sc * scope, (sc + 1) * scope)
        route = route_ref[rows, :]
        slot1 = route[:, 0:1]
        slot2 = route[:, 1:2]
        gate1 = route[:, 2:3]
        gate2 = route[:, 3:4]
        gmat = (jnp.where(s_iota == slot1, gate1, 0.0) + jnp.where(s_iota == slot2, gate2, 0.0)).astype(BF16)
        y_lo, y_hi = _unpack_rows(ys_ref[sc * AREA_ROWS:(sc + 1) * AREA_ROWS, :])
        f = jnp.concatenate([jnp.dot(gmat, y_lo, preferred_element_type=F32),
                             jnp.dot(gmat, y_hi, preferred_element_type=F32)], axis=1)
        y_ref[rows, :] = _layer_norm(ALPHA * h_ref[rows, :] + f, l2g_ref[...], l2b_ref[...])


def _excl_cumsum(a, axis):
    return jnp.cumsum(a, axis=axis) - a


def _plan_blocks(cnt, q_max):
    n_areas = cnt.shape[0]
    c8 = (cnt + (SUBLANES - 1)) // SUBLANES
    used8 = jnp.sum(c8, axis=1)
    base8 = jnp.arange(n_areas, dtype=jnp.int32) * AREA_CHUNKS
    src_runs = (base8[:, None] + _excl_cumsum(c8, 1)).T
    len_runs = c8.T
    tot = jnp.sum(len_runs, axis=1)
    ptot = (tot + BLOCK_CHUNKS - 1) // BLOCK_CHUNKS * BLOCK_CHUNKS
    pstart = _excl_cumsum(ptot, 0)
    dst_runs = pstart[:, None] + _excl_cumsum(len_runs, 1)
    off = (src_runs - dst_runs).reshape(-1)
    diff = off - jnp.concatenate([jnp.zeros((1,), jnp.int32), off[:-1]])
    q = jnp.arange(q_max, dtype=jnp.int32)
    src = q + jnp.sum(jnp.where(dst_runs.reshape(1, -1) <= q[:, None], diff[None, :], 0), axis=1)
    valid = jnp.any((pstart[None, :] <= q[:, None]) & (q[:, None] < (pstart + tot)[None, :]), axis=1)
    zero_chunk = AREA_CHUNKS - 1
    ahead = jnp.full(((GATHER_BUFS - 1) * BLOCK_CHUNKS,), zero_chunk, jnp.int32)
    src_tab = jnp.concatenate([jnp.where(valid, src, zero_chunk).astype(jnp.int32), ahead])
    qv = jnp.arange(-SCATTER_LAG * BLOCK_CHUNKS, q_max, dtype=jnp.int32)
    dump = n_areas * AREA_CHUNKS + ((qv // BLOCK_CHUNKS) % SCATTER_BUFS) * BLOCK_CHUNKS + qv % BLOCK_CHUNKS
    lead = jnp.zeros((SCATTER_LAG * BLOCK_CHUNKS,), jnp.int32)
    dst_tab = jnp.where(jnp.concatenate([lead > 0, valid]), jnp.concatenate([lead, src]), dump).astype(jnp.int32)
    bstart = jnp.concatenate([pstart, jnp.sum(ptot, keepdims=True)]) // BLOCK_CHUNKS
    return src_tab, dst_tab, bstart.astype(jnp.int32), used8.astype(jnp.int32)


def _rope_tables(pos):
    half = HEAD_DIM // 2
    inv = ROPE_THETA ** (-jnp.arange(half, dtype=F32) / half)
    ang = pos.astype(F32)[:, None] * inv
    cos = jnp.cos(ang)
    sin = jnp.sin(ang)
    reps = LANES // HEAD_DIM
    cos_t = jnp.concatenate([cos, cos] * reps, axis=1)
    sin_t = jnp.concatenate([-sin, sin] * reps, axis=1)
    q_scale = ATTN_SCALE * LOG2E
    return jnp.concatenate([cos_t, sin_t, cos_t * q_scale, sin_t * q_scale], axis=1)


def _full(shape):
    nd = len(shape)
    return pl.BlockSpec(shape, lambda *_: (0,) * nd)


def kernel(x_prompt, x_sample, cache_k, cache_v, state_conv, w_in, b_in, sinks, conv_w, conv_b, ln_conv_g,
           ln_conv_b, w_out, b_out, ln1_g, ln1_b, w_router_group, w_router_expert, w1, w3, w2, ln2_g, ln2_b):
    assert w_in.shape[0] == DEPTH
    bsz, seq, _ = x_prompt.shape
    dec_b, dec_t, _ = x_sample.shape
    assert dec_t == 1 and seq % TILE_Q == 0 and dec_b % SAMPLE_TB == 0 and 2 * dec_b + 7 * N_EXPERTS <= AREA_ROWS
    n_prompt = bsz * seq
    tiles = seq // TILE_Q
    spt = TILE_Q // PROMPT_SCOPE
    p_areas = n_prompt // PROMPT_SCOPE
    n_areas = p_areas + spt
    q_max = (n_areas * AREA_CHUNKS + N_EXPERTS * (BLOCK_CHUNKS - 1) + BLOCK_CHUNKS - 1) // BLOCK_CHUNKS * BLOCK_CHUNKS

    row = lambda a: a.reshape(1, -1)
    qk = ATTN_DIM + KV_DIM
    half = HEAD_DIM // 2

    def swap_halves(a):
        parts = a[..., :qk].reshape(a.shape[:-1] + (qk // HEAD_DIM, 2, half))
        return parts[..., ::-1, :].reshape(a.shape[:-1] + (qk,))

    w_in_b = jnp.concatenate([w_in[0], swap_halves(w_in[0])], axis=1).astype(BF16)
    b_in_x = jnp.concatenate([b_in[0], swap_halves(b_in[0])])
    w_out_b = w_out[0].astype(BF16)
    w_r = jnp.concatenate([w_router_group[0], w_router_expert[0].reshape(D_MODEL, N_EXPERTS),
                           jnp.zeros((D_MODEL, LANES - ELANE0 - N_EXPERTS), F32)], axis=1).astype(BF16)
    shared = (w_in_b, row(b_in_x), conv_w[0], row(conv_b[0]), row(ln_conv_g[0]), row(ln_conv_b[0]),
              w_out_b, row(b_out[0]), row(ln1_g[0]), row(ln1_b[0]), w_r)
    shared_specs = [_full(a.shape) for a in shared]
    smem = pl.BlockSpec(memory_space=pltpu.SMEM)
    rope_p = _rope_tables(jnp.arange(seq))
    rope_s = _rope_tables(PAST_LEN + jnp.arange(dec_t))

    cparams = lambda sem: pltpu.CompilerParams(dimension_semantics=sem, vmem_limit_bytes=VMEM_LIMIT)

    ck = cache_k[0].reshape(dec_b, WINDOW, KV_DIM)
    cv = cache_v[0].reshape(dec_b, WINDOW, KV_DIM)
    st = state_conv[0]
    tb = SAMPLE_TB
    h_s, nk_s, nv_s, nc_s, xs_s, route_s, cnt_s = pl.pallas_call(
        _mix_sample_kernel,
        grid=(dec_b // tb,),
        in_specs=[smem,
                  pl.BlockSpec((tb, D_MODEL), lambda t: (t, 0)),
                  _full(rope_s.shape),
                  pl.BlockSpec((tb, WINDOW, KV_DIM), lambda t: (t, 0, 0)),
                  pl.BlockSpec((tb, WINDOW, KV_DIM), lambda t: (t, 0, 0)),
                  pl.BlockSpec((tb, CONV_WIDTH - 1, CONV_CH), lambda t: (t, 0, 0))] + shared_specs,
        out_specs=[pl.BlockSpec((tb, D_MODEL), lambda t: (t, 0)),
                   pl.BlockSpec((tb, WINDOW, KV_DIM), lambda t: (t, 0, 0)),
                   pl.BlockSpec((tb, WINDOW, KV_DIM), lambda t: (t, 0, 0)),
                   pl.BlockSpec((tb, CONV_WIDTH - 1, CONV_CH), lambda t: (t, 0, 0)),
                   _full((AREA_ROWS, HALF)),
                   _full((dec_b, LANES)),
                   _full((1, SUBLANES, LANES))],
        out_shape=[jax.ShapeDtypeStruct((dec_b, D_MODEL), F32),
                   jax.ShapeDtypeStruct((dec_b, WINDOW, KV_DIM), F32),
                   jax.ShapeDtypeStruct((dec_b, WINDOW, KV_DIM), F32),
                   jax.ShapeDtypeStruct((dec_b, CONV_WIDTH - 1, CONV_CH), F32),
                   jax.ShapeDtypeStruct((AREA_ROWS, HALF), U32),
                   jax.ShapeDtypeStruct((dec_b, LANES), F32),
                   jax.ShapeDtypeStruct((1, SUBLANES, LANES), F32)],
        scratch_shapes=[pltpu.VMEM((dec_b, D_MODEL), F32)],
        compiler_params=cparams(("arbitrary",)),
        name="mix_sample",
    )(sinks[0], x_sample.reshape(dec_b, D_MODEL), rope_s, ck, cv, st, *shared)

    n_real = bsz * tiles
    real = lambda t: jnp.minimum(t, n_real - 1)
    h_p, xs, route_p, cnt_p, nk_p, nv_p, nc_p = pl.pallas_call(
        functools.partial(_mix_prompt_kernel, tiles=tiles),
        grid=(n_real + 1,),
        in_specs=[smem,
                  pl.BlockSpec((1, TILE_Q, D_MODEL), lambda t: (real(t) // tiles, real(t) % tiles, 0)),
                  pl.BlockSpec((TILE_Q, 4 * LANES), lambda t: (real(t) % tiles, 0)),
                  _full((AREA_ROWS, HALF))] + shared_specs,
        out_specs=[pl.BlockSpec((1, TILE_Q, D_MODEL), lambda t: (real(t) // tiles, real(t) % tiles, 0)),
                   pl.BlockSpec((spt * AREA_ROWS, HALF), lambda t: (t, 0)),
                   pl.BlockSpec((TILE_Q, LANES), lambda t: (real(t), 0)),
                   pl.BlockSpec((spt, SUBLANES, LANES), lambda t: (t, 0, 0)),
                   pl.BlockSpec((1, WINDOW, KV_DIM), lambda t: (real(t) // tiles, 0, 0)),
                   pl.BlockSpec((1, WINDOW, KV_DIM), lambda t: (real(t) // tiles, 0, 0)),
                   pl.BlockSpec((1, CONV_WIDTH - 1, CONV_CH), lambda t: (real(t) // tiles, 0, 0))],
        out_shape=[jax.ShapeDtypeStruct((bsz, seq, D_MODEL), F32),
                   jax.ShapeDtypeStruct((n_areas * AREA_ROWS, HALF), U32),
                   jax.ShapeDtypeStruct((n_prompt, LANES), F32),
                   jax.ShapeDtypeStruct((n_areas, SUBLANES, LANES), F32),
                   jax.ShapeDtypeStruct((bsz, WINDOW, KV_DIM), F32),
                   jax.ShapeDtypeStruct((bsz, WINDOW, KV_DIM), F32),
                   jax.ShapeDtypeStruct((bsz, CONV_WIDTH - 1, CONV_CH), F32)],
        scratch_shapes=[pltpu.VMEM((WINDOW, KV_DIM), F32), pltpu.VMEM((WINDOW, KV_DIM), F32),
                        pltpu.VMEM((TILE_Q + CONV_PRE, CONV_CH), F32),
                        pltpu.VMEM((SUBLANES - 1, TILE_Q + CONV_SHIFT_EXTRA, CONV_CH), F32)],
        compiler_params=cparams(("arbitrary",)),
        name="mix_prompt",
    )(sinks[0], x_prompt, rope_p, xs_s, *shared)

    cnt = jnp.concatenate([cnt_p[:p_areas, 0], cnt_s[:, 0], cnt_p[p_areas + 1:, 0]], axis=0)
    cnt = cnt[:, ELANE0:ELANE0 + N_EXPERTS].astype(jnp.int32)
    src_tab, dst_tab, bstart, used8 = _plan_blocks(cnt, q_max)
    w_idx = lambda e, *_: (e, 0, 0)
    ys = pl.pallas_call(
        _moe_ffn_kernel,
        grid_spec=pltpu.PrefetchScalarGridSpec(
            num_scalar_prefetch=4,
            grid=(N_EXPERTS,),
            in_specs=[pl.BlockSpec(memory_space=pl.ANY),
                      pl.BlockSpec((1, D_MODEL, D_FF_EXPERT), w_idx),
                      pl.BlockSpec((1, D_MODEL, D_FF_EXPERT), w_idx),
                      pl.BlockSpec((1, D_FF_EXPERT, D_MODEL), w_idx)],
            out_specs=pl.BlockSpec(memory_space=pl.ANY),
            scratch_shapes=[pltpu.VMEM((GATHER_BUFS, BLOCK_ROWS, HALF), U32),
                            pltpu.VMEM((SCATTER_BUFS, BLOCK_ROWS, HALF), U32),
                            pltpu.VMEM((SCATTER_BUFS, BLOCK_ROWS, D_FF_EXPERT), BF16),
                            pltpu.VMEM((ZERO_ROWS, HALF), U32),
                            pltpu.VMEM((D_MODEL, D_FF_EXPERT), BF16), pltpu.VMEM((D_MODEL, D_FF_EXPERT), BF16),
                            pltpu.VMEM((D_FF_EXPERT, D_MODEL), BF16),
                            pltpu.SemaphoreType.DMA((GATHER_BUFS,)), pltpu.SemaphoreType.DMA((SCATTER_BUFS,)),
                            pltpu.SemaphoreType.DMA(()), pltpu.SemaphoreType.DMA(())]),
        out_shape=jax.ShapeDtypeStruct((n_areas * AREA_ROWS + DUMP_CHUNKS * SUBLANES, HALF), U32),
        compiler_params=cparams(("arbitrary",)),
        name="moe_ffn",
    )(src_tab, dst_tab, bstart, used8, xs, w1[0], w3[0], w2[0])

    def combine(h2d, route, first_area, scope, tile):
        n = h2d.shape[0]
        apt = tile // scope
        return pl.pallas_call(
            _moe_combine_kernel,
            grid=(n // tile,),
            in_specs=[pl.BlockSpec((tile, D_MODEL), lambda t: (t, 0)),
                      pl.BlockSpec((tile, LANES), lambda t: (t, 0)),
                      pl.BlockSpec((apt * AREA_ROWS, HALF), lambda t: (first_area // apt + t, 0)),
                      _full((1, D_MODEL)), _full((1, D_MODEL))],
            out_specs=pl.BlockSpec((tile, D_MODEL), lambda t: (t, 0)),
            out_shape=jax.ShapeDtypeStruct((n, D_MODEL), F32),
            compiler_params=cparams(("arbitrary",)),
            name="moe_combine",
        )(h2d, route, ys, row(ln2_g[0]), row(ln2_b[0]))

    y_p = combine(h_p.reshape(n_prompt, D_MODEL), route_p, 0, PROMPT_SCOPE, TILE_Q)
    y_s = combine(h_s, route_s, p_areas, dec_b, dec_b)

    kv_shape = lambda n: (DEPTH, n, WINDOW, N_KV_HEADS, HEAD_DIM)
    return (y_p.reshape(bsz, seq, D_MODEL), y_s.reshape(dec_b, dec_t, D_MODEL),
            nk_p.reshape(kv_shape(bsz)), nv_p.reshape(kv_shape(bsz)), nc_p[None],
            nk_s.reshape(kv_shape(dec_b)), nv_s.reshape(kv_shape(dec_b)), nc_s[None])
```

```python
import functools

import jax
import jax.numpy as jnp
import numpy as np
from jax import lax
from jax.experimental import pallas as pl
from jax.experimental.pallas import tpu as pltpu

F32 = jnp.float32
BF16 = jnp.bfloat16
U32 = jnp.uint32

D_MODEL = 1024
N_HEADS = 8
N_KV_HEADS = 2
HEAD_DIM = 64
Q_GROUP = N_HEADS // N_KV_HEADS
ATTN_DIM = N_HEADS * HEAD_DIM
KV_DIM = N_KV_HEADS * HEAD_DIM
WINDOW = 128
ROPE_THETA = 10000.0
ATTN_SCALE = HEAD_DIM ** -0.5
CONV_CH = D_MODEL - ATTN_DIM
CONV_WIDTH = 31
IN_DIM = ATTN_DIM + 2 * KV_DIM + 2 * CONV_CH
N_GROUPS = 4
EXPERTS_PER_GROUP = 8
N_EXPERTS = N_GROUPS * EXPERTS_PER_GROUP
D_FF_EXPERT = 256
LN_EPS = 1e-5
NEG_INF = -1e30
DEPTH = 1
ALPHA = (2.0 * DEPTH) ** 0.25
PAST_LEN = 16384
LOG2E = 1.4426950408889634

LANES = 128
SUBLANES = 8
HALF = D_MODEL // 2
ELANE0 = N_GROUPS
PROMPT_SCOPE = 256
AREA_ROWS = 768
AREA_CHUNKS = AREA_ROWS // SUBLANES
BLOCK_CHUNKS = 32
BLOCK_ROWS = BLOCK_CHUNKS * SUBLANES
TILE_Q = 512
COMBINE_TILE = 1024
CONV_PRE = 32
CONV_SHIFT_EXTRA = 24
SAMPLE_TB = 32
SAMPLE_GRP = 8
VMEM_LIMIT = 56 * 1024 * 1024


def _layer_norm(x, g, b):
    mu = jnp.mean(x, axis=-1, keepdims=True)
    xc = x - mu
    var = jnp.mean(xc * xc, axis=-1, keepdims=True)
    return xc * lax.rsqrt(var + LN_EPS) * g + b


def _rope(x, x_swapped, cos, sin_signed):
    pieces = [x[:, g * LANES:(g + 1) * LANES] * cos + x_swapped[:, g * LANES:(g + 1) * LANES] * sin_signed
              for g in range(x.shape[-1] // LANES)]
    return pieces[0] if len(pieces) == 1 else jnp.concatenate(pieces, axis=-1)


def _in_proj_glu(xb, w_in_ref, b_in_ref):
    lo, hi = ATTN_DIM + 2 * KV_DIM, IN_DIM
    cacg = jnp.dot(xb, w_in_ref[:, lo:hi], preferred_element_type=F32) + b_in_ref[:, lo:hi]
    return cacg[:, :CONV_CH] * jax.nn.sigmoid(cacg[:, CONV_CH:])


def _in_proj_qkv(xb, w_in_ref, b_in_ref, w_sw_ref, b_sw_ref, rope):
    o_v, o_e = ATTN_DIM + KV_DIM, ATTN_DIM + 2 * KV_DIM
    qkv = jnp.dot(xb, w_in_ref[:, 0:o_e], preferred_element_type=F32) + b_in_ref[:, 0:o_e]
    sw = jnp.dot(xb, w_sw_ref[...], preferred_element_type=F32) + b_sw_ref[...]
    k = _rope(qkv[:, ATTN_DIM:o_v], sw[:, ATTN_DIM:], rope[:, 0:LANES], rope[:, LANES:2 * LANES])
    q = _rope(qkv[:, :ATTN_DIM], sw[:, :ATTN_DIM], rope[:, 2 * LANES:3 * LANES], rope[:, 3 * LANES:])
    return q, k, qkv[:, o_v:]


def _pack_rows(v):
    lo = lax.bitcast_convert_type(v[:, :HALF], U32)
    hi = lax.bitcast_convert_type(v[:, HALF:], U32)
    return (lo >> 16) | (hi & jnp.uint32(0xFFFF0000))


def _unpack_rows(w):
    lo = lax.bitcast_convert_type(w << 16, F32).astype(BF16)
    hi = lax.bitcast_convert_type(w & jnp.uint32(0xFFFF0000), F32).astype(BF16)
    return lo, hi


def _route_and_sort(h, w_r, scope):
    n = h.shape[0]
    hb = h.astype(BF16)
    logits = jnp.dot(hb, w_r, preferred_element_type=F32)
    lane = lax.broadcasted_iota(jnp.int32, (n, LANES), 1)
    lane_f = lane.astype(F32)
    is_g = lane < N_GROUPS
    lg = jnp.where(is_g, logits, -jnp.inf)
    gmax = jnp.max(lg, axis=-1, keepdims=True)
    gidx = jnp.min(jnp.where(lg == gmax, lane_f, float(LANES)), axis=-1, keepdims=True)
    p_g = 1.0 / jnp.sum(jnp.where(is_g, jnp.exp(logits - gmax), 0.0), axis=-1, keepdims=True)
    egrp = ((lane - ELANE0) >> 3).astype(F32)
    emask = (lane >= ELANE0) & (lane < ELANE0 + N_EXPERTS) & (egrp == gidx)
    sel = jnp.where(emask, logits, -jnp.inf)
    v1 = jnp.max(sel, axis=-1, keepdims=True)
    i1 = jnp.min(jnp.where(sel == v1, lane_f, float(LANES)), axis=-1, keepdims=True)
    sel2 = jnp.where(lane_f == i1, -jnp.inf, sel)
    v2 = jnp.max(sel2, axis=-1, keepdims=True)
    i2 = jnp.min(jnp.where(sel2 == v2, lane_f, float(LANES)), axis=-1, keepdims=True)
    t = jnp.exp(v2 - v1)
    gate1 = p_g / (1.0 + t)
    gate2 = p_g * t / (1.0 + t)
    pick1 = lane_f == i1
    pick2 = lane_f == i2
    member = jnp.where(pick1 | pick2, 1.0, 0.0)

    r_i = lax.broadcasted_iota(jnp.int32, (scope, scope), 0)
    c_i = lax.broadcasted_iota(jnp.int32, (scope, scope), 1)
    lower = jnp.where(c_i < r_i, 1.0, 0.0).astype(BF16)
    e_r = lax.broadcasted_iota(jnp.int32, (LANES, LANES), 0)
    e_c = lax.broadcasted_iota(jnp.int32, (LANES, LANES), 1)
    upper = jnp.where(e_r < e_c, 1.0, 0.0).astype(BF16)
    s_iota = lax.broadcasted_iota(jnp.int32, (scope, AREA_ROWS), 1).astype(F32)

    slots1, slots2, areas, counts = [], [], [], []
    for sc in range(n // scope):
        rows = slice(sc * scope, (sc + 1) * scope)
        m_sc = member[rows]
        before = jnp.dot(lower, m_sc.astype(BF16), preferred_element_type=F32)
        cnt = jnp.sum(m_sc, axis=0, keepdims=True)
        c8 = jnp.floor((cnt + (SUBLANES - 1)) * (1.0 / SUBLANES))
        c8b = jnp.broadcast_to(c8, (SUBLANES, LANES)).astype(BF16)
        off8 = jnp.dot(c8b, upper, preferred_element_type=F32)[0:1]
        slot_all = before + off8 * float(SUBLANES)
        s1 = jnp.sum(jnp.where(pick1[rows], slot_all, 0.0), axis=-1, keepdims=True)
        s2 = jnp.sum(jnp.where(pick2[rows], slot_all, 0.0), axis=-1, keepdims=True)
        perm_t = jnp.where((s_iota == s1) | (s_iota == s2), 1.0, 0.0).astype(BF16)
        sorted_rows = lax.dot_general(perm_t, hb[rows], (((0,), (0,)), ((), ())),
                                      preferred_element_type=F32)
        areas.append(_pack_rows(sorted_rows))
        counts.append(cnt)
        slots1.append(s1)
        slots2.append(s2)
    slot1 = jnp.concatenate(slots1, axis=0) if len(slots1) > 1 else slots1[0]
    slot2 = jnp.concatenate(slots2, axis=0) if len(slots2) > 1 else slots2[0]
    route = jnp.where(lane == 0, slot1, jnp.where(lane == 1, slot2, jnp.where(lane == 2, gate1,
                      jnp.where(lane == 3, gate2, 0.0))))
    return route, areas, counts


def _softmax_rows_with_sink(s, sink, extra=None):
    m = jnp.maximum(jnp.max(s, axis=-1, keepdims=True), sink)
    if extra is not None:
        m = jnp.maximum(m, extra)
    e = jnp.exp2(s - m)
    den = jnp.sum(e, axis=-1, keepdims=True) + jnp.exp2(sink - m)
    if extra is not None:
        ee = jnp.exp2(extra - m)
        den = den + ee
        return e / den, ee / den
    return e / den, None


def _mix_tail(x, attn, conv_pre, lcg, lcb, w_out_ref, b_out, l1g, l1b):
    conv = _layer_norm(conv_pre, lcg, lcb)
    conv = conv * jax.nn.sigmoid(conv)
    mix = (jnp.dot(attn.astype(BF16), w_out_ref[0:ATTN_DIM, :], preferred_element_type=F32)
           + jnp.dot(conv.astype(BF16), w_out_ref[ATTN_DIM:, :], preferred_element_type=F32) + b_out)
    return _layer_norm(ALPHA * x + mix, l1g, l1b)


def _mix_prompt_kernel(sinks_ref, x_ref, rope_ref, xs_s_ref, *rest, tiles):
    xs_ref = rest[14]
    cnt_ref = rest[16]
    step = pl.program_id(0)
    n_real = pl.num_programs(0) - 1

    @pl.when(step < n_real)
    def _():
        _mix_prompt_tile(sinks_ref, x_ref, rope_ref, *rest, i=step % tiles)

    @pl.when(step == n_real)
    def _():
        xs_ref[0:AREA_ROWS, :] = xs_s_ref[...]
        xs_ref[AREA_ROWS:, :] = jnp.zeros((xs_ref.shape[0] - AREA_ROWS, HALF), U32)
        cnt_ref[...] = jnp.zeros(cnt_ref.shape, F32)


def _mix_prompt_tile(sinks_ref, x_ref, rope_ref, w_in_ref, b_in_ref, w_sw_ref, b_sw_ref, conv_w_ref, conv_b_ref,
                     lcg_ref, lcb_ref, w_out_ref, b_out_ref, l1g_ref, l1b_ref, w_r_ref,
                     h_ref, xs_ref, route_ref, cnt_ref, nk_ref, nv_ref, nc_ref,
                     kprev, vprev, u_scr, u_shift, *, i):
    tq = x_ref.shape[1]
    pre = CONV_PRE
    n_sc = tq // PROMPT_SCOPE

    @pl.when(i == 0)
    def _():
        kprev[...] = jnp.zeros_like(kprev)
        vprev[...] = jnp.zeros_like(vprev)
        u_scr[0:pre, :] = jnp.zeros((pre, CONV_CH), F32)

    x = x_ref[0]
    xb = x.astype(BF16)

    for sc in range(n_sc):
        rows = slice(sc * PROMPT_SCOPE, (sc + 1) * PROMPT_SCOPE)
        u_scr[pre + rows.start:pre + rows.stop, :] = _in_proj_glu(xb[rows], w_in_ref, b_in_ref)
    conv_pre = []
    for sc in range(n_sc):
        base = sc * PROMPT_SCOPE
        lo = 0 if sc == 0 else base + CONV_SHIFT_EXTRA
        hi = base + PROMPT_SCOPE + CONV_SHIFT_EXTRA
        for r in range(1, SUBLANES):
            u_shift[r - 1, lo:hi, :] = u_scr[lo + r:hi + r, :]
        acc = jnp.zeros((PROMPT_SCOPE, CONV_CH), F32)
        for j in range(CONV_WIDTH):
            a, r = divmod(pre - (CONV_WIDTH - 1) + j, SUBLANES)
            start = base + a * SUBLANES
            src = (u_scr[start:start + PROMPT_SCOPE, :] if r == 0
                   else u_shift[r - 1, start:start + PROMPT_SCOPE, :])
            acc = acc + src * conv_w_ref[j:j + 1, :]
        conv_pre.append(acc + conv_b_ref[...])

    nc_ref[0] = u_scr[pre + tq - (CONV_WIDTH - 1):pre + tq, :]
    tail = u_scr[tq:tq + pre, :]
    u_scr[0:pre, :] = tail

    q, k, v = _in_proj_qkv(xb, w_in_ref, b_in_ref, w_sw_ref, b_sw_ref, rope_ref[...])
    qb = q.astype(BF16)
    kb = jnp.concatenate([kprev[...], k], axis=0).astype(BF16)
    vb = jnp.concatenate([vprev[...], v], axis=0).astype(BF16)
    row = lax.broadcasted_iota(jnp.int32, (WINDOW, 2 * WINDOW), 0)
    col = lax.broadcasted_iota(jnp.int32, (WINDOW, 2 * WINDOW), 1)
    dist = row + WINDOW - col
    band = (dist >= 0) & (dist < WINDOW)
    first_col = jnp.where(i > 0, 0, WINDOW)
    attn_blocks = []
    for j in range(tq // WINDOW):
        msk = (band & (col >= first_col)) if j == 0 else band
        heads = [None] * N_HEADS
        for c in range(N_KV_HEADS):
            kc = kb[j * WINDOW:(j + 2) * WINDOW, c * HEAD_DIM:(c + 1) * HEAD_DIM]
            vc = vb[j * WINDOW:(j + 2) * WINDOW, c * HEAD_DIM:(c + 1) * HEAD_DIM]
            qc = jnp.concatenate(
                [qb[j * WINDOW:(j + 1) * WINDOW, (c * Q_GROUP + g) * HEAD_DIM:(c * Q_GROUP + g + 1) * HEAD_DIM]
                 for g in range(Q_GROUP)], axis=0)
            s = lax.dot_general(qc, kc, (((1,), (1,)), ((), ())), preferred_element_type=F32)
            probs = []
            for g in range(Q_GROUP):
                sg = jnp.where(msk, s[g * WINDOW:(g + 1) * WINDOW], NEG_INF)
                p, _ = _softmax_rows_with_sink(sg, sinks_ref[c * Q_GROUP + g] * LOG2E)
                probs.append(p.astype(BF16))
            o = jnp.dot(jnp.concatenate(probs, axis=0), vc, preferred_element_type=F32)
            for g in range(Q_GROUP):
                heads[c * Q_GROUP + g] = o[g * WINDOW:(g + 1) * WINDOW]
        attn_blocks.append(jnp.concatenate(heads, axis=1))
    attn = jnp.concatenate(attn_blocks, axis=0)
    kprev[...] = k[tq - WINDOW:, :]
    vprev[...] = v[tq - WINDOW:, :]
    nk_ref[0] = k[tq - WINDOW:, :]
    nv_ref[0] = v[tq - WINDOW:, :]

    for sc in range(n_sc):
        rows = slice(sc * PROMPT_SCOPE, (sc + 1) * PROMPT_SCOPE)
        h = _mix_tail(x[rows], attn[rows], conv_pre[sc], lcg_ref[...], lcb_ref[...], w_out_ref, b_out_ref[...],
                      l1g_ref[...], l1b_ref[...])
        h_ref[0, rows, :] = h
        route, areas, counts = _route_and_sort(h, w_r_ref[...], PROMPT_SCOPE)
        route_ref[rows, :] = route
        xs_ref[sc * AREA_ROWS:(sc + 1) * AREA_ROWS, :] = areas[0]
        cnt_ref[sc] = jnp.broadcast_to(counts[0], (SUBLANES, LANES))


def _mix_sample_kernel(sinks_ref, x_ref, rope_ref, ck_ref, cv_ref, st_ref, w_in_ref, b_in_ref, w_sw_ref, b_sw_ref,
                       conv_w_ref, conv_b_ref, lcg_ref, lcb_ref, w_out_ref, b_out_ref, l1g_ref, l1b_ref,
                       w_r_ref,
                       h_ref, nk_ref, nv_ref, nc_ref, xs_ref, route_ref, cnt_ref,
                       h_scr):
    tb = x_ref.shape[0]
    n_tok = h_scr.shape[0]
    step = pl.program_id(0)
    x = x_ref[...]
    xb = x.astype(BF16)
    u = _in_proj_glu(xb, w_in_ref, b_in_ref)
    q, k_new, v_new = _in_proj_qkv(xb, w_in_ref, b_in_ref, w_sw_ref, b_sw_ref, rope_ref[...])

    qb = q.astype(BF16)
    knb = k_new.astype(BF16).astype(F32)
    vnb = v_new.astype(BF16).astype(F32)
    gq = SAMPLE_GRP * Q_GROUP
    nkeys = SAMPLE_GRP * WINDOW
    r_i = lax.broadcasted_iota(jnp.int32, (gq, nkeys), 0)
    c_i = lax.broadcasted_iota(jnp.int32, (gq, nkeys), 1)
    msk = ((c_i >> 7) == (r_i & (SAMPLE_GRP - 1))) & ((c_i & (WINDOW - 1)) >= 1)
    r_col = lax.broadcasted_iota(jnp.int32, (gq, 1), 0) >> 3
    attn_groups = []
    for grp in range(tb // SAMPLE_GRP):
        r0 = grp * SAMPLE_GRP
        kc_all = ck_ref[r0:r0 + SAMPLE_GRP].reshape(nkeys, KV_DIM).astype(BF16)
        vc_all = cv_ref[r0:r0 + SAMPLE_GRP].reshape(nkeys, KV_DIM).astype(BF16)
        heads = [None] * N_HEADS
        for c in range(N_KV_HEADS):
            lanes = slice(c * HEAD_DIM, (c + 1) * HEAD_DIM)
            qc = jnp.concatenate(
                [qb[r0:r0 + SAMPLE_GRP, (c * Q_GROUP + g) * HEAD_DIM:(c * Q_GROUP + g + 1) * HEAD_DIM]
                 for g in range(Q_GROUP)], axis=0)
            s = lax.dot_general(qc, kc_all[:, lanes], (((1,), (1,)), ((), ())), preferred_element_type=F32)
            s = jnp.where(msk, s, NEG_INF)
            kn = jnp.concatenate([knb[r0:r0 + SAMPLE_GRP, lanes]] * Q_GROUP, axis=0)
            vn = jnp.concatenate([vnb[r0:r0 + SAMPLE_GRP, lanes]] * Q_GROUP, axis=0)
            s_new = jnp.sum(qc.astype(F32) * kn, axis=-1, keepdims=True)
            sink = jnp.zeros((gq, 1), F32)
            for g in range(Q_GROUP):
                sink = jnp.where(r_col == g, sinks_ref[c * Q_GROUP + g] * LOG2E, sink)
            p, p_new = _softmax_rows_with_sink(s, sink, s_new)
            o = (jnp.dot(p.astype(BF16), vc_all[:, lanes], preferred_element_type=F32)
                 + p_new.astype(BF16).astype(F32) * vn)
            for g in range(Q_GROUP):
                heads[c * Q_GROUP + g] = o[g * SAMPLE_GRP:(g + 1) * SAMPLE_GRP]
        attn_groups.append(jnp.concatenate(heads, axis=1))
    attn = jnp.concatenate(attn_groups, axis=0)

    st = st_ref[...]
    acc = jnp.sum(st * conv_w_ref[0:CONV_WIDTH - 1, :][None], axis=1)
    conv_pre = acc + u * conv_w_ref[CONV_WIDTH - 1:CONV_WIDTH, :] + conv_b_ref[...]

    nk_ref[:, 0:WINDOW - 1, :] = ck_ref[:, 1:WINDOW, :]
    nv_ref[:, 0:WINDOW - 1, :] = cv_ref[:, 1:WINDOW, :]
    nc_ref[:, 0:CONV_WIDTH - 2, :] = st_ref[:, 1:CONV_WIDTH - 1, :]
    for b in range(tb):
        nk_ref[b, WINDOW - 1:WINDOW, :] = k_new[b:b + 1, :]
        nv_ref[b, WINDOW - 1:WINDOW, :] = v_new[b:b + 1, :]
        nc_ref[b, CONV_WIDTH - 2:CONV_WIDTH - 1, :] = u[b:b + 1, :]

    h = _mix_tail(x, attn, conv_pre, lcg_ref[...], lcb_ref[...], w_out_ref, b_out_ref[...],
                  l1g_ref[...], l1b_ref[...])
    h_ref[...] = h
    h_scr[pl.ds(pl.multiple_of(step * tb, tb), tb), :] = h

    @pl.when(step == pl.num_programs(0) - 1)
    def _():
        route, areas, counts = _route_and_sort(h_scr[...], w_r_ref[...], n_tok)
        route_ref[...] = route
        xs_ref[...] = areas[0]
        cnt_ref[0] = jnp.broadcast_to(counts[0], (SUBLANES, LANES))


GATHER_BUFS = 8
SCATTER_BUFS = 8
SCATTER_LAG = 2
ZERO_BITS = (64, 32, 16, 8, 4, 2, 1)
DUMP_CHUNKS = SCATTER_BUFS * BLOCK_CHUNKS
ZERO_ROWS = max(ZERO_BITS[0], DUMP_CHUNKS) * SUBLANES


def _moe_ffn_kernel(src_ref, dst_ref, bstart_ref, used_ref,
                    xs_hbm, w1_ref, w3_ref, w2_ref,
                    ys_hbm,
                    xbuf, ybuf, hbuf, zbuf, w1b, w3b, w2b, sem_in, sem_out, sem_zero, sem_dump):
    e = pl.program_id(0)
    n_areas = used_ref.shape[0]
    total = bstart_ref[N_EXPERTS]
    dump_row0 = n_areas * AREA_ROWS
    lookahead = GATHER_BUFS - 1
    any_rows = pl.ds(0, SUBLANES)

    def chunk_at(ref, idx):
        return pl.ds(pl.multiple_of(ref[idx] * SUBLANES, SUBLANES), SUBLANES)

    def in_copy(rows, sl, c):
        return pltpu.make_async_copy(xs_hbm.at[rows], xbuf.at[sl, pl.ds(c * SUBLANES, SUBLANES)], sem_in.at[sl])

    def out_copy(rows, sl, c):
        return pltpu.make_async_copy(ybuf.at[sl, pl.ds(c * SUBLANES, SUBLANES)], ys_hbm.at[rows], sem_out.at[sl])

    def start_gather(blk):
        for c in range(BLOCK_CHUNKS):
            in_copy(chunk_at(src_ref, blk * BLOCK_CHUNKS + c), blk % GATHER_BUFS, c).start()

    def wait_gather(blk):
        for c in range(BLOCK_CHUNKS):
            in_copy(any_rows, blk % GATHER_BUFS, c).wait()

    def start_scatter(blk):
        for c in range(BLOCK_CHUNKS):
            out_copy(chunk_at(dst_ref, (blk + SCATTER_LAG) * BLOCK_CHUNKS + c), blk % SCATTER_BUFS, c).start()

    def wait_scatter(sl):
        for c in range(BLOCK_CHUNKS):
            out_copy(any_rows, sl, c).wait()

    def load_rows(blk):
        return _unpack_rows(xbuf[blk % GATHER_BUFS])

    def issue_dmas(blk):
        start_scatter(blk - SCATTER_LAG)
        start_gather(blk + lookahead)

    def up_gate(blk, rows):
        x_lo, x_hi = rows
        a1 = (jnp.dot(x_lo, w1b[0:HALF, :], preferred_element_type=F32)
              + jnp.dot(x_hi, w1b[HALF:, :], preferred_element_type=F32))
        a3 = (jnp.dot(x_lo, w3b[0:HALF, :], preferred_element_type=F32)
              + jnp.dot(x_hi, w3b[HALF:, :], preferred_element_type=F32))
        hbuf[blk % SCATTER_BUFS] = (a1 * jax.nn.sigmoid(a1) * a3).astype(BF16)

    def down(blk):
        y = jnp.dot(hbuf[blk % SCATTER_BUFS], w2b[...], preferred_element_type=F32)
        ybuf[blk % SCATTER_BUFS] = _pack_rows(y.astype(BF16).astype(F32))

    def for_each_tail_piece(fn):
        def area_body(s, carry):
            used = used_ref[s]
            tail = AREA_CHUNKS - used
            row = (s * AREA_CHUNKS + used) * SUBLANES
            for bit in ZERO_BITS:
                take = (tail & bit) != 0

                @pl.when(take)
                def _(row=row, bit=bit):
                    fn(pltpu.make_async_copy(zbuf.at[pl.ds(0, bit * SUBLANES)],
                                             ys_hbm.at[pl.ds(pl.multiple_of(row, SUBLANES), bit * SUBLANES)],
                                             sem_zero))
                row = row + jnp.where(take, bit * SUBLANES, 0)
            return carry
        lax.fori_loop(0, n_areas, area_body, 0)

    @pl.when(e == 0)
    def _():
        zbuf[...] = jnp.zeros(zbuf.shape, U32)
        ybuf[...] = jnp.zeros(ybuf.shape, U32)
        dump = pltpu.make_async_copy(zbuf.at[pl.ds(0, DUMP_CHUNKS * SUBLANES)],
                                     ys_hbm.at[pl.ds(dump_row0, DUMP_CHUNKS * SUBLANES)], sem_dump)
        dump.start()
        dump.wait()
        for_each_tail_piece(lambda cp: cp.start())
        for sl in range(SCATTER_BUFS - SCATTER_LAG):
            for c in range(BLOCK_CHUNKS):
                out_copy(pl.ds(dump_row0 + (sl * BLOCK_CHUNKS + c) * SUBLANES, SUBLANES), sl, c).start()
        for k in range(lookahead):
            start_gather(k)

    w1b[...] = w1_ref[0].astype(BF16)
    w3b[...] = w3_ref[0].astype(BF16)
    w2b[...] = w2_ref[0].astype(BF16)
    b0 = bstart_ref[e]
    b1 = bstart_ref[e + 1]

    @pl.when(b1 > b0)
    def _():
        wait_gather(b0)
        rows = load_rows(b0)
        issue_dmas(b0)
        up_gate(b0, rows)

    def block_body(blk, carry):
        wait_gather(blk)
        wait_scatter((blk - 1) % SCATTER_BUFS)
        rows = load_rows(blk)
        issue_dmas(blk)
        down(blk - 1)
        up_gate(blk, rows)
        return carry

    lax.fori_loop(b0 + 1, b1, block_body, 0)

    @pl.when(b1 > b0)
    def _():
        wait_scatter((b1 - 1) % SCATTER_BUFS)
        down(b1 - 1)

    @pl.when(e == N_EXPERTS - 1)
    def _():
        for k in range(SCATTER_LAG):
            start_scatter(total - SCATTER_LAG + k)
        for sl in range(SCATTER_BUFS):
            wait_scatter(sl)
        for k in range(lookahead):
            wait_gather(total + k)
        for_each_tail_piece(lambda cp: cp.wait())


def _moe_combine_kernel(h_ref, route_ref, ys_ref, l2g_ref, l2b_ref, y_ref):
    n = h_ref.shape[0]
    scope = n // (ys_ref.shape[0] // AREA_ROWS)
    s_iota = lax.broadcasted_iota(jnp.int32, (scope, AREA_ROWS), 1).astype(F32)
    for sc in range(n // scope):
        rows = slice(sc * scope, (sc + 1) * scope)
        route = route_ref[rows, :]
        slot1 = route[:, 0:1]
        slot2 = route[:, 1:2]
        gate1 = route[:, 2:3]
        gate2 = route[:, 3:4]
        gmat = (jnp.where(s_iota == slot1, gate1, 0.0) + jnp.where(s_iota == slot2, gate2, 0.0)).astype(BF16)
        y_lo, y_hi = _unpack_rows(ys_ref[sc * AREA_ROWS:(sc + 1) * AREA_ROWS, :])
        f = jnp.concatenate([jnp.dot(gmat, y_lo, preferred_element_type=F32),
                             jnp.dot(gmat, y_hi, preferred_element_type=F32)], axis=1)
        y_ref[rows, :] = _layer_norm(ALPHA * h_ref[rows, :] + f, l2g_ref[...], l2b_ref[...])


def _excl_cumsum(a, axis):
    return jnp.cumsum(a, axis=axis) - a


def _plan_blocks(cnt, q_max):
    n_areas = cnt.shape[0]
    c8 = (cnt + (SUBLANES - 1)) // SUBLANES
    used8 = jnp.sum(c8, axis=1)
    base8 = jnp.arange(n_areas, dtype=jnp.int32) * AREA_CHUNKS
    src_runs = (base8[:, None] + _excl_cumsum(c8, 1)).T
    len_runs = c8.T
    tot = jnp.sum(len_runs, axis=1)
    ptot = (tot + BLOCK_CHUNKS - 1) // BLOCK_CHUNKS * BLOCK_CHUNKS
    pstart = _excl_cumsum(ptot, 0)
    dst_runs = pstart[:, None] + _excl_cumsum(len_runs, 1)
    off = (src_runs - dst_runs).reshape(-1)
    diff = off - jnp.concatenate([jnp.zeros((1,), jnp.int32), off[:-1]])
    q = jnp.arange(q_max, dtype=jnp.int32)
    src = q + jnp.sum(jnp.where(dst_runs.reshape(1, -1) <= q[:, None], diff[None, :], 0), axis=1)
    valid = jnp.any((pstart[None, :] <= q[:, None]) & (q[:, None] < (pstart + tot)[None, :]), axis=1)
    zero_chunk = AREA_CHUNKS - 1
    ahead = jnp.full(((GATHER_BUFS - 1) * BLOCK_CHUNKS,), zero_chunk, jnp.int32)
    src_tab = jnp.concatenate([jnp.where(valid, src, zero_chunk).astype(jnp.int32), ahead])
    qv = jnp.arange(-SCATTER_LAG * BLOCK_CHUNKS, q_max, dtype=jnp.int32)
    dump = n_areas * AREA_CHUNKS + ((qv // BLOCK_CHUNKS) % SCATTER_BUFS) * BLOCK_CHUNKS + qv % BLOCK_CHUNKS
    lead = jnp.zeros((SCATTER_LAG * BLOCK_CHUNKS,), jnp.int32)
    dst_tab = jnp.where(jnp.concatenate([lead > 0, valid]), jnp.concatenate([lead, src]), dump).astype(jnp.int32)
    bstart = jnp.concatenate([pstart, jnp.sum(ptot, keepdims=True)]) // BLOCK_CHUNKS
    return src_tab, dst_tab, bstart.astype(jnp.int32), used8.astype(jnp.int32)


def _rope_tables(pos):
    half = HEAD_DIM // 2
    inv = (ROPE_THETA ** (-np.arange(half, dtype=np.float64) / half)).astype(np.float32)
    ang = (np.asarray(pos, np.float32)[:, None] * inv).astype(np.float64)
    cos = np.cos(ang)
    sin = np.sin(ang)
    reps = LANES // HEAD_DIM
    cos_t = np.concatenate([cos, cos] * reps, axis=1)
    sin_t = np.concatenate([-sin, sin] * reps, axis=1)
    q_scale = ATTN_SCALE * LOG2E
    return jnp.asarray(np.concatenate([cos_t, sin_t, cos_t * q_scale, sin_t * q_scale], axis=1), F32)


def _full(shape):
    nd = len(shape)
    return pl.BlockSpec(shape, lambda *_: (0,) * nd)


def kernel(x_prompt, x_sample, cache_k, cache_v, state_conv, w_in, b_in, sinks, conv_w, conv_b, ln_conv_g,
           ln_conv_b, w_out, b_out, ln1_g, ln1_b, w_router_group, w_router_expert, w1, w3, w2, ln2_g, ln2_b):
    assert w_in.shape[0] == DEPTH
    bsz, seq, _ = x_prompt.shape
    dec_b, dec_t, _ = x_sample.shape
    assert dec_t == 1 and seq % TILE_Q == 0 and dec_b % SAMPLE_TB == 0 and 2 * dec_b + 7 * N_EXPERTS <= AREA_ROWS
    assert (bsz * seq) % COMBINE_TILE == 0
    n_prompt = bsz * seq
    tiles = seq // TILE_Q
    spt = TILE_Q // PROMPT_SCOPE
    p_areas = n_prompt // PROMPT_SCOPE
    n_areas = p_areas + spt
    q_max = (n_areas * AREA_CHUNKS + N_EXPERTS * (BLOCK_CHUNKS - 1) + BLOCK_CHUNKS - 1) // BLOCK_CHUNKS * BLOCK_CHUNKS

    row = lambda a: a.reshape(1, -1)
    qk = ATTN_DIM + KV_DIM
    half = HEAD_DIM // 2

    def swap_halves(a):
        parts = a[..., :qk].reshape(a.shape[:-1] + (qk // HEAD_DIM, 2, half))
        return parts[..., ::-1, :].reshape(a.shape[:-1] + (qk,))

    w_in_b = w_in[0].astype(BF16)
    w_sw_b = swap_halves(w_in[0]).astype(BF16)
    w_out_b = w_out[0].astype(BF16)
    w_r = jnp.concatenate([w_router_group[0], w_router_expert[0].reshape(D_MODEL, N_EXPERTS),
                           jnp.zeros((D_MODEL, LANES - ELANE0 - N_EXPERTS), F32)], axis=1).astype(BF16)
    shared = (w_in_b, row(b_in[0]), w_sw_b, row(swap_halves(b_in[0])), conv_w[0], row(conv_b[0]), row(ln_conv_g[0]), row(ln_conv_b[0]),
              w_out_b, row(b_out[0]), row(ln1_g[0]), row(ln1_b[0]), w_r)
    shared_specs = [_full(a.shape) for a in shared]
    smem = pl.BlockSpec(memory_space=pltpu.SMEM)
    rope_p = _rope_tables(np.arange(seq))
    rope_s = _rope_tables(PAST_LEN + np.arange(dec_t))

    cparams = lambda sem: pltpu.CompilerParams(dimension_semantics=sem, vmem_limit_bytes=VMEM_LIMIT)

    ck = cache_k[0].reshape(dec_b, WINDOW, KV_DIM)
    cv = cache_v[0].reshape(dec_b, WINDOW, KV_DIM)
    st = state_conv[0]
    tb = SAMPLE_TB
    h_s, nk_s, nv_s, nc_s, xs_s, route_s, cnt_s = pl.pallas_call(
        _mix_sample_kernel,
        grid=(dec_b // tb,),
        in_specs=[smem,
                  pl.BlockSpec((tb, D_MODEL), lambda t: (t, 0)),
                  _full(rope_s.shape),
                  pl.BlockSpec((tb, WINDOW, KV_DIM), lambda t: (t, 0, 0)),
                  pl.BlockSpec((tb, WINDOW, KV_DIM), lambda t: (t, 0, 0)),
                  pl.BlockSpec((tb, CONV_WIDTH - 1, CONV_CH), lambda t: (t, 0, 0))] + shared_specs,
        out_specs=[pl.BlockSpec((tb, D_MODEL), lambda t: (t, 0)),
                   pl.BlockSpec((tb, WINDOW, KV_DIM), lambda t: (t, 0, 0)),
                   pl.BlockSpec((tb, WINDOW, KV_DIM), lambda t: (t, 0, 0)),
                   pl.BlockSpec((tb, CONV_WIDTH - 1, CONV_CH), lambda t: (t, 0, 0)),
                   _full((AREA_ROWS, HALF)),
                   _full((dec_b, LANES)),
                   _full((1, SUBLANES, LANES))],
        out_shape=[jax.ShapeDtypeStruct((dec_b, D_MODEL), F32),
                   jax.ShapeDtypeStruct((dec_b, WINDOW, KV_DIM), F32),
                   jax.ShapeDtypeStruct((dec_b, WINDOW, KV_DIM), F32),
                   jax.ShapeDtypeStruct((dec_b, CONV_WIDTH - 1, CONV_CH), F32),
                   jax.ShapeDtypeStruct((AREA_ROWS, HALF), U32),
                   jax.ShapeDtypeStruct((dec_b, LANES), F32),
                   jax.ShapeDtypeStruct((1, SUBLANES, LANES), F32)],
        scratch_shapes=[pltpu.VMEM((dec_b, D_MODEL), F32)],
        compiler_params=cparams(("arbitrary",)),
        name="mix_sample",
    )(sinks[0], x_sample.reshape(dec_b, D_MODEL), rope_s, ck, cv, st, *shared)

    n_real = bsz * tiles
    real = lambda t: jnp.minimum(t, n_real - 1)
    h_p, xs, route_p, cnt_p, nk_p, nv_p, nc_p = pl.pallas_call(
        functools.partial(_mix_prompt_kernel, tiles=tiles),
        grid=(n_real + 1,),
        in_specs=[smem,
                  pl.BlockSpec((1, TILE_Q, D_MODEL), lambda t: (real(t) // tiles, real(t) % tiles, 0)),
                  pl.BlockSpec((TILE_Q, 4 * LANES), lambda t: (real(t) % tiles, 0)),
                  _full((AREA_ROWS, HALF))] + shared_specs,
        out_specs=[pl.BlockSpec((1, TILE_Q, D_MODEL), lambda t: (real(t) // tiles, real(t) % tiles, 0)),
                   pl.BlockSpec((spt * AREA_ROWS, HALF), lambda t: (t, 0)),
                   pl.BlockSpec((TILE_Q, LANES), lambda t: (real(t), 0)),
                   pl.BlockSpec((spt, SUBLANES, LANES), lambda t: (t, 0, 0)),
                   pl.BlockSpec((1, WINDOW, KV_DIM), lambda t: (real(t) // tiles, 0, 0)),
                   pl.BlockSpec((1, WINDOW, KV_DIM), lambda t: (real(t) // tiles, 0, 0)),
                   pl.BlockSpec((1, CONV_WIDTH - 1, CONV_CH), lambda t: (real(t) // tiles, 0, 0))],
        out_shape=[jax.ShapeDtypeStruct((bsz, seq, D_MODEL), F32),
                   jax.ShapeDtypeStruct((n_areas * AREA_ROWS, HALF), U32),
                   jax.ShapeDtypeStruct((n_prompt, LANES), F32),
                   jax.ShapeDtypeStruct((n_areas, SUBLANES, LANES), F32),
                   jax.ShapeDtypeStruct((bsz, WINDOW, KV_DIM), F32),
                   jax.ShapeDtypeStruct((bsz, WINDOW, KV_DIM), F32),
                   jax.ShapeDtypeStruct((bsz, CONV_WIDTH - 1, CONV_CH), F32)],
        scratch_shapes=[pltpu.VMEM((WINDOW, KV_DIM), F32), pltpu.VMEM((WINDOW, KV_DIM), F32),
                        pltpu.VMEM((TILE_Q + CONV_PRE, CONV_CH), F32),
                        pltpu.VMEM((SUBLANES - 1, TILE_Q + CONV_SHIFT_EXTRA, CONV_CH), F32)],
        compiler_params=cparams(("arbitrary",)),
        name="mix_prompt",
    )(sinks[0], x_prompt, rope_p, xs_s, *shared)

    cnt = jnp.concatenate([cnt_p[:p_areas, 0], cnt_s[:, 0], cnt_p[p_areas + 1:, 0]], axis=0)
    cnt = cnt[:, ELANE0:ELANE0 + N_EXPERTS].astype(jnp.int32)
    src_tab, dst_tab, bstart, used8 = _plan_blocks(cnt, q_max)
    w_idx = lambda e, *_: (e, 0, 0)
    ys = pl.pallas_call(
        _moe_ffn_kernel,
        grid_spec=pltpu.PrefetchScalarGridSpec(
            num_scalar_prefetch=4,
            grid=(N_EXPERTS,),
            in_specs=[pl.BlockSpec(memory_space=pl.ANY),
                      pl.BlockSpec((1, D_MODEL, D_FF_EXPERT), w_idx),
                      pl.BlockSpec((1, D_MODEL, D_FF_EXPERT), w_idx),
                      pl.BlockSpec((1, D_FF_EXPERT, D_MODEL), w_idx)],
            out_specs=pl.BlockSpec(memory_space=pl.ANY),
            scratch_shapes=[pltpu.VMEM((GATHER_BUFS, BLOCK_ROWS, HALF), U32),
                            pltpu.VMEM((SCATTER_BUFS, BLOCK_ROWS, HALF), U32),
                            pltpu.VMEM((SCATTER_BUFS, BLOCK_ROWS, D_FF_EXPERT), BF16),
                            pltpu.VMEM((ZERO_ROWS, HALF), U32),
                            pltpu.VMEM((D_MODEL, D_FF_EXPERT), BF16), pltpu.VMEM((D_MODEL, D_FF_EXPERT), BF16),
                            pltpu.VMEM((D_FF_EXPERT, D_MODEL), BF16),
                            pltpu.SemaphoreType.DMA((GATHER_BUFS,)), pltpu.SemaphoreType.DMA((SCATTER_BUFS,)),
                            pltpu.SemaphoreType.DMA(()), pltpu.SemaphoreType.DMA(())]),
        out_shape=jax.ShapeDtypeStruct((n_areas * AREA_ROWS + DUMP_CHUNKS * SUBLANES, HALF), U32),
        compiler_params=cparams(("arbitrary",)),
        name="moe_ffn",
    )(src_tab, dst_tab, bstart, used8, xs, w1[0], w3[0], w2[0])

    def combine(h2d, route, first_area, scope, tile):
        n = h2d.shape[0]
        apt = tile // scope
        return pl.pallas_call(
            _moe_combine_kernel,
            grid=(n // tile,),
            in_specs=[pl.BlockSpec((tile, D_MODEL), lambda t: (t, 0)),
                      pl.BlockSpec((tile, LANES), lambda t: (t, 0)),
                      pl.BlockSpec((apt * AREA_ROWS, HALF), lambda t: (first_area // apt + t, 0)),
                      _full((1, D_MODEL)), _full((1, D_MODEL))],
            out_specs=pl.BlockSpec((tile, D_MODEL), lambda t: (t, 0)),
            out_shape=jax.ShapeDtypeStruct((n, D_MODEL), F32),
            compiler_params=cparams(("arbitrary",)),
            name="moe_combine",
        )(h2d, route, ys, row(ln2_g[0]), row(ln2_b[0]))

    y_p = combine(h_p.reshape(n_prompt, D_MODEL), route_p, 0, PROMPT_SCOPE, COMBINE_TILE)
    y_s = combine(h_s, route_s, p_areas, dec_b, dec_b)

    kv_shape = lambda n: (DEPTH, n, WINDOW, N_KV_HEADS, HEAD_DIM)
    return (y_p.reshape(bsz, seq, D_MODEL), y_s.reshape(dec_b, dec_t, D_MODEL),
            nk_p.reshape(kv_shape(bsz)), nv_p.reshape(kv_shape(bsz)), nc_p[None],
            nk_s.reshape(kv_shape(dec_b)), nv_s.reshape(kv_shape(dec_b)), nc_s[None])
```

```python
import functools

import jax
import jax.numpy as jnp
import numpy as np
from jax import lax
from jax.experimental import pallas as pl
from jax.experimental.pallas import tpu as pltpu

F32 = jnp.float32
BF16 = jnp.bfloat16
U32 = jnp.uint32

D_MODEL = 1024
N_HEADS = 8
N_KV_HEADS = 2
HEAD_DIM = 64
Q_GROUP = N_HEADS // N_KV_HEADS
ATTN_DIM = N_HEADS * HEAD_DIM
KV_DIM = N_KV_HEADS * HEAD_DIM
WINDOW = 128
ROPE_THETA = 10000.0
ATTN_SCALE = HEAD_DIM ** -0.5
CONV_CH = D_MODEL - ATTN_DIM
CONV_WIDTH = 31
IN_DIM = ATTN_DIM + 2 * KV_DIM + 2 * CONV_CH
N_GROUPS = 4
EXPERTS_PER_GROUP = 8
N_EXPERTS = N_GROUPS * EXPERTS_PER_GROUP
D_FF_EXPERT = 256
LN_EPS = 1e-5
NEG_INF = -1e30
DEPTH = 1
ALPHA = (2.0 * DEPTH) ** 0.25
PAST_LEN = 16384
LOG2E = 1.4426950408889634

LANES = 128
SUBLANES = 8
HALF = D_MODEL // 2
ELANE0 = N_GROUPS
PROMPT_SCOPE = 256
AREA_ROWS = 768
AREA_CHUNKS = AREA_ROWS // SUBLANES
BLOCK_CHUNKS = 32
BLOCK_ROWS = BLOCK_CHUNKS * SUBLANES
TILE_Q = 512
COMBINE_TILE = 1024
CONV_PRE = 32
CONV_SHIFT_EXTRA = 24
SAMPLE_TB = 32
SAMPLE_GRP = 8
VMEM_LIMIT = 56 * 1024 * 1024


def _layer_norm(x, g, b):
    mu = jnp.mean(x, axis=-1, keepdims=True)
    xc = x - mu
    var = jnp.mean(xc * xc, axis=-1, keepdims=True)
    return xc * lax.rsqrt(var + LN_EPS) * g + b


def _rope(x, x_swapped, cos, sin_signed):
    pieces = [x[:, g * LANES:(g + 1) * LANES] * cos + x_swapped[:, g * LANES:(g + 1) * LANES] * sin_signed
              for g in range(x.shape[-1] // LANES)]
    return pieces[0] if len(pieces) == 1 else jnp.concatenate(pieces, axis=-1)


def _in_proj_glu(xb, w_in_ref, b_in_ref):
    lo, hi = ATTN_DIM + 2 * KV_DIM, IN_DIM
    cacg = jnp.dot(xb, w_in_ref[:, lo:hi], preferred_element_type=F32) + b_in_ref[:, lo:hi]
    return cacg[:, :CONV_CH] * jax.nn.sigmoid(cacg[:, CONV_CH:])


def _in_proj_qkv(xb, w_in_ref, b_in_ref, w_sw_ref, b_sw_ref, rope):
    o_v, o_e = ATTN_DIM + KV_DIM, ATTN_DIM + 2 * KV_DIM
    qkv = jnp.dot(xb, w_in_ref[:, 0:o_e], preferred_element_type=F32) + b_in_ref[:, 0:o_e]
    sw = jnp.dot(xb, w_sw_ref[...], preferred_element_type=F32) + b_sw_ref[...]
    k = _rope(qkv[:, ATTN_DIM:o_v], sw[:, ATTN_DIM:], rope[:, 0:LANES], rope[:, LANES:2 * LANES])
    q = _rope(qkv[:, :ATTN_DIM], sw[:, :ATTN_DIM], rope[:, 2 * LANES:3 * LANES], rope[:, 3 * LANES:])
    return q, k, qkv[:, o_v:]


def _pack_rows(v):
    lo = lax.bitcast_convert_type(v[:, :HALF], U32)
    hi = lax.bitcast_convert_type(v[:, HALF:], U32)
    return (lo >> 16) | (hi & jnp.uint32(0xFFFF0000))


def _unpack_rows(w):
    lo = lax.bitcast_convert_type(w << 16, F32).astype(BF16)
    hi = lax.bitcast_convert_type(w & jnp.uint32(0xFFFF0000), F32).astype(BF16)
    return lo, hi


def _route_and_sort(h, w_r, scope):
    n = h.shape[0]
    hb = h.astype(BF16)
    logits = jnp.dot(hb, w_r, preferred_element_type=F32)
    lane = lax.broadcasted_iota(jnp.int32, (n, LANES), 1)
    lane_f = lane.astype(F32)
    is_g = lane < N_GROUPS
    lg = jnp.where(is_g, logits, -jnp.inf)
    gmax = jnp.max(lg, axis=-1, keepdims=True)
    gidx = jnp.min(jnp.where(lg == gmax, lane_f, float(LANES)), axis=-1, keepdims=True)
    p_g = 1.0 / jnp.sum(jnp.where(is_g, jnp.exp(logits - gmax), 0.0), axis=-1, keepdims=True)
    egrp = ((lane - ELANE0) >> 3).astype(F32)
    emask = (lane >= ELANE0) & (lane < ELANE0 + N_EXPERTS) & (egrp == gidx)
    sel = jnp.where(emask, logits, -jnp.inf)
    v1 = jnp.max(sel, axis=-1, keepdims=True)
    i1 = jnp.min(jnp.where(sel == v1, lane_f, float(LANES)), axis=-1, keepdims=True)
    sel2 = jnp.where(lane_f == i1, -jnp.inf, sel)
    v2 = jnp.max(sel2, axis=-1, keepdims=True)
    i2 = jnp.min(jnp.where(sel2 == v2, lane_f, float(LANES)), axis=-1, keepdims=True)
    t = jnp.exp(v2 - v1)
    gate1 = p_g / (1.0 + t)
    gate2 = p_g * t / (1.0 + t)
    pick1 = lane_f == i1
    pick2 = lane_f == i2
    member = jnp.where(pick1 | pick2, 1.0, 0.0)

    r_i = lax.broadcasted_iota(jnp.int32, (scope, scope), 0)
    c_i = lax.broadcasted_iota(jnp.int32, (scope, scope), 1)
    lower = jnp.where(c_i < r_i, 1.0, 0.0).astype(BF16)
    e_r = lax.broadcasted_iota(jnp.int32, (LANES, LANES), 0)
    e_c = lax.broadcasted_iota(jnp.int32, (LANES, LANES), 1)
    upper = jnp.where(e_r < e_c, 1.0, 0.0).astype(BF16)
    s_iota = lax.broadcasted_iota(jnp.int32, (scope, AREA_ROWS), 1).astype(F32)

    slots1, slots2, areas, counts = [], [], [], []
    for sc in range(n // scope):
        rows = slice(sc * scope, (sc + 1) * scope)
        m_sc = member[rows]
        before = jnp.dot(lower, m_sc.astype(BF16), preferred_element_type=F32)
        cnt = jnp.sum(m_sc, axis=0, keepdims=True)
        c8 = jnp.floor((cnt + (SUBLANES - 1)) * (1.0 / SUBLANES))
        c8b = jnp.broadcast_to(c8, (SUBLANES, LANES)).astype(BF16)
        off8 = jnp.dot(c8b, upper, preferred_element_type=F32)[0:1]
        slot_all = before + off8 * float(SUBLANES)
        s1 = jnp.sum(jnp.where(pick1[rows], slot_all, 0.0), axis=-1, keepdims=True)
        s2 = jnp.sum(jnp.where(pick2[rows], slot_all, 0.0), axis=-1, keepdims=True)
        perm_t = jnp.where((s_iota == s1) | (s_iota == s2), 1.0, 0.0).astype(BF16)
        sorted_rows = lax.dot_general(perm_t, hb[rows], (((0,), (0,)), ((), ())),
                                      preferred_element_type=F32)
        areas.append(_pack_rows(sorted_rows))
        counts.append(cnt)
        slots1.append(s1)
        slots2.append(s2)
    slot1 = jnp.concatenate(slots1, axis=0) if len(slots1) > 1 else slots1[0]
    slot2 = jnp.concatenate(slots2, axis=0) if len(slots2) > 1 else slots2[0]
    route = jnp.where(lane == 0, slot1, jnp.where(lane == 1, slot2, jnp.where(lane == 2, gate1,
                      jnp.where(lane == 3, gate2, 0.0))))
    return route, areas, counts


def _softmax_rows_with_sink(s, sink, extra=None):
    m = jnp.maximum(jnp.max(s, axis=-1, keepdims=True), sink)
    if extra is not None:
        m = jnp.maximum(m, extra)
    e = jnp.exp2(s - m)
    den = jnp.sum(e, axis=-1, keepdims=True) + jnp.exp2(sink - m)
    if extra is not None:
        ee = jnp.exp2(extra - m)
        den = den + ee
        return e / den, ee / den
    return e / den, None


def _mix_tail(x, attn, conv_pre, lcg, lcb, w_out_ref, b_out, l1g, l1b):
    conv = _layer_norm(conv_pre, lcg, lcb)
    conv = conv * jax.nn.sigmoid(conv)
    mix = (jnp.dot(attn.astype(BF16), w_out_ref[0:ATTN_DIM, :], preferred_element_type=F32)
           + jnp.dot(conv.astype(BF16), w_out_ref[ATTN_DIM:, :], preferred_element_type=F32) + b_out)
    return _layer_norm(ALPHA * x + mix, l1g, l1b)


def _mix_prompt_kernel(sinks_ref, x_ref, rope_ref, xs_s_ref, *rest, tiles):
    xs_ref = rest[16]
    cnt_ref = rest[18]
    step = pl.program_id(0)
    n_real = pl.num_programs(0) - 1

    @pl.when(step < n_real)
    def _():
        _mix_prompt_tile(sinks_ref, x_ref, rope_ref, *rest, i=step % tiles)

    @pl.when(step == n_real)
    def _():
        xs_ref[0:AREA_ROWS, :] = xs_s_ref[...]
        xs_ref[AREA_ROWS:, :] = jnp.zeros((xs_ref.shape[0] - AREA_ROWS, HALF), U32)
        cnt_ref[...] = jnp.zeros(cnt_ref.shape, F32)


def _mix_prompt_tile(sinks_ref, x_ref, rope_ref, w_glu_ref, b_glu_ref,
                     w_in_ref, b_in_ref, w_sw_ref, b_sw_ref, conv_w_ref, conv_b_ref,
                     lcg_ref, lcb_ref, w_out_ref, b_out_ref, l1g_ref, l1b_ref, w_r_ref,
                     h_ref, xs_ref, route_ref, cnt_ref, nk_ref, nv_ref, nc_ref,
                     kprev, vprev, u_scr, u_shift, *, i):
    tq = x_ref.shape[1]
    pre = CONV_PRE
    n_sc = tq // PROMPT_SCOPE

    @pl.when(i == 0)
    def _():
        kprev[...] = jnp.zeros_like(kprev)
        vprev[...] = jnp.zeros_like(vprev)
        u_scr[0:pre, :] = jnp.zeros((pre, CONV_CH), F32)

    x = x_ref[0]
    xb = x.astype(BF16)

    def glu_block(cb):
        cols = slice(2 * LANES * cb, 2 * LANES * (cb + 1))
        vg = jnp.dot(xb, w_glu_ref[:, cols], preferred_element_type=F32) + b_glu_ref[:, cols]
        return vg[:, :LANES] * jax.nn.sigmoid(vg[:, LANES:])

    n_cb = CONV_CH // LANES
    acc_blocks = [[None] * n_cb for _ in range(n_sc)]
    u_next = glu_block(0)
    for cb in range(n_cb):
        lanes = slice(LANES * cb, LANES * (cb + 1))
        u_scr[pre:pre + tq, lanes] = u_next
        if cb + 1 < n_cb:
            u_next = glu_block(cb + 1)
        for r in range(1, SUBLANES):
            u_shift[r - 1, :, lanes] = u_scr[r:r + tq + CONV_SHIFT_EXTRA, lanes]
        for sc in range(n_sc):
            base = sc * PROMPT_SCOPE
            acc = jnp.zeros((PROMPT_SCOPE, LANES), F32)
            for j in range(CONV_WIDTH):
                a, r = divmod(pre - (CONV_WIDTH - 1) + j, SUBLANES)
                start = base + a * SUBLANES
                src = (u_scr[start:start + PROMPT_SCOPE, lanes] if r == 0
                       else u_shift[r - 1, start:start + PROMPT_SCOPE, lanes])
                acc = acc + src * conv_w_ref[j:j + 1, lanes]
            acc_blocks[sc][cb] = acc
    conv_pre = [jnp.concatenate(blocks, axis=1) + conv_b_ref[...] for blocks in acc_blocks]

    nc_ref[0] = u_scr[pre + tq - (CONV_WIDTH - 1):pre + tq, :]
    tail = u_scr[tq:tq + pre, :]
    u_scr[0:pre, :] = tail

    q, k, v = _in_proj_qkv(xb, w_in_ref, b_in_ref, w_sw_ref, b_sw_ref, rope_ref[...])
    qb = q.astype(BF16)
    kb = jnp.concatenate([kprev[...], k], axis=0).astype(BF16)
    vb = jnp.concatenate([vprev[...], v], axis=0).astype(BF16)
    row = lax.broadcasted_iota(jnp.int32, (WINDOW, 2 * WINDOW), 0)
    col = lax.broadcasted_iota(jnp.int32, (WINDOW, 2 * WINDOW), 1)
    dist = row + WINDOW - col
    band = (dist >= 0) & (dist < WINDOW)
    first_col = jnp.where(i > 0, 0, WINDOW)
    attn_blocks = []
    for j in range(tq // WINDOW):
        msk = (band & (col >= first_col)) if j == 0 else band
        heads = [None] * N_HEADS
        for c in range(N_KV_HEADS):
            kc = kb[j * WINDOW:(j + 2) * WINDOW, c * HEAD_DIM:(c + 1) * HEAD_DIM]
            vc = vb[j * WINDOW:(j + 2) * WINDOW, c * HEAD_DIM:(c + 1) * HEAD_DIM]
            qc = jnp.concatenate(
                [qb[j * WINDOW:(j + 1) * WINDOW, (c * Q_GROUP + g) * HEAD_DIM:(c * Q_GROUP + g + 1) * HEAD_DIM]
                 for g in range(Q_GROUP)], axis=0)
            s = lax.dot_general(qc, kc, (((1,), (1,)), ((), ())), preferred_element_type=F32)
            probs = []
            for g in range(Q_GROUP):
                sg = jnp.where(msk, s[g * WINDOW:(g + 1) * WINDOW], NEG_INF)
                p, _ = _softmax_rows_with_sink(sg, sinks_ref[c * Q_GROUP + g] * LOG2E)
                probs.append(p.astype(BF16))
            o = jnp.dot(jnp.concatenate(probs, axis=0), vc, preferred_element_type=F32)
            for g in range(Q_GROUP):
                heads[c * Q_GROUP + g] = o[g * WINDOW:(g + 1) * WINDOW]
        attn_blocks.append(jnp.concatenate(heads, axis=1))
    attn = jnp.concatenate(attn_blocks, axis=0)
    kprev[...] = k[tq - WINDOW:, :]
    vprev[...] = v[tq - WINDOW:, :]
    nk_ref[0] = k[tq - WINDOW:, :]
    nv_ref[0] = v[tq - WINDOW:, :]

    for sc in range(n_sc):
        rows = slice(sc * PROMPT_SCOPE, (sc + 1) * PROMPT_SCOPE)
        h = _mix_tail(x[rows], attn[rows], conv_pre[sc], lcg_ref[...], lcb_ref[...], w_out_ref, b_out_ref[...],
                      l1g_ref[...], l1b_ref[...])
        h_ref[0, rows, :] = h
        route, areas, counts = _route_and_sort(h, w_r_ref[...], PROMPT_SCOPE)
        route_ref[rows, :] = route
        xs_ref[sc * AREA_ROWS:(sc + 1) * AREA_ROWS, :] = areas[0]
        cnt_ref[sc] = jnp.broadcast_to(counts[0], (SUBLANES, LANES))


def _mix_sample_kernel(sinks_ref, x_ref, rope_ref, ck_ref, cv_ref, st_ref, w_in_ref, b_in_ref, w_sw_ref, b_sw_ref,
                       conv_w_ref, conv_b_ref, lcg_ref, lcb_ref, w_out_ref, b_out_ref, l1g_ref, l1b_ref,
                       w_r_ref,
                       h_ref, nk_ref, nv_ref, nc_ref, xs_ref, route_ref, cnt_ref,
                       h_scr):
    tb = x_ref.shape[0]
    n_tok = h_scr.shape[0]
    step = pl.program_id(0)
    x = x_ref[...]
    xb = x.astype(BF16)
    u = _in_proj_glu(xb, w_in_ref, b_in_ref)
    q, k_new, v_new = _in_proj_qkv(xb, w_in_ref, b_in_ref, w_sw_ref, b_sw_ref, rope_ref[...])

    qb = q.astype(BF16)
    knb = k_new.astype(BF16).astype(F32)
    vnb = v_new.astype(BF16).astype(F32)
    gq = SAMPLE_GRP * Q_GROUP
    nkeys = SAMPLE_GRP * WINDOW
    r_i = lax.broadcasted_iota(jnp.int32, (gq, nkeys), 0)
    c_i = lax.broadcasted_iota(jnp.int32, (gq, nkeys), 1)
    msk = ((c_i >> 7) == (r_i & (SAMPLE_GRP - 1))) & ((c_i & (WINDOW - 1)) >= 1)
    r_col = lax.broadcasted_iota(jnp.int32, (gq, 1), 0) >> 3
    attn_groups = []
    for grp in range(tb // SAMPLE_GRP):
        r0 = grp * SAMPLE_GRP
        kc_all = ck_ref[r0:r0 + SAMPLE_GRP].reshape(nkeys, KV_DIM).astype(BF16)
        vc_all = cv_ref[r0:r0 + SAMPLE_GRP].reshape(nkeys, KV_DIM).astype(BF16)
        heads = [None] * N_HEADS
        for c in range(N_KV_HEADS):
            lanes = slice(c * HEAD_DIM, (c + 1) * HEAD_DIM)
            qc = jnp.concatenate(
                [qb[r0:r0 + SAMPLE_GRP, (c * Q_GROUP + g) * HEAD_DIM:(c * Q_GROUP + g + 1) * HEAD_DIM]
                 for g in range(Q_GROUP)], axis=0)
            s = lax.dot_general(qc, kc_all[:, lanes], (((1,), (1,)), ((), ())), preferred_element_type=F32)
            s = jnp.where(msk, s, NEG_INF)
            kn = jnp.concatenate([knb[r0:r0 + SAMPLE_GRP, lanes]] * Q_GROUP, axis=0)
            vn = jnp.concatenate([vnb[r0:r0 + SAMPLE_GRP, lanes]] * Q_GROUP, axis=0)
            s_new = jnp.sum(qc.astype(F32) * kn, axis=-1, keepdims=True)
            sink = jnp.zeros((gq, 1), F32)
            for g in range(Q_GROUP):
                sink = jnp.where(r_col == g, sinks_ref[c * Q_GROUP + g] * LOG2E, sink)
            p, p_new = _softmax_rows_with_sink(s, sink, s_new)
            o = (jnp.dot(p.astype(BF16), vc_all[:, lanes], preferred_element_type=F32)
                 + p_new.astype(BF16).astype(F32) * vn)
            for g in range(Q_GROUP):
                heads[c * Q_GROUP + g] = o[g * SAMPLE_GRP:(g + 1) * SAMPLE_GRP]
        attn_groups.append(jnp.concatenate(heads, axis=1))
    attn = jnp.concatenate(attn_groups, axis=0)

    st = st_ref[...]
    acc = jnp.sum(st * conv_w_ref[0:CONV_WIDTH - 1, :][None], axis=1)
    conv_pre = acc + u * conv_w_ref[CONV_WIDTH - 1:CONV_WIDTH, :] + conv_b_ref[...]

    nk_ref[:, 0:WINDOW - 1, :] = ck_ref[:, 1:WINDOW, :]
    nv_ref[:, 0:WINDOW - 1, :] = cv_ref[:, 1:WINDOW, :]
    nc_ref[:, 0:CONV_WIDTH - 2, :] = st_ref[:, 1:CONV_WIDTH - 1, :]
    for b in range(tb):
        nk_ref[b, WINDOW - 1:WINDOW, :] = k_new[b:b + 1, :]
        nv_ref[b, WINDOW - 1:WINDOW, :] = v_new[b:b + 1, :]
        nc_ref[b, CONV_WIDTH - 2:CONV_WIDTH - 1, :] = u[b:b + 1, :]

    h = _mix_tail(x, attn, conv_pre, lcg_ref[...], lcb_ref[...], w_out_ref, b_out_ref[...],
                  l1g_ref[...], l1b_ref[...])
    h_ref[...] = h
    h_scr[pl.ds(pl.multiple_of(step * tb, tb), tb), :] = h

    @pl.when(step == pl.num_programs(0) - 1)
    def _():
        route, areas, counts = _route_and_sort(h_scr[...], w_r_ref[...], n_tok)
        route_ref[...] = route
        xs_ref[...] = areas[0]
        cnt_ref[0] = jnp.broadcast_to(counts[0], (SUBLANES, LANES))


GATHER_BUFS = 8
SCATTER_BUFS = 8
SCATTER_LAG = 2
ZERO_BITS = (64, 32, 16, 8, 4, 2, 1)
DUMP_CHUNKS = SCATTER_BUFS * BLOCK_CHUNKS
ZERO_ROWS = max(ZERO_BITS[0], DUMP_CHUNKS) * SUBLANES


def _moe_ffn_kernel(src_ref, dst_ref, bstart_ref, used_ref,
                    xs_hbm, w1_ref, w3_ref, w2_ref,
                    ys_hbm,
                    xbuf, ybuf, hbuf, zbuf, w1b, w3b, w2b, sem_in, sem_out, sem_zero, sem_dump):
    e = pl.program_id(0)
    n_areas = used_ref.shape[0]
    total = bstart_ref[N_EXPERTS]
    dump_row0 = n_areas * AREA_ROWS
    lookahead = GATHER_BUFS - 1
    any_rows = pl.ds(0, SUBLANES)

    def chunk_at(ref, idx):
        return pl.ds(pl.multiple_of(ref[idx] * SUBLANES, SUBLANES), SUBLANES)

    def in_copy(rows, sl, c):
        return pltpu.make_async_copy(xs_hbm.at[rows], xbuf.at[sl, pl.ds(c * SUBLANES, SUBLANES)], sem_in.at[sl])

    def out_copy(rows, sl, c):
        return pltpu.make_async_copy(ybuf.at[sl, pl.ds(c * SUBLANES, SUBLANES)], ys_hbm.at[rows], sem_out.at[sl])

    def start_gather(blk):
        for c in range(BLOCK_CHUNKS):
            in_copy(chunk_at(src_ref, blk * BLOCK_CHUNKS + c), blk % GATHER_BUFS, c).start()

    def wait_gather(blk):
        for c in range(BLOCK_CHUNKS):
            in_copy(any_rows, blk % GATHER_BUFS, c).wait()

    def start_scatter(blk):
        for c in range(BLOCK_CHUNKS):
            out_copy(chunk_at(dst_ref, (blk + SCATTER_LAG) * BLOCK_CHUNKS + c), blk % SCATTER_BUFS, c).start()

    def wait_scatter(sl):
        for c in range(BLOCK_CHUNKS):
            out_copy(any_rows, sl, c).wait()

    def load_rows(blk):
        return _unpack_rows(xbuf[blk % GATHER_BUFS])

    def issue_dmas(blk):
        start_scatter(blk - SCATTER_LAG)
        start_gather(blk + lookahead)

    def up_gate(blk, rows):
        x_lo, x_hi = rows
        a1 = (jnp.dot(x_lo, w1b[0:HALF, :], preferred_element_type=F32)
              + jnp.dot(x_hi, w1b[HALF:, :], preferred_element_type=F32))
        a3 = (jnp.dot(x_lo, w3b[0:HALF, :], preferred_element_type=F32)
              + jnp.dot(x_hi, w3b[HALF:, :], preferred_element_type=F32))
        hbuf[blk % SCATTER_BUFS] = (a1 * jax.nn.sigmoid(a1) * a3).astype(BF16)

    def down(blk):
        y = jnp.dot(hbuf[blk % SCATTER_BUFS], w2b[...], preferred_element_type=F32)
        ybuf[blk % SCATTER_BUFS] = _pack_rows(y.astype(BF16).astype(F32))

    def for_each_tail_piece(fn):
        def area_body(s, carry):
            used = used_ref[s]
            tail = AREA_CHUNKS - used
            row = (s * AREA_CHUNKS + used) * SUBLANES
            for bit in ZERO_BITS:
                take = (tail & bit) != 0

                @pl.when(take)
                def _(row=row, bit=bit):
                    fn(pltpu.make_async_copy(zbuf.at[pl.ds(0, bit * SUBLANES)],
                                             ys_hbm.at[pl.ds(pl.multiple_of(row, SUBLANES), bit * SUBLANES)],
                                             sem_zero))
                row = row + jnp.where(take, bit * SUBLANES, 0)
            return carry
        lax.fori_loop(0, n_areas, area_body, 0)

    @pl.when(e == 0)
    def _():
        zbuf[...] = jnp.zeros(zbuf.shape, U32)
        ybuf[...] = jnp.zeros(ybuf.shape, U32)
        dump = pltpu.make_async_copy(zbuf.at[pl.ds(0, DUMP_CHUNKS * SUBLANES)],
                                     ys_hbm.at[pl.ds(dump_row0, DUMP_CHUNKS * SUBLANES)], sem_dump)
        dump.start()
        dump.wait()
        for_each_tail_piece(lambda cp: cp.start())
        for sl in range(SCATTER_BUFS - SCATTER_LAG):
            for c in range(BLOCK_CHUNKS):
                out_copy(pl.ds(dump_row0 + (sl * BLOCK_CHUNKS + c) * SUBLANES, SUBLANES), sl, c).start()
        for k in range(lookahead):
            start_gather(k)

    w1b[...] = w1_ref[0].astype(BF16)
    w3b[...] = w3_ref[0].astype(BF16)
    w2b[...] = w2_ref[0].astype(BF16)
    b0 = bstart_ref[e]
    b1 = bstart_ref[e + 1]

    @pl.when(b1 > b0)
    def _():
        wait_gather(b0)
        rows = load_rows(b0)
        issue_dmas(b0)
        up_gate(b0, rows)

    def block_body(blk, carry):
        wait_gather(blk)
        wait_scatter((blk - 1) % SCATTER_BUFS)
        rows = load_rows(blk)
        issue_dmas(blk)
        down(blk - 1)
        up_gate(blk, rows)
        return carry

    lax.fori_loop(b0 + 1, b1, block_body, 0)

    @pl.when(b1 > b0)
    def _():
        wait_scatter((b1 - 1) % SCATTER_BUFS)
        down(b1 - 1)

    @pl.when(e == N_EXPERTS - 1)
    def _():
        for k in range(SCATTER_LAG):
            start_scatter(total - SCATTER_LAG + k)
        for sl in range(SCATTER_BUFS):
            wait_scatter(sl)
        for k in range(lookahead):
            wait_gather(total + k)
        for_each_tail_piece(lambda cp: cp.wait())


def _moe_combine_kernel(h_ref, route_ref, ys_ref, l2g_ref, l2b_ref, y_ref):
    n = h_ref.shape[0]
    scope = n // (ys_ref.shape[0] // AREA_ROWS)
    s_iota = lax.broadcasted_iota(jnp.int32, (scope, AREA_ROWS), 1).astype(F32)
    for sc in range(n // scope):
        rows = slice(sc * scope, (sc + 1) * scope)
        route = route_ref[rows, :]
        slot1 = route[:, 0:1]
        slot2 = route[:, 1:2]
        gate1 = route[:, 2:3]
        gate2 = route[:, 3:4]
        gmat = (jnp.where(s_iota == slot1, gate1, 0.0) + jnp.where(s_iota == slot2, gate2, 0.0)).astype(BF16)
        y_lo, y_hi = _unpack_rows(ys_ref[sc * AREA_ROWS:(sc + 1) * AREA_ROWS, :])
        f = jnp.concatenate([jnp.dot(gmat, y_lo, preferred_element_type=F32),
                             jnp.dot(gmat, y_hi, preferred_element_type=F32)], axis=1)
        y_ref[rows, :] = _layer_norm(ALPHA * h_ref[rows, :] + f, l2g_ref[...], l2b_ref[...])


def _excl_cumsum(a, axis):
    return jnp.cumsum(a, axis=axis) - a


def _plan_blocks(cnt, q_max):
    n_areas = cnt.shape[0]
    c8 = (cnt + (SUBLANES - 1)) // SUBLANES
    used8 = jnp.sum(c8, axis=1)
    base8 = jnp.arange(n_areas, dtype=jnp.int32) * AREA_CHUNKS
    src_runs = (base8[:, None] + _excl_cumsum(c8, 1)).T
    len_runs = c8.T
    tot = jnp.sum(len_runs, axis=1)
    ptot = (tot + BLOCK_CHUNKS - 1) // BLOCK_CHUNKS * BLOCK_CHUNKS
    pstart = _excl_cumsum(ptot, 0)
    dst_runs = pstart[:, None] + _excl_cumsum(len_runs, 1)
    off = (src_runs - dst_runs).reshape(-1)
    diff = off - jnp.concatenate([jnp.zeros((1,), jnp.int32), off[:-1]])
    q = jnp.arange(q_max, dtype=jnp.int32)
    src = q + jnp.sum(jnp.where(dst_runs.reshape(1, -1) <= q[:, None], diff[None, :], 0), axis=1)
    valid = jnp.any((pstart[None, :] <= q[:, None]) & (q[:, None] < (pstart + tot)[None, :]), axis=1)
    zero_chunk = AREA_CHUNKS - 1
    ahead = jnp.full(((GATHER_BUFS - 1) * BLOCK_CHUNKS,), zero_chunk, jnp.int32)
    src_tab = jnp.concatenate([jnp.where(valid, src, zero_chunk).astype(jnp.int32), ahead])
    qv = jnp.arange(-SCATTER_LAG * BLOCK_CHUNKS, q_max, dtype=jnp.int32)
    dump = n_areas * AREA_CHUNKS + ((qv // BLOCK_CHUNKS) % SCATTER_BUFS) * BLOCK_CHUNKS + qv % BLOCK_CHUNKS
    lead = jnp.zeros((SCATTER_LAG * BLOCK_CHUNKS,), jnp.int32)
    dst_tab = jnp.where(jnp.concatenate([lead > 0, valid]), jnp.concatenate([lead, src]), dump).astype(jnp.int32)
    bstart = jnp.concatenate([pstart, jnp.sum(ptot, keepdims=True)]) // BLOCK_CHUNKS
    return src_tab, dst_tab, bstart.astype(jnp.int32), used8.astype(jnp.int32)


def _rope_tables(pos):
    half = HEAD_DIM // 2
    inv = (ROPE_THETA ** (-np.arange(half, dtype=np.float64) / half)).astype(np.float32)
    ang = (np.asarray(pos, np.float32)[:, None] * inv).astype(np.float64)
    cos = np.cos(ang)
    sin = np.sin(ang)
    reps = LANES // HEAD_DIM
    cos_t = np.concatenate([cos, cos] * reps, axis=1)
    sin_t = np.concatenate([-sin, sin] * reps, axis=1)
    q_scale = ATTN_SCALE * LOG2E
    return jnp.asarray(np.concatenate([cos_t, sin_t, cos_t * q_scale, sin_t * q_scale], axis=1), F32)


def _full(shape):
    nd = len(shape)
    return pl.BlockSpec(shape, lambda *_: (0,) * nd)


def kernel(x_prompt, x_sample, cache_k, cache_v, state_conv, w_in, b_in, sinks, conv_w, conv_b, ln_conv_g,
           ln_conv_b, w_out, b_out, ln1_g, ln1_b, w_router_group, w_router_expert, w1, w3, w2, ln2_g, ln2_b):
    assert w_in.shape[0] == DEPTH
    bsz, seq, _ = x_prompt.shape
    dec_b, dec_t, _ = x_sample.shape
    assert dec_t == 1 and seq % TILE_Q == 0 and dec_b % SAMPLE_TB == 0 and 2 * dec_b + 7 * N_EXPERTS <= AREA_ROWS
    assert (bsz * seq) % COMBINE_TILE == 0
    n_prompt = bsz * seq
    tiles = seq // TILE_Q
    spt = TILE_Q // PROMPT_SCOPE
    p_areas = n_prompt // PROMPT_SCOPE
    n_areas = p_areas + spt
    q_max = (n_areas * AREA_CHUNKS + N_EXPERTS * (BLOCK_CHUNKS - 1) + BLOCK_CHUNKS - 1) // BLOCK_CHUNKS * BLOCK_CHUNKS

    row = lambda a: a.reshape(1, -1)
    qk = ATTN_DIM + KV_DIM
    half = HEAD_DIM // 2

    def swap_halves(a):
        parts = a[..., :qk].reshape(a.shape[:-1] + (qk // HEAD_DIM, 2, half))
        return parts[..., ::-1, :].reshape(a.shape[:-1] + (qk,))

    w_in_b = w_in[0].astype(BF16)
    w_sw_b = swap_halves(w_in[0]).astype(BF16)
    w_out_b = w_out[0].astype(BF16)
    w_r = jnp.concatenate([w_router_group[0], w_router_expert[0].reshape(D_MODEL, N_EXPERTS),
                           jnp.zeros((D_MODEL, LANES - ELANE0 - N_EXPERTS), F32)], axis=1).astype(BF16)
    shared = (w_in_b, row(b_in[0]), w_sw_b, row(swap_halves(b_in[0])), conv_w[0], row(conv_b[0]),
              row(ln_conv_g[0]), row(ln_conv_b[0]), w_out_b, row(b_out[0]), row(ln1_g[0]), row(ln1_b[0]), w_r)
    shared_specs = [_full(a.shape) for a in shared]

    def glu_blocks(a):
        lo = ATTN_DIM + 2 * KV_DIM
        vg = a[..., lo:lo + 2 * CONV_CH].reshape(a.shape[:-1] + (2, CONV_CH // LANES, LANES))
        return jnp.swapaxes(vg, -3, -2).reshape(a.shape[:-1] + (2 * CONV_CH,))

    glu = (glu_blocks(w_in[0]).astype(BF16), row(glu_blocks(b_in[0])))
    glu_specs = [_full(a.shape) for a in glu]
    smem = pl.BlockSpec(memory_space=pltpu.SMEM)
    rope_p = _rope_tables(np.arange(seq))
    rope_s = _rope_tables(PAST_LEN + np.arange(dec_t))

    cparams = lambda sem: pltpu.CompilerParams(dimension_semantics=sem, vmem_limit_bytes=VMEM_LIMIT)

    ck = cache_k[0].reshape(dec_b, WINDOW, KV_DIM)
    cv = cache_v[0].reshape(dec_b, WINDOW, KV_DIM)
    st = state_conv[0]
    tb = SAMPLE_TB
    h_s, nk_s, nv_s, nc_s, xs_s, route_s, cnt_s = pl.pallas_call(
        _mix_sample_kernel,
        grid=(dec_b // tb,),
        in_specs=[smem,
                  pl.BlockSpec((tb, D_MODEL), lambda t: (t, 0)),
                  _full(rope_s.shape),
                  pl.BlockSpec((tb, WINDOW, KV_DIM), lambda t: (t, 0, 0)),
                  pl.BlockSpec((tb, WINDOW, KV_DIM), lambda t: (t, 0, 0)),
                  pl.BlockSpec((tb, CONV_WIDTH - 1, CONV_CH), lambda t: (t, 0, 0))] + shared_specs,
        out_specs=[pl.BlockSpec((tb, D_MODEL), lambda t: (t, 0)),
                   pl.BlockSpec((tb, WINDOW, KV_DIM), lambda t: (t, 0, 0)),
                   pl.BlockSpec((tb, WINDOW, KV_DIM), lambda t: (t, 0, 0)),
                   pl.BlockSpec((tb, CONV_WIDTH - 1, CONV_CH), lambda t: (t, 0, 0)),
                   _full((AREA_ROWS, HALF)),
                   _full((dec_b, LANES)),
                   _full((1, SUBLANES, LANES))],
        out_shape=[jax.ShapeDtypeStruct((dec_b, D_MODEL), F32),
                   jax.ShapeDtypeStruct((dec_b, WINDOW, KV_DIM), F32),
                   jax.ShapeDtypeStruct((dec_b, WINDOW, KV_DIM), F32),
                   jax.ShapeDtypeStruct((dec_b, CONV_WIDTH - 1, CONV_CH), F32),
                   jax.ShapeDtypeStruct((AREA_ROWS, HALF), U32),
                   jax.ShapeDtypeStruct((dec_b, LANES), F32),
                   jax.ShapeDtypeStruct((1, SUBLANES, LANES), F32)],
        scratch_shapes=[pltpu.VMEM((dec_b, D_MODEL), F32)],
        compiler_params=cparams(("arbitrary",)),
        name="mix_sample",
    )(sinks[0], x_sample.reshape(dec_b, D_MODEL), rope_s, ck, cv, st, *shared)

    n_real = bsz * tiles
    real = lambda t: jnp.minimum(t, n_real - 1)
    h_p, xs, route_p, cnt_p, nk_p, nv_p, nc_p = pl.pallas_call(
        functools.partial(_mix_prompt_kernel, tiles=tiles),
        grid=(n_real + 1,),
        in_specs=[smem,
                  pl.BlockSpec((1, TILE_Q, D_MODEL), lambda t: (real(t) // tiles, real(t) % tiles, 0)),
                  pl.BlockSpec((TILE_Q, 4 * LANES), lambda t: (real(t) % tiles, 0)),
                  _full((AREA_ROWS, HALF))] + glu_specs + shared_specs,
        out_specs=[pl.BlockSpec((1, TILE_Q, D_MODEL), lambda t: (real(t) // tiles, real(t) % tiles, 0)),
                   pl.BlockSpec((spt * AREA_ROWS, HALF), lambda t: (t, 0)),
                   pl.BlockSpec((TILE_Q, LANES), lambda t: (real(t), 0)),
                   pl.BlockSpec((spt, SUBLANES, LANES), lambda t: (t, 0, 0)),
                   pl.BlockSpec((1, WINDOW, KV_DIM), lambda t: (real(t) // tiles, 0, 0)),
                   pl.BlockSpec((1, WINDOW, KV_DIM), lambda t: (real(t) // tiles, 0, 0)),
                   pl.BlockSpec((1, CONV_WIDTH - 1, CONV_CH), lambda t: (real(t) // tiles, 0, 0))],
        out_shape=[jax.ShapeDtypeStruct((bsz, seq, D_MODEL), F32),
                   jax.ShapeDtypeStruct((n_areas * AREA_ROWS, HALF), U32),
                   jax.ShapeDtypeStruct((n_prompt, LANES), F32),
                   jax.ShapeDtypeStruct((n_areas, SUBLANES, LANES), F32),
                   jax.ShapeDtypeStruct((bsz, WINDOW, KV_DIM), F32),
                   jax.ShapeDtypeStruct((bsz, WINDOW, KV_DIM), F32),
                   jax.ShapeDtypeStruct((bsz, CONV_WIDTH - 1, CONV_CH), F32)],
        scratch_shapes=[pltpu.VMEM((WINDOW, KV_DIM), F32), pltpu.VMEM((WINDOW, KV_DIM), F32),
                        pltpu.VMEM((TILE_Q + CONV_PRE, CONV_CH), F32),
                        pltpu.VMEM((SUBLANES - 1, TILE_Q + CONV_SHIFT_EXTRA, CONV_CH), F32)],
        compiler_params=cparams(("arbitrary",)),
        name="mix_prompt",
    )(sinks[0], x_prompt, rope_p, xs_s, *glu, *shared)

    cnt = jnp.concatenate([cnt_p[:p_areas, 0], cnt_s[:, 0], cnt_p[p_areas + 1:, 0]], axis=0)
    cnt = cnt[:, ELANE0:ELANE0 + N_EXPERTS].astype(jnp.int32)
    src_tab, dst_tab, bstart, used8 = _plan_blocks(cnt, q_max)
    w_idx = lambda e, *_: (e, 0, 0)
    ys = pl.pallas_call(
        _moe_ffn_kernel,
        grid_spec=pltpu.PrefetchScalarGridSpec(
            num_scalar_prefetch=4,
            grid=(N_EXPERTS,),
            in_specs=[pl.BlockSpec(memory_space=pl.ANY),
                      pl.BlockSpec((1, D_MODEL, D_FF_EXPERT), w_idx),
                      pl.BlockSpec((1, D_MODEL, D_FF_EXPERT), w_idx),
                      pl.BlockSpec((1, D_FF_EXPERT, D_MODEL), w_idx)],
            out_specs=pl.BlockSpec(memory_space=pl.ANY),
            scratch_shapes=[pltpu.VMEM((GATHER_BUFS, BLOCK_ROWS, HALF), U32),
                            pltpu.VMEM((SCATTER_BUFS, BLOCK_ROWS, HALF), U32),
                            pltpu.VMEM((SCATTER_BUFS, BLOCK_ROWS, D_FF_EXPERT), BF16),
                            pltpu.VMEM((ZERO_ROWS, HALF), U32),
                            pltpu.VMEM((D_MODEL, D_FF_EXPERT), BF16), pltpu.VMEM((D_MODEL, D_FF_EXPERT), BF16),
                            pltpu.VMEM((D_FF_EXPERT, D_MODEL), BF16),
                            pltpu.SemaphoreType.DMA((GATHER_BUFS,)), pltpu.SemaphoreType.DMA((SCATTER_BUFS,)),
                            pltpu.SemaphoreType.DMA(()), pltpu.SemaphoreType.DMA(())]),
        out_shape=jax.ShapeDtypeStruct((n_areas * AREA_ROWS + DUMP_CHUNKS * SUBLANES, HALF), U32),
        compiler_params=cparams(("arbitrary",)),
        name="moe_ffn",
    )(src_tab, dst_tab, bstart, used8, xs, w1[0], w3[0], w2[0])

    def combine(h2d, route, first_area, scope, tile):
        n = h2d.shape[0]
        apt = tile // scope
        return pl.pallas_call(
            _moe_combine_kernel,
            grid=(n // tile,),
            in_specs=[pl.BlockSpec((tile, D_MODEL), lambda t: (t, 0)),
                      pl.BlockSpec((tile, LANES), lambda t: (t, 0)),
                      pl.BlockSpec((apt * AREA_ROWS, HALF), lambda t: (first_area // apt + t, 0)),
                      _full((1, D_MODEL)), _full((1, D_MODEL))],
            out_specs=pl.BlockSpec((tile, D_MODEL), lambda t: (t, 0)),
            out_shape=jax.ShapeDtypeStruct((n, D_MODEL), F32),
            compiler_params=cparams(("arbitrary",)),
            name="moe_combine",
        )(h2d, route, ys, row(ln2_g[0]), row(ln2_b[0]))

    y_p = combine(h_p.reshape(n_prompt, D_MODEL), route_p, 0, PROMPT_SCOPE, COMBINE_TILE)
    y_s = combine(h_s, route_s, p_areas, dec_b, dec_b)

    kv_shape = lambda n: (DEPTH, n, WINDOW, N_KV_HEADS, HEAD_DIM)
    return (y_p.reshape(bsz, seq, D_MODEL), y_s.reshape(dec_b, dec_t, D_MODEL),
            nk_p.reshape(kv_shape(bsz)), nv_p.reshape(kv_shape(bsz)), nc_p[None],
            nk_s.reshape(kv_shape(dec_b)), nv_s.reshape(kv_shape(dec_b)), nc_s[None])
```

```python
import functools

import jax
import jax.numpy as jnp
import numpy as np
from jax import lax
from jax.experimental import pallas as pl
from jax.experimental.pallas import tpu as pltpu

F32 = jnp.float32
BF16 = jnp.bfloat16
U32 = jnp.uint32

D_MODEL = 1024
N_HEADS = 8
N_KV_HEADS = 2
HEAD_DIM = 64
Q_GROUP = N_HEADS // N_KV_HEADS
ATTN_DIM = N_HEADS * HEAD_DIM
KV_DIM = N_KV_HEADS * HEAD_DIM
WINDOW = 128
ROPE_THETA = 10000.0
ATTN_SCALE = HEAD_DIM ** -0.5
CONV_CH = D_MODEL - ATTN_DIM
CONV_WIDTH = 31
IN_DIM = ATTN_DIM + 2 * KV_DIM + 2 * CONV_CH
N_GROUPS = 4
EXPERTS_PER_GROUP = 8
N_EXPERTS = N_GROUPS * EXPERTS_PER_GROUP
D_FF_EXPERT = 256
LN_EPS = 1e-5
NEG_INF = -1e30
DEPTH = 1
ALPHA = (2.0 * DEPTH) ** 0.25
PAST_LEN = 16384
LOG2E = 1.4426950408889634

LANES = 128
SUBLANES = 8
HALF = D_MODEL // 2
ELANE0 = N_GROUPS
PROMPT_SCOPE = 256
AREA_ROWS = 768
AREA_CHUNKS = AREA_ROWS // SUBLANES
BLOCK_CHUNKS = 32
BLOCK_ROWS = BLOCK_CHUNKS * SUBLANES
TILE_Q = 512
COMBINE_TILE = 1024
CONV_PRE = 32
CONV_SHIFT_EXTRA = 24
SAMPLE_TB = 32
SAMPLE_GRP = 8
VMEM_LIMIT = 56 * 1024 * 1024


def _layer_norm(x, g, b):
    mu = jnp.mean(x, axis=-1, keepdims=True)
    xc = x - mu
    var = jnp.mean(xc * xc, axis=-1, keepdims=True)
    return xc * lax.rsqrt(var + LN_EPS) * g + b


def _rope(x, x_swapped, cos, sin_signed):
    pieces = [x[:, g * LANES:(g + 1) * LANES] * cos + x_swapped[:, g * LANES:(g + 1) * LANES] * sin_signed
              for g in range(x.shape[-1] // LANES)]
    return pieces[0] if len(pieces) == 1 else jnp.concatenate(pieces, axis=-1)


def _in_proj_glu(xb, w_in_ref, b_in_ref):
    lo, hi = ATTN_DIM + 2 * KV_DIM, IN_DIM
    cacg = jnp.dot(xb, w_in_ref[:, lo:hi], preferred_element_type=F32) + b_in_ref[:, lo:hi]
    return cacg[:, :CONV_CH] * jax.nn.sigmoid(cacg[:, CONV_CH:])


def _in_proj_qkv(xb, w_in_ref, b_in_ref, w_sw_ref, b_sw_ref, rope):
    o_v, o_e = ATTN_DIM + KV_DIM, ATTN_DIM + 2 * KV_DIM
    qkv = jnp.dot(xb, w_in_ref[:, 0:o_e], preferred_element_type=F32) + b_in_ref[:, 0:o_e]
    sw = jnp.dot(xb, w_sw_ref[...], preferred_element_type=F32) + b_sw_ref[...]
    k = _rope(qkv[:, ATTN_DIM:o_v], sw[:, ATTN_DIM:], rope[:, 0:LANES], rope[:, LANES:2 * LANES])
    q = _rope(qkv[:, :ATTN_DIM], sw[:, :ATTN_DIM], rope[:, 2 * LANES:3 * LANES], rope[:, 3 * LANES:])
    return q, k, qkv[:, o_v:]


def _pack_rows(v):
    lo = lax.bitcast_convert_type(v[:, :HALF], U32)
    hi = lax.bitcast_convert_type(v[:, HALF:], U32)
    return (lo >> 16) | (hi & jnp.uint32(0xFFFF0000))


def _unpack_rows(w):
    lo = lax.bitcast_convert_type(w << 16, F32).astype(BF16)
    hi = lax.bitcast_convert_type(w & jnp.uint32(0xFFFF0000), F32).astype(BF16)
    return lo, hi


def _route_and_sort(h, w_r, scope):
    n = h.shape[0]
    hb = h.astype(BF16)
    logits = jnp.dot(hb, w_r, preferred_element_type=F32)
    lane = lax.broadcasted_iota(jnp.int32, (n, LANES), 1)
    lane_f = lane.astype(F32)
    is_g = lane < N_GROUPS
    lg = jnp.where(is_g, logits, -jnp.inf)
    gmax = jnp.max(lg, axis=-1, keepdims=True)
    gidx = jnp.min(jnp.where(lg == gmax, lane_f, float(LANES)), axis=-1, keepdims=True)
    p_g = 1.0 / jnp.sum(jnp.where(is_g, jnp.exp(logits - gmax), 0.0), axis=-1, keepdims=True)
    egrp = ((lane - ELANE0) >> 3).astype(F32)
    emask = (lane >= ELANE0) & (lane < ELANE0 + N_EXPERTS) & (egrp == gidx)
    sel = jnp.where(emask, logits, -jnp.inf)
    v1 = jnp.max(sel, axis=-1, keepdims=True)
    i1 = jnp.min(jnp.where(sel == v1, lane_f, float(LANES)), axis=-1, keepdims=True)
    sel2 = jnp.where(lane_f == i1, -jnp.inf, sel)
    v2 = jnp.max(sel2, axis=-1, keepdims=True)
    i2 = jnp.min(jnp.where(sel2 == v2, lane_f, float(LANES)), axis=-1, keepdims=True)
    t = jnp.exp(v2 - v1)
    gate1 = p_g / (1.0 + t)
    gate2 = p_g * t / (1.0 + t)
    pick1 = lane_f == i1
    pick2 = lane_f == i2
    member = jnp.where(pick1 | pick2, 1.0, 0.0)

    r_i = lax.broadcasted_iota(jnp.int32, (scope, scope), 0)
    c_i = lax.broadcasted_iota(jnp.int32, (scope, scope), 1)
    lower = jnp.where(c_i < r_i, 1.0, 0.0).astype(BF16)
    e_r = lax.broadcasted_iota(jnp.int32, (LANES, LANES), 0)
    e_c = lax.broadcasted_iota(jnp.int32, (LANES, LANES), 1)
    upper = jnp.where(e_r < e_c, 1.0, 0.0).astype(BF16)
    s_iota = lax.broadcasted_iota(jnp.int32, (scope, AREA_ROWS), 1).astype(F32)

    slots1, slots2, areas, counts = [], [], [], []
    for sc in range(n // scope):
        rows = slice(sc * scope, (sc + 1) * scope)
        m_sc = member[rows]
        before = jnp.dot(lower, m_sc.astype(BF16), preferred_element_type=F32)
        cnt = jnp.sum(m_sc, axis=0, keepdims=True)
        c8 = jnp.floor((cnt + (SUBLANES - 1)) * (1.0 / SUBLANES))
        c8b = jnp.broadcast_to(c8, (SUBLANES, LANES)).astype(BF16)
        off8 = jnp.dot(c8b, upper, preferred_element_type=F32)[0:1]
        slot_all = before + off8 * float(SUBLANES)
        s1 = jnp.sum(jnp.where(pick1[rows], slot_all, 0.0), axis=-1, keepdims=True)
        s2 = jnp.sum(jnp.where(pick2[rows], slot_all, 0.0), axis=-1, keepdims=True)
        perm_t = jnp.where((s_iota == s1) | (s_iota == s2), 1.0, 0.0).astype(BF16)
        sorted_rows = lax.dot_general(perm_t, hb[rows], (((0,), (0,)), ((), ())),
                                      preferred_element_type=F32)
        areas.append(_pack_rows(sorted_rows))
        counts.append(cnt)
        slots1.append(s1)
        slots2.append(s2)
    slot1 = jnp.concatenate(slots1, axis=0) if len(slots1) > 1 else slots1[0]
    slot2 = jnp.concatenate(slots2, axis=0) if len(slots2) > 1 else slots2[0]
    route = jnp.where(lane == 0, slot1, jnp.where(lane == 1, slot2, jnp.where(lane == 2, gate1,
                      jnp.where(lane == 3, gate2, 0.0))))
    return route, areas, counts


def _softmax_rows_with_sink(s, sink, extra=None):
    m = jnp.maximum(jnp.max(s, axis=-1, keepdims=True), sink)
    if extra is not None:
        m = jnp.maximum(m, extra)
    e = jnp.exp2(s - m)
    den = jnp.sum(e, axis=-1, keepdims=True) + jnp.exp2(sink - m)
    if extra is not None:
        ee = jnp.exp2(extra - m)
        den = den + ee
        return e / den, ee / den
    return e / den, None


def _mix_tail(x, attn, conv_pre, lcg, lcb, w_out_ref, b_out, l1g, l1b):
    conv = _layer_norm(conv_pre, lcg, lcb)
    conv = conv * jax.nn.sigmoid(conv)
    mix = (jnp.dot(attn.astype(BF16), w_out_ref[0:ATTN_DIM, :], preferred_element_type=F32)
           + jnp.dot(conv.astype(BF16), w_out_ref[ATTN_DIM:, :], preferred_element_type=F32) + b_out)
    return _layer_norm(ALPHA * x + mix, l1g, l1b)


def _mix_prompt_kernel(sinks_ref, x_ref, rope_ref, xs_s_ref, *rest, tiles):
    xs_ref = rest[16]
    cnt_ref = rest[18]
    step = pl.program_id(0)
    n_real = pl.num_programs(0) - 1

    @pl.when(step < n_real)
    def _():
        _mix_prompt_tile(sinks_ref, x_ref, rope_ref, *rest, i=step % tiles)

    @pl.when(step == n_real)
    def _():
        xs_ref[0:AREA_ROWS, :] = xs_s_ref[...]
        xs_ref[AREA_ROWS:, :] = jnp.zeros((xs_ref.shape[0] - AREA_ROWS, HALF), U32)
        cnt_ref[...] = jnp.zeros(cnt_ref.shape, F32)


def _mix_prompt_tile(sinks_ref, x_ref, rope_ref, w_glu_ref, b_glu_ref,
                     w_in_ref, b_in_ref, w_sw_ref, b_sw_ref, conv_w_ref, conv_b_ref,
                     lcg_ref, lcb_ref, w_out_ref, b_out_ref, l1g_ref, l1b_ref, w_r_ref,
                     h_ref, xs_ref, route_ref, cnt_ref, nk_ref, nv_ref, nc_ref,
                     kprev, vprev, u_scr, u_shift, *, i):
    tq = x_ref.shape[1]
    pre = CONV_PRE
    n_sc = tq // PROMPT_SCOPE

    @pl.when(i == 0)
    def _():
        kprev[...] = jnp.zeros_like(kprev)
        vprev[...] = jnp.zeros_like(vprev)
        u_scr[0:pre, :] = jnp.zeros((pre, CONV_CH), F32)

    x = x_ref[0]
    xb = x.astype(BF16)

    def glu_block(cb):
        cols = slice(2 * LANES * cb, 2 * LANES * (cb + 1))
        vg = jnp.dot(xb, w_glu_ref[:, cols], preferred_element_type=F32) + b_glu_ref[:, cols]
        return vg[:, :LANES] * jax.nn.sigmoid(vg[:, LANES:])

    n_cb = CONV_CH // LANES
    acc_blocks = [[None] * n_cb for _ in range(n_sc)]
    u_next = glu_block(0)
    for cb in range(n_cb):
        lanes = slice(LANES * cb, LANES * (cb + 1))
        u_scr[pre:pre + tq, lanes] = u_next
        if cb + 1 < n_cb:
            u_next = glu_block(cb + 1)
        for r in range(1, SUBLANES):
            u_shift[r - 1, :, lanes] = u_scr[r:r + tq + CONV_SHIFT_EXTRA, lanes]
        for sc in range(n_sc):
            base = sc * PROMPT_SCOPE
            acc = jnp.zeros((PROMPT_SCOPE, LANES), F32)
            for j in range(CONV_WIDTH):
                a, r = divmod(pre - (CONV_WIDTH - 1) + j, SUBLANES)
                start = base + a * SUBLANES
                src = (u_scr[start:start + PROMPT_SCOPE, lanes] if r == 0
                       else u_shift[r - 1, start:start + PROMPT_SCOPE, lanes])
                acc = acc + src * conv_w_ref[j:j + 1, lanes]
            acc_blocks[sc][cb] = acc
    conv_pre = [jnp.concatenate(blocks, axis=1) + conv_b_ref[...] for blocks in acc_blocks]

    nc_ref[0] = u_scr[pre + tq - (CONV_WIDTH - 1):pre + tq, :]
    tail = u_scr[tq:tq + pre, :]
    u_scr[0:pre, :] = tail

    q, k, v = _in_proj_qkv(xb, w_in_ref, b_in_ref, w_sw_ref, b_sw_ref, rope_ref[...])
    qb = q.astype(BF16)
    kb = jnp.concatenate([kprev[...], k], axis=0).astype(BF16)
    vb = jnp.concatenate([vprev[...], v], axis=0).astype(BF16)
    row = lax.broadcasted_iota(jnp.int32, (WINDOW, 2 * WINDOW), 0)
    col = lax.broadcasted_iota(jnp.int32, (WINDOW, 2 * WINDOW), 1)
    dist = row + WINDOW - col
    band = (dist >= 0) & (dist < WINDOW)
    first_col = jnp.where(i > 0, 0, WINDOW)
    attn_blocks = []
    for j in range(tq // WINDOW):
        msk = (band & (col >= first_col)) if j == 0 else band
        heads = [None] * N_HEADS
        for c in range(N_KV_HEADS):
            kc = kb[j * WINDOW:(j + 2) * WINDOW, c * HEAD_DIM:(c + 1) * HEAD_DIM]
            vc = vb[j * WINDOW:(j + 2) * WINDOW, c * HEAD_DIM:(c + 1) * HEAD_DIM]
            qc = jnp.concatenate(
                [qb[j * WINDOW:(j + 1) * WINDOW, (c * Q_GROUP + g) * HEAD_DIM:(c * Q_GROUP + g + 1) * HEAD_DIM]
                 for g in range(Q_GROUP)], axis=0)
            s = lax.dot_general(qc, kc, (((1,), (1,)), ((), ())), preferred_element_type=F32)
            probs = []
            for g in range(Q_GROUP):
                sg = jnp.where(msk, s[g * WINDOW:(g + 1) * WINDOW], NEG_INF)
                p, _ = _softmax_rows_with_sink(sg, sinks_ref[c * Q_GROUP + g] * LOG2E)
                probs.append(p.astype(BF16))
            o = jnp.dot(jnp.concatenate(probs, axis=0), vc, preferred_element_type=F32)
            for g in range(Q_GROUP):
                heads[c * Q_GROUP + g] = o[g * WINDOW:(g + 1) * WINDOW]
        attn_blocks.append(jnp.concatenate(heads, axis=1))
    attn = jnp.concatenate(attn_blocks, axis=0)
    kprev[...] = k[tq - WINDOW:, :]
    vprev[...] = v[tq - WINDOW:, :]
    nk_ref[0] = k[tq - WINDOW:, :]
    nv_ref[0] = v[tq - WINDOW:, :]

    for sc in range(n_sc):
        rows = slice(sc * PROMPT_SCOPE, (sc + 1) * PROMPT_SCOPE)
        h = _mix_tail(x[rows], attn[rows], conv_pre[sc], lcg_ref[...], lcb_ref[...], w_out_ref, b_out_ref[...],
                      l1g_ref[...], l1b_ref[...])
        h_ref[0, rows, :] = h
        route, areas, counts = _route_and_sort(h, w_r_ref[...], PROMPT_SCOPE)
        route_ref[rows, :] = route
        xs_ref[sc * AREA_ROWS:(sc + 1) * AREA_ROWS, :] = areas[0]
        cnt_ref[sc] = jnp.broadcast_to(counts[0], (SUBLANES, LANES))


def _mix_sample_kernel(sinks_ref, x_ref, rope_ref, ck_ref, cv_ref, st_ref, w_in_ref, b_in_ref, w_sw_ref, b_sw_ref,
                       conv_w_ref, conv_b_ref, lcg_ref, lcb_ref, w_out_ref, b_out_ref, l1g_ref, l1b_ref,
                       w_r_ref,
                       h_ref, nk_ref, nv_ref, nc_ref, xs_ref, route_ref, cnt_ref,
                       h_scr):
    tb = x_ref.shape[0]
    n_tok = h_scr.shape[0]
    step = pl.program_id(0)
    x = x_ref[...]
    xb = x.astype(BF16)
    u = _in_proj_glu(xb, w_in_ref, b_in_ref)
    q, k_new, v_new = _in_proj_qkv(xb, w_in_ref, b_in_ref, w_sw_ref, b_sw_ref, rope_ref[...])

    qb = q.astype(BF16)
    knb = k_new.astype(BF16).astype(F32)
    vnb = v_new.astype(BF16).astype(F32)
    gq = SAMPLE_GRP * Q_GROUP
    nkeys = SAMPLE_GRP * WINDOW
    r_i = lax.broadcasted_iota(jnp.int32, (gq, nkeys), 0)
    c_i = lax.broadcasted_iota(jnp.int32, (gq, nkeys), 1)
    msk = ((c_i >> 7) == (r_i & (SAMPLE_GRP - 1))) & ((c_i & (WINDOW - 1)) >= 1)
    r_col = lax.broadcasted_iota(jnp.int32, (gq, 1), 0) >> 3
    attn_groups = []
    for grp in range(tb // SAMPLE_GRP):
        r0 = grp * SAMPLE_GRP
        kc_all = ck_ref[r0:r0 + SAMPLE_GRP].reshape(nkeys, KV_DIM).astype(BF16)
        vc_all = cv_ref[r0:r0 + SAMPLE_GRP].reshape(nkeys, KV_DIM).astype(BF16)
        heads = [None] * N_HEADS
        for c in range(N_KV_HEADS):
            lanes = slice(c * HEAD_DIM, (c + 1) * HEAD_DIM)
            qc = jnp.concatenate(
                [qb[r0:r0 + SAMPLE_GRP, (c * Q_GROUP + g) * HEAD_DIM:(c * Q_GROUP + g + 1) * HEAD_DIM]
                 for g in range(Q_GROUP)], axis=0)
            s = lax.dot_general(qc, kc_all[:, lanes], (((1,), (1,)), ((), ())), preferred_element_type=F32)
            s = jnp.where(msk, s, NEG_INF)
            kn = jnp.concatenate([knb[r0:r0 + SAMPLE_GRP, lanes]] * Q_GROUP, axis=0)
            vn = jnp.concatenate([vnb[r0:r0 + SAMPLE_GRP, lanes]] * Q_GROUP, axis=0)
            s_new = jnp.sum(qc.astype(F32) * kn, axis=-1, keepdims=True)
            sink = jnp.zeros((gq, 1), F32)
            for g in range(Q_GROUP):
                sink = jnp.where(r_col == g, sinks_ref[c * Q_GROUP + g] * LOG2E, sink)
            p, p_new = _softmax_rows_with_sink(s, sink, s_new)
            o = (jnp.dot(p.astype(BF16), vc_all[:, lanes], preferred_element_type=F32)
                 + p_new.astype(BF16).astype(F32) * vn)
            for g in range(Q_GROUP):
                heads[c * Q_GROUP + g] = o[g * SAMPLE_GRP:(g + 1) * SAMPLE_GRP]
        attn_groups.append(jnp.concatenate(heads, axis=1))
    attn = jnp.concatenate(attn_groups, axis=0)

    st = st_ref[...]
    acc = jnp.sum(st * conv_w_ref[0:CONV_WIDTH - 1, :][None], axis=1)
    conv_pre = acc + u * conv_w_ref[CONV_WIDTH - 1:CONV_WIDTH, :] + conv_b_ref[...]

    nk_ref[:, 0:WINDOW - 1, :] = ck_ref[:, 1:WINDOW, :]
    nv_ref[:, 0:WINDOW - 1, :] = cv_ref[:, 1:WINDOW, :]
    nc_ref[:, 0:CONV_WIDTH - 2, :] = st_ref[:, 1:CONV_WIDTH - 1, :]
    for b in range(tb):
        nk_ref[b, WINDOW - 1:WINDOW, :] = k_new[b:b + 1, :]
        nv_ref[b, WINDOW - 1:WINDOW, :] = v_new[b:b + 1, :]
        nc_ref[b, CONV_WIDTH - 2:CONV_WIDTH - 1, :] = u[b:b + 1, :]

    h = _mix_tail(x, attn, conv_pre, lcg_ref[...], lcb_ref[...], w_out_ref, b_out_ref[...],
                  l1g_ref[...], l1b_ref[...])
    h_ref[...] = h
    h_scr[pl.ds(pl.multiple_of(step * tb, tb), tb), :] = h

    @pl.when(step == pl.num_programs(0) - 1)
    def _():
        route, areas, counts = _route_and_sort(h_scr[...], w_r_ref[...], n_tok)
        route_ref[...] = route
        xs_ref[...] = areas[0]
        cnt_ref[0] = jnp.broadcast_to(counts[0], (SUBLANES, LANES))


GATHER_BUFS = 8
SCATTER_BUFS = 8
SCATTER_LAG = 2
ZERO_BITS = (64, 32, 16, 8, 4, 2, 1)
DUMP_CHUNKS = SCATTER_BUFS * BLOCK_CHUNKS
ZERO_ROWS = max(ZERO_BITS[0], DUMP_CHUNKS) * SUBLANES


def _moe_ffn_kernel(src_ref, dst_ref, bgen_ref, elist_ref, meta_ref, used_ref,
                    xs_hbm, w1_hbm, w3_hbm, w2_hbm,
                    ys_hbm,
                    xbuf, ybuf, hbuf, zbuf, wst1, wst3, wst2, w1b, w3b, w2b,
                    sem_in, sem_out, sem_zero, sem_dump, sem_w):
    n_areas = used_ref.shape[0]
    total = meta_ref[0]
    n_gen = meta_ref[1]
    dump_row0 = n_areas * AREA_ROWS
    lookahead = GATHER_BUFS - 1
    any_rows = pl.ds(0, SUBLANES)

    def chunk_at(ref, idx):
        return pl.ds(pl.multiple_of(ref[idx] * SUBLANES, SUBLANES), SUBLANES)

    def in_copy(rows, sl, c):
        return pltpu.make_async_copy(xs_hbm.at[rows], xbuf.at[sl, pl.ds(c * SUBLANES, SUBLANES)], sem_in.at[sl])

    def out_copy(rows, sl, c):
        return pltpu.make_async_copy(ybuf.at[sl, pl.ds(c * SUBLANES, SUBLANES)], ys_hbm.at[rows], sem_out.at[sl])

    def start_gather(blk):
        for c in range(BLOCK_CHUNKS):
            in_copy(chunk_at(src_ref, blk * BLOCK_CHUNKS + c), blk % GATHER_BUFS, c).start()

    def wait_gather(blk):
        for c in range(BLOCK_CHUNKS):
            in_copy(any_rows, blk % GATHER_BUFS, c).wait()

    def start_scatter(blk):
        for c in range(BLOCK_CHUNKS):
            out_copy(chunk_at(dst_ref, (blk + SCATTER_LAG) * BLOCK_CHUNKS + c), blk % SCATTER_BUFS, c).start()

    def wait_scatter(sl):
        for c in range(BLOCK_CHUNKS):
            out_copy(any_rows, sl, c).wait()

    def load_rows(blk):
        return _unpack_rows(xbuf[blk % GATHER_BUFS])

    def issue_dmas(blk):
        start_scatter(blk - SCATTER_LAG)
        start_gather(blk + lookahead)

    def up_gate(blk, rows, ws):
        x_lo, x_hi = rows
        a1 = (jnp.dot(x_lo, w1b[ws, 0:HALF, :], preferred_element_type=F32)
              + jnp.dot(x_hi, w1b[ws, HALF:, :], preferred_element_type=F32))
        a3 = (jnp.dot(x_lo, w3b[ws, 0:HALF, :], preferred_element_type=F32)
              + jnp.dot(x_hi, w3b[ws, HALF:, :], preferred_element_type=F32))
        hbuf[blk % SCATTER_BUFS] = (a1 * jax.nn.sigmoid(a1) * a3).astype(BF16)

    def down(blk, ws):
        y = jnp.dot(hbuf[blk % SCATTER_BUFS], w2b[ws], preferred_element_type=F32)
        ybuf[blk % SCATTER_BUFS] = _pack_rows(y.astype(BF16).astype(F32))

    def for_each_weight_copy(gen, fn):
        ws = gen % 2
        expert = elist_ref[gen]
        for hbm, staging in ((w1_hbm, wst1), (w3_hbm, wst3), (w2_hbm, wst2)):
            fn(pltpu.make_async_copy(hbm.at[expert], staging.at[ws], sem_w.at[ws]))

    def switch_weights(gen):
        ws = gen % 2
        for_each_weight_copy(gen, lambda cp: cp.wait())
        w1b[ws] = wst1[ws].astype(BF16)
        w3b[ws] = wst3[ws].astype(BF16)
        w2b[ws] = wst2[ws].astype(BF16)

        @pl.when(gen + 1 < n_gen)
        def _():
            for_each_weight_copy(gen + 1, lambda cp: cp.start())

    def for_each_tail_piece(fn):
        def area_body(s, carry):
            used = used_ref[s]
            tail = AREA_CHUNKS - used
            row = (s * AREA_CHUNKS + used) * SUBLANES
            for bit in ZERO_BITS:
                take = (tail & bit) != 0

                @pl.when(take)
                def _(row=row, bit=bit):
                    fn(pltpu.make_async_copy(zbuf.at[pl.ds(0, bit * SUBLANES)],
                                             ys_hbm.at[pl.ds(pl.multiple_of(row, SUBLANES), bit * SUBLANES)],
                                             sem_zero))
                row = row + jnp.where(take, bit * SUBLANES, 0)
            return carry
        lax.fori_loop(0, n_areas, area_body, 0)

    @pl.when(n_gen > 0)
    def _():
        for_each_weight_copy(0, lambda cp: cp.start())

    zbuf[...] = jnp.zeros(zbuf.shape, U32)
    ybuf[...] = jnp.zeros(ybuf.shape, U32)
    dump = pltpu.make_async_copy(zbuf.at[pl.ds(0, DUMP_CHUNKS * SUBLANES)],
                                 ys_hbm.at[pl.ds(dump_row0, DUMP_CHUNKS * SUBLANES)], sem_dump)
    dump.start()
    dump.wait()
    for_each_tail_piece(lambda cp: cp.start())
    for sl in range(SCATTER_BUFS - SCATTER_LAG):
        for c in range(BLOCK_CHUNKS):
            out_copy(pl.ds(dump_row0 + (sl * BLOCK_CHUNKS + c) * SUBLANES, SUBLANES), sl, c).start()
    for k in range(lookahead):
        start_gather(k)

    @pl.when(total > 0)
    def _():
        switch_weights(0)
        wait_gather(0)
        rows = load_rows(0)
        issue_dmas(0)
        up_gate(0, rows, 0)

        def block_body(blk, carry):
            gen = bgen_ref[blk]
            gen_prev = bgen_ref[blk - 1]

            @pl.when(gen != gen_prev)
            def _():
                switch_weights(gen)

            wait_gather(blk)
            wait_scatter((blk - 1) % SCATTER_BUFS)
            rows = load_rows(blk)
            issue_dmas(blk)
            down(blk - 1, gen_prev % 2)
            up_gate(blk, rows, gen % 2)
            return carry

        lax.fori_loop(1, total, block_body, 0)
        wait_scatter((total - 1) % SCATTER_BUFS)
        down(total - 1, bgen_ref[total - 1] % 2)

    for k in range(SCATTER_LAG):
        start_scatter(total - SCATTER_LAG + k)
    for sl in range(SCATTER_BUFS):
        wait_scatter(sl)
    for k in range(lookahead):
        wait_gather(total + k)
    for_each_tail_piece(lambda cp: cp.wait())


def _moe_combine_kernel(h_ref, route_ref, ys_ref, l2g_ref, l2b_ref, y_ref):
    n = h_ref.shape[0]
    scope = n // (ys_ref.shape[0] // AREA_ROWS)
    s_iota = lax.broadcasted_iota(jnp.int32, (scope, AREA_ROWS), 1).astype(F32)
    for sc in range(n // scope):
        rows = slice(sc * scope, (sc + 1) * scope)
        route = route_ref[rows, :]
        slot1 = route[:, 0:1]
        slot2 = route[:, 1:2]
        gate1 = route[:, 2:3]
        gate2 = route[:, 3:4]
        gmat = (jnp.where(s_iota == slot1, gate1, 0.0) + jnp.where(s_iota == slot2, gate2, 0.0)).astype(BF16)
        y_lo, y_hi = _unpack_rows(ys_ref[sc * AREA_ROWS:(sc + 1) * AREA_ROWS, :])
        f = jnp.concatenate([jnp.dot(gmat, y_lo, preferred_element_type=F32),
                             jnp.dot(gmat, y_hi, preferred_element_type=F32)], axis=1)
        y_ref[rows, :] = _layer_norm(ALPHA * h_ref[rows, :] + f, l2g_ref[...], l2b_ref[...])


def _excl_cumsum(a, axis):
    return jnp.cumsum(a, axis=axis) - a


def _plan_blocks(cnt, q_max):
    n_areas = cnt.shape[0]
    c8 = (cnt + (SUBLANES - 1)) // SUBLANES
    used8 = jnp.sum(c8, axis=1)
    base8 = jnp.arange(n_areas, dtype=jnp.int32) * AREA_CHUNKS
    src_runs = (base8[:, None] + _excl_cumsum(c8, 1)).T
    len_runs = c8.T
    tot = jnp.sum(len_runs, axis=1)
    ptot = (tot + BLOCK_CHUNKS - 1) // BLOCK_CHUNKS * BLOCK_CHUNKS
    pstart = _excl_cumsum(ptot, 0)
    dst_runs = pstart[:, None] + _excl_cumsum(len_runs, 1)
    off = (src_runs - dst_runs).reshape(-1)
    diff = off - jnp.concatenate([jnp.zeros((1,), jnp.int32), off[:-1]])
    q = jnp.arange(q_max, dtype=jnp.int32)
    src = q + jnp.sum(jnp.where(dst_runs.reshape(1, -1) <= q[:, None], diff[None, :], 0), axis=1)
    valid = jnp.any((pstart[None, :] <= q[:, None]) & (q[:, None] < (pstart + tot)[None, :]), axis=1)
    zero_chunk = AREA_CHUNKS - 1
    ahead = jnp.full(((GATHER_BUFS - 1) * BLOCK_CHUNKS,), zero_chunk, jnp.int32)
    src_tab = jnp.concatenate([jnp.where(valid, src, zero_chunk).astype(jnp.int32), ahead])
    qv = jnp.arange(-SCATTER_LAG * BLOCK_CHUNKS, q_max, dtype=jnp.int32)
    dump = n_areas * AREA_CHUNKS + ((qv // BLOCK_CHUNKS) % SCATTER_BUFS) * BLOCK_CHUNKS + qv % BLOCK_CHUNKS
    lead = jnp.zeros((SCATTER_LAG * BLOCK_CHUNKS,), jnp.int32)
    dst_tab = jnp.where(jnp.concatenate([lead > 0, valid]), jnp.concatenate([lead, src]), dump).astype(jnp.int32)
    nblk = ptot // BLOCK_CHUNKS
    bstart = pstart // BLOCK_CHUNKS
    has = nblk > 0
    gen_of = jnp.cumsum(has.astype(jnp.int32)) - 1
    experts = jnp.arange(N_EXPERTS, dtype=jnp.int32)
    elist = jnp.sum(jnp.where(has[None, :] & (gen_of[None, :] == experts[:, None]), experts[None, :], 0), axis=1)
    blk = jnp.arange(q_max // BLOCK_CHUNKS, dtype=jnp.int32)
    owner = (bstart[None, :] <= blk[:, None]) & (blk[:, None] < (bstart + nblk)[None, :])
    bgen = jnp.sum(jnp.where(owner, gen_of[None, :], 0), axis=1)
    meta = jnp.stack([jnp.sum(nblk), jnp.sum(has.astype(jnp.int32))])
    return (src_tab, dst_tab, bgen.astype(jnp.int32), elist.astype(jnp.int32), meta.astype(jnp.int32),
            used8.astype(jnp.int32))


def _rope_tables(pos):
    half = HEAD_DIM // 2
    inv = (ROPE_THETA ** (-np.arange(half, dtype=np.float64) / half)).astype(np.float32)
    ang = (np.asarray(pos, np.float32)[:, None] * inv).astype(np.float64)
    cos = np.cos(ang)
    sin = np.sin(ang)
    reps = LANES // HEAD_DIM
    cos_t = np.concatenate([cos, cos] * reps, axis=1)
    sin_t = np.concatenate([-sin, sin] * reps, axis=1)
    q_scale = ATTN_SCALE * LOG2E
    return jnp.asarray(np.concatenate([cos_t, sin_t, cos_t * q_scale, sin_t * q_scale], axis=1), F32)


def _full(shape):
    nd = len(shape)
    return pl.BlockSpec(shape, lambda *_: (0,) * nd)


def kernel(x_prompt, x_sample, cache_k, cache_v, state_conv, w_in, b_in, sinks, conv_w, conv_b, ln_conv_g,
           ln_conv_b, w_out, b_out, ln1_g, ln1_b, w_router_group, w_router_expert, w1, w3, w2, ln2_g, ln2_b):
    assert w_in.shape[0] == DEPTH
    bsz, seq, _ = x_prompt.shape
    dec_b, dec_t, _ = x_sample.shape
    assert dec_t == 1 and seq % TILE_Q == 0 and dec_b % SAMPLE_TB == 0 and 2 * dec_b + 7 * N_EXPERTS <= AREA_ROWS
    assert (bsz * seq) % COMBINE_TILE == 0
    n_prompt = bsz * seq
    tiles = seq // TILE_Q
    spt = TILE_Q // PROMPT_SCOPE
    p_areas = n_prompt // PROMPT_SCOPE
    n_areas = p_areas + spt
    q_max = (n_areas * AREA_CHUNKS + N_EXPERTS * (BLOCK_CHUNKS - 1) + BLOCK_CHUNKS - 1) // BLOCK_CHUNKS * BLOCK_CHUNKS

    row = lambda a: a.reshape(1, -1)
    qk = ATTN_DIM + KV_DIM
    half = HEAD_DIM // 2

    def swap_halves(a):
        parts = a[..., :qk].reshape(a.shape[:-1] + (qk // HEAD_DIM, 2, half))
        return parts[..., ::-1, :].reshape(a.shape[:-1] + (qk,))

    w_in_b = w_in[0].astype(BF16)
    w_sw_b = swap_halves(w_in[0]).astype(BF16)
    w_out_b = w_out[0].astype(BF16)
    w_r = jnp.concatenate([w_router_group[0], w_router_expert[0].reshape(D_MODEL, N_EXPERTS),
                           jnp.zeros((D_MODEL, LANES - ELANE0 - N_EXPERTS), F32)], axis=1).astype(BF16)
    shared = (w_in_b, row(b_in[0]), w_sw_b, row(swap_halves(b_in[0])), conv_w[0], row(conv_b[0]),
              row(ln_conv_g[0]), row(ln_conv_b[0]), w_out_b, row(b_out[0]), row(ln1_g[0]), row(ln1_b[0]), w_r)
    shared_specs = [_full(a.shape) for a in shared]

    def glu_blocks(a):
        lo = ATTN_DIM + 2 * KV_DIM
        vg = a[..., lo:lo + 2 * CONV_CH].reshape(a.shape[:-1] + (2, CONV_CH // LANES, LANES))
        return jnp.swapaxes(vg, -3, -2).reshape(a.shape[:-1] + (2 * CONV_CH,))

    glu = (glu_blocks(w_in[0]).astype(BF16), row(glu_blocks(b_in[0])))
    glu_specs = [_full(a.shape) for a in glu]
    smem = pl.BlockSpec(memory_space=pltpu.SMEM)
    rope_p = _rope_tables(np.arange(seq))
    rope_s = _rope_tables(PAST_LEN + np.arange(dec_t))

    cparams = lambda sem: pltpu.CompilerParams(dimension_semantics=sem, vmem_limit_bytes=VMEM_LIMIT)

    ck = cache_k[0].reshape(dec_b, WINDOW, KV_DIM)
    cv = cache_v[0].reshape(dec_b, WINDOW, KV_DIM)
    st = state_conv[0]
    tb = SAMPLE_TB
    h_s, nk_s, nv_s, nc_s, xs_s, route_s, cnt_s = pl.pallas_call(
        _mix_sample_kernel,
        grid=(dec_b // tb,),
        in_specs=[smem,
                  pl.BlockSpec((tb, D_MODEL), lambda t: (t, 0)),
                  _full(rope_s.shape),
                  pl.BlockSpec((tb, WINDOW, KV_DIM), lambda t: (t, 0, 0)),
                  pl.BlockSpec((tb, WINDOW, KV_DIM), lambda t: (t, 0, 0)),
                  pl.BlockSpec((tb, CONV_WIDTH - 1, CONV_CH), lambda t: (t, 0, 0))] + shared_specs,
        out_specs=[pl.BlockSpec((tb, D_MODEL), lambda t: (t, 0)),
                   pl.BlockSpec((tb, WINDOW, KV_DIM), lambda t: (t, 0, 0)),
                   pl.BlockSpec((tb, WINDOW, KV_DIM), lambda t: (t, 0, 0)),
                   pl.BlockSpec((tb, CONV_WIDTH - 1, CONV_CH), lambda t: (t, 0, 0)),
                   _full((AREA_ROWS, HALF)),
                   _full((dec_b, LANES)),
                   _full((1, SUBLANES, LANES))],
        out_shape=[jax.ShapeDtypeStruct((dec_b, D_MODEL), F32),
                   jax.ShapeDtypeStruct((dec_b, WINDOW, KV_DIM), F32),
                   jax.ShapeDtypeStruct((dec_b, WINDOW, KV_DIM), F32),
                   jax.ShapeDtypeStruct((dec_b, CONV_WIDTH - 1, CONV_CH), F32),
                   jax.ShapeDtypeStruct((AREA_ROWS, HALF), U32),
                   jax.ShapeDtypeStruct((dec_b, LANES), F32),
                   jax.ShapeDtypeStruct((1, SUBLANES, LANES), F32)],
        scratch_shapes=[pltpu.VMEM((dec_b, D_MODEL), F32)],
        compiler_params=cparams(("arbitrary",)),
        name="mix_sample",
    )(sinks[0], x_sample.reshape(dec_b, D_MODEL), rope_s, ck, cv, st, *shared)

    n_real = bsz * tiles
    real = lambda t: jnp.minimum(t, n_real - 1)
    h_p, xs, route_p, cnt_p, nk_p, nv_p, nc_p = pl.pallas_call(
        functools.partial(_mix_prompt_kernel, tiles=tiles),
        grid=(n_real + 1,),
        in_specs=[smem,
                  pl.BlockSpec((1, TILE_Q, D_MODEL), lambda t: (real(t) // tiles, real(t) % tiles, 0)),
                  pl.BlockSpec((TILE_Q, 4 * LANES), lambda t: (real(t) % tiles, 0)),
                  _full((AREA_ROWS, HALF))] + glu_specs + shared_specs,
        out_specs=[pl.BlockSpec((1, TILE_Q, D_MODEL), lambda t: (real(t) // tiles, real(t) % tiles, 0)),
                   pl.BlockSpec((spt * AREA_ROWS, HALF), lambda t: (t, 0)),
                   pl.BlockSpec((TILE_Q, LANES), lambda t: (real(t), 0)),
                   pl.BlockSpec((spt, SUBLANES, LANES), lambda t: (t, 0, 0)),
                   pl.BlockSpec((1, WINDOW, KV_DIM), lambda t: (real(t) // tiles, 0, 0)),
                   pl.BlockSpec((1, WINDOW, KV_DIM), lambda t: (real(t) // tiles, 0, 0)),
                   pl.BlockSpec((1, CONV_WIDTH - 1, CONV_CH), lambda t: (real(t) // tiles, 0, 0))],
        out_shape=[jax.ShapeDtypeStruct((bsz, seq, D_MODEL), F32),
                   jax.ShapeDtypeStruct((n_areas * AREA_ROWS, HALF), U32),
                   jax.ShapeDtypeStruct((n_prompt, LANES), F32),
                   jax.ShapeDtypeStruct((n_areas, SUBLANES, LANES), F32),
                   jax.ShapeDtypeStruct((bsz, WINDOW, KV_DIM), F32),
                   jax.ShapeDtypeStruct((bsz, WINDOW, KV_DIM), F32),
                   jax.ShapeDtypeStruct((bsz, CONV_WIDTH - 1, CONV_CH), F32)],
        scratch_shapes=[pltpu.VMEM((WINDOW, KV_DIM), F32), pltpu.VMEM((WINDOW, KV_DIM), F32),
                        pltpu.VMEM((TILE_Q + CONV_PRE, CONV_CH), F32),
                        pltpu.VMEM((SUBLANES - 1, TILE_Q + CONV_SHIFT_EXTRA, CONV_CH), F32)],
        compiler_params=cparams(("arbitrary",)),
        name="mix_prompt",
    )(sinks[0], x_prompt, rope_p, xs_s, *glu, *shared)

    cnt = jnp.concatenate([cnt_p[:p_areas, 0], cnt_s[:, 0], cnt_p[p_areas + 1:, 0]], axis=0)
    cnt = cnt[:, ELANE0:ELANE0 + N_EXPERTS].astype(jnp.int32)
    plan = _plan_blocks(cnt, q_max)
    hbm = pl.BlockSpec(memory_space=pl.ANY)
    up_shape, down_shape = (2, D_MODEL, D_FF_EXPERT), (2, D_FF_EXPERT, D_MODEL)
    ys = pl.pallas_call(
        _moe_ffn_kernel,
        grid_spec=pltpu.PrefetchScalarGridSpec(
            num_scalar_prefetch=len(plan),
            grid=(1,),
            in_specs=[hbm, hbm, hbm, hbm],
            out_specs=hbm,
            scratch_shapes=[pltpu.VMEM((GATHER_BUFS, BLOCK_ROWS, HALF), U32),
                            pltpu.VMEM((SCATTER_BUFS, BLOCK_ROWS, HALF), U32),
                            pltpu.VMEM((SCATTER_BUFS, BLOCK_ROWS, D_FF_EXPERT), BF16),
                            pltpu.VMEM((ZERO_ROWS, HALF), U32),
                            pltpu.VMEM(up_shape, F32), pltpu.VMEM(up_shape, F32), pltpu.VMEM(down_shape, F32),
                            pltpu.VMEM(up_shape, BF16), pltpu.VMEM(up_shape, BF16), pltpu.VMEM(down_shape, BF16),
                            pltpu.SemaphoreType.DMA((GATHER_BUFS,)), pltpu.SemaphoreType.DMA((SCATTER_BUFS,)),
                            pltpu.SemaphoreType.DMA(()), pltpu.SemaphoreType.DMA(()),
                            pltpu.SemaphoreType.DMA((2,))]),
        out_shape=jax.ShapeDtypeStruct((n_areas * AREA_ROWS + DUMP_CHUNKS * SUBLANES, HALF), U32),
        compiler_params=cparams(("arbitrary",)),
        name="moe_ffn",
    )(*plan, xs, w1[0], w3[0], w2[0])

    def combine(h2d, route, first_area, scope, tile):
        n = h2d.shape[0]
        apt = tile // scope
        return pl.pallas_call(
            _moe_combine_kernel,
            grid=(n // tile,),
            in_specs=[pl.BlockSpec((tile, D_MODEL), lambda t: (t, 0)),
                      pl.BlockSpec((tile, LANES), lambda t: (t, 0)),
                      pl.BlockSpec((apt * AREA_ROWS, HALF), lambda t: (first_area // apt + t, 0)),
                      _full((1, D_MODEL)), _full((1, D_MODEL))],
            out_specs=pl.BlockSpec((tile, D_MODEL), lambda t: (t, 0)),
            out_shape=jax.ShapeDtypeStruct((n, D_MODEL), F32),
            compiler_params=cparams(("arbitrary",)),
            name="moe_combine",
        )(h2d, route, ys, row(ln2_g[0]), row(ln2_b[0]))

    y_p = combine(h_p.reshape(n_prompt, D_MODEL), route_p, 0, PROMPT_SCOPE, COMBINE_TILE)
    y_s = combine(h_s, route_s, p_areas, dec_b, dec_b)

    kv_shape = lambda n: (DEPTH, n, WINDOW, N_KV_HEADS, HEAD_DIM)
    return (y_p.reshape(bsz, seq, D_MODEL), y_s.reshape(dec_b, dec_t, D_MODEL),
            nk_p.reshape(kv_shape(bsz)), nv_p.reshape(kv_shape(bsz)), nc_p[None],
            nk_s.reshape(kv_shape(dec_b)), nv_s.reshape(kv_shape(dec_b)), nc_s[None])
```

```python
import functools

import jax
import jax.numpy as jnp
import numpy as np
from jax import lax
from jax.experimental import pallas as pl
from jax.experimental.pallas import tpu as pltpu

F32 = jnp.float32
BF16 = jnp.bfloat16
U32 = jnp.uint32

D_MODEL = 1024
N_HEADS = 8
N_KV_HEADS = 2
HEAD_DIM = 64
Q_GROUP = N_HEADS // N_KV_HEADS
ATTN_DIM = N_HEADS * HEAD_DIM
KV_DIM = N_KV_HEADS * HEAD_DIM
WINDOW = 128
ROPE_THETA = 10000.0
ATTN_SCALE = HEAD_DIM ** -0.5
CONV_CH = D_MODEL - ATTN_DIM
CONV_WIDTH = 31
IN_DIM = ATTN_DIM + 2 * KV_DIM + 2 * CONV_CH
N_GROUPS = 4
EXPERTS_PER_GROUP = 8
N_EXPERTS = N_GROUPS * EXPERTS_PER_GROUP
D_FF_EXPERT = 256
LN_EPS = 1e-5
NEG_INF = -1e30
DEPTH = 1
ALPHA = (2.0 * DEPTH) ** 0.25
PAST_LEN = 16384
LOG2E = 1.4426950408889634

LANES = 128
SUBLANES = 8
HALF = D_MODEL // 2
ELANE0 = N_GROUPS
PROMPT_SCOPE = 256
AREA_ROWS = 768
AREA_CHUNKS = AREA_ROWS // SUBLANES
BLOCK_CHUNKS = 32
BLOCK_ROWS = BLOCK_CHUNKS * SUBLANES
TILE_Q = 512
COMBINE_TILE = 1024
CONV_PRE = 32
CONV_SHIFT_EXTRA = 24
SAMPLE_TB = 32
SAMPLE_GRP = 8
VMEM_LIMIT = 56 * 1024 * 1024


def _layer_norm(x, g, b):
    mu = jnp.mean(x, axis=-1, keepdims=True)
    xc = x - mu
    var = jnp.mean(xc * xc, axis=-1, keepdims=True)
    return xc * lax.rsqrt(var + LN_EPS) * g + b


def _rope(x, x_swapped, cos, sin_signed):
    pieces = [x[:, g * LANES:(g + 1) * LANES] * cos + x_swapped[:, g * LANES:(g + 1) * LANES] * sin_signed
              for g in range(x.shape[-1] // LANES)]
    return pieces[0] if len(pieces) == 1 else jnp.concatenate(pieces, axis=-1)


def _in_proj_glu(xb, w_in_ref, b_in_ref):
    lo, hi = ATTN_DIM + 2 * KV_DIM, IN_DIM
    cacg = jnp.dot(xb, w_in_ref[:, lo:hi], preferred_element_type=F32) + b_in_ref[:, lo:hi]
    return cacg[:, :CONV_CH] * jax.nn.sigmoid(cacg[:, CONV_CH:])


def _in_proj_qkv(xb, w_in_ref, b_in_ref, w_sw_ref, b_sw_ref, rope):
    o_v, o_e = ATTN_DIM + KV_DIM, ATTN_DIM + 2 * KV_DIM
    qkv = jnp.dot(xb, w_in_ref[:, 0:o_e], preferred_element_type=F32) + b_in_ref[:, 0:o_e]
    sw = jnp.dot(xb, w_sw_ref[...], preferred_element_type=F32) + b_sw_ref[...]
    k = _rope(qkv[:, ATTN_DIM:o_v], sw[:, ATTN_DIM:], rope[:, 0:LANES], rope[:, LANES:2 * LANES])
    q = _rope(qkv[:, :ATTN_DIM], sw[:, :ATTN_DIM], rope[:, 2 * LANES:3 * LANES], rope[:, 3 * LANES:])
    return q, k, qkv[:, o_v:]


def _pack_rows(v):
    lo = lax.bitcast_convert_type(v[:, :HALF], U32)
    hi = lax.bitcast_convert_type(v[:, HALF:], U32)
    return (lo >> 16) | (hi & jnp.uint32(0xFFFF0000))


def _unpack_rows(w):
    lo = lax.bitcast_convert_type(w << 16, F32).astype(BF16)
    hi = lax.bitcast_convert_type(w & jnp.uint32(0xFFFF0000), F32).astype(BF16)
    return lo, hi


def _route_and_sort(h, w_r, scope):
    n = h.shape[0]
    hb = h.astype(BF16)
    logits = jnp.dot(hb, w_r, preferred_element_type=F32)
    lane = lax.broadcasted_iota(jnp.int32, (n, LANES), 1)
    lane_f = lane.astype(F32)
    is_g = lane < N_GROUPS
    lg = jnp.where(is_g, logits, -jnp.inf)
    gmax = jnp.max(lg, axis=-1, keepdims=True)
    gidx = jnp.min(jnp.where(lg == gmax, lane_f, float(LANES)), axis=-1, keepdims=True)
    p_g = 1.0 / jnp.sum(jnp.where(is_g, jnp.exp(logits - gmax), 0.0), axis=-1, keepdims=True)
    egrp = ((lane - ELANE0) >> 3).astype(F32)
    emask = (lane >= ELANE0) & (lane < ELANE0 + N_EXPERTS) & (egrp == gidx)
    sel = jnp.where(emask, logits, -jnp.inf)
    v1 = jnp.max(sel, axis=-1, keepdims=True)
    i1 = jnp.min(jnp.where(sel == v1, lane_f, float(LANES)), axis=-1, keepdims=True)
    sel2 = jnp.where(lane_f == i1, -jnp.inf, sel)
    v2 = jnp.max(sel2, axis=-1, keepdims=True)
    i2 = jnp.min(jnp.where(sel2 == v2, lane_f, float(LANES)), axis=-1, keepdims=True)
    t = jnp.exp(v2 - v1)
    gate1 = p_g / (1.0 + t)
    gate2 = p_g * t / (1.0 + t)
    pick1 = lane_f == i1
    pick2 = lane_f == i2
    member = jnp.where(pick1 | pick2, 1.0, 0.0)

    r_i = lax.broadcasted_iota(jnp.int32, (scope, scope), 0)
    c_i = lax.broadcasted_iota(jnp.int32, (scope, scope), 1)
    lower = jnp.where(c_i < r_i, 1.0, 0.0).astype(BF16)
    e_r = lax.broadcasted_iota(jnp.int32, (LANES, LANES), 0)
    e_c = lax.broadcasted_iota(jnp.int32, (LANES, LANES), 1)
    upper = jnp.where(e_r < e_c, 1.0, 0.0).astype(BF16)
    s_iota = lax.broadcasted_iota(jnp.int32, (scope, AREA_ROWS), 1).astype(F32)

    slots1, slots2, areas, counts = [], [], [], []
    for sc in range(n // scope):
        rows = slice(sc * scope, (sc + 1) * scope)
        m_sc = member[rows]
        before = jnp.dot(lower, m_sc.astype(BF16), preferred_element_type=F32)
        cnt = jnp.sum(m_sc, axis=0, keepdims=True)
        c8 = jnp.floor((cnt + (SUBLANES - 1)) * (1.0 / SUBLANES))
        c8b = jnp.broadcast_to(c8, (SUBLANES, LANES)).astype(BF16)
        off8 = jnp.dot(c8b, upper, preferred_element_type=F32)[0:1]
        slot_all = before + off8 * float(SUBLANES)
        s1 = jnp.sum(jnp.where(pick1[rows], slot_all, 0.0), axis=-1, keepdims=True)
        s2 = jnp.sum(jnp.where(pick2[rows], slot_all, 0.0), axis=-1, keepdims=True)
        perm_t = jnp.where((s_iota == s1) | (s_iota == s2), 1.0, 0.0).astype(BF16)
        sorted_rows = lax.dot_general(perm_t, hb[rows], (((0,), (0,)), ((), ())),
                                      preferred_element_type=F32)
        areas.append(_pack_rows(sorted_rows))
        counts.append(cnt)
        slots1.append(s1)
        slots2.append(s2)
    slot1 = jnp.concatenate(slots1, axis=0) if len(slots1) > 1 else slots1[0]
    slot2 = jnp.concatenate(slots2, axis=0) if len(slots2) > 1 else slots2[0]
    route = jnp.where(lane == 0, slot1, jnp.where(lane == 1, slot2, jnp.where(lane == 2, gate1,
                      jnp.where(lane == 3, gate2, 0.0))))
    return route, areas, counts


def _softmax_rows_with_sink(s, sink, extra=None):
    m = jnp.maximum(jnp.max(s, axis=-1, keepdims=True), sink)
    if extra is not None:
        m = jnp.maximum(m, extra)
    e = jnp.exp2(s - m)
    den = jnp.sum(e, axis=-1, keepdims=True) + jnp.exp2(sink - m)
    if extra is not None:
        ee = jnp.exp2(extra - m)
        den = den + ee
        return e / den, ee / den
    return e / den, None


def _mix_tail(x, attn, conv_pre, lcg, lcb, w_out_ref, b_out, l1g, l1b):
    conv = _layer_norm(conv_pre, lcg, lcb)
    conv = conv * jax.nn.sigmoid(conv)
    mix = (jnp.dot(attn.astype(BF16), w_out_ref[0:ATTN_DIM, :], preferred_element_type=F32)
           + jnp.dot(conv.astype(BF16), w_out_ref[ATTN_DIM:, :], preferred_element_type=F32) + b_out)
    return _layer_norm(ALPHA * x + mix, l1g, l1b)


def _mix_prompt_kernel(sinks_ref, x_ref, rope_ref, xs_s_ref, *rest, tiles):
    xs_ref = rest[16]
    cnt_ref = rest[18]
    step = pl.program_id(0)
    n_real = pl.num_programs(0) - 1

    @pl.when(step < n_real)
    def _():
        _mix_prompt_tile(sinks_ref, x_ref, rope_ref, *rest, i=step % tiles)

    @pl.when(step == n_real)
    def _():
        xs_ref[0:AREA_ROWS, :] = xs_s_ref[...]
        xs_ref[AREA_ROWS:, :] = jnp.zeros((xs_ref.shape[0] - AREA_ROWS, HALF), U32)
        cnt_ref[...] = jnp.zeros(cnt_ref.shape, F32)


def _mix_prompt_tile(sinks_ref, x_ref, rope_ref, w_glu_ref, b_glu_ref,
                     w_in_ref, b_in_ref, w_sw_ref, b_sw_ref, conv_w_ref, conv_b_ref,
                     lcg_ref, lcb_ref, w_out_ref, b_out_ref, l1g_ref, l1b_ref, w_r_ref,
                     h_ref, xs_ref, route_ref, cnt_ref, nk_ref, nv_ref, nc_ref,
                     kprev, vprev, u_scr, u_shift, *, i):
    tq = x_ref.shape[1]
    pre = CONV_PRE
    n_sc = tq // PROMPT_SCOPE

    @pl.when(i == 0)
    def _():
        kprev[...] = jnp.zeros_like(kprev)
        vprev[...] = jnp.zeros_like(vprev)
        u_scr[0:pre, :] = jnp.zeros((pre, CONV_CH), F32)

    x = x_ref[0]
    xb = x.astype(BF16)

    def glu_block(cb):
        cols = slice(2 * LANES * cb, 2 * LANES * (cb + 1))
        vg = jnp.dot(xb, w_glu_ref[:, cols], preferred_element_type=F32) + b_glu_ref[:, cols]
        return vg[:, :LANES] * jax.nn.sigmoid(vg[:, LANES:])

    n_cb = CONV_CH // LANES
    acc_blocks = [[None] * n_cb for _ in range(n_sc)]
    u_next = glu_block(0)
    for cb in range(n_cb):
        lanes = slice(LANES * cb, LANES * (cb + 1))
        u_scr[pre:pre + tq, lanes] = u_next
        if cb + 1 < n_cb:
            u_next = glu_block(cb + 1)
        for r in range(1, SUBLANES):
            u_shift[r - 1, :, lanes] = u_scr[r:r + tq + CONV_SHIFT_EXTRA, lanes]
        for sc in range(n_sc):
            base = sc * PROMPT_SCOPE
            acc = jnp.zeros((PROMPT_SCOPE, LANES), F32)
            for j in range(CONV_WIDTH):
                a, r = divmod(pre - (CONV_WIDTH - 1) + j, SUBLANES)
                start = base + a * SUBLANES
                src = (u_scr[start:start + PROMPT_SCOPE, lanes] if r == 0
                       else u_shift[r - 1, start:start + PROMPT_SCOPE, lanes])
                acc = acc + src * conv_w_ref[j:j + 1, lanes]
            acc_blocks[sc][cb] = acc
    conv_pre = [jnp.concatenate(blocks, axis=1) + conv_b_ref[...] for blocks in acc_blocks]

    nc_ref[0] = u_scr[pre + tq - (CONV_WIDTH - 1):pre + tq, :]
    tail = u_scr[tq:tq + pre, :]
    u_scr[0:pre, :] = tail

    q, k, v = _in_proj_qkv(xb, w_in_ref, b_in_ref, w_sw_ref, b_sw_ref, rope_ref[...])
    qb = q.astype(BF16)
    kb = jnp.concatenate([kprev[...], k], axis=0).astype(BF16)
    vb = jnp.concatenate([vprev[...], v], axis=0).astype(BF16)
    row = lax.broadcasted_iota(jnp.int32, (WINDOW, 2 * WINDOW), 0)
    col = lax.broadcasted_iota(jnp.int32, (WINDOW, 2 * WINDOW), 1)
    dist = row + WINDOW - col
    band = (dist >= 0) & (dist < WINDOW)
    first_col = jnp.where(i > 0, 0, WINDOW)
    attn_blocks = []
    for j in range(tq // WINDOW):
        msk = (band & (col >= first_col)) if j == 0 else band
        heads = [None] * N_HEADS
        for c in range(N_KV_HEADS):
            kc = kb[j * WINDOW:(j + 2) * WINDOW, c * HEAD_DIM:(c + 1) * HEAD_DIM]
            vc = vb[j * WINDOW:(j + 2) * WINDOW, c * HEAD_DIM:(c + 1) * HEAD_DIM]
            qc = jnp.concatenate(
                [qb[j * WINDOW:(j + 1) * WINDOW, (c * Q_GROUP + g) * HEAD_DIM:(c * Q_GROUP + g + 1) * HEAD_DIM]
                 for g in range(Q_GROUP)], axis=0)
            s = lax.dot_general(qc, kc, (((1,), (1,)), ((), ())), preferred_element_type=F32)
            probs = []
            for g in range(Q_GROUP):
                sg = jnp.where(msk, s[g * WINDOW:(g + 1) * WINDOW], NEG_INF)
                p, _ = _softmax_rows_with_sink(sg, sinks_ref[c * Q_GROUP + g] * LOG2E)
                probs.append(p.astype(BF16))
            o = jnp.dot(jnp.concatenate(probs, axis=0), vc, preferred_element_type=F32)
            for g in range(Q_GROUP):
                heads[c * Q_GROUP + g] = o[g * WINDOW:(g + 1) * WINDOW]
        attn_blocks.append(jnp.concatenate(heads, axis=1))
    attn = jnp.concatenate(attn_blocks, axis=0)
    kprev[...] = k[tq - WINDOW:, :]
    vprev[...] = v[tq - WINDOW:, :]
    nk_ref[0] = k[tq - WINDOW:, :]
    nv_ref[0] = v[tq - WINDOW:, :]

    for sc in range(n_sc):
        rows = slice(sc * PROMPT_SCOPE, (sc + 1) * PROMPT_SCOPE)
        h = _mix_tail(x[rows], attn[rows], conv_pre[sc], lcg_ref[...], lcb_ref[...], w_out_ref, b_out_ref[...],
                      l1g_ref[...], l1b_ref[...])
        h_ref[0, rows, :] = h
        route, areas, counts = _route_and_sort(h, w_r_ref[...], PROMPT_SCOPE)
        route_ref[rows, :] = route
        xs_ref[sc * AREA_ROWS:(sc + 1) * AREA_ROWS, :] = areas[0]
        cnt_ref[sc] = jnp.broadcast_to(counts[0], (SUBLANES, LANES))


def _mix_sample_kernel(sinks_ref, x_ref, rope_ref, ck_ref, cv_ref, st_ref, w_in_ref, b_in_ref, w_sw_ref, b_sw_ref,
                       conv_w_ref, conv_b_ref, lcg_ref, lcb_ref, w_out_ref, b_out_ref, l1g_ref, l1b_ref,
                       w_r_ref,
                       h_ref, nk_ref, nv_ref, nc_ref, xs_ref, route_ref, cnt_ref,
                       h_scr):
    tb = x_ref.shape[0]
    n_tok = h_scr.shape[0]
    step = pl.program_id(0)
    x = x_ref[...]
    xb = x.astype(BF16)
    u = _in_proj_glu(xb, w_in_ref, b_in_ref)
    q, k_new, v_new = _in_proj_qkv(xb, w_in_ref, b_in_ref, w_sw_ref, b_sw_ref, rope_ref[...])

    qb = q.astype(BF16)
    knb = k_new.astype(BF16).astype(F32)
    vnb = v_new.astype(BF16).astype(F32)
    gq = SAMPLE_GRP * Q_GROUP
    nkeys = SAMPLE_GRP * WINDOW
    r_i = lax.broadcasted_iota(jnp.int32, (gq, nkeys), 0)
    c_i = lax.broadcasted_iota(jnp.int32, (gq, nkeys), 1)
    msk = ((c_i >> 7) == (r_i & (SAMPLE_GRP - 1))) & ((c_i & (WINDOW - 1)) >= 1)
    r_col = lax.broadcasted_iota(jnp.int32, (gq, 1), 0) >> 3
    attn_groups = []
    for grp in range(tb // SAMPLE_GRP):
        r0 = grp * SAMPLE_GRP
        kc_all = ck_ref[r0:r0 + SAMPLE_GRP].reshape(nkeys, KV_DIM).astype(BF16)
        vc_all = cv_ref[r0:r0 + SAMPLE_GRP].reshape(nkeys, KV_DIM).astype(BF16)
        heads = [None] * N_HEADS
        for c in range(N_KV_HEADS):
            lanes = slice(c * HEAD_DIM, (c + 1) * HEAD_DIM)
            qc = jnp.concatenate(
                [qb[r0:r0 + SAMPLE_GRP, (c * Q_GROUP + g) * HEAD_DIM:(c * Q_GROUP + g + 1) * HEAD_DIM]
                 for g in range(Q_GROUP)], axis=0)
            s = lax.dot_general(qc, kc_all[:, lanes], (((1,), (1,)), ((), ())), preferred_element_type=F32)
            s = jnp.where(msk, s, NEG_INF)
            kn = jnp.concatenate([knb[r0:r0 + SAMPLE_GRP, lanes]] * Q_GROUP, axis=0)
            vn = jnp.concatenate([vnb[r0:r0 + SAMPLE_GRP, lanes]] * Q_GROUP, axis=0)
            s_new = jnp.sum(qc.astype(F32) * kn, axis=-1, keepdims=True)
            sink = jnp.zeros((gq, 1), F32)
            for g in range(Q_GROUP):
                sink = jnp.where(r_col == g, sinks_ref[c * Q_GROUP + g] * LOG2E, sink)
            p, p_new = _softmax_rows_with_sink(s, sink, s_new)
            o = (jnp.dot(p.astype(BF16), vc_all[:, lanes], preferred_element_type=F32)
                 + p_new.astype(BF16).astype(F32) * vn)
            for g in range(Q_GROUP):
                heads[c * Q_GROUP + g] = o[g * SAMPLE_GRP:(g + 1) * SAMPLE_GRP]
        attn_groups.append(jnp.concatenate(heads, axis=1))
    attn = jnp.concatenate(attn_groups, axis=0)

    st = st_ref[...]
    acc = jnp.sum(st * conv_w_ref[0:CONV_WIDTH - 1, :][None], axis=1)
    conv_pre = acc + u * conv_w_ref[CONV_WIDTH - 1:CONV_WIDTH, :] + conv_b_ref[...]

    nk_ref[:, 0:WINDOW - 1, :] = ck_ref[:, 1:WINDOW, :]
    nv_ref[:, 0:WINDOW - 1, :] = cv_ref[:, 1:WINDOW, :]
    nc_ref[:, 0:CONV_WIDTH - 2, :] = st_ref[:, 1:CONV_WIDTH - 1, :]
    for b in range(tb):
        nk_ref[b, WINDOW - 1:WINDOW, :] = k_new[b:b + 1, :]
        nv_ref[b, WINDOW - 1:WINDOW, :] = v_new[b:b + 1, :]
        nc_ref[b, CONV_WIDTH - 2:CONV_WIDTH - 1, :] = u[b:b + 1, :]

    h = _mix_tail(x, attn, conv_pre, lcg_ref[...], lcb_ref[...], w_out_ref, b_out_ref[...],
                  l1g_ref[...], l1b_ref[...])
    h_ref[...] = h
    h_scr[pl.ds(pl.multiple_of(step * tb, tb), tb), :] = h

    @pl.when(step == pl.num_programs(0) - 1)
    def _():
        route, areas, counts = _route_and_sort(h_scr[...], w_r_ref[...], n_tok)
        route_ref[...] = route
        xs_ref[...] = areas[0]
        cnt_ref[0] = jnp.broadcast_to(counts[0], (SUBLANES, LANES))


GATHER_BUFS = 8
SCATTER_BUFS = 8
SCATTER_LAG = 2
ZERO_BITS = (64, 32, 16, 8, 4, 2, 1)
DUMP_CHUNKS = SCATTER_BUFS * BLOCK_CHUNKS
ZERO_ROWS = max(ZERO_BITS[0], DUMP_CHUNKS) * SUBLANES


def _moe_ffn_kernel(src_ref, dst_ref, bgen_ref, elist_ref, meta_ref, used_ref,
                    xs_hbm, w1_hbm, w3_hbm, w2_hbm,
                    ys_hbm,
                    xbuf, ybuf, hbuf, zbuf, wst1, wst3, wst2, w1b, w3b, w2b,
                    sem_in, sem_out, sem_zero, sem_dump, sem_w):
    n_areas = used_ref.shape[0]
    total = meta_ref[0]
    n_gen = meta_ref[1]
    dump_row0 = n_areas * AREA_ROWS
    lookahead = GATHER_BUFS - 1
    any_rows = pl.ds(0, SUBLANES)

    def chunk_at(ref, idx):
        return pl.ds(pl.multiple_of(ref[idx] * SUBLANES, SUBLANES), SUBLANES)

    def in_copy(rows, sl, c):
        return pltpu.make_async_copy(xs_hbm.at[rows], xbuf.at[sl, pl.ds(c * SUBLANES, SUBLANES)], sem_in.at[sl])

    def out_copy(rows, sl, c):
        return pltpu.make_async_copy(ybuf.at[sl, pl.ds(c * SUBLANES, SUBLANES)], ys_hbm.at[rows], sem_out.at[sl])

    def start_gather(blk):
        for c in range(BLOCK_CHUNKS):
            in_copy(chunk_at(src_ref, blk * BLOCK_CHUNKS + c), blk % GATHER_BUFS, c).start()

    def wait_gather(blk):
        for c in range(BLOCK_CHUNKS):
            in_copy(any_rows, blk % GATHER_BUFS, c).wait()

    def start_scatter(blk):
        for c in range(BLOCK_CHUNKS):
            out_copy(chunk_at(dst_ref, (blk + SCATTER_LAG) * BLOCK_CHUNKS + c), blk % SCATTER_BUFS, c).start()

    def wait_scatter(sl):
        for c in range(BLOCK_CHUNKS):
            out_copy(any_rows, sl, c).wait()

    def load_rows(blk):
        return _unpack_rows(xbuf[blk % GATHER_BUFS])

    def issue_dmas(blk):
        start_scatter(blk - SCATTER_LAG)
        start_gather(blk + lookahead)

    def up_gate(blk, rows, ws):
        x_lo, x_hi = rows
        a1 = (jnp.dot(x_lo, w1b[ws, 0:HALF, :], preferred_element_type=F32)
              + jnp.dot(x_hi, w1b[ws, HALF:, :], preferred_element_type=F32))
        a3 = (jnp.dot(x_lo, w3b[ws, 0:HALF, :], preferred_element_type=F32)
              + jnp.dot(x_hi, w3b[ws, HALF:, :], preferred_element_type=F32))
        hbuf[blk % SCATTER_BUFS] = (a1 * jax.nn.sigmoid(a1) * a3).astype(BF16)

    def down(blk, ws):
        y = jnp.dot(hbuf[blk % SCATTER_BUFS], w2b[ws], preferred_element_type=F32)
        ybuf[blk % SCATTER_BUFS] = _pack_rows(y.astype(BF16).astype(F32))

    def for_each_weight_copy(gen, fn):
        ws = gen % 2
        expert = elist_ref[gen]
        for hbm, staging in ((w1_hbm, wst1), (w3_hbm, wst3), (w2_hbm, wst2)):
            fn(pltpu.make_async_copy(hbm.at[expert], staging.at[ws], sem_w.at[ws]))

    def switch_weights(gen):
        ws = gen % 2
        for_each_weight_copy(gen, lambda cp: cp.wait())
        w1b[ws] = wst1[ws].astype(BF16)
        w3b[ws] = wst3[ws].astype(BF16)
        w2b[ws] = wst2[ws].astype(BF16)

        @pl.when(gen + 1 < n_gen)
        def _():
            for_each_weight_copy(gen + 1, lambda cp: cp.start())

    def for_each_tail_piece(fn):
        def area_body(s, carry):
            used = used_ref[s]
            tail = AREA_CHUNKS - used
            row = (s * AREA_CHUNKS + used) * SUBLANES
            for bit in ZERO_BITS:
                take = (tail & bit) != 0

                @pl.when(take)
                def _(row=row, bit=bit):
                    fn(pltpu.make_async_copy(zbuf.at[pl.ds(0, bit * SUBLANES)],
                                             ys_hbm.at[pl.ds(pl.multiple_of(row, SUBLANES), bit * SUBLANES)],
                                             sem_zero))
                row = row + jnp.where(take, bit * SUBLANES, 0)
            return carry
        lax.fori_loop(0, n_areas, area_body, 0)

    @pl.when(n_gen > 0)
    def _():
        for_each_weight_copy(0, lambda cp: cp.start())

    zbuf[...] = jnp.zeros(zbuf.shape, U32)
    ybuf[...] = jnp.zeros(ybuf.shape, U32)
    dump = pltpu.make_async_copy(zbuf.at[pl.ds(0, DUMP_CHUNKS * SUBLANES)],
                                 ys_hbm.at[pl.ds(dump_row0, DUMP_CHUNKS * SUBLANES)], sem_dump)
    dump.start()
    dump.wait()
    for_each_tail_piece(lambda cp: cp.start())
    for sl in range(SCATTER_BUFS - SCATTER_LAG):
        for c in range(BLOCK_CHUNKS):
            out_copy(pl.ds(dump_row0 + (sl * BLOCK_CHUNKS + c) * SUBLANES, SUBLANES), sl, c).start()
    for k in range(lookahead):
        start_gather(k)

    @pl.when(total > 0)
    def _():
        switch_weights(0)
        wait_gather(0)
        rows = load_rows(0)
        issue_dmas(0)
        up_gate(0, rows, 0)

        def block_body(blk, carry):
            gen = bgen_ref[blk]
            gen_prev = bgen_ref[blk - 1]

            @pl.when(gen != gen_prev)
            def _():
                switch_weights(gen)

            wait_gather(blk)
            wait_scatter((blk - 1) % SCATTER_BUFS)
            rows = load_rows(blk)
            issue_dmas(blk)
            down(blk - 1, gen_prev % 2)
            up_gate(blk, rows, gen % 2)
            return carry

        lax.fori_loop(1, total, block_body, 0)
        wait_scatter((total - 1) % SCATTER_BUFS)
        down(total - 1, bgen_ref[total - 1] % 2)

    for k in range(SCATTER_LAG):
        start_scatter(total - SCATTER_LAG + k)
    for sl in range(SCATTER_BUFS):
        wait_scatter(sl)
    for k in range(lookahead):
        wait_gather(total + k)
    for_each_tail_piece(lambda cp: cp.wait())


def _moe_combine_kernel(h_ref, route_ref, ys_ref, l2g_ref, l2b_ref, y_ref):
    n = h_ref.shape[0]
    scope = n // (ys_ref.shape[0] // AREA_ROWS)
    s_iota = lax.broadcasted_iota(jnp.int32, (scope, AREA_ROWS), 1).astype(F32)
    for sc in range(n // scope):
        rows = slice(sc * scope, (sc + 1) * scope)
        route = route_ref[rows, :]
        slot1 = route[:, 0:1]
        slot2 = route[:, 1:2]
        gate1 = route[:, 2:3]
        gate2 = route[:, 3:4]
        gmat = (jnp.where(s_iota == slot1, gate1, 0.0) + jnp.where(s_iota == slot2, gate2, 0.0)).astype(BF16)
        y_lo, y_hi = _unpack_rows(ys_ref[sc * AREA_ROWS:(sc + 1) * AREA_ROWS, :])
        f = jnp.concatenate([jnp.dot(gmat, y_lo, preferred_element_type=F32),
                             jnp.dot(gmat, y_hi, preferred_element_type=F32)], axis=1)
        y_ref[rows, :] = _layer_norm(ALPHA * h_ref[rows, :] + f, l2g_ref[...], l2b_ref[...])


def _excl_cumsum(a, axis):
    return jnp.cumsum(a, axis=axis) - a


def _plan_blocks(cnt, q_max):
    n_areas = cnt.shape[0]
    c8 = (cnt + (SUBLANES - 1)) // SUBLANES
    used8 = jnp.sum(c8, axis=1)
    base8 = jnp.arange(n_areas, dtype=jnp.int32) * AREA_CHUNKS
    src_runs = (base8[:, None] + _excl_cumsum(c8, 1)).T
    len_runs = c8.T
    tot = jnp.sum(len_runs, axis=1)
    ptot = (tot + BLOCK_CHUNKS - 1) // BLOCK_CHUNKS * BLOCK_CHUNKS
    pstart = _excl_cumsum(ptot, 0)
    dst_runs = pstart[:, None] + _excl_cumsum(len_runs, 1)
    off = (src_runs - dst_runs).reshape(-1)
    diff = off - jnp.concatenate([jnp.zeros((1,), jnp.int32), off[:-1]])
    q = jnp.arange(q_max, dtype=jnp.int32)
    src = q + jnp.sum(jnp.where(dst_runs.reshape(1, -1) <= q[:, None], diff[None, :], 0), axis=1)
    valid = jnp.any((pstart[None, :] <= q[:, None]) & (q[:, None] < (pstart + tot)[None, :]), axis=1)
    zero_chunk = AREA_CHUNKS - 1
    ahead = jnp.full(((GATHER_BUFS - 1) * BLOCK_CHUNKS,), zero_chunk, jnp.int32)
    src_tab = jnp.concatenate([jnp.where(valid, src, zero_chunk).astype(jnp.int32), ahead])
    qv = jnp.arange(-SCATTER_LAG * BLOCK_CHUNKS, q_max, dtype=jnp.int32)
    dump = n_areas * AREA_CHUNKS + ((qv // BLOCK_CHUNKS) % SCATTER_BUFS) * BLOCK_CHUNKS + qv % BLOCK_CHUNKS
    lead = jnp.zeros((SCATTER_LAG * BLOCK_CHUNKS,), jnp.int32)
    dst_tab = jnp.where(jnp.concatenate([lead > 0, valid]), jnp.concatenate([lead, src]), dump).astype(jnp.int32)
    nblk = ptot // BLOCK_CHUNKS
    bstart = pstart // BLOCK_CHUNKS
    has = nblk > 0
    gen_of = jnp.cumsum(has.astype(jnp.int32)) - 1
    experts = jnp.arange(N_EXPERTS, dtype=jnp.int32)
    elist = jnp.sum(jnp.where(has[None, :] & (gen_of[None, :] == experts[:, None]), experts[None, :], 0), axis=1)
    blk = jnp.arange(q_max // BLOCK_CHUNKS, dtype=jnp.int32)
    owner = (bstart[None, :] <= blk[:, None]) & (blk[:, None] < (bstart + nblk)[None, :])
    bgen = jnp.sum(jnp.where(owner, gen_of[None, :], 0), axis=1)
    meta = jnp.stack([jnp.sum(nblk), jnp.sum(has.astype(jnp.int32))])
    return (src_tab, dst_tab, bgen.astype(jnp.int32), elist.astype(jnp.int32), meta.astype(jnp.int32),
            used8.astype(jnp.int32))


def _rope_tables(pos):
    half = HEAD_DIM // 2
    inv = (ROPE_THETA ** (-np.arange(half, dtype=np.float64) / half)).astype(np.float32)
    ang = (np.asarray(pos, np.float32)[:, None] * inv).astype(np.float64)
    cos = np.cos(ang)
    sin = np.sin(ang)
    reps = LANES // HEAD_DIM
    cos_t = np.concatenate([cos, cos] * reps, axis=1)
    sin_t = np.concatenate([-sin, sin] * reps, axis=1)
    q_scale = ATTN_SCALE * LOG2E
    return jnp.asarray(np.concatenate([cos_t, sin_t, cos_t * q_scale, sin_t * q_scale], axis=1), F32)


def _full(shape):
    nd = len(shape)
    return pl.BlockSpec(shape, lambda *_: (0,) * nd)


def kernel(x_prompt, x_sample, cache_k, cache_v, state_conv, w_in, b_in, sinks, conv_w, conv_b, ln_conv_g,
           ln_conv_b, w_out, b_out, ln1_g, ln1_b, w_router_group, w_router_expert, w1, w3, w2, ln2_g, ln2_b):
    assert w_in.shape[0] == DEPTH
    bsz, seq, _ = x_prompt.shape
    dec_b, dec_t, _ = x_sample.shape
    assert dec_t == 1 and seq % TILE_Q == 0 and dec_b % SAMPLE_TB == 0 and 2 * dec_b + 7 * N_EXPERTS <= AREA_ROWS
    assert (bsz * seq) % COMBINE_TILE == 0
    n_prompt = bsz * seq
    tiles = seq // TILE_Q
    spt = TILE_Q // PROMPT_SCOPE
    p_areas = n_prompt // PROMPT_SCOPE
    n_areas = p_areas + spt
    q_max = (n_areas * AREA_CHUNKS + N_EXPERTS * (BLOCK_CHUNKS - 1) + BLOCK_CHUNKS - 1) // BLOCK_CHUNKS * BLOCK_CHUNKS

    row = lambda a: a.reshape(1, -1)
    qk = ATTN_DIM + KV_DIM
    half = HEAD_DIM // 2

    def swap_halves(a):
        parts = a[..., :qk].reshape(a.shape[:-1] + (qk // HEAD_DIM, 2, half))
        return parts[..., ::-1, :].reshape(a.shape[:-1] + (qk,))

    w_in_b = w_in[0].astype(BF16)
    w_sw_b = swap_halves(w_in[0]).astype(BF16)
    w_out_b = w_out[0].astype(BF16)
    w_r = jnp.concatenate([w_router_group[0], w_router_expert[0].reshape(D_MODEL, N_EXPERTS),
                           jnp.zeros((D_MODEL, LANES - ELANE0 - N_EXPERTS), F32)], axis=1).astype(BF16)
    shared = (w_in_b, row(b_in[0]), w_sw_b, row(swap_halves(b_in[0])), conv_w[0], row(conv_b[0]),
              row(ln_conv_g[0]), row(ln_conv_b[0]), w_out_b, row(b_out[0]), row(ln1_g[0]), row(ln1_b[0]), w_r)
    shared_specs = [_full(a.shape) for a in shared]

    def glu_blocks(a):
        lo = ATTN_DIM + 2 * KV_DIM
        vg = a[..., lo:lo + 2 * CONV_CH].reshape(a.shape[:-1] + (2, CONV_CH // LANES, LANES))
        return jnp.swapaxes(vg, -3, -2).reshape(a.shape[:-1] + (2 * CONV_CH,))

    glu = (glu_blocks(w_in[0]).astype(BF16), row(glu_blocks(b_in[0])))
    glu_specs = [_full(a.shape) for a in glu]
    smem = pl.BlockSpec(memory_space=pltpu.SMEM)
    rope_p = _rope_tables(np.arange(seq))
    rope_s = _rope_tables(PAST_LEN + np.arange(dec_t))

    cparams = lambda sem: pltpu.CompilerParams(dimension_semantics=sem, vmem_limit_bytes=VMEM_LIMIT)

    ck = cache_k[0].reshape(dec_b, WINDOW, KV_DIM)
    cv = cache_v[0].reshape(dec_b, WINDOW, KV_DIM)
    tb = SAMPLE_TB
    conv_state_spec = pl.BlockSpec((None, tb, CONV_WIDTH - 1, CONV_CH), lambda t: (0, t, 0, 0))
    h_s, nk_s, nv_s, nc_s, xs_s, route_s, cnt_s = pl.pallas_call(
        _mix_sample_kernel,
        grid=(dec_b // tb,),
        in_specs=[smem,
                  pl.BlockSpec((tb, None, D_MODEL), lambda t: (t, 0, 0)),
                  _full(rope_s.shape),
                  pl.BlockSpec((tb, WINDOW, KV_DIM), lambda t: (t, 0, 0)),
                  pl.BlockSpec((tb, WINDOW, KV_DIM), lambda t: (t, 0, 0)),
                  conv_state_spec] + shared_specs,
        out_specs=[pl.BlockSpec((tb, D_MODEL), lambda t: (t, 0)),
                   pl.BlockSpec((tb, WINDOW, KV_DIM), lambda t: (t, 0, 0)),
                   pl.BlockSpec((tb, WINDOW, KV_DIM), lambda t: (t, 0, 0)),
                   conv_state_spec,
                   _full((AREA_ROWS, HALF)),
                   _full((dec_b, LANES)),
                   _full((1, SUBLANES, LANES))],
        out_shape=[jax.ShapeDtypeStruct((dec_b, D_MODEL), F32),
                   jax.ShapeDtypeStruct((dec_b, WINDOW, KV_DIM), F32),
                   jax.ShapeDtypeStruct((dec_b, WINDOW, KV_DIM), F32),
                   jax.ShapeDtypeStruct((DEPTH, dec_b, CONV_WIDTH - 1, CONV_CH), F32),
                   jax.ShapeDtypeStruct((AREA_ROWS, HALF), U32),
                   jax.ShapeDtypeStruct((dec_b, LANES), F32),
                   jax.ShapeDtypeStruct((1, SUBLANES, LANES), F32)],
        scratch_shapes=[pltpu.VMEM((dec_b, D_MODEL), F32)],
        compiler_params=cparams(("arbitrary",)),
        name="mix_sample",
    )(sinks[0], x_sample, rope_s, ck, cv, state_conv, *shared)

    n_real = bsz * tiles
    real = lambda t: jnp.minimum(t, n_real - 1)
    h_p, xs, route_p, cnt_p, nk_p, nv_p, nc_p = pl.pallas_call(
        functools.partial(_mix_prompt_kernel, tiles=tiles),
        grid=(n_real + 1,),
        in_specs=[smem,
                  pl.BlockSpec((1, TILE_Q, D_MODEL), lambda t: (real(t) // tiles, real(t) % tiles, 0)),
                  pl.BlockSpec((TILE_Q, 4 * LANES), lambda t: (real(t) % tiles, 0)),
                  _full((AREA_ROWS, HALF))] + glu_specs + shared_specs,
        out_specs=[pl.BlockSpec((1, TILE_Q, D_MODEL), lambda t: (real(t) // tiles, real(t) % tiles, 0)),
                   pl.BlockSpec((spt * AREA_ROWS, HALF), lambda t: (t, 0)),
                   pl.BlockSpec((TILE_Q, LANES), lambda t: (real(t), 0)),
                   pl.BlockSpec((spt, SUBLANES, LANES), lambda t: (t, 0, 0)),
                   pl.BlockSpec((1, WINDOW, KV_DIM), lambda t: (real(t) // tiles, 0, 0)),
                   pl.BlockSpec((1, WINDOW, KV_DIM), lambda t: (real(t) // tiles, 0, 0)),
                   pl.BlockSpec((1, CONV_WIDTH - 1, CONV_CH), lambda t: (real(t) // tiles, 0, 0))],
        out_shape=[jax.ShapeDtypeStruct((bsz, seq, D_MODEL), F32),
                   jax.ShapeDtypeStruct((n_areas * AREA_ROWS, HALF), U32),
                   jax.ShapeDtypeStruct((n_prompt, LANES), F32),
                   jax.ShapeDtypeStruct((n_areas, SUBLANES, LANES), F32),
                   jax.ShapeDtypeStruct((bsz, WINDOW, KV_DIM), F32),
                   jax.ShapeDtypeStruct((bsz, WINDOW, KV_DIM), F32),
                   jax.ShapeDtypeStruct((bsz, CONV_WIDTH - 1, CONV_CH), F32)],
        scratch_shapes=[pltpu.VMEM((WINDOW, KV_DIM), F32), pltpu.VMEM((WINDOW, KV_DIM), F32),
                        pltpu.VMEM((TILE_Q + CONV_PRE, CONV_CH), F32),
                        pltpu.VMEM((SUBLANES - 1, TILE_Q + CONV_SHIFT_EXTRA, CONV_CH), F32)],
        compiler_params=cparams(("arbitrary",)),
        name="mix_prompt",
    )(sinks[0], x_prompt, rope_p, xs_s, *glu, *shared)

    cnt = jnp.concatenate([cnt_p[:p_areas, 0], cnt_s[:, 0], cnt_p[p_areas + 1:, 0]], axis=0)
    cnt = cnt[:, ELANE0:ELANE0 + N_EXPERTS].astype(jnp.int32)
    plan = _plan_blocks(cnt, q_max)
    hbm = pl.BlockSpec(memory_space=pl.ANY)
    up_shape, down_shape = (2, D_MODEL, D_FF_EXPERT), (2, D_FF_EXPERT, D_MODEL)
    ys = pl.pallas_call(
        _moe_ffn_kernel,
        grid_spec=pltpu.PrefetchScalarGridSpec(
            num_scalar_prefetch=len(plan),
            grid=(1,),
            in_specs=[hbm, hbm, hbm, hbm],
            out_specs=hbm,
            scratch_shapes=[pltpu.VMEM((GATHER_BUFS, BLOCK_ROWS, HALF), U32),
                            pltpu.VMEM((SCATTER_BUFS, BLOCK_ROWS, HALF), U32),
                            pltpu.VMEM((SCATTER_BUFS, BLOCK_ROWS, D_FF_EXPERT), BF16),
                            pltpu.VMEM((ZERO_ROWS, HALF), U32),
                            pltpu.VMEM(up_shape, F32), pltpu.VMEM(up_shape, F32), pltpu.VMEM(down_shape, F32),
                            pltpu.VMEM(up_shape, BF16), pltpu.VMEM(up_shape, BF16), pltpu.VMEM(down_shape, BF16),
                            pltpu.SemaphoreType.DMA((GATHER_BUFS,)), pltpu.SemaphoreType.DMA((SCATTER_BUFS,)),
                            pltpu.SemaphoreType.DMA(()), pltpu.SemaphoreType.DMA(()),
                            pltpu.SemaphoreType.DMA((2,))]),
        out_shape=jax.ShapeDtypeStruct((n_areas * AREA_ROWS + DUMP_CHUNKS * SUBLANES, HALF), U32),
        compiler_params=cparams(("arbitrary",)),
        name="moe_ffn",
    )(*plan, xs, w1[0], w3[0], w2[0])

    def combine(h2d, route, first_area, scope, tile):
        n = h2d.shape[0]
        apt = tile // scope
        return pl.pallas_call(
            _moe_combine_kernel,
            grid=(n // tile,),
            in_specs=[pl.BlockSpec((tile, D_MODEL), lambda t: (t, 0)),
                      pl.BlockSpec((tile, LANES), lambda t: (t, 0)),
                      pl.BlockSpec((apt * AREA_ROWS, HALF), lambda t: (first_area // apt + t, 0)),
                      _full((1, D_MODEL)), _full((1, D_MODEL))],
            out_specs=pl.BlockSpec((tile, D_MODEL), lambda t: (t, 0)),
            out_shape=jax.ShapeDtypeStruct((n, D_MODEL), F32),
            compiler_params=cparams(("arbitrary",)),
            name="moe_combine",
        )(h2d, route, ys, row(ln2_g[0]), row(ln2_b[0]))

    y_p = combine(h_p.reshape(n_prompt, D_MODEL), route_p, 0, PROMPT_SCOPE, COMBINE_TILE)
    y_s = combine(h_s, route_s, p_areas, dec_b, dec_b)

    kv_shape = lambda n: (DEPTH, n, WINDOW, N_KV_HEADS, HEAD_DIM)
    return (y_p.reshape(bsz, seq, D_MODEL), y_s.reshape(dec_b, dec_t, D_MODEL),
            nk_p.reshape(kv_shape(bsz)), nv_p.reshape(kv_shape(bsz)), nc_p[None],
            nk_s.reshape(kv_shape(dec_b)), nv_s.reshape(kv_shape(dec_b)), nc_s)
```

```python
import functools

import jax
import jax.numpy as jnp
import numpy as np
from jax import lax
from jax.experimental import pallas as pl
from jax.experimental.pallas import tpu as pltpu

F32 = jnp.float32
BF16 = jnp.bfloat16
U32 = jnp.uint32

D_MODEL = 1024
N_HEADS = 8
N_KV_HEADS = 2
HEAD_DIM = 64
Q_GROUP = N_HEADS // N_KV_HEADS
ATTN_DIM = N_HEADS * HEAD_DIM
KV_DIM = N_KV_HEADS * HEAD_DIM
WINDOW = 128
ROPE_THETA = 10000.0
ATTN_SCALE = HEAD_DIM ** -0.5
CONV_CH = D_MODEL - ATTN_DIM
CONV_WIDTH = 31
IN_DIM = ATTN_DIM + 2 * KV_DIM + 2 * CONV_CH
N_GROUPS = 4
EXPERTS_PER_GROUP = 8
N_EXPERTS = N_GROUPS * EXPERTS_PER_GROUP
D_FF_EXPERT = 256
LN_EPS = 1e-5
NEG_INF = -1e30
DEPTH = 1
ALPHA = (2.0 * DEPTH) ** 0.25
PAST_LEN = 16384
LOG2E = 1.4426950408889634

LANES = 128
SUBLANES = 8
HALF = D_MODEL // 2
ELANE0 = N_GROUPS
PROMPT_SCOPE = 256
AREA_ROWS = -(-(2 * PROMPT_SCOPE + (SUBLANES - 1) * N_EXPERTS) // LANES) * LANES
AREA_CHUNKS = AREA_ROWS // SUBLANES
BLOCK_CHUNKS = 32
BLOCK_ROWS = BLOCK_CHUNKS * SUBLANES
TILE_Q = 512
COMBINE_TILE = 1024
CONV_PRE = 32
CONV_SHIFT_EXTRA = 24
SAMPLE_TB = 32
SAMPLE_GRP = 8
VMEM_LIMIT = 56 * 1024 * 1024


def _layer_norm(x, g, b):
    mu = jnp.mean(x, axis=-1, keepdims=True)
    xc = x - mu
    var = jnp.mean(xc * xc, axis=-1, keepdims=True)
    return xc * lax.rsqrt(var + LN_EPS) * g + b


def _rope(x, x_swapped, cos, sin_signed):
    pieces = [x[:, g * LANES:(g + 1) * LANES] * cos + x_swapped[:, g * LANES:(g + 1) * LANES] * sin_signed
              for g in range(x.shape[-1] // LANES)]
    return pieces[0] if len(pieces) == 1 else jnp.concatenate(pieces, axis=-1)


def _in_proj_glu(xb, w_in_ref, b_in_ref):
    lo, hi = ATTN_DIM + 2 * KV_DIM, IN_DIM
    cacg = jnp.dot(xb, w_in_ref[:, lo:hi], preferred_element_type=F32) + b_in_ref[:, lo:hi]
    return cacg[:, :CONV_CH] * jax.nn.sigmoid(cacg[:, CONV_CH:])


def _in_proj_qkv(xb, w_in_ref, b_in_ref, w_sw_ref, b_sw_ref, rope):
    o_v, o_e = ATTN_DIM + KV_DIM, ATTN_DIM + 2 * KV_DIM
    qkv = jnp.dot(xb, w_in_ref[:, 0:o_e], preferred_element_type=F32) + b_in_ref[:, 0:o_e]
    sw = jnp.dot(xb, w_sw_ref[...], preferred_element_type=F32) + b_sw_ref[...]
    k = _rope(qkv[:, ATTN_DIM:o_v], sw[:, ATTN_DIM:], rope[:, 0:LANES], rope[:, LANES:2 * LANES])
    q = _rope(qkv[:, :ATTN_DIM], sw[:, :ATTN_DIM], rope[:, 2 * LANES:3 * LANES], rope[:, 3 * LANES:])
    return q, k, qkv[:, o_v:]


def _pack_rows(v):
    lo = lax.bitcast_convert_type(v[:, :HALF], U32)
    hi = lax.bitcast_convert_type(v[:, HALF:], U32)
    return (lo >> 16) | (hi & jnp.uint32(0xFFFF0000))


def _unpack_rows(w):
    lo = lax.bitcast_convert_type(w << 16, F32).astype(BF16)
    hi = lax.bitcast_convert_type(w & jnp.uint32(0xFFFF0000), F32).astype(BF16)
    return lo, hi


def _drain(stages):
    try:
        while True:
            next(stages)
    except StopIteration as stop:
        return stop.value


def _round_robin(*stage_generators):
    live = list(stage_generators)
    while live:
        for stages in list(live):
            try:
                next(stages)
            except StopIteration:
                live.remove(stages)


def _route_and_sort_stages(h, w_r):
    n = h.shape[0]
    hb = h.astype(BF16)
    logits = jnp.dot(hb, w_r, preferred_element_type=F32)
    yield
    lane = lax.broadcasted_iota(jnp.int32, (n, LANES), 1)
    lane_f = lane.astype(F32)
    is_g = lane < N_GROUPS
    lg = jnp.where(is_g, logits, -jnp.inf)
    gmax = jnp.max(lg, axis=-1, keepdims=True)
    yield
    gidx = jnp.min(jnp.where(lg == gmax, lane_f, float(LANES)), axis=-1, keepdims=True)
    p_g = 1.0 / jnp.sum(jnp.where(is_g, jnp.exp(logits - gmax), 0.0), axis=-1, keepdims=True)
    yield
    egrp = ((lane - ELANE0) >> (EXPERTS_PER_GROUP.bit_length() - 1)).astype(F32)
    emask = (lane >= ELANE0) & (lane < ELANE0 + N_EXPERTS) & (egrp == gidx)
    sel = jnp.where(emask, logits, -jnp.inf)
    v1 = jnp.max(sel, axis=-1, keepdims=True)
    yield
    i1 = jnp.min(jnp.where(sel == v1, lane_f, float(LANES)), axis=-1, keepdims=True)
    yield
    sel2 = jnp.where(lane_f == i1, -jnp.inf, sel)
    v2 = jnp.max(sel2, axis=-1, keepdims=True)
    yield
    i2 = jnp.min(jnp.where(sel2 == v2, lane_f, float(LANES)), axis=-1, keepdims=True)
    t = jnp.exp(v2 - v1)
    gate1 = p_g / (1.0 + t)
    gate2 = p_g * t / (1.0 + t)
    pick1 = lane_f == i1
    pick2 = lane_f == i2
    member = jnp.where(pick1 | pick2, 1.0, 0.0)
    yield

    r_i = lax.broadcasted_iota(jnp.int32, (n, n), 0)
    c_i = lax.broadcasted_iota(jnp.int32, (n, n), 1)
    lower = jnp.where(c_i < r_i, 1.0, 0.0).astype(BF16)
    e_r = lax.broadcasted_iota(jnp.int32, (LANES, LANES), 0)
    e_c = lax.broadcasted_iota(jnp.int32, (LANES, LANES), 1)
    upper = jnp.where(e_r < e_c, 1.0, 0.0).astype(BF16)
    before = jnp.dot(lower, member.astype(BF16), preferred_element_type=F32)
    cnt = jnp.sum(member, axis=0, keepdims=True)
    c8 = jnp.floor((cnt + (SUBLANES - 1)) * (1.0 / SUBLANES))
    c8b = jnp.broadcast_to(c8, (SUBLANES, LANES)).astype(BF16)
    off8 = jnp.dot(c8b, upper, preferred_element_type=F32)[0:1]
    yield
    slot_all = before + off8 * float(SUBLANES)
    slot1 = jnp.sum(jnp.where(pick1, slot_all, 0.0), axis=-1, keepdims=True)
    slot2 = jnp.sum(jnp.where(pick2, slot_all, 0.0), axis=-1, keepdims=True)
    yield
    s_iota = lax.broadcasted_iota(jnp.int32, (n, AREA_ROWS), 1).astype(F32)
    perm_t = jnp.where((s_iota == slot1) | (s_iota == slot2), 1.0, 0.0).astype(BF16)
    sorted_rows = lax.dot_general(perm_t, hb, (((0,), (0,)), ((), ())),
                                  preferred_element_type=F32)
    yield
    route = jnp.where(lane == 0, slot1, jnp.where(lane == 1, slot2, jnp.where(lane == 2, gate1,
                      jnp.where(lane == 3, gate2, 0.0))))
    return route, _pack_rows(sorted_rows), cnt


def _softmax_rows_with_sink(s, sink, extra=None):
    m = jnp.maximum(jnp.max(s, axis=-1, keepdims=True), sink)
    if extra is not None:
        m = jnp.maximum(m, extra)
    e = jnp.exp2(s - m)
    den = jnp.sum(e, axis=-1, keepdims=True) + jnp.exp2(sink - m)
    if extra is not None:
        ee = jnp.exp2(extra - m)
        den = den + ee
        return e / den, ee / den
    return e / den, None


def _mix_out_stages(x, attn, conv_pre, lcg, lcb, w_out_ref, b_out, l1g, l1b):
    conv = _layer_norm(conv_pre, lcg, lcb)
    conv = conv * jax.nn.sigmoid(conv)
    yield
    mix = (jnp.dot(attn.astype(BF16), w_out_ref[0:ATTN_DIM, :], preferred_element_type=F32)
           + jnp.dot(conv.astype(BF16), w_out_ref[ATTN_DIM:, :], preferred_element_type=F32) + b_out)
    yield
    h = _layer_norm(ALPHA * x + mix, l1g, l1b)
    yield
    return h


N_PROMPT_WEIGHTS = 15


def _mix_prompt_kernel(sinks_ref, x_ref, rope_ref, xs_s_ref, *rest, tiles):
    _, xs_ref, _, cnt_ref = rest[N_PROMPT_WEIGHTS:N_PROMPT_WEIGHTS + 4]
    step = pl.program_id(0)
    n_real = pl.num_programs(0) - 1

    @pl.when(step < n_real)
    def _():
        _mix_prompt_tile(sinks_ref, x_ref, rope_ref, *rest, i=step % tiles)

    @pl.when(step == n_real)
    def _():
        xs_ref[0:AREA_ROWS, :] = xs_s_ref[...]
        xs_ref[AREA_ROWS:, :] = jnp.zeros((xs_ref.shape[0] - AREA_ROWS, HALF), U32)
        cnt_ref[...] = jnp.zeros(cnt_ref.shape, F32)


def _mix_prompt_tile(sinks_ref, x_ref, rope_ref, w_glu_ref, b_glu_ref,
                     w_in_ref, b_in_ref, w_sw_ref, b_sw_ref, conv_w_ref, conv_b_ref,
                     lcg_ref, lcb_ref, w_out_ref, b_out_ref, l1g_ref, l1b_ref, w_r_ref,
                     h_ref, xs_ref, route_ref, cnt_ref, nk_ref, nv_ref, nc_ref,
                     kprev, vprev, u_scr, u_shift, *, i):
    tq = x_ref.shape[1]
    pre = CONV_PRE
    n_sc = tq // PROMPT_SCOPE

    @pl.when(i == 0)
    def _():
        kprev[...] = jnp.zeros_like(kprev)
        vprev[...] = jnp.zeros_like(vprev)
        u_scr[0:pre, :] = jnp.zeros((pre, CONV_CH), F32)

    x = x_ref[0]
    xb = x.astype(BF16)

    def glu_block(cb):
        cols = slice(2 * LANES * cb, 2 * LANES * (cb + 1))
        vg = jnp.dot(xb, w_glu_ref[:, cols], preferred_element_type=F32) + b_glu_ref[:, cols]
        return vg[:, :LANES] * jax.nn.sigmoid(vg[:, LANES:])

    n_cb = CONV_CH // LANES
    acc_blocks = [[None] * n_cb for _ in range(n_sc)]
    u_next = glu_block(0)
    for cb in range(n_cb):
        lanes = slice(LANES * cb, LANES * (cb + 1))
        u_scr[pre:pre + tq, lanes] = u_next
        if cb + 1 < n_cb:
            u_next = glu_block(cb + 1)
        for r in range(1, SUBLANES):
            u_shift[r - 1, :, lanes] = u_scr[r:r + tq + CONV_SHIFT_EXTRA, lanes]
        for sc in range(n_sc):
            base = sc * PROMPT_SCOPE
            acc = jnp.zeros((PROMPT_SCOPE, LANES), F32)
            for j in range(CONV_WIDTH):
                a, r = divmod(pre - (CONV_WIDTH - 1) + j, SUBLANES)
                start = base + a * SUBLANES
                src = (u_scr[start:start + PROMPT_SCOPE, lanes] if r == 0
                       else u_shift[r - 1, start:start + PROMPT_SCOPE, lanes])
                acc = acc + src * conv_w_ref[j:j + 1, lanes]
            acc_blocks[sc][cb] = acc
    conv_pre = [jnp.concatenate(blocks, axis=1) + conv_b_ref[...] for blocks in acc_blocks]

    nc_ref[0] = u_scr[pre + tq - (CONV_WIDTH - 1):pre + tq, :]
    tail = u_scr[tq:tq + pre, :]
    u_scr[0:pre, :] = tail

    q, k, v = _in_proj_qkv(xb, w_in_ref, b_in_ref, w_sw_ref, b_sw_ref, rope_ref[...])
    qb = q.astype(BF16)
    kb = jnp.concatenate([kprev[...], k], axis=0).astype(BF16)
    vb = jnp.concatenate([vprev[...], v], axis=0).astype(BF16)
    row = lax.broadcasted_iota(jnp.int32, (WINDOW, 2 * WINDOW), 0)
    col = lax.broadcasted_iota(jnp.int32, (WINDOW, 2 * WINDOW), 1)
    dist = row + WINDOW - col
    band = (dist >= 0) & (dist < WINDOW)
    first_col = jnp.where(i > 0, 0, WINDOW)
    attn_blocks = []
    for j in range(tq // WINDOW):
        msk = (band & (col >= first_col)) if j == 0 else band
        heads = [None] * N_HEADS
        for c in range(N_KV_HEADS):
            kc = kb[j * WINDOW:(j + 2) * WINDOW, c * HEAD_DIM:(c + 1) * HEAD_DIM]
            vc = vb[j * WINDOW:(j + 2) * WINDOW, c * HEAD_DIM:(c + 1) * HEAD_DIM]
            qc = jnp.concatenate(
                [qb[j * WINDOW:(j + 1) * WINDOW, (c * Q_GROUP + g) * HEAD_DIM:(c * Q_GROUP + g + 1) * HEAD_DIM]
                 for g in range(Q_GROUP)], axis=0)
            s = lax.dot_general(qc, kc, (((1,), (1,)), ((), ())), preferred_element_type=F32)
            probs = []
            for g in range(Q_GROUP):
                sg = jnp.where(msk, s[g * WINDOW:(g + 1) * WINDOW], NEG_INF)
                p, _ = _softmax_rows_with_sink(sg, sinks_ref[c * Q_GROUP + g] * LOG2E)
                probs.append(p.astype(BF16))
            o = jnp.dot(jnp.concatenate(probs, axis=0), vc, preferred_element_type=F32)
            for g in range(Q_GROUP):
                heads[c * Q_GROUP + g] = o[g * WINDOW:(g + 1) * WINDOW]
        attn_blocks.append(jnp.concatenate(heads, axis=1))
    attn = jnp.concatenate(attn_blocks, axis=0)
    kprev[...] = k[tq - WINDOW:, :]
    vprev[...] = v[tq - WINDOW:, :]
    nk_ref[0] = k[tq - WINDOW:, :]
    nv_ref[0] = v[tq - WINDOW:, :]

    def scope_tail(sc):
        rows = slice(sc * PROMPT_SCOPE, (sc + 1) * PROMPT_SCOPE)
        h = yield from _mix_out_stages(x[rows], attn[rows], conv_pre[sc], lcg_ref[...], lcb_ref[...],
                                       w_out_ref, b_out_ref[...], l1g_ref[...], l1b_ref[...])
        route, area, cnt = yield from _route_and_sort_stages(h, w_r_ref[...])
        h_ref[0, rows, :] = h
        route_ref[rows, :] = route
        xs_ref[sc * AREA_ROWS:(sc + 1) * AREA_ROWS, :] = area
        cnt_ref[sc] = jnp.broadcast_to(cnt, (SUBLANES, LANES))

    _round_robin(*[scope_tail(sc) for sc in range(n_sc)])


def _mix_sample_kernel(sinks_ref, x_ref, rope_ref, ck_ref, cv_ref, st_ref, w_in_ref, b_in_ref, w_sw_ref, b_sw_ref,
                       conv_w_ref, conv_b_ref, lcg_ref, lcb_ref, w_out_ref, b_out_ref, l1g_ref, l1b_ref,
                       w_r_ref,
                       h_ref, nk_ref, nv_ref, nc_ref, xs_ref, route_ref, cnt_ref,
                       h_scr):
    tb = x_ref.shape[0]
    n_tok = h_scr.shape[0]
    step = pl.program_id(0)
    x = x_ref[...]
    xb = x.astype(BF16)
    u = _in_proj_glu(xb, w_in_ref, b_in_ref)
    q, k_new, v_new = _in_proj_qkv(xb, w_in_ref, b_in_ref, w_sw_ref, b_sw_ref, rope_ref[...])

    qb = q.astype(BF16)
    knb = k_new.astype(BF16).astype(F32)
    vnb = v_new.astype(BF16).astype(F32)
    gq = SAMPLE_GRP * Q_GROUP
    nkeys = SAMPLE_GRP * WINDOW
    r_i = lax.broadcasted_iota(jnp.int32, (gq, nkeys), 0)
    c_i = lax.broadcasted_iota(jnp.int32, (gq, nkeys), 1)
    window_shift, grp_shift = WINDOW.bit_length() - 1, SAMPLE_GRP.bit_length() - 1
    msk = ((c_i >> window_shift) == (r_i & (SAMPLE_GRP - 1))) & ((c_i & (WINDOW - 1)) >= 1)
    r_col = lax.broadcasted_iota(jnp.int32, (gq, 1), 0) >> grp_shift
    attn_groups = []
    for grp in range(tb // SAMPLE_GRP):
        r0 = grp * SAMPLE_GRP
        kc_all = ck_ref[r0:r0 + SAMPLE_GRP].reshape(nkeys, KV_DIM).astype(BF16)
        vc_all = cv_ref[r0:r0 + SAMPLE_GRP].reshape(nkeys, KV_DIM).astype(BF16)
        heads = [None] * N_HEADS
        for c in range(N_KV_HEADS):
            lanes = slice(c * HEAD_DIM, (c + 1) * HEAD_DIM)
            qc = jnp.concatenate(
                [qb[r0:r0 + SAMPLE_GRP, (c * Q_GROUP + g) * HEAD_DIM:(c * Q_GROUP + g + 1) * HEAD_DIM]
                 for g in range(Q_GROUP)], axis=0)
            s = lax.dot_general(qc, kc_all[:, lanes], (((1,), (1,)), ((), ())), preferred_element_type=F32)
            s = jnp.where(msk, s, NEG_INF)
            kn = jnp.concatenate([knb[r0:r0 + SAMPLE_GRP, lanes]] * Q_GROUP, axis=0)
            vn = jnp.concatenate([vnb[r0:r0 + SAMPLE_GRP, lanes]] * Q_GROUP, axis=0)
            s_new = jnp.sum(qc.astype(F32) * kn, axis=-1, keepdims=True)
            sink = jnp.zeros((gq, 1), F32)
            for g in range(Q_GROUP):
                sink = jnp.where(r_col == g, sinks_ref[c * Q_GROUP + g] * LOG2E, sink)
            p, p_new = _softmax_rows_with_sink(s, sink, s_new)
            o = (jnp.dot(p.astype(BF16), vc_all[:, lanes], preferred_element_type=F32)
                 + p_new.astype(BF16).astype(F32) * vn)
            for g in range(Q_GROUP):
                heads[c * Q_GROUP + g] = o[g * SAMPLE_GRP:(g + 1) * SAMPLE_GRP]
        attn_groups.append(jnp.concatenate(heads, axis=1))
    attn = jnp.concatenate(attn_groups, axis=0)

    st = st_ref[...]
    acc = jnp.sum(st * conv_w_ref[0:CONV_WIDTH - 1, :][None], axis=1)
    conv_pre = acc + u * conv_w_ref[CONV_WIDTH - 1:CONV_WIDTH, :] + conv_b_ref[...]

    nk_ref[:, 0:WINDOW - 1, :] = ck_ref[:, 1:WINDOW, :]
    nv_ref[:, 0:WINDOW - 1, :] = cv_ref[:, 1:WINDOW, :]
    nc_ref[:, 0:CONV_WIDTH - 2, :] = st_ref[:, 1:CONV_WIDTH - 1, :]
    for b in range(tb):
        nk_ref[b, WINDOW - 1:WINDOW, :] = k_new[b:b + 1, :]
        nv_ref[b, WINDOW - 1:WINDOW, :] = v_new[b:b + 1, :]
        nc_ref[b, CONV_WIDTH - 2:CONV_WIDTH - 1, :] = u[b:b + 1, :]

    h = _drain(_mix_out_stages(x, attn, conv_pre, lcg_ref[...], lcb_ref[...], w_out_ref, b_out_ref[...],
                               l1g_ref[...], l1b_ref[...]))
    h_ref[...] = h
    h_scr[pl.ds(pl.multiple_of(step * tb, tb), tb), :] = h

    @pl.when(step == pl.num_programs(0) - 1)
    def _():
        route, area, cnt = _drain(_route_and_sort_stages(h_scr[...], w_r_ref[...]))
        route_ref[...] = route
        xs_ref[...] = area
        cnt_ref[0] = jnp.broadcast_to(cnt, (SUBLANES, LANES))


GATHER_BUFS = 8
SCATTER_BUFS = 8
SCATTER_LAG = 2
ZERO_BITS = (64, 32, 16, 8, 4, 2, 1)
DUMP_CHUNKS = SCATTER_BUFS * BLOCK_CHUNKS
ZERO_ROWS = max(ZERO_BITS[0], DUMP_CHUNKS) * SUBLANES


def _moe_ffn_kernel(src_ref, dst_ref, bgen_ref, elist_ref, meta_ref, used_ref,
                    xs_hbm, w1_hbm, w3_hbm, w2_hbm,
                    ys_hbm,
                    xbuf, ybuf, hbuf, zbuf, wst1, wst3, wst2, w1b, w3b, w2b,
                    sem_in, sem_out, sem_zero, sem_dump, sem_w):
    n_areas = used_ref.shape[0]
    total = meta_ref[0]
    n_gen = meta_ref[1]
    dump_row0 = n_areas * AREA_ROWS
    lookahead = GATHER_BUFS - 1
    any_rows = pl.ds(0, SUBLANES)

    def chunk_at(ref, idx):
        return pl.ds(pl.multiple_of(ref[idx] * SUBLANES, SUBLANES), SUBLANES)

    def in_copy(rows, sl, c):
        return pltpu.make_async_copy(xs_hbm.at[rows], xbuf.at[sl, pl.ds(c * SUBLANES, SUBLANES)], sem_in.at[sl])

    def out_copy(rows, sl, c):
        return pltpu.make_async_copy(ybuf.at[sl, pl.ds(c * SUBLANES, SUBLANES)], ys_hbm.at[rows], sem_out.at[sl])

    def start_gather(blk):
        for c in range(BLOCK_CHUNKS):
            in_copy(chunk_at(src_ref, blk * BLOCK_CHUNKS + c), blk % GATHER_BUFS, c).start()

    def wait_gather(blk):
        for c in range(BLOCK_CHUNKS):
            in_copy(any_rows, blk % GATHER_BUFS, c).wait()

    def start_scatter(blk):
        for c in range(BLOCK_CHUNKS):
            out_copy(chunk_at(dst_ref, (blk + SCATTER_LAG) * BLOCK_CHUNKS + c), blk % SCATTER_BUFS, c).start()

    def wait_scatter(sl):
        for c in range(BLOCK_CHUNKS):
            out_copy(any_rows, sl, c).wait()

    def load_rows(blk):
        return _unpack_rows(xbuf[blk % GATHER_BUFS])

    def issue_dmas(blk):
        start_scatter(blk - SCATTER_LAG)
        start_gather(blk + lookahead)

    def up_gate(blk, rows, ws):
        x_lo, x_hi = rows
        a1 = (jnp.dot(x_lo, w1b[ws, 0:HALF, :], preferred_element_type=F32)
              + jnp.dot(x_hi, w1b[ws, HALF:, :], preferred_element_type=F32))
        a3 = (jnp.dot(x_lo, w3b[ws, 0:HALF, :], preferred_element_type=F32)
              + jnp.dot(x_hi, w3b[ws, HALF:, :], preferred_element_type=F32))
        hbuf[blk % SCATTER_BUFS] = (a1 * jax.nn.sigmoid(a1) * a3).astype(BF16)

    def down(blk, ws):
        y = jnp.dot(hbuf[blk % SCATTER_BUFS], w2b[ws], preferred_element_type=F32)
        ybuf[blk % SCATTER_BUFS] = _pack_rows(y.astype(BF16).astype(F32))

    def for_each_weight_copy(gen, fn):
        ws = gen % 2
        expert = elist_ref[gen]
        for hbm, staging in ((w1_hbm, wst1), (w3_hbm, wst3), (w2_hbm, wst2)):
            fn(pltpu.make_async_copy(hbm.at[expert], staging.at[ws], sem_w.at[ws]))

    def switch_weights(gen):
        ws = gen % 2
        for_each_weight_copy(gen, lambda cp: cp.wait())
        w1b[ws] = wst1[ws].astype(BF16)
        w3b[ws] = wst3[ws].astype(BF16)
        w2b[ws] = wst2[ws].astype(BF16)

        @pl.when(gen + 1 < n_gen)
        def _():
            for_each_weight_copy(gen + 1, lambda cp: cp.start())

    def for_each_tail_piece(fn):
        def area_body(s, carry):
            used = used_ref[s]
            tail = AREA_CHUNKS - used
            row = (s * AREA_CHUNKS + used) * SUBLANES
            for bit in ZERO_BITS:
                take = (tail & bit) != 0

                @pl.when(take)
                def _(row=row, bit=bit):
                    fn(pltpu.make_async_copy(zbuf.at[pl.ds(0, bit * SUBLANES)],
                                             ys_hbm.at[pl.ds(pl.multiple_of(row, SUBLANES), bit * SUBLANES)],
                                             sem_zero))
                row = row + jnp.where(take, bit * SUBLANES, 0)
            return carry
        lax.fori_loop(0, n_areas, area_body, 0)

    @pl.when(n_gen > 0)
    def _():
        for_each_weight_copy(0, lambda cp: cp.start())

    zbuf[...] = jnp.zeros(zbuf.shape, U32)
    ybuf[...] = jnp.zeros(ybuf.shape, U32)
    dump = pltpu.make_async_copy(zbuf.at[pl.ds(0, DUMP_CHUNKS * SUBLANES)],
                                 ys_hbm.at[pl.ds(dump_row0, DUMP_CHUNKS * SUBLANES)], sem_dump)
    dump.start()
    dump.wait()
    for_each_tail_piece(lambda cp: cp.start())
    for sl in range(SCATTER_BUFS - SCATTER_LAG):
        for c in range(BLOCK_CHUNKS):
            out_copy(pl.ds(dump_row0 + (sl * BLOCK_CHUNKS + c) * SUBLANES, SUBLANES), sl, c).start()
    for k in range(lookahead):
        start_gather(k)

    @pl.when(total > 0)
    def _():
        switch_weights(0)
        wait_gather(0)
        rows = load_rows(0)
        issue_dmas(0)
        up_gate(0, rows, 0)

        def block_body(blk, carry):
            gen = bgen_ref[blk]
            gen_prev = bgen_ref[blk - 1]

            @pl.when(gen != gen_prev)
            def _():
                switch_weights(gen)

            wait_gather(blk)
            wait_scatter((blk - 1) % SCATTER_BUFS)
            rows = load_rows(blk)
            issue_dmas(blk)
            down(blk - 1, gen_prev % 2)
            up_gate(blk, rows, gen % 2)
            return carry

        lax.fori_loop(1, total, block_body, 0)
        wait_scatter((total - 1) % SCATTER_BUFS)
        down(total - 1, bgen_ref[total - 1] % 2)

    for k in range(SCATTER_LAG):
        start_scatter(total - SCATTER_LAG + k)
    for sl in range(SCATTER_BUFS):
        wait_scatter(sl)
    for k in range(lookahead):
        wait_gather(total + k)
    for_each_tail_piece(lambda cp: cp.wait())


def _moe_combine_kernel(h_ref, route_ref, ys_ref, l2g_ref, l2b_ref, y_ref):
    n = h_ref.shape[0]
    scope = n // (ys_ref.shape[0] // AREA_ROWS)
    s_iota = lax.broadcasted_iota(jnp.int32, (scope, AREA_ROWS), 1).astype(F32)
    for sc in range(n // scope):
        rows = slice(sc * scope, (sc + 1) * scope)
        route = route_ref[rows, :]
        slot1 = route[:, 0:1]
        slot2 = route[:, 1:2]
        gate1 = route[:, 2:3]
        gate2 = route[:, 3:4]
        gmat = (jnp.where(s_iota == slot1, gate1, 0.0) + jnp.where(s_iota == slot2, gate2, 0.0)).astype(BF16)
        y_lo, y_hi = _unpack_rows(ys_ref[sc * AREA_ROWS:(sc + 1) * AREA_ROWS, :])
        f = jnp.concatenate([jnp.dot(gmat, y_lo, preferred_element_type=F32),
                             jnp.dot(gmat, y_hi, preferred_element_type=F32)], axis=1)
        y_ref[rows, :] = _layer_norm(ALPHA * h_ref[rows, :] + f, l2g_ref[...], l2b_ref[...])


def _excl_cumsum(a, axis):
    return jnp.cumsum(a, axis=axis) - a


def _plan_blocks(cnt, q_max):
    n_areas = cnt.shape[0]
    c8 = (cnt + (SUBLANES - 1)) // SUBLANES
    used8 = jnp.sum(c8, axis=1)
    base8 = jnp.arange(n_areas, dtype=jnp.int32) * AREA_CHUNKS
    src_runs = (base8[:, None] + _excl_cumsum(c8, 1)).T
    len_runs = c8.T
    tot = jnp.sum(len_runs, axis=1)
    ptot = (tot + BLOCK_CHUNKS - 1) // BLOCK_CHUNKS * BLOCK_CHUNKS
    pstart = _excl_cumsum(ptot, 0)
    dst_runs = pstart[:, None] + _excl_cumsum(len_runs, 1)
    off = (src_runs - dst_runs).reshape(-1)
    diff = off - jnp.concatenate([jnp.zeros((1,), jnp.int32), off[:-1]])
    q = jnp.arange(q_max, dtype=jnp.int32)
    src = q + jnp.sum(jnp.where(dst_runs.reshape(1, -1) <= q[:, None], diff[None, :], 0), axis=1)
    valid = jnp.any((pstart[None, :] <= q[:, None]) & (q[:, None] < (pstart + tot)[None, :]), axis=1)
    zero_chunk = AREA_CHUNKS - 1
    ahead = jnp.full(((GATHER_BUFS - 1) * BLOCK_CHUNKS,), zero_chunk, jnp.int32)
    src_tab = jnp.concatenate([jnp.where(valid, src, zero_chunk).astype(jnp.int32), ahead])
    qv = jnp.arange(-SCATTER_LAG * BLOCK_CHUNKS, q_max, dtype=jnp.int32)
    dump = n_areas * AREA_CHUNKS + ((qv // BLOCK_CHUNKS) % SCATTER_BUFS) * BLOCK_CHUNKS + qv % BLOCK_CHUNKS
    lead = jnp.zeros((SCATTER_LAG * BLOCK_CHUNKS,), jnp.int32)
    dst_tab = jnp.where(jnp.concatenate([lead > 0, valid]), jnp.concatenate([lead, src]), dump).astype(jnp.int32)
    nblk = ptot // BLOCK_CHUNKS
    bstart = pstart // BLOCK_CHUNKS
    has = nblk > 0
    gen_of = jnp.cumsum(has.astype(jnp.int32)) - 1
    experts = jnp.arange(N_EXPERTS, dtype=jnp.int32)
    elist = jnp.sum(jnp.where(has[None, :] & (gen_of[None, :] == experts[:, None]), experts[None, :], 0), axis=1)
    blk = jnp.arange(q_max // BLOCK_CHUNKS, dtype=jnp.int32)
    owner = (bstart[None, :] <= blk[:, None]) & (blk[:, None] < (bstart + nblk)[None, :])
    bgen = jnp.sum(jnp.where(owner, gen_of[None, :], 0), axis=1)
    meta = jnp.stack([jnp.sum(nblk), jnp.sum(has.astype(jnp.int32))])
    return (src_tab, dst_tab, bgen.astype(jnp.int32), elist.astype(jnp.int32), meta.astype(jnp.int32),
            used8.astype(jnp.int32))


def _rope_tables(pos):
    half = HEAD_DIM // 2
    inv = (ROPE_THETA ** (-np.arange(half, dtype=np.float64) / half)).astype(np.float32)
    ang = (np.asarray(pos, np.float32)[:, None] * inv).astype(np.float64)
    cos = np.cos(ang)
    sin = np.sin(ang)
    reps = LANES // HEAD_DIM
    cos_t = np.concatenate([cos, cos] * reps, axis=1)
    sin_t = np.concatenate([-sin, sin] * reps, axis=1)
    q_scale = ATTN_SCALE * LOG2E
    return jnp.asarray(np.concatenate([cos_t, sin_t, cos_t * q_scale, sin_t * q_scale], axis=1), F32)


def _full(shape):
    nd = len(shape)
    return pl.BlockSpec(shape, lambda *_: (0,) * nd)


def kernel(x_prompt, x_sample, cache_k, cache_v, state_conv, w_in, b_in, sinks, conv_w, conv_b, ln_conv_g,
           ln_conv_b, w_out, b_out, ln1_g, ln1_b, w_router_group, w_router_expert, w1, w3, w2, ln2_g, ln2_b):
    assert w_in.shape[0] == DEPTH
    bsz, seq, _ = x_prompt.shape
    dec_b, dec_t, _ = x_sample.shape
    assert dec_t == 1 and seq % TILE_Q == 0 and dec_b % SAMPLE_TB == 0 and 2 * dec_b + 7 * N_EXPERTS <= AREA_ROWS
    assert (bsz * seq) % COMBINE_TILE == 0
    n_prompt = bsz * seq
    tiles = seq // TILE_Q
    spt = TILE_Q // PROMPT_SCOPE
    p_areas = n_prompt // PROMPT_SCOPE
    n_areas = p_areas + spt
    q_max = (n_areas * AREA_CHUNKS + N_EXPERTS * (BLOCK_CHUNKS - 1) + BLOCK_CHUNKS - 1) // BLOCK_CHUNKS * BLOCK_CHUNKS

    row = lambda a: a.reshape(1, -1)
    qk = ATTN_DIM + KV_DIM
    half = HEAD_DIM // 2

    def swap_halves(a):
        parts = a[..., :qk].reshape(a.shape[:-1] + (qk // HEAD_DIM, 2, half))
        return parts[..., ::-1, :].reshape(a.shape[:-1] + (qk,))

    w_in_b = w_in[0].astype(BF16)
    w_sw_b = swap_halves(w_in[0]).astype(BF16)
    w_out_b = w_out[0].astype(BF16)
    w_r = jnp.concatenate([w_router_group[0], w_router_expert[0].reshape(D_MODEL, N_EXPERTS),
                           jnp.zeros((D_MODEL, LANES - ELANE0 - N_EXPERTS), F32)], axis=1).astype(BF16)
    shared = (w_in_b, row(b_in[0]), w_sw_b, row(swap_halves(b_in[0])), conv_w[0], row(conv_b[0]),
              row(ln_conv_g[0]), row(ln_conv_b[0]), w_out_b, row(b_out[0]), row(ln1_g[0]), row(ln1_b[0]), w_r)
    shared_specs = [_full(a.shape) for a in shared]

    def glu_blocks(a):
        lo = ATTN_DIM + 2 * KV_DIM
        vg = a[..., lo:lo + 2 * CONV_CH].reshape(a.shape[:-1] + (2, CONV_CH // LANES, LANES))
        return jnp.swapaxes(vg, -3, -2).reshape(a.shape[:-1] + (2 * CONV_CH,))

    glu = (glu_blocks(w_in[0]).astype(BF16), row(glu_blocks(b_in[0])))
    glu_specs = [_full(a.shape) for a in glu]
    assert len(glu) + len(shared) == N_PROMPT_WEIGHTS
    smem = pl.BlockSpec(memory_space=pltpu.SMEM)
    rope_p = _rope_tables(np.arange(seq))
    rope_s = _rope_tables(PAST_LEN + np.arange(dec_t))

    cparams = lambda sem: pltpu.CompilerParams(dimension_semantics=sem, vmem_limit_bytes=VMEM_LIMIT)

    ck = cache_k[0].reshape(dec_b, WINDOW, KV_DIM)
    cv = cache_v[0].reshape(dec_b, WINDOW, KV_DIM)
    tb = SAMPLE_TB
    conv_state_spec = pl.BlockSpec((None, tb, CONV_WIDTH - 1, CONV_CH), lambda t: (0, t, 0, 0))
    h_s, nk_s, nv_s, nc_s, xs_s, route_s, cnt_s = pl.pallas_call(
        _mix_sample_kernel,
        grid=(dec_b // tb,),
        in_specs=[smem,
                  pl.BlockSpec((tb, None, D_MODEL), lambda t: (t, 0, 0)),
                  _full(rope_s.shape),
                  pl.BlockSpec((tb, WINDOW, KV_DIM), lambda t: (t, 0, 0)),
                  pl.BlockSpec((tb, WINDOW, KV_DIM), lambda t: (t, 0, 0)),
                  conv_state_spec] + shared_specs,
        out_specs=[pl.BlockSpec((tb, D_MODEL), lambda t: (t, 0)),
                   pl.BlockSpec((tb, WINDOW, KV_DIM), lambda t: (t, 0, 0)),
                   pl.BlockSpec((tb, WINDOW, KV_DIM), lambda t: (t, 0, 0)),
                   conv_state_spec,
                   _full((AREA_ROWS, HALF)),
                   _full((dec_b, LANES)),
                   _full((1, SUBLANES, LANES))],
        out_shape=[jax.ShapeDtypeStruct((dec_b, D_MODEL), F32),
                   jax.ShapeDtypeStruct((dec_b, WINDOW, KV_DIM), F32),
                   jax.ShapeDtypeStruct((dec_b, WINDOW, KV_DIM), F32),
                   jax.ShapeDtypeStruct((DEPTH, dec_b, CONV_WIDTH - 1, CONV_CH), F32),
                   jax.ShapeDtypeStruct((AREA_ROWS, HALF), U32),
                   jax.ShapeDtypeStruct((dec_b, LANES), F32),
                   jax.ShapeDtypeStruct((1, SUBLANES, LANES), F32)],
        scratch_shapes=[pltpu.VMEM((dec_b, D_MODEL), F32)],
        compiler_params=cparams(("arbitrary",)),
        name="mix_sample",
    )(sinks[0], x_sample, rope_s, ck, cv, state_conv, *shared)

    n_real = bsz * tiles
    real = lambda t: jnp.minimum(t, n_real - 1)
    h_p, xs, route_p, cnt_p, nk_p, nv_p, nc_p = pl.pallas_call(
        functools.partial(_mix_prompt_kernel, tiles=tiles),
        grid=(n_real + 1,),
        in_specs=[smem,
                  pl.BlockSpec((1, TILE_Q, D_MODEL), lambda t: (real(t) // tiles, real(t) % tiles, 0)),
                  pl.BlockSpec((TILE_Q, 4 * LANES), lambda t: (real(t) % tiles, 0)),
                  _full((AREA_ROWS, HALF))] + glu_specs + shared_specs,
        out_specs=[pl.BlockSpec((1, TILE_Q, D_MODEL), lambda t: (real(t) // tiles, real(t) % tiles, 0)),
                   pl.BlockSpec((spt * AREA_ROWS, HALF), lambda t: (t, 0)),
                   pl.BlockSpec((TILE_Q, LANES), lambda t: (real(t), 0)),
                   pl.BlockSpec((spt, SUBLANES, LANES), lambda t: (t, 0, 0)),
                   pl.BlockSpec((1, WINDOW, KV_DIM), lambda t: (real(t) // tiles, 0, 0)),
                   pl.BlockSpec((1, WINDOW, KV_DIM), lambda t: (real(t) // tiles, 0, 0)),
                   pl.BlockSpec((1, CONV_WIDTH - 1, CONV_CH), lambda t: (real(t) // tiles, 0, 0))],
        out_shape=[jax.ShapeDtypeStruct((bsz, seq, D_MODEL), F32),
                   jax.ShapeDtypeStruct((n_areas * AREA_ROWS, HALF), U32),
                   jax.ShapeDtypeStruct((n_prompt, LANES), F32),
                   jax.ShapeDtypeStruct((n_areas, SUBLANES, LANES), F32),
                   jax.ShapeDtypeStruct((bsz, WINDOW, KV_DIM), F32),
                   jax.ShapeDtypeStruct((bsz, WINDOW, KV_DIM), F32),
                   jax.ShapeDtypeStruct((bsz, CONV_WIDTH - 1, CONV_CH), F32)],
        scratch_shapes=[pltpu.VMEM((WINDOW, KV_DIM), F32), pltpu.VMEM((WINDOW, KV_DIM), F32),
                        pltpu.VMEM((TILE_Q + CONV_PRE, CONV_CH), F32),
                        pltpu.VMEM((SUBLANES - 1, TILE_Q + CONV_SHIFT_EXTRA, CONV_CH), F32)],
        compiler_params=cparams(("arbitrary",)),
        name="mix_prompt",
    )(sinks[0], x_prompt, rope_p, xs_s, *glu, *shared)

    cnt = jnp.concatenate([cnt_p[:p_areas, 0], cnt_s[:, 0], cnt_p[p_areas + 1:, 0]], axis=0)
    cnt = cnt[:, ELANE0:ELANE0 + N_EXPERTS].astype(jnp.int32)
    plan = _plan_blocks(cnt, q_max)
    hbm = pl.BlockSpec(memory_space=pl.ANY)
    up_shape, down_shape = (2, D_MODEL, D_FF_EXPERT), (2, D_FF_EXPERT, D_MODEL)
    ys = pl.pallas_call(
        _moe_ffn_kernel,
        grid_spec=pltpu.PrefetchScalarGridSpec(
            num_scalar_prefetch=len(plan),
            grid=(1,),
            in_specs=[hbm, hbm, hbm, hbm],
            out_specs=hbm,
            scratch_shapes=[pltpu.VMEM((GATHER_BUFS, BLOCK_ROWS, HALF), U32),
                            pltpu.VMEM((SCATTER_BUFS, BLOCK_ROWS, HALF), U32),
                            pltpu.VMEM((SCATTER_BUFS, BLOCK_ROWS, D_FF_EXPERT), BF16),
                            pltpu.VMEM((ZERO_ROWS, HALF), U32),
                            pltpu.VMEM(up_shape, F32), pltpu.VMEM(up_shape, F32), pltpu.VMEM(down_shape, F32),
                            pltpu.VMEM(up_shape, BF16), pltpu.VMEM(up_shape, BF16), pltpu.VMEM(down_shape, BF16),
                            pltpu.SemaphoreType.DMA((GATHER_BUFS,)), pltpu.SemaphoreType.DMA((SCATTER_BUFS,)),
                            pltpu.SemaphoreType.DMA(()), pltpu.SemaphoreType.DMA(()),
                            pltpu.SemaphoreType.DMA((2,))]),
        out_shape=jax.ShapeDtypeStruct((n_areas * AREA_ROWS + DUMP_CHUNKS * SUBLANES, HALF), U32),
        compiler_params=cparams(("arbitrary",)),
        name="moe_ffn",
    )(*plan, xs, w1[0], w3[0], w2[0])

    def combine(h2d, route, first_area, scope, tile):
        n = h2d.shape[0]
        apt = tile // scope
        return pl.pallas_call(
            _moe_combine_kernel,
            grid=(n // tile,),
            in_specs=[pl.BlockSpec((tile, D_MODEL), lambda t: (t, 0)),
                      pl.BlockSpec((tile, LANES), lambda t: (t, 0)),
                      pl.BlockSpec((apt * AREA_ROWS, HALF), lambda t: (first_area // apt + t, 0)),
                      _full((1, D_MODEL)), _full((1, D_MODEL))],
            out_specs=pl.BlockSpec((tile, D_MODEL), lambda t: (t, 0)),
            out_shape=jax.ShapeDtypeStruct((n, D_MODEL), F32),
            compiler_params=cparams(("arbitrary",)),
            name="moe_combine",
        )(h2d, route, ys, row(ln2_g[0]), row(ln2_b[0]))

    y_p = combine(h_p.reshape(n_prompt, D_MODEL), route_p, 0, PROMPT_SCOPE, COMBINE_TILE)
    y_s = combine(h_s, route_s, p_areas, dec_b, dec_b)

    kv_shape = lambda n: (DEPTH, n, WINDOW, N_KV_HEADS, HEAD_DIM)
    return (y_p.reshape(bsz, seq, D_MODEL), y_s.reshape(dec_b, dec_t, D_MODEL),
            nk_p.reshape(kv_shape(bsz)), nv_p.reshape(kv_shape(bsz)), nc_p[None],
            nk_s.reshape(kv_shape(dec_b)), nv_s.reshape(kv_shape(dec_b)), nc_s)
```

```python
import functools

import jax
import jax.numpy as jnp
import numpy as np
from jax import lax
from jax.experimental import pallas as pl
from jax.experimental.pallas import tpu as pltpu

F32 = jnp.float32
BF16 = jnp.bfloat16
U32 = jnp.uint32

D_MODEL = 1024
N_HEADS = 8
N_KV_HEADS = 2
HEAD_DIM = 64
Q_GROUP = N_HEADS // N_KV_HEADS
ATTN_DIM = N_HEADS * HEAD_DIM
KV_DIM = N_KV_HEADS * HEAD_DIM
WINDOW = 128
ROPE_THETA = 10000.0
ATTN_SCALE = HEAD_DIM ** -0.5
CONV_CH = D_MODEL - ATTN_DIM
CONV_WIDTH = 31
IN_DIM = ATTN_DIM + 2 * KV_DIM + 2 * CONV_CH
N_GROUPS = 4
EXPERTS_PER_GROUP = 8
N_EXPERTS = N_GROUPS * EXPERTS_PER_GROUP
D_FF_EXPERT = 256
LN_EPS = 1e-5
NEG_INF = -1e30
DEPTH = 1
ALPHA = (2.0 * DEPTH) ** 0.25
PAST_LEN = 16384
LOG2E = 1.4426950408889634

LANES = 128
SUBLANES = 8
HALF = D_MODEL // 2
ELANE0 = N_GROUPS
PROMPT_SCOPE = 256
AREA_ROWS = -(-(2 * PROMPT_SCOPE + (SUBLANES - 1) * N_EXPERTS) // LANES) * LANES
AREA_CHUNKS = AREA_ROWS // SUBLANES
BLOCK_CHUNKS = 32
BLOCK_ROWS = BLOCK_CHUNKS * SUBLANES
TILE_Q = 512
COMBINE_TILE = 1024
CONV_PRE = 32
CONV_SHIFT_EXTRA = 24
SAMPLE_TB = 32
SAMPLE_GRP = 8
VMEM_LIMIT = 56 * 1024 * 1024


def _layer_norm(x, g, b):
    mu = jnp.mean(x, axis=-1, keepdims=True)
    xc = x - mu
    var = jnp.mean(xc * xc, axis=-1, keepdims=True)
    return xc * lax.rsqrt(var + LN_EPS) * g + b


def _rope(x, x_swapped, cos, sin_signed):
    pieces = [x[:, g * LANES:(g + 1) * LANES] * cos + x_swapped[:, g * LANES:(g + 1) * LANES] * sin_signed
              for g in range(x.shape[-1] // LANES)]
    return pieces[0] if len(pieces) == 1 else jnp.concatenate(pieces, axis=-1)


def _in_proj_glu(xb, w_in_ref, b_in_ref):
    lo, hi = ATTN_DIM + 2 * KV_DIM, IN_DIM
    cacg = jnp.dot(xb, w_in_ref[:, lo:hi], preferred_element_type=F32) + b_in_ref[:, lo:hi]
    return cacg[:, :CONV_CH] * jax.nn.sigmoid(cacg[:, CONV_CH:])


def _in_proj_qkv(xb, w_in_ref, b_in_ref, w_sw_ref, b_sw_ref, rope):
    o_v, o_e = ATTN_DIM + KV_DIM, ATTN_DIM + 2 * KV_DIM
    qkv = jnp.dot(xb, w_in_ref[:, 0:o_e], preferred_element_type=F32) + b_in_ref[:, 0:o_e]
    sw = jnp.dot(xb, w_sw_ref[...], preferred_element_type=F32) + b_sw_ref[...]
    k = _rope(qkv[:, ATTN_DIM:o_v], sw[:, ATTN_DIM:], rope[:, 0:LANES], rope[:, LANES:2 * LANES])
    q = _rope(qkv[:, :ATTN_DIM], sw[:, :ATTN_DIM], rope[:, 2 * LANES:3 * LANES], rope[:, 3 * LANES:])
    return q, k, qkv[:, o_v:]


def _pack_rows(v):
    lo = lax.bitcast_convert_type(v[:, :HALF], U32)
    hi = lax.bitcast_convert_type(v[:, HALF:], U32)
    return (lo >> 16) | (hi & jnp.uint32(0xFFFF0000))


def _unpack_rows(w):
    lo = lax.bitcast_convert_type(w << 16, F32).astype(BF16)
    hi = lax.bitcast_convert_type(w & jnp.uint32(0xFFFF0000), F32).astype(BF16)
    return lo, hi


def _drain(stages):
    try:
        while True:
            next(stages)
    except StopIteration as stop:
        return stop.value


def _round_robin(*stage_generators):
    live = list(stage_generators)
    while live:
        for stages in list(live):
            try:
                next(stages)
            except StopIteration:
                live.remove(stages)


def _route_and_sort_stages(h, w_r):
    n = h.shape[0]
    hb = h.astype(BF16)
    logits = jnp.dot(hb, w_r, preferred_element_type=F32)
    yield
    lane = lax.broadcasted_iota(jnp.int32, (n, LANES), 1)
    lane_f = lane.astype(F32)
    is_g = lane < N_GROUPS
    lg = jnp.where(is_g, logits, -jnp.inf)
    gmax = jnp.max(lg, axis=-1, keepdims=True)
    yield
    gidx = jnp.min(jnp.where(lg == gmax, lane_f, float(LANES)), axis=-1, keepdims=True)
    p_g = 1.0 / jnp.sum(jnp.where(is_g, jnp.exp(logits - gmax), 0.0), axis=-1, keepdims=True)
    yield
    egrp = ((lane - ELANE0) >> (EXPERTS_PER_GROUP.bit_length() - 1)).astype(F32)
    emask = (lane >= ELANE0) & (lane < ELANE0 + N_EXPERTS) & (egrp == gidx)
    sel = jnp.where(emask, logits, -jnp.inf)
    v1 = jnp.max(sel, axis=-1, keepdims=True)
    yield
    i1 = jnp.min(jnp.where(sel == v1, lane_f, float(LANES)), axis=-1, keepdims=True)
    yield
    sel2 = jnp.where(lane_f == i1, -jnp.inf, sel)
    v2 = jnp.max(sel2, axis=-1, keepdims=True)
    yield
    i2 = jnp.min(jnp.where(sel2 == v2, lane_f, float(LANES)), axis=-1, keepdims=True)
    t = jnp.exp(v2 - v1)
    gate1 = p_g / (1.0 + t)
    gate2 = p_g * t / (1.0 + t)
    pick1 = lane_f == i1
    pick2 = lane_f == i2
    member = jnp.where(pick1 | pick2, 1.0, 0.0)
    yield

    r_i = lax.broadcasted_iota(jnp.int32, (n, n), 0)
    c_i = lax.broadcasted_iota(jnp.int32, (n, n), 1)
    lower = jnp.where(c_i < r_i, 1.0, 0.0).astype(BF16)
    e_r = lax.broadcasted_iota(jnp.int32, (LANES, LANES), 0)
    e_c = lax.broadcasted_iota(jnp.int32, (LANES, LANES), 1)
    upper = jnp.where(e_r < e_c, 1.0, 0.0).astype(BF16)
    before = jnp.dot(lower, member.astype(BF16), preferred_element_type=F32)
    cnt = jnp.sum(member, axis=0, keepdims=True)
    c8 = jnp.floor((cnt + (SUBLANES - 1)) * (1.0 / SUBLANES))
    c8b = jnp.broadcast_to(c8, (SUBLANES, LANES)).astype(BF16)
    off8 = jnp.dot(c8b, upper, preferred_element_type=F32)[0:1]
    yield
    slot_all = before + off8 * float(SUBLANES)
    slot1 = jnp.sum(jnp.where(pick1, slot_all, 0.0), axis=-1, keepdims=True)
    slot2 = jnp.sum(jnp.where(pick2, slot_all, 0.0), axis=-1, keepdims=True)
    yield
    s_iota = lax.broadcasted_iota(jnp.int32, (n, AREA_ROWS), 1).astype(F32)
    perm_t = jnp.where((s_iota == slot1) | (s_iota == slot2), 1.0, 0.0).astype(BF16)
    sorted_rows = lax.dot_general(perm_t, hb, (((0,), (0,)), ((), ())),
                                  preferred_element_type=F32)
    yield
    route = jnp.where(lane == 0, slot1, jnp.where(lane == 1, slot2, jnp.where(lane == 2, gate1,
                      jnp.where(lane == 3, gate2, 0.0))))
    return route, _pack_rows(sorted_rows), cnt


def _softmax_rows_with_sink(s, sink, extra=None):
    m = jnp.maximum(jnp.max(s, axis=-1, keepdims=True), sink)
    if extra is not None:
        m = jnp.maximum(m, extra)
    e = jnp.exp2(s - m)
    den = jnp.sum(e, axis=-1, keepdims=True) + jnp.exp2(sink - m)
    if extra is not None:
        ee = jnp.exp2(extra - m)
        den = den + ee
        return e / den, ee / den
    return e / den, None


def _mix_out_stages(x, attn, conv_pre, lcg, lcb, w_out_ref, b_out, l1g, l1b):
    conv = _layer_norm(conv_pre, lcg, lcb)
    conv = conv * jax.nn.sigmoid(conv)
    yield
    mix = (jnp.dot(attn.astype(BF16), w_out_ref[0:ATTN_DIM, :], preferred_element_type=F32)
           + jnp.dot(conv.astype(BF16), w_out_ref[ATTN_DIM:, :], preferred_element_type=F32) + b_out)
    yield
    h = _layer_norm(ALPHA * x + mix, l1g, l1b)
    yield
    return h


N_PROMPT_WEIGHTS = 15


def _mix_prompt_kernel(sinks_ref, x_ref, rope_ref, xs_s_ref, *rest, tiles):
    _, xs_ref, _, cnt_ref = rest[N_PROMPT_WEIGHTS:N_PROMPT_WEIGHTS + 4]
    step = pl.program_id(0)
    n_real = pl.num_programs(0) - 1

    @pl.when(step < n_real)
    def _():
        _mix_prompt_tile(sinks_ref, x_ref, rope_ref, *rest, i=step % tiles)

    @pl.when(step == n_real)
    def _():
        xs_ref[0:AREA_ROWS, :] = xs_s_ref[...]
        xs_ref[AREA_ROWS:, :] = jnp.zeros((xs_ref.shape[0] - AREA_ROWS, HALF), U32)
        cnt_ref[...] = jnp.zeros(cnt_ref.shape, F32)


def _mix_prompt_tile(sinks_ref, x_ref, rope_ref, w_glu_ref, b_glu_ref,
                     w_in_ref, b_in_ref, w_sw_ref, b_sw_ref, conv_w_ref, conv_b_ref,
                     lcg_ref, lcb_ref, w_out_ref, b_out_ref, l1g_ref, l1b_ref, w_r_ref,
                     h_ref, xs_ref, route_ref, cnt_ref, nk_ref, nv_ref, nc_ref,
                     kprev, vprev, u_scr, u_shift, *, i):
    tq = x_ref.shape[1]
    pre = CONV_PRE
    n_sc = tq // PROMPT_SCOPE

    @pl.when(i == 0)
    def _():
        kprev[...] = jnp.zeros_like(kprev)
        vprev[...] = jnp.zeros_like(vprev)
        u_scr[0:pre, :] = jnp.zeros((pre, CONV_CH), F32)

    x = x_ref[0]
    xb = x.astype(BF16)

    def glu_block(cb):
        cols = slice(2 * LANES * cb, 2 * LANES * (cb + 1))
        vg = jnp.dot(xb, w_glu_ref[:, cols], preferred_element_type=F32) + b_glu_ref[:, cols]
        return vg[:, :LANES] * jax.nn.sigmoid(vg[:, LANES:])

    n_cb = CONV_CH // LANES
    acc_blocks = [[None] * n_cb for _ in range(n_sc)]
    u_next = glu_block(0)
    for cb in range(n_cb):
        lanes = slice(LANES * cb, LANES * (cb + 1))
        u_scr[pre:pre + tq, lanes] = u_next
        if cb + 1 < n_cb:
            u_next = glu_block(cb + 1)
        for r in range(1, SUBLANES):
            u_shift[r - 1, :, lanes] = u_scr[r:r + tq + CONV_SHIFT_EXTRA, lanes]
        for sc in range(n_sc):
            base = sc * PROMPT_SCOPE
            acc = jnp.zeros((PROMPT_SCOPE, LANES), F32)
            for j in range(CONV_WIDTH):
                a, r = divmod(pre - (CONV_WIDTH - 1) + j, SUBLANES)
                start = base + a * SUBLANES
                src = (u_scr[start:start + PROMPT_SCOPE, lanes] if r == 0
                       else u_shift[r - 1, start:start + PROMPT_SCOPE, lanes])
                acc = acc + src * conv_w_ref[j:j + 1, lanes]
            acc_blocks[sc][cb] = acc
    conv_pre = [jnp.concatenate(blocks, axis=1) + conv_b_ref[...] for blocks in acc_blocks]

    nc_ref[0] = u_scr[pre + tq - (CONV_WIDTH - 1):pre + tq, :]
    tail = u_scr[tq:tq + pre, :]
    u_scr[0:pre, :] = tail

    q, k, v = _in_proj_qkv(xb, w_in_ref, b_in_ref, w_sw_ref, b_sw_ref, rope_ref[...])
    qb = q.astype(BF16)
    kb = jnp.concatenate([kprev[...], k], axis=0).astype(BF16)
    vb = jnp.concatenate([vprev[...], v], axis=0).astype(BF16)
    row = lax.broadcasted_iota(jnp.int32, (WINDOW, 2 * WINDOW), 0)
    col = lax.broadcasted_iota(jnp.int32, (WINDOW, 2 * WINDOW), 1)
    dist = row + WINDOW - col
    band = (dist >= 0) & (dist < WINDOW)
    first_col = jnp.where(i > 0, 0, WINDOW)
    kprev[...] = k[tq - WINDOW:, :]
    vprev[...] = v[tq - WINDOW:, :]
    nk_ref[0] = k[tq - WINDOW:, :]
    nv_ref[0] = v[tq - WINDOW:, :]
    blocks_per_scope = PROMPT_SCOPE // WINDOW
    attn_blocks = [None] * (tq // WINDOW)

    def scope_attention(sc):
        for j in range(sc * blocks_per_scope, (sc + 1) * blocks_per_scope):
            msk = (band & (col >= first_col)) if j == 0 else band
            heads = [None] * N_HEADS
            for c in range(N_KV_HEADS):
                kc = kb[j * WINDOW:(j + 2) * WINDOW, c * HEAD_DIM:(c + 1) * HEAD_DIM]
                vc = vb[j * WINDOW:(j + 2) * WINDOW, c * HEAD_DIM:(c + 1) * HEAD_DIM]
                qc = jnp.concatenate(
                    [qb[j * WINDOW:(j + 1) * WINDOW, (c * Q_GROUP + g) * HEAD_DIM:(c * Q_GROUP + g + 1) * HEAD_DIM]
                     for g in range(Q_GROUP)], axis=0)
                s = lax.dot_general(qc, kc, (((1,), (1,)), ((), ())), preferred_element_type=F32)
                yield
                probs = []
                for g in range(Q_GROUP):
                    sg = jnp.where(msk, s[g * WINDOW:(g + 1) * WINDOW], NEG_INF)
                    p, _ = _softmax_rows_with_sink(sg, sinks_ref[c * Q_GROUP + g] * LOG2E)
                    probs.append(p.astype(BF16))
                    if g % 2 == 1:
                        yield
                o = jnp.dot(jnp.concatenate(probs, axis=0), vc, preferred_element_type=F32)
                for g in range(Q_GROUP):
                    heads[c * Q_GROUP + g] = o[g * WINDOW:(g + 1) * WINDOW]
                yield
            attn_blocks[j] = jnp.concatenate(heads, axis=1)

    def scope_tail(sc):
        rows = slice(sc * PROMPT_SCOPE, (sc + 1) * PROMPT_SCOPE)
        attn = jnp.concatenate(attn_blocks[sc * blocks_per_scope:(sc + 1) * blocks_per_scope], axis=0)
        h = yield from _mix_out_stages(x[rows], attn, conv_pre[sc], lcg_ref[...], lcb_ref[...],
                                       w_out_ref, b_out_ref[...], l1g_ref[...], l1b_ref[...])
        route, area, cnt = yield from _route_and_sort_stages(h, w_r_ref[...])
        h_ref[0, rows, :] = h
        route_ref[rows, :] = route
        xs_ref[sc * AREA_ROWS:(sc + 1) * AREA_ROWS, :] = area
        cnt_ref[sc] = jnp.broadcast_to(cnt, (SUBLANES, LANES))

    _round_robin(*[scope_attention(sc) for sc in range(n_sc)])
    _round_robin(*[scope_tail(sc) for sc in range(n_sc)])


def _mix_sample_kernel(sinks_ref, x_ref, rope_ref, ck_ref, cv_ref, st_ref, w_in_ref, b_in_ref, w_sw_ref, b_sw_ref,
                       conv_w_ref, conv_b_ref, lcg_ref, lcb_ref, w_out_ref, b_out_ref, l1g_ref, l1b_ref,
                       w_r_ref,
                       h_ref, nk_ref, nv_ref, nc_ref, xs_ref, route_ref, cnt_ref,
                       h_scr):
    tb = x_ref.shape[0]
    n_tok = h_scr.shape[0]
    step = pl.program_id(0)
    x = x_ref[...]
    xb = x.astype(BF16)
    u = _in_proj_glu(xb, w_in_ref, b_in_ref)
    q, k_new, v_new = _in_proj_qkv(xb, w_in_ref, b_in_ref, w_sw_ref, b_sw_ref, rope_ref[...])

    qb = q.astype(BF16)
    knb = k_new.astype(BF16).astype(F32)
    vnb = v_new.astype(BF16).astype(F32)
    gq = SAMPLE_GRP * Q_GROUP
    nkeys = SAMPLE_GRP * WINDOW
    r_i = lax.broadcasted_iota(jnp.int32, (gq, nkeys), 0)
    c_i = lax.broadcasted_iota(jnp.int32, (gq, nkeys), 1)
    window_shift, grp_shift = WINDOW.bit_length() - 1, SAMPLE_GRP.bit_length() - 1
    msk = ((c_i >> window_shift) == (r_i & (SAMPLE_GRP - 1))) & ((c_i & (WINDOW - 1)) >= 1)
    r_col = lax.broadcasted_iota(jnp.int32, (gq, 1), 0) >> grp_shift
    attn_groups = []
    for grp in range(tb // SAMPLE_GRP):
        r0 = grp * SAMPLE_GRP
        kc_all = ck_ref[r0:r0 + SAMPLE_GRP].reshape(nkeys, KV_DIM).astype(BF16)
        vc_all = cv_ref[r0:r0 + SAMPLE_GRP].reshape(nkeys, KV_DIM).astype(BF16)
        heads = [None] * N_HEADS
        for c in range(N_KV_HEADS):
            lanes = slice(c * HEAD_DIM, (c + 1) * HEAD_DIM)
            qc = jnp.concatenate(
                [qb[r0:r0 + SAMPLE_GRP, (c * Q_GROUP + g) * HEAD_DIM:(c * Q_GROUP + g + 1) * HEAD_DIM]
                 for g in range(Q_GROUP)], axis=0)
            s = lax.dot_general(qc, kc_all[:, lanes], (((1,), (1,)), ((), ())), preferred_element_type=F32)
            s = jnp.where(msk, s, NEG_INF)
            kn = jnp.concatenate([knb[r0:r0 + SAMPLE_GRP, lanes]] * Q_GROUP, axis=0)
            vn = jnp.concatenate([vnb[r0:r0 + SAMPLE_GRP, lanes]] * Q_GROUP, axis=0)
            s_new = jnp.sum(qc.astype(F32) * kn, axis=-1, keepdims=True)
            sink = jnp.zeros((gq, 1), F32)
            for g in range(Q_GROUP):
                sink = jnp.where(r_col == g, sinks_ref[c * Q_GROUP + g] * LOG2E, sink)
            p, p_new = _softmax_rows_with_sink(s, sink, s_new)
            o = (jnp.dot(p.astype(BF16), vc_all[:, lanes], preferred_element_type=F32)
                 + p_new.astype(BF16).astype(F32) * vn)
            for g in range(Q_GROUP):
                heads[c * Q_GROUP + g] = o[g * SAMPLE_GRP:(g + 1) * SAMPLE_GRP]
        attn_groups.append(jnp.concatenate(heads, axis=1))
    attn = jnp.concatenate(attn_groups, axis=0)

    st = st_ref[...]
    acc = jnp.sum(st * conv_w_ref[0:CONV_WIDTH - 1, :][None], axis=1)
    conv_pre = acc + u * conv_w_ref[CONV_WIDTH - 1:CONV_WIDTH, :] + conv_b_ref[...]

    nk_ref[:, 0:WINDOW - 1, :] = ck_ref[:, 1:WINDOW, :]
    nv_ref[:, 0:WINDOW - 1, :] = cv_ref[:, 1:WINDOW, :]
    nc_ref[:, 0:CONV_WIDTH - 2, :] = st_ref[:, 1:CONV_WIDTH - 1, :]
    for b in range(tb):
        nk_ref[b, WINDOW - 1:WINDOW, :] = k_new[b:b + 1, :]
        nv_ref[b, WINDOW - 1:WINDOW, :] = v_new[b:b + 1, :]
        nc_ref[b, CONV_WIDTH - 2:CONV_WIDTH - 1, :] = u[b:b + 1, :]

    h = _drain(_mix_out_stages(x, attn, conv_pre, lcg_ref[...], lcb_ref[...], w_out_ref, b_out_ref[...],
                               l1g_ref[...], l1b_ref[...]))
    h_ref[...] = h
    h_scr[pl.ds(pl.multiple_of(step * tb, tb), tb), :] = h

    @pl.when(step == pl.num_programs(0) - 1)
    def _():
        route, area, cnt = _drain(_route_and_sort_stages(h_scr[...], w_r_ref[...]))
        route_ref[...] = route
        xs_ref[...] = area
        cnt_ref[0] = jnp.broadcast_to(cnt, (SUBLANES, LANES))


GATHER_BUFS = 8
SCATTER_BUFS = 8
SCATTER_LAG = 2
ZERO_BITS = (64, 32, 16, 8, 4, 2, 1)
DUMP_CHUNKS = SCATTER_BUFS * BLOCK_CHUNKS
ZERO_ROWS = max(ZERO_BITS[0], DUMP_CHUNKS) * SUBLANES


def _moe_ffn_kernel(src_ref, dst_ref, bgen_ref, elist_ref, meta_ref, used_ref,
                    xs_hbm, w1_hbm, w3_hbm, w2_hbm,
                    ys_hbm,
                    xbuf, ybuf, hbuf, zbuf, wst1, wst3, wst2, w1b, w3b, w2b,
                    sem_in, sem_out, sem_zero, sem_dump, sem_w):
    n_areas = used_ref.shape[0]
    total = meta_ref[0]
    n_gen = meta_ref[1]
    dump_row0 = n_areas * AREA_ROWS
    lookahead = GATHER_BUFS - 1
    any_rows = pl.ds(0, SUBLANES)

    def chunk_at(ref, idx):
        return pl.ds(pl.multiple_of(ref[idx] * SUBLANES, SUBLANES), SUBLANES)

    def in_copy(rows, sl, c):
        return pltpu.make_async_copy(xs_hbm.at[rows], xbuf.at[sl, pl.ds(c * SUBLANES, SUBLANES)], sem_in.at[sl])

    def out_copy(rows, sl, c):
        return pltpu.make_async_copy(ybuf.at[sl, pl.ds(c * SUBLANES, SUBLANES)], ys_hbm.at[rows], sem_out.at[sl])

    def start_gather(blk):
        for c in range(BLOCK_CHUNKS):
            in_copy(chunk_at(src_ref, blk * BLOCK_CHUNKS + c), blk % GATHER_BUFS, c).start()

    def wait_gather(blk):
        for c in range(BLOCK_CHUNKS):
            in_copy(any_rows, blk % GATHER_BUFS, c).wait()

    def start_scatter(blk):
        for c in range(BLOCK_CHUNKS):
            out_copy(chunk_at(dst_ref, (blk + SCATTER_LAG) * BLOCK_CHUNKS + c), blk % SCATTER_BUFS, c).start()

    def wait_scatter(sl):
        for c in range(BLOCK_CHUNKS):
            out_copy(any_rows, sl, c).wait()

    def load_rows(blk):
        return _unpack_rows(xbuf[blk % GATHER_BUFS])

    def issue_dmas(blk):
        start_scatter(blk - SCATTER_LAG)
        start_gather(blk + lookahead)

    def up_gate(blk, rows, ws):
        x_lo, x_hi = rows
        a1 = (jnp.dot(x_lo, w1b[ws, 0:HALF, :], preferred_element_type=F32)
              + jnp.dot(x_hi, w1b[ws, HALF:, :], preferred_element_type=F32))
        a3 = (jnp.dot(x_lo, w3b[ws, 0:HALF, :], preferred_element_type=F32)
              + jnp.dot(x_hi, w3b[ws, HALF:, :], preferred_element_type=F32))
        hbuf[blk % SCATTER_BUFS] = (a1 * jax.nn.sigmoid(a1) * a3).astype(BF16)

    def down(blk, ws):
        y = jnp.dot(hbuf[blk % SCATTER_BUFS], w2b[ws], preferred_element_type=F32)
        ybuf[blk % SCATTER_BUFS] = _pack_rows(y.astype(BF16).astype(F32))

    def for_each_weight_copy(gen, fn):
        ws = gen % 2
        expert = elist_ref[gen]
        for hbm, staging in ((w1_hbm, wst1), (w3_hbm, wst3), (w2_hbm, wst2)):
            fn(pltpu.make_async_copy(hbm.at[expert], staging.at[ws], sem_w.at[ws]))

    def switch_weights(gen):
        ws = gen % 2
        for_each_weight_copy(gen, lambda cp: cp.wait())
        w1b[ws] = wst1[ws].astype(BF16)
        w3b[ws] = wst3[ws].astype(BF16)
        w2b[ws] = wst2[ws].astype(BF16)

        @pl.when(gen + 1 < n_gen)
        def _():
            for_each_weight_copy(gen + 1, lambda cp: cp.start())

    def for_each_tail_piece(fn):
        def area_body(s, carry):
            used = used_ref[s]
            tail = AREA_CHUNKS - used
            row = (s * AREA_CHUNKS + used) * SUBLANES
            for bit in ZERO_BITS:
                take = (tail & bit) != 0

                @pl.when(take)
                def _(row=row, bit=bit):
                    fn(pltpu.make_async_copy(zbuf.at[pl.ds(0, bit * SUBLANES)],
                                             ys_hbm.at[pl.ds(pl.multiple_of(row, SUBLANES), bit * SUBLANES)],
                                             sem_zero))
                row = row + jnp.where(take, bit * SUBLANES, 0)
            return carry
        lax.fori_loop(0, n_areas, area_body, 0)

    @pl.when(n_gen > 0)
    def _():
        for_each_weight_copy(0, lambda cp: cp.start())

    zbuf[...] = jnp.zeros(zbuf.shape, U32)
    ybuf[...] = jnp.zeros(ybuf.shape, U32)
    dump = pltpu.make_async_copy(zbuf.at[pl.ds(0, DUMP_CHUNKS * SUBLANES)],
                                 ys_hbm.at[pl.ds(dump_row0, DUMP_CHUNKS * SUBLANES)], sem_dump)
    dump.start()
    dump.wait()
    for_each_tail_piece(lambda cp: cp.start())
    for sl in range(SCATTER_BUFS - SCATTER_LAG):
        for c in range(BLOCK_CHUNKS):
            out_copy(pl.ds(dump_row0 + (sl * BLOCK_CHUNKS + c) * SUBLANES, SUBLANES), sl, c).start()
    for k in range(lookahead):
        start_gather(k)

    @pl.when(total > 0)
    def _():
        switch_weights(0)
        wait_gather(0)
        rows = load_rows(0)
        issue_dmas(0)
        up_gate(0, rows, 0)

        def block_body(blk, carry):
            gen = bgen_ref[blk]
            gen_prev = bgen_ref[blk - 1]

            @pl.when(gen != gen_prev)
            def _():
                switch_weights(gen)

            wait_gather(blk)
            wait_scatter((blk - 1) % SCATTER_BUFS)
            rows = load_rows(blk)
            issue_dmas(blk)
            down(blk - 1, gen_prev % 2)
            up_gate(blk, rows, gen % 2)
            return carry

        lax.fori_loop(1, total, block_body, 0)
        wait_scatter((total - 1) % SCATTER_BUFS)
        down(total - 1, bgen_ref[total - 1] % 2)

    for k in range(SCATTER_LAG):
        start_scatter(total - SCATTER_LAG + k)
    for sl in range(SCATTER_BUFS):
        wait_scatter(sl)
    for k in range(lookahead):
        wait_gather(total + k)
    for_each_tail_piece(lambda cp: cp.wait())


def _moe_combine_kernel(h_ref, route_ref, ys_ref, l2g_ref, l2b_ref, y_ref):
    n = h_ref.shape[0]
    scope = n // (ys_ref.shape[0] // AREA_ROWS)
    s_iota = lax.broadcasted_iota(jnp.int32, (scope, AREA_ROWS), 1).astype(F32)
    for sc in range(n // scope):
        rows = slice(sc * scope, (sc + 1) * scope)
        route = route_ref[rows, :]
        slot1 = route[:, 0:1]
        slot2 = route[:, 1:2]
        gate1 = route[:, 2:3]
        gate2 = route[:, 3:4]
        gmat = (jnp.where(s_iota == slot1, gate1, 0.0) + jnp.where(s_iota == slot2, gate2, 0.0)).astype(BF16)
        y_lo, y_hi = _unpack_rows(ys_ref[sc * AREA_ROWS:(sc + 1) * AREA_ROWS, :])
        f = jnp.concatenate([jnp.dot(gmat, y_lo, preferred_element_type=F32),
                             jnp.dot(gmat, y_hi, preferred_element_type=F32)], axis=1)
        y_ref[rows, :] = _layer_norm(ALPHA * h_ref[rows, :] + f, l2g_ref[...], l2b_ref[...])


def _excl_cumsum(a, axis):
    return jnp.cumsum(a, axis=axis) - a


def _plan_blocks(cnt, q_max):
    n_areas = cnt.shape[0]
    c8 = (cnt + (SUBLANES - 1)) // SUBLANES
    used8 = jnp.sum(c8, axis=1)
    base8 = jnp.arange(n_areas, dtype=jnp.int32) * AREA_CHUNKS
    src_runs = (base8[:, None] + _excl_cumsum(c8, 1)).T
    len_runs = c8.T
    tot = jnp.sum(len_runs, axis=1)
    ptot = (tot + BLOCK_CHUNKS - 1) // BLOCK_CHUNKS * BLOCK_CHUNKS
    pstart = _excl_cumsum(ptot, 0)
    dst_runs = pstart[:, None] + _excl_cumsum(len_runs, 1)
    off = (src_runs - dst_runs).reshape(-1)
    diff = off - jnp.concatenate([jnp.zeros((1,), jnp.int32), off[:-1]])
    q = jnp.arange(q_max, dtype=jnp.int32)
    src = q + jnp.sum(jnp.where(dst_runs.reshape(1, -1) <= q[:, None], diff[None, :], 0), axis=1)
    valid = jnp.any((pstart[None, :] <= q[:, None]) & (q[:, None] < (pstart + tot)[None, :]), axis=1)
    zero_chunk = AREA_CHUNKS - 1
    ahead = jnp.full(((GATHER_BUFS - 1) * BLOCK_CHUNKS,), zero_chunk, jnp.int32)
    src_tab = jnp.concatenate([jnp.where(valid, src, zero_chunk).astype(jnp.int32), ahead])
    qv = jnp.arange(-SCATTER_LAG * BLOCK_CHUNKS, q_max, dtype=jnp.int32)
    dump = n_areas * AREA_CHUNKS + ((qv // BLOCK_CHUNKS) % SCATTER_BUFS) * BLOCK_CHUNKS + qv % BLOCK_CHUNKS
    lead = jnp.zeros((SCATTER_LAG * BLOCK_CHUNKS,), jnp.int32)
    dst_tab = jnp.where(jnp.concatenate([lead > 0, valid]), jnp.concatenate([lead, src]), dump).astype(jnp.int32)
    nblk = ptot // BLOCK_CHUNKS
    bstart = pstart // BLOCK_CHUNKS
    has = nblk > 0
    gen_of = jnp.cumsum(has.astype(jnp.int32)) - 1
    experts = jnp.arange(N_EXPERTS, dtype=jnp.int32)
    elist = jnp.sum(jnp.where(has[None, :] & (gen_of[None, :] == experts[:, None]), experts[None, :], 0), axis=1)
    blk = jnp.arange(q_max // BLOCK_CHUNKS, dtype=jnp.int32)
    owner = (bstart[None, :] <= blk[:, None]) & (blk[:, None] < (bstart + nblk)[None, :])
    bgen = jnp.sum(jnp.where(owner, gen_of[None, :], 0), axis=1)
    meta = jnp.stack([jnp.sum(nblk), jnp.sum(has.astype(jnp.int32))])
    return (src_tab, dst_tab, bgen.astype(jnp.int32), elist.astype(jnp.int32), meta.astype(jnp.int32),
            used8.astype(jnp.int32))


def _rope_tables(pos):
    half = HEAD_DIM // 2
    inv = (ROPE_THETA ** (-np.arange(half, dtype=np.float64) / half)).astype(np.float32)
    ang = (np.asarray(pos, np.float32)[:, None] * inv).astype(np.float64)
    cos = np.cos(ang)
    sin = np.sin(ang)
    reps = LANES // HEAD_DIM
    cos_t = np.concatenate([cos, cos] * reps, axis=1)
    sin_t = np.concatenate([-sin, sin] * reps, axis=1)
    q_scale = ATTN_SCALE * LOG2E
    return jnp.asarray(np.concatenate([cos_t, sin_t, cos_t * q_scale, sin_t * q_scale], axis=1), F32)


def _full(shape):
    nd = len(shape)
    return pl.BlockSpec(shape, lambda *_: (0,) * nd)


def kernel(x_prompt, x_sample, cache_k, cache_v, state_conv, w_in, b_in, sinks, conv_w, conv_b, ln_conv_g,
           ln_conv_b, w_out, b_out, ln1_g, ln1_b, w_router_group, w_router_expert, w1, w3, w2, ln2_g, ln2_b):
    assert w_in.shape[0] == DEPTH
    bsz, seq, _ = x_prompt.shape
    dec_b, dec_t, _ = x_sample.shape
    assert dec_t == 1 and seq % TILE_Q == 0 and dec_b % SAMPLE_TB == 0 and 2 * dec_b + 7 * N_EXPERTS <= AREA_ROWS
    assert (bsz * seq) % COMBINE_TILE == 0
    n_prompt = bsz * seq
    tiles = seq // TILE_Q
    spt = TILE_Q // PROMPT_SCOPE
    p_areas = n_prompt // PROMPT_SCOPE
    n_areas = p_areas + spt
    q_max = (n_areas * AREA_CHUNKS + N_EXPERTS * (BLOCK_CHUNKS - 1) + BLOCK_CHUNKS - 1) // BLOCK_CHUNKS * BLOCK_CHUNKS

    row = lambda a: a.reshape(1, -1)
    qk = ATTN_DIM + KV_DIM
    half = HEAD_DIM // 2

    def swap_halves(a):
        parts = a[..., :qk].reshape(a.shape[:-1] + (qk // HEAD_DIM, 2, half))
        return parts[..., ::-1, :].reshape(a.shape[:-1] + (qk,))

    w_in_b = w_in[0].astype(BF16)
    w_sw_b = swap_halves(w_in[0]).astype(BF16)
    w_out_b = w_out[0].astype(BF16)
    w_r = jnp.concatenate([w_router_group[0], w_router_expert[0].reshape(D_MODEL, N_EXPERTS),
                           jnp.zeros((D_MODEL, LANES - ELANE0 - N_EXPERTS), F32)], axis=1).astype(BF16)
    shared = (w_in_b, row(b_in[0]), w_sw_b, row(swap_halves(b_in[0])), conv_w[0], row(conv_b[0]),
              row(ln_conv_g[0]), row(ln_conv_b[0]), w_out_b, row(b_out[0]), row(ln1_g[0]), row(ln1_b[0]), w_r)
    shared_specs = [_full(a.shape) for a in shared]

    def glu_blocks(a):
        lo = ATTN_DIM + 2 * KV_DIM
        vg = a[..., lo:lo + 2 * CONV_CH].reshape(a.shape[:-1] + (2, CONV_CH // LANES, LANES))
        return jnp.swapaxes(vg, -3, -2).reshape(a.shape[:-1] + (2 * CONV_CH,))

    glu = (glu_blocks(w_in[0]).astype(BF16), row(glu_blocks(b_in[0])))
    glu_specs = [_full(a.shape) for a in glu]
    assert len(glu) + len(shared) == N_PROMPT_WEIGHTS
    smem = pl.BlockSpec(memory_space=pltpu.SMEM)
    rope_p = _rope_tables(np.arange(seq))
    rope_s = _rope_tables(PAST_LEN + np.arange(dec_t))

    cparams = lambda sem: pltpu.CompilerParams(dimension_semantics=sem, vmem_limit_bytes=VMEM_LIMIT)

    ck = cache_k[0].reshape(dec_b, WINDOW, KV_DIM)
    cv = cache_v[0].reshape(dec_b, WINDOW, KV_DIM)
    tb = SAMPLE_TB
    conv_state_spec = pl.BlockSpec((None, tb, CONV_WIDTH - 1, CONV_CH), lambda t: (0, t, 0, 0))
    h_s, nk_s, nv_s, nc_s, xs_s, route_s, cnt_s = pl.pallas_call(
        _mix_sample_kernel,
        grid=(dec_b // tb,),
        in_specs=[smem,
                  pl.BlockSpec((tb, None, D_MODEL), lambda t: (t, 0, 0)),
                  _full(rope_s.shape),
                  pl.BlockSpec((tb, WINDOW, KV_DIM), lambda t: (t, 0, 0)),
                  pl.BlockSpec((tb, WINDOW, KV_DIM), lambda t: (t, 0, 0)),
                  conv_state_spec] + shared_specs,
        out_specs=[pl.BlockSpec((tb, D_MODEL), lambda t: (t, 0)),
                   pl.BlockSpec((tb, WINDOW, KV_DIM), lambda t: (t, 0, 0)),
                   pl.BlockSpec((tb, WINDOW, KV_DIM), lambda t: (t, 0, 0)),
                   conv_state_spec,
                   _full((AREA_ROWS, HALF)),
                   _full((dec_b, LANES)),
                   _full((1, SUBLANES, LANES))],
        out_shape=[jax.ShapeDtypeStruct((dec_b, D_MODEL), F32),
                   jax.ShapeDtypeStruct((dec_b, WINDOW, KV_DIM), F32),
                   jax.ShapeDtypeStruct((dec_b, WINDOW, KV_DIM), F32),
                   jax.ShapeDtypeStruct((DEPTH, dec_b, CONV_WIDTH - 1, CONV_CH), F32),
                   jax.ShapeDtypeStruct((AREA_ROWS, HALF), U32),
                   jax.ShapeDtypeStruct((dec_b, LANES), F32),
                   jax.ShapeDtypeStruct((1, SUBLANES, LANES), F32)],
        scratch_shapes=[pltpu.VMEM((dec_b, D_MODEL), F32)],
        compiler_params=cparams(("arbitrary",)),
        name="mix_sample",
    )(sinks[0], x_sample, rope_s, ck, cv, state_conv, *shared)

    n_real = bsz * tiles
    real = lambda t: jnp.minimum(t, n_real - 1)
    h_p, xs, route_p, cnt_p, nk_p, nv_p, nc_p = pl.pallas_call(
        functools.partial(_mix_prompt_kernel, tiles=tiles),
        grid=(n_real + 1,),
        in_specs=[smem,
                  pl.BlockSpec((1, TILE_Q, D_MODEL), lambda t: (real(t) // tiles, real(t) % tiles, 0)),
                  pl.BlockSpec((TILE_Q, 4 * LANES), lambda t: (real(t) % tiles, 0)),
                  _full((AREA_ROWS, HALF))] + glu_specs + shared_specs,
        out_specs=[pl.BlockSpec((1, TILE_Q, D_MODEL), lambda t: (real(t) // tiles, real(t) % tiles, 0)),
                   pl.BlockSpec((spt * AREA_ROWS, HALF), lambda t: (t, 0)),
                   pl.BlockSpec((TILE_Q, LANES), lambda t: (real(t), 0)),
                   pl.BlockSpec((spt, SUBLANES, LANES), lambda t: (t, 0, 0)),
                   pl.BlockSpec((1, WINDOW, KV_DIM), lambda t: (real(t) // tiles, 0, 0)),
                   pl.BlockSpec((1, WINDOW, KV_DIM), lambda t: (real(t) // tiles, 0, 0)),
                   pl.BlockSpec((1, CONV_WIDTH - 1, CONV_CH), lambda t: (real(t) // tiles, 0, 0))],
        out_shape=[jax.ShapeDtypeStruct((bsz, seq, D_MODEL), F32),
                   jax.ShapeDtypeStruct((n_areas * AREA_ROWS, HALF), U32),
                   jax.ShapeDtypeStruct((n_prompt, LANES), F32),
                   jax.ShapeDtypeStruct((n_areas, SUBLANES, LANES), F32),
                   jax.ShapeDtypeStruct((bsz, WINDOW, KV_DIM), F32),
                   jax.ShapeDtypeStruct((bsz, WINDOW, KV_DIM), F32),
                   jax.ShapeDtypeStruct((bsz, CONV_WIDTH - 1, CONV_CH), F32)],
        scratch_shapes=[pltpu.VMEM((WINDOW, KV_DIM), F32), pltpu.VMEM((WINDOW, KV_DIM), F32),
                        pltpu.VMEM((TILE_Q + CONV_PRE, CONV_CH), F32),
                        pltpu.VMEM((SUBLANES - 1, TILE_Q + CONV_SHIFT_EXTRA, CONV_CH), F32)],
        compiler_params=cparams(("arbitrary",)),
        name="mix_prompt",
    )(sinks[0], x_prompt, rope_p, xs_s, *glu, *shared)

    cnt = jnp.concatenate([cnt_p[:p_areas, 0], cnt_s[:, 0], cnt_p[p_areas + 1:, 0]], axis=0)
    cnt = cnt[:, ELANE0:ELANE0 + N_EXPERTS].astype(jnp.int32)
    plan = _plan_blocks(cnt, q_max)
    hbm = pl.BlockSpec(memory_space=pl.ANY)
    up_shape, down_shape = (2, D_MODEL, D_FF_EXPERT), (2, D_FF_EXPERT, D_MODEL)
    ys = pl.pallas_call(
        _moe_ffn_kernel,
        grid_spec=pltpu.PrefetchScalarGridSpec(
            num_scalar_prefetch=len(plan),
            grid=(1,),
            in_specs=[hbm, hbm, hbm, hbm],
            out_specs=hbm,
            scratch_shapes=[pltpu.VMEM((GATHER_BUFS, BLOCK_ROWS, HALF), U32),
                            pltpu.VMEM((SCATTER_BUFS, BLOCK_ROWS, HALF), U32),
                            pltpu.VMEM((SCATTER_BUFS, BLOCK_ROWS, D_FF_EXPERT), BF16),
                            pltpu.VMEM((ZERO_ROWS, HALF), U32),
                            pltpu.VMEM(up_shape, F32), pltpu.VMEM(up_shape, F32), pltpu.VMEM(down_shape, F32),
                            pltpu.VMEM(up_shape, BF16), pltpu.VMEM(up_shape, BF16), pltpu.VMEM(down_shape, BF16),
                            pltpu.SemaphoreType.DMA((GATHER_BUFS,)), pltpu.SemaphoreType.DMA((SCATTER_BUFS,)),
                            pltpu.SemaphoreType.DMA(()), pltpu.SemaphoreType.DMA(()),
                            pltpu.SemaphoreType.DMA((2,))]),
        out_shape=jax.ShapeDtypeStruct((n_areas * AREA_ROWS + DUMP_CHUNKS * SUBLANES, HALF), U32),
        compiler_params=cparams(("arbitrary",)),
        name="moe_ffn",
    )(*plan, xs, w1[0], w3[0], w2[0])

    def combine(h2d, route, first_area, scope, tile):
        n = h2d.shape[0]
        apt = tile // scope
        return pl.pallas_call(
            _moe_combine_kernel,
            grid=(n // tile,),
            in_specs=[pl.BlockSpec((tile, D_MODEL), lambda t: (t, 0)),
                      pl.BlockSpec((tile, LANES), lambda t: (t, 0)),
                      pl.BlockSpec((apt * AREA_ROWS, HALF), lambda t: (first_area // apt + t, 0)),
                      _full((1, D_MODEL)), _full((1, D_MODEL))],
            out_specs=pl.BlockSpec((tile, D_MODEL), lambda t: (t, 0)),
            out_shape=jax.ShapeDtypeStruct((n, D_MODEL), F32),
            compiler_params=cparams(("arbitrary",)),
            name="moe_combine",
        )(h2d, route, ys, row(ln2_g[0]), row(ln2_b[0]))

    y_p = combine(h_p.reshape(n_prompt, D_MODEL), route_p, 0, PROMPT_SCOPE, COMBINE_TILE)
    y_s = combine(h_s, route_s, p_areas, dec_b, dec_b)

    kv_shape = lambda n: (DEPTH, n, WINDOW, N_KV_HEADS, HEAD_DIM)
    return (y_p.reshape(bsz, seq, D_MODEL), y_s.reshape(dec_b, dec_t, D_MODEL),
            nk_p.reshape(kv_shape(bsz)), nv_p.reshape(kv_shape(bsz)), nc_p[None],
            nk_s.reshape(kv_shape(dec_b)), nv_s.reshape(kv_shape(dec_b)), nc_s)
```

```python
import functools

import jax
import jax.numpy as jnp
import numpy as np
from jax import lax
from jax.experimental import pallas as pl
from jax.experimental.pallas import tpu as pltpu

F32 = jnp.float32
BF16 = jnp.bfloat16
U32 = jnp.uint32

D_MODEL = 1024
N_HEADS = 8
N_KV_HEADS = 2
HEAD_DIM = 64
Q_GROUP = N_HEADS // N_KV_HEADS
ATTN_DIM = N_HEADS * HEAD_DIM
KV_DIM = N_KV_HEADS * HEAD_DIM
WINDOW = 128
ROPE_THETA = 10000.0
ATTN_SCALE = HEAD_DIM ** -0.5
CONV_CH = D_MODEL - ATTN_DIM
CONV_WIDTH = 31
IN_DIM = ATTN_DIM + 2 * KV_DIM + 2 * CONV_CH
N_GROUPS = 4
EXPERTS_PER_GROUP = 8
N_EXPERTS = N_GROUPS * EXPERTS_PER_GROUP
D_FF_EXPERT = 256
LN_EPS = 1e-5
NEG_INF = -1e30
DEPTH = 1
ALPHA = (2.0 * DEPTH) ** 0.25
PAST_LEN = 16384
LOG2E = 1.4426950408889634

LANES = 128
SUBLANES = 8
HALF = D_MODEL // 2
ELANE0 = N_GROUPS
PROMPT_SCOPE = 256
AREA_ROWS = -(-(2 * PROMPT_SCOPE + (SUBLANES - 1) * N_EXPERTS) // LANES) * LANES
AREA_CHUNKS = AREA_ROWS // SUBLANES
BLOCK_CHUNKS = 32
BLOCK_ROWS = BLOCK_CHUNKS * SUBLANES
TILE_Q = 512
COMBINE_TILE = 1024
CONV_PRE = 32
CONV_SHIFT_EXTRA = 24
SAMPLE_TB = 32
SAMPLE_GRP = 8
VMEM_LIMIT = 56 * 1024 * 1024


def _layer_norm(x, g, b):
    mu = jnp.mean(x, axis=-1, keepdims=True)
    xc = x - mu
    var = jnp.mean(xc * xc, axis=-1, keepdims=True)
    return xc * lax.rsqrt(var + LN_EPS) * g + b


def _rope(x, x_swapped, cos, sin_signed):
    pieces = [x[:, g * LANES:(g + 1) * LANES] * cos + x_swapped[:, g * LANES:(g + 1) * LANES] * sin_signed
              for g in range(x.shape[-1] // LANES)]
    return pieces[0] if len(pieces) == 1 else jnp.concatenate(pieces, axis=-1)


def _in_proj_glu(xb, w_in_ref, b_in_ref):
    lo, hi = ATTN_DIM + 2 * KV_DIM, IN_DIM
    cacg = jnp.dot(xb, w_in_ref[:, lo:hi], preferred_element_type=F32) + b_in_ref[:, lo:hi]
    return cacg[:, :CONV_CH] * jax.nn.sigmoid(cacg[:, CONV_CH:])


def _in_proj_qkv(xb, w_in_ref, b_in_ref, w_sw_ref, b_sw_ref, rope):
    o_v, o_e = ATTN_DIM + KV_DIM, ATTN_DIM + 2 * KV_DIM
    qkv = jnp.dot(xb, w_in_ref[:, 0:o_e], preferred_element_type=F32) + b_in_ref[:, 0:o_e]
    sw = jnp.dot(xb, w_sw_ref[...], preferred_element_type=F32) + b_sw_ref[...]
    k = _rope(qkv[:, ATTN_DIM:o_v], sw[:, ATTN_DIM:], rope[:, 0:LANES], rope[:, LANES:2 * LANES])
    q = _rope(qkv[:, :ATTN_DIM], sw[:, :ATTN_DIM], rope[:, 2 * LANES:3 * LANES], rope[:, 3 * LANES:])
    return q, k, qkv[:, o_v:]


def _pack_rows(v):
    lo = lax.bitcast_convert_type(v[:, :HALF], U32)
    hi = lax.bitcast_convert_type(v[:, HALF:], U32)
    return (lo >> 16) | (hi & jnp.uint32(0xFFFF0000))


def _unpack_rows(w):
    lo = lax.bitcast_convert_type(w << 16, F32).astype(BF16)
    hi = lax.bitcast_convert_type(w & jnp.uint32(0xFFFF0000), F32).astype(BF16)
    return lo, hi


def _drain(stages):
    try:
        while True:
            next(stages)
    except StopIteration as stop:
        return stop.value


def _round_robin(*stage_generators):
    live = list(stage_generators)
    while live:
        for stages in list(live):
            try:
                next(stages)
            except StopIteration:
                live.remove(stages)


def _route_and_sort_stages(h, w_r):
    n = h.shape[0]
    hb = h.astype(BF16)
    logits = jnp.dot(hb, w_r, preferred_element_type=F32)
    yield
    lane = lax.broadcasted_iota(jnp.int32, (n, LANES), 1)
    lane_f = lane.astype(F32)
    is_g = lane < N_GROUPS
    lg = jnp.where(is_g, logits, -jnp.inf)
    gmax = jnp.max(lg, axis=-1, keepdims=True)
    yield
    gidx = jnp.min(jnp.where(lg == gmax, lane_f, float(LANES)), axis=-1, keepdims=True)
    p_g = 1.0 / jnp.sum(jnp.where(is_g, jnp.exp(logits - gmax), 0.0), axis=-1, keepdims=True)
    yield
    egrp = ((lane - ELANE0) >> (EXPERTS_PER_GROUP.bit_length() - 1)).astype(F32)
    emask = (lane >= ELANE0) & (lane < ELANE0 + N_EXPERTS) & (egrp == gidx)
    sel = jnp.where(emask, logits, -jnp.inf)
    v1 = jnp.max(sel, axis=-1, keepdims=True)
    yield
    i1 = jnp.min(jnp.where(sel == v1, lane_f, float(LANES)), axis=-1, keepdims=True)
    yield
    sel2 = jnp.where(lane_f == i1, -jnp.inf, sel)
    v2 = jnp.max(sel2, axis=-1, keepdims=True)
    yield
    i2 = jnp.min(jnp.where(sel2 == v2, lane_f, float(LANES)), axis=-1, keepdims=True)
    t = jnp.exp(v2 - v1)
    gate1 = p_g / (1.0 + t)
    gate2 = p_g * t / (1.0 + t)
    pick1 = lane_f == i1
    pick2 = lane_f == i2
    member = jnp.where(pick1 | pick2, 1.0, 0.0)
    yield

    r_i = lax.broadcasted_iota(jnp.int32, (n, n), 0)
    c_i = lax.broadcasted_iota(jnp.int32, (n, n), 1)
    lower = jnp.where(c_i < r_i, 1.0, 0.0).astype(BF16)
    e_r = lax.broadcasted_iota(jnp.int32, (LANES, LANES), 0)
    e_c = lax.broadcasted_iota(jnp.int32, (LANES, LANES), 1)
    upper = jnp.where(e_r < e_c, 1.0, 0.0).astype(BF16)
    before = jnp.dot(lower, member.astype(BF16), preferred_element_type=F32)
    cnt = jnp.sum(member, axis=0, keepdims=True)
    c8 = jnp.floor((cnt + (SUBLANES - 1)) * (1.0 / SUBLANES))
    c8b = jnp.broadcast_to(c8, (SUBLANES, LANES)).astype(BF16)
    off8 = jnp.dot(c8b, upper, preferred_element_type=F32)[0:1]
    yield
    slot_all = before + off8 * float(SUBLANES)
    slot1 = jnp.sum(jnp.where(pick1, slot_all, 0.0), axis=-1, keepdims=True)
    slot2 = jnp.sum(jnp.where(pick2, slot_all, 0.0), axis=-1, keepdims=True)
    yield
    s_iota = lax.broadcasted_iota(jnp.int32, (n, AREA_ROWS), 1).astype(F32)
    perm_t = jnp.where((s_iota == slot1) | (s_iota == slot2), 1.0, 0.0).astype(BF16)
    sorted_rows = lax.dot_general(perm_t, hb, (((0,), (0,)), ((), ())),
                                  preferred_element_type=F32)
    yield
    route = jnp.where(lane == 0, slot1, jnp.where(lane == 1, slot2, jnp.where(lane == 2, gate1,
                      jnp.where(lane == 3, gate2, 0.0))))
    return route, _pack_rows(sorted_rows), cnt


def _softmax_rows_with_sink(s, sink, extra=None):
    m = jnp.maximum(jnp.max(s, axis=-1, keepdims=True), sink)
    if extra is not None:
        m = jnp.maximum(m, extra)
    e = jnp.exp2(s - m)
    den = jnp.sum(e, axis=-1, keepdims=True) + jnp.exp2(sink - m)
    if extra is not None:
        ee = jnp.exp2(extra - m)
        den = den + ee
        return e / den, ee / den
    return e / den, None


def _mix_out_stages(x, attn, conv_pre, lcg, lcb, w_out_ref, b_out, l1g, l1b):
    conv = _layer_norm(conv_pre, lcg, lcb)
    conv = conv * jax.nn.sigmoid(conv)
    yield
    mix = (jnp.dot(attn.astype(BF16), w_out_ref[0:ATTN_DIM, :], preferred_element_type=F32)
           + jnp.dot(conv.astype(BF16), w_out_ref[ATTN_DIM:, :], preferred_element_type=F32) + b_out)
    yield
    h = _layer_norm(ALPHA * x + mix, l1g, l1b)
    yield
    return h


N_PROMPT_WEIGHTS = 15


def _mix_prompt_kernel(sinks_ref, x_ref, rope_ref, xs_s_ref, *rest, tiles):
    _, xs_ref, _, cnt_ref = rest[N_PROMPT_WEIGHTS:N_PROMPT_WEIGHTS + 4]
    step = pl.program_id(0)
    n_real = pl.num_programs(0) - 1

    @pl.when(step < n_real)
    def _():
        _mix_prompt_tile(sinks_ref, x_ref, rope_ref, *rest, i=step % tiles)

    @pl.when(step == n_real)
    def _():
        xs_ref[0:AREA_ROWS, :] = xs_s_ref[...]
        xs_ref[AREA_ROWS:, :] = jnp.zeros((xs_ref.shape[0] - AREA_ROWS, HALF), U32)
        cnt_ref[...] = jnp.zeros(cnt_ref.shape, F32)


def _mix_prompt_tile(sinks_ref, x_ref, rope_ref, w_glu_ref, b_glu_ref,
                     w_in_ref, b_in_ref, w_sw_ref, b_sw_ref, conv_w_ref, conv_b_ref,
                     lcg_ref, lcb_ref, w_out_ref, b_out_ref, l1g_ref, l1b_ref, w_r_ref,
                     h_ref, xs_ref, route_ref, cnt_ref, nk_ref, nv_ref, nc_ref,
                     kprev, vprev, u_scr, u_shift, *, i):
    tq = x_ref.shape[1]
    pre = CONV_PRE
    n_sc = tq // PROMPT_SCOPE

    @pl.when(i == 0)
    def _():
        kprev[...] = jnp.zeros_like(kprev)
        vprev[...] = jnp.zeros_like(vprev)
        u_scr[0:pre, :] = jnp.zeros((pre, CONV_CH), F32)

    x = x_ref[0]
    xb = x.astype(BF16)

    def glu_block(cb):
        cols = slice(2 * LANES * cb, 2 * LANES * (cb + 1))
        vg = jnp.dot(xb, w_glu_ref[:, cols], preferred_element_type=F32) + b_glu_ref[:, cols]
        return vg[:, :LANES] * jax.nn.sigmoid(vg[:, LANES:])

    n_cb = CONV_CH // LANES
    acc_blocks = [[None] * n_cb for _ in range(n_sc)]
    u_next = glu_block(0)
    for cb in range(n_cb):
        lanes = slice(LANES * cb, LANES * (cb + 1))
        u_scr[pre:pre + tq, lanes] = u_next
        if cb + 1 < n_cb:
            u_next = glu_block(cb + 1)
        for r in range(1, SUBLANES):
            u_shift[r - 1, :, lanes] = u_scr[r:r + tq + CONV_SHIFT_EXTRA, lanes]
        for sc in range(n_sc):
            base = sc * PROMPT_SCOPE
            acc = jnp.zeros((PROMPT_SCOPE, LANES), F32)
            for j in range(CONV_WIDTH):
                a, r = divmod(pre - (CONV_WIDTH - 1) + j, SUBLANES)
                start = base + a * SUBLANES
                src = (u_scr[start:start + PROMPT_SCOPE, lanes] if r == 0
                       else u_shift[r - 1, start:start + PROMPT_SCOPE, lanes])
                acc = acc + src * conv_w_ref[j:j + 1, lanes]
            acc_blocks[sc][cb] = acc
    conv_pre = [jnp.concatenate(blocks, axis=1) + conv_b_ref[...] for blocks in acc_blocks]

    nc_ref[0] = u_scr[pre + tq - (CONV_WIDTH - 1):pre + tq, :]
    tail = u_scr[tq:tq + pre, :]
    u_scr[0:pre, :] = tail

    q, k, v = _in_proj_qkv(xb, w_in_ref, b_in_ref, w_sw_ref, b_sw_ref, rope_ref[...])
    qb = q.astype(BF16)
    kb = jnp.concatenate([kprev[...], k], axis=0).astype(BF16)
    vb = jnp.concatenate([vprev[...], v], axis=0).astype(BF16)
    row = lax.broadcasted_iota(jnp.int32, (WINDOW, 2 * WINDOW), 0)
    col = lax.broadcasted_iota(jnp.int32, (WINDOW, 2 * WINDOW), 1)
    dist = row + WINDOW - col
    band = (dist >= 0) & (dist < WINDOW)
    first_col = jnp.where(i > 0, 0, WINDOW)
    kprev[...] = k[tq - WINDOW:, :]
    vprev[...] = v[tq - WINDOW:, :]
    nk_ref[0] = k[tq - WINDOW:, :]
    nv_ref[0] = v[tq - WINDOW:, :]
    blocks_per_scope = PROMPT_SCOPE // WINDOW
    attn_blocks = [None] * (tq // WINDOW)

    def scope_attention(sc):
        for j in range(sc * blocks_per_scope, (sc + 1) * blocks_per_scope):
            msk = (band & (col >= first_col)) if j == 0 else band
            heads = [None] * N_HEADS
            for c in range(N_KV_HEADS):
                kc = kb[j * WINDOW:(j + 2) * WINDOW, c * HEAD_DIM:(c + 1) * HEAD_DIM]
                vc = vb[j * WINDOW:(j + 2) * WINDOW, c * HEAD_DIM:(c + 1) * HEAD_DIM]
                qc = jnp.concatenate(
                    [qb[j * WINDOW:(j + 1) * WINDOW, (c * Q_GROUP + g) * HEAD_DIM:(c * Q_GROUP + g + 1) * HEAD_DIM]
                     for g in range(Q_GROUP)], axis=0)
                s = lax.dot_general(qc, kc, (((1,), (1,)), ((), ())), preferred_element_type=F32)
                yield
                probs = []
                for g in range(Q_GROUP):
                    sg = jnp.where(msk, s[g * WINDOW:(g + 1) * WINDOW], NEG_INF)
                    p, _ = _softmax_rows_with_sink(sg, sinks_ref[c * Q_GROUP + g] * LOG2E)
                    probs.append(p.astype(BF16))
                    if g % 2 == 1:
                        yield
                o = jnp.dot(jnp.concatenate(probs, axis=0), vc, preferred_element_type=F32)
                for g in range(Q_GROUP):
                    heads[c * Q_GROUP + g] = o[g * WINDOW:(g + 1) * WINDOW]
                yield
            attn_blocks[j] = jnp.concatenate(heads, axis=1)

    def scope_tail(sc):
        rows = slice(sc * PROMPT_SCOPE, (sc + 1) * PROMPT_SCOPE)
        attn = jnp.concatenate(attn_blocks[sc * blocks_per_scope:(sc + 1) * blocks_per_scope], axis=0)
        h = yield from _mix_out_stages(x[rows], attn, conv_pre[sc], lcg_ref[...], lcb_ref[...],
                                       w_out_ref, b_out_ref[...], l1g_ref[...], l1b_ref[...])
        route, area, cnt = yield from _route_and_sort_stages(h, w_r_ref[...])
        h_ref[0, rows, :] = h
        route_ref[rows, :] = route
        xs_ref[sc * AREA_ROWS:(sc + 1) * AREA_ROWS, :] = area
        cnt_ref[sc] = jnp.broadcast_to(cnt, (SUBLANES, LANES))

    _round_robin(*[scope_attention(sc) for sc in range(n_sc)])
    _round_robin(*[scope_tail(sc) for sc in range(n_sc)])


def _mix_sample_kernel(sinks_ref, x_ref, rope_ref, ck_ref, cv_ref, st_ref, w_in_ref, b_in_ref, w_sw_ref, b_sw_ref,
                       conv_w_ref, conv_b_ref, lcg_ref, lcb_ref, w_out_ref, b_out_ref, l1g_ref, l1b_ref,
                       w_r_ref,
                       h_ref, nk_ref, nv_ref, nc_ref, xs_ref, route_ref, cnt_ref,
                       h_scr):
    tb = x_ref.shape[0]
    n_tok = h_scr.shape[0]
    step = pl.program_id(0)
    x = x_ref[...]
    xb = x.astype(BF16)
    u = _in_proj_glu(xb, w_in_ref, b_in_ref)
    q, k_new, v_new = _in_proj_qkv(xb, w_in_ref, b_in_ref, w_sw_ref, b_sw_ref, rope_ref[...])

    qb = q.astype(BF16)
    knb = k_new.astype(BF16).astype(F32)
    vnb = v_new.astype(BF16).astype(F32)
    gq = SAMPLE_GRP * Q_GROUP
    nkeys = SAMPLE_GRP * WINDOW
    r_i = lax.broadcasted_iota(jnp.int32, (gq, nkeys), 0)
    c_i = lax.broadcasted_iota(jnp.int32, (gq, nkeys), 1)
    window_shift, grp_shift = WINDOW.bit_length() - 1, SAMPLE_GRP.bit_length() - 1
    msk = ((c_i >> window_shift) == (r_i & (SAMPLE_GRP - 1))) & ((c_i & (WINDOW - 1)) >= 1)
    r_col = lax.broadcasted_iota(jnp.int32, (gq, 1), 0) >> grp_shift
    attn_groups = []
    for grp in range(tb // SAMPLE_GRP):
        r0 = grp * SAMPLE_GRP
        kc_all = ck_ref[r0:r0 + SAMPLE_GRP].reshape(nkeys, KV_DIM).astype(BF16)
        vc_all = cv_ref[r0:r0 + SAMPLE_GRP].reshape(nkeys, KV_DIM).astype(BF16)
        heads = [None] * N_HEADS
        for c in range(N_KV_HEADS):
            lanes = slice(c * HEAD_DIM, (c + 1) * HEAD_DIM)
            qc = jnp.concatenate(
                [qb[r0:r0 + SAMPLE_GRP, (c * Q_GROUP + g) * HEAD_DIM:(c * Q_GROUP + g + 1) * HEAD_DIM]
                 for g in range(Q_GROUP)], axis=0)
            s = lax.dot_general(qc, kc_all[:, lanes], (((1,), (1,)), ((), ())), preferred_element_type=F32)
            s = jnp.where(msk, s, NEG_INF)
            kn = jnp.concatenate([knb[r0:r0 + SAMPLE_GRP, lanes]] * Q_GROUP, axis=0)
            vn = jnp.concatenate([vnb[r0:r0 + SAMPLE_GRP, lanes]] * Q_GROUP, axis=0)
            s_new = jnp.sum(qc.astype(F32) * kn, axis=-1, keepdims=True)
            sink = jnp.zeros((gq, 1), F32)
            for g in range(Q_GROUP):
                sink = jnp.where(r_col == g, sinks_ref[c * Q_GROUP + g] * LOG2E, sink)
            p, p_new = _softmax_rows_with_sink(s, sink, s_new)
            o = (jnp.dot(p.astype(BF16), vc_all[:, lanes], preferred_element_type=F32)
                 + p_new.astype(BF16).astype(F32) * vn)
            for g in range(Q_GROUP):
                heads[c * Q_GROUP + g] = o[g * SAMPLE_GRP:(g + 1) * SAMPLE_GRP]
        attn_groups.append(jnp.concatenate(heads, axis=1))
    attn = jnp.concatenate(attn_groups, axis=0)

    st = st_ref[...]
    acc = jnp.sum(st * conv_w_ref[0:CONV_WIDTH - 1, :][None], axis=1)
    conv_pre = acc + u * conv_w_ref[CONV_WIDTH - 1:CONV_WIDTH, :] + conv_b_ref[...]

    nk_ref[:, 0:WINDOW - 1, :] = ck_ref[:, 1:WINDOW, :]
    nv_ref[:, 0:WINDOW - 1, :] = cv_ref[:, 1:WINDOW, :]
    nc_ref[:, 0:CONV_WIDTH - 2, :] = st_ref[:, 1:CONV_WIDTH - 1, :]
    for b in range(tb):
        nk_ref[b, WINDOW - 1:WINDOW, :] = k_new[b:b + 1, :]
        nv_ref[b, WINDOW - 1:WINDOW, :] = v_new[b:b + 1, :]
        nc_ref[b, CONV_WIDTH - 2:CONV_WIDTH - 1, :] = u[b:b + 1, :]

    h = _drain(_mix_out_stages(x, attn, conv_pre, lcg_ref[...], lcb_ref[...], w_out_ref, b_out_ref[...],
                               l1g_ref[...], l1b_ref[...]))
    h_ref[...] = h
    h_scr[pl.ds(pl.multiple_of(step * tb, tb), tb), :] = h

    @pl.when(step == pl.num_programs(0) - 1)
    def _():
        route, area, cnt = _drain(_route_and_sort_stages(h_scr[...], w_r_ref[...]))
        route_ref[...] = route
        xs_ref[...] = area
        cnt_ref[0] = jnp.broadcast_to(cnt, (SUBLANES, LANES))


GATHER_BUFS = 8
SCATTER_BUFS = 8
SCATTER_LAG = 2
ZERO_BITS = (64, 32, 16, 8, 4, 2, 1)
DUMP_CHUNKS = SCATTER_BUFS * BLOCK_CHUNKS
ZERO_ROWS = max(ZERO_BITS[0], DUMP_CHUNKS) * SUBLANES


def _moe_ffn_kernel(src_ref, dst_ref, bgen_ref, elist_ref, meta_ref, used_ref,
                    xs_hbm, w1_hbm, w3_hbm, w2_hbm,
                    ys_hbm,
                    xbuf, ybuf, hbuf, zbuf, wst1, wst3, wst2, w1b, w3b, w2b,
                    sem_in, sem_out, sem_zero, sem_dump, sem_w):
    n_areas = used_ref.shape[0]
    total = meta_ref[0]
    n_gen = meta_ref[1]
    dump_row0 = n_areas * AREA_ROWS
    lookahead = GATHER_BUFS - 1
    any_rows = pl.ds(0, SUBLANES)

    def chunk_at(ref, idx):
        return pl.ds(pl.multiple_of(ref[idx] * SUBLANES, SUBLANES), SUBLANES)

    def in_copy(rows, sl, c):
        return pltpu.make_async_copy(xs_hbm.at[rows], xbuf.at[sl, pl.ds(c * SUBLANES, SUBLANES)], sem_in.at[sl])

    def out_copy(rows, sl, c):
        return pltpu.make_async_copy(ybuf.at[sl, pl.ds(c * SUBLANES, SUBLANES)], ys_hbm.at[rows], sem_out.at[sl])

    def start_gather(blk):
        for c in range(BLOCK_CHUNKS):
            in_copy(chunk_at(src_ref, blk * BLOCK_CHUNKS + c), blk % GATHER_BUFS, c).start()

    def wait_gather(blk):
        for c in range(BLOCK_CHUNKS):
            in_copy(any_rows, blk % GATHER_BUFS, c).wait()

    def start_scatter(blk):
        for c in range(BLOCK_CHUNKS):
            out_copy(chunk_at(dst_ref, (blk + SCATTER_LAG) * BLOCK_CHUNKS + c), blk % SCATTER_BUFS, c).start()

    def wait_scatter(sl):
        for c in range(BLOCK_CHUNKS):
            out_copy(any_rows, sl, c).wait()

    def load_rows(blk):
        return _unpack_rows(xbuf[blk % GATHER_BUFS])

    def issue_dmas(blk):
        start_scatter(blk - SCATTER_LAG)
        start_gather(blk + lookahead)

    def up_gate(blk, rows, ws):
        x_lo, x_hi = rows
        a1 = (jnp.dot(x_lo, w1b[ws, 0:HALF, :], preferred_element_type=F32)
              + jnp.dot(x_hi, w1b[ws, HALF:, :], preferred_element_type=F32))
        a3 = (jnp.dot(x_lo, w3b[ws, 0:HALF, :], preferred_element_type=F32)
              + jnp.dot(x_hi, w3b[ws, HALF:, :], preferred_element_type=F32))
        hbuf[blk % SCATTER_BUFS] = (a1 * jax.nn.sigmoid(a1) * a3).astype(BF16)

    def down(blk, ws):
        y = jnp.dot(hbuf[blk % SCATTER_BUFS], w2b[ws], preferred_element_type=F32)
        ybuf[blk % SCATTER_BUFS] = _pack_rows(y.astype(BF16).astype(F32))

    def for_each_weight_copy(gen, fn):
        ws = gen % 2
        expert = elist_ref[gen]
        for hbm, staging in ((w1_hbm, wst1), (w3_hbm, wst3), (w2_hbm, wst2)):
            fn(pltpu.make_async_copy(hbm.at[expert], staging.at[ws], sem_w.at[ws]))

    def switch_weights(gen):
        ws = gen % 2
        for_each_weight_copy(gen, lambda cp: cp.wait())
        w1b[ws] = wst1[ws].astype(BF16)
        w3b[ws] = wst3[ws].astype(BF16)
        w2b[ws] = wst2[ws].astype(BF16)

        @pl.when(gen + 1 < n_gen)
        def _():
            for_each_weight_copy(gen + 1, lambda cp: cp.start())

    def for_each_tail_piece(fn):
        def area_body(s, carry):
            used = used_ref[s]
            tail = AREA_CHUNKS - used
            row = (s * AREA_CHUNKS + used) * SUBLANES
            for bit in ZERO_BITS:
                take = (tail & bit) != 0

                @pl.when(take)
                def _(row=row, bit=bit):
                    fn(pltpu.make_async_copy(zbuf.at[pl.ds(0, bit * SUBLANES)],
                                             ys_hbm.at[pl.ds(pl.multiple_of(row, SUBLANES), bit * SUBLANES)],
                                             sem_zero))
                row = row + jnp.where(take, bit * SUBLANES, 0)
            return carry
        lax.fori_loop(0, n_areas, area_body, 0)

    @pl.when(n_gen > 0)
    def _():
        for_each_weight_copy(0, lambda cp: cp.start())

    zbuf[...] = jnp.zeros(zbuf.shape, U32)
    ybuf[...] = jnp.zeros(ybuf.shape, U32)
    dump = pltpu.make_async_copy(zbuf.at[pl.ds(0, DUMP_CHUNKS * SUBLANES)],
                                 ys_hbm.at[pl.ds(dump_row0, DUMP_CHUNKS * SUBLANES)], sem_dump)
    dump.start()
    dump.wait()
    for_each_tail_piece(lambda cp: cp.start())
    for sl in range(SCATTER_BUFS - SCATTER_LAG):
        for c in range(BLOCK_CHUNKS):
            out_copy(pl.ds(dump_row0 + (sl * BLOCK_CHUNKS + c) * SUBLANES, SUBLANES), sl, c).start()
    for k in range(lookahead):
        start_gather(k)

    @pl.when(total > 0)
    def _():
        switch_weights(0)
        wait_gather(0)
        rows = load_rows(0)
        issue_dmas(0)
        up_gate(0, rows, 0)

        def block_body(blk, carry):
            gen = bgen_ref[blk]
            gen_prev = bgen_ref[blk - 1]

            @pl.when(gen != gen_prev)
            def _():
                switch_weights(gen)

            wait_gather(blk)
            wait_scatter((blk - 1) % SCATTER_BUFS)
            rows = load_rows(blk)
            issue_dmas(blk)
            down(blk - 1, gen_prev % 2)
            up_gate(blk, rows, gen % 2)
            return carry

        lax.fori_loop(1, total, block_body, 0)
        wait_scatter((total - 1) % SCATTER_BUFS)
        down(total - 1, bgen_ref[total - 1] % 2)

    for k in range(SCATTER_LAG):
        start_scatter(total - SCATTER_LAG + k)
    for sl in range(SCATTER_BUFS):
        wait_scatter(sl)
    for k in range(lookahead):
        wait_gather(total + k)
    for_each_tail_piece(lambda cp: cp.wait())


def _moe_combine_kernel(h_ref, route_ref, ys_ref, l2g_ref, l2b_ref, y_ref):
    n = h_ref.shape[0]
    scope = n // (ys_ref.shape[0] // AREA_ROWS)
    s_iota = lax.broadcasted_iota(jnp.int32, (scope, AREA_ROWS), 1).astype(F32)
    for sc in range(n // scope):
        rows = slice(sc * scope, (sc + 1) * scope)
        route = route_ref[rows, :]
        slot1 = route[:, 0:1]
        slot2 = route[:, 1:2]
        gate1 = route[:, 2:3]
        gate2 = route[:, 3:4]
        gmat = (jnp.where(s_iota == slot1, gate1, 0.0) + jnp.where(s_iota == slot2, gate2, 0.0)).astype(BF16)
        y_lo, y_hi = _unpack_rows(ys_ref[sc * AREA_ROWS:(sc + 1) * AREA_ROWS, :])
        f = jnp.concatenate([jnp.dot(gmat, y_lo, preferred_element_type=F32),
                             jnp.dot(gmat, y_hi, preferred_element_type=F32)], axis=1)
        y_ref[rows, :] = _layer_norm(ALPHA * h_ref[rows, :] + f, l2g_ref[...], l2b_ref[...])


def _excl_cumsum(a, axis):
    return jnp.cumsum(a, axis=axis) - a


def _plan_blocks(cnt, q_max):
    n_areas = cnt.shape[0]
    c8 = (cnt + (SUBLANES - 1)) // SUBLANES
    used8 = jnp.sum(c8, axis=1)
    base8 = jnp.arange(n_areas, dtype=jnp.int32) * AREA_CHUNKS
    src_runs = (base8[:, None] + _excl_cumsum(c8, 1)).T
    len_runs = c8.T
    tot = jnp.sum(len_runs, axis=1)
    ptot = (tot + BLOCK_CHUNKS - 1) // BLOCK_CHUNKS * BLOCK_CHUNKS
    pstart = _excl_cumsum(ptot, 0)
    dst_runs = pstart[:, None] + _excl_cumsum(len_runs, 1)
    off = (src_runs - dst_runs).reshape(-1)
    diff = off - jnp.concatenate([jnp.zeros((1,), jnp.int32), off[:-1]])
    q = jnp.arange(q_max, dtype=jnp.int32)
    src = q + jnp.sum(jnp.where(dst_runs.reshape(1, -1) <= q[:, None], diff[None, :], 0), axis=1)
    valid = jnp.any((pstart[None, :] <= q[:, None]) & (q[:, None] < (pstart + tot)[None, :]), axis=1)
    zero_chunk = AREA_CHUNKS - 1
    ahead = jnp.full(((GATHER_BUFS - 1) * BLOCK_CHUNKS,), zero_chunk, jnp.int32)
    src_tab = jnp.concatenate([jnp.where(valid, src, zero_chunk).astype(jnp.int32), ahead])
    qv = jnp.arange(-SCATTER_LAG * BLOCK_CHUNKS, q_max, dtype=jnp.int32)
    dump = n_areas * AREA_CHUNKS + ((qv // BLOCK_CHUNKS) % SCATTER_BUFS) * BLOCK_CHUNKS + qv % BLOCK_CHUNKS
    lead = jnp.zeros((SCATTER_LAG * BLOCK_CHUNKS,), jnp.int32)
    dst_tab = jnp.where(jnp.concatenate([lead > 0, valid]), jnp.concatenate([lead, src]), dump).astype(jnp.int32)
    nblk = ptot // BLOCK_CHUNKS
    bstart = pstart // BLOCK_CHUNKS
    has = nblk > 0
    gen_of = jnp.cumsum(has.astype(jnp.int32)) - 1
    experts = jnp.arange(N_EXPERTS, dtype=jnp.int32)
    elist = jnp.sum(jnp.where(has[None, :] & (gen_of[None, :] == experts[:, None]), experts[None, :], 0), axis=1)
    blk = jnp.arange(q_max // BLOCK_CHUNKS, dtype=jnp.int32)
    owner = (bstart[None, :] <= blk[:, None]) & (blk[:, None] < (bstart + nblk)[None, :])
    bgen = jnp.sum(jnp.where(owner, gen_of[None, :], 0), axis=1)
    meta = jnp.stack([jnp.sum(nblk), jnp.sum(has.astype(jnp.int32))])
    return (src_tab, dst_tab, bgen.astype(jnp.int32), elist.astype(jnp.int32), meta.astype(jnp.int32),
            used8.astype(jnp.int32))


PREP_STEPS = 4


def _swap_head_halves(a):
    half = HEAD_DIM // 2
    pieces = []
    for g in range(a.shape[-1] // LANES):
        p = a[:, g * LANES:(g + 1) * LANES]
        lane = lax.broadcasted_iota(jnp.int32, p.shape, 1)
        pieces.append(jnp.where((lane & half) == 0, pltpu.roll(p, LANES - half, 1), pltpu.roll(p, half, 1)))
    return jnp.concatenate(pieces, axis=-1)


def _glu_blocks(a):
    lo = ATTN_DIM + 2 * KV_DIM
    pieces = []
    for cb in range(CONV_CH // LANES):
        pieces.append(a[:, lo + cb * LANES:lo + (cb + 1) * LANES])
        pieces.append(a[:, lo + CONV_CH + cb * LANES:lo + CONV_CH + (cb + 1) * LANES])
    return jnp.concatenate(pieces, axis=-1)


def _prep_weights_kernel(w_in_ref, b_in_ref, w_out_ref, w_in_b_ref, w_sw_ref, w_glu_ref, w_out_b_ref,
                         b_sw_ref, b_glu_ref):
    qk = ATTN_DIM + KV_DIM
    w = w_in_ref[...]
    w_in_b_ref[...] = w.astype(BF16)
    w_sw_ref[...] = _swap_head_halves(w[:, :qk]).astype(BF16)
    w_glu_ref[...] = _glu_blocks(w).astype(BF16)
    w_out_b_ref[...] = w_out_ref[...].astype(BF16)
    b = jnp.broadcast_to(b_in_ref[...], (SUBLANES, IN_DIM))
    b_sw_ref[...] = _swap_head_halves(b[:, :qk])[0:1]
    b_glu_ref[...] = _glu_blocks(b)[0:1]


def _rope_tables(pos):
    half = HEAD_DIM // 2
    inv = (ROPE_THETA ** (-np.arange(half, dtype=np.float64) / half)).astype(np.float32)
    ang = (np.asarray(pos, np.float32)[:, None] * inv).astype(np.float64)
    cos = np.cos(ang)
    sin = np.sin(ang)
    reps = LANES // HEAD_DIM
    cos_t = np.concatenate([cos, cos] * reps, axis=1)
    sin_t = np.concatenate([-sin, sin] * reps, axis=1)
    q_scale = ATTN_SCALE * LOG2E
    return jnp.asarray(np.concatenate([cos_t, sin_t, cos_t * q_scale, sin_t * q_scale], axis=1), F32)


def _full(shape):
    nd = len(shape)
    return pl.BlockSpec(shape, lambda *_: (0,) * nd)


def kernel(x_prompt, x_sample, cache_k, cache_v, state_conv, w_in, b_in, sinks, conv_w, conv_b, ln_conv_g,
           ln_conv_b, w_out, b_out, ln1_g, ln1_b, w_router_group, w_router_expert, w1, w3, w2, ln2_g, ln2_b):
    assert w_in.shape[0] == DEPTH
    bsz, seq, _ = x_prompt.shape
    dec_b, dec_t, _ = x_sample.shape
    assert dec_t == 1 and seq % TILE_Q == 0 and dec_b % SAMPLE_TB == 0 and 2 * dec_b + 7 * N_EXPERTS <= AREA_ROWS
    assert (bsz * seq) % COMBINE_TILE == 0
    n_prompt = bsz * seq
    tiles = seq // TILE_Q
    spt = TILE_Q // PROMPT_SCOPE
    p_areas = n_prompt // PROMPT_SCOPE
    n_areas = p_areas + spt
    q_max = (n_areas * AREA_CHUNKS + N_EXPERTS * (BLOCK_CHUNKS - 1) + BLOCK_CHUNKS - 1) // BLOCK_CHUNKS * BLOCK_CHUNKS

    row = lambda a: a.reshape(1, -1)
    cparams = lambda sem: pltpu.CompilerParams(dimension_semantics=sem, vmem_limit_bytes=VMEM_LIMIT)

    qk = ATTN_DIM + KV_DIM
    wrows = D_MODEL // PREP_STEPS
    wspec = lambda cols: pl.BlockSpec((wrows, cols), lambda t: (t, 0))
    w_in_b, w_sw_b, w_glu_b, w_out_b, b_sw, b_glu = pl.pallas_call(
        _prep_weights_kernel,
        grid=(PREP_STEPS,),
        in_specs=[wspec(IN_DIM), _full((1, IN_DIM)), wspec(D_MODEL)],
        out_specs=[wspec(IN_DIM), wspec(qk), wspec(2 * CONV_CH), wspec(D_MODEL),
                   _full((1, qk)), _full((1, 2 * CONV_CH))],
        out_shape=[jax.ShapeDtypeStruct((D_MODEL, IN_DIM), BF16), jax.ShapeDtypeStruct((D_MODEL, qk), BF16),
                   jax.ShapeDtypeStruct((D_MODEL, 2 * CONV_CH), BF16), jax.ShapeDtypeStruct((D_MODEL, D_MODEL), BF16),
                   jax.ShapeDtypeStruct((1, qk), F32), jax.ShapeDtypeStruct((1, 2 * CONV_CH), F32)],
        compiler_params=cparams(("arbitrary",)),
        name="prep_weights",
    )(w_in[0], row(b_in[0]), w_out[0])
    w_r = jnp.concatenate([w_router_group[0], w_router_expert[0].reshape(D_MODEL, N_EXPERTS),
                           jnp.zeros((D_MODEL, LANES - ELANE0 - N_EXPERTS), F32)], axis=1).astype(BF16)
    shared = (w_in_b, row(b_in[0]), w_sw_b, b_sw, conv_w[0], row(conv_b[0]),
              row(ln_conv_g[0]), row(ln_conv_b[0]), w_out_b, row(b_out[0]), row(ln1_g[0]), row(ln1_b[0]), w_r)
    shared_specs = [_full(a.shape) for a in shared]
    glu = (w_glu_b, b_glu)
    glu_specs = [_full(a.shape) for a in glu]
    assert len(glu) + len(shared) == N_PROMPT_WEIGHTS
    smem = pl.BlockSpec(memory_space=pltpu.SMEM)
    rope_p = _rope_tables(np.arange(seq))
    rope_s = _rope_tables(PAST_LEN + np.arange(dec_t))

    ck = cache_k[0].reshape(dec_b, WINDOW, KV_DIM)
    cv = cache_v[0].reshape(dec_b, WINDOW, KV_DIM)
    tb = SAMPLE_TB
    conv_state_spec = pl.BlockSpec((None, tb, CONV_WIDTH - 1, CONV_CH), lambda t: (0, t, 0, 0))
    h_s, nk_s, nv_s, nc_s, xs_s, route_s, cnt_s = pl.pallas_call(
        _mix_sample_kernel,
        grid=(dec_b // tb,),
        in_specs=[smem,
                  pl.BlockSpec((tb, None, D_MODEL), lambda t: (t, 0, 0)),
                  _full(rope_s.shape),
                  pl.BlockSpec((tb, WINDOW, KV_DIM), lambda t: (t, 0, 0)),
                  pl.BlockSpec((tb, WINDOW, KV_DIM), lambda t: (t, 0, 0)),
                  conv_state_spec] + shared_specs,
        out_specs=[pl.BlockSpec((tb, D_MODEL), lambda t: (t, 0)),
                   pl.BlockSpec((tb, WINDOW, KV_DIM), lambda t: (t, 0, 0)),
                   pl.BlockSpec((tb, WINDOW, KV_DIM), lambda t: (t, 0, 0)),
                   conv_state_spec,
                   _full((AREA_ROWS, HALF)),
                   _full((dec_b, LANES)),
                   _full((1, SUBLANES, LANES))],
        out_shape=[jax.ShapeDtypeStruct((dec_b, D_MODEL), F32),
                   jax.ShapeDtypeStruct((dec_b, WINDOW, KV_DIM), F32),
                   jax.ShapeDtypeStruct((dec_b, WINDOW, KV_DIM), F32),
                   jax.ShapeDtypeStruct((DEPTH, dec_b, CONV_WIDTH - 1, CONV_CH), F32),
                   jax.ShapeDtypeStruct((AREA_ROWS, HALF), U32),
                   jax.ShapeDtypeStruct((dec_b, LANES), F32),
                   jax.ShapeDtypeStruct((1, SUBLANES, LANES), F32)],
        scratch_shapes=[pltpu.VMEM((dec_b, D_MODEL), F32)],
        compiler_params=cparams(("arbitrary",)),
        name="mix_sample",
    )(sinks[0], x_sample, rope_s, ck, cv, state_conv, *shared)

    n_real = bsz * tiles
    real = lambda t: jnp.minimum(t, n_real - 1)
    h_p, xs, route_p, cnt_p, nk_p, nv_p, nc_p = pl.pallas_call(
        functools.partial(_mix_prompt_kernel, tiles=tiles),
        grid=(n_real + 1,),
        in_specs=[smem,
                  pl.BlockSpec((1, TILE_Q, D_MODEL), lambda t: (real(t) // tiles, real(t) % tiles, 0)),
                  pl.BlockSpec((TILE_Q, 4 * LANES), lambda t: (real(t) % tiles, 0)),
                  _full((AREA_ROWS, HALF))] + glu_specs + shared_specs,
        out_specs=[pl.BlockSpec((1, TILE_Q, D_MODEL), lambda t: (real(t) // tiles, real(t) % tiles, 0)),
                   pl.BlockSpec((spt * AREA_ROWS, HALF), lambda t: (t, 0)),
                   pl.BlockSpec((TILE_Q, LANES), lambda t: (real(t), 0)),
                   pl.BlockSpec((spt, SUBLANES, LANES), lambda t: (t, 0, 0)),
                   pl.BlockSpec((1, WINDOW, KV_DIM), lambda t: (real(t) // tiles, 0, 0)),
                   pl.BlockSpec((1, WINDOW, KV_DIM), lambda t: (real(t) // tiles, 0, 0)),
                   pl.BlockSpec((1, CONV_WIDTH - 1, CONV_CH), lambda t: (real(t) // tiles, 0, 0))],
        out_shape=[jax.ShapeDtypeStruct((bsz, seq, D_MODEL), F32),
                   jax.ShapeDtypeStruct((n_areas * AREA_ROWS, HALF), U32),
                   jax.ShapeDtypeStruct((n_prompt, LANES), F32),
                   jax.ShapeDtypeStruct((n_areas, SUBLANES, LANES), F32),
                   jax.ShapeDtypeStruct((bsz, WINDOW, KV_DIM), F32),
                   jax.ShapeDtypeStruct((bsz, WINDOW, KV_DIM), F32),
                   jax.ShapeDtypeStruct((bsz, CONV_WIDTH - 1, CONV_CH), F32)],
        scratch_shapes=[pltpu.VMEM((WINDOW, KV_DIM), F32), pltpu.VMEM((WINDOW, KV_DIM), F32),
                        pltpu.VMEM((TILE_Q + CONV_PRE, CONV_CH), F32),
                        pltpu.VMEM((SUBLANES - 1, TILE_Q + CONV_SHIFT_EXTRA, CONV_CH), F32)],
        compiler_params=cparams(("arbitrary",)),
        name="mix_prompt",
    )(sinks[0], x_prompt, rope_p, xs_s, *glu, *shared)

    cnt = jnp.concatenate([cnt_p[:p_areas, 0], cnt_s[:, 0], cnt_p[p_areas + 1:, 0]], axis=0)
    cnt = cnt[:, ELANE0:ELANE0 + N_EXPERTS].astype(jnp.int32)
    plan = _plan_blocks(cnt, q_max)
    hbm = pl.BlockSpec(memory_space=pl.ANY)
    up_shape, down_shape = (2, D_MODEL, D_FF_EXPERT), (2, D_FF_EXPERT, D_MODEL)
    ys = pl.pallas_call(
        _moe_ffn_kernel,
        grid_spec=pltpu.PrefetchScalarGridSpec(
            num_scalar_prefetch=len(plan),
            grid=(1,),
            in_specs=[hbm, hbm, hbm, hbm],
            out_specs=hbm,
            scratch_shapes=[pltpu.VMEM((GATHER_BUFS, BLOCK_ROWS, HALF), U32),
                            pltpu.VMEM((SCATTER_BUFS, BLOCK_ROWS, HALF), U32),
                            pltpu.VMEM((SCATTER_BUFS, BLOCK_ROWS, D_FF_EXPERT), BF16),
                            pltpu.VMEM((ZERO_ROWS, HALF), U32),
                            pltpu.VMEM(up_shape, F32), pltpu.VMEM(up_shape, F32), pltpu.VMEM(down_shape, F32),
                            pltpu.VMEM(up_shape, BF16), pltpu.VMEM(up_shape, BF16), pltpu.VMEM(down_shape, BF16),
                            pltpu.SemaphoreType.DMA((GATHER_BUFS,)), pltpu.SemaphoreType.DMA((SCATTER_BUFS,)),
                            pltpu.SemaphoreType.DMA(()), pltpu.SemaphoreType.DMA(()),
                            pltpu.SemaphoreType.DMA((2,))]),
        out_shape=jax.ShapeDtypeStruct((n_areas * AREA_ROWS + DUMP_CHUNKS * SUBLANES, HALF), U32),
        compiler_params=cparams(("arbitrary",)),
        name="moe_ffn",
    )(*plan, xs, w1[0], w3[0], w2[0])

    def combine(h2d, route, first_area, scope, tile):
        n = h2d.shape[0]
        apt = tile // scope
        return pl.pallas_call(
            _moe_combine_kernel,
            grid=(n // tile,),
            in_specs=[pl.BlockSpec((tile, D_MODEL), lambda t: (t, 0)),
                      pl.BlockSpec((tile, LANES), lambda t: (t, 0)),
                      pl.BlockSpec((apt * AREA_ROWS, HALF), lambda t: (first_area // apt + t, 0)),
                      _full((1, D_MODEL)), _full((1, D_MODEL))],
            out_specs=pl.BlockSpec((tile, D_MODEL), lambda t: (t, 0)),
            out_shape=jax.ShapeDtypeStruct((n, D_MODEL), F32),
            compiler_params=cparams(("arbitrary",)),
            name="moe_combine",
        )(h2d, route, ys, row(ln2_g[0]), row(ln2_b[0]))

    y_p = combine(h_p.reshape(n_prompt, D_MODEL), route_p, 0, PROMPT_SCOPE, COMBINE_TILE)
    y_s = combine(h_s, route_s, p_areas, dec_b, dec_b)

    kv_shape = lambda n: (DEPTH, n, WINDOW, N_KV_HEADS, HEAD_DIM)
    return (y_p.reshape(bsz, seq, D_MODEL), y_s.reshape(dec_b, dec_t, D_MODEL),
            nk_p.reshape(kv_shape(bsz)), nv_p.reshape(kv_shape(bsz)), nc_p[None],
            nk_s.reshape(kv_shape(dec_b)), nv_s.reshape(kv_shape(dec_b)), nc_s)
```

```python
import functools

import jax
import jax.numpy as jnp
import numpy as np
from jax import lax
from jax.experimental import pallas as pl
from jax.experimental.pallas import tpu as pltpu

F32 = jnp.float32
BF16 = jnp.bfloat16
U32 = jnp.uint32

D_MODEL = 1024
N_HEADS = 8
N_KV_HEADS = 2
HEAD_DIM = 64
Q_GROUP = N_HEADS // N_KV_HEADS
ATTN_DIM = N_HEADS * HEAD_DIM
KV_DIM = N_KV_HEADS * HEAD_DIM
WINDOW = 128
ROPE_THETA = 10000.0
ATTN_SCALE = HEAD_DIM ** -0.5
CONV_CH = D_MODEL - ATTN_DIM
CONV_WIDTH = 31
IN_DIM = ATTN_DIM + 2 * KV_DIM + 2 * CONV_CH
N_GROUPS = 4
EXPERTS_PER_GROUP = 8
N_EXPERTS = N_GROUPS * EXPERTS_PER_GROUP
D_FF_EXPERT = 256
LN_EPS = 1e-5
NEG_INF = -1e30
DEPTH = 1
ALPHA = (2.0 * DEPTH) ** 0.25
PAST_LEN = 16384
LOG2E = 1.4426950408889634

LANES = 128
SUBLANES = 8
HALF = D_MODEL // 2
ELANE0 = N_GROUPS
PROMPT_SCOPE = 256
AREA_ROWS = -(-(2 * PROMPT_SCOPE + (SUBLANES - 1) * N_EXPERTS) // LANES) * LANES
AREA_CHUNKS = AREA_ROWS // SUBLANES
BLOCK_CHUNKS = 32
BLOCK_ROWS = BLOCK_CHUNKS * SUBLANES
TILE_Q = 512
COMBINE_TILE = 1024
CONV_PRE = 32
CONV_SHIFT_EXTRA = 24
SAMPLE_TB = 32
SAMPLE_GRP = 8
VMEM_LIMIT = 56 * 1024 * 1024


def _layer_norm(x, g, b):
    mu = jnp.mean(x, axis=-1, keepdims=True)
    xc = x - mu
    var = jnp.mean(xc * xc, axis=-1, keepdims=True)
    return xc * lax.rsqrt(var + LN_EPS) * g + b


def _rope(x, x_swapped, cos, sin_signed):
    pieces = [x[:, g * LANES:(g + 1) * LANES] * cos + x_swapped[:, g * LANES:(g + 1) * LANES] * sin_signed
              for g in range(x.shape[-1] // LANES)]
    return pieces[0] if len(pieces) == 1 else jnp.concatenate(pieces, axis=-1)


def _in_proj_glu(xb, w_in_ref, b_in_ref):
    lo, hi = ATTN_DIM + 2 * KV_DIM, IN_DIM
    cacg = jnp.dot(xb, w_in_ref[:, lo:hi], preferred_element_type=F32) + b_in_ref[:, lo:hi]
    return cacg[:, :CONV_CH] * jax.nn.sigmoid(cacg[:, CONV_CH:])


def _in_proj_qkv(xb, w_in_ref, b_in_ref, w_sw_ref, b_sw_ref, rope):
    o_v, o_e = ATTN_DIM + KV_DIM, ATTN_DIM + 2 * KV_DIM
    qkv = jnp.dot(xb, w_in_ref[:, 0:o_e], preferred_element_type=F32) + b_in_ref[:, 0:o_e]
    sw = jnp.dot(xb, w_sw_ref[...], preferred_element_type=F32) + b_sw_ref[...]
    k = _rope(qkv[:, ATTN_DIM:o_v], sw[:, ATTN_DIM:], rope[:, 0:LANES], rope[:, LANES:2 * LANES])
    q = _rope(qkv[:, :ATTN_DIM], sw[:, :ATTN_DIM], rope[:, 2 * LANES:3 * LANES], rope[:, 3 * LANES:])
    return q, k, qkv[:, o_v:]


def _pack_rows(v):
    lo = lax.bitcast_convert_type(v[:, :HALF], U32)
    hi = lax.bitcast_convert_type(v[:, HALF:], U32)
    return (lo >> 16) | (hi & jnp.uint32(0xFFFF0000))


def _unpack_rows(w):
    lo = lax.bitcast_convert_type(w << 16, F32).astype(BF16)
    hi = lax.bitcast_convert_type(w & jnp.uint32(0xFFFF0000), F32).astype(BF16)
    return lo, hi


def _drain(stages):
    try:
        while True:
            next(stages)
    except StopIteration as stop:
        return stop.value


def _round_robin(*stage_generators):
    live = list(stage_generators)
    while live:
        for stages in list(live):
            try:
                next(stages)
            except StopIteration:
                live.remove(stages)


def _route_and_sort_stages(h, w_r):
    n = h.shape[0]
    hb = h.astype(BF16)
    logits = jnp.dot(hb, w_r, preferred_element_type=F32)
    yield
    lane = lax.broadcasted_iota(jnp.int32, (n, LANES), 1)
    lane_f = lane.astype(F32)
    is_g = lane < N_GROUPS
    lg = jnp.where(is_g, logits, -jnp.inf)
    gmax = jnp.max(lg, axis=-1, keepdims=True)
    yield
    gidx = jnp.min(jnp.where(lg == gmax, lane_f, float(LANES)), axis=-1, keepdims=True)
    p_g = 1.0 / jnp.sum(jnp.where(is_g, jnp.exp(logits - gmax), 0.0), axis=-1, keepdims=True)
    yield
    egrp = ((lane - ELANE0) >> (EXPERTS_PER_GROUP.bit_length() - 1)).astype(F32)
    emask = (lane >= ELANE0) & (lane < ELANE0 + N_EXPERTS) & (egrp == gidx)
    sel = jnp.where(emask, logits, -jnp.inf)
    v1 = jnp.max(sel, axis=-1, keepdims=True)
    yield
    i1 = jnp.min(jnp.where(sel == v1, lane_f, float(LANES)), axis=-1, keepdims=True)
    yield
    sel2 = jnp.where(lane_f == i1, -jnp.inf, sel)
    v2 = jnp.max(sel2, axis=-1, keepdims=True)
    yield
    i2 = jnp.min(jnp.where(sel2 == v2, lane_f, float(LANES)), axis=-1, keepdims=True)
    t = jnp.exp(v2 - v1)
    gate1 = p_g / (1.0 + t)
    gate2 = p_g * t / (1.0 + t)
    pick1 = lane_f == i1
    pick2 = lane_f == i2
    member = jnp.where(pick1 | pick2, 1.0, 0.0)
    yield

    r_i = lax.broadcasted_iota(jnp.int32, (n, n), 0)
    c_i = lax.broadcasted_iota(jnp.int32, (n, n), 1)
    lower = jnp.where(c_i < r_i, 1.0, 0.0).astype(BF16)
    e_r = lax.broadcasted_iota(jnp.int32, (LANES, LANES), 0)
    e_c = lax.broadcasted_iota(jnp.int32, (LANES, LANES), 1)
    upper = jnp.where(e_r < e_c, 1.0, 0.0).astype(BF16)
    before = jnp.dot(lower, member.astype(BF16), preferred_element_type=F32)
    cnt = jnp.sum(member, axis=0, keepdims=True)
    c8 = jnp.floor((cnt + (SUBLANES - 1)) * (1.0 / SUBLANES))
    c8b = jnp.broadcast_to(c8, (SUBLANES, LANES)).astype(BF16)
    off8 = jnp.dot(c8b, upper, preferred_element_type=F32)[0:1]
    yield
    slot_all = before + off8 * float(SUBLANES)
    slot1 = jnp.sum(jnp.where(pick1, slot_all, 0.0), axis=-1, keepdims=True)
    slot2 = jnp.sum(jnp.where(pick2, slot_all, 0.0), axis=-1, keepdims=True)
    yield
    s_iota = lax.broadcasted_iota(jnp.int32, (n, AREA_ROWS), 1).astype(F32)
    perm_t = jnp.where((s_iota == slot1) | (s_iota == slot2), 1.0, 0.0).astype(BF16)
    sorted_rows = lax.dot_general(perm_t, hb, (((0,), (0,)), ((), ())),
                                  preferred_element_type=F32)
    yield
    route = jnp.where(lane == 0, slot1, jnp.where(lane == 1, slot2, jnp.where(lane == 2, gate1,
                      jnp.where(lane == 3, gate2, 0.0))))
    return route, _pack_rows(sorted_rows), cnt


def _softmax_rows_with_sink(s, sink, extra=None):
    m = jnp.maximum(jnp.max(s, axis=-1, keepdims=True), sink)
    if extra is not None:
        m = jnp.maximum(m, extra)
    e = jnp.exp2(s - m)
    den = jnp.sum(e, axis=-1, keepdims=True) + jnp.exp2(sink - m)
    if extra is not None:
        ee = jnp.exp2(extra - m)
        den = den + ee
        return e / den, ee / den
    return e / den, None


def _mix_out_stages(x, attn, conv_pre, lcg, lcb, w_out_ref, b_out, l1g, l1b):
    conv = _layer_norm(conv_pre, lcg, lcb)
    conv = conv * jax.nn.sigmoid(conv)
    yield
    mix = (jnp.dot(attn.astype(BF16), w_out_ref[0:ATTN_DIM, :], preferred_element_type=F32)
           + jnp.dot(conv.astype(BF16), w_out_ref[ATTN_DIM:, :], preferred_element_type=F32) + b_out)
    yield
    h = _layer_norm(ALPHA * x + mix, l1g, l1b)
    yield
    return h


N_PROMPT_WEIGHTS = 15


def _mix_prompt_kernel(sinks_ref, x_ref, rope_ref, xs_s_ref, *rest, tiles):
    _, xs_ref, _, cnt_ref = rest[N_PROMPT_WEIGHTS:N_PROMPT_WEIGHTS + 4]
    step = pl.program_id(0)
    n_real = pl.num_programs(0) - 1

    @pl.when(step < n_real)
    def _():
        _mix_prompt_tile(sinks_ref, x_ref, rope_ref, *rest, i=step % tiles)

    @pl.when(step == n_real)
    def _():
        xs_ref[0:AREA_ROWS, :] = xs_s_ref[...]
        xs_ref[AREA_ROWS:, :] = jnp.zeros((xs_ref.shape[0] - AREA_ROWS, HALF), U32)
        cnt_ref[...] = jnp.zeros(cnt_ref.shape, F32)


def _mix_prompt_tile(sinks_ref, x_ref, rope_ref, w_glu_ref, b_glu_ref,
                     w_in_ref, b_in_ref, w_sw_ref, b_sw_ref, conv_w_ref, conv_b_ref,
                     lcg_ref, lcb_ref, w_out_ref, b_out_ref, l1g_ref, l1b_ref, w_r_ref,
                     h_ref, xs_ref, route_ref, cnt_ref, nk_ref, nv_ref, nc_ref,
                     kprev, vprev, u_scr, u_shift, *, i):
    tq = x_ref.shape[1]
    pre = CONV_PRE
    n_sc = tq // PROMPT_SCOPE

    @pl.when(i == 0)
    def _():
        kprev[...] = jnp.zeros_like(kprev)
        vprev[...] = jnp.zeros_like(vprev)
        u_scr[0:pre, :] = jnp.zeros((pre, CONV_CH), F32)

    x = x_ref[0]
    xb = x.astype(BF16)

    def glu_block(cb):
        cols = slice(2 * LANES * cb, 2 * LANES * (cb + 1))
        vg = jnp.dot(xb, w_glu_ref[:, cols], preferred_element_type=F32) + b_glu_ref[:, cols]
        return vg[:, :LANES] * jax.nn.sigmoid(vg[:, LANES:])

    n_cb = CONV_CH // LANES
    acc_blocks = [[None] * n_cb for _ in range(n_sc)]
    u_next = glu_block(0)
    for cb in range(n_cb):
        lanes = slice(LANES * cb, LANES * (cb + 1))
        u_scr[pre:pre + tq, lanes] = u_next
        if cb + 1 < n_cb:
            u_next = glu_block(cb + 1)
        for r in range(1, SUBLANES):
            u_shift[r - 1, :, lanes] = u_scr[r:r + tq + CONV_SHIFT_EXTRA, lanes]
        for sc in range(n_sc):
            base = sc * PROMPT_SCOPE
            acc = jnp.zeros((PROMPT_SCOPE, LANES), F32)
            for j in range(CONV_WIDTH):
                a, r = divmod(pre - (CONV_WIDTH - 1) + j, SUBLANES)
                start = base + a * SUBLANES
                src = (u_scr[start:start + PROMPT_SCOPE, lanes] if r == 0
                       else u_shift[r - 1, start:start + PROMPT_SCOPE, lanes])
                acc = acc + src * conv_w_ref[j:j + 1, lanes]
            acc_blocks[sc][cb] = acc
    conv_pre = [jnp.concatenate(blocks, axis=1) + conv_b_ref[...] for blocks in acc_blocks]

    nc_ref[0] = u_scr[pre + tq - (CONV_WIDTH - 1):pre + tq, :]
    tail = u_scr[tq:tq + pre, :]
    u_scr[0:pre, :] = tail

    q, k, v = _in_proj_qkv(xb, w_in_ref, b_in_ref, w_sw_ref, b_sw_ref, rope_ref[...])
    qb = q.astype(BF16)
    kb = jnp.concatenate([kprev[...], k], axis=0).astype(BF16)
    vb = jnp.concatenate([vprev[...], v], axis=0).astype(BF16)
    row = lax.broadcasted_iota(jnp.int32, (WINDOW, 2 * WINDOW), 0)
    col = lax.broadcasted_iota(jnp.int32, (WINDOW, 2 * WINDOW), 1)
    dist = row + WINDOW - col
    band = (dist >= 0) & (dist < WINDOW)
    first_col = jnp.where(i > 0, 0, WINDOW)
    kprev[...] = k[tq - WINDOW:, :]
    vprev[...] = v[tq - WINDOW:, :]
    nk_ref[0] = k[tq - WINDOW:, :]
    nv_ref[0] = v[tq - WINDOW:, :]
    blocks_per_scope = PROMPT_SCOPE // WINDOW
    attn_blocks = [None] * (tq // WINDOW)

    def scope_attention(sc):
        for j in range(sc * blocks_per_scope, (sc + 1) * blocks_per_scope):
            msk = (band & (col >= first_col)) if j == 0 else band
            heads = [None] * N_HEADS
            for c in range(N_KV_HEADS):
                kc = kb[j * WINDOW:(j + 2) * WINDOW, c * HEAD_DIM:(c + 1) * HEAD_DIM]
                vc = vb[j * WINDOW:(j + 2) * WINDOW, c * HEAD_DIM:(c + 1) * HEAD_DIM]
                qc = jnp.concatenate(
                    [qb[j * WINDOW:(j + 1) * WINDOW, (c * Q_GROUP + g) * HEAD_DIM:(c * Q_GROUP + g + 1) * HEAD_DIM]
                     for g in range(Q_GROUP)], axis=0)
                s = lax.dot_general(qc, kc, (((1,), (1,)), ((), ())), preferred_element_type=F32)
                yield
                probs = []
                for g in range(Q_GROUP):
                    sg = jnp.where(msk, s[g * WINDOW:(g + 1) * WINDOW], NEG_INF)
                    p, _ = _softmax_rows_with_sink(sg, sinks_ref[c * Q_GROUP + g] * LOG2E)
                    probs.append(p.astype(BF16))
                    if g % 2 == 1:
                        yield
                o = jnp.dot(jnp.concatenate(probs, axis=0), vc, preferred_element_type=F32)
                for g in range(Q_GROUP):
                    heads[c * Q_GROUP + g] = o[g * WINDOW:(g + 1) * WINDOW]
                yield
            attn_blocks[j] = jnp.concatenate(heads, axis=1)

    def scope_tail(sc):
        rows = slice(sc * PROMPT_SCOPE, (sc + 1) * PROMPT_SCOPE)
        attn = jnp.concatenate(attn_blocks[sc * blocks_per_scope:(sc + 1) * blocks_per_scope], axis=0)
        h = yield from _mix_out_stages(x[rows], attn, conv_pre[sc], lcg_ref[...], lcb_ref[...],
                                       w_out_ref, b_out_ref[...], l1g_ref[...], l1b_ref[...])
        route, area, cnt = yield from _route_and_sort_stages(h, w_r_ref[...])
        h_ref[0, rows, :] = h
        route_ref[rows, :] = route
        xs_ref[sc * AREA_ROWS:(sc + 1) * AREA_ROWS, :] = area
        cnt_ref[sc] = jnp.broadcast_to(cnt, (SUBLANES, LANES))

    _round_robin(*[scope_attention(sc) for sc in range(n_sc)])
    _round_robin(*[scope_tail(sc) for sc in range(n_sc)])


def _mix_sample_kernel(sinks_ref, x_ref, rope_ref, ck_ref, cv_ref, st_ref, w_in_ref, b_in_ref, w_sw_ref, b_sw_ref,
                       conv_w_ref, conv_b_ref, lcg_ref, lcb_ref, w_out_ref, b_out_ref, l1g_ref, l1b_ref,
                       w_r_ref,
                       h_ref, nk_ref, nv_ref, nc_ref, xs_ref, route_ref, cnt_ref,
                       h_scr):
    tb = x_ref.shape[0]
    n_tok = h_scr.shape[0]
    step = pl.program_id(0)
    x = x_ref[...]
    xb = x.astype(BF16)
    u = _in_proj_glu(xb, w_in_ref, b_in_ref)
    q, k_new, v_new = _in_proj_qkv(xb, w_in_ref, b_in_ref, w_sw_ref, b_sw_ref, rope_ref[...])

    qb = q.astype(BF16)
    knb = k_new.astype(BF16).astype(F32)
    vnb = v_new.astype(BF16).astype(F32)
    gq = SAMPLE_GRP * Q_GROUP
    nkeys = SAMPLE_GRP * WINDOW
    r_i = lax.broadcasted_iota(jnp.int32, (gq, nkeys), 0)
    c_i = lax.broadcasted_iota(jnp.int32, (gq, nkeys), 1)
    window_shift, grp_shift = WINDOW.bit_length() - 1, SAMPLE_GRP.bit_length() - 1
    msk = ((c_i >> window_shift) == (r_i & (SAMPLE_GRP - 1))) & ((c_i & (WINDOW - 1)) >= 1)
    r_col = lax.broadcasted_iota(jnp.int32, (gq, 1), 0) >> grp_shift
    attn_groups = [None] * (tb // SAMPLE_GRP)

    def group_attention(grp):
        r0 = grp * SAMPLE_GRP
        kc_all = ck_ref[r0:r0 + SAMPLE_GRP].reshape(nkeys, KV_DIM).astype(BF16)
        vc_all = cv_ref[r0:r0 + SAMPLE_GRP].reshape(nkeys, KV_DIM).astype(BF16)
        yield
        heads = [None] * N_HEADS
        for c in range(N_KV_HEADS):
            lanes = slice(c * HEAD_DIM, (c + 1) * HEAD_DIM)
            qc = jnp.concatenate(
                [qb[r0:r0 + SAMPLE_GRP, (c * Q_GROUP + g) * HEAD_DIM:(c * Q_GROUP + g + 1) * HEAD_DIM]
                 for g in range(Q_GROUP)], axis=0)
            s = lax.dot_general(qc, kc_all[:, lanes], (((1,), (1,)), ((), ())), preferred_element_type=F32)
            s = jnp.where(msk, s, NEG_INF)
            kn = jnp.concatenate([knb[r0:r0 + SAMPLE_GRP, lanes]] * Q_GROUP, axis=0)
            vn = jnp.concatenate([vnb[r0:r0 + SAMPLE_GRP, lanes]] * Q_GROUP, axis=0)
            s_new = jnp.sum(qc.astype(F32) * kn, axis=-1, keepdims=True)
            sink = jnp.zeros((gq, 1), F32)
            for g in range(Q_GROUP):
                sink = jnp.where(r_col == g, sinks_ref[c * Q_GROUP + g] * LOG2E, sink)
            yield
            p, p_new = _softmax_rows_with_sink(s, sink, s_new)
            yield
            o = (jnp.dot(p.astype(BF16), vc_all[:, lanes], preferred_element_type=F32)
                 + p_new.astype(BF16).astype(F32) * vn)
            for g in range(Q_GROUP):
                heads[c * Q_GROUP + g] = o[g * SAMPLE_GRP:(g + 1) * SAMPLE_GRP]
            yield
        attn_groups[grp] = jnp.concatenate(heads, axis=1)

    _round_robin(*[group_attention(grp) for grp in range(len(attn_groups))])
    attn = jnp.concatenate(attn_groups, axis=0)

    st = st_ref[...]
    acc = jnp.sum(st * conv_w_ref[0:CONV_WIDTH - 1, :][None], axis=1)
    conv_pre = acc + u * conv_w_ref[CONV_WIDTH - 1:CONV_WIDTH, :] + conv_b_ref[...]

    nk_ref[:, 0:WINDOW - 1, :] = ck_ref[:, 1:WINDOW, :]
    nv_ref[:, 0:WINDOW - 1, :] = cv_ref[:, 1:WINDOW, :]
    nc_ref[:, 0:CONV_WIDTH - 2, :] = st_ref[:, 1:CONV_WIDTH - 1, :]
    for b in range(tb):
        nk_ref[b, WINDOW - 1:WINDOW, :] = k_new[b:b + 1, :]
        nv_ref[b, WINDOW - 1:WINDOW, :] = v_new[b:b + 1, :]
        nc_ref[b, CONV_WIDTH - 2:CONV_WIDTH - 1, :] = u[b:b + 1, :]

    h = _drain(_mix_out_stages(x, attn, conv_pre, lcg_ref[...], lcb_ref[...], w_out_ref, b_out_ref[...],
                               l1g_ref[...], l1b_ref[...]))
    h_ref[...] = h
    h_scr[pl.ds(pl.multiple_of(step * tb, tb), tb), :] = h

    @pl.when(step == pl.num_programs(0) - 1)
    def _():
        route, area, cnt = _drain(_route_and_sort_stages(h_scr[...], w_r_ref[...]))
        route_ref[...] = route
        xs_ref[...] = area
        cnt_ref[0] = jnp.broadcast_to(cnt, (SUBLANES, LANES))


GATHER_BUFS = 8
SCATTER_BUFS = 8
SCATTER_LAG = 2
ZERO_BITS = (64, 32, 16, 8, 4, 2, 1)
DUMP_CHUNKS = SCATTER_BUFS * BLOCK_CHUNKS
ZERO_ROWS = max(ZERO_BITS[0], DUMP_CHUNKS) * SUBLANES


def _moe_ffn_kernel(src_ref, dst_ref, bgen_ref, elist_ref, meta_ref, used_ref,
                    xs_hbm, w1_hbm, w3_hbm, w2_hbm,
                    ys_hbm,
                    xbuf, ybuf, hbuf, zbuf, wst1, wst3, wst2, w1b, w3b, w2b,
                    sem_in, sem_out, sem_zero, sem_dump, sem_w):
    n_areas = used_ref.shape[0]
    total = meta_ref[0]
    n_gen = meta_ref[1]
    dump_row0 = n_areas * AREA_ROWS
    lookahead = GATHER_BUFS - 1
    any_rows = pl.ds(0, SUBLANES)

    def chunk_at(ref, idx):
        return pl.ds(pl.multiple_of(ref[idx] * SUBLANES, SUBLANES), SUBLANES)

    def in_copy(rows, sl, c):
        return pltpu.make_async_copy(xs_hbm.at[rows], xbuf.at[sl, pl.ds(c * SUBLANES, SUBLANES)], sem_in.at[sl])

    def out_copy(rows, sl, c):
        return pltpu.make_async_copy(ybuf.at[sl, pl.ds(c * SUBLANES, SUBLANES)], ys_hbm.at[rows], sem_out.at[sl])

    def start_gather(blk):
        for c in range(BLOCK_CHUNKS):
            in_copy(chunk_at(src_ref, blk * BLOCK_CHUNKS + c), blk % GATHER_BUFS, c).start()

    def wait_gather(blk):
        for c in range(BLOCK_CHUNKS):
            in_copy(any_rows, blk % GATHER_BUFS, c).wait()

    def start_scatter(blk):
        for c in range(BLOCK_CHUNKS):
            out_copy(chunk_at(dst_ref, (blk + SCATTER_LAG) * BLOCK_CHUNKS + c), blk % SCATTER_BUFS, c).start()

    def wait_scatter(sl):
        for c in range(BLOCK_CHUNKS):
            out_copy(any_rows, sl, c).wait()

    def load_rows(blk):
        return _unpack_rows(xbuf[blk % GATHER_BUFS])

    def issue_dmas(blk):
        start_scatter(blk - SCATTER_LAG)
        start_gather(blk + lookahead)

    def up_gate(blk, rows, ws):
        x_lo, x_hi = rows
        a1 = (jnp.dot(x_lo, w1b[ws, 0:HALF, :], preferred_element_type=F32)
              + jnp.dot(x_hi, w1b[ws, HALF:, :], preferred_element_type=F32))
        a3 = (jnp.dot(x_lo, w3b[ws, 0:HALF, :], preferred_element_type=F32)
              + jnp.dot(x_hi, w3b[ws, HALF:, :], preferred_element_type=F32))
        hbuf[blk % SCATTER_BUFS] = (a1 * jax.nn.sigmoid(a1) * a3).astype(BF16)

    def down(blk, ws):
        y = jnp.dot(hbuf[blk % SCATTER_BUFS], w2b[ws], preferred_element_type=F32)
        ybuf[blk % SCATTER_BUFS] = _pack_rows(y.astype(BF16).astype(F32))

    def for_each_weight_copy(gen, fn):
        ws = gen % 2
        expert = elist_ref[gen]
        for hbm, staging in ((w1_hbm, wst1), (w3_hbm, wst3), (w2_hbm, wst2)):
            fn(pltpu.make_async_copy(hbm.at[expert], staging.at[ws], sem_w.at[ws]))

    def switch_weights(gen):
        ws = gen % 2
        for_each_weight_copy(gen, lambda cp: cp.wait())
        w1b[ws] = wst1[ws].astype(BF16)
        w3b[ws] = wst3[ws].astype(BF16)
        w2b[ws] = wst2[ws].astype(BF16)

        @pl.when(gen + 1 < n_gen)
        def _():
            for_each_weight_copy(gen + 1, lambda cp: cp.start())

    def for_each_tail_piece(fn):
        def area_body(s, carry):
            used = used_ref[s]
            tail = AREA_CHUNKS - used
            row = (s * AREA_CHUNKS + used) * SUBLANES
            for bit in ZERO_BITS:
                take = (tail & bit) != 0

                @pl.when(take)
                def _(row=row, bit=bit):
                    fn(pltpu.make_async_copy(zbuf.at[pl.ds(0, bit * SUBLANES)],
                                             ys_hbm.at[pl.ds(pl.multiple_of(row, SUBLANES), bit * SUBLANES)],
                                             sem_zero))
                row = row + jnp.where(take, bit * SUBLANES, 0)
            return carry
        lax.fori_loop(0, n_areas, area_body, 0)

    @pl.when(n_gen > 0)
    def _():
        for_each_weight_copy(0, lambda cp: cp.start())

    zbuf[...] = jnp.zeros(zbuf.shape, U32)
    ybuf[...] = jnp.zeros(ybuf.shape, U32)
    dump = pltpu.make_async_copy(zbuf.at[pl.ds(0, DUMP_CHUNKS * SUBLANES)],
                                 ys_hbm.at[pl.ds(dump_row0, DUMP_CHUNKS * SUBLANES)], sem_dump)
    dump.start()
    dump.wait()
    for_each_tail_piece(lambda cp: cp.start())
    for sl in range(SCATTER_BUFS - SCATTER_LAG):
        for c in range(BLOCK_CHUNKS):
            out_copy(pl.ds(dump_row0 + (sl * BLOCK_CHUNKS + c) * SUBLANES, SUBLANES), sl, c).start()
    for k in range(lookahead):
        start_gather(k)

    @pl.when(total > 0)
    def _():
        switch_weights(0)
        wait_gather(0)
        rows = load_rows(0)
        issue_dmas(0)
        up_gate(0, rows, 0)

        def block_body(blk, carry):
            gen = bgen_ref[blk]
            gen_prev = bgen_ref[blk - 1]

            @pl.when(gen != gen_prev)
            def _():
                switch_weights(gen)

            wait_gather(blk)
            wait_scatter((blk - 1) % SCATTER_BUFS)
            rows = load_rows(blk)
            issue_dmas(blk)
            down(blk - 1, gen_prev % 2)
            up_gate(blk, rows, gen % 2)
            return carry

        lax.fori_loop(1, total, block_body, 0)
        wait_scatter((total - 1) % SCATTER_BUFS)
        down(total - 1, bgen_ref[total - 1] % 2)

    for k in range(SCATTER_LAG):
        start_scatter(total - SCATTER_LAG + k)
    for sl in range(SCATTER_BUFS):
        wait_scatter(sl)
    for k in range(lookahead):
        wait_gather(total + k)
    for_each_tail_piece(lambda cp: cp.wait())


def _moe_combine_kernel(h_ref, route_ref, ys_ref, l2g_ref, l2b_ref, y_ref):
    n = h_ref.shape[0]
    scope = n // (ys_ref.shape[0] // AREA_ROWS)
    s_iota = lax.broadcasted_iota(jnp.int32, (scope, AREA_ROWS), 1).astype(F32)
    for sc in range(n // scope):
        rows = slice(sc * scope, (sc + 1) * scope)
        route = route_ref[rows, :]
        slot1 = route[:, 0:1]
        slot2 = route[:, 1:2]
        gate1 = route[:, 2:3]
        gate2 = route[:, 3:4]
        gmat = (jnp.where(s_iota == slot1, gate1, 0.0) + jnp.where(s_iota == slot2, gate2, 0.0)).astype(BF16)
        y_lo, y_hi = _unpack_rows(ys_ref[sc * AREA_ROWS:(sc + 1) * AREA_ROWS, :])
        f = jnp.concatenate([jnp.dot(gmat, y_lo, preferred_element_type=F32),
                             jnp.dot(gmat, y_hi, preferred_element_type=F32)], axis=1)
        y_ref[rows, :] = _layer_norm(ALPHA * h_ref[rows, :] + f, l2g_ref[...], l2b_ref[...])


def _excl_cumsum(a, axis):
    return jnp.cumsum(a, axis=axis) - a


def _plan_blocks(cnt, q_max):
    n_areas = cnt.shape[0]
    c8 = (cnt + (SUBLANES - 1)) // SUBLANES
    used8 = jnp.sum(c8, axis=1)
    base8 = jnp.arange(n_areas, dtype=jnp.int32) * AREA_CHUNKS
    src_runs = (base8[:, None] + _excl_cumsum(c8, 1)).T
    len_runs = c8.T
    tot = jnp.sum(len_runs, axis=1)
    ptot = (tot + BLOCK_CHUNKS - 1) // BLOCK_CHUNKS * BLOCK_CHUNKS
    pstart = _excl_cumsum(ptot, 0)
    dst_runs = pstart[:, None] + _excl_cumsum(len_runs, 1)
    off = (src_runs - dst_runs).reshape(-1)
    diff = off - jnp.concatenate([jnp.zeros((1,), jnp.int32), off[:-1]])
    q = jnp.arange(q_max, dtype=jnp.int32)
    src = q + jnp.sum(jnp.where(dst_runs.reshape(1, -1) <= q[:, None], diff[None, :], 0), axis=1)
    valid = jnp.any((pstart[None, :] <= q[:, None]) & (q[:, None] < (pstart + tot)[None, :]), axis=1)
    zero_chunk = AREA_CHUNKS - 1
    ahead = jnp.full(((GATHER_BUFS - 1) * BLOCK_CHUNKS,), zero_chunk, jnp.int32)
    src_tab = jnp.concatenate([jnp.where(valid, src, zero_chunk).astype(jnp.int32), ahead])
    qv = jnp.arange(-SCATTER_LAG * BLOCK_CHUNKS, q_max, dtype=jnp.int32)
    dump = n_areas * AREA_CHUNKS + ((qv // BLOCK_CHUNKS) % SCATTER_BUFS) * BLOCK_CHUNKS + qv % BLOCK_CHUNKS
    lead = jnp.zeros((SCATTER_LAG * BLOCK_CHUNKS,), jnp.int32)
    dst_tab = jnp.where(jnp.concatenate([lead > 0, valid]), jnp.concatenate([lead, src]), dump).astype(jnp.int32)
    nblk = ptot // BLOCK_CHUNKS
    bstart = pstart // BLOCK_CHUNKS
    has = nblk > 0
    gen_of = jnp.cumsum(has.astype(jnp.int32)) - 1
    experts = jnp.arange(N_EXPERTS, dtype=jnp.int32)
    elist = jnp.sum(jnp.where(has[None, :] & (gen_of[None, :] == experts[:, None]), experts[None, :], 0), axis=1)
    blk = jnp.arange(q_max // BLOCK_CHUNKS, dtype=jnp.int32)
    owner = (bstart[None, :] <= blk[:, None]) & (blk[:, None] < (bstart + nblk)[None, :])
    bgen = jnp.sum(jnp.where(owner, gen_of[None, :], 0), axis=1)
    meta = jnp.stack([jnp.sum(nblk), jnp.sum(has.astype(jnp.int32))])
    return (src_tab, dst_tab, bgen.astype(jnp.int32), elist.astype(jnp.int32), meta.astype(jnp.int32),
            used8.astype(jnp.int32))


PREP_STEPS = 4


def _swap_head_halves(a):
    half = HEAD_DIM // 2
    pieces = []
    for g in range(a.shape[-1] // LANES):
        p = a[:, g * LANES:(g + 1) * LANES]
        lane = lax.broadcasted_iota(jnp.int32, p.shape, 1)
        pieces.append(jnp.where((lane & half) == 0, pltpu.roll(p, LANES - half, 1), pltpu.roll(p, half, 1)))
    return jnp.concatenate(pieces, axis=-1)


def _glu_blocks(a):
    lo = ATTN_DIM + 2 * KV_DIM
    pieces = []
    for cb in range(CONV_CH // LANES):
        pieces.append(a[:, lo + cb * LANES:lo + (cb + 1) * LANES])
        pieces.append(a[:, lo + CONV_CH + cb * LANES:lo + CONV_CH + (cb + 1) * LANES])
    return jnp.concatenate(pieces, axis=-1)


def _prep_weights_kernel(w_in_ref, b_in_ref, w_out_ref, w_in_b_ref, w_sw_ref, w_glu_ref, w_out_b_ref,
                         b_sw_ref, b_glu_ref):
    qk = ATTN_DIM + KV_DIM
    w = w_in_ref[...]
    w_in_b_ref[...] = w.astype(BF16)
    w_sw_ref[...] = _swap_head_halves(w[:, :qk]).astype(BF16)
    w_glu_ref[...] = _glu_blocks(w).astype(BF16)
    w_out_b_ref[...] = w_out_ref[...].astype(BF16)
    b = jnp.broadcast_to(b_in_ref[...], (SUBLANES, IN_DIM))
    b_sw_ref[...] = _swap_head_halves(b[:, :qk])[0:1]
    b_glu_ref[...] = _glu_blocks(b)[0:1]


def _rope_tables(pos):
    half = HEAD_DIM // 2
    inv = (ROPE_THETA ** (-np.arange(half, dtype=np.float64) / half)).astype(np.float32)
    ang = (np.asarray(pos, np.float32)[:, None] * inv).astype(np.float64)
    cos = np.cos(ang)
    sin = np.sin(ang)
    reps = LANES // HEAD_DIM
    cos_t = np.concatenate([cos, cos] * reps, axis=1)
    sin_t = np.concatenate([-sin, sin] * reps, axis=1)
    q_scale = ATTN_SCALE * LOG2E
    return jnp.asarray(np.concatenate([cos_t, sin_t, cos_t * q_scale, sin_t * q_scale], axis=1), F32)


def _full(shape):
    nd = len(shape)
    return pl.BlockSpec(shape, lambda *_: (0,) * nd)


def kernel(x_prompt, x_sample, cache_k, cache_v, state_conv, w_in, b_in, sinks, conv_w, conv_b, ln_conv_g,
           ln_conv_b, w_out, b_out, ln1_g, ln1_b, w_router_group, w_router_expert, w1, w3, w2, ln2_g, ln2_b):
    assert w_in.shape[0] == DEPTH
    bsz, seq, _ = x_prompt.shape
    dec_b, dec_t, _ = x_sample.shape
    assert dec_t == 1 and seq % TILE_Q == 0 and dec_b % SAMPLE_TB == 0 and 2 * dec_b + 7 * N_EXPERTS <= AREA_ROWS
    assert (bsz * seq) % COMBINE_TILE == 0
    n_prompt = bsz * seq
    tiles = seq // TILE_Q
    spt = TILE_Q // PROMPT_SCOPE
    p_areas = n_prompt // PROMPT_SCOPE
    n_areas = p_areas + spt
    q_max = (n_areas * AREA_CHUNKS + N_EXPERTS * (BLOCK_CHUNKS - 1) + BLOCK_CHUNKS - 1) // BLOCK_CHUNKS * BLOCK_CHUNKS

    row = lambda a: a.reshape(1, -1)
    cparams = lambda sem: pltpu.CompilerParams(dimension_semantics=sem, vmem_limit_bytes=VMEM_LIMIT)

    qk = ATTN_DIM + KV_DIM
    wrows = D_MODEL // PREP_STEPS
    wspec = lambda cols: pl.BlockSpec((wrows, cols), lambda t: (t, 0))
    w_in_b, w_sw_b, w_glu_b, w_out_b, b_sw, b_glu = pl.pallas_call(
        _prep_weights_kernel,
        grid=(PREP_STEPS,),
        in_specs=[wspec(IN_DIM), _full((1, IN_DIM)), wspec(D_MODEL)],
        out_specs=[wspec(IN_DIM), wspec(qk), wspec(2 * CONV_CH), wspec(D_MODEL),
                   _full((1, qk)), _full((1, 2 * CONV_CH))],
        out_shape=[jax.ShapeDtypeStruct((D_MODEL, IN_DIM), BF16), jax.ShapeDtypeStruct((D_MODEL, qk), BF16),
                   jax.ShapeDtypeStruct((D_MODEL, 2 * CONV_CH), BF16), jax.ShapeDtypeStruct((D_MODEL, D_MODEL), BF16),
                   jax.ShapeDtypeStruct((1, qk), F32), jax.ShapeDtypeStruct((1, 2 * CONV_CH), F32)],
        compiler_params=cparams(("arbitrary",)),
        name="prep_weights",
    )(w_in[0], row(b_in[0]), w_out[0])
    w_r = jnp.concatenate([w_router_group[0], w_router_expert[0].reshape(D_MODEL, N_EXPERTS),
                           jnp.zeros((D_MODEL, LANES - ELANE0 - N_EXPERTS), F32)], axis=1).astype(BF16)
    shared = (w_in_b, row(b_in[0]), w_sw_b, b_sw, conv_w[0], row(conv_b[0]),
              row(ln_conv_g[0]), row(ln_conv_b[0]), w_out_b, row(b_out[0]), row(ln1_g[0]), row(ln1_b[0]), w_r)
    shared_specs = [_full(a.shape) for a in shared]
    glu = (w_glu_b, b_glu)
    glu_specs = [_full(a.shape) for a in glu]
    assert len(glu) + len(shared) == N_PROMPT_WEIGHTS
    smem = pl.BlockSpec(memory_space=pltpu.SMEM)
    rope_p = _rope_tables(np.arange(seq))
    rope_s = _rope_tables(PAST_LEN + np.arange(dec_t))

    ck = cache_k[0].reshape(dec_b, WINDOW, KV_DIM)
    cv = cache_v[0].reshape(dec_b, WINDOW, KV_DIM)
    tb = SAMPLE_TB
    conv_state_spec = pl.BlockSpec((None, tb, CONV_WIDTH - 1, CONV_CH), lambda t: (0, t, 0, 0))
    h_s, nk_s, nv_s, nc_s, xs_s, route_s, cnt_s = pl.pallas_call(
        _mix_sample_kernel,
        grid=(dec_b // tb,),
        in_specs=[smem,
                  pl.BlockSpec((tb, None, D_MODEL), lambda t: (t, 0, 0)),
                  _full(rope_s.shape),
                  pl.BlockSpec((tb, WINDOW, KV_DIM), lambda t: (t, 0, 0)),
                  pl.BlockSpec((tb, WINDOW, KV_DIM), lambda t: (t, 0, 0)),
                  conv_state_spec] + shared_specs,
        out_specs=[pl.BlockSpec((tb, D_MODEL), lambda t: (t, 0)),
                   pl.BlockSpec((tb, WINDOW, KV_DIM), lambda t: (t, 0, 0)),
                   pl.BlockSpec((tb, WINDOW, KV_DIM), lambda t: (t, 0, 0)),
                   conv_state_spec,
                   _full((AREA_ROWS, HALF)),
                   _full((dec_b, LANES)),
                   _full((1, SUBLANES, LANES))],
        out_shape=[jax.ShapeDtypeStruct((dec_b, D_MODEL), F32),
                   jax.ShapeDtypeStruct((dec_b, WINDOW, KV_DIM), F32),
                   jax.ShapeDtypeStruct((dec_b, WINDOW, KV_DIM), F32),
                   jax.ShapeDtypeStruct((DEPTH, dec_b, CONV_WIDTH - 1, CONV_CH), F32),
                   jax.ShapeDtypeStruct((AREA_ROWS, HALF), U32),
                   jax.ShapeDtypeStruct((dec_b, LANES), F32),
                   jax.ShapeDtypeStruct((1, SUBLANES, LANES), F32)],
        scratch_shapes=[pltpu.VMEM((dec_b, D_MODEL), F32)],
        compiler_params=cparams(("arbitrary",)),
        name="mix_sample",
    )(sinks[0], x_sample, rope_s, ck, cv, state_conv, *shared)

    n_real = bsz * tiles
    real = lambda t: jnp.minimum(t, n_real - 1)
    h_p, xs, route_p, cnt_p, nk_p, nv_p, nc_p = pl.pallas_call(
        functools.partial(_mix_prompt_kernel, tiles=tiles),
        grid=(n_real + 1,),
        in_specs=[smem,
                  pl.BlockSpec((1, TILE_Q, D_MODEL), lambda t: (real(t) // tiles, real(t) % tiles, 0)),
                  pl.BlockSpec((TILE_Q, 4 * LANES), lambda t: (real(t) % tiles, 0)),
                  _full((AREA_ROWS, HALF))] + glu_specs + shared_specs,
        out_specs=[pl.BlockSpec((1, TILE_Q, D_MODEL), lambda t: (real(t) // tiles, real(t) % tiles, 0)),
                   pl.BlockSpec((spt * AREA_ROWS, HALF), lambda t: (t, 0)),
                   pl.BlockSpec((TILE_Q, LANES), lambda t: (real(t), 0)),
                   pl.BlockSpec((spt, SUBLANES, LANES), lambda t: (t, 0, 0)),
                   pl.BlockSpec((1, WINDOW, KV_DIM), lambda t: (real(t) // tiles, 0, 0)),
                   pl.BlockSpec((1, WINDOW, KV_DIM), lambda t: (real(t) // tiles, 0, 0)),
                   pl.BlockSpec((1, CONV_WIDTH - 1, CONV_CH), lambda t: (real(t) // tiles, 0, 0))],
        out_shape=[jax.ShapeDtypeStruct((bsz, seq, D_MODEL), F32),
                   jax.ShapeDtypeStruct((n_areas * AREA_ROWS, HALF), U32),
                   jax.ShapeDtypeStruct((n_prompt, LANES), F32),
                   jax.ShapeDtypeStruct((n_areas, SUBLANES, LANES), F32),
                   jax.ShapeDtypeStruct((bsz, WINDOW, KV_DIM), F32),
                   jax.ShapeDtypeStruct((bsz, WINDOW, KV_DIM), F32),
                   jax.ShapeDtypeStruct((bsz, CONV_WIDTH - 1, CONV_CH), F32)],
        scratch_shapes=[pltpu.VMEM((WINDOW, KV_DIM), F32), pltpu.VMEM((WINDOW, KV_DIM), F32),
                        pltpu.VMEM((TILE_Q + CONV_PRE, CONV_CH), F32),
                        pltpu.VMEM((SUBLANES - 1, TILE_Q + CONV_SHIFT_EXTRA, CONV_CH), F32)],
        compiler_params=cparams(("arbitrary",)),
        name="mix_prompt",
    )(sinks[0], x_prompt, rope_p, xs_s, *glu, *shared)

    cnt = jnp.concatenate([cnt_p[:p_areas, 0], cnt_s[:, 0], cnt_p[p_areas + 1:, 0]], axis=0)
    cnt = cnt[:, ELANE0:ELANE0 + N_EXPERTS].astype(jnp.int32)
    plan = _plan_blocks(cnt, q_max)
    hbm = pl.BlockSpec(memory_space=pl.ANY)
    up_shape, down_shape = (2, D_MODEL, D_FF_EXPERT), (2, D_FF_EXPERT, D_MODEL)
    ys = pl.pallas_call(
        _moe_ffn_kernel,
        grid_spec=pltpu.PrefetchScalarGridSpec(
            num_scalar_prefetch=len(plan),
            grid=(1,),
            in_specs=[hbm, hbm, hbm, hbm],
            out_specs=hbm,
            scratch_shapes=[pltpu.VMEM((GATHER_BUFS, BLOCK_ROWS, HALF), U32),
                            pltpu.VMEM((SCATTER_BUFS, BLOCK_ROWS, HALF), U32),
                            pltpu.VMEM((SCATTER_BUFS, BLOCK_ROWS, D_FF_EXPERT), BF16),
                            pltpu.VMEM((ZERO_ROWS, HALF), U32),
                            pltpu.VMEM(up_shape, F32), pltpu.VMEM(up_shape, F32), pltpu.VMEM(down_shape, F32),
                            pltpu.VMEM(up_shape, BF16), pltpu.VMEM(up_shape, BF16), pltpu.VMEM(down_shape, BF16),
                            pltpu.SemaphoreType.DMA((GATHER_BUFS,)), pltpu.SemaphoreType.DMA((SCATTER_BUFS,)),
                            pltpu.SemaphoreType.DMA(()), pltpu.SemaphoreType.DMA(()),
                            pltpu.SemaphoreType.DMA((2,))]),
        out_shape=jax.ShapeDtypeStruct((n_areas * AREA_ROWS + DUMP_CHUNKS * SUBLANES, HALF), U32),
        compiler_params=cparams(("arbitrary",)),
        name="moe_ffn",
    )(*plan, xs, w1[0], w3[0], w2[0])

    def combine(h2d, route, first_area, scope, tile):
        n = h2d.shape[0]
        apt = tile // scope
        return pl.pallas_call(
            _moe_combine_kernel,
            grid=(n // tile,),
            in_specs=[pl.BlockSpec((tile, D_MODEL), lambda t: (t, 0)),
                      pl.BlockSpec((tile, LANES), lambda t: (t, 0)),
                      pl.BlockSpec((apt * AREA_ROWS, HALF), lambda t: (first_area // apt + t, 0)),
                      _full((1, D_MODEL)), _full((1, D_MODEL))],
            out_specs=pl.BlockSpec((tile, D_MODEL), lambda t: (t, 0)),
            out_shape=jax.ShapeDtypeStruct((n, D_MODEL), F32),
            compiler_params=cparams(("arbitrary",)),
            name="moe_combine",
        )(h2d, route, ys, row(ln2_g[0]), row(ln2_b[0]))

    y_p = combine(h_p.reshape(n_prompt, D_MODEL), route_p, 0, PROMPT_SCOPE, COMBINE_TILE)
    y_s = combine(h_s, route_s, p_areas, dec_b, dec_b)

    kv_shape = lambda n: (DEPTH, n, WINDOW, N_KV_HEADS, HEAD_DIM)
    return (y_p.reshape(bsz, seq, D_MODEL), y_s.reshape(dec_b, dec_t, D_MODEL),
            nk_p.reshape(kv_shape(bsz)), nv_p.reshape(kv_shape(bsz)), nc_p[None],
            nk_s.reshape(kv_shape(dec_b)), nv_s.reshape(kv_shape(dec_b)), nc_s)
```

```python
import functools

import jax
import jax.numpy as jnp
import numpy as np
from jax import lax
from jax.experimental import pallas as pl
from jax.experimental.pallas import tpu as pltpu

F32 = jnp.float32
BF16 = jnp.bfloat16
U32 = jnp.uint32

D_MODEL = 1024
N_HEADS = 8
N_KV_HEADS = 2
HEAD_DIM = 64
Q_GROUP = N_HEADS // N_KV_HEADS
ATTN_DIM = N_HEADS * HEAD_DIM
KV_DIM = N_KV_HEADS * HEAD_DIM
WINDOW = 128
ROPE_THETA = 10000.0
ATTN_SCALE = HEAD_DIM ** -0.5
CONV_CH = D_MODEL - ATTN_DIM
CONV_WIDTH = 31
IN_DIM = ATTN_DIM + 2 * KV_DIM + 2 * CONV_CH
N_GROUPS = 4
EXPERTS_PER_GROUP = 8
N_EXPERTS = N_GROUPS * EXPERTS_PER_GROUP
D_FF_EXPERT = 256
LN_EPS = 1e-5
NEG_INF = -1e30
DEPTH = 1
ALPHA = (2.0 * DEPTH) ** 0.25
PAST_LEN = 16384
LOG2E = 1.4426950408889634

LANES = 128
SUBLANES = 8
HALF = D_MODEL // 2
ELANE0 = N_GROUPS
PROMPT_SCOPE = 256
AREA_ROWS = -(-(2 * PROMPT_SCOPE + (SUBLANES - 1) * N_EXPERTS) // LANES) * LANES
AREA_CHUNKS = AREA_ROWS // SUBLANES
BLOCK_CHUNKS = 32
BLOCK_ROWS = BLOCK_CHUNKS * SUBLANES
TILE_Q = 512
COMBINE_TILE = 1024
CONV_PRE = 32
CONV_SHIFT_EXTRA = 24
SAMPLE_TB = 32
SAMPLE_GRP = 8
VMEM_LIMIT = 56 * 1024 * 1024


def _layer_norm(x, g, b):
    mu = jnp.mean(x, axis=-1, keepdims=True)
    xc = x - mu
    var = jnp.mean(xc * xc, axis=-1, keepdims=True)
    return xc * lax.rsqrt(var + LN_EPS) * g + b


def _rope(x, x_swapped, cos, sin_signed):
    pieces = [x[:, g * LANES:(g + 1) * LANES] * cos + x_swapped[:, g * LANES:(g + 1) * LANES] * sin_signed
              for g in range(x.shape[-1] // LANES)]
    return pieces[0] if len(pieces) == 1 else jnp.concatenate(pieces, axis=-1)


def _in_proj_glu(xb, w_in_ref, b_in_ref):
    lo, hi = ATTN_DIM + 2 * KV_DIM, IN_DIM
    cacg = jnp.dot(xb, w_in_ref[:, lo:hi], preferred_element_type=F32) + b_in_ref[:, lo:hi]
    return cacg[:, :CONV_CH] * jax.nn.sigmoid(cacg[:, CONV_CH:])


def _in_proj_qkv(xb, w_in_ref, b_in_ref, w_sw_ref, b_sw_ref, rope):
    o_v, o_e = ATTN_DIM + KV_DIM, ATTN_DIM + 2 * KV_DIM
    qkv = jnp.dot(xb, w_in_ref[:, 0:o_e], preferred_element_type=F32) + b_in_ref[:, 0:o_e]
    sw = jnp.dot(xb, w_sw_ref[...], preferred_element_type=F32) + b_sw_ref[...]
    k = _rope(qkv[:, ATTN_DIM:o_v], sw[:, ATTN_DIM:], rope[:, 0:LANES], rope[:, LANES:2 * LANES])
    q = _rope(qkv[:, :ATTN_DIM], sw[:, :ATTN_DIM], rope[:, 2 * LANES:3 * LANES], rope[:, 3 * LANES:])
    return q, k, qkv[:, o_v:]


def _pack_rows(v):
    lo = lax.bitcast_convert_type(v[:, :HALF], U32)
    hi = lax.bitcast_convert_type(v[:, HALF:], U32)
    return (lo >> 16) | (hi & jnp.uint32(0xFFFF0000))


def _unpack_rows(w):
    lo = lax.bitcast_convert_type(w << 16, F32).astype(BF16)
    hi = lax.bitcast_convert_type(w & jnp.uint32(0xFFFF0000), F32).astype(BF16)
    return lo, hi


def _drain(stages):
    try:
        while True:
            next(stages)
    except StopIteration as stop:
        return stop.value


def _round_robin(*stage_generators):
    live = list(stage_generators)
    while live:
        for stages in list(live):
            try:
                next(stages)
            except StopIteration:
                live.remove(stages)


def _route_and_sort_stages(h, w_r):
    n = h.shape[0]
    hb = h.astype(BF16)
    logits = jnp.dot(hb, w_r, preferred_element_type=F32)
    yield
    lane = lax.broadcasted_iota(jnp.int32, (n, LANES), 1)
    lane_f = lane.astype(F32)
    is_g = lane < N_GROUPS
    lg = jnp.where(is_g, logits, -jnp.inf)
    gmax = jnp.max(lg, axis=-1, keepdims=True)
    yield
    gidx = jnp.min(jnp.where(lg == gmax, lane_f, float(LANES)), axis=-1, keepdims=True)
    p_g = 1.0 / jnp.sum(jnp.where(is_g, jnp.exp(logits - gmax), 0.0), axis=-1, keepdims=True)
    yield
    egrp = ((lane - ELANE0) >> (EXPERTS_PER_GROUP.bit_length() - 1)).astype(F32)
    emask = (lane >= ELANE0) & (lane < ELANE0 + N_EXPERTS) & (egrp == gidx)
    sel = jnp.where(emask, logits, -jnp.inf)
    v1 = jnp.max(sel, axis=-1, keepdims=True)
    yield
    i1 = jnp.min(jnp.where(sel == v1, lane_f, float(LANES)), axis=-1, keepdims=True)
    yield
    sel2 = jnp.where(lane_f == i1, -jnp.inf, sel)
    v2 = jnp.max(sel2, axis=-1, keepdims=True)
    yield
    i2 = jnp.min(jnp.where(sel2 == v2, lane_f, float(LANES)), axis=-1, keepdims=True)
    t = jnp.exp(v2 - v1)
    gate1 = p_g / (1.0 + t)
    gate2 = p_g * t / (1.0 + t)
    pick1 = lane_f == i1
    pick2 = lane_f == i2
    member = jnp.where(pick1 | pick2, 1.0, 0.0)
    yield

    r_i = lax.broadcasted_iota(jnp.int32, (n, n), 0)
    c_i = lax.broadcasted_iota(jnp.int32, (n, n), 1)
    lower = jnp.where(c_i < r_i, 1.0, 0.0).astype(BF16)
    e_r = lax.broadcasted_iota(jnp.int32, (LANES, LANES), 0)
    e_c = lax.broadcasted_iota(jnp.int32, (LANES, LANES), 1)
    upper = jnp.where(e_r < e_c, 1.0, 0.0).astype(BF16)
    before = jnp.dot(lower, member.astype(BF16), preferred_element_type=F32)
    cnt = jnp.sum(member, axis=0, keepdims=True)
    c8 = jnp.floor((cnt + (SUBLANES - 1)) * (1.0 / SUBLANES))
    c8b = jnp.broadcast_to(c8, (SUBLANES, LANES)).astype(BF16)
    off8 = jnp.dot(c8b, upper, preferred_element_type=F32)[0:1]
    yield
    slot_all = before + off8 * float(SUBLANES)
    slot1 = jnp.sum(jnp.where(pick1, slot_all, 0.0), axis=-1, keepdims=True)
    slot2 = jnp.sum(jnp.where(pick2, slot_all, 0.0), axis=-1, keepdims=True)
    yield
    s_iota = lax.broadcasted_iota(jnp.int32, (n, AREA_ROWS), 1).astype(F32)
    perm_t = jnp.where((s_iota == slot1) | (s_iota == slot2), 1.0, 0.0).astype(BF16)
    sorted_rows = lax.dot_general(perm_t, hb, (((0,), (0,)), ((), ())),
                                  preferred_element_type=F32)
    yield
    route = jnp.where(lane == 0, slot1, jnp.where(lane == 1, slot2, jnp.where(lane == 2, gate1,
                      jnp.where(lane == 3, gate2, 0.0))))
    return route, _pack_rows(sorted_rows), cnt


def _softmax_rows_with_sink(s, sink, extra=None):
    m = jnp.maximum(jnp.max(s, axis=-1, keepdims=True), sink)
    if extra is not None:
        m = jnp.maximum(m, extra)
    e = jnp.exp2(s - m)
    den = jnp.sum(e, axis=-1, keepdims=True) + jnp.exp2(sink - m)
    if extra is not None:
        ee = jnp.exp2(extra - m)
        den = den + ee
        return e / den, ee / den
    return e / den, None


def _mix_out_stages(x, attn, conv_pre, lcg, lcb, w_out_ref, b_out, l1g, l1b):
    conv = _layer_norm(conv_pre, lcg, lcb)
    conv = conv * jax.nn.sigmoid(conv)
    yield
    mix = (jnp.dot(attn.astype(BF16), w_out_ref[0:ATTN_DIM, :], preferred_element_type=F32)
           + jnp.dot(conv.astype(BF16), w_out_ref[ATTN_DIM:, :], preferred_element_type=F32) + b_out)
    yield
    h = _layer_norm(ALPHA * x + mix, l1g, l1b)
    yield
    return h


N_PROMPT_WEIGHTS = 15


def _mix_prompt_kernel(sinks_ref, x_ref, rope_ref, xs_s_ref, *rest, tiles):
    _, xs_ref, _, cnt_ref = rest[N_PROMPT_WEIGHTS:N_PROMPT_WEIGHTS + 4]
    step = pl.program_id(0)
    n_real = pl.num_programs(0) - 1

    @pl.when(step < n_real)
    def _():
        _mix_prompt_tile(sinks_ref, x_ref, rope_ref, *rest, i=step % tiles)

    @pl.when(step == n_real)
    def _():
        xs_ref[0:AREA_ROWS, :] = xs_s_ref[...]
        xs_ref[AREA_ROWS:, :] = jnp.zeros((xs_ref.shape[0] - AREA_ROWS, HALF), U32)
        cnt_ref[...] = jnp.zeros(cnt_ref.shape, F32)


def _mix_prompt_tile(sinks_ref, x_ref, rope_ref, w_glu_ref, b_glu_ref,
                     w_in_ref, b_in_ref, w_sw_ref, b_sw_ref, conv_w_ref, conv_b_ref,
                     lcg_ref, lcb_ref, w_out_ref, b_out_ref, l1g_ref, l1b_ref, w_r_ref,
                     h_ref, xs_ref, route_ref, cnt_ref, nk_ref, nv_ref, nc_ref,
                     kprev, vprev, u_scr, u_shift, *, i):
    tq = x_ref.shape[1]
    pre = CONV_PRE
    n_sc = tq // PROMPT_SCOPE

    @pl.when(i == 0)
    def _():
        kprev[...] = jnp.zeros_like(kprev)
        vprev[...] = jnp.zeros_like(vprev)
        u_scr[0:pre, :] = jnp.zeros((pre, CONV_CH), F32)

    x = x_ref[0]
    xb = x.astype(BF16)

    def glu_block(cb):
        cols = slice(2 * LANES * cb, 2 * LANES * (cb + 1))
        vg = jnp.dot(xb, w_glu_ref[:, cols], preferred_element_type=F32) + b_glu_ref[:, cols]
        return vg[:, :LANES] * jax.nn.sigmoid(vg[:, LANES:])

    n_cb = CONV_CH // LANES
    acc_blocks = [[None] * n_cb for _ in range(n_sc)]
    u_next = glu_block(0)
    for cb in range(n_cb):
        lanes = slice(LANES * cb, LANES * (cb + 1))
        u_scr[pre:pre + tq, lanes] = u_next
        if cb + 1 < n_cb:
            u_next = glu_block(cb + 1)
        for r in range(1, SUBLANES):
            u_shift[r - 1, :, lanes] = u_scr[r:r + tq + CONV_SHIFT_EXTRA, lanes]
        for sc in range(n_sc):
            base = sc * PROMPT_SCOPE
            acc = jnp.zeros((PROMPT_SCOPE, LANES), F32)
            for j in range(CONV_WIDTH):
                a, r = divmod(pre - (CONV_WIDTH - 1) + j, SUBLANES)
                start = base + a * SUBLANES
                src = (u_scr[start:start + PROMPT_SCOPE, lanes] if r == 0
                       else u_shift[r - 1, start:start + PROMPT_SCOPE, lanes])
                acc = acc + src * conv_w_ref[j:j + 1, lanes]
            acc_blocks[sc][cb] = acc
    conv_pre = [jnp.concatenate(blocks, axis=1) + conv_b_ref[...] for blocks in acc_blocks]

    nc_ref[0] = u_scr[pre + tq - (CONV_WIDTH - 1):pre + tq, :]
    tail = u_scr[tq:tq + pre, :]
    u_scr[0:pre, :] = tail

    q, k, v = _in_proj_qkv(xb, w_in_ref, b_in_ref, w_sw_ref, b_sw_ref, rope_ref[...])
    qb = q.astype(BF16)
    kb = jnp.concatenate([kprev[...], k], axis=0).astype(BF16)
    vb = jnp.concatenate([vprev[...], v], axis=0).astype(BF16)
    row = lax.broadcasted_iota(jnp.int32, (WINDOW, 2 * WINDOW), 0)
    col = lax.broadcasted_iota(jnp.int32, (WINDOW, 2 * WINDOW), 1)
    dist = row + WINDOW - col
    band = (dist >= 0) & (dist < WINDOW)
    first_col = jnp.where(i > 0, 0, WINDOW)
    kprev[...] = k[tq - WINDOW:, :]
    vprev[...] = v[tq - WINDOW:, :]
    nk_ref[0] = k[tq - WINDOW:, :]
    nv_ref[0] = v[tq - WINDOW:, :]
    blocks_per_scope = PROMPT_SCOPE // WINDOW
    attn_blocks = [None] * (tq // WINDOW)

    def scope_attention(sc):
        for j in range(sc * blocks_per_scope, (sc + 1) * blocks_per_scope):
            msk = (band & (col >= first_col)) if j == 0 else band
            heads = [None] * N_HEADS
            for c in range(N_KV_HEADS):
                kc = kb[j * WINDOW:(j + 2) * WINDOW, c * HEAD_DIM:(c + 1) * HEAD_DIM]
                vc = vb[j * WINDOW:(j + 2) * WINDOW, c * HEAD_DIM:(c + 1) * HEAD_DIM]
                qc = jnp.concatenate(
                    [qb[j * WINDOW:(j + 1) * WINDOW, (c * Q_GROUP + g) * HEAD_DIM:(c * Q_GROUP + g + 1) * HEAD_DIM]
                     for g in range(Q_GROUP)], axis=0)
                s = lax.dot_general(qc, kc, (((1,), (1,)), ((), ())), preferred_element_type=F32)
                yield
                probs = []
                for g in range(Q_GROUP):
                    sg = jnp.where(msk, s[g * WINDOW:(g + 1) * WINDOW], NEG_INF)
                    p, _ = _softmax_rows_with_sink(sg, sinks_ref[c * Q_GROUP + g] * LOG2E)
                    probs.append(p.astype(BF16))
                    if g % 2 == 1:
                        yield
                o = jnp.dot(jnp.concatenate(probs, axis=0), vc, preferred_element_type=F32)
                for g in range(Q_GROUP):
                    heads[c * Q_GROUP + g] = o[g * WINDOW:(g + 1) * WINDOW]
                yield
            attn_blocks[j] = jnp.concatenate(heads, axis=1)

    def scope_tail(sc):
        rows = slice(sc * PROMPT_SCOPE, (sc + 1) * PROMPT_SCOPE)
        attn = jnp.concatenate(attn_blocks[sc * blocks_per_scope:(sc + 1) * blocks_per_scope], axis=0)
        h = yield from _mix_out_stages(x[rows], attn, conv_pre[sc], lcg_ref[...], lcb_ref[...],
                                       w_out_ref, b_out_ref[...], l1g_ref[...], l1b_ref[...])
        route, area, cnt = yield from _route_and_sort_stages(h, w_r_ref[...])
        h_ref[0, rows, :] = h
        route_ref[rows, :] = route
        xs_ref[sc * AREA_ROWS:(sc + 1) * AREA_ROWS, :] = area
        cnt_ref[sc] = jnp.broadcast_to(cnt, (SUBLANES, LANES))

    _round_robin(*[scope_attention(sc) for sc in range(n_sc)])
    _round_robin(*[scope_tail(sc) for sc in range(n_sc)])


def _mix_sample_kernel(sinks_ref, x_ref, rope_ref, ck_ref, cv_ref, st_ref, w_in_ref, b_in_ref, w_sw_ref, b_sw_ref,
                       conv_w_ref, conv_b_ref, lcg_ref, lcb_ref, w_out_ref, b_out_ref, l1g_ref, l1b_ref,
                       w_r_ref,
                       h_ref, nk_ref, nv_ref, nc_ref, xs_ref, route_ref, cnt_ref,
                       h_scr):
    tb = x_ref.shape[0]
    n_tok = h_scr.shape[0]
    step = pl.program_id(0)
    x = x_ref[...]
    xb = x.astype(BF16)
    u = _in_proj_glu(xb, w_in_ref, b_in_ref)
    q, k_new, v_new = _in_proj_qkv(xb, w_in_ref, b_in_ref, w_sw_ref, b_sw_ref, rope_ref[...])

    qb = q.astype(BF16)
    knb = k_new.astype(BF16).astype(F32)
    vnb = v_new.astype(BF16).astype(F32)
    gq = SAMPLE_GRP * Q_GROUP
    nkeys = SAMPLE_GRP * WINDOW
    r_i = lax.broadcasted_iota(jnp.int32, (gq, nkeys), 0)
    c_i = lax.broadcasted_iota(jnp.int32, (gq, nkeys), 1)
    window_shift, grp_shift = WINDOW.bit_length() - 1, SAMPLE_GRP.bit_length() - 1
    msk = ((c_i >> window_shift) == (r_i & (SAMPLE_GRP - 1))) & ((c_i & (WINDOW - 1)) >= 1)
    r_col = lax.broadcasted_iota(jnp.int32, (gq, 1), 0) >> grp_shift
    attn_groups = [None] * (tb // SAMPLE_GRP)

    def group_attention(grp):
        r0 = grp * SAMPLE_GRP
        kc_all = ck_ref[r0:r0 + SAMPLE_GRP].reshape(nkeys, KV_DIM).astype(BF16)
        vc_all = cv_ref[r0:r0 + SAMPLE_GRP].reshape(nkeys, KV_DIM).astype(BF16)
        yield
        heads = [None] * N_HEADS
        for c in range(N_KV_HEADS):
            lanes = slice(c * HEAD_DIM, (c + 1) * HEAD_DIM)
            qc = jnp.concatenate(
                [qb[r0:r0 + SAMPLE_GRP, (c * Q_GROUP + g) * HEAD_DIM:(c * Q_GROUP + g + 1) * HEAD_DIM]
                 for g in range(Q_GROUP)], axis=0)
            s = lax.dot_general(qc, kc_all[:, lanes], (((1,), (1,)), ((), ())), preferred_element_type=F32)
            s = jnp.where(msk, s, NEG_INF)
            kn = jnp.concatenate([knb[r0:r0 + SAMPLE_GRP, lanes]] * Q_GROUP, axis=0)
            vn = jnp.concatenate([vnb[r0:r0 + SAMPLE_GRP, lanes]] * Q_GROUP, axis=0)
            s_new = jnp.sum(qc.astype(F32) * kn, axis=-1, keepdims=True)
            sink = jnp.zeros((gq, 1), F32)
            for g in range(Q_GROUP):
                sink = jnp.where(r_col == g, sinks_ref[c * Q_GROUP + g] * LOG2E, sink)
            yield
            p, p_new = _softmax_rows_with_sink(s, sink, s_new)
            yield
            o = (jnp.dot(p.astype(BF16), vc_all[:, lanes], preferred_element_type=F32)
                 + p_new.astype(BF16).astype(F32) * vn)
            for g in range(Q_GROUP):
                heads[c * Q_GROUP + g] = o[g * SAMPLE_GRP:(g + 1) * SAMPLE_GRP]
            yield
        attn_groups[grp] = jnp.concatenate(heads, axis=1)

    _round_robin(*[group_attention(grp) for grp in range(len(attn_groups))])
    attn = jnp.concatenate(attn_groups, axis=0)

    st = st_ref[...]
    acc = jnp.sum(st * conv_w_ref[0:CONV_WIDTH - 1, :][None], axis=1)
    conv_pre = acc + u * conv_w_ref[CONV_WIDTH - 1:CONV_WIDTH, :] + conv_b_ref[...]

    nk_ref[:, 0:WINDOW - 1, :] = ck_ref[:, 1:WINDOW, :]
    nv_ref[:, 0:WINDOW - 1, :] = cv_ref[:, 1:WINDOW, :]
    nc_ref[:, 0:CONV_WIDTH - 2, :] = st_ref[:, 1:CONV_WIDTH - 1, :]
    for b in range(tb):
        nk_ref[b, WINDOW - 1:WINDOW, :] = k_new[b:b + 1, :]
        nv_ref[b, WINDOW - 1:WINDOW, :] = v_new[b:b + 1, :]
        nc_ref[b, CONV_WIDTH - 2:CONV_WIDTH - 1, :] = u[b:b + 1, :]

    h = _drain(_mix_out_stages(x, attn, conv_pre, lcg_ref[...], lcb_ref[...], w_out_ref, b_out_ref[...],
                               l1g_ref[...], l1b_ref[...]))
    h_ref[...] = h
    h_scr[pl.ds(pl.multiple_of(step * tb, tb), tb), :] = h

    @pl.when(step == pl.num_programs(0) - 1)
    def _():
        route, area, cnt = _drain(_route_and_sort_stages(h_scr[...], w_r_ref[...]))
        route_ref[...] = route
        xs_ref[...] = area
        cnt_ref[0] = jnp.broadcast_to(cnt, (SUBLANES, LANES))


GATHER_BUFS = 8
SCATTER_BUFS = 8
SCATTER_LAG = 2
ZERO_BITS = (64, 32, 16, 8, 4, 2, 1)
DUMP_CHUNKS = SCATTER_BUFS * BLOCK_CHUNKS
ZERO_ROWS = max(ZERO_BITS[0], DUMP_CHUNKS) * SUBLANES


def _moe_ffn_kernel(src_ref, dst_ref, bgen_ref, elist_ref, meta_ref, used_ref,
                    xs_hbm, w1_hbm, w3_hbm, w2_hbm,
                    ys_hbm,
                    xbuf, ybuf, hbuf, zbuf, wst1, wst3, wst2, w1b, w3b, w2b,
                    sem_in, sem_out, sem_zero, sem_dump, sem_w):
    n_areas = used_ref.shape[0]
    total = meta_ref[0]
    n_gen = meta_ref[1]
    dump_row0 = n_areas * AREA_ROWS
    lookahead = GATHER_BUFS - 1
    any_rows = pl.ds(0, SUBLANES)

    def chunk_at(ref, idx):
        return pl.ds(pl.multiple_of(ref[idx] * SUBLANES, SUBLANES), SUBLANES)

    def in_copy(rows, sl, c):
        return pltpu.make_async_copy(xs_hbm.at[rows], xbuf.at[sl, pl.ds(c * SUBLANES, SUBLANES)], sem_in.at[sl])

    def out_copy(rows, sl, c):
        return pltpu.make_async_copy(ybuf.at[sl, pl.ds(c * SUBLANES, SUBLANES)], ys_hbm.at[rows], sem_out.at[sl])

    def start_gather(blk):
        for c in range(BLOCK_CHUNKS):
            in_copy(chunk_at(src_ref, blk * BLOCK_CHUNKS + c), blk % GATHER_BUFS, c).start(priority=c % 2)

    def wait_gather(blk):
        for c in range(BLOCK_CHUNKS):
            in_copy(any_rows, blk % GATHER_BUFS, c).wait()

    def start_scatter(blk):
        for c in range(BLOCK_CHUNKS):
            out_copy(chunk_at(dst_ref, (blk + SCATTER_LAG) * BLOCK_CHUNKS + c), blk % SCATTER_BUFS,
                     c).start(priority=c % 2)

    def wait_scatter(sl):
        for c in range(BLOCK_CHUNKS):
            out_copy(any_rows, sl, c).wait()

    def load_rows(blk):
        return _unpack_rows(xbuf[blk % GATHER_BUFS])

    def issue_dmas(blk):
        start_scatter(blk - SCATTER_LAG)
        start_gather(blk + lookahead)

    def up_gate(blk, rows, ws):
        x_lo, x_hi = rows
        a1 = (jnp.dot(x_lo, w1b[ws, 0:HALF, :], preferred_element_type=F32)
              + jnp.dot(x_hi, w1b[ws, HALF:, :], preferred_element_type=F32))
        a3 = (jnp.dot(x_lo, w3b[ws, 0:HALF, :], preferred_element_type=F32)
              + jnp.dot(x_hi, w3b[ws, HALF:, :], preferred_element_type=F32))
        hbuf[blk % SCATTER_BUFS] = (a1 * jax.nn.sigmoid(a1) * a3).astype(BF16)

    def down(blk, ws):
        y = jnp.dot(hbuf[blk % SCATTER_BUFS], w2b[ws], preferred_element_type=F32)
        ybuf[blk % SCATTER_BUFS] = _pack_rows(y.astype(BF16).astype(F32))

    def for_each_weight_copy(gen, fn):
        ws = gen % 2
        expert = elist_ref[gen]
        for hbm, staging in ((w1_hbm, wst1), (w3_hbm, wst3), (w2_hbm, wst2)):
            fn(pltpu.make_async_copy(hbm.at[expert], staging.at[ws], sem_w.at[ws]))

    def switch_weights(gen):
        ws = gen % 2
        for_each_weight_copy(gen, lambda cp: cp.wait())
        w1b[ws] = wst1[ws].astype(BF16)
        w3b[ws] = wst3[ws].astype(BF16)
        w2b[ws] = wst2[ws].astype(BF16)

        @pl.when(gen + 1 < n_gen)
        def _():
            for_each_weight_copy(gen + 1, lambda cp: cp.start())

    def for_each_tail_piece(fn):
        def area_body(s, carry):
            used = used_ref[s]
            tail = AREA_CHUNKS - used
            row = (s * AREA_CHUNKS + used) * SUBLANES
            for bit in ZERO_BITS:
                take = (tail & bit) != 0

                @pl.when(take)
                def _(row=row, bit=bit):
                    fn(pltpu.make_async_copy(zbuf.at[pl.ds(0, bit * SUBLANES)],
                                             ys_hbm.at[pl.ds(pl.multiple_of(row, SUBLANES), bit * SUBLANES)],
                                             sem_zero))
                row = row + jnp.where(take, bit * SUBLANES, 0)
            return carry
        lax.fori_loop(0, n_areas, area_body, 0)

    @pl.when(n_gen > 0)
    def _():
        for_each_weight_copy(0, lambda cp: cp.start())

    zbuf[...] = jnp.zeros(zbuf.shape, U32)
    ybuf[...] = jnp.zeros(ybuf.shape, U32)
    dump = pltpu.make_async_copy(zbuf.at[pl.ds(0, DUMP_CHUNKS * SUBLANES)],
                                 ys_hbm.at[pl.ds(dump_row0, DUMP_CHUNKS * SUBLANES)], sem_dump)
    dump.start()
    dump.wait()
    for_each_tail_piece(lambda cp: cp.start())
    for sl in range(SCATTER_BUFS - SCATTER_LAG):
        for c in range(BLOCK_CHUNKS):
            out_copy(pl.ds(dump_row0 + (sl * BLOCK_CHUNKS + c) * SUBLANES, SUBLANES), sl, c).start()
    for k in range(lookahead):
        start_gather(k)

    @pl.when(total > 0)
    def _():
        switch_weights(0)
        wait_gather(0)
        rows = load_rows(0)
        issue_dmas(0)
        up_gate(0, rows, 0)

        def block_body(blk, carry):
            gen = bgen_ref[blk]
            gen_prev = bgen_ref[blk - 1]

            @pl.when(gen != gen_prev)
            def _():
                switch_weights(gen)

            wait_gather(blk)
            wait_scatter((blk - 1) % SCATTER_BUFS)
            rows = load_rows(blk)
            issue_dmas(blk)
            down(blk - 1, gen_prev % 2)
            up_gate(blk, rows, gen % 2)
            return carry

        lax.fori_loop(1, total, block_body, 0)
        wait_scatter((total - 1) % SCATTER_BUFS)
        down(total - 1, bgen_ref[total - 1] % 2)

    for k in range(SCATTER_LAG):
        start_scatter(total - SCATTER_LAG + k)
    for sl in range(SCATTER_BUFS):
        wait_scatter(sl)
    for k in range(lookahead):
        wait_gather(total + k)
    for_each_tail_piece(lambda cp: cp.wait())


def _moe_combine_kernel(h_ref, route_ref, ys_ref, l2g_ref, l2b_ref, y_ref):
    n = h_ref.shape[0]
    scope = n // (ys_ref.shape[0] // AREA_ROWS)
    s_iota = lax.broadcasted_iota(jnp.int32, (scope, AREA_ROWS), 1).astype(F32)
    for sc in range(n // scope):
        rows = slice(sc * scope, (sc + 1) * scope)
        route = route_ref[rows, :]
        slot1 = route[:, 0:1]
        slot2 = route[:, 1:2]
        gate1 = route[:, 2:3]
        gate2 = route[:, 3:4]
        gmat = (jnp.where(s_iota == slot1, gate1, 0.0) + jnp.where(s_iota == slot2, gate2, 0.0)).astype(BF16)
        y_lo, y_hi = _unpack_rows(ys_ref[sc * AREA_ROWS:(sc + 1) * AREA_ROWS, :])
        f = jnp.concatenate([jnp.dot(gmat, y_lo, preferred_element_type=F32),
                             jnp.dot(gmat, y_hi, preferred_element_type=F32)], axis=1)
        y_ref[rows, :] = _layer_norm(ALPHA * h_ref[rows, :] + f, l2g_ref[...], l2b_ref[...])


def _excl_cumsum(a, axis):
    return jnp.cumsum(a, axis=axis) - a


def _plan_blocks(cnt, q_max):
    n_areas = cnt.shape[0]
    c8 = (cnt + (SUBLANES - 1)) // SUBLANES
    used8 = jnp.sum(c8, axis=1)
    base8 = jnp.arange(n_areas, dtype=jnp.int32) * AREA_CHUNKS
    src_runs = (base8[:, None] + _excl_cumsum(c8, 1)).T
    len_runs = c8.T
    tot = jnp.sum(len_runs, axis=1)
    ptot = (tot + BLOCK_CHUNKS - 1) // BLOCK_CHUNKS * BLOCK_CHUNKS
    pstart = _excl_cumsum(ptot, 0)
    dst_runs = pstart[:, None] + _excl_cumsum(len_runs, 1)
    off = (src_runs - dst_runs).reshape(-1)
    diff = off - jnp.concatenate([jnp.zeros((1,), jnp.int32), off[:-1]])
    q = jnp.arange(q_max, dtype=jnp.int32)
    src = q + jnp.sum(jnp.where(dst_runs.reshape(1, -1) <= q[:, None], diff[None, :], 0), axis=1)
    valid = jnp.any((pstart[None, :] <= q[:, None]) & (q[:, None] < (pstart + tot)[None, :]), axis=1)
    zero_chunk = AREA_CHUNKS - 1
    ahead = jnp.full(((GATHER_BUFS - 1) * BLOCK_CHUNKS,), zero_chunk, jnp.int32)
    src_tab = jnp.concatenate([jnp.where(valid, src, zero_chunk).astype(jnp.int32), ahead])
    qv = jnp.arange(-SCATTER_LAG * BLOCK_CHUNKS, q_max, dtype=jnp.int32)
    dump = n_areas * AREA_CHUNKS + ((qv // BLOCK_CHUNKS) % SCATTER_BUFS) * BLOCK_CHUNKS + qv % BLOCK_CHUNKS
    lead = jnp.zeros((SCATTER_LAG * BLOCK_CHUNKS,), jnp.int32)
    dst_tab = jnp.where(jnp.concatenate([lead > 0, valid]), jnp.concatenate([lead, src]), dump).astype(jnp.int32)
    nblk = ptot // BLOCK_CHUNKS
    bstart = pstart // BLOCK_CHUNKS
    has = nblk > 0
    gen_of = jnp.cumsum(has.astype(jnp.int32)) - 1
    experts = jnp.arange(N_EXPERTS, dtype=jnp.int32)
    elist = jnp.sum(jnp.where(has[None, :] & (gen_of[None, :] == experts[:, None]), experts[None, :], 0), axis=1)
    blk = jnp.arange(q_max // BLOCK_CHUNKS, dtype=jnp.int32)
    owner = (bstart[None, :] <= blk[:, None]) & (blk[:, None] < (bstart + nblk)[None, :])
    bgen = jnp.sum(jnp.where(owner, gen_of[None, :], 0), axis=1)
    meta = jnp.stack([jnp.sum(nblk), jnp.sum(has.astype(jnp.int32))])
    return (src_tab, dst_tab, bgen.astype(jnp.int32), elist.astype(jnp.int32), meta.astype(jnp.int32),
            used8.astype(jnp.int32))


PREP_STEPS = 4


def _swap_head_halves(a):
    half = HEAD_DIM // 2
    pieces = []
    for g in range(a.shape[-1] // LANES):
        p = a[:, g * LANES:(g + 1) * LANES]
        lane = lax.broadcasted_iota(jnp.int32, p.shape, 1)
        pieces.append(jnp.where((lane & half) == 0, pltpu.roll(p, LANES - half, 1), pltpu.roll(p, half, 1)))
    return jnp.concatenate(pieces, axis=-1)


def _glu_blocks(a):
    lo = ATTN_DIM + 2 * KV_DIM
    pieces = []
    for cb in range(CONV_CH // LANES):
        pieces.append(a[:, lo + cb * LANES:lo + (cb + 1) * LANES])
        pieces.append(a[:, lo + CONV_CH + cb * LANES:lo + CONV_CH + (cb + 1) * LANES])
    return jnp.concatenate(pieces, axis=-1)


def _prep_weights_kernel(w_in_ref, b_in_ref, w_out_ref, w_in_b_ref, w_sw_ref, w_glu_ref, w_out_b_ref,
                         b_sw_ref, b_glu_ref):
    qk = ATTN_DIM + KV_DIM
    w = w_in_ref[...]
    w_in_b_ref[...] = w.astype(BF16)
    w_sw_ref[...] = _swap_head_halves(w[:, :qk]).astype(BF16)
    w_glu_ref[...] = _glu_blocks(w).astype(BF16)
    w_out_b_ref[...] = w_out_ref[...].astype(BF16)
    b = jnp.broadcast_to(b_in_ref[...], (SUBLANES, IN_DIM))
    b_sw_ref[...] = _swap_head_halves(b[:, :qk])[0:1]
    b_glu_ref[...] = _glu_blocks(b)[0:1]


def _rope_tables(pos):
    half = HEAD_DIM // 2
    inv = (ROPE_THETA ** (-np.arange(half, dtype=np.float64) / half)).astype(np.float32)
    ang = (np.asarray(pos, np.float32)[:, None] * inv).astype(np.float64)
    cos = np.cos(ang)
    sin = np.sin(ang)
    reps = LANES // HEAD_DIM
    cos_t = np.concatenate([cos, cos] * reps, axis=1)
    sin_t = np.concatenate([-sin, sin] * reps, axis=1)
    q_scale = ATTN_SCALE * LOG2E
    return jnp.asarray(np.concatenate([cos_t, sin_t, cos_t * q_scale, sin_t * q_scale], axis=1), F32)


def _full(shape):
    nd = len(shape)
    return pl.BlockSpec(shape, lambda *_: (0,) * nd)


def kernel(x_prompt, x_sample, cache_k, cache_v, state_conv, w_in, b_in, sinks, conv_w, conv_b, ln_conv_g,
           ln_conv_b, w_out, b_out, ln1_g, ln1_b, w_router_group, w_router_expert, w1, w3, w2, ln2_g, ln2_b):
    assert w_in.shape[0] == DEPTH
    bsz, seq, _ = x_prompt.shape
    dec_b, dec_t, _ = x_sample.shape
    assert dec_t == 1 and seq % TILE_Q == 0 and dec_b % SAMPLE_TB == 0 and 2 * dec_b + 7 * N_EXPERTS <= AREA_ROWS
    assert (bsz * seq) % COMBINE_TILE == 0
    n_prompt = bsz * seq
    tiles = seq // TILE_Q
    spt = TILE_Q // PROMPT_SCOPE
    p_areas = n_prompt // PROMPT_SCOPE
    n_areas = p_areas + spt
    q_max = (n_areas * AREA_CHUNKS + N_EXPERTS * (BLOCK_CHUNKS - 1) + BLOCK_CHUNKS - 1) // BLOCK_CHUNKS * BLOCK_CHUNKS

    row = lambda a: a.reshape(1, -1)
    cparams = lambda sem: pltpu.CompilerParams(dimension_semantics=sem, vmem_limit_bytes=VMEM_LIMIT)

    qk = ATTN_DIM + KV_DIM
    wrows = D_MODEL // PREP_STEPS
    wspec = lambda cols: pl.BlockSpec((wrows, cols), lambda t: (t, 0))
    w_in_b, w_sw_b, w_glu_b, w_out_b, b_sw, b_glu = pl.pallas_call(
        _prep_weights_kernel,
        grid=(PREP_STEPS,),
        in_specs=[wspec(IN_DIM), _full((1, IN_DIM)), wspec(D_MODEL)],
        out_specs=[wspec(IN_DIM), wspec(qk), wspec(2 * CONV_CH), wspec(D_MODEL),
                   _full((1, qk)), _full((1, 2 * CONV_CH))],
        out_shape=[jax.ShapeDtypeStruct((D_MODEL, IN_DIM), BF16), jax.ShapeDtypeStruct((D_MODEL, qk), BF16),
                   jax.ShapeDtypeStruct((D_MODEL, 2 * CONV_CH), BF16), jax.ShapeDtypeStruct((D_MODEL, D_MODEL), BF16),
                   jax.ShapeDtypeStruct((1, qk), F32), jax.ShapeDtypeStruct((1, 2 * CONV_CH), F32)],
        compiler_params=cparams(("arbitrary",)),
        name="prep_weights",
    )(w_in[0], row(b_in[0]), w_out[0])
    w_r = jnp.concatenate([w_router_group[0], w_router_expert[0].reshape(D_MODEL, N_EXPERTS),
                           jnp.zeros((D_MODEL, LANES - ELANE0 - N_EXPERTS), F32)], axis=1).astype(BF16)
    shared = (w_in_b, row(b_in[0]), w_sw_b, b_sw, conv_w[0], row(conv_b[0]),
              row(ln_conv_g[0]), row(ln_conv_b[0]), w_out_b, row(b_out[0]), row(ln1_g[0]), row(ln1_b[0]), w_r)
    shared_specs = [_full(a.shape) for a in shared]
    glu = (w_glu_b, b_glu)
    glu_specs = [_full(a.shape) for a in glu]
    assert len(glu) + len(shared) == N_PROMPT_WEIGHTS
    smem = pl.BlockSpec(memory_space=pltpu.SMEM)
    rope_p = _rope_tables(np.arange(seq))
    rope_s = _rope_tables(PAST_LEN + np.arange(dec_t))

    ck = cache_k[0].reshape(dec_b, WINDOW, KV_DIM)
    cv = cache_v[0].reshape(dec_b, WINDOW, KV_DIM)
    tb = SAMPLE_TB
    conv_state_spec = pl.BlockSpec((None, tb, CONV_WIDTH - 1, CONV_CH), lambda t: (0, t, 0, 0))
    h_s, nk_s, nv_s, nc_s, xs_s, route_s, cnt_s = pl.pallas_call(
        _mix_sample_kernel,
        grid=(dec_b // tb,),
        in_specs=[smem,
                  pl.BlockSpec((tb, None, D_MODEL), lambda t: (t, 0, 0)),
                  _full(rope_s.shape),
                  pl.BlockSpec((tb, WINDOW, KV_DIM), lambda t: (t, 0, 0)),
                  pl.BlockSpec((tb, WINDOW, KV_DIM), lambda t: (t, 0, 0)),
                  conv_state_spec] + shared_specs,
        out_specs=[pl.BlockSpec((tb, D_MODEL), lambda t: (t, 0)),
                   pl.BlockSpec((tb, WINDOW, KV_DIM), lambda t: (t, 0, 0)),
                   pl.BlockSpec((tb, WINDOW, KV_DIM), lambda t: (t, 0, 0)),
                   conv_state_spec,
                   _full((AREA_ROWS, HALF)),
                   _full((dec_b, LANES)),
                   _full((1, SUBLANES, LANES))],
        out_shape=[jax.ShapeDtypeStruct((dec_b, D_MODEL), F32),
                   jax.ShapeDtypeStruct((dec_b, WINDOW, KV_DIM), F32),
                   jax.ShapeDtypeStruct((dec_b, WINDOW, KV_DIM), F32),
                   jax.ShapeDtypeStruct((DEPTH, dec_b, CONV_WIDTH - 1, CONV_CH), F32),
                   jax.ShapeDtypeStruct((AREA_ROWS, HALF), U32),
                   jax.ShapeDtypeStruct((dec_b, LANES), F32),
                   jax.ShapeDtypeStruct((1, SUBLANES, LANES), F32)],
        scratch_shapes=[pltpu.VMEM((dec_b, D_MODEL), F32)],
        compiler_params=cparams(("arbitrary",)),
        name="mix_sample",
    )(sinks[0], x_sample, rope_s, ck, cv, state_conv, *shared)

    n_real = bsz * tiles
    real = lambda t: jnp.minimum(t, n_real - 1)
    h_p, xs, route_p, cnt_p, nk_p, nv_p, nc_p = pl.pallas_call(
        functools.partial(_mix_prompt_kernel, tiles=tiles),
        grid=(n_real + 1,),
        in_specs=[smem,
                  pl.BlockSpec((1, TILE_Q, D_MODEL), lambda t: (real(t) // tiles, real(t) % tiles, 0)),
                  pl.BlockSpec((TILE_Q, 4 * LANES), lambda t: (real(t) % tiles, 0)),
                  _full((AREA_ROWS, HALF))] + glu_specs + shared_specs,
        out_specs=[pl.BlockSpec((1, TILE_Q, D_MODEL), lambda t: (real(t) // tiles, real(t) % tiles, 0)),
                   pl.BlockSpec((spt * AREA_ROWS, HALF), lambda t: (t, 0)),
                   pl.BlockSpec((TILE_Q, LANES), lambda t: (real(t), 0)),
                   pl.BlockSpec((spt, SUBLANES, LANES), lambda t: (t, 0, 0)),
                   pl.BlockSpec((1, WINDOW, KV_DIM), lambda t: (real(t) // tiles, 0, 0)),
                   pl.BlockSpec((1, WINDOW, KV_DIM), lambda t: (real(t) // tiles, 0, 0)),
                   pl.BlockSpec((1, CONV_WIDTH - 1, CONV_CH), lambda t: (real(t) // tiles, 0, 0))],
        out_shape=[jax.ShapeDtypeStruct((bsz, seq, D_MODEL), F32),
                   jax.ShapeDtypeStruct((n_areas * AREA_ROWS, HALF), U32),
                   jax.ShapeDtypeStruct((n_prompt, LANES), F32),
                   jax.ShapeDtypeStruct((n_areas, SUBLANES, LANES), F32),
                   jax.ShapeDtypeStruct((bsz, WINDOW, KV_DIM), F32),
                   jax.ShapeDtypeStruct((bsz, WINDOW, KV_DIM), F32),
                   jax.ShapeDtypeStruct((bsz, CONV_WIDTH - 1, CONV_CH), F32)],
        scratch_shapes=[pltpu.VMEM((WINDOW, KV_DIM), F32), pltpu.VMEM((WINDOW, KV_DIM), F32),
                        pltpu.VMEM((TILE_Q + CONV_PRE, CONV_CH), F32),
                        pltpu.VMEM((SUBLANES - 1, TILE_Q + CONV_SHIFT_EXTRA, CONV_CH), F32)],
        compiler_params=cparams(("arbitrary",)),
        name="mix_prompt",
    )(sinks[0], x_prompt, rope_p, xs_s, *glu, *shared)

    cnt = jnp.concatenate([cnt_p[:p_areas, 0], cnt_s[:, 0], cnt_p[p_areas + 1:, 0]], axis=0)
    cnt = cnt[:, ELANE0:ELANE0 + N_EXPERTS].astype(jnp.int32)
    plan = _plan_blocks(cnt, q_max)
    hbm = pl.BlockSpec(memory_space=pl.ANY)
    up_shape, down_shape = (2, D_MODEL, D_FF_EXPERT), (2, D_FF_EXPERT, D_MODEL)
    ys = pl.pallas_call(
        _moe_ffn_kernel,
        grid_spec=pltpu.PrefetchScalarGridSpec(
            num_scalar_prefetch=len(plan),
            grid=(1,),
            in_specs=[hbm, hbm, hbm, hbm],
            out_specs=hbm,
            scratch_shapes=[pltpu.VMEM((GATHER_BUFS, BLOCK_ROWS, HALF), U32),
                            pltpu.VMEM((SCATTER_BUFS, BLOCK_ROWS, HALF), U32),
                            pltpu.VMEM((SCATTER_BUFS, BLOCK_ROWS, D_FF_EXPERT), BF16),
                            pltpu.VMEM((ZERO_ROWS, HALF), U32),
                            pltpu.VMEM(up_shape, F32), pltpu.VMEM(up_shape, F32), pltpu.VMEM(down_shape, F32),
                            pltpu.VMEM(up_shape, BF16), pltpu.VMEM(up_shape, BF16), pltpu.VMEM(down_shape, BF16),
                            pltpu.SemaphoreType.DMA((GATHER_BUFS,)), pltpu.SemaphoreType.DMA((SCATTER_BUFS,)),
                            pltpu.SemaphoreType.DMA(()), pltpu.SemaphoreType.DMA(()),
                            pltpu.SemaphoreType.DMA((2,))]),
        out_shape=jax.ShapeDtypeStruct((n_areas * AREA_ROWS + DUMP_CHUNKS * SUBLANES, HALF), U32),
        compiler_params=cparams(("arbitrary",)),
        name="moe_ffn",
    )(*plan, xs, w1[0], w3[0], w2[0])

    def combine(h2d, route, first_area, scope, tile):
        n = h2d.shape[0]
        apt = tile // scope
        return pl.pallas_call(
            _moe_combine_kernel,
            grid=(n // tile,),
            in_specs=[pl.BlockSpec((tile, D_MODEL), lambda t: (t, 0)),
                      pl.BlockSpec((tile, LANES), lambda t: (t, 0)),
                      pl.BlockSpec((apt * AREA_ROWS, HALF), lambda t: (first_area // apt + t, 0)),
                      _full((1, D_MODEL)), _full((1, D_MODEL))],
            out_specs=pl.BlockSpec((tile, D_MODEL), lambda t: (t, 0)),
            out_shape=jax.ShapeDtypeStruct((n, D_MODEL), F32),
            compiler_params=cparams(("arbitrary",)),
            name="moe_combine",
        )(h2d, route, ys, row(ln2_g[0]), row(ln2_b[0]))

    y_p = combine(h_p.reshape(n_prompt, D_MODEL), route_p, 0, PROMPT_SCOPE, COMBINE_TILE)
    y_s = combine(h_s, route_s, p_areas, dec_b, dec_b)

    kv_shape = lambda n: (DEPTH, n, WINDOW, N_KV_HEADS, HEAD_DIM)
    return (y_p.reshape(bsz, seq, D_MODEL), y_s.reshape(dec_b, dec_t, D_MODEL),
            nk_p.reshape(kv_shape(bsz)), nv_p.reshape(kv_shape(bsz)), nc_p[None],
            nk_s.reshape(kv_shape(dec_b)), nv_s.reshape(kv_shape(dec_b)), nc_s)
```

```python
import functools

import jax
import jax.numpy as jnp
import numpy as np
from jax import lax
from jax.experimental import pallas as pl
from jax.experimental.pallas import tpu as pltpu

F32 = jnp.float32
BF16 = jnp.bfloat16
U32 = jnp.uint32

D_MODEL = 1024
N_HEADS = 8
N_KV_HEADS = 2
HEAD_DIM = 64
Q_GROUP = N_HEADS // N_KV_HEADS
ATTN_DIM = N_HEADS * HEAD_DIM
KV_DIM = N_KV_HEADS * HEAD_DIM
WINDOW = 128
ROPE_THETA = 10000.0
ATTN_SCALE = HEAD_DIM ** -0.5
CONV_CH = D_MODEL - ATTN_DIM
CONV_WIDTH = 31
IN_DIM = ATTN_DIM + 2 * KV_DIM + 2 * CONV_CH
N_GROUPS = 4
EXPERTS_PER_GROUP = 8
N_EXPERTS = N_GROUPS * EXPERTS_PER_GROUP
D_FF_EXPERT = 256
LN_EPS = 1e-5
NEG_INF = -1e30
DEPTH = 1
ALPHA = (2.0 * DEPTH) ** 0.25
PAST_LEN = 16384
LOG2E = 1.4426950408889634

LANES = 128
SUBLANES = 8
HALF = D_MODEL // 2
ELANE0 = N_GROUPS
PROMPT_SCOPE = 256
AREA_ROWS = -(-(2 * PROMPT_SCOPE + (SUBLANES - 1) * N_EXPERTS) // LANES) * LANES
AREA_CHUNKS = AREA_ROWS // SUBLANES
BLOCK_CHUNKS = 32
BLOCK_ROWS = BLOCK_CHUNKS * SUBLANES
TILE_Q = 1024
COMBINE_TILE = 1024
CONV_PRE = 32
CONV_SHIFT_EXTRA = 24
SAMPLE_TB = 32
SAMPLE_GRP = 8
VMEM_LIMIT = 58 * 1024 * 1024


def _layer_norm(x, g, b):
    mu = jnp.mean(x, axis=-1, keepdims=True)
    xc = x - mu
    var = jnp.mean(xc * xc, axis=-1, keepdims=True)
    return xc * lax.rsqrt(var + LN_EPS) * g + b


def _rope(x, x_swapped, cos, sin_signed):
    pieces = [x[:, g * LANES:(g + 1) * LANES] * cos + x_swapped[:, g * LANES:(g + 1) * LANES] * sin_signed
              for g in range(x.shape[-1] // LANES)]
    return pieces[0] if len(pieces) == 1 else jnp.concatenate(pieces, axis=-1)


def _in_proj_glu(xb, w_in_ref, b_in_ref):
    lo, hi = ATTN_DIM + 2 * KV_DIM, IN_DIM
    cacg = jnp.dot(xb, w_in_ref[:, lo:hi], preferred_element_type=F32) + b_in_ref[:, lo:hi]
    return cacg[:, :CONV_CH] * jax.nn.sigmoid(cacg[:, CONV_CH:])


def _in_proj_qkv(xb, w_in_ref, b_in_ref, w_sw_ref, b_sw_ref, rope):
    o_v, o_e = ATTN_DIM + KV_DIM, ATTN_DIM + 2 * KV_DIM
    qkv = jnp.dot(xb, w_in_ref[:, 0:o_e], preferred_element_type=F32) + b_in_ref[:, 0:o_e]
    sw = jnp.dot(xb, w_sw_ref[...], preferred_element_type=F32) + b_sw_ref[...]
    cos, sin_signed = rope[:, 0:LANES], rope[:, LANES:]
    q_scale = ATTN_SCALE * LOG2E
    k = _rope(qkv[:, ATTN_DIM:o_v], sw[:, ATTN_DIM:], cos, sin_signed)
    q = _rope(qkv[:, :ATTN_DIM], sw[:, :ATTN_DIM], cos * q_scale, sin_signed * q_scale)
    return q, k, qkv[:, o_v:]


def _pack_rows(v):
    lo = lax.bitcast_convert_type(v[:, :HALF], U32)
    hi = lax.bitcast_convert_type(v[:, HALF:], U32)
    return (lo >> 16) | (hi & jnp.uint32(0xFFFF0000))


def _unpack_rows(w):
    lo = lax.bitcast_convert_type(w << 16, F32).astype(BF16)
    hi = lax.bitcast_convert_type(w & jnp.uint32(0xFFFF0000), F32).astype(BF16)
    return lo, hi


def _drain(stages):
    try:
        while True:
            next(stages)
    except StopIteration as stop:
        return stop.value


def _round_robin(*stage_generators):
    live = list(stage_generators)
    while live:
        for stages in list(live):
            try:
                next(stages)
            except StopIteration:
                live.remove(stages)


def _route_and_sort_stages(h, w_r):
    n = h.shape[0]
    hb = h.astype(BF16)
    logits = jnp.dot(hb, w_r, preferred_element_type=F32)
    yield
    lane = lax.broadcasted_iota(jnp.int32, (n, LANES), 1)
    lane_f = lane.astype(F32)
    is_g = lane < N_GROUPS
    lg = jnp.where(is_g, logits, -jnp.inf)
    gmax = jnp.max(lg, axis=-1, keepdims=True)
    yield
    gidx = jnp.min(jnp.where(lg == gmax, lane_f, float(LANES)), axis=-1, keepdims=True)
    p_g = 1.0 / jnp.sum(jnp.where(is_g, jnp.exp(logits - gmax), 0.0), axis=-1, keepdims=True)
    yield
    egrp = ((lane - ELANE0) >> (EXPERTS_PER_GROUP.bit_length() - 1)).astype(F32)
    emask = (lane >= ELANE0) & (lane < ELANE0 + N_EXPERTS) & (egrp == gidx)
    sel = jnp.where(emask, logits, -jnp.inf)
    v1 = jnp.max(sel, axis=-1, keepdims=True)
    yield
    i1 = jnp.min(jnp.where(sel == v1, lane_f, float(LANES)), axis=-1, keepdims=True)
    yield
    sel2 = jnp.where(lane_f == i1, -jnp.inf, sel)
    v2 = jnp.max(sel2, axis=-1, keepdims=True)
    yield
    i2 = jnp.min(jnp.where(sel2 == v2, lane_f, float(LANES)), axis=-1, keepdims=True)
    t = jnp.exp(v2 - v1)
    gate1 = p_g / (1.0 + t)
    gate2 = p_g * t / (1.0 + t)
    pick1 = lane_f == i1
    pick2 = lane_f == i2
    member = jnp.where(pick1 | pick2, 1.0, 0.0)
    yield

    r_i = lax.broadcasted_iota(jnp.int32, (n, n), 0)
    c_i = lax.broadcasted_iota(jnp.int32, (n, n), 1)
    lower = jnp.where(c_i < r_i, 1.0, 0.0).astype(BF16)
    e_r = lax.broadcasted_iota(jnp.int32, (LANES, LANES), 0)
    e_c = lax.broadcasted_iota(jnp.int32, (LANES, LANES), 1)
    upper = jnp.where(e_r < e_c, 1.0, 0.0).astype(BF16)
    before = jnp.dot(lower, member.astype(BF16), preferred_element_type=F32)
    cnt = jnp.sum(member, axis=0, keepdims=True)
    c8 = jnp.floor((cnt + (SUBLANES - 1)) * (1.0 / SUBLANES))
    c8b = jnp.broadcast_to(c8, (SUBLANES, LANES)).astype(BF16)
    off8 = jnp.dot(c8b, upper, preferred_element_type=F32)[0:1]
    yield
    slot_all = before + off8 * float(SUBLANES)
    slot1 = jnp.sum(jnp.where(pick1, slot_all, 0.0), axis=-1, keepdims=True)
    slot2 = jnp.sum(jnp.where(pick2, slot_all, 0.0), axis=-1, keepdims=True)
    yield
    s_iota = lax.broadcasted_iota(jnp.int32, (n, AREA_ROWS), 1).astype(F32)
    perm_t = jnp.where((s_iota == slot1) | (s_iota == slot2), 1.0, 0.0).astype(BF16)
    sorted_rows = lax.dot_general(perm_t, hb, (((0,), (0,)), ((), ())),
                                  preferred_element_type=F32)
    yield
    route = jnp.where(lane == 0, slot1, jnp.where(lane == 1, slot2, jnp.where(lane == 2, gate1,
                      jnp.where(lane == 3, gate2, 0.0))))
    return route, _pack_rows(sorted_rows), cnt


def _softmax_rows_with_sink(s, sink, extra=None):
    m = jnp.maximum(jnp.max(s, axis=-1, keepdims=True), sink)
    if extra is not None:
        m = jnp.maximum(m, extra)
    e = jnp.exp2(s - m)
    den = jnp.sum(e, axis=-1, keepdims=True) + jnp.exp2(sink - m)
    if extra is not None:
        ee = jnp.exp2(extra - m)
        den = den + ee
        return e / den, ee / den
    return e / den, None


def _mix_out_stages(x, attn, conv_pre, lcg, lcb, w_out_ref, b_out, l1g, l1b):
    conv = _layer_norm(conv_pre, lcg, lcb)
    conv = conv * jax.nn.sigmoid(conv)
    yield
    mix = (jnp.dot(attn.astype(BF16), w_out_ref[0:ATTN_DIM, :], preferred_element_type=F32)
           + jnp.dot(conv.astype(BF16), w_out_ref[ATTN_DIM:, :], preferred_element_type=F32) + b_out)
    yield
    h = _layer_norm(ALPHA * x + mix, l1g, l1b)
    yield
    return h


N_PROMPT_WEIGHTS = 15


def _mix_prompt_kernel(sinks_ref, x_ref, rope_ref, xs_s_ref, *rest, tiles):
    _, xs_ref, _, cnt_ref = rest[N_PROMPT_WEIGHTS:N_PROMPT_WEIGHTS + 4]
    step = pl.program_id(0)
    n_real = pl.num_programs(0) - 1

    @pl.when(step < n_real)
    def _():
        _mix_prompt_tile(sinks_ref, x_ref, rope_ref, *rest, i=step % tiles)

    @pl.when(step == n_real)
    def _():
        xs_ref[0:AREA_ROWS, :] = xs_s_ref[...]
        xs_ref[AREA_ROWS:, :] = jnp.zeros((xs_ref.shape[0] - AREA_ROWS, HALF), U32)
        cnt_ref[...] = jnp.zeros(cnt_ref.shape, F32)


def _mix_prompt_tile(sinks_ref, x_ref, rope_ref, w_glu_ref, b_glu_ref,
                     w_in_ref, b_in_ref, w_sw_ref, b_sw_ref, conv_w_ref, conv_b_ref,
                     lcg_ref, lcb_ref, w_out_ref, b_out_ref, l1g_ref, l1b_ref, w_r_ref,
                     h_ref, xs_ref, route_ref, cnt_ref, nk_ref, nv_ref, nc_ref,
                     kprev, vprev, u_scr, u_shift, *, i):
    tq = x_ref.shape[1]
    pre = CONV_PRE
    n_sc = tq // PROMPT_SCOPE

    @pl.when(i == 0)
    def _():
        kprev[...] = jnp.zeros_like(kprev)
        vprev[...] = jnp.zeros_like(vprev)
        u_scr[0:pre, :] = jnp.zeros((pre, CONV_CH), F32)

    x = x_ref[0]
    xb = x.astype(BF16)

    def glu_block(cb):
        cols = slice(2 * LANES * cb, 2 * LANES * (cb + 1))
        vg = jnp.dot(xb, w_glu_ref[:, cols], preferred_element_type=F32) + b_glu_ref[:, cols]
        return vg[:, :LANES] * jax.nn.sigmoid(vg[:, LANES:])

    n_cb = CONV_CH // LANES
    acc_blocks = [[None] * n_cb for _ in range(n_sc)]
    u_next = glu_block(0)
    for cb in range(n_cb):
        lanes = slice(LANES * cb, LANES * (cb + 1))
        u_scr[pre:pre + tq, lanes] = u_next
        if cb + 1 < n_cb:
            u_next = glu_block(cb + 1)
        for r in range(1, SUBLANES):
            u_shift[r - 1] = u_scr[r:r + tq + CONV_SHIFT_EXTRA, lanes]
        for sc in range(n_sc):
            base = sc * PROMPT_SCOPE
            acc = jnp.zeros((PROMPT_SCOPE, LANES), F32)
            for j in range(CONV_WIDTH):
                a, r = divmod(pre - (CONV_WIDTH - 1) + j, SUBLANES)
                start = base + a * SUBLANES
                src = (u_scr[start:start + PROMPT_SCOPE, lanes] if r == 0
                       else u_shift[r - 1, start:start + PROMPT_SCOPE, :])
                acc = acc + src * conv_w_ref[j:j + 1, lanes]
            acc_blocks[sc][cb] = acc
    conv_pre = [jnp.concatenate(blocks, axis=1) + conv_b_ref[...] for blocks in acc_blocks]

    nc_ref[0] = u_scr[pre + tq - (CONV_WIDTH - 1):pre + tq, :]
    tail = u_scr[tq:tq + pre, :]
    u_scr[0:pre, :] = tail

    q, k, v = _in_proj_qkv(xb, w_in_ref, b_in_ref, w_sw_ref, b_sw_ref, rope_ref[...])
    qb = q.astype(BF16)
    kb = jnp.concatenate([kprev[...], k], axis=0).astype(BF16)
    vb = jnp.concatenate([vprev[...], v], axis=0).astype(BF16)
    row = lax.broadcasted_iota(jnp.int32, (WINDOW, 2 * WINDOW), 0)
    col = lax.broadcasted_iota(jnp.int32, (WINDOW, 2 * WINDOW), 1)
    dist = row + WINDOW - col
    band = (dist >= 0) & (dist < WINDOW)
    first_col = jnp.where(i > 0, 0, WINDOW)
    kprev[...] = k[tq - WINDOW:, :]
    vprev[...] = v[tq - WINDOW:, :]
    nk_ref[0] = k[tq - WINDOW:, :]
    nv_ref[0] = v[tq - WINDOW:, :]
    blocks_per_scope = PROMPT_SCOPE // WINDOW
    attn_blocks = [None] * (tq // WINDOW)

    def scope_attention(sc):
        for j in range(sc * blocks_per_scope, (sc + 1) * blocks_per_scope):
            msk = (band & (col >= first_col)) if j == 0 else band
            heads = [None] * N_HEADS
            for c in range(N_KV_HEADS):
                kc = kb[j * WINDOW:(j + 2) * WINDOW, c * HEAD_DIM:(c + 1) * HEAD_DIM]
                vc = vb[j * WINDOW:(j + 2) * WINDOW, c * HEAD_DIM:(c + 1) * HEAD_DIM]
                qc = jnp.concatenate(
                    [qb[j * WINDOW:(j + 1) * WINDOW, (c * Q_GROUP + g) * HEAD_DIM:(c * Q_GROUP + g + 1) * HEAD_DIM]
                     for g in range(Q_GROUP)], axis=0)
                s = lax.dot_general(qc, kc, (((1,), (1,)), ((), ())), preferred_element_type=F32)
                yield
                probs = []
                for g in range(Q_GROUP):
                    sg = jnp.where(msk, s[g * WINDOW:(g + 1) * WINDOW], NEG_INF)
                    p, _ = _softmax_rows_with_sink(sg, sinks_ref[c * Q_GROUP + g] * LOG2E)
                    probs.append(p.astype(BF16))
                    if g % 2 == 1:
                        yield
                o = jnp.dot(jnp.concatenate(probs, axis=0), vc, preferred_element_type=F32)
                for g in range(Q_GROUP):
                    heads[c * Q_GROUP + g] = o[g * WINDOW:(g + 1) * WINDOW]
                yield
            attn_blocks[j] = jnp.concatenate(heads, axis=1)

    def scope_tail(sc):
        rows = slice(sc * PROMPT_SCOPE, (sc + 1) * PROMPT_SCOPE)
        attn = jnp.concatenate(attn_blocks[sc * blocks_per_scope:(sc + 1) * blocks_per_scope], axis=0)
        h = yield from _mix_out_stages(x[rows], attn, conv_pre[sc], lcg_ref[...], lcb_ref[...],
                                       w_out_ref, b_out_ref[...], l1g_ref[...], l1b_ref[...])
        route, area, cnt = yield from _route_and_sort_stages(h, w_r_ref[...])
        h_ref[0, rows, :] = h
        route_ref[rows, :] = route
        xs_ref[sc * AREA_ROWS:(sc + 1) * AREA_ROWS, :] = area
        cnt_ref[sc] = jnp.broadcast_to(cnt, (SUBLANES, LANES))

    _round_robin(*[scope_attention(sc) for sc in range(n_sc)])
    _round_robin(*[scope_tail(sc) for sc in range(n_sc)])


def _mix_sample_kernel(sinks_ref, x_ref, rope_ref, ck_ref, cv_ref, st_ref, w_in_ref, b_in_ref, w_sw_ref, b_sw_ref,
                       conv_w_ref, conv_b_ref, lcg_ref, lcb_ref, w_out_ref, b_out_ref, l1g_ref, l1b_ref,
                       w_r_ref,
                       h_ref, nk_ref, nv_ref, nc_ref, xs_ref, route_ref, cnt_ref,
                       h_scr):
    tb = x_ref.shape[0]
    n_tok = h_scr.shape[0]
    step = pl.program_id(0)
    x = x_ref[...]
    xb = x.astype(BF16)
    u = _in_proj_glu(xb, w_in_ref, b_in_ref)
    q, k_new, v_new = _in_proj_qkv(xb, w_in_ref, b_in_ref, w_sw_ref, b_sw_ref, rope_ref[...])

    qb = q.astype(BF16)
    knb = k_new.astype(BF16).astype(F32)
    vnb = v_new.astype(BF16).astype(F32)
    gq = SAMPLE_GRP * Q_GROUP
    nkeys = SAMPLE_GRP * WINDOW
    r_i = lax.broadcasted_iota(jnp.int32, (gq, nkeys), 0)
    c_i = lax.broadcasted_iota(jnp.int32, (gq, nkeys), 1)
    window_shift, grp_shift = WINDOW.bit_length() - 1, SAMPLE_GRP.bit_length() - 1
    msk = ((c_i >> window_shift) == (r_i & (SAMPLE_GRP - 1))) & ((c_i & (WINDOW - 1)) >= 1)
    r_col = lax.broadcasted_iota(jnp.int32, (gq, 1), 0) >> grp_shift
    attn_groups = [None] * (tb // SAMPLE_GRP)

    def group_attention(grp):
        r0 = grp * SAMPLE_GRP
        kc_all = ck_ref[r0:r0 + SAMPLE_GRP].reshape(nkeys, KV_DIM).astype(BF16)
        vc_all = cv_ref[r0:r0 + SAMPLE_GRP].reshape(nkeys, KV_DIM).astype(BF16)
        yield
        heads = [None] * N_HEADS
        for c in range(N_KV_HEADS):
            lanes = slice(c * HEAD_DIM, (c + 1) * HEAD_DIM)
            qc = jnp.concatenate(
                [qb[r0:r0 + SAMPLE_GRP, (c * Q_GROUP + g) * HEAD_DIM:(c * Q_GROUP + g + 1) * HEAD_DIM]
                 for g in range(Q_GROUP)], axis=0)
            s = lax.dot_general(qc, kc_all[:, lanes], (((1,), (1,)), ((), ())), preferred_element_type=F32)
            s = jnp.where(msk, s, NEG_INF)
            kn = jnp.concatenate([knb[r0:r0 + SAMPLE_GRP, lanes]] * Q_GROUP, axis=0)
            vn = jnp.concatenate([vnb[r0:r0 + SAMPLE_GRP, lanes]] * Q_GROUP, axis=0)
            s_new = jnp.sum(qc.astype(F32) * kn, axis=-1, keepdims=True)
            sink = jnp.zeros((gq, 1), F32)
            for g in range(Q_GROUP):
                sink = jnp.where(r_col == g, sinks_ref[c * Q_GROUP + g] * LOG2E, sink)
            yield
            p, p_new = _softmax_rows_with_sink(s, sink, s_new)
            yield
            o = (jnp.dot(p.astype(BF16), vc_all[:, lanes], preferred_element_type=F32)
                 + p_new.astype(BF16).astype(F32) * vn)
            for g in range(Q_GROUP):
                heads[c * Q_GROUP + g] = o[g * SAMPLE_GRP:(g + 1) * SAMPLE_GRP]
            yield
        attn_groups[grp] = jnp.concatenate(heads, axis=1)

    _round_robin(*[group_attention(grp) for grp in range(len(attn_groups))])
    attn = jnp.concatenate(attn_groups, axis=0)

    st = st_ref[...]
    acc = jnp.sum(st * conv_w_ref[0:CONV_WIDTH - 1, :][None], axis=1)
    conv_pre = acc + u * conv_w_ref[CONV_WIDTH - 1:CONV_WIDTH, :] + conv_b_ref[...]

    nk_ref[:, 0:WINDOW - 1, :] = ck_ref[:, 1:WINDOW, :]
    nv_ref[:, 0:WINDOW - 1, :] = cv_ref[:, 1:WINDOW, :]
    nc_ref[:, 0:CONV_WIDTH - 2, :] = st_ref[:, 1:CONV_WIDTH - 1, :]
    for b in range(tb):
        nk_ref[b, WINDOW - 1:WINDOW, :] = k_new[b:b + 1, :]
        nv_ref[b, WINDOW - 1:WINDOW, :] = v_new[b:b + 1, :]
        nc_ref[b, CONV_WIDTH - 2:CONV_WIDTH - 1, :] = u[b:b + 1, :]

    h = _drain(_mix_out_stages(x, attn, conv_pre, lcg_ref[...], lcb_ref[...], w_out_ref, b_out_ref[...],
                               l1g_ref[...], l1b_ref[...]))
    h_ref[...] = h
    h_scr[pl.ds(pl.multiple_of(step * tb, tb), tb), :] = h

    @pl.when(step == pl.num_programs(0) - 1)
    def _():
        route, area, cnt = _drain(_route_and_sort_stages(h_scr[...], w_r_ref[...]))
        route_ref[...] = route
        xs_ref[...] = area
        cnt_ref[0] = jnp.broadcast_to(cnt, (SUBLANES, LANES))


GATHER_BUFS = 8
SCATTER_BUFS = 8
SCATTER_LAG = 2
ZERO_BITS = (64, 32, 16, 8, 4, 2, 1)
DUMP_CHUNKS = SCATTER_BUFS * BLOCK_CHUNKS
ZERO_ROWS = max(ZERO_BITS[0], DUMP_CHUNKS) * SUBLANES


def _moe_ffn_kernel(src_ref, dst_ref, bgen_ref, elist_ref, meta_ref, used_ref,
                    xs_hbm, w1_hbm, w3_hbm, w2_hbm,
                    ys_hbm,
                    xbuf, ybuf, hbuf, zbuf, wst1, wst3, wst2, w1b, w3b, w2b,
                    sem_in, sem_out, sem_zero, sem_dump, sem_w):
    n_areas = used_ref.shape[0]
    total = meta_ref[0]
    n_gen = meta_ref[1]
    dump_row0 = n_areas * AREA_ROWS
    lookahead = GATHER_BUFS - 1
    any_rows = pl.ds(0, SUBLANES)

    def chunk_at(ref, idx):
        return pl.ds(pl.multiple_of(ref[idx] * SUBLANES, SUBLANES), SUBLANES)

    def in_copy(rows, sl, c):
        return pltpu.make_async_copy(xs_hbm.at[rows], xbuf.at[sl, pl.ds(c * SUBLANES, SUBLANES)], sem_in.at[sl])

    def out_copy(rows, sl, c):
        return pltpu.make_async_copy(ybuf.at[sl, pl.ds(c * SUBLANES, SUBLANES)], ys_hbm.at[rows], sem_out.at[sl])

    def start_gather(blk):
        for c in range(BLOCK_CHUNKS):
            in_copy(chunk_at(src_ref, blk * BLOCK_CHUNKS + c), blk % GATHER_BUFS, c).start()

    def wait_gather(blk):
        for c in range(BLOCK_CHUNKS):
            in_copy(any_rows, blk % GATHER_BUFS, c).wait()

    def start_scatter(blk):
        for c in range(BLOCK_CHUNKS):
            out_copy(chunk_at(dst_ref, (blk + SCATTER_LAG) * BLOCK_CHUNKS + c), blk % SCATTER_BUFS, c).start()

    def wait_scatter(sl):
        for c in range(BLOCK_CHUNKS):
            out_copy(any_rows, sl, c).wait()

    def load_rows(blk):
        return _unpack_rows(xbuf[blk % GATHER_BUFS])

    def issue_dmas(blk):
        start_scatter(blk - SCATTER_LAG)
        start_gather(blk + lookahead)

    def up_gate(blk, rows, ws):
        x_lo, x_hi = rows
        a1 = (jnp.dot(x_lo, w1b[ws, 0:HALF, :], preferred_element_type=F32)
              + jnp.dot(x_hi, w1b[ws, HALF:, :], preferred_element_type=F32))
        a3 = (jnp.dot(x_lo, w3b[ws, 0:HALF, :], preferred_element_type=F32)
              + jnp.dot(x_hi, w3b[ws, HALF:, :], preferred_element_type=F32))
        hbuf[blk % SCATTER_BUFS] = (a1 * jax.nn.sigmoid(a1) * a3).astype(BF16)

    def down(blk, ws):
        y = jnp.dot(hbuf[blk % SCATTER_BUFS], w2b[ws], preferred_element_type=F32)
        ybuf[blk % SCATTER_BUFS] = _pack_rows(y.astype(BF16).astype(F32))

    def for_each_weight_copy(gen, fn):
        ws = gen % 2
        expert = elist_ref[gen]
        for hbm, staging in ((w1_hbm, wst1), (w3_hbm, wst3), (w2_hbm, wst2)):
            fn(pltpu.make_async_copy(hbm.at[expert], staging.at[ws], sem_w.at[ws]))

    def switch_weights(gen):
        ws = gen % 2
        for_each_weight_copy(gen, lambda cp: cp.wait())
        w1b[ws] = wst1[ws].astype(BF16)
        w3b[ws] = wst3[ws].astype(BF16)
        w2b[ws] = wst2[ws].astype(BF16)

        @pl.when(gen + 1 < n_gen)
        def _():
            for_each_weight_copy(gen + 1, lambda cp: cp.start())

    def for_each_tail_piece(fn):
        def area_body(s, carry):
            used = used_ref[s]
            tail = AREA_CHUNKS - used
            row = (s * AREA_CHUNKS + used) * SUBLANES
            for bit in ZERO_BITS:
                take = (tail & bit) != 0

                @pl.when(take)
                def _(row=row, bit=bit):
                    fn(pltpu.make_async_copy(zbuf.at[pl.ds(0, bit * SUBLANES)],
                                             ys_hbm.at[pl.ds(pl.multiple_of(row, SUBLANES), bit * SUBLANES)],
                                             sem_zero))
                row = row + jnp.where(take, bit * SUBLANES, 0)
            return carry
        lax.fori_loop(0, n_areas, area_body, 0)

    @pl.when(n_gen > 0)
    def _():
        for_each_weight_copy(0, lambda cp: cp.start())

    zbuf[...] = jnp.zeros(zbuf.shape, U32)
    ybuf[...] = jnp.zeros(ybuf.shape, U32)
    dump = pltpu.make_async_copy(zbuf.at[pl.ds(0, DUMP_CHUNKS * SUBLANES)],
                                 ys_hbm.at[pl.ds(dump_row0, DUMP_CHUNKS * SUBLANES)], sem_dump)
    dump.start()
    dump.wait()
    for_each_tail_piece(lambda cp: cp.start())
    for sl in range(SCATTER_BUFS - SCATTER_LAG):
        for c in range(BLOCK_CHUNKS):
            out_copy(pl.ds(dump_row0 + (sl * BLOCK_CHUNKS + c) * SUBLANES, SUBLANES), sl, c).start()
    for k in range(lookahead):
        start_gather(k)

    @pl.when(total > 0)
    def _():
        switch_weights(0)
        wait_gather(0)
        rows = load_rows(0)
        issue_dmas(0)
        up_gate(0, rows, 0)

        def block_body(blk, carry):
            gen = bgen_ref[blk]
            gen_prev = bgen_ref[blk - 1]

            @pl.when(gen != gen_prev)
            def _():
                switch_weights(gen)

            wait_gather(blk)
            wait_scatter((blk - 1) % SCATTER_BUFS)
            rows = load_rows(blk)
            issue_dmas(blk)
            down(blk - 1, gen_prev % 2)
            up_gate(blk, rows, gen % 2)
            return carry

        lax.fori_loop(1, total, block_body, 0)
        wait_scatter((total - 1) % SCATTER_BUFS)
        down(total - 1, bgen_ref[total - 1] % 2)

    for k in range(SCATTER_LAG):
        start_scatter(total - SCATTER_LAG + k)
    for sl in range(SCATTER_BUFS):
        wait_scatter(sl)
    for k in range(lookahead):
        wait_gather(total + k)
    for_each_tail_piece(lambda cp: cp.wait())


def _moe_combine_kernel(h_ref, route_ref, ys_ref, l2g_ref, l2b_ref, y_ref):
    n = h_ref.shape[0]
    scope = n // (ys_ref.shape[0] // AREA_ROWS)
    s_iota = lax.broadcasted_iota(jnp.int32, (scope, AREA_ROWS), 1).astype(F32)
    for sc in range(n // scope):
        rows = slice(sc * scope, (sc + 1) * scope)
        route = route_ref[rows, :]
        slot1 = route[:, 0:1]
        slot2 = route[:, 1:2]
        gate1 = route[:, 2:3]
        gate2 = route[:, 3:4]
        gmat = (jnp.where(s_iota == slot1, gate1, 0.0) + jnp.where(s_iota == slot2, gate2, 0.0)).astype(BF16)
        y_lo, y_hi = _unpack_rows(ys_ref[sc * AREA_ROWS:(sc + 1) * AREA_ROWS, :])
        f = jnp.concatenate([jnp.dot(gmat, y_lo, preferred_element_type=F32),
                             jnp.dot(gmat, y_hi, preferred_element_type=F32)], axis=1)
        y_ref[rows, :] = _layer_norm(ALPHA * h_ref[rows, :] + f, l2g_ref[...], l2b_ref[...])


def _excl_cumsum(a, axis):
    return jnp.cumsum(a, axis=axis) - a


def _plan_blocks(cnt, q_max):
    n_areas = cnt.shape[0]
    c8 = (cnt + (SUBLANES - 1)) // SUBLANES
    used8 = jnp.sum(c8, axis=1)
    base8 = jnp.arange(n_areas, dtype=jnp.int32) * AREA_CHUNKS
    src_runs = (base8[:, None] + _excl_cumsum(c8, 1)).T
    len_runs = c8.T
    tot = jnp.sum(len_runs, axis=1)
    ptot = (tot + BLOCK_CHUNKS - 1) // BLOCK_CHUNKS * BLOCK_CHUNKS
    pstart = _excl_cumsum(ptot, 0)
    dst_runs = pstart[:, None] + _excl_cumsum(len_runs, 1)
    off = (src_runs - dst_runs).reshape(-1)
    diff = off - jnp.concatenate([jnp.zeros((1,), jnp.int32), off[:-1]])
    q = jnp.arange(q_max, dtype=jnp.int32)
    src = q + jnp.sum(jnp.where(dst_runs.reshape(1, -1) <= q[:, None], diff[None, :], 0), axis=1)
    valid = jnp.any((pstart[None, :] <= q[:, None]) & (q[:, None] < (pstart + tot)[None, :]), axis=1)
    zero_chunk = AREA_CHUNKS - 1
    ahead = jnp.full(((GATHER_BUFS - 1) * BLOCK_CHUNKS,), zero_chunk, jnp.int32)
    src_tab = jnp.concatenate([jnp.where(valid, src, zero_chunk).astype(jnp.int32), ahead])
    qv = jnp.arange(-SCATTER_LAG * BLOCK_CHUNKS, q_max, dtype=jnp.int32)
    dump = n_areas * AREA_CHUNKS + ((qv // BLOCK_CHUNKS) % SCATTER_BUFS) * BLOCK_CHUNKS + qv % BLOCK_CHUNKS
    lead = jnp.zeros((SCATTER_LAG * BLOCK_CHUNKS,), jnp.int32)
    dst_tab = jnp.where(jnp.concatenate([lead > 0, valid]), jnp.concatenate([lead, src]), dump).astype(jnp.int32)
    nblk = ptot // BLOCK_CHUNKS
    bstart = pstart // BLOCK_CHUNKS
    has = nblk > 0
    gen_of = jnp.cumsum(has.astype(jnp.int32)) - 1
    experts = jnp.arange(N_EXPERTS, dtype=jnp.int32)
    elist = jnp.sum(jnp.where(has[None, :] & (gen_of[None, :] == experts[:, None]), experts[None, :], 0), axis=1)
    blk = jnp.arange(q_max // BLOCK_CHUNKS, dtype=jnp.int32)
    owner = (bstart[None, :] <= blk[:, None]) & (blk[:, None] < (bstart + nblk)[None, :])
    bgen = jnp.sum(jnp.where(owner, gen_of[None, :], 0), axis=1)
    meta = jnp.stack([jnp.sum(nblk), jnp.sum(has.astype(jnp.int32))])
    return (src_tab, dst_tab, bgen.astype(jnp.int32), elist.astype(jnp.int32), meta.astype(jnp.int32),
            used8.astype(jnp.int32))


PREP_STEPS = 4


def _swap_head_halves(a):
    half = HEAD_DIM // 2
    pieces = []
    for g in range(a.shape[-1] // LANES):
        p = a[:, g * LANES:(g + 1) * LANES]
        lane = lax.broadcasted_iota(jnp.int32, p.shape, 1)
        pieces.append(jnp.where((lane & half) == 0, pltpu.roll(p, LANES - half, 1), pltpu.roll(p, half, 1)))
    return jnp.concatenate(pieces, axis=-1)


def _glu_blocks(a):
    lo = ATTN_DIM + 2 * KV_DIM
    pieces = []
    for cb in range(CONV_CH // LANES):
        pieces.append(a[:, lo + cb * LANES:lo + (cb + 1) * LANES])
        pieces.append(a[:, lo + CONV_CH + cb * LANES:lo + CONV_CH + (cb + 1) * LANES])
    return jnp.concatenate(pieces, axis=-1)


def _prep_weights_kernel(w_in_ref, b_in_ref, w_out_ref, w_in_b_ref, w_sw_ref, w_glu_ref, w_out_b_ref,
                         b_sw_ref, b_glu_ref):
    qk = ATTN_DIM + KV_DIM
    w = w_in_ref[...]
    w_in_b_ref[...] = w.astype(BF16)
    w_sw_ref[...] = _swap_head_halves(w[:, :qk]).astype(BF16)
    w_glu_ref[...] = _glu_blocks(w).astype(BF16)
    w_out_b_ref[...] = w_out_ref[...].astype(BF16)
    b = jnp.broadcast_to(b_in_ref[...], (SUBLANES, IN_DIM))
    b_sw_ref[...] = _swap_head_halves(b[:, :qk])[0:1]
    b_glu_ref[...] = _glu_blocks(b)[0:1]


def _rope_tables(pos):
    half = HEAD_DIM // 2
    inv = (ROPE_THETA ** (-np.arange(half, dtype=np.float64) / half)).astype(np.float32)
    ang = (np.asarray(pos, np.float32)[:, None] * inv).astype(np.float64)
    cos = np.cos(ang)
    sin = np.sin(ang)
    reps = LANES // HEAD_DIM
    cos_t = np.concatenate([cos, cos] * reps, axis=1)
    sin_t = np.concatenate([-sin, sin] * reps, axis=1)
    return jnp.asarray(np.concatenate([cos_t, sin_t], axis=1), F32)


def _full(shape):
    nd = len(shape)
    return pl.BlockSpec(shape, lambda *_: (0,) * nd)


def kernel(x_prompt, x_sample, cache_k, cache_v, state_conv, w_in, b_in, sinks, conv_w, conv_b, ln_conv_g,
           ln_conv_b, w_out, b_out, ln1_g, ln1_b, w_router_group, w_router_expert, w1, w3, w2, ln2_g, ln2_b):
    assert w_in.shape[0] == DEPTH
    bsz, seq, _ = x_prompt.shape
    dec_b, dec_t, _ = x_sample.shape
    assert dec_t == 1 and seq % TILE_Q == 0 and dec_b % SAMPLE_TB == 0 and 2 * dec_b + 7 * N_EXPERTS <= AREA_ROWS
    assert (bsz * seq) % COMBINE_TILE == 0
    n_prompt = bsz * seq
    tiles = seq // TILE_Q
    spt = TILE_Q // PROMPT_SCOPE
    p_areas = n_prompt // PROMPT_SCOPE
    n_areas = p_areas + spt
    q_max = (n_areas * AREA_CHUNKS + N_EXPERTS * (BLOCK_CHUNKS - 1) + BLOCK_CHUNKS - 1) // BLOCK_CHUNKS * BLOCK_CHUNKS

    row = lambda a: a.reshape(1, -1)
    cparams = lambda sem: pltpu.CompilerParams(dimension_semantics=sem, vmem_limit_bytes=VMEM_LIMIT)

    qk = ATTN_DIM + KV_DIM
    wrows = D_MODEL // PREP_STEPS
    wspec = lambda cols: pl.BlockSpec((wrows, cols), lambda t: (t, 0))
    w_in_b, w_sw_b, w_glu_b, w_out_b, b_sw, b_glu = pl.pallas_call(
        _prep_weights_kernel,
        grid=(PREP_STEPS,),
        in_specs=[wspec(IN_DIM), _full((1, IN_DIM)), wspec(D_MODEL)],
        out_specs=[wspec(IN_DIM), wspec(qk), wspec(2 * CONV_CH), wspec(D_MODEL),
                   _full((1, qk)), _full((1, 2 * CONV_CH))],
        out_shape=[jax.ShapeDtypeStruct((D_MODEL, IN_DIM), BF16), jax.ShapeDtypeStruct((D_MODEL, qk), BF16),
                   jax.ShapeDtypeStruct((D_MODEL, 2 * CONV_CH), BF16), jax.ShapeDtypeStruct((D_MODEL, D_MODEL), BF16),
                   jax.ShapeDtypeStruct((1, qk), F32), jax.ShapeDtypeStruct((1, 2 * CONV_CH), F32)],
        compiler_params=cparams(("arbitrary",)),
        name="prep_weights",
    )(w_in[0], row(b_in[0]), w_out[0])
    w_r = jnp.concatenate([w_router_group[0], w_router_expert[0].reshape(D_MODEL, N_EXPERTS),
                           jnp.zeros((D_MODEL, LANES - ELANE0 - N_EXPERTS), F32)], axis=1).astype(BF16)
    shared = (w_in_b, row(b_in[0]), w_sw_b, b_sw, conv_w[0], row(conv_b[0]),
              row(ln_conv_g[0]), row(ln_conv_b[0]), w_out_b, row(b_out[0]), row(ln1_g[0]), row(ln1_b[0]), w_r)
    shared_specs = [_full(a.shape) for a in shared]
    glu = (w_glu_b, b_glu)
    glu_specs = [_full(a.shape) for a in glu]
    assert len(glu) + len(shared) == N_PROMPT_WEIGHTS
    smem = pl.BlockSpec(memory_space=pltpu.SMEM)
    rope_p = _rope_tables(np.arange(seq))
    rope_s = _rope_tables(PAST_LEN + np.arange(dec_t))

    ck = cache_k[0].reshape(dec_b, WINDOW, KV_DIM)
    cv = cache_v[0].reshape(dec_b, WINDOW, KV_DIM)
    tb = SAMPLE_TB
    conv_state_spec = pl.BlockSpec((None, tb, CONV_WIDTH - 1, CONV_CH), lambda t: (0, t, 0, 0))
    h_s, nk_s, nv_s, nc_s, xs_s, route_s, cnt_s = pl.pallas_call(
        _mix_sample_kernel,
        grid=(dec_b // tb,),
        in_specs=[smem,
                  pl.BlockSpec((tb, None, D_MODEL), lambda t: (t, 0, 0)),
                  _full(rope_s.shape),
                  pl.BlockSpec((tb, WINDOW, KV_DIM), lambda t: (t, 0, 0)),
                  pl.BlockSpec((tb, WINDOW, KV_DIM), lambda t: (t, 0, 0)),
                  conv_state_spec] + shared_specs,
        out_specs=[pl.BlockSpec((tb, D_MODEL), lambda t: (t, 0)),
                   pl.BlockSpec((tb, WINDOW, KV_DIM), lambda t: (t, 0, 0)),
                   pl.BlockSpec((tb, WINDOW, KV_DIM), lambda t: (t, 0, 0)),
                   conv_state_spec,
                   _full((AREA_ROWS, HALF)),
                   _full((dec_b, LANES)),
                   _full((1, SUBLANES, LANES))],
        out_shape=[jax.ShapeDtypeStruct((dec_b, D_MODEL), F32),
                   jax.ShapeDtypeStruct((dec_b, WINDOW, KV_DIM), F32),
                   jax.ShapeDtypeStruct((dec_b, WINDOW, KV_DIM), F32),
                   jax.ShapeDtypeStruct((DEPTH, dec_b, CONV_WIDTH - 1, CONV_CH), F32),
                   jax.ShapeDtypeStruct((AREA_ROWS, HALF), U32),
                   jax.ShapeDtypeStruct((dec_b, LANES), F32),
                   jax.ShapeDtypeStruct((1, SUBLANES, LANES), F32)],
        scratch_shapes=[pltpu.VMEM((dec_b, D_MODEL), F32)],
        compiler_params=cparams(("arbitrary",)),
        name="mix_sample",
    )(sinks[0], x_sample, rope_s, ck, cv, state_conv, *shared)

    n_real = bsz * tiles
    real = lambda t: jnp.minimum(t, n_real - 1)
    h_p, xs, route_p, cnt_p, nk_p, nv_p, nc_p = pl.pallas_call(
        functools.partial(_mix_prompt_kernel, tiles=tiles),
        grid=(n_real + 1,),
        in_specs=[smem,
                  pl.BlockSpec((1, TILE_Q, D_MODEL), lambda t: (real(t) // tiles, real(t) % tiles, 0)),
                  pl.BlockSpec((TILE_Q, 2 * LANES), lambda t: (real(t) % tiles, 0)),
                  _full((AREA_ROWS, HALF))] + glu_specs + shared_specs,
        out_specs=[pl.BlockSpec((1, TILE_Q, D_MODEL), lambda t: (real(t) // tiles, real(t) % tiles, 0)),
                   pl.BlockSpec((spt * AREA_ROWS, HALF), lambda t: (t, 0)),
                   pl.BlockSpec((TILE_Q, LANES), lambda t: (real(t), 0)),
                   pl.BlockSpec((spt, SUBLANES, LANES), lambda t: (t, 0, 0)),
                   pl.BlockSpec((1, WINDOW, KV_DIM), lambda t: (real(t) // tiles, 0, 0)),
                   pl.BlockSpec((1, WINDOW, KV_DIM), lambda t: (real(t) // tiles, 0, 0)),
                   pl.BlockSpec((1, CONV_WIDTH - 1, CONV_CH), lambda t: (real(t) // tiles, 0, 0))],
        out_shape=[jax.ShapeDtypeStruct((bsz, seq, D_MODEL), F32),
                   jax.ShapeDtypeStruct((n_areas * AREA_ROWS, HALF), U32),
                   jax.ShapeDtypeStruct((n_prompt, LANES), F32),
                   jax.ShapeDtypeStruct((n_areas, SUBLANES, LANES), F32),
                   jax.ShapeDtypeStruct((bsz, WINDOW, KV_DIM), F32),
                   jax.ShapeDtypeStruct((bsz, WINDOW, KV_DIM), F32),
                   jax.ShapeDtypeStruct((bsz, CONV_WIDTH - 1, CONV_CH), F32)],
        scratch_shapes=[pltpu.VMEM((WINDOW, KV_DIM), F32), pltpu.VMEM((WINDOW, KV_DIM), F32),
                        pltpu.VMEM((TILE_Q + CONV_PRE, CONV_CH), F32),
                        pltpu.VMEM((SUBLANES - 1, TILE_Q + CONV_SHIFT_EXTRA, LANES), F32)],
        compiler_params=cparams(("arbitrary",)),
        name="mix_prompt",
    )(sinks[0], x_prompt, rope_p, xs_s, *glu, *shared)

    cnt = jnp.concatenate([cnt_p[:p_areas, 0], cnt_s[:, 0], cnt_p[p_areas + 1:, 0]], axis=0)
    cnt = cnt[:, ELANE0:ELANE0 + N_EXPERTS].astype(jnp.int32)
    plan = _plan_blocks(cnt, q_max)
    hbm = pl.BlockSpec(memory_space=pl.ANY)
    up_shape, down_shape = (2, D_MODEL, D_FF_EXPERT), (2, D_FF_EXPERT, D_MODEL)
    ys = pl.pallas_call(
        _moe_ffn_kernel,
        grid_spec=pltpu.PrefetchScalarGridSpec(
            num_scalar_prefetch=len(plan),
            grid=(1,),
            in_specs=[hbm, hbm, hbm, hbm],
            out_specs=hbm,
            scratch_shapes=[pltpu.VMEM((GATHER_BUFS, BLOCK_ROWS, HALF), U32),
                            pltpu.VMEM((SCATTER_BUFS, BLOCK_ROWS, HALF), U32),
                            pltpu.VMEM((SCATTER_BUFS, BLOCK_ROWS, D_FF_EXPERT), BF16),
                            pltpu.VMEM((ZERO_ROWS, HALF), U32),
                            pltpu.VMEM(up_shape, F32), pltpu.VMEM(up_shape, F32), pltpu.VMEM(down_shape, F32),
                            pltpu.VMEM(up_shape, BF16), pltpu.VMEM(up_shape, BF16), pltpu.VMEM(down_shape, BF16),
                            pltpu.SemaphoreType.DMA((GATHER_BUFS,)), pltpu.SemaphoreType.DMA((SCATTER_BUFS,)),
                            pltpu.SemaphoreType.DMA(()), pltpu.SemaphoreType.DMA(()),
                            pltpu.SemaphoreType.DMA((2,))]),
        out_shape=jax.ShapeDtypeStruct((n_areas * AREA_ROWS + DUMP_CHUNKS * SUBLANES, HALF), U32),
        compiler_params=cparams(("arbitrary",)),
        name="moe_ffn",
    )(*plan, xs, w1[0], w3[0], w2[0])

    def combine(h2d, route, first_area, scope, tile):
        n = h2d.shape[0]
        apt = tile // scope
        return pl.pallas_call(
            _moe_combine_kernel,
            grid=(n // tile,),
            in_specs=[pl.BlockSpec((tile, D_MODEL), lambda t: (t, 0)),
                      pl.BlockSpec((tile, LANES), lambda t: (t, 0)),
                      pl.BlockSpec((apt * AREA_ROWS, HALF), lambda t: (first_area // apt + t, 0)),
                      _full((1, D_MODEL)), _full((1, D_MODEL))],
            out_specs=pl.BlockSpec((tile, D_MODEL), lambda t: (t, 0)),
            out_shape=jax.ShapeDtypeStruct((n, D_MODEL), F32),
            compiler_params=cparams(("arbitrary",)),
            name="moe_combine",
        )(h2d, route, ys, row(ln2_g[0]), row(ln2_b[0]))

    y_p = combine(h_p.reshape(n_prompt, D_MODEL), route_p, 0, PROMPT_SCOPE, COMBINE_TILE)
    y_s = combine(h_s, route_s, p_areas, dec_b, dec_b)

    kv_shape = lambda n: (DEPTH, n, WINDOW, N_KV_HEADS, HEAD_DIM)
    return (y_p.reshape(bsz, seq, D_MODEL), y_s.reshape(dec_b, dec_t, D_MODEL),
            nk_p.reshape(kv_shape(bsz)), nv_p.reshape(kv_shape(bsz)), nc_p[None],
            nk_s.reshape(kv_shape(dec_b)), nv_s.reshape(kv_shape(dec_b)), nc_s)
```

```python
import functools

import jax
import jax.numpy as jnp
import numpy as np
from jax import lax
from jax.experimental import pallas as pl
from jax.experimental.pallas import tpu as pltpu

F32 = jnp.float32
BF16 = jnp.bfloat16
U32 = jnp.uint32

D_MODEL = 1024
N_HEADS = 8
N_KV_HEADS = 2
HEAD_DIM = 64
Q_GROUP = N_HEADS // N_KV_HEADS
ATTN_DIM = N_HEADS * HEAD_DIM
KV_DIM = N_KV_HEADS * HEAD_DIM
WINDOW = 128
ROPE_THETA = 10000.0
ATTN_SCALE = HEAD_DIM ** -0.5
CONV_CH = D_MODEL - ATTN_DIM
CONV_WIDTH = 31
IN_DIM = ATTN_DIM + 2 * KV_DIM + 2 * CONV_CH
N_GROUPS = 4
EXPERTS_PER_GROUP = 8
N_EXPERTS = N_GROUPS * EXPERTS_PER_GROUP
D_FF_EXPERT = 256
LN_EPS = 1e-5
NEG_INF = -1e30
DEPTH = 1
ALPHA = (2.0 * DEPTH) ** 0.25
PAST_LEN = 16384
LOG2E = 1.4426950408889634

LANES = 128
SUBLANES = 8
HALF = D_MODEL // 2
ELANE0 = N_GROUPS
PROMPT_SCOPE = 256
AREA_ROWS = -(-(2 * PROMPT_SCOPE + (SUBLANES - 1) * N_EXPERTS) // LANES) * LANES
AREA_CHUNKS = AREA_ROWS // SUBLANES
BLOCK_CHUNKS = 32
BLOCK_ROWS = BLOCK_CHUNKS * SUBLANES
TILE_Q = 512
COMBINE_TILE = 1024
CONV_PRE = 32
CONV_SHIFT_EXTRA = 24
SAMPLE_TB = 32
SAMPLE_GRP = 8
VMEM_LIMIT = 56 * 1024 * 1024


def _layer_norm(x, g, b):
    mu = jnp.mean(x, axis=-1, keepdims=True)
    xc = x - mu
    var = jnp.mean(xc * xc, axis=-1, keepdims=True)
    return xc * lax.rsqrt(var + LN_EPS) * g + b


def _rope(x, x_swapped, cos, sin_signed):
    pieces = [x[:, g * LANES:(g + 1) * LANES] * cos + x_swapped[:, g * LANES:(g + 1) * LANES] * sin_signed
              for g in range(x.shape[-1] // LANES)]
    return pieces[0] if len(pieces) == 1 else jnp.concatenate(pieces, axis=-1)


def _in_proj_glu(xb, w_in_ref, b_in_ref):
    lo, hi = ATTN_DIM + 2 * KV_DIM, IN_DIM
    cacg = jnp.dot(xb, w_in_ref[:, lo:hi], preferred_element_type=F32) + b_in_ref[:, lo:hi]
    return cacg[:, :CONV_CH] * jax.nn.sigmoid(cacg[:, CONV_CH:])


def _in_proj_qkv(xb, w_in_ref, b_in_ref, w_sw_ref, b_sw_ref, rope):
    o_v, o_e = ATTN_DIM + KV_DIM, ATTN_DIM + 2 * KV_DIM
    qkv = jnp.dot(xb, w_in_ref[:, 0:o_e], preferred_element_type=F32) + b_in_ref[:, 0:o_e]
    sw = jnp.dot(xb, w_sw_ref[...], preferred_element_type=F32) + b_sw_ref[...]
    k = _rope(qkv[:, ATTN_DIM:o_v], sw[:, ATTN_DIM:], rope[:, 0:LANES], rope[:, LANES:2 * LANES])
    q = _rope(qkv[:, :ATTN_DIM], sw[:, :ATTN_DIM], rope[:, 2 * LANES:3 * LANES], rope[:, 3 * LANES:])
    return q, k, qkv[:, o_v:]


def _pack_rows(v):
    lo = lax.bitcast_convert_type(v[:, :HALF], U32)
    hi = lax.bitcast_convert_type(v[:, HALF:], U32)
    return (lo >> 16) | hi


def _unpack_rows(w):
    lo = lax.bitcast_convert_type(w << 16, F32).astype(BF16)
    hi = lax.bitcast_convert_type(w & jnp.uint32(0xFFFF0000), F32).astype(BF16)
    return lo, hi


def _drain(stages):
    try:
        while True:
            next(stages)
    except StopIteration as stop:
        return stop.value


def _round_robin(*stage_generators):
    live = list(stage_generators)
    while live:
        for stages in list(live):
            try:
                next(stages)
            except StopIteration:
                live.remove(stages)


def _route_and_sort_stages(h, w_r):
    n = h.shape[0]
    hb = h.astype(BF16)
    logits = jnp.dot(hb, w_r, preferred_element_type=F32)
    yield
    lane = lax.broadcasted_iota(jnp.int32, (n, LANES), 1)
    lane_f = lane.astype(F32)
    is_g = lane < N_GROUPS
    lg = jnp.where(is_g, logits, -jnp.inf)
    gmax = jnp.max(lg, axis=-1, keepdims=True)
    yield
    gidx = jnp.min(jnp.where(lg == gmax, lane_f, float(LANES)), axis=-1, keepdims=True)
    p_g = 1.0 / jnp.sum(jnp.where(is_g, jnp.exp(logits - gmax), 0.0), axis=-1, keepdims=True)
    yield
    egrp = ((lane - ELANE0) >> (EXPERTS_PER_GROUP.bit_length() - 1)).astype(F32)
    emask = (lane >= ELANE0) & (lane < ELANE0 + N_EXPERTS) & (egrp == gidx)
    sel = jnp.where(emask, logits, -jnp.inf)
    v1 = jnp.max(sel, axis=-1, keepdims=True)
    yield
    i1 = jnp.min(jnp.where(sel == v1, lane_f, float(LANES)), axis=-1, keepdims=True)
    yield
    sel2 = jnp.where(lane_f == i1, -jnp.inf, sel)
    v2 = jnp.max(sel2, axis=-1, keepdims=True)
    yield
    i2 = jnp.min(jnp.where(sel2 == v2, lane_f, float(LANES)), axis=-1, keepdims=True)
    t = jnp.exp(v2 - v1)
    gate1 = p_g / (1.0 + t)
    gate2 = p_g * t / (1.0 + t)
    pick1 = lane_f == i1
    pick2 = lane_f == i2
    member = jnp.where(pick1 | pick2, 1.0, 0.0)
    yield

    r_i = lax.broadcasted_iota(jnp.int32, (n, n), 0)
    c_i = lax.broadcasted_iota(jnp.int32, (n, n), 1)
    lower = jnp.where(c_i < r_i, 1.0, 0.0).astype(BF16)
    e_r = lax.broadcasted_iota(jnp.int32, (LANES, LANES), 0)
    e_c = lax.broadcasted_iota(jnp.int32, (LANES, LANES), 1)
    upper = jnp.where(e_r < e_c, 1.0, 0.0).astype(BF16)
    before = jnp.dot(lower, member.astype(BF16), preferred_element_type=F32)
    cnt = jnp.sum(member, axis=0, keepdims=True)
    c8 = jnp.floor((cnt + (SUBLANES - 1)) * (1.0 / SUBLANES))
    c8b = jnp.broadcast_to(c8, (SUBLANES, LANES)).astype(BF16)
    off8 = jnp.dot(c8b, upper, preferred_element_type=F32)[0:1]
    yield
    slot_all = before + off8 * float(SUBLANES)
    slot1 = jnp.sum(jnp.where(pick1, slot_all, 0.0), axis=-1, keepdims=True)
    slot2 = jnp.sum(jnp.where(pick2, slot_all, 0.0), axis=-1, keepdims=True)
    yield
    s_iota = lax.broadcasted_iota(jnp.int32, (n, AREA_ROWS), 1).astype(F32)
    perm_t = jnp.where((s_iota == slot1) | (s_iota == slot2), 1.0, 0.0).astype(BF16)
    sorted_rows = lax.dot_general(perm_t, hb, (((0,), (0,)), ((), ())),
                                  preferred_element_type=F32)
    yield
    route = jnp.where(lane == 0, slot1, jnp.where(lane == 1, slot2, jnp.where(lane == 2, gate1,
                      jnp.where(lane == 3, gate2, 0.0))))
    return route, _pack_rows(sorted_rows), cnt


def _softmax_rows_with_sink(s, sink, extra=None):
    m = jnp.maximum(jnp.max(s, axis=-1, keepdims=True), sink)
    if extra is not None:
        m = jnp.maximum(m, extra)
    e = jnp.exp2(s - m)
    den = jnp.sum(e, axis=-1, keepdims=True) + jnp.exp2(sink - m)
    if extra is not None:
        ee = jnp.exp2(extra - m)
        den = den + ee
        return e / den, ee / den
    return e / den, None


def _mix_out_stages(x, attn, conv_pre, lcg, lcb, w_out_ref, b_out, l1g, l1b):
    conv = _layer_norm(conv_pre, lcg, lcb)
    conv = conv * jax.nn.sigmoid(conv)
    yield
    mixed = jnp.concatenate([attn.astype(BF16), conv.astype(BF16)], axis=1)
    mix = jnp.dot(mixed, w_out_ref[...], preferred_element_type=F32) + b_out
    yield
    h = _layer_norm(ALPHA * x + mix, l1g, l1b)
    yield
    return h


N_PROMPT_WEIGHTS = 15


def _mix_prompt_kernel(sinks_ref, x_ref, rope_ref, xs_s_ref, *rest, tiles):
    _, xs_ref, _, cnt_ref = rest[N_PROMPT_WEIGHTS:N_PROMPT_WEIGHTS + 4]
    step = pl.program_id(0)
    n_real = pl.num_programs(0) - 1

    @pl.when(step < n_real)
    def _():
        _mix_prompt_tile(sinks_ref, x_ref, rope_ref, *rest, i=step % tiles)

    @pl.when(step == n_real)
    def _():
        xs_ref[0:AREA_ROWS, :] = xs_s_ref[...]
        xs_ref[AREA_ROWS:, :] = jnp.zeros((xs_ref.shape[0] - AREA_ROWS, HALF), U32)
        cnt_ref[...] = jnp.zeros(cnt_ref.shape, F32)


def _mix_prompt_tile(sinks_ref, x_ref, rope_ref, w_glu_ref, b_glu_ref,
                     w_in_ref, b_in_ref, w_sw_ref, b_sw_ref, conv_w_ref, conv_b_ref,
                     lcg_ref, lcb_ref, w_out_ref, b_out_ref, l1g_ref, l1b_ref, w_r_ref,
                     h_ref, xs_ref, route_ref, cnt_ref, nk_ref, nv_ref, nc_ref,
                     kprev, vprev, u_scr, u_shift, *, i):
    tq = x_ref.shape[1]
    pre = CONV_PRE
    n_sc = tq // PROMPT_SCOPE

    @pl.when(i == 0)
    def _():
        kprev[...] = jnp.zeros_like(kprev)
        vprev[...] = jnp.zeros_like(vprev)
        u_scr[0:pre, :] = jnp.zeros((pre, CONV_CH), F32)

    x = x_ref[0]
    xb = x.astype(BF16)

    def glu_block(cb):
        cols = slice(2 * LANES * cb, 2 * LANES * (cb + 1))
        vg = jnp.dot(xb, w_glu_ref[:, cols], preferred_element_type=F32) + b_glu_ref[:, cols]
        return vg[:, :LANES] * jax.nn.sigmoid(vg[:, LANES:])

    n_cb = CONV_CH // LANES
    acc_blocks = [[None] * n_cb for _ in range(n_sc)]
    u_next = glu_block(0)
    for cb in range(n_cb):
        lanes = slice(LANES * cb, LANES * (cb + 1))
        u_scr[pre:pre + tq, lanes] = u_next
        if cb + 1 < n_cb:
            u_next = glu_block(cb + 1)
        for r in range(1, SUBLANES):
            u_shift[r - 1, :, lanes] = u_scr[r:r + tq + CONV_SHIFT_EXTRA, lanes]
        for sc in range(n_sc):
            base = sc * PROMPT_SCOPE
            acc = jnp.zeros((PROMPT_SCOPE, LANES), F32)
            for j in range(CONV_WIDTH):
                a, r = divmod(pre - (CONV_WIDTH - 1) + j, SUBLANES)
                start = base + a * SUBLANES
                src = (u_scr[start:start + PROMPT_SCOPE, lanes] if r == 0
                       else u_shift[r - 1, start:start + PROMPT_SCOPE, lanes])
                acc = acc + src * conv_w_ref[j:j + 1, lanes]
            acc_blocks[sc][cb] = acc
    conv_pre = [jnp.concatenate(blocks, axis=1) + conv_b_ref[...] for blocks in acc_blocks]

    nc_ref[0] = u_scr[pre + tq - (CONV_WIDTH - 1):pre + tq, :]
    tail = u_scr[tq:tq + pre, :]
    u_scr[0:pre, :] = tail

    q, k, v = _in_proj_qkv(xb, w_in_ref, b_in_ref, w_sw_ref, b_sw_ref, rope_ref[...])
    qb = q.astype(BF16)
    kb = jnp.concatenate([kprev[...], k], axis=0).astype(BF16)
    vb = jnp.concatenate([vprev[...], v], axis=0).astype(BF16)
    row = lax.broadcasted_iota(jnp.int32, (WINDOW, 2 * WINDOW), 0)
    col = lax.broadcasted_iota(jnp.int32, (WINDOW, 2 * WINDOW), 1)
    dist = row + WINDOW - col
    band = (dist >= 0) & (dist < WINDOW)
    first_col = jnp.where(i > 0, 0, WINDOW)
    kprev[...] = k[tq - WINDOW:, :]
    vprev[...] = v[tq - WINDOW:, :]
    nk_ref[0] = k[tq - WINDOW:, :]
    nv_ref[0] = v[tq - WINDOW:, :]
    blocks_per_scope = PROMPT_SCOPE // WINDOW
    attn_blocks = [None] * (tq // WINDOW)

    def scope_attention(sc):
        for j in range(sc * blocks_per_scope, (sc + 1) * blocks_per_scope):
            msk = (band & (col >= first_col)) if j == 0 else band
            heads = [None] * N_HEADS
            for c in range(N_KV_HEADS):
                kc = kb[j * WINDOW:(j + 2) * WINDOW, c * HEAD_DIM:(c + 1) * HEAD_DIM]
                vc = vb[j * WINDOW:(j + 2) * WINDOW, c * HEAD_DIM:(c + 1) * HEAD_DIM]
                qc = jnp.concatenate(
                    [qb[j * WINDOW:(j + 1) * WINDOW, (c * Q_GROUP + g) * HEAD_DIM:(c * Q_GROUP + g + 1) * HEAD_DIM]
                     for g in range(Q_GROUP)], axis=0)
                s = lax.dot_general(qc, kc, (((1,), (1,)), ((), ())), preferred_element_type=F32)
                yield
                probs = []
                for g in range(Q_GROUP):
                    sg = jnp.where(msk, s[g * WINDOW:(g + 1) * WINDOW], NEG_INF)
                    p, _ = _softmax_rows_with_sink(sg, sinks_ref[c * Q_GROUP + g] * LOG2E)
                    probs.append(p.astype(BF16))
                    if g % 2 == 1:
                        yield
                o = jnp.dot(jnp.concatenate(probs, axis=0), vc, preferred_element_type=F32)
                for g in range(Q_GROUP):
                    heads[c * Q_GROUP + g] = o[g * WINDOW:(g + 1) * WINDOW]
                yield
            attn_blocks[j] = jnp.concatenate(heads, axis=1)

    def scope_tail(sc):
        rows = slice(sc * PROMPT_SCOPE, (sc + 1) * PROMPT_SCOPE)
        attn = jnp.concatenate(attn_blocks[sc * blocks_per_scope:(sc + 1) * blocks_per_scope], axis=0)
        h = yield from _mix_out_stages(x[rows], attn, conv_pre[sc], lcg_ref[...], lcb_ref[...],
                                       w_out_ref, b_out_ref[...], l1g_ref[...], l1b_ref[...])
        route, area, cnt = yield from _route_and_sort_stages(h, w_r_ref[...])
        h_ref[0, rows, :] = h
        route_ref[rows, :] = route
        xs_ref[sc * AREA_ROWS:(sc + 1) * AREA_ROWS, :] = area
        cnt_ref[sc] = jnp.broadcast_to(cnt, (SUBLANES, LANES))

    _round_robin(*[scope_attention(sc) for sc in range(n_sc)])
    _round_robin(*[scope_tail(sc) for sc in range(n_sc)])


def _mix_sample_kernel(sinks_ref, x_ref, rope_ref, ck_ref, cv_ref, st_ref, w_in_ref, b_in_ref, w_sw_ref, b_sw_ref,
                       conv_w_ref, conv_b_ref, lcg_ref, lcb_ref, w_out_ref, b_out_ref, l1g_ref, l1b_ref,
                       w_r_ref,
                       h_ref, nk_ref, nv_ref, nc_ref, xs_ref, route_ref, cnt_ref,
                       h_scr):
    tb = x_ref.shape[0]
    n_tok = h_scr.shape[0]
    step = pl.program_id(0)
    x = x_ref[...]
    xb = x.astype(BF16)
    u = _in_proj_glu(xb, w_in_ref, b_in_ref)
    q, k_new, v_new = _in_proj_qkv(xb, w_in_ref, b_in_ref, w_sw_ref, b_sw_ref, rope_ref[...])

    qb = q.astype(BF16)
    knb = k_new.astype(BF16).astype(F32)
    vnb = v_new.astype(BF16).astype(F32)
    gq = SAMPLE_GRP * Q_GROUP
    nkeys = SAMPLE_GRP * WINDOW
    r_i = lax.broadcasted_iota(jnp.int32, (gq, nkeys), 0)
    c_i = lax.broadcasted_iota(jnp.int32, (gq, nkeys), 1)
    window_shift, grp_shift = WINDOW.bit_length() - 1, SAMPLE_GRP.bit_length() - 1
    msk = ((c_i >> window_shift) == (r_i & (SAMPLE_GRP - 1))) & ((c_i & (WINDOW - 1)) >= 1)
    r_col = lax.broadcasted_iota(jnp.int32, (gq, 1), 0) >> grp_shift
    attn_groups = [None] * (tb // SAMPLE_GRP)

    def group_attention(grp):
        r0 = grp * SAMPLE_GRP
        kc_all = ck_ref[r0:r0 + SAMPLE_GRP].reshape(nkeys, KV_DIM).astype(BF16)
        vc_all = cv_ref[r0:r0 + SAMPLE_GRP].reshape(nkeys, KV_DIM).astype(BF16)
        yield
        heads = [None] * N_HEADS
        for c in range(N_KV_HEADS):
            lanes = slice(c * HEAD_DIM, (c + 1) * HEAD_DIM)
            qc = jnp.concatenate(
                [qb[r0:r0 + SAMPLE_GRP, (c * Q_GROUP + g) * HEAD_DIM:(c * Q_GROUP + g + 1) * HEAD_DIM]
                 for g in range(Q_GROUP)], axis=0)
            s = lax.dot_general(qc, kc_all[:, lanes], (((1,), (1,)), ((), ())), preferred_element_type=F32)
            s = jnp.where(msk, s, NEG_INF)
            kn = jnp.concatenate([knb[r0:r0 + SAMPLE_GRP, lanes]] * Q_GROUP, axis=0)
            vn = jnp.concatenate([vnb[r0:r0 + SAMPLE_GRP, lanes]] * Q_GROUP, axis=0)
            s_new = jnp.sum(qc.astype(F32) * kn, axis=-1, keepdims=True)
            sink = jnp.zeros((gq, 1), F32)
            for g in range(Q_GROUP):
                sink = jnp.where(r_col == g, sinks_ref[c * Q_GROUP + g] * LOG2E, sink)
            yield
            p, p_new = _softmax_rows_with_sink(s, sink, s_new)
            yield
            o = (jnp.dot(p.astype(BF16), vc_all[:, lanes], preferred_element_type=F32)
                 + p_new.astype(BF16).astype(F32) * vn)
            for g in range(Q_GROUP):
                heads[c * Q_GROUP + g] = o[g * SAMPLE_GRP:(g + 1) * SAMPLE_GRP]
            yield
        attn_groups[grp] = jnp.concatenate(heads, axis=1)

    _round_robin(*[group_attention(grp) for grp in range(len(attn_groups))])
    attn = jnp.concatenate(attn_groups, axis=0)

    st = st_ref[...]
    acc = jnp.sum(st * conv_w_ref[0:CONV_WIDTH - 1, :][None], axis=1)
    conv_pre = acc + u * conv_w_ref[CONV_WIDTH - 1:CONV_WIDTH, :] + conv_b_ref[...]

    nk_ref[:, 0:WINDOW - 1, :] = ck_ref[:, 1:WINDOW, :]
    nv_ref[:, 0:WINDOW - 1, :] = cv_ref[:, 1:WINDOW, :]
    nc_ref[:, 0:CONV_WIDTH - 2, :] = st_ref[:, 1:CONV_WIDTH - 1, :]
    for b in range(tb):
        nk_ref[b, WINDOW - 1:WINDOW, :] = k_new[b:b + 1, :]
        nv_ref[b, WINDOW - 1:WINDOW, :] = v_new[b:b + 1, :]
        nc_ref[b, CONV_WIDTH - 2:CONV_WIDTH - 1, :] = u[b:b + 1, :]

    h = _drain(_mix_out_stages(x, attn, conv_pre, lcg_ref[...], lcb_ref[...], w_out_ref, b_out_ref[...],
                               l1g_ref[...], l1b_ref[...]))
    h_ref[...] = h
    h_scr[pl.ds(pl.multiple_of(step * tb, tb), tb), :] = h

    @pl.when(step == pl.num_programs(0) - 1)
    def _():
        route, area, cnt = _drain(_route_and_sort_stages(h_scr[...], w_r_ref[...]))
        route_ref[...] = route
        xs_ref[...] = area
        cnt_ref[0] = jnp.broadcast_to(cnt, (SUBLANES, LANES))


GATHER_BUFS = 8
SCATTER_BUFS = 8
SCATTER_LAG = 2
ZERO_BITS = (64, 32, 16, 8, 4, 2, 1)
DUMP_CHUNKS = SCATTER_BUFS * BLOCK_CHUNKS
ZERO_ROWS = max(ZERO_BITS[0], DUMP_CHUNKS) * SUBLANES


def _moe_ffn_kernel(src_ref, dst_ref, bgen_ref, elist_ref, meta_ref, used_ref,
                    xs_hbm, w1_hbm, w3_hbm, w2_hbm,
                    ys_hbm,
                    xbuf, ybuf, hbuf, zbuf, wst1, wst3, wst2, w1b, w3b, w2b,
                    sem_in, sem_out, sem_zero, sem_dump, sem_w):
    n_areas = used_ref.shape[0]
    total = meta_ref[0]
    n_gen = meta_ref[1]
    dump_row0 = n_areas * AREA_ROWS
    lookahead = GATHER_BUFS - 1
    any_rows = pl.ds(0, SUBLANES)

    def chunk_at(ref, idx):
        return pl.ds(pl.multiple_of(ref[idx] * SUBLANES, SUBLANES), SUBLANES)

    def in_copy(rows, sl, c):
        return pltpu.make_async_copy(xs_hbm.at[rows], xbuf.at[sl, pl.ds(c * SUBLANES, SUBLANES)], sem_in.at[sl])

    def out_copy(rows, sl, c):
        return pltpu.make_async_copy(ybuf.at[sl, pl.ds(c * SUBLANES, SUBLANES)], ys_hbm.at[rows], sem_out.at[sl])

    def start_gather(blk):
        for c in range(BLOCK_CHUNKS):
            in_copy(chunk_at(src_ref, blk * BLOCK_CHUNKS + c), blk % GATHER_BUFS, c).start()

    def wait_gather(blk):
        for c in range(BLOCK_CHUNKS):
            in_copy(any_rows, blk % GATHER_BUFS, c).wait()

    def start_scatter(blk):
        for c in range(BLOCK_CHUNKS):
            out_copy(chunk_at(dst_ref, (blk + SCATTER_LAG) * BLOCK_CHUNKS + c), blk % SCATTER_BUFS, c).start()

    def wait_scatter(sl):
        for c in range(BLOCK_CHUNKS):
            out_copy(any_rows, sl, c).wait()

    def load_rows(blk):
        return _unpack_rows(xbuf[blk % GATHER_BUFS])

    def issue_dmas(blk):
        start_scatter(blk - SCATTER_LAG)
        start_gather(blk + lookahead)

    def up_gate(blk, rows, ws):
        x_lo, x_hi = rows
        a1 = (jnp.dot(x_lo, w1b[ws, 0:HALF, :], preferred_element_type=F32)
              + jnp.dot(x_hi, w1b[ws, HALF:, :], preferred_element_type=F32))
        a3 = (jnp.dot(x_lo, w3b[ws, 0:HALF, :], preferred_element_type=F32)
              + jnp.dot(x_hi, w3b[ws, HALF:, :], preferred_element_type=F32))
        hbuf[blk % SCATTER_BUFS] = (a1 * jax.nn.sigmoid(a1) * a3).astype(BF16)

    def down(blk, ws):
        y = jnp.dot(hbuf[blk % SCATTER_BUFS], w2b[ws], preferred_element_type=F32)
        ybuf[blk % SCATTER_BUFS] = _pack_rows(y.astype(BF16).astype(F32))

    def for_each_weight_copy(gen, fn):
        ws = gen % 2
        expert = elist_ref[gen]
        for hbm, staging in ((w1_hbm, wst1), (w3_hbm, wst3), (w2_hbm, wst2)):
            fn(pltpu.make_async_copy(hbm.at[expert], staging.at[ws], sem_w.at[ws]))

    def switch_weights(gen):
        ws = gen % 2
        for_each_weight_copy(gen, lambda cp: cp.wait())
        w1b[ws] = wst1[ws].astype(BF16)
        w3b[ws] = wst3[ws].astype(BF16)
        w2b[ws] = wst2[ws].astype(BF16)

        @pl.when(gen + 1 < n_gen)
        def _():
            for_each_weight_copy(gen + 1, lambda cp: cp.start())

    def for_each_tail_piece(fn):
        def area_body(s, carry):
            used = used_ref[s]
            tail = AREA_CHUNKS - used
            row = (s * AREA_CHUNKS + used) * SUBLANES
            for bit in ZERO_BITS:
                take = (tail & bit) != 0

                @pl.when(take)
                def _(row=row, bit=bit):
                    fn(pltpu.make_async_copy(zbuf.at[pl.ds(0, bit * SUBLANES)],
                                             ys_hbm.at[pl.ds(pl.multiple_of(row, SUBLANES), bit * SUBLANES)],
                                             sem_zero))
                row = row + jnp.where(take, bit * SUBLANES, 0)
            return carry
        lax.fori_loop(0, n_areas, area_body, 0)

    @pl.when(n_gen > 0)
    def _():
        for_each_weight_copy(0, lambda cp: cp.start())

    zbuf[...] = jnp.zeros(zbuf.shape, U32)
    ybuf[...] = jnp.zeros(ybuf.shape, U32)
    dump = pltpu.make_async_copy(zbuf.at[pl.ds(0, DUMP_CHUNKS * SUBLANES)],
                                 ys_hbm.at[pl.ds(dump_row0, DUMP_CHUNKS * SUBLANES)], sem_dump)
    dump.start()
    dump.wait()
    for_each_tail_piece(lambda cp: cp.start())
    for sl in range(SCATTER_BUFS - SCATTER_LAG):
        for c in range(BLOCK_CHUNKS):
            out_copy(pl.ds(dump_row0 + (sl * BLOCK_CHUNKS + c) * SUBLANES, SUBLANES), sl, c).start()
    for k in range(lookahead):
        start_gather(k)

    @pl.when(total > 0)
    def _():
        switch_weights(0)
        wait_gather(0)
        rows = load_rows(0)
        issue_dmas(0)
        up_gate(0, rows, 0)

        def block_body(blk, carry):
            gen = bgen_ref[blk]
            gen_prev = bgen_ref[blk - 1]

            @pl.when(gen != gen_prev)
            def _():
                switch_weights(gen)

            wait_gather(blk)
            wait_scatter((blk - 1) % SCATTER_BUFS)
            rows = load_rows(blk)
            issue_dmas(blk)
            down(blk - 1, gen_prev % 2)
            up_gate(blk, rows, gen % 2)
            return carry

        lax.fori_loop(1, total, block_body, 0)
        wait_scatter((total - 1) % SCATTER_BUFS)
        down(total - 1, bgen_ref[total - 1] % 2)

    for k in range(SCATTER_LAG):
        start_scatter(total - SCATTER_LAG + k)
    for sl in range(SCATTER_BUFS):
        wait_scatter(sl)
    for k in range(lookahead):
        wait_gather(total + k)
    for_each_tail_piece(lambda cp: cp.wait())


def _moe_combine_kernel(h_ref, route_ref, ys_ref, l2g_ref, l2b_ref, y_ref):
    n = h_ref.shape[0]
    scope = n // (ys_ref.shape[0] // AREA_ROWS)
    s_iota = lax.broadcasted_iota(jnp.int32, (scope, AREA_ROWS), 1).astype(F32)
    for sc in range(n // scope):
        rows = slice(sc * scope, (sc + 1) * scope)
        route = route_ref[rows, :]
        slot1 = route[:, 0:1]
        slot2 = route[:, 1:2]
        gate1 = route[:, 2:3]
        gate2 = route[:, 3:4]
        gmat = (jnp.where(s_iota == slot1, gate1, 0.0) + jnp.where(s_iota == slot2, gate2, 0.0)).astype(BF16)
        y_lo, y_hi = _unpack_rows(ys_ref[sc * AREA_ROWS:(sc + 1) * AREA_ROWS, :])
        f = jnp.concatenate([jnp.dot(gmat, y_lo, preferred_element_type=F32),
                             jnp.dot(gmat, y_hi, preferred_element_type=F32)], axis=1)
        y_ref[rows, :] = _layer_norm(ALPHA * h_ref[rows, :] + f, l2g_ref[...], l2b_ref[...])


def _excl_cumsum(a, axis):
    return jnp.cumsum(a, axis=axis) - a


def _plan_blocks(cnt, q_max):
    n_areas = cnt.shape[0]
    c8 = (cnt + (SUBLANES - 1)) // SUBLANES
    used8 = jnp.sum(c8, axis=1)
    base8 = jnp.arange(n_areas, dtype=jnp.int32) * AREA_CHUNKS
    src_runs = (base8[:, None] + _excl_cumsum(c8, 1)).T
    len_runs = c8.T
    tot = jnp.sum(len_runs, axis=1)
    ptot = (tot + BLOCK_CHUNKS - 1) // BLOCK_CHUNKS * BLOCK_CHUNKS
    pstart = _excl_cumsum(ptot, 0)
    dst_runs = pstart[:, None] + _excl_cumsum(len_runs, 1)
    off = (src_runs - dst_runs).reshape(-1)
    diff = off - jnp.concatenate([jnp.zeros((1,), jnp.int32), off[:-1]])
    q = jnp.arange(q_max, dtype=jnp.int32)
    src = q + jnp.sum(jnp.where(dst_runs.reshape(1, -1) <= q[:, None], diff[None, :], 0), axis=1)
    valid = jnp.any((pstart[None, :] <= q[:, None]) & (q[:, None] < (pstart + tot)[None, :]), axis=1)
    zero_chunk = AREA_CHUNKS - 1
    ahead = jnp.full(((GATHER_BUFS - 1) * BLOCK_CHUNKS,), zero_chunk, jnp.int32)
    src_tab = jnp.concatenate([jnp.where(valid, src, zero_chunk).astype(jnp.int32), ahead])
    qv = jnp.arange(-SCATTER_LAG * BLOCK_CHUNKS, q_max, dtype=jnp.int32)
    dump = n_areas * AREA_CHUNKS + ((qv // BLOCK_CHUNKS) % SCATTER_BUFS) * BLOCK_CHUNKS + qv % BLOCK_CHUNKS
    lead = jnp.zeros((SCATTER_LAG * BLOCK_CHUNKS,), jnp.int32)
    dst_tab = jnp.where(jnp.concatenate([lead > 0, valid]), jnp.concatenate([lead, src]), dump).astype(jnp.int32)
    nblk = ptot // BLOCK_CHUNKS
    bstart = pstart // BLOCK_CHUNKS
    has = nblk > 0
    gen_of = jnp.cumsum(has.astype(jnp.int32)) - 1
    experts = jnp.arange(N_EXPERTS, dtype=jnp.int32)
    elist = jnp.sum(jnp.where(has[None, :] & (gen_of[None, :] == experts[:, None]), experts[None, :], 0), axis=1)
    blk = jnp.arange(q_max // BLOCK_CHUNKS, dtype=jnp.int32)
    owner = (bstart[None, :] <= blk[:, None]) & (blk[:, None] < (bstart + nblk)[None, :])
    bgen = jnp.sum(jnp.where(owner, gen_of[None, :], 0), axis=1)
    meta = jnp.stack([jnp.sum(nblk), jnp.sum(has.astype(jnp.int32))])
    return (src_tab, dst_tab, bgen.astype(jnp.int32), elist.astype(jnp.int32), meta.astype(jnp.int32),
            used8.astype(jnp.int32))


PREP_STEPS = 4


def _swap_head_halves(a):
    half = HEAD_DIM // 2
    pieces = []
    for g in range(a.shape[-1] // LANES):
        p = a[:, g * LANES:(g + 1) * LANES]
        lane = lax.broadcasted_iota(jnp.int32, p.shape, 1)
        pieces.append(jnp.where((lane & half) == 0, pltpu.roll(p, LANES - half, 1), pltpu.roll(p, half, 1)))
    return jnp.concatenate(pieces, axis=-1)


def _glu_blocks(a):
    lo = ATTN_DIM + 2 * KV_DIM
    pieces = []
    for cb in range(CONV_CH // LANES):
        pieces.append(a[:, lo + cb * LANES:lo + (cb + 1) * LANES])
        pieces.append(a[:, lo + CONV_CH + cb * LANES:lo + CONV_CH + (cb + 1) * LANES])
    return jnp.concatenate(pieces, axis=-1)


def _prep_weights_kernel(w_in_ref, b_in_ref, w_out_ref, w_in_b_ref, w_sw_ref, w_glu_ref, w_out_b_ref,
                         b_sw_ref, b_glu_ref):
    qk = ATTN_DIM + KV_DIM
    w = w_in_ref[...]
    w_in_b_ref[...] = w.astype(BF16)
    w_sw_ref[...] = _swap_head_halves(w[:, :qk]).astype(BF16)
    w_glu_ref[...] = _glu_blocks(w).astype(BF16)
    w_out_b_ref[...] = w_out_ref[...].astype(BF16)
    b = jnp.broadcast_to(b_in_ref[...], (SUBLANES, IN_DIM))
    b_sw_ref[...] = _swap_head_halves(b[:, :qk])[0:1]
    b_glu_ref[...] = _glu_blocks(b)[0:1]


def _rope_tables(pos):
    half = HEAD_DIM // 2
    inv = (ROPE_THETA ** (-np.arange(half, dtype=np.float64) / half)).astype(np.float32)
    ang = (np.asarray(pos, np.float32)[:, None] * inv).astype(np.float64)
    cos = np.cos(ang)
    sin = np.sin(ang)
    reps = LANES // HEAD_DIM
    cos_t = np.concatenate([cos, cos] * reps, axis=1)
    sin_t = np.concatenate([-sin, sin] * reps, axis=1)
    q_scale = ATTN_SCALE * LOG2E
    return jnp.asarray(np.concatenate([cos_t, sin_t, cos_t * q_scale, sin_t * q_scale], axis=1), F32)


def _full(shape):
    nd = len(shape)
    return pl.BlockSpec(shape, lambda *_: (0,) * nd)


def kernel(x_prompt, x_sample, cache_k, cache_v, state_conv, w_in, b_in, sinks, conv_w, conv_b, ln_conv_g,
           ln_conv_b, w_out, b_out, ln1_g, ln1_b, w_router_group, w_router_expert, w1, w3, w2, ln2_g, ln2_b):
    assert w_in.shape[0] == DEPTH
    bsz, seq, _ = x_prompt.shape
    dec_b, dec_t, _ = x_sample.shape
    assert dec_t == 1 and seq % TILE_Q == 0 and dec_b % SAMPLE_TB == 0 and 2 * dec_b + 7 * N_EXPERTS <= AREA_ROWS
    assert (bsz * seq) % COMBINE_TILE == 0
    n_prompt = bsz * seq
    tiles = seq // TILE_Q
    spt = TILE_Q // PROMPT_SCOPE
    p_areas = n_prompt // PROMPT_SCOPE
    n_areas = p_areas + spt
    q_max = (n_areas * AREA_CHUNKS + N_EXPERTS * (BLOCK_CHUNKS - 1) + BLOCK_CHUNKS - 1) // BLOCK_CHUNKS * BLOCK_CHUNKS

    row = lambda a: a.reshape(1, -1)
    cparams = lambda sem: pltpu.CompilerParams(dimension_semantics=sem, vmem_limit_bytes=VMEM_LIMIT)

    qk = ATTN_DIM + KV_DIM
    wrows = D_MODEL // PREP_STEPS
    wspec = lambda cols: pl.BlockSpec((wrows, cols), lambda t: (t, 0))
    w_in_b, w_sw_b, w_glu_b, w_out_b, b_sw, b_glu = pl.pallas_call(
        _prep_weights_kernel,
        grid=(PREP_STEPS,),
        in_specs=[wspec(IN_DIM), _full((1, IN_DIM)), wspec(D_MODEL)],
        out_specs=[wspec(IN_DIM), wspec(qk), wspec(2 * CONV_CH), wspec(D_MODEL),
                   _full((1, qk)), _full((1, 2 * CONV_CH))],
        out_shape=[jax.ShapeDtypeStruct((D_MODEL, IN_DIM), BF16), jax.ShapeDtypeStruct((D_MODEL, qk), BF16),
                   jax.ShapeDtypeStruct((D_MODEL, 2 * CONV_CH), BF16), jax.ShapeDtypeStruct((D_MODEL, D_MODEL), BF16),
                   jax.ShapeDtypeStruct((1, qk), F32), jax.ShapeDtypeStruct((1, 2 * CONV_CH), F32)],
        compiler_params=cparams(("arbitrary",)),
        name="prep_weights",
    )(w_in[0], row(b_in[0]), w_out[0])
    w_r = jnp.concatenate([w_router_group[0], w_router_expert[0].reshape(D_MODEL, N_EXPERTS),
                           jnp.zeros((D_MODEL, LANES - ELANE0 - N_EXPERTS), F32)], axis=1).astype(BF16)
    shared = (w_in_b, row(b_in[0]), w_sw_b, b_sw, conv_w[0], row(conv_b[0]),
              row(ln_conv_g[0]), row(ln_conv_b[0]), w_out_b, row(b_out[0]), row(ln1_g[0]), row(ln1_b[0]), w_r)
    shared_specs = [_full(a.shape) for a in shared]
    glu = (w_glu_b, b_glu)
    glu_specs = [_full(a.shape) for a in glu]
    assert len(glu) + len(shared) == N_PROMPT_WEIGHTS
    smem = pl.BlockSpec(memory_space=pltpu.SMEM)
    rope_p = _rope_tables(np.arange(seq))
    rope_s = _rope_tables(PAST_LEN + np.arange(dec_t))

    ck = cache_k[0].reshape(dec_b, WINDOW, KV_DIM)
    cv = cache_v[0].reshape(dec_b, WINDOW, KV_DIM)
    tb = SAMPLE_TB
    conv_state_spec = pl.BlockSpec((None, tb, CONV_WIDTH - 1, CONV_CH), lambda t: (0, t, 0, 0))
    h_s, nk_s, nv_s, nc_s, xs_s, route_s, cnt_s = pl.pallas_call(
        _mix_sample_kernel,
        grid=(dec_b // tb,),
        in_specs=[smem,
                  pl.BlockSpec((tb, None, D_MODEL), lambda t: (t, 0, 0)),
                  _full(rope_s.shape),
                  pl.BlockSpec((tb, WINDOW, KV_DIM), lambda t: (t, 0, 0)),
                  pl.BlockSpec((tb, WINDOW, KV_DIM), lambda t: (t, 0, 0)),
                  conv_state_spec] + shared_specs,
        out_specs=[pl.BlockSpec((tb, D_MODEL), lambda t: (t, 0)),
                   pl.BlockSpec((tb, WINDOW, KV_DIM), lambda t: (t, 0, 0)),
                   pl.BlockSpec((tb, WINDOW, KV_DIM), lambda t: (t, 0, 0)),
                   conv_state_spec,
                   _full((AREA_ROWS, HALF)),
                   _full((dec_b, LANES)),
                   _full((1, SUBLANES, LANES))],
        out_shape=[jax.ShapeDtypeStruct((dec_b, D_MODEL), F32),
                   jax.ShapeDtypeStruct((dec_b, WINDOW, KV_DIM), F32),
                   jax.ShapeDtypeStruct((dec_b, WINDOW, KV_DIM), F32),
                   jax.ShapeDtypeStruct((DEPTH, dec_b, CONV_WIDTH - 1, CONV_CH), F32),
                   jax.ShapeDtypeStruct((AREA_ROWS, HALF), U32),
                   jax.ShapeDtypeStruct((dec_b, LANES), F32),
                   jax.ShapeDtypeStruct((1, SUBLANES, LANES), F32)],
        scratch_shapes=[pltpu.VMEM((dec_b, D_MODEL), F32)],
        compiler_params=cparams(("arbitrary",)),
        name="mix_sample",
    )(sinks[0], x_sample, rope_s, ck, cv, state_conv, *shared)

    n_real = bsz * tiles
    real = lambda t: jnp.minimum(t, n_real - 1)
    h_p, xs, route_p, cnt_p, nk_p, nv_p, nc_p = pl.pallas_call(
        functools.partial(_mix_prompt_kernel, tiles=tiles),
        grid=(n_real + 1,),
        in_specs=[smem,
                  pl.BlockSpec((1, TILE_Q, D_MODEL), lambda t: (real(t) // tiles, real(t) % tiles, 0)),
                  pl.BlockSpec((TILE_Q, 4 * LANES), lambda t: (real(t) % tiles, 0)),
                  _full((AREA_ROWS, HALF))] + glu_specs + shared_specs,
        out_specs=[pl.BlockSpec((1, TILE_Q, D_MODEL), lambda t: (real(t) // tiles, real(t) % tiles, 0)),
                   pl.BlockSpec((spt * AREA_ROWS, HALF), lambda t: (t, 0)),
                   pl.BlockSpec((TILE_Q, LANES), lambda t: (real(t), 0)),
                   pl.BlockSpec((spt, SUBLANES, LANES), lambda t: (t, 0, 0)),
                   pl.BlockSpec((1, WINDOW, KV_DIM), lambda t: (real(t) // tiles, 0, 0)),
                   pl.BlockSpec((1, WINDOW, KV_DIM), lambda t: (real(t) // tiles, 0, 0)),
                   pl.BlockSpec((1, CONV_WIDTH - 1, CONV_CH), lambda t: (real(t) // tiles, 0, 0))],
        out_shape=[jax.ShapeDtypeStruct((bsz, seq, D_MODEL), F32),
                   jax.ShapeDtypeStruct((n_areas * AREA_ROWS, HALF), U32),
                   jax.ShapeDtypeStruct((n_prompt, LANES), F32),
                   jax.ShapeDtypeStruct((n_areas, SUBLANES, LANES), F32),
                   jax.ShapeDtypeStruct((bsz, WINDOW, KV_DIM), F32),
                   jax.ShapeDtypeStruct((bsz, WINDOW, KV_DIM), F32),
                   jax.ShapeDtypeStruct((bsz, CONV_WIDTH - 1, CONV_CH), F32)],
        scratch_shapes=[pltpu.VMEM((WINDOW, KV_DIM), F32), pltpu.VMEM((WINDOW, KV_DIM), F32),
                        pltpu.VMEM((TILE_Q + CONV_PRE, CONV_CH), F32),
                        pltpu.VMEM((SUBLANES - 1, TILE_Q + CONV_SHIFT_EXTRA, CONV_CH), F32)],
        compiler_params=cparams(("arbitrary",)),
        name="mix_prompt",
    )(sinks[0], x_prompt, rope_p, xs_s, *glu, *shared)

    cnt = jnp.concatenate([cnt_p[:p_areas, 0], cnt_s[:, 0], cnt_p[p_areas + 1:, 0]], axis=0)
    cnt = cnt[:, ELANE0:ELANE0 + N_EXPERTS].astype(jnp.int32)
    plan = _plan_blocks(cnt, q_max)
    hbm = pl.BlockSpec(memory_space=pl.ANY)
    up_shape, down_shape = (2, D_MODEL, D_FF_EXPERT), (2, D_FF_EXPERT, D_MODEL)
    ys = pl.pallas_call(
        _moe_ffn_kernel,
        grid_spec=pltpu.PrefetchScalarGridSpec(
            num_scalar_prefetch=len(plan),
            grid=(1,),
            in_specs=[hbm, hbm, hbm, hbm],
            out_specs=hbm,
            scratch_shapes=[pltpu.VMEM((GATHER_BUFS, BLOCK_ROWS, HALF), U32),
                            pltpu.VMEM((SCATTER_BUFS, BLOCK_ROWS, HALF), U32),
                            pltpu.VMEM((SCATTER_BUFS, BLOCK_ROWS, D_FF_EXPERT), BF16),
                            pltpu.VMEM((ZERO_ROWS, HALF), U32),
                            pltpu.VMEM(up_shape, F32), pltpu.VMEM(up_shape, F32), pltpu.VMEM(down_shape, F32),
                            pltpu.VMEM(up_shape, BF16), pltpu.VMEM(up_shape, BF16), pltpu.VMEM(down_shape, BF16),
                            pltpu.SemaphoreType.DMA((GATHER_BUFS,)), pltpu.SemaphoreType.DMA((SCATTER_BUFS,)),
                            pltpu.SemaphoreType.DMA(()), pltpu.SemaphoreType.DMA(()),
                            pltpu.SemaphoreType.DMA((2,))]),
        out_shape=jax.ShapeDtypeStruct((n_areas * AREA_ROWS + DUMP_CHUNKS * SUBLANES, HALF), U32),
        compiler_params=cparams(("arbitrary",)),
        name="moe_ffn",
    )(*plan, xs, w1[0], w3[0], w2[0])

    def combine(h2d, route, first_area, scope, tile):
        n = h2d.shape[0]
        apt = tile // scope
        return pl.pallas_call(
            _moe_combine_kernel,
            grid=(n // tile,),
            in_specs=[pl.BlockSpec((tile, D_MODEL), lambda t: (t, 0)),
                      pl.BlockSpec((tile, LANES), lambda t: (t, 0)),
                      pl.BlockSpec((apt * AREA_ROWS, HALF), lambda t: (first_area // apt + t, 0)),
                      _full((1, D_MODEL)), _full((1, D_MODEL))],
            out_specs=pl.BlockSpec((tile, D_MODEL), lambda t: (t, 0)),
            out_shape=jax.ShapeDtypeStruct((n, D_MODEL), F32),
            compiler_params=cparams(("arbitrary",)),
            name="moe_combine",
        )(h2d, route, ys, row(ln2_g[0]), row(ln2_b[0]))

    y_p = combine(h_p.reshape(n_prompt, D_MODEL), route_p, 0, PROMPT_SCOPE, COMBINE_TILE)
    y_s = combine(h_s, route_s, p_areas, dec_b, dec_b)

    kv_shape = lambda n: (DEPTH, n, WINDOW, N_KV_HEADS, HEAD_DIM)
    return (y_p.reshape(bsz, seq, D_MODEL), y_s.reshape(dec_b, dec_t, D_MODEL),
            nk_p.reshape(kv_shape(bsz)), nv_p.reshape(kv_shape(bsz)), nc_p[None],
            nk_s.reshape(kv_shape(dec_b)), nv_s.reshape(kv_shape(dec_b)), nc_s)
```

```python
import functools

import jax
import jax.numpy as jnp
import numpy as np
from jax import lax
from jax.experimental import pallas as pl
from jax.experimental.pallas import tpu as pltpu

F32 = jnp.float32
BF16 = jnp.bfloat16
U32 = jnp.uint32

D_MODEL = 1024
N_HEADS = 8
N_KV_HEADS = 2
HEAD_DIM = 64
Q_GROUP = N_HEADS // N_KV_HEADS
ATTN_DIM = N_HEADS * HEAD_DIM
KV_DIM = N_KV_HEADS * HEAD_DIM
WINDOW = 128
ROPE_THETA = 10000.0
ATTN_SCALE = HEAD_DIM ** -0.5
CONV_CH = D_MODEL - ATTN_DIM
CONV_WIDTH = 31
IN_DIM = ATTN_DIM + 2 * KV_DIM + 2 * CONV_CH
N_GROUPS = 4
EXPERTS_PER_GROUP = 8
N_EXPERTS = N_GROUPS * EXPERTS_PER_GROUP
D_FF_EXPERT = 256
LN_EPS = 1e-5
NEG_INF = -1e30
DEPTH = 1
ALPHA = (2.0 * DEPTH) ** 0.25
PAST_LEN = 16384
LOG2E = 1.4426950408889634

LANES = 128
SUBLANES = 8
HALF = D_MODEL // 2
ELANE0 = N_GROUPS
PROMPT_SCOPE = 256
AREA_ROWS = -(-(2 * PROMPT_SCOPE + (SUBLANES - 1) * N_EXPERTS) // LANES) * LANES
AREA_CHUNKS = AREA_ROWS // SUBLANES
BLOCK_CHUNKS = 32
BLOCK_ROWS = BLOCK_CHUNKS * SUBLANES
TILE_Q = 512
COMBINE_TILE = 1024
CONV_PRE = 32
CONV_SHIFT_EXTRA = 24
SAMPLE_TB = 32
SAMPLE_GRP = 8
VMEM_LIMIT = 56 * 1024 * 1024


def _layer_norm(x, g, b):
    mu = jnp.mean(x, axis=-1, keepdims=True)
    xc = x - mu
    var = jnp.mean(xc * xc, axis=-1, keepdims=True)
    return xc * lax.rsqrt(var + LN_EPS) * g + b


def _rope(x, x_swapped, cos, sin_signed):
    pieces = [x[:, g * LANES:(g + 1) * LANES] * cos + x_swapped[:, g * LANES:(g + 1) * LANES] * sin_signed
              for g in range(x.shape[-1] // LANES)]
    return pieces[0] if len(pieces) == 1 else jnp.concatenate(pieces, axis=-1)


def _in_proj_glu(xb, w_in_ref, b_in_ref):
    lo, hi = ATTN_DIM + 2 * KV_DIM, IN_DIM
    cacg = jnp.dot(xb, w_in_ref[:, lo:hi], preferred_element_type=F32) + b_in_ref[:, lo:hi]
    return cacg[:, :CONV_CH] * jax.nn.sigmoid(cacg[:, CONV_CH:])


def _in_proj_qkv(xb, w_in_ref, b_in_ref, w_sw_ref, b_sw_ref, rope):
    o_v, o_e = ATTN_DIM + KV_DIM, ATTN_DIM + 2 * KV_DIM
    qkv = jnp.dot(xb, w_in_ref[:, 0:o_e], preferred_element_type=F32) + b_in_ref[:, 0:o_e]
    sw = jnp.dot(xb, w_sw_ref[...], preferred_element_type=F32) + b_sw_ref[...]
    k = _rope(qkv[:, ATTN_DIM:o_v], sw[:, ATTN_DIM:], rope[:, 0:LANES], rope[:, LANES:2 * LANES])
    q = _rope(qkv[:, :ATTN_DIM], sw[:, :ATTN_DIM], rope[:, 2 * LANES:3 * LANES], rope[:, 3 * LANES:])
    return q, k, qkv[:, o_v:]


def _pack_rows(v):
    lo = lax.bitcast_convert_type(v[:, :HALF], U32)
    hi = lax.bitcast_convert_type(v[:, HALF:], U32)
    return (lo >> 16) | hi


def _unpack_rows(w):
    lo = lax.bitcast_convert_type(w << 16, F32).astype(BF16)
    hi = lax.bitcast_convert_type(w & jnp.uint32(0xFFFF0000), F32).astype(BF16)
    return lo, hi


def _drain(stages):
    try:
        while True:
            next(stages)
    except StopIteration as stop:
        return stop.value


def _round_robin(*stage_generators):
    live = list(stage_generators)
    while live:
        for stages in list(live):
            try:
                next(stages)
            except StopIteration:
                live.remove(stages)


def _route_and_sort_stages(h, w_r):
    n = h.shape[0]
    hb = h.astype(BF16)
    logits = jnp.dot(hb, w_r, preferred_element_type=F32)
    yield
    lane = lax.broadcasted_iota(jnp.int32, (n, LANES), 1)
    lane_f = lane.astype(F32)
    is_g = lane < N_GROUPS
    lg = jnp.where(is_g, logits, -jnp.inf)
    gmax = jnp.max(lg, axis=-1, keepdims=True)
    yield
    gidx = jnp.min(jnp.where(lg == gmax, lane_f, float(LANES)), axis=-1, keepdims=True)
    p_g = 1.0 / jnp.sum(jnp.where(is_g, jnp.exp(logits - gmax), 0.0), axis=-1, keepdims=True)
    yield
    egrp = ((lane - ELANE0) >> (EXPERTS_PER_GROUP.bit_length() - 1)).astype(F32)
    emask = (lane >= ELANE0) & (lane < ELANE0 + N_EXPERTS) & (egrp == gidx)
    sel = jnp.where(emask, logits, -jnp.inf)
    v1 = jnp.max(sel, axis=-1, keepdims=True)
    yield
    i1 = jnp.min(jnp.where(sel == v1, lane_f, float(LANES)), axis=-1, keepdims=True)
    yield
    sel2 = jnp.where(lane_f == i1, -jnp.inf, sel)
    v2 = jnp.max(sel2, axis=-1, keepdims=True)
    yield
    i2 = jnp.min(jnp.where(sel2 == v2, lane_f, float(LANES)), axis=-1, keepdims=True)
    t = jnp.exp(v2 - v1)
    gate1 = p_g / (1.0 + t)
    gate2 = p_g * t / (1.0 + t)
    pick1 = lane_f == i1
    pick2 = lane_f == i2
    member = jnp.where(pick1 | pick2, 1.0, 0.0)
    yield

    r_i = lax.broadcasted_iota(jnp.int32, (n, n), 0)
    c_i = lax.broadcasted_iota(jnp.int32, (n, n), 1)
    lower = jnp.where(c_i < r_i, 1.0, 0.0).astype(BF16)
    e_r = lax.broadcasted_iota(jnp.int32, (LANES, LANES), 0)
    e_c = lax.broadcasted_iota(jnp.int32, (LANES, LANES), 1)
    upper = jnp.where(e_r < e_c, 1.0, 0.0).astype(BF16)
    before = jnp.dot(lower, member.astype(BF16), preferred_element_type=F32)
    cnt = jnp.sum(member, axis=0, keepdims=True)
    c8 = jnp.floor((cnt + (SUBLANES - 1)) * (1.0 / SUBLANES))
    c8b = jnp.broadcast_to(c8, (SUBLANES, LANES)).astype(BF16)
    off8 = jnp.dot(c8b, upper, preferred_element_type=F32)[0:1]
    yield
    slot_all = before + off8 * float(SUBLANES)
    slot1 = jnp.sum(jnp.where(pick1, slot_all, 0.0), axis=-1, keepdims=True)
    slot2 = jnp.sum(jnp.where(pick2, slot_all, 0.0), axis=-1, keepdims=True)
    yield
    s_iota = lax.broadcasted_iota(jnp.int32, (n, AREA_ROWS), 1).astype(F32)
    perm_t = jnp.where((s_iota == slot1) | (s_iota == slot2), 1.0, 0.0).astype(BF16)
    sorted_rows = lax.dot_general(perm_t, hb, (((0,), (0,)), ((), ())),
                                  preferred_element_type=F32)
    yield
    route = jnp.where(lane == 0, slot1, jnp.where(lane == 1, slot2, jnp.where(lane == 2, gate1,
                      jnp.where(lane == 3, gate2, 0.0))))
    return route, _pack_rows(sorted_rows), cnt


def _softmax_rows_with_sink(s, sink, extra=None):
    m = jnp.maximum(jnp.max(s, axis=-1, keepdims=True), sink)
    if extra is not None:
        m = jnp.maximum(m, extra)
    e = jnp.exp2(s - m)
    den = jnp.sum(e, axis=-1, keepdims=True) + jnp.exp2(sink - m)
    if extra is not None:
        ee = jnp.exp2(extra - m)
        den = den + ee
        return e / den, ee / den
    return e / den, None


def _mix_out_stages(x, attn, conv_pre, lcg, lcb, w_out_ref, b_out, l1g, l1b):
    conv = _layer_norm(conv_pre, lcg, lcb)
    conv = conv * jax.nn.sigmoid(conv)
    yield
    mixed = jnp.concatenate([attn.astype(BF16), conv.astype(BF16)], axis=1)
    mix = jnp.dot(mixed, w_out_ref[...], preferred_element_type=F32) + b_out
    yield
    h = _layer_norm(ALPHA * x + mix, l1g, l1b)
    yield
    return h


N_PROMPT_WEIGHTS = 15


def _mix_prompt_kernel(sinks_ref, x_ref, rope_ref, xs_s_ref, *rest, tiles):
    _, xs_ref, _, cnt_ref = rest[N_PROMPT_WEIGHTS:N_PROMPT_WEIGHTS + 4]
    step = pl.program_id(0)
    n_real = pl.num_programs(0) - 1

    @pl.when(step < n_real)
    def _():
        _mix_prompt_tile(sinks_ref, x_ref, rope_ref, *rest, i=step % tiles)

    @pl.when(step == n_real)
    def _():
        xs_ref[0:AREA_ROWS, :] = xs_s_ref[...]
        xs_ref[AREA_ROWS:, :] = jnp.zeros((xs_ref.shape[0] - AREA_ROWS, HALF), U32)
        cnt_ref[...] = jnp.zeros(cnt_ref.shape, F32)


def _mix_prompt_tile(sinks_ref, x_ref, rope_ref, w_glu_ref, b_glu_ref,
                     w_in_ref, b_in_ref, w_sw_ref, b_sw_ref, conv_w_ref, conv_b_ref,
                     lcg_ref, lcb_ref, w_out_ref, b_out_ref, l1g_ref, l1b_ref, w_r_ref,
                     h_ref, xs_ref, route_ref, cnt_ref, nk_ref, nv_ref, nc_ref,
                     kprev, vprev, u_scr, u_shift, *, i):
    tq = x_ref.shape[1]
    pre = CONV_PRE
    n_sc = tq // PROMPT_SCOPE

    @pl.when(i == 0)
    def _():
        kprev[...] = jnp.zeros_like(kprev)
        vprev[...] = jnp.zeros_like(vprev)
        u_scr[0:pre, :] = jnp.zeros((pre, CONV_CH), F32)

    x = x_ref[0]
    xb = x.astype(BF16)

    def glu_block(cb):
        cols = slice(2 * LANES * cb, 2 * LANES * (cb + 1))
        vg = jnp.dot(xb, w_glu_ref[:, cols], preferred_element_type=F32) + b_glu_ref[:, cols]
        return vg[:, :LANES] * jax.nn.sigmoid(vg[:, LANES:])

    n_cb = CONV_CH // LANES
    acc_blocks = [[None] * n_cb for _ in range(n_sc)]
    u_next = glu_block(0)
    for cb in range(n_cb):
        lanes = slice(LANES * cb, LANES * (cb + 1))
        u_scr[pre:pre + tq, lanes] = u_next
        if cb + 1 < n_cb:
            u_next = glu_block(cb + 1)
        for r in range(1, SUBLANES):
            u_shift[r - 1, :, lanes] = u_scr[r:r + tq + CONV_SHIFT_EXTRA, lanes]
        for sc in range(n_sc):
            base = sc * PROMPT_SCOPE
            acc = jnp.zeros((PROMPT_SCOPE, LANES), F32)
            for j in range(CONV_WIDTH):
                a, r = divmod(pre - (CONV_WIDTH - 1) + j, SUBLANES)
                start = base + a * SUBLANES
                src = (u_scr[start:start + PROMPT_SCOPE, lanes] if r == 0
                       else u_shift[r - 1, start:start + PROMPT_SCOPE, lanes])
                acc = acc + src * conv_w_ref[j:j + 1, lanes]
            acc_blocks[sc][cb] = acc
    conv_pre = [jnp.concatenate(blocks, axis=1) + conv_b_ref[...] for blocks in acc_blocks]

    nc_ref[0] = u_scr[pre + tq - (CONV_WIDTH - 1):pre + tq, :]
    tail = u_scr[tq:tq + pre, :]
    u_scr[0:pre, :] = tail

    q, k, v = _in_proj_qkv(xb, w_in_ref, b_in_ref, w_sw_ref, b_sw_ref, rope_ref[...])
    qb = q.astype(BF16)
    kb = jnp.concatenate([kprev[...], k], axis=0).astype(BF16)
    vb = jnp.concatenate([vprev[...], v], axis=0).astype(BF16)
    row = lax.broadcasted_iota(jnp.int32, (WINDOW, 2 * WINDOW), 0)
    col = lax.broadcasted_iota(jnp.int32, (WINDOW, 2 * WINDOW), 1)
    dist = row + WINDOW - col
    band = (dist >= 0) & (dist < WINDOW)
    first_col = jnp.where(i > 0, 0, WINDOW)
    kprev[...] = k[tq - WINDOW:, :]
    vprev[...] = v[tq - WINDOW:, :]
    nk_ref[0] = k[tq - WINDOW:, :]
    nv_ref[0] = v[tq - WINDOW:, :]
    blocks_per_scope = PROMPT_SCOPE // WINDOW
    attn_blocks = [None] * (tq // WINDOW)

    def scope_attention(sc):
        for j in range(sc * blocks_per_scope, (sc + 1) * blocks_per_scope):
            msk = (band & (col >= first_col)) if j == 0 else band
            heads = [None] * N_HEADS
            for c in range(N_KV_HEADS):
                kc = kb[j * WINDOW:(j + 2) * WINDOW, c * HEAD_DIM:(c + 1) * HEAD_DIM]
                vc = vb[j * WINDOW:(j + 2) * WINDOW, c * HEAD_DIM:(c + 1) * HEAD_DIM]
                qc = jnp.concatenate(
                    [qb[j * WINDOW:(j + 1) * WINDOW, (c * Q_GROUP + g) * HEAD_DIM:(c * Q_GROUP + g + 1) * HEAD_DIM]
                     for g in range(Q_GROUP)], axis=0)
                s = lax.dot_general(qc, kc, (((1,), (1,)), ((), ())), preferred_element_type=F32)
                yield
                probs = []
                for g in range(Q_GROUP):
                    sg = jnp.where(msk, s[g * WINDOW:(g + 1) * WINDOW], NEG_INF)
                    p, _ = _softmax_rows_with_sink(sg, sinks_ref[c * Q_GROUP + g] * LOG2E)
                    probs.append(p.astype(BF16))
                    if g % 2 == 1:
                        yield
                o = jnp.dot(jnp.concatenate(probs, axis=0), vc, preferred_element_type=F32)
                for g in range(Q_GROUP):
                    heads[c * Q_GROUP + g] = o[g * WINDOW:(g + 1) * WINDOW]
                yield
            attn_blocks[j] = jnp.concatenate(heads, axis=1)

    def scope_tail(sc):
        rows = slice(sc * PROMPT_SCOPE, (sc + 1) * PROMPT_SCOPE)
        attn = jnp.concatenate(attn_blocks[sc * blocks_per_scope:(sc + 1) * blocks_per_scope], axis=0)
        h = yield from _mix_out_stages(x[rows], attn, conv_pre[sc], lcg_ref[...], lcb_ref[...],
                                       w_out_ref, b_out_ref[...], l1g_ref[...], l1b_ref[...])
        route, area, cnt = yield from _route_and_sort_stages(h, w_r_ref[...])
        h_ref[0, rows, :] = h
        route_ref[rows, :] = route
        xs_ref[sc * AREA_ROWS:(sc + 1) * AREA_ROWS, :] = area
        cnt_ref[sc] = jnp.broadcast_to(cnt, (SUBLANES, LANES))

    _round_robin(*[scope_attention(sc) for sc in range(n_sc)])
    _round_robin(*[scope_tail(sc) for sc in range(n_sc)])


def _mix_sample_kernel(sinks_ref, x_ref, rope_ref, ck_ref, cv_ref, st_ref, w_in_ref, b_in_ref, w_sw_ref, b_sw_ref,
                       conv_w_ref, conv_b_ref, lcg_ref, lcb_ref, w_out_ref, b_out_ref, l1g_ref, l1b_ref,
                       w_r_ref,
                       h_ref, nk_ref, nv_ref, nc_ref, xs_ref, route_ref, cnt_ref,
                       h_scr):
    tb = x_ref.shape[0]
    n_tok = h_scr.shape[0]
    step = pl.program_id(0)
    x = x_ref[...]
    xb = x.astype(BF16)
    u = _in_proj_glu(xb, w_in_ref, b_in_ref)
    q, k_new, v_new = _in_proj_qkv(xb, w_in_ref, b_in_ref, w_sw_ref, b_sw_ref, rope_ref[...])

    qb = q.astype(BF16)
    knb = k_new.astype(BF16).astype(F32)
    vnb = v_new.astype(BF16).astype(F32)
    gq = SAMPLE_GRP * Q_GROUP
    nkeys = SAMPLE_GRP * WINDOW
    r_i = lax.broadcasted_iota(jnp.int32, (gq, nkeys), 0)
    c_i = lax.broadcasted_iota(jnp.int32, (gq, nkeys), 1)
    window_shift, grp_shift = WINDOW.bit_length() - 1, SAMPLE_GRP.bit_length() - 1
    msk = ((c_i >> window_shift) == (r_i & (SAMPLE_GRP - 1))) & ((c_i & (WINDOW - 1)) >= 1)
    r_col = lax.broadcasted_iota(jnp.int32, (gq, 1), 0) >> grp_shift
    attn_groups = [None] * (tb // SAMPLE_GRP)

    def group_attention(grp):
        r0 = grp * SAMPLE_GRP
        heads = [None] * N_HEADS
        for c in range(N_KV_HEADS):
            lanes = slice(c * HEAD_DIM, (c + 1) * HEAD_DIM)
            kt_all = jnp.concatenate([ck_ref[r0 + b, lanes, :] for b in range(SAMPLE_GRP)], axis=1).astype(BF16)
            vt_all = jnp.concatenate([cv_ref[r0 + b, lanes, :] for b in range(SAMPLE_GRP)], axis=1).astype(BF16)
            yield
            qc = jnp.concatenate(
                [qb[r0:r0 + SAMPLE_GRP, (c * Q_GROUP + g) * HEAD_DIM:(c * Q_GROUP + g + 1) * HEAD_DIM]
                 for g in range(Q_GROUP)], axis=0)
            s = jnp.dot(qc, kt_all, preferred_element_type=F32)
            s = jnp.where(msk, s, NEG_INF)
            kn = jnp.concatenate([knb[r0:r0 + SAMPLE_GRP, lanes]] * Q_GROUP, axis=0)
            vn = jnp.concatenate([vnb[r0:r0 + SAMPLE_GRP, lanes]] * Q_GROUP, axis=0)
            s_new = jnp.sum(qc.astype(F32) * kn, axis=-1, keepdims=True)
            sink = jnp.zeros((gq, 1), F32)
            for g in range(Q_GROUP):
                sink = jnp.where(r_col == g, sinks_ref[c * Q_GROUP + g] * LOG2E, sink)
            yield
            p, p_new = _softmax_rows_with_sink(s, sink, s_new)
            yield
            o = (lax.dot_general(p.astype(BF16), vt_all, (((1,), (1,)), ((), ())), preferred_element_type=F32)
                 + p_new.astype(BF16).astype(F32) * vn)
            for g in range(Q_GROUP):
                heads[c * Q_GROUP + g] = o[g * SAMPLE_GRP:(g + 1) * SAMPLE_GRP]
            yield
        attn_groups[grp] = jnp.concatenate(heads, axis=1)

    _round_robin(*[group_attention(grp) for grp in range(len(attn_groups))])
    attn = jnp.concatenate(attn_groups, axis=0)

    acc = u * conv_w_ref[CONV_WIDTH - 1:CONV_WIDTH, :] + conv_b_ref[...]
    for j in range(CONV_WIDTH - 1):
        acc = acc + st_ref[j] * conv_w_ref[j:j + 1, :]
    conv_pre = acc

    nc_ref[0:CONV_WIDTH - 2] = st_ref[1:CONV_WIDTH - 1]
    nc_ref[CONV_WIDTH - 2] = u
    k_cols = k_new.T
    v_cols = v_new.T
    last = lax.broadcasted_iota(jnp.int32, (KV_DIM, WINDOW), 1) == WINDOW - 1
    for b in range(tb):
        nk_ref[b] = jnp.where(last, k_cols[:, b:b + 1], pltpu.roll(ck_ref[b], WINDOW - 1, 1))
        nv_ref[b] = jnp.where(last, v_cols[:, b:b + 1], pltpu.roll(cv_ref[b], WINDOW - 1, 1))

    h = _drain(_mix_out_stages(x, attn, conv_pre, lcg_ref[...], lcb_ref[...], w_out_ref, b_out_ref[...],
                               l1g_ref[...], l1b_ref[...]))
    h_ref[...] = h
    h_scr[pl.ds(pl.multiple_of(step * tb, tb), tb), :] = h

    @pl.when(step == pl.num_programs(0) - 1)
    def _():
        route, area, cnt = _drain(_route_and_sort_stages(h_scr[...], w_r_ref[...]))
        route_ref[...] = route
        xs_ref[...] = area
        cnt_ref[0] = jnp.broadcast_to(cnt, (SUBLANES, LANES))


GATHER_BUFS = 8
SCATTER_BUFS = 8
SCATTER_LAG = 2
ZERO_BITS = (64, 32, 16, 8, 4, 2, 1)
DUMP_CHUNKS = SCATTER_BUFS * BLOCK_CHUNKS
ZERO_ROWS = max(ZERO_BITS[0], DUMP_CHUNKS) * SUBLANES


def _moe_ffn_kernel(src_ref, dst_ref, bgen_ref, elist_ref, meta_ref, used_ref,
                    xs_hbm, w1_hbm, w3_hbm, w2_hbm,
                    ys_hbm,
                    xbuf, ybuf, hbuf, zbuf, wst1, wst3, wst2, w1b, w3b, w2b,
                    sem_in, sem_out, sem_zero, sem_dump, sem_w):
    n_areas = used_ref.shape[0]
    total = meta_ref[0]
    n_gen = meta_ref[1]
    dump_row0 = n_areas * AREA_ROWS
    lookahead = GATHER_BUFS - 1
    any_rows = pl.ds(0, SUBLANES)

    def chunk_at(ref, idx):
        return pl.ds(pl.multiple_of(ref[idx] * SUBLANES, SUBLANES), SUBLANES)

    def in_copy(rows, sl, c):
        return pltpu.make_async_copy(xs_hbm.at[rows], xbuf.at[sl, pl.ds(c * SUBLANES, SUBLANES)], sem_in.at[sl])

    def out_copy(rows, sl, c):
        return pltpu.make_async_copy(ybuf.at[sl, pl.ds(c * SUBLANES, SUBLANES)], ys_hbm.at[rows], sem_out.at[sl])

    def start_gather(blk):
        for c in range(BLOCK_CHUNKS):
            in_copy(chunk_at(src_ref, blk * BLOCK_CHUNKS + c), blk % GATHER_BUFS, c).start()

    def wait_gather(blk):
        for c in range(BLOCK_CHUNKS):
            in_copy(any_rows, blk % GATHER_BUFS, c).wait()

    def start_scatter(blk):
        for c in range(BLOCK_CHUNKS):
            out_copy(chunk_at(dst_ref, (blk + SCATTER_LAG) * BLOCK_CHUNKS + c), blk % SCATTER_BUFS, c).start()

    def wait_scatter(sl):
        for c in range(BLOCK_CHUNKS):
            out_copy(any_rows, sl, c).wait()

    def load_rows(blk):
        return _unpack_rows(xbuf[blk % GATHER_BUFS])

    def issue_dmas(blk):
        start_scatter(blk - SCATTER_LAG)
        start_gather(blk + lookahead)

    def up_gate(blk, rows, ws):
        x_lo, x_hi = rows
        a1 = (jnp.dot(x_lo, w1b[ws, 0:HALF, :], preferred_element_type=F32)
              + jnp.dot(x_hi, w1b[ws, HALF:, :], preferred_element_type=F32))
        a3 = (jnp.dot(x_lo, w3b[ws, 0:HALF, :], preferred_element_type=F32)
              + jnp.dot(x_hi, w3b[ws, HALF:, :], preferred_element_type=F32))
        hbuf[blk % SCATTER_BUFS] = (a1 * jax.nn.sigmoid(a1) * a3).astype(BF16)

    def down(blk, ws):
        y = jnp.dot(hbuf[blk % SCATTER_BUFS], w2b[ws], preferred_element_type=F32)
        ybuf[blk % SCATTER_BUFS] = _pack_rows(y.astype(BF16).astype(F32))

    def for_each_weight_copy(gen, fn):
        ws = gen % 2
        expert = elist_ref[gen]
        for hbm, staging in ((w1_hbm, wst1), (w3_hbm, wst3), (w2_hbm, wst2)):
            fn(pltpu.make_async_copy(hbm.at[expert], staging.at[ws], sem_w.at[ws]))

    def switch_weights(gen):
        ws = gen % 2
        for_each_weight_copy(gen, lambda cp: cp.wait())
        w1b[ws] = wst1[ws].astype(BF16)
        w3b[ws] = wst3[ws].astype(BF16)
        w2b[ws] = wst2[ws].astype(BF16)

        @pl.when(gen + 1 < n_gen)
        def _():
            for_each_weight_copy(gen + 1, lambda cp: cp.start())

    def for_each_tail_piece(fn):
        def area_body(s, carry):
            used = used_ref[s]
            tail = AREA_CHUNKS - used
            row = (s * AREA_CHUNKS + used) * SUBLANES
            for bit in ZERO_BITS:
                take = (tail & bit) != 0

                @pl.when(take)
                def _(row=row, bit=bit):
                    fn(pltpu.make_async_copy(zbuf.at[pl.ds(0, bit * SUBLANES)],
                                             ys_hbm.at[pl.ds(pl.multiple_of(row, SUBLANES), bit * SUBLANES)],
                                             sem_zero))
                row = row + jnp.where(take, bit * SUBLANES, 0)
            return carry
        lax.fori_loop(0, n_areas, area_body, 0)

    @pl.when(n_gen > 0)
    def _():
        for_each_weight_copy(0, lambda cp: cp.start())

    zbuf[...] = jnp.zeros(zbuf.shape, U32)
    ybuf[...] = jnp.zeros(ybuf.shape, U32)
    dump = pltpu.make_async_copy(zbuf.at[pl.ds(0, DUMP_CHUNKS * SUBLANES)],
                                 ys_hbm.at[pl.ds(dump_row0, DUMP_CHUNKS * SUBLANES)], sem_dump)
    dump.start()
    dump.wait()
    for_each_tail_piece(lambda cp: cp.start())
    for sl in range(SCATTER_BUFS - SCATTER_LAG):
        for c in range(BLOCK_CHUNKS):
            out_copy(pl.ds(dump_row0 + (sl * BLOCK_CHUNKS + c) * SUBLANES, SUBLANES), sl, c).start()
    for k in range(lookahead):
        start_gather(k)

    @pl.when(total > 0)
    def _():
        switch_weights(0)
        wait_gather(0)
        rows = load_rows(0)
        issue_dmas(0)
        up_gate(0, rows, 0)

        def block_body(blk, carry):
            gen = bgen_ref[blk]
            gen_prev = bgen_ref[blk - 1]

            @pl.when(gen != gen_prev)
            def _():
                switch_weights(gen)

            wait_gather(blk)
            wait_scatter((blk - 1) % SCATTER_BUFS)
            rows = load_rows(blk)
            issue_dmas(blk)
            down(blk - 1, gen_prev % 2)
            up_gate(blk, rows, gen % 2)
            return carry

        lax.fori_loop(1, total, block_body, 0)
        wait_scatter((total - 1) % SCATTER_BUFS)
        down(total - 1, bgen_ref[total - 1] % 2)

    for k in range(SCATTER_LAG):
        start_scatter(total - SCATTER_LAG + k)
    for sl in range(SCATTER_BUFS):
        wait_scatter(sl)
    for k in range(lookahead):
        wait_gather(total + k)
    for_each_tail_piece(lambda cp: cp.wait())


def _moe_combine_kernel(h_ref, route_ref, ys_ref, l2g_ref, l2b_ref, y_ref):
    n = h_ref.shape[0]
    scope = n // (ys_ref.shape[0] // AREA_ROWS)
    s_iota = lax.broadcasted_iota(jnp.int32, (scope, AREA_ROWS), 1).astype(F32)
    for sc in range(n // scope):
        rows = slice(sc * scope, (sc + 1) * scope)
        route = route_ref[rows, :]
        slot1 = route[:, 0:1]
        slot2 = route[:, 1:2]
        gate1 = route[:, 2:3]
        gate2 = route[:, 3:4]
        gmat = (jnp.where(s_iota == slot1, gate1, 0.0) + jnp.where(s_iota == slot2, gate2, 0.0)).astype(BF16)
        y_lo, y_hi = _unpack_rows(ys_ref[sc * AREA_ROWS:(sc + 1) * AREA_ROWS, :])
        f = jnp.concatenate([jnp.dot(gmat, y_lo, preferred_element_type=F32),
                             jnp.dot(gmat, y_hi, preferred_element_type=F32)], axis=1)
        y_ref[rows, :] = _layer_norm(ALPHA * h_ref[rows, :] + f, l2g_ref[...], l2b_ref[...])


def _excl_cumsum(a, axis):
    return jnp.cumsum(a, axis=axis) - a


def _plan_blocks(cnt, q_max):
    n_areas = cnt.shape[0]
    c8 = (cnt + (SUBLANES - 1)) // SUBLANES
    used8 = jnp.sum(c8, axis=1)
    base8 = jnp.arange(n_areas, dtype=jnp.int32) * AREA_CHUNKS
    src_runs = (base8[:, None] + _excl_cumsum(c8, 1)).T
    len_runs = c8.T
    tot = jnp.sum(len_runs, axis=1)
    ptot = (tot + BLOCK_CHUNKS - 1) // BLOCK_CHUNKS * BLOCK_CHUNKS
    pstart = _excl_cumsum(ptot, 0)
    dst_runs = pstart[:, None] + _excl_cumsum(len_runs, 1)
    off = (src_runs - dst_runs).reshape(-1)
    diff = off - jnp.concatenate([jnp.zeros((1,), jnp.int32), off[:-1]])
    q = jnp.arange(q_max, dtype=jnp.int32)
    src = q + jnp.sum(jnp.where(dst_runs.reshape(1, -1) <= q[:, None], diff[None, :], 0), axis=1)
    valid = jnp.any((pstart[None, :] <= q[:, None]) & (q[:, None] < (pstart + tot)[None, :]), axis=1)
    zero_chunk = AREA_CHUNKS - 1
    ahead = jnp.full(((GATHER_BUFS - 1) * BLOCK_CHUNKS,), zero_chunk, jnp.int32)
    src_tab = jnp.concatenate([jnp.where(valid, src, zero_chunk).astype(jnp.int32), ahead])
    qv = jnp.arange(-SCATTER_LAG * BLOCK_CHUNKS, q_max, dtype=jnp.int32)
    dump = n_areas * AREA_CHUNKS + ((qv // BLOCK_CHUNKS) % SCATTER_BUFS) * BLOCK_CHUNKS + qv % BLOCK_CHUNKS
    lead = jnp.zeros((SCATTER_LAG * BLOCK_CHUNKS,), jnp.int32)
    dst_tab = jnp.where(jnp.concatenate([lead > 0, valid]), jnp.concatenate([lead, src]), dump).astype(jnp.int32)
    nblk = ptot // BLOCK_CHUNKS
    bstart = pstart // BLOCK_CHUNKS
    has = nblk > 0
    gen_of = jnp.cumsum(has.astype(jnp.int32)) - 1
    experts = jnp.arange(N_EXPERTS, dtype=jnp.int32)
    elist = jnp.sum(jnp.where(has[None, :] & (gen_of[None, :] == experts[:, None]), experts[None, :], 0), axis=1)
    blk = jnp.arange(q_max // BLOCK_CHUNKS, dtype=jnp.int32)
    owner = (bstart[None, :] <= blk[:, None]) & (blk[:, None] < (bstart + nblk)[None, :])
    bgen = jnp.sum(jnp.where(owner, gen_of[None, :], 0), axis=1)
    meta = jnp.stack([jnp.sum(nblk), jnp.sum(has.astype(jnp.int32))])
    return (src_tab, dst_tab, bgen.astype(jnp.int32), elist.astype(jnp.int32), meta.astype(jnp.int32),
            used8.astype(jnp.int32))


PREP_STEPS = 4


def _swap_head_halves(a):
    half = HEAD_DIM // 2
    pieces = []
    for g in range(a.shape[-1] // LANES):
        p = a[:, g * LANES:(g + 1) * LANES]
        lane = lax.broadcasted_iota(jnp.int32, p.shape, 1)
        pieces.append(jnp.where((lane & half) == 0, pltpu.roll(p, LANES - half, 1), pltpu.roll(p, half, 1)))
    return jnp.concatenate(pieces, axis=-1)


def _glu_blocks(a):
    lo = ATTN_DIM + 2 * KV_DIM
    pieces = []
    for cb in range(CONV_CH // LANES):
        pieces.append(a[:, lo + cb * LANES:lo + (cb + 1) * LANES])
        pieces.append(a[:, lo + CONV_CH + cb * LANES:lo + CONV_CH + (cb + 1) * LANES])
    return jnp.concatenate(pieces, axis=-1)


def _prep_weights_kernel(w_in_ref, b_in_ref, w_out_ref, w_in_b_ref, w_sw_ref, w_glu_ref, w_out_b_ref,
                         b_sw_ref, b_glu_ref):
    qk = ATTN_DIM + KV_DIM
    w = w_in_ref[...]
    w_in_b_ref[...] = w.astype(BF16)
    w_sw_ref[...] = _swap_head_halves(w[:, :qk]).astype(BF16)
    w_glu_ref[...] = _glu_blocks(w).astype(BF16)
    w_out_b_ref[...] = w_out_ref[...].astype(BF16)
    b = jnp.broadcast_to(b_in_ref[...], (SUBLANES, IN_DIM))
    b_sw_ref[...] = _swap_head_halves(b[:, :qk])[0:1]
    b_glu_ref[...] = _glu_blocks(b)[0:1]


def _rope_tables(pos):
    half = HEAD_DIM // 2
    inv = (ROPE_THETA ** (-np.arange(half, dtype=np.float64) / half)).astype(np.float32)
    ang = (np.asarray(pos, np.float32)[:, None] * inv).astype(np.float64)
    cos = np.cos(ang)
    sin = np.sin(ang)
    reps = LANES // HEAD_DIM
    cos_t = np.concatenate([cos, cos] * reps, axis=1)
    sin_t = np.concatenate([-sin, sin] * reps, axis=1)
    q_scale = ATTN_SCALE * LOG2E
    return jnp.asarray(np.concatenate([cos_t, sin_t, cos_t * q_scale, sin_t * q_scale], axis=1), F32)


def _full(shape):
    nd = len(shape)
    return pl.BlockSpec(shape, lambda *_: (0,) * nd)


def kernel(x_prompt, x_sample, cache_k, cache_v, state_conv, w_in, b_in, sinks, conv_w, conv_b, ln_conv_g,
           ln_conv_b, w_out, b_out, ln1_g, ln1_b, w_router_group, w_router_expert, w1, w3, w2, ln2_g, ln2_b):
    assert w_in.shape[0] == DEPTH
    bsz, seq, _ = x_prompt.shape
    dec_b, dec_t, _ = x_sample.shape
    assert dec_t == 1 and seq % TILE_Q == 0 and dec_b % SAMPLE_TB == 0 and 2 * dec_b + 7 * N_EXPERTS <= AREA_ROWS
    assert (bsz * seq) % COMBINE_TILE == 0
    n_prompt = bsz * seq
    tiles = seq // TILE_Q
    spt = TILE_Q // PROMPT_SCOPE
    p_areas = n_prompt // PROMPT_SCOPE
    n_areas = p_areas + spt
    q_max = (n_areas * AREA_CHUNKS + N_EXPERTS * (BLOCK_CHUNKS - 1) + BLOCK_CHUNKS - 1) // BLOCK_CHUNKS * BLOCK_CHUNKS

    row = lambda a: a.reshape(1, -1)
    cparams = lambda sem: pltpu.CompilerParams(dimension_semantics=sem, vmem_limit_bytes=VMEM_LIMIT)

    qk = ATTN_DIM + KV_DIM
    wrows = D_MODEL // PREP_STEPS
    wspec = lambda cols: pl.BlockSpec((wrows, cols), lambda t: (t, 0))
    w_in_b, w_sw_b, w_glu_b, w_out_b, b_sw, b_glu = pl.pallas_call(
        _prep_weights_kernel,
        grid=(PREP_STEPS,),
        in_specs=[wspec(IN_DIM), _full((1, IN_DIM)), wspec(D_MODEL)],
        out_specs=[wspec(IN_DIM), wspec(qk), wspec(2 * CONV_CH), wspec(D_MODEL),
                   _full((1, qk)), _full((1, 2 * CONV_CH))],
        out_shape=[jax.ShapeDtypeStruct((D_MODEL, IN_DIM), BF16), jax.ShapeDtypeStruct((D_MODEL, qk), BF16),
                   jax.ShapeDtypeStruct((D_MODEL, 2 * CONV_CH), BF16), jax.ShapeDtypeStruct((D_MODEL, D_MODEL), BF16),
                   jax.ShapeDtypeStruct((1, qk), F32), jax.ShapeDtypeStruct((1, 2 * CONV_CH), F32)],
        compiler_params=cparams(("arbitrary",)),
        name="prep_weights",
    )(w_in[0], row(b_in[0]), w_out[0])
    w_r = jnp.concatenate([w_router_group[0], w_router_expert[0].reshape(D_MODEL, N_EXPERTS),
                           jnp.zeros((D_MODEL, LANES - ELANE0 - N_EXPERTS), F32)], axis=1).astype(BF16)
    shared = (w_in_b, row(b_in[0]), w_sw_b, b_sw, conv_w[0], row(conv_b[0]),
              row(ln_conv_g[0]), row(ln_conv_b[0]), w_out_b, row(b_out[0]), row(ln1_g[0]), row(ln1_b[0]), w_r)
    shared_specs = [_full(a.shape) for a in shared]
    glu = (w_glu_b, b_glu)
    glu_specs = [_full(a.shape) for a in glu]
    assert len(glu) + len(shared) == N_PROMPT_WEIGHTS
    smem = pl.BlockSpec(memory_space=pltpu.SMEM)
    rope_p = _rope_tables(np.arange(seq))
    rope_s = _rope_tables(PAST_LEN + np.arange(dec_t))

    to_window_minor = lambda a: jnp.transpose(a[0], (0, 2, 3, 1)).reshape(dec_b, KV_DIM, WINDOW)
    from_window_minor = lambda a: jnp.transpose(a.reshape(dec_b, N_KV_HEADS, HEAD_DIM, WINDOW), (0, 3, 1, 2))[None]
    ck = to_window_minor(cache_k)
    cv = to_window_minor(cache_v)
    st = jnp.transpose(state_conv[0], (1, 0, 2))
    tb = SAMPLE_TB
    cache_spec = pl.BlockSpec((tb, KV_DIM, WINDOW), lambda t: (t, 0, 0))
    conv_state_spec = pl.BlockSpec((CONV_WIDTH - 1, tb, CONV_CH), lambda t: (0, t, 0))
    h_s, nk_s, nv_s, nc_s, xs_s, route_s, cnt_s = pl.pallas_call(
        _mix_sample_kernel,
        grid=(dec_b // tb,),
        in_specs=[smem,
                  pl.BlockSpec((tb, None, D_MODEL), lambda t: (t, 0, 0)),
                  _full(rope_s.shape),
                  cache_spec, cache_spec, conv_state_spec] + shared_specs,
        out_specs=[pl.BlockSpec((tb, D_MODEL), lambda t: (t, 0)),
                   cache_spec, cache_spec, conv_state_spec,
                   _full((AREA_ROWS, HALF)),
                   _full((dec_b, LANES)),
                   _full((1, SUBLANES, LANES))],
        out_shape=[jax.ShapeDtypeStruct((dec_b, D_MODEL), F32),
                   jax.ShapeDtypeStruct((dec_b, KV_DIM, WINDOW), F32),
                   jax.ShapeDtypeStruct((dec_b, KV_DIM, WINDOW), F32),
                   jax.ShapeDtypeStruct((CONV_WIDTH - 1, dec_b, CONV_CH), F32),
                   jax.ShapeDtypeStruct((AREA_ROWS, HALF), U32),
                   jax.ShapeDtypeStruct((dec_b, LANES), F32),
                   jax.ShapeDtypeStruct((1, SUBLANES, LANES), F32)],
        scratch_shapes=[pltpu.VMEM((dec_b, D_MODEL), F32)],
        compiler_params=cparams(("arbitrary",)),
        name="mix_sample",
    )(sinks[0], x_sample, rope_s, ck, cv, st, *shared)

    n_real = bsz * tiles
    real = lambda t: jnp.minimum(t, n_real - 1)
    h_p, xs, route_p, cnt_p, nk_p, nv_p, nc_p = pl.pallas_call(
        functools.partial(_mix_prompt_kernel, tiles=tiles),
        grid=(n_real + 1,),
        in_specs=[smem,
                  pl.BlockSpec((1, TILE_Q, D_MODEL), lambda t: (real(t) // tiles, real(t) % tiles, 0)),
                  pl.BlockSpec((TILE_Q, 4 * LANES), lambda t: (real(t) % tiles, 0)),
                  _full((AREA_ROWS, HALF))] + glu_specs + shared_specs,
        out_specs=[pl.BlockSpec((1, TILE_Q, D_MODEL), lambda t: (real(t) // tiles, real(t) % tiles, 0)),
                   pl.BlockSpec((spt * AREA_ROWS, HALF), lambda t: (t, 0)),
                   pl.BlockSpec((TILE_Q, LANES), lambda t: (real(t), 0)),
                   pl.BlockSpec((spt, SUBLANES, LANES), lambda t: (t, 0, 0)),
                   pl.BlockSpec((1, WINDOW, KV_DIM), lambda t: (real(t) // tiles, 0, 0)),
                   pl.BlockSpec((1, WINDOW, KV_DIM), lambda t: (real(t) // tiles, 0, 0)),
                   pl.BlockSpec((1, CONV_WIDTH - 1, CONV_CH), lambda t: (real(t) // tiles, 0, 0))],
        out_shape=[jax.ShapeDtypeStruct((bsz, seq, D_MODEL), F32),
                   jax.ShapeDtypeStruct((n_areas * AREA_ROWS, HALF), U32),
                   jax.ShapeDtypeStruct((n_prompt, LANES), F32),
                   jax.ShapeDtypeStruct((n_areas, SUBLANES, LANES), F32),
                   jax.ShapeDtypeStruct((bsz, WINDOW, KV_DIM), F32),
                   jax.ShapeDtypeStruct((bsz, WINDOW, KV_DIM), F32),
                   jax.ShapeDtypeStruct((bsz, CONV_WIDTH - 1, CONV_CH), F32)],
        scratch_shapes=[pltpu.VMEM((WINDOW, KV_DIM), F32), pltpu.VMEM((WINDOW, KV_DIM), F32),
                        pltpu.VMEM((TILE_Q + CONV_PRE, CONV_CH), F32),
                        pltpu.VMEM((SUBLANES - 1, TILE_Q + CONV_SHIFT_EXTRA, CONV_CH), F32)],
        compiler_params=cparams(("arbitrary",)),
        name="mix_prompt",
    )(sinks[0], x_prompt, rope_p, xs_s, *glu, *shared)

    cnt = jnp.concatenate([cnt_p[:p_areas, 0], cnt_s[:, 0], cnt_p[p_areas + 1:, 0]], axis=0)
    cnt = cnt[:, ELANE0:ELANE0 + N_EXPERTS].astype(jnp.int32)
    plan = _plan_blocks(cnt, q_max)
    hbm = pl.BlockSpec(memory_space=pl.ANY)
    up_shape, down_shape = (2, D_MODEL, D_FF_EXPERT), (2, D_FF_EXPERT, D_MODEL)
    ys = pl.pallas_call(
        _moe_ffn_kernel,
        grid_spec=pltpu.PrefetchScalarGridSpec(
            num_scalar_prefetch=len(plan),
            grid=(1,),
            in_specs=[hbm, hbm, hbm, hbm],
            out_specs=hbm,
            scratch_shapes=[pltpu.VMEM((GATHER_BUFS, BLOCK_ROWS, HALF), U32),
                            pltpu.VMEM((SCATTER_BUFS, BLOCK_ROWS, HALF), U32),
                            pltpu.VMEM((SCATTER_BUFS, BLOCK_ROWS, D_FF_EXPERT), BF16),
                            pltpu.VMEM((ZERO_ROWS, HALF), U32),
                            pltpu.VMEM(up_shape, F32), pltpu.VMEM(up_shape, F32), pltpu.VMEM(down_shape, F32),
                            pltpu.VMEM(up_shape, BF16), pltpu.VMEM(up_shape, BF16), pltpu.VMEM(down_shape, BF16),
                            pltpu.SemaphoreType.DMA((GATHER_BUFS,)), pltpu.SemaphoreType.DMA((SCATTER_BUFS,)),
                            pltpu.SemaphoreType.DMA(()), pltpu.SemaphoreType.DMA(()),
                            pltpu.SemaphoreType.DMA((2,))]),
        out_shape=jax.ShapeDtypeStruct((n_areas * AREA_ROWS + DUMP_CHUNKS * SUBLANES, HALF), U32),
        compiler_params=cparams(("arbitrary",)),
        name="moe_ffn",
    )(*plan, xs, w1[0], w3[0], w2[0])

    def combine(h2d, route, first_area, scope, tile):
        n = h2d.shape[0]
        apt = tile // scope
        return pl.pallas_call(
            _moe_combine_kernel,
            grid=(n // tile,),
            in_specs=[pl.BlockSpec((tile, D_MODEL), lambda t: (t, 0)),
                      pl.BlockSpec((tile, LANES), lambda t: (t, 0)),
                      pl.BlockSpec((apt * AREA_ROWS, HALF), lambda t: (first_area // apt + t, 0)),
                      _full((1, D_MODEL)), _full((1, D_MODEL))],
            out_specs=pl.BlockSpec((tile, D_MODEL), lambda t: (t, 0)),
            out_shape=jax.ShapeDtypeStruct((n, D_MODEL), F32),
            compiler_params=cparams(("arbitrary",)),
            name="moe_combine",
        )(h2d, route, ys, row(ln2_g[0]), row(ln2_b[0]))

    y_p = combine(h_p.reshape(n_prompt, D_MODEL), route_p, 0, PROMPT_SCOPE, COMBINE_TILE)
    y_s = combine(h_s, route_s, p_areas, dec_b, dec_b)

    kv_shape = lambda n: (DEPTH, n, WINDOW, N_KV_HEADS, HEAD_DIM)
    return (y_p.reshape(bsz, seq, D_MODEL), y_s.reshape(dec_b, dec_t, D_MODEL),
            nk_p.reshape(kv_shape(bsz)), nv_p.reshape(kv_shape(bsz)), nc_p[None],
            from_window_minor(nk_s), from_window_minor(nv_s), jnp.transpose(nc_s, (1, 0, 2))[None])
```

```python
import functools

import jax
import jax.numpy as jnp
import numpy as np
from jax import lax
from jax.experimental import pallas as pl
from jax.experimental.pallas import tpu as pltpu

F32 = jnp.float32
BF16 = jnp.bfloat16
U32 = jnp.uint32

D_MODEL = 1024
N_HEADS = 8
N_KV_HEADS = 2
HEAD_DIM = 64
Q_GROUP = N_HEADS // N_KV_HEADS
ATTN_DIM = N_HEADS * HEAD_DIM
KV_DIM = N_KV_HEADS * HEAD_DIM
WINDOW = 128
ROPE_THETA = 10000.0
ATTN_SCALE = HEAD_DIM ** -0.5
CONV_CH = D_MODEL - ATTN_DIM
CONV_WIDTH = 31
IN_DIM = ATTN_DIM + 2 * KV_DIM + 2 * CONV_CH
N_GROUPS = 4
EXPERTS_PER_GROUP = 8
N_EXPERTS = N_GROUPS * EXPERTS_PER_GROUP
D_FF_EXPERT = 256
LN_EPS = 1e-5
NEG_INF = -1e30
DEPTH = 1
ALPHA = (2.0 * DEPTH) ** 0.25
PAST_LEN = 16384
LOG2E = 1.4426950408889634

LANES = 128
SUBLANES = 8
HALF = D_MODEL // 2
ELANE0 = N_GROUPS
PROMPT_SCOPE = 256
AREA_ROWS = -(-(2 * PROMPT_SCOPE + (SUBLANES - 1) * N_EXPERTS) // LANES) * LANES
AREA_CHUNKS = AREA_ROWS // SUBLANES
BLOCK_CHUNKS = 32
BLOCK_ROWS = BLOCK_CHUNKS * SUBLANES
TILE_Q = 512
COMBINE_TILE = 1024
CONV_PRE = 32
CONV_SHIFT_EXTRA = 24
SAMPLE_TB = 32
SAMPLE_GRP = 8
VMEM_LIMIT = 56 * 1024 * 1024


def _layer_norm(x, g, b):
    mu = jnp.mean(x, axis=-1, keepdims=True)
    xc = x - mu
    var = jnp.mean(xc * xc, axis=-1, keepdims=True)
    return xc * lax.rsqrt(var + LN_EPS) * g + b


def _rope(x, x_swapped, cos, sin_signed):
    pieces = [x[:, g * LANES:(g + 1) * LANES] * cos + x_swapped[:, g * LANES:(g + 1) * LANES] * sin_signed
              for g in range(x.shape[-1] // LANES)]
    return pieces[0] if len(pieces) == 1 else jnp.concatenate(pieces, axis=-1)


def _in_proj_glu(xb, w_in_ref, b_in_ref):
    lo, hi = ATTN_DIM + 2 * KV_DIM, IN_DIM
    cacg = jnp.dot(xb, w_in_ref[:, lo:hi], preferred_element_type=F32) + b_in_ref[:, lo:hi]
    return cacg[:, :CONV_CH] * jax.nn.sigmoid(cacg[:, CONV_CH:])


def _in_proj_qkv(xb, w_in_ref, b_in_ref, w_sw_ref, b_sw_ref, rope):
    o_v, o_e = ATTN_DIM + KV_DIM, ATTN_DIM + 2 * KV_DIM
    qkv = jnp.dot(xb, w_in_ref[:, 0:o_e], preferred_element_type=F32) + b_in_ref[:, 0:o_e]
    sw = jnp.dot(xb, w_sw_ref[...], preferred_element_type=F32) + b_sw_ref[...]
    k = _rope(qkv[:, ATTN_DIM:o_v], sw[:, ATTN_DIM:], rope[:, 0:LANES], rope[:, LANES:2 * LANES])
    q = _rope(qkv[:, :ATTN_DIM], sw[:, :ATTN_DIM], rope[:, 2 * LANES:3 * LANES], rope[:, 3 * LANES:])
    return q, k, qkv[:, o_v:]


def _pack_rows(v):
    lo = lax.bitcast_convert_type(v[:, :HALF], U32)
    hi = lax.bitcast_convert_type(v[:, HALF:], U32)
    return (lo >> 16) | hi


def _unpack_rows(w):
    lo = lax.bitcast_convert_type(w << 16, F32).astype(BF16)
    hi = lax.bitcast_convert_type(w & jnp.uint32(0xFFFF0000), F32).astype(BF16)
    return lo, hi


def _drain(stages):
    try:
        while True:
            next(stages)
    except StopIteration as stop:
        return stop.value


def _round_robin(*stage_generators):
    live = list(stage_generators)
    while live:
        for stages in list(live):
            try:
                next(stages)
            except StopIteration:
                live.remove(stages)


def _route_and_sort_stages(h, w_r):
    n = h.shape[0]
    hb = h.astype(BF16)
    logits = jnp.dot(hb, w_r, preferred_element_type=F32)
    yield
    lane = lax.broadcasted_iota(jnp.int32, (n, LANES), 1)
    lane_f = lane.astype(F32)
    is_g = lane < N_GROUPS
    lg = jnp.where(is_g, logits, -jnp.inf)
    gmax = jnp.max(lg, axis=-1, keepdims=True)
    yield
    gidx = jnp.min(jnp.where(lg == gmax, lane_f, float(LANES)), axis=-1, keepdims=True)
    p_g = 1.0 / jnp.sum(jnp.where(is_g, jnp.exp(logits - gmax), 0.0), axis=-1, keepdims=True)
    yield
    egrp = ((lane - ELANE0) >> (EXPERTS_PER_GROUP.bit_length() - 1)).astype(F32)
    emask = (lane >= ELANE0) & (lane < ELANE0 + N_EXPERTS) & (egrp == gidx)
    sel = jnp.where(emask, logits, -jnp.inf)
    v1 = jnp.max(sel, axis=-1, keepdims=True)
    yield
    i1 = jnp.min(jnp.where(sel == v1, lane_f, float(LANES)), axis=-1, keepdims=True)
    yield
    sel2 = jnp.where(lane_f == i1, -jnp.inf, sel)
    v2 = jnp.max(sel2, axis=-1, keepdims=True)
    yield
    i2 = jnp.min(jnp.where(sel2 == v2, lane_f, float(LANES)), axis=-1, keepdims=True)
    t = jnp.exp(v2 - v1)
    gate1 = p_g / (1.0 + t)
    gate2 = p_g * t / (1.0 + t)
    pick1 = lane_f == i1
    pick2 = lane_f == i2
    member = jnp.where(pick1 | pick2, 1.0, 0.0)
    yield

    r_i = lax.broadcasted_iota(jnp.int32, (n, n), 0)
    c_i = lax.broadcasted_iota(jnp.int32, (n, n), 1)
    lower = jnp.where(c_i < r_i, 1.0, 0.0).astype(BF16)
    e_r = lax.broadcasted_iota(jnp.int32, (LANES, LANES), 0)
    e_c = lax.broadcasted_iota(jnp.int32, (LANES, LANES), 1)
    upper = jnp.where(e_r < e_c, 1.0, 0.0).astype(BF16)
    before = jnp.dot(lower, member.astype(BF16), preferred_element_type=F32)
    cnt = jnp.sum(member, axis=0, keepdims=True)
    c8 = jnp.floor((cnt + (SUBLANES - 1)) * (1.0 / SUBLANES))
    c8b = jnp.broadcast_to(c8, (SUBLANES, LANES)).astype(BF16)
    off8 = jnp.dot(c8b, upper, preferred_element_type=F32)[0:1]
    yield
    slot_all = before + off8 * float(SUBLANES)
    slot1 = jnp.sum(jnp.where(pick1, slot_all, 0.0), axis=-1, keepdims=True)
    slot2 = jnp.sum(jnp.where(pick2, slot_all, 0.0), axis=-1, keepdims=True)
    yield
    s_iota = lax.broadcasted_iota(jnp.int32, (n, AREA_ROWS), 1).astype(F32)
    perm_t = jnp.where((s_iota == slot1) | (s_iota == slot2), 1.0, 0.0).astype(BF16)
    sorted_rows = lax.dot_general(perm_t, hb, (((0,), (0,)), ((), ())),
                                  preferred_element_type=F32)
    yield
    route = jnp.where(lane == 0, slot1, jnp.where(lane == 1, slot2, jnp.where(lane == 2, gate1,
                      jnp.where(lane == 3, gate2, 0.0))))
    return route, _pack_rows(sorted_rows), cnt


def _softmax_rows_with_sink(s, sink, extra=None):
    m = jnp.maximum(jnp.max(s, axis=-1, keepdims=True), sink)
    if extra is not None:
        m = jnp.maximum(m, extra)
    e = jnp.exp2(s - m)
    den = jnp.sum(e, axis=-1, keepdims=True) + jnp.exp2(sink - m)
    if extra is not None:
        ee = jnp.exp2(extra - m)
        den = den + ee
        return e / den, ee / den
    return e / den, None


def _mix_out_stages(x, attn, conv_pre, lcg, lcb, w_out_ref, b_out, l1g, l1b):
    conv = _layer_norm(conv_pre, lcg, lcb)
    conv = conv * jax.nn.sigmoid(conv)
    yield
    mixed = jnp.concatenate([attn.astype(BF16), conv.astype(BF16)], axis=1)
    mix = jnp.dot(mixed, w_out_ref[...], preferred_element_type=F32) + b_out
    yield
    h = _layer_norm(ALPHA * x + mix, l1g, l1b)
    yield
    return h


N_PROMPT_WEIGHTS = 15


def _mix_prompt_kernel(sinks_ref, x_ref, rope_ref, xs_s_ref, *rest, tiles):
    _, xs_ref, _, cnt_ref = rest[N_PROMPT_WEIGHTS:N_PROMPT_WEIGHTS + 4]
    step = pl.program_id(0)
    n_real = pl.num_programs(0) - 1

    @pl.when(step < n_real)
    def _():
        _mix_prompt_tile(sinks_ref, x_ref, rope_ref, *rest, i=step % tiles)

    @pl.when(step == n_real)
    def _():
        xs_ref[0:AREA_ROWS, :] = xs_s_ref[...]
        xs_ref[AREA_ROWS:, :] = jnp.zeros((xs_ref.shape[0] - AREA_ROWS, HALF), U32)
        cnt_ref[...] = jnp.zeros(cnt_ref.shape, F32)


def _mix_prompt_tile(sinks_ref, x_ref, rope_ref, w_glu_ref, b_glu_ref,
                     w_in_ref, b_in_ref, w_sw_ref, b_sw_ref, conv_w_ref, conv_b_ref,
                     lcg_ref, lcb_ref, w_out_ref, b_out_ref, l1g_ref, l1b_ref, w_r_ref,
                     h_ref, xs_ref, route_ref, cnt_ref, nk_ref, nv_ref, nc_ref,
                     kprev, vprev, u_scr, u_shift, *, i):
    tq = x_ref.shape[1]
    pre = CONV_PRE
    n_sc = tq // PROMPT_SCOPE

    @pl.when(i == 0)
    def _():
        kprev[...] = jnp.zeros_like(kprev)
        vprev[...] = jnp.zeros_like(vprev)
        u_scr[0:pre, :] = jnp.zeros((pre, CONV_CH), F32)

    x = x_ref[0]
    xb = x.astype(BF16)

    def glu_block(cb):
        cols = slice(2 * LANES * cb, 2 * LANES * (cb + 1))
        vg = jnp.dot(xb, w_glu_ref[:, cols], preferred_element_type=F32) + b_glu_ref[:, cols]
        return vg[:, :LANES] * jax.nn.sigmoid(vg[:, LANES:])

    n_cb = CONV_CH // LANES
    acc_blocks = [[None] * n_cb for _ in range(n_sc)]
    u_next = glu_block(0)
    for cb in range(n_cb):
        lanes = slice(LANES * cb, LANES * (cb + 1))
        u_scr[pre:pre + tq, lanes] = u_next
        if cb + 1 < n_cb:
            u_next = glu_block(cb + 1)
        for r in range(1, SUBLANES):
            u_shift[r - 1, :, lanes] = u_scr[r:r + tq + CONV_SHIFT_EXTRA, lanes]
        for sc in range(n_sc):
            base = sc * PROMPT_SCOPE
            acc = jnp.zeros((PROMPT_SCOPE, LANES), F32)
            for j in range(CONV_WIDTH):
                a, r = divmod(pre - (CONV_WIDTH - 1) + j, SUBLANES)
                start = base + a * SUBLANES
                src = (u_scr[start:start + PROMPT_SCOPE, lanes] if r == 0
                       else u_shift[r - 1, start:start + PROMPT_SCOPE, lanes])
                acc = acc + src * conv_w_ref[j:j + 1, lanes]
            acc_blocks[sc][cb] = acc
    conv_pre = [jnp.concatenate(blocks, axis=1) + conv_b_ref[...] for blocks in acc_blocks]

    nc_ref[0] = u_scr[pre + tq - (CONV_WIDTH - 1):pre + tq, :]
    tail = u_scr[tq:tq + pre, :]
    u_scr[0:pre, :] = tail

    q, k, v = _in_proj_qkv(xb, w_in_ref, b_in_ref, w_sw_ref, b_sw_ref, rope_ref[...])
    qb = q.astype(BF16)
    kb = jnp.concatenate([kprev[...], k], axis=0).astype(BF16)
    vb = jnp.concatenate([vprev[...], v], axis=0).astype(BF16)
    row = lax.broadcasted_iota(jnp.int32, (WINDOW, 2 * WINDOW), 0)
    col = lax.broadcasted_iota(jnp.int32, (WINDOW, 2 * WINDOW), 1)
    dist = row + WINDOW - col
    band = (dist >= 0) & (dist < WINDOW)
    first_col = jnp.where(i > 0, 0, WINDOW)
    kprev[...] = k[tq - WINDOW:, :]
    vprev[...] = v[tq - WINDOW:, :]
    nk_ref[0] = k[tq - WINDOW:, :].T
    nv_ref[0] = v[tq - WINDOW:, :].T
    blocks_per_scope = PROMPT_SCOPE // WINDOW
    attn_blocks = [None] * (tq // WINDOW)

    def scope_attention(sc):
        for j in range(sc * blocks_per_scope, (sc + 1) * blocks_per_scope):
            msk = (band & (col >= first_col)) if j == 0 else band
            heads = [None] * N_HEADS
            for c in range(N_KV_HEADS):
                kc = kb[j * WINDOW:(j + 2) * WINDOW, c * HEAD_DIM:(c + 1) * HEAD_DIM]
                vc = vb[j * WINDOW:(j + 2) * WINDOW, c * HEAD_DIM:(c + 1) * HEAD_DIM]
                qc = jnp.concatenate(
                    [qb[j * WINDOW:(j + 1) * WINDOW, (c * Q_GROUP + g) * HEAD_DIM:(c * Q_GROUP + g + 1) * HEAD_DIM]
                     for g in range(Q_GROUP)], axis=0)
                s = lax.dot_general(qc, kc, (((1,), (1,)), ((), ())), preferred_element_type=F32)
                yield
                probs = []
                for g in range(Q_GROUP):
                    sg = jnp.where(msk, s[g * WINDOW:(g + 1) * WINDOW], NEG_INF)
                    p, _ = _softmax_rows_with_sink(sg, sinks_ref[c * Q_GROUP + g] * LOG2E)
                    probs.append(p.astype(BF16))
                    if g % 2 == 1:
                        yield
                o = jnp.dot(jnp.concatenate(probs, axis=0), vc, preferred_element_type=F32)
                for g in range(Q_GROUP):
                    heads[c * Q_GROUP + g] = o[g * WINDOW:(g + 1) * WINDOW]
                yield
            attn_blocks[j] = jnp.concatenate(heads, axis=1)

    def scope_tail(sc):
        rows = slice(sc * PROMPT_SCOPE, (sc + 1) * PROMPT_SCOPE)
        attn = jnp.concatenate(attn_blocks[sc * blocks_per_scope:(sc + 1) * blocks_per_scope], axis=0)
        h = yield from _mix_out_stages(x[rows], attn, conv_pre[sc], lcg_ref[...], lcb_ref[...],
                                       w_out_ref, b_out_ref[...], l1g_ref[...], l1b_ref[...])
        route, area, cnt = yield from _route_and_sort_stages(h, w_r_ref[...])
        h_ref[0, rows, :] = h
        route_ref[rows, :] = route
        xs_ref[sc * AREA_ROWS:(sc + 1) * AREA_ROWS, :] = area
        cnt_ref[sc] = jnp.broadcast_to(cnt, (SUBLANES, LANES))

    _round_robin(*[scope_attention(sc) for sc in range(n_sc)])
    _round_robin(*[scope_tail(sc) for sc in range(n_sc)])


def _mix_sample_kernel(sinks_ref, x_ref, rope_ref, ck_ref, cv_ref, st_ref, w_in_ref, b_in_ref, w_sw_ref, b_sw_ref,
                       conv_w_ref, conv_b_ref, lcg_ref, lcb_ref, w_out_ref, b_out_ref, l1g_ref, l1b_ref,
                       w_r_ref,
                       h_ref, nk_ref, nv_ref, nc_ref, xs_ref, route_ref, cnt_ref,
                       h_scr):
    tb = x_ref.shape[0]
    n_tok = h_scr.shape[0]
    step = pl.program_id(0)
    x = x_ref[...]
    xb = x.astype(BF16)
    u = _in_proj_glu(xb, w_in_ref, b_in_ref)
    q, k_new, v_new = _in_proj_qkv(xb, w_in_ref, b_in_ref, w_sw_ref, b_sw_ref, rope_ref[...])

    qb = q.astype(BF16)
    knb = k_new.astype(BF16).astype(F32)
    vnb = v_new.astype(BF16).astype(F32)
    gq = SAMPLE_GRP * Q_GROUP
    nkeys = SAMPLE_GRP * WINDOW
    r_i = lax.broadcasted_iota(jnp.int32, (gq, nkeys), 0)
    c_i = lax.broadcasted_iota(jnp.int32, (gq, nkeys), 1)
    window_shift, grp_shift = WINDOW.bit_length() - 1, SAMPLE_GRP.bit_length() - 1
    msk = ((c_i >> window_shift) == (r_i & (SAMPLE_GRP - 1))) & ((c_i & (WINDOW - 1)) >= 1)
    r_col = lax.broadcasted_iota(jnp.int32, (gq, 1), 0) >> grp_shift
    attn_groups = [None] * (tb // SAMPLE_GRP)

    def group_attention(grp):
        r0 = grp * SAMPLE_GRP
        heads = [None] * N_HEADS
        for c in range(N_KV_HEADS):
            lanes = slice(c * HEAD_DIM, (c + 1) * HEAD_DIM)
            kt_all = jnp.concatenate([ck_ref[r0 + b, lanes, :] for b in range(SAMPLE_GRP)], axis=1).astype(BF16)
            vt_all = jnp.concatenate([cv_ref[r0 + b, lanes, :] for b in range(SAMPLE_GRP)], axis=1).astype(BF16)
            yield
            qc = jnp.concatenate(
                [qb[r0:r0 + SAMPLE_GRP, (c * Q_GROUP + g) * HEAD_DIM:(c * Q_GROUP + g + 1) * HEAD_DIM]
                 for g in range(Q_GROUP)], axis=0)
            s = jnp.dot(qc, kt_all, preferred_element_type=F32)
            s = jnp.where(msk, s, NEG_INF)
            kn = jnp.concatenate([knb[r0:r0 + SAMPLE_GRP, lanes]] * Q_GROUP, axis=0)
            vn = jnp.concatenate([vnb[r0:r0 + SAMPLE_GRP, lanes]] * Q_GROUP, axis=0)
            s_new = jnp.sum(qc.astype(F32) * kn, axis=-1, keepdims=True)
            sink = jnp.zeros((gq, 1), F32)
            for g in range(Q_GROUP):
                sink = jnp.where(r_col == g, sinks_ref[c * Q_GROUP + g] * LOG2E, sink)
            yield
            p, p_new = _softmax_rows_with_sink(s, sink, s_new)
            yield
            o = (lax.dot_general(p.astype(BF16), vt_all, (((1,), (1,)), ((), ())), preferred_element_type=F32)
                 + p_new.astype(BF16).astype(F32) * vn)
            for g in range(Q_GROUP):
                heads[c * Q_GROUP + g] = o[g * SAMPLE_GRP:(g + 1) * SAMPLE_GRP]
            yield
        attn_groups[grp] = jnp.concatenate(heads, axis=1)

    _round_robin(*[group_attention(grp) for grp in range(len(attn_groups))])
    attn = jnp.concatenate(attn_groups, axis=0)

    acc = u * conv_w_ref[CONV_WIDTH - 1:CONV_WIDTH, :] + conv_b_ref[...]
    for j in range(CONV_WIDTH - 1):
        acc = acc + st_ref[j] * conv_w_ref[j:j + 1, :]
    conv_pre = acc

    nc_ref[0:CONV_WIDTH - 2] = st_ref[1:CONV_WIDTH - 1]
    nc_ref[CONV_WIDTH - 2] = u
    k_cols = k_new.T
    v_cols = v_new.T
    last = lax.broadcasted_iota(jnp.int32, (KV_DIM, WINDOW), 1) == WINDOW - 1
    for b in range(tb):
        nk_ref[b] = jnp.where(last, k_cols[:, b:b + 1], pltpu.roll(ck_ref[b], WINDOW - 1, 1))
        nv_ref[b] = jnp.where(last, v_cols[:, b:b + 1], pltpu.roll(cv_ref[b], WINDOW - 1, 1))

    h = _drain(_mix_out_stages(x, attn, conv_pre, lcg_ref[...], lcb_ref[...], w_out_ref, b_out_ref[...],
                               l1g_ref[...], l1b_ref[...]))
    h_ref[...] = h
    h_scr[pl.ds(pl.multiple_of(step * tb, tb), tb), :] = h

    @pl.when(step == pl.num_programs(0) - 1)
    def _():
        route, area, cnt = _drain(_route_and_sort_stages(h_scr[...], w_r_ref[...]))
        route_ref[...] = route
        xs_ref[...] = area
        cnt_ref[0] = jnp.broadcast_to(cnt, (SUBLANES, LANES))


GATHER_BUFS = 8
SCATTER_BUFS = 8
SCATTER_LAG = 2
ZERO_BITS = (64, 32, 16, 8, 4, 2, 1)
DUMP_CHUNKS = SCATTER_BUFS * BLOCK_CHUNKS
ZERO_ROWS = max(ZERO_BITS[0], DUMP_CHUNKS) * SUBLANES


def _moe_ffn_kernel(src_ref, dst_ref, bgen_ref, elist_ref, meta_ref, used_ref,
                    xs_hbm, w1_hbm, w3_hbm, w2_hbm,
                    ys_hbm,
                    xbuf, ybuf, hbuf, zbuf, wst1, wst3, wst2, w1b, w3b, w2b,
                    sem_in, sem_out, sem_zero, sem_dump, sem_w):
    n_areas = used_ref.shape[0]
    total = meta_ref[0]
    n_gen = meta_ref[1]
    dump_row0 = n_areas * AREA_ROWS
    lookahead = GATHER_BUFS - 1
    any_rows = pl.ds(0, SUBLANES)

    def chunk_at(ref, idx):
        return pl.ds(pl.multiple_of(ref[idx] * SUBLANES, SUBLANES), SUBLANES)

    def in_copy(rows, sl, c):
        return pltpu.make_async_copy(xs_hbm.at[rows], xbuf.at[sl, pl.ds(c * SUBLANES, SUBLANES)], sem_in.at[sl])

    def out_copy(rows, sl, c):
        return pltpu.make_async_copy(ybuf.at[sl, pl.ds(c * SUBLANES, SUBLANES)], ys_hbm.at[rows], sem_out.at[sl])

    def start_gather(blk):
        for c in range(BLOCK_CHUNKS):
            in_copy(chunk_at(src_ref, blk * BLOCK_CHUNKS + c), blk % GATHER_BUFS, c).start()

    def wait_gather(blk):
        for c in range(BLOCK_CHUNKS):
            in_copy(any_rows, blk % GATHER_BUFS, c).wait()

    def start_scatter(blk):
        for c in range(BLOCK_CHUNKS):
            out_copy(chunk_at(dst_ref, (blk + SCATTER_LAG) * BLOCK_CHUNKS + c), blk % SCATTER_BUFS, c).start()

    def wait_scatter(sl):
        for c in range(BLOCK_CHUNKS):
            out_copy(any_rows, sl, c).wait()

    def load_rows(blk):
        return _unpack_rows(xbuf[blk % GATHER_BUFS])

    def issue_dmas(blk):
        start_scatter(blk - SCATTER_LAG)
        start_gather(blk + lookahead)

    def up_gate(blk, rows, ws):
        x_lo, x_hi = rows
        a1 = (jnp.dot(x_lo, w1b[ws, 0:HALF, :], preferred_element_type=F32)
              + jnp.dot(x_hi, w1b[ws, HALF:, :], preferred_element_type=F32))
        a3 = (jnp.dot(x_lo, w3b[ws, 0:HALF, :], preferred_element_type=F32)
              + jnp.dot(x_hi, w3b[ws, HALF:, :], preferred_element_type=F32))
        hbuf[blk % SCATTER_BUFS] = (a1 * jax.nn.sigmoid(a1) * a3).astype(BF16)

    def down(blk, ws):
        y = jnp.dot(hbuf[blk % SCATTER_BUFS], w2b[ws], preferred_element_type=F32)
        ybuf[blk % SCATTER_BUFS] = _pack_rows(y.astype(BF16).astype(F32))

    def for_each_weight_copy(gen, fn):
        ws = gen % 2
        expert = elist_ref[gen]
        for hbm, staging in ((w1_hbm, wst1), (w3_hbm, wst3), (w2_hbm, wst2)):
            fn(pltpu.make_async_copy(hbm.at[expert], staging.at[ws], sem_w.at[ws]))

    def switch_weights(gen):
        ws = gen % 2
        for_each_weight_copy(gen, lambda cp: cp.wait())
        w1b[ws] = wst1[ws].astype(BF16)
        w3b[ws] = wst3[ws].astype(BF16)
        w2b[ws] = wst2[ws].astype(BF16)

        @pl.when(gen + 1 < n_gen)
        def _():
            for_each_weight_copy(gen + 1, lambda cp: cp.start())

    def for_each_tail_piece(fn):
        def area_body(s, carry):
            used = used_ref[s]
            tail = AREA_CHUNKS - used
            row = (s * AREA_CHUNKS + used) * SUBLANES
            for bit in ZERO_BITS:
                take = (tail & bit) != 0

                @pl.when(take)
                def _(row=row, bit=bit):
                    fn(pltpu.make_async_copy(zbuf.at[pl.ds(0, bit * SUBLANES)],
                                             ys_hbm.at[pl.ds(pl.multiple_of(row, SUBLANES), bit * SUBLANES)],
                                             sem_zero))
                row = row + jnp.where(take, bit * SUBLANES, 0)
            return carry
        lax.fori_loop(0, n_areas, area_body, 0)

    @pl.when(n_gen > 0)
    def _():
        for_each_weight_copy(0, lambda cp: cp.start())

    zbuf[...] = jnp.zeros(zbuf.shape, U32)
    ybuf[...] = jnp.zeros(ybuf.shape, U32)
    dump = pltpu.make_async_copy(zbuf.at[pl.ds(0, DUMP_CHUNKS * SUBLANES)],
                                 ys_hbm.at[pl.ds(dump_row0, DUMP_CHUNKS * SUBLANES)], sem_dump)
    dump.start()
    dump.wait()
    for_each_tail_piece(lambda cp: cp.start())
    for sl in range(SCATTER_BUFS - SCATTER_LAG):
        for c in range(BLOCK_CHUNKS):
            out_copy(pl.ds(dump_row0 + (sl * BLOCK_CHUNKS + c) * SUBLANES, SUBLANES), sl, c).start()
    for k in range(lookahead):
        start_gather(k)

    @pl.when(total > 0)
    def _():
        switch_weights(0)
        wait_gather(0)
        rows = load_rows(0)
        issue_dmas(0)
        up_gate(0, rows, 0)

        def block_body(blk, carry):
            gen = bgen_ref[blk]
            gen_prev = bgen_ref[blk - 1]

            @pl.when(gen != gen_prev)
            def _():
                switch_weights(gen)

            wait_gather(blk)
            wait_scatter((blk - 1) % SCATTER_BUFS)
            rows = load_rows(blk)
            issue_dmas(blk)
            down(blk - 1, gen_prev % 2)
            up_gate(blk, rows, gen % 2)
            return carry

        lax.fori_loop(1, total, block_body, 0)
        wait_scatter((total - 1) % SCATTER_BUFS)
        down(total - 1, bgen_ref[total - 1] % 2)

    for k in range(SCATTER_LAG):
        start_scatter(total - SCATTER_LAG + k)
    for sl in range(SCATTER_BUFS):
        wait_scatter(sl)
    for k in range(lookahead):
        wait_gather(total + k)
    for_each_tail_piece(lambda cp: cp.wait())


def _moe_combine_kernel(h_ref, route_ref, ys_ref, l2g_ref, l2b_ref, y_ref):
    n = h_ref.shape[0]
    scope = n // (ys_ref.shape[0] // AREA_ROWS)
    s_iota = lax.broadcasted_iota(jnp.int32, (scope, AREA_ROWS), 1).astype(F32)
    for sc in range(n // scope):
        rows = slice(sc * scope, (sc + 1) * scope)
        route = route_ref[rows, :]
        slot1 = route[:, 0:1]
        slot2 = route[:, 1:2]
        gate1 = route[:, 2:3]
        gate2 = route[:, 3:4]
        gmat = (jnp.where(s_iota == slot1, gate1, 0.0) + jnp.where(s_iota == slot2, gate2, 0.0)).astype(BF16)
        y_lo, y_hi = _unpack_rows(ys_ref[sc * AREA_ROWS:(sc + 1) * AREA_ROWS, :])
        f = jnp.concatenate([jnp.dot(gmat, y_lo, preferred_element_type=F32),
                             jnp.dot(gmat, y_hi, preferred_element_type=F32)], axis=1)
        y_ref[rows, :] = _layer_norm(ALPHA * h_ref[rows, :] + f, l2g_ref[...], l2b_ref[...])


def _excl_cumsum(a, axis):
    return jnp.cumsum(a, axis=axis) - a


def _plan_blocks(cnt, q_max):
    n_areas = cnt.shape[0]
    c8 = (cnt + (SUBLANES - 1)) // SUBLANES
    used8 = jnp.sum(c8, axis=1)
    base8 = jnp.arange(n_areas, dtype=jnp.int32) * AREA_CHUNKS
    src_runs = (base8[:, None] + _excl_cumsum(c8, 1)).T
    len_runs = c8.T
    tot = jnp.sum(len_runs, axis=1)
    ptot = (tot + BLOCK_CHUNKS - 1) // BLOCK_CHUNKS * BLOCK_CHUNKS
    pstart = _excl_cumsum(ptot, 0)
    dst_runs = pstart[:, None] + _excl_cumsum(len_runs, 1)
    off = (src_runs - dst_runs).reshape(-1)
    diff = off - jnp.concatenate([jnp.zeros((1,), jnp.int32), off[:-1]])
    q = jnp.arange(q_max, dtype=jnp.int32)
    src = q + jnp.sum(jnp.where(dst_runs.reshape(1, -1) <= q[:, None], diff[None, :], 0), axis=1)
    valid = jnp.any((pstart[None, :] <= q[:, None]) & (q[:, None] < (pstart + tot)[None, :]), axis=1)
    zero_chunk = AREA_CHUNKS - 1
    ahead = jnp.full(((GATHER_BUFS - 1) * BLOCK_CHUNKS,), zero_chunk, jnp.int32)
    src_tab = jnp.concatenate([jnp.where(valid, src, zero_chunk).astype(jnp.int32), ahead])
    qv = jnp.arange(-SCATTER_LAG * BLOCK_CHUNKS, q_max, dtype=jnp.int32)
    dump = n_areas * AREA_CHUNKS + ((qv // BLOCK_CHUNKS) % SCATTER_BUFS) * BLOCK_CHUNKS + qv % BLOCK_CHUNKS
    lead = jnp.zeros((SCATTER_LAG * BLOCK_CHUNKS,), jnp.int32)
    dst_tab = jnp.where(jnp.concatenate([lead > 0, valid]), jnp.concatenate([lead, src]), dump).astype(jnp.int32)
    nblk = ptot // BLOCK_CHUNKS
    bstart = pstart // BLOCK_CHUNKS
    has = nblk > 0
    gen_of = jnp.cumsum(has.astype(jnp.int32)) - 1
    experts = jnp.arange(N_EXPERTS, dtype=jnp.int32)
    elist = jnp.sum(jnp.where(has[None, :] & (gen_of[None, :] == experts[:, None]), experts[None, :], 0), axis=1)
    blk = jnp.arange(q_max // BLOCK_CHUNKS, dtype=jnp.int32)
    owner = (bstart[None, :] <= blk[:, None]) & (blk[:, None] < (bstart + nblk)[None, :])
    bgen = jnp.sum(jnp.where(owner, gen_of[None, :], 0), axis=1)
    meta = jnp.stack([jnp.sum(nblk), jnp.sum(has.astype(jnp.int32))])
    return (src_tab, dst_tab, bgen.astype(jnp.int32), elist.astype(jnp.int32), meta.astype(jnp.int32),
            used8.astype(jnp.int32))


PREP_STEPS = 4


def _swap_head_halves(a):
    half = HEAD_DIM // 2
    pieces = []
    for g in range(a.shape[-1] // LANES):
        p = a[:, g * LANES:(g + 1) * LANES]
        lane = lax.broadcasted_iota(jnp.int32, p.shape, 1)
        pieces.append(jnp.where((lane & half) == 0, pltpu.roll(p, LANES - half, 1), pltpu.roll(p, half, 1)))
    return jnp.concatenate(pieces, axis=-1)


def _glu_blocks(a):
    lo = ATTN_DIM + 2 * KV_DIM
    pieces = []
    for cb in range(CONV_CH // LANES):
        pieces.append(a[:, lo + cb * LANES:lo + (cb + 1) * LANES])
        pieces.append(a[:, lo + CONV_CH + cb * LANES:lo + CONV_CH + (cb + 1) * LANES])
    return jnp.concatenate(pieces, axis=-1)


def _prep_weights_kernel(w_in_ref, b_in_ref, w_out_ref, w_in_b_ref, w_sw_ref, w_glu_ref, w_out_b_ref,
                         b_sw_ref, b_glu_ref):
    qk = ATTN_DIM + KV_DIM
    w = w_in_ref[...]
    w_in_b_ref[...] = w.astype(BF16)
    w_sw_ref[...] = _swap_head_halves(w[:, :qk]).astype(BF16)
    w_glu_ref[...] = _glu_blocks(w).astype(BF16)
    w_out_b_ref[...] = w_out_ref[...].astype(BF16)
    b = jnp.broadcast_to(b_in_ref[...], (SUBLANES, IN_DIM))
    b_sw_ref[...] = _swap_head_halves(b[:, :qk])[0:1]
    b_glu_ref[...] = _glu_blocks(b)[0:1]


def _rope_tables(pos):
    half = HEAD_DIM // 2
    inv = (ROPE_THETA ** (-np.arange(half, dtype=np.float64) / half)).astype(np.float32)
    ang = (np.asarray(pos, np.float32)[:, None] * inv).astype(np.float64)
    cos = np.cos(ang)
    sin = np.sin(ang)
    reps = LANES // HEAD_DIM
    cos_t = np.concatenate([cos, cos] * reps, axis=1)
    sin_t = np.concatenate([-sin, sin] * reps, axis=1)
    q_scale = ATTN_SCALE * LOG2E
    return jnp.asarray(np.concatenate([cos_t, sin_t, cos_t * q_scale, sin_t * q_scale], axis=1), F32)


def _full(shape):
    nd = len(shape)
    return pl.BlockSpec(shape, lambda *_: (0,) * nd)


def kernel(x_prompt, x_sample, cache_k, cache_v, state_conv, w_in, b_in, sinks, conv_w, conv_b, ln_conv_g,
           ln_conv_b, w_out, b_out, ln1_g, ln1_b, w_router_group, w_router_expert, w1, w3, w2, ln2_g, ln2_b):
    assert w_in.shape[0] == DEPTH
    bsz, seq, _ = x_prompt.shape
    dec_b, dec_t, _ = x_sample.shape
    assert dec_t == 1 and seq % TILE_Q == 0 and dec_b % SAMPLE_TB == 0 and 2 * dec_b + 7 * N_EXPERTS <= AREA_ROWS
    assert (bsz * seq) % COMBINE_TILE == 0
    n_prompt = bsz * seq
    tiles = seq // TILE_Q
    spt = TILE_Q // PROMPT_SCOPE
    p_areas = n_prompt // PROMPT_SCOPE
    n_areas = p_areas + spt
    q_max = (n_areas * AREA_CHUNKS + N_EXPERTS * (BLOCK_CHUNKS - 1) + BLOCK_CHUNKS - 1) // BLOCK_CHUNKS * BLOCK_CHUNKS

    row = lambda a: a.reshape(1, -1)
    cparams = lambda sem: pltpu.CompilerParams(dimension_semantics=sem, vmem_limit_bytes=VMEM_LIMIT)

    qk = ATTN_DIM + KV_DIM
    wrows = D_MODEL // PREP_STEPS
    wspec = lambda cols: pl.BlockSpec((wrows, cols), lambda t: (t, 0))
    w_in_b, w_sw_b, w_glu_b, w_out_b, b_sw, b_glu = pl.pallas_call(
        _prep_weights_kernel,
        grid=(PREP_STEPS,),
        in_specs=[wspec(IN_DIM), _full((1, IN_DIM)), wspec(D_MODEL)],
        out_specs=[wspec(IN_DIM), wspec(qk), wspec(2 * CONV_CH), wspec(D_MODEL),
                   _full((1, qk)), _full((1, 2 * CONV_CH))],
        out_shape=[jax.ShapeDtypeStruct((D_MODEL, IN_DIM), BF16), jax.ShapeDtypeStruct((D_MODEL, qk), BF16),
                   jax.ShapeDtypeStruct((D_MODEL, 2 * CONV_CH), BF16), jax.ShapeDtypeStruct((D_MODEL, D_MODEL), BF16),
                   jax.ShapeDtypeStruct((1, qk), F32), jax.ShapeDtypeStruct((1, 2 * CONV_CH), F32)],
        compiler_params=cparams(("arbitrary",)),
        name="prep_weights",
    )(w_in[0], row(b_in[0]), w_out[0])
    w_r = jnp.concatenate([w_router_group[0], w_router_expert[0].reshape(D_MODEL, N_EXPERTS),
                           jnp.zeros((D_MODEL, LANES - ELANE0 - N_EXPERTS), F32)], axis=1).astype(BF16)
    shared = (w_in_b, row(b_in[0]), w_sw_b, b_sw, conv_w[0], row(conv_b[0]),
              row(ln_conv_g[0]), row(ln_conv_b[0]), w_out_b, row(b_out[0]), row(ln1_g[0]), row(ln1_b[0]), w_r)
    shared_specs = [_full(a.shape) for a in shared]
    glu = (w_glu_b, b_glu)
    glu_specs = [_full(a.shape) for a in glu]
    assert len(glu) + len(shared) == N_PROMPT_WEIGHTS
    smem = pl.BlockSpec(memory_space=pltpu.SMEM)
    rope_p = _rope_tables(np.arange(seq))
    rope_s = _rope_tables(PAST_LEN + np.arange(dec_t))

    to_window_minor = lambda a: jnp.transpose(a[0], (0, 2, 3, 1)).reshape(dec_b, KV_DIM, WINDOW)
    from_window_minor = lambda a: jnp.transpose(a.reshape(-1, N_KV_HEADS, HEAD_DIM, WINDOW), (0, 3, 1, 2))[None]
    from_tap_major = lambda a: jnp.transpose(a, (1, 0, 2))[None]
    ck = to_window_minor(cache_k)
    cv = to_window_minor(cache_v)
    st = jnp.transpose(state_conv[0], (1, 0, 2))
    tb = SAMPLE_TB
    cache_spec = pl.BlockSpec((tb, KV_DIM, WINDOW), lambda t: (t, 0, 0))
    conv_state_spec = pl.BlockSpec((CONV_WIDTH - 1, tb, CONV_CH), lambda t: (0, t, 0))
    h_s, nk_s, nv_s, nc_s, xs_s, route_s, cnt_s = pl.pallas_call(
        _mix_sample_kernel,
        grid=(dec_b // tb,),
        in_specs=[smem,
                  pl.BlockSpec((tb, None, D_MODEL), lambda t: (t, 0, 0)),
                  _full(rope_s.shape),
                  cache_spec, cache_spec, conv_state_spec] + shared_specs,
        out_specs=[pl.BlockSpec((tb, D_MODEL), lambda t: (t, 0)),
                   cache_spec, cache_spec, conv_state_spec,
                   _full((AREA_ROWS, HALF)),
                   _full((dec_b, LANES)),
                   _full((1, SUBLANES, LANES))],
        out_shape=[jax.ShapeDtypeStruct((dec_b, D_MODEL), F32),
                   jax.ShapeDtypeStruct((dec_b, KV_DIM, WINDOW), F32),
                   jax.ShapeDtypeStruct((dec_b, KV_DIM, WINDOW), F32),
                   jax.ShapeDtypeStruct((CONV_WIDTH - 1, dec_b, CONV_CH), F32),
                   jax.ShapeDtypeStruct((AREA_ROWS, HALF), U32),
                   jax.ShapeDtypeStruct((dec_b, LANES), F32),
                   jax.ShapeDtypeStruct((1, SUBLANES, LANES), F32)],
        scratch_shapes=[pltpu.VMEM((dec_b, D_MODEL), F32)],
        compiler_params=cparams(("arbitrary",)),
        name="mix_sample",
    )(sinks[0], x_sample, rope_s, ck, cv, st, *shared)

    n_real = bsz * tiles
    real = lambda t: jnp.minimum(t, n_real - 1)
    h_p, xs, route_p, cnt_p, nk_p, nv_p, nc_p = pl.pallas_call(
        functools.partial(_mix_prompt_kernel, tiles=tiles),
        grid=(n_real + 1,),
        in_specs=[smem,
                  pl.BlockSpec((1, TILE_Q, D_MODEL), lambda t: (real(t) // tiles, real(t) % tiles, 0)),
                  pl.BlockSpec((TILE_Q, 4 * LANES), lambda t: (real(t) % tiles, 0)),
                  _full((AREA_ROWS, HALF))] + glu_specs + shared_specs,
        out_specs=[pl.BlockSpec((1, TILE_Q, D_MODEL), lambda t: (real(t) // tiles, real(t) % tiles, 0)),
                   pl.BlockSpec((spt * AREA_ROWS, HALF), lambda t: (t, 0)),
                   pl.BlockSpec((TILE_Q, LANES), lambda t: (real(t), 0)),
                   pl.BlockSpec((spt, SUBLANES, LANES), lambda t: (t, 0, 0)),
                   pl.BlockSpec((1, KV_DIM, WINDOW), lambda t: (real(t) // tiles, 0, 0)),
                   pl.BlockSpec((1, KV_DIM, WINDOW), lambda t: (real(t) // tiles, 0, 0)),
                   pl.BlockSpec((1, CONV_WIDTH - 1, CONV_CH), lambda t: (real(t) // tiles, 0, 0))],
        out_shape=[jax.ShapeDtypeStruct((bsz, seq, D_MODEL), F32),
                   jax.ShapeDtypeStruct((n_areas * AREA_ROWS, HALF), U32),
                   jax.ShapeDtypeStruct((n_prompt, LANES), F32),
                   jax.ShapeDtypeStruct((n_areas, SUBLANES, LANES), F32),
                   jax.ShapeDtypeStruct((bsz, KV_DIM, WINDOW), F32),
                   jax.ShapeDtypeStruct((bsz, KV_DIM, WINDOW), F32),
                   jax.ShapeDtypeStruct((bsz, CONV_WIDTH - 1, CONV_CH), F32)],
        scratch_shapes=[pltpu.VMEM((WINDOW, KV_DIM), F32), pltpu.VMEM((WINDOW, KV_DIM), F32),
                        pltpu.VMEM((TILE_Q + CONV_PRE, CONV_CH), F32),
                        pltpu.VMEM((SUBLANES - 1, TILE_Q + CONV_SHIFT_EXTRA, CONV_CH), F32)],
        compiler_params=cparams(("arbitrary",)),
        name="mix_prompt",
    )(sinks[0], x_prompt, rope_p, xs_s, *glu, *shared)

    cnt = jnp.concatenate([cnt_p[:p_areas, 0], cnt_s[:, 0], cnt_p[p_areas + 1:, 0]], axis=0)
    cnt = cnt[:, ELANE0:ELANE0 + N_EXPERTS].astype(jnp.int32)
    plan = _plan_blocks(cnt, q_max)
    hbm = pl.BlockSpec(memory_space=pl.ANY)
    up_shape, down_shape = (2, D_MODEL, D_FF_EXPERT), (2, D_FF_EXPERT, D_MODEL)
    ys = pl.pallas_call(
        _moe_ffn_kernel,
        grid_spec=pltpu.PrefetchScalarGridSpec(
            num_scalar_prefetch=len(plan),
            grid=(1,),
            in_specs=[hbm, hbm, hbm, hbm],
            out_specs=hbm,
            scratch_shapes=[pltpu.VMEM((GATHER_BUFS, BLOCK_ROWS, HALF), U32),
                            pltpu.VMEM((SCATTER_BUFS, BLOCK_ROWS, HALF), U32),
                            pltpu.VMEM((SCATTER_BUFS, BLOCK_ROWS, D_FF_EXPERT), BF16),
                            pltpu.VMEM((ZERO_ROWS, HALF), U32),
                            pltpu.VMEM(up_shape, F32), pltpu.VMEM(up_shape, F32), pltpu.VMEM(down_shape, F32),
                            pltpu.VMEM(up_shape, BF16), pltpu.VMEM(up_shape, BF16), pltpu.VMEM(down_shape, BF16),
                            pltpu.SemaphoreType.DMA((GATHER_BUFS,)), pltpu.SemaphoreType.DMA((SCATTER_BUFS,)),
                            pltpu.SemaphoreType.DMA(()), pltpu.SemaphoreType.DMA(()),
                            pltpu.SemaphoreType.DMA((2,))]),
        out_shape=jax.ShapeDtypeStruct((n_areas * AREA_ROWS + DUMP_CHUNKS * SUBLANES, HALF), U32),
        compiler_params=cparams(("arbitrary",)),
        name="moe_ffn",
    )(*plan, xs, w1[0], w3[0], w2[0])

    def combine(h2d, route, first_area, scope, tile):
        n = h2d.shape[0]
        apt = tile // scope
        return pl.pallas_call(
            _moe_combine_kernel,
            grid=(n // tile,),
            in_specs=[pl.BlockSpec((tile, D_MODEL), lambda t: (t, 0)),
                      pl.BlockSpec((tile, LANES), lambda t: (t, 0)),
                      pl.BlockSpec((apt * AREA_ROWS, HALF), lambda t: (first_area // apt + t, 0)),
                      _full((1, D_MODEL)), _full((1, D_MODEL))],
            out_specs=pl.BlockSpec((tile, D_MODEL), lambda t: (t, 0)),
            out_shape=jax.ShapeDtypeStruct((n, D_MODEL), F32),
            compiler_params=cparams(("arbitrary",)),
            name="moe_combine",
        )(h2d, route, ys, row(ln2_g[0]), row(ln2_b[0]))

    y_p = combine(h_p.reshape(n_prompt, D_MODEL), route_p, 0, PROMPT_SCOPE, COMBINE_TILE)
    y_s = combine(h_s, route_s, p_areas, dec_b, dec_b)

    return (y_p.reshape(bsz, seq, D_MODEL), y_s.reshape(dec_b, dec_t, D_MODEL),
            from_window_minor(nk_p), from_window_minor(nv_p), nc_p[None],
            from_window_minor(nk_s), from_window_minor(nv_s), from_tap_major(nc_s))
```

```python
import functools

import jax
import jax.numpy as jnp
import numpy as np
from jax import lax
from jax.experimental import pallas as pl
from jax.experimental.pallas import tpu as pltpu

F32 = jnp.float32
BF16 = jnp.bfloat16
U32 = jnp.uint32

D_MODEL = 1024
N_HEADS = 8
N_KV_HEADS = 2
HEAD_DIM = 64
Q_GROUP = N_HEADS // N_KV_HEADS
ATTN_DIM = N_HEADS * HEAD_DIM
KV_DIM = N_KV_HEADS * HEAD_DIM
WINDOW = 128
ROPE_THETA = 10000.0
ATTN_SCALE = HEAD_DIM ** -0.5
CONV_CH = D_MODEL - ATTN_DIM
CONV_WIDTH = 31
IN_DIM = ATTN_DIM + 2 * KV_DIM + 2 * CONV_CH
N_GROUPS = 4
EXPERTS_PER_GROUP = 8
N_EXPERTS = N_GROUPS * EXPERTS_PER_GROUP
D_FF_EXPERT = 256
LN_EPS = 1e-5
NEG_INF = -1e30
DEPTH = 1
ALPHA = (2.0 * DEPTH) ** 0.25
PAST_LEN = 16384
LOG2E = 1.4426950408889634

LANES = 128
SUBLANES = 8
HALF = D_MODEL // 2
ELANE0 = N_GROUPS
PROMPT_SCOPE = 256
AREA_ROWS = -(-(2 * PROMPT_SCOPE + (SUBLANES - 1) * N_EXPERTS) // LANES) * LANES
AREA_CHUNKS = AREA_ROWS // SUBLANES
BLOCK_CHUNKS = 32
BLOCK_ROWS = BLOCK_CHUNKS * SUBLANES
TILE_Q = 512
COMBINE_TILE = 1024
CONV_PRE = 32
CONV_SHIFT_EXTRA = 24
SAMPLE_TB = 32
SAMPLE_GRP = 8
VMEM_LIMIT = 56 * 1024 * 1024


def _layer_norm(x, g, b):
    mu = jnp.mean(x, axis=-1, keepdims=True)
    xc = x - mu
    var = jnp.mean(xc * xc, axis=-1, keepdims=True)
    return xc * lax.rsqrt(var + LN_EPS) * g + b


def _rope(x, x_swapped, cos, sin_signed):
    pieces = [x[:, g * LANES:(g + 1) * LANES] * cos + x_swapped[:, g * LANES:(g + 1) * LANES] * sin_signed
              for g in range(x.shape[-1] // LANES)]
    return pieces[0] if len(pieces) == 1 else jnp.concatenate(pieces, axis=-1)


def _in_proj_glu(xb, w_in_ref, b_in_ref):
    lo, hi = ATTN_DIM + 2 * KV_DIM, IN_DIM
    cacg = jnp.dot(xb, w_in_ref[:, lo:hi], preferred_element_type=F32) + b_in_ref[:, lo:hi]
    return cacg[:, :CONV_CH] * jax.nn.sigmoid(cacg[:, CONV_CH:])


def _in_proj_qkv(xb, w_in_ref, b_in_ref, w_sw_ref, b_sw_ref, rope):
    o_v, o_e = ATTN_DIM + KV_DIM, ATTN_DIM + 2 * KV_DIM
    qkv = jnp.dot(xb, w_in_ref[:, 0:o_e], preferred_element_type=F32) + b_in_ref[:, 0:o_e]
    sw = jnp.dot(xb, w_sw_ref[...], preferred_element_type=F32) + b_sw_ref[...]
    k = _rope(qkv[:, ATTN_DIM:o_v], sw[:, ATTN_DIM:], rope[:, 0:LANES], rope[:, LANES:2 * LANES])
    q = _rope(qkv[:, :ATTN_DIM], sw[:, :ATTN_DIM], rope[:, 2 * LANES:3 * LANES], rope[:, 3 * LANES:])
    return q, k, qkv[:, o_v:]


def _pack_rows(v):
    lo = lax.bitcast_convert_type(v[:, :HALF], U32)
    hi = lax.bitcast_convert_type(v[:, HALF:], U32)
    return (lo >> 16) | hi


def _unpack_rows(w):
    lo = lax.bitcast_convert_type(w << 16, F32).astype(BF16)
    hi = lax.bitcast_convert_type(w & jnp.uint32(0xFFFF0000), F32).astype(BF16)
    return lo, hi


def _drain(stages):
    try:
        while True:
            next(stages)
    except StopIteration as stop:
        return stop.value


def _round_robin(*stage_generators):
    live = list(stage_generators)
    while live:
        for stages in list(live):
            try:
                next(stages)
            except StopIteration:
                live.remove(stages)


def _route_and_sort_stages(h, w_r):
    n = h.shape[0]
    hb = h.astype(BF16)
    logits = jnp.dot(hb, w_r, preferred_element_type=F32)
    yield
    lane = lax.broadcasted_iota(jnp.int32, (n, LANES), 1)
    lane_f = lane.astype(F32)
    is_g = lane < N_GROUPS
    lg = jnp.where(is_g, logits, -jnp.inf)
    gmax = jnp.max(lg, axis=-1, keepdims=True)
    yield
    gidx = jnp.min(jnp.where(lg == gmax, lane_f, float(LANES)), axis=-1, keepdims=True)
    p_g = 1.0 / jnp.sum(jnp.where(is_g, jnp.exp(logits - gmax), 0.0), axis=-1, keepdims=True)
    yield
    egrp = ((lane - ELANE0) >> (EXPERTS_PER_GROUP.bit_length() - 1)).astype(F32)
    emask = (lane >= ELANE0) & (lane < ELANE0 + N_EXPERTS) & (egrp == gidx)
    sel = jnp.where(emask, logits, -jnp.inf)
    v1 = jnp.max(sel, axis=-1, keepdims=True)
    yield
    i1 = jnp.min(jnp.where(sel == v1, lane_f, float(LANES)), axis=-1, keepdims=True)
    yield
    sel2 = jnp.where(lane_f == i1, -jnp.inf, sel)
    v2 = jnp.max(sel2, axis=-1, keepdims=True)
    yield
    i2 = jnp.min(jnp.where(sel2 == v2, lane_f, float(LANES)), axis=-1, keepdims=True)
    t = jnp.exp(v2 - v1)
    gate1 = p_g / (1.0 + t)
    gate2 = p_g * t / (1.0 + t)
    pick1 = lane_f == i1
    pick2 = lane_f == i2
    member = jnp.where(pick1 | pick2, 1.0, 0.0)
    yield

    r_i = lax.broadcasted_iota(jnp.int32, (n, n), 0)
    c_i = lax.broadcasted_iota(jnp.int32, (n, n), 1)
    lower = jnp.where(c_i < r_i, 1.0, 0.0).astype(BF16)
    e_r = lax.broadcasted_iota(jnp.int32, (LANES, LANES), 0)
    e_c = lax.broadcasted_iota(jnp.int32, (LANES, LANES), 1)
    upper = jnp.where(e_r < e_c, 1.0, 0.0).astype(BF16)
    before = jnp.dot(lower, member.astype(BF16), preferred_element_type=F32)
    cnt = jnp.sum(member, axis=0, keepdims=True)
    c8 = jnp.floor((cnt + (SUBLANES - 1)) * (1.0 / SUBLANES))
    c8b = jnp.broadcast_to(c8, (SUBLANES, LANES)).astype(BF16)
    off8 = jnp.dot(c8b, upper, preferred_element_type=F32)[0:1]
    yield
    slot_all = before + off8 * float(SUBLANES)
    slot1 = jnp.sum(jnp.where(pick1, slot_all, 0.0), axis=-1, keepdims=True)
    slot2 = jnp.sum(jnp.where(pick2, slot_all, 0.0), axis=-1, keepdims=True)
    yield
    s_iota = lax.broadcasted_iota(jnp.int32, (n, AREA_ROWS), 1).astype(F32)
    perm_t = jnp.where((s_iota == slot1) | (s_iota == slot2), 1.0, 0.0).astype(BF16)
    sorted_rows = lax.dot_general(perm_t, hb, (((0,), (0,)), ((), ())),
                                  preferred_element_type=F32)
    yield
    route = jnp.where(lane == 0, slot1, jnp.where(lane == 1, slot2, jnp.where(lane == 2, gate1,
                      jnp.where(lane == 3, gate2, 0.0))))
    return route, _pack_rows(sorted_rows), cnt


def _softmax_rows_with_sink(s, sink, extra=None):
    m = jnp.maximum(jnp.max(s, axis=-1, keepdims=True), sink)
    if extra is not None:
        m = jnp.maximum(m, extra)
    e = jnp.exp2(s - m)
    den = jnp.sum(e, axis=-1, keepdims=True) + jnp.exp2(sink - m)
    if extra is not None:
        ee = jnp.exp2(extra - m)
        den = den + ee
        return e / den, ee / den
    return e / den, None


def _mix_out_stages(x, attn, conv_pre, lcg, lcb, w_out_ref, b_out, l1g, l1b):
    conv = _layer_norm(conv_pre, lcg, lcb)
    conv = conv * jax.nn.sigmoid(conv)
    yield
    mixed = jnp.concatenate([attn.astype(BF16), conv.astype(BF16)], axis=1)
    mix = jnp.dot(mixed, w_out_ref[...], preferred_element_type=F32) + b_out
    yield
    h = _layer_norm(ALPHA * x + mix, l1g, l1b)
    yield
    return h


N_PROMPT_WEIGHTS = 15


def _mix_prompt_kernel(sinks_ref, x_ref, rope_ref, xs_s_ref, *rest, tiles):
    _, xs_ref, _, cnt_ref = rest[N_PROMPT_WEIGHTS:N_PROMPT_WEIGHTS + 4]
    step = pl.program_id(0)
    n_real = pl.num_programs(0) - 1

    @pl.when(step < n_real)
    def _():
        _mix_prompt_tile(sinks_ref, x_ref, rope_ref, *rest, i=step % tiles)

    @pl.when(step == n_real)
    def _():
        xs_ref[0:AREA_ROWS, :] = xs_s_ref[...]
        xs_ref[AREA_ROWS:, :] = jnp.zeros((xs_ref.shape[0] - AREA_ROWS, HALF), U32)
        cnt_ref[...] = jnp.zeros(cnt_ref.shape, F32)


def _mix_prompt_tile(sinks_ref, x_ref, rope_ref, w_glu_ref, b_glu_ref,
                     w_in_ref, b_in_ref, w_sw_ref, b_sw_ref, conv_w_ref, conv_b_ref,
                     lcg_ref, lcb_ref, w_out_ref, b_out_ref, l1g_ref, l1b_ref, w_r_ref,
                     h_ref, xs_ref, route_ref, cnt_ref, nk_ref, nv_ref, nc_ref,
                     kprev, vprev, u_scr, u_shift, xb_scr, mix_scr, *, i):
    tq = x_ref.shape[1]
    pre = CONV_PRE
    n_sc = tq // PROMPT_SCOPE

    @pl.when(i == 0)
    def _():
        kprev[...] = jnp.zeros_like(kprev)
        vprev[...] = jnp.zeros_like(vprev)
        u_scr[0:pre, :] = jnp.zeros((pre, CONV_CH), F32)

    xb_scr[...] = x_ref[0].astype(BF16)

    def glu_block(cb):
        cols = slice(2 * LANES * cb, 2 * LANES * (cb + 1))
        vg = jnp.dot(xb_scr[...], w_glu_ref[:, cols], preferred_element_type=F32) + b_glu_ref[:, cols]
        return vg[:, :LANES] * jax.nn.sigmoid(vg[:, LANES:])

    n_cb = CONV_CH // LANES
    acc_blocks = [[None] * n_cb for _ in range(n_sc)]
    u_next = glu_block(0)
    for cb in range(n_cb):
        lanes = slice(LANES * cb, LANES * (cb + 1))
        u_scr[pre:pre + tq, lanes] = u_next
        if cb + 1 < n_cb:
            u_next = glu_block(cb + 1)
        for r in range(1, SUBLANES):
            u_shift[r - 1, :, lanes] = u_scr[r:r + tq + CONV_SHIFT_EXTRA, lanes]
        for sc in range(n_sc):
            base = sc * PROMPT_SCOPE
            acc = jnp.zeros((PROMPT_SCOPE, LANES), F32)
            for j in range(CONV_WIDTH):
                a, r = divmod(pre - (CONV_WIDTH - 1) + j, SUBLANES)
                start = base + a * SUBLANES
                src = (u_scr[start:start + PROMPT_SCOPE, lanes] if r == 0
                       else u_shift[r - 1, start:start + PROMPT_SCOPE, lanes])
                acc = acc + src * conv_w_ref[j:j + 1, lanes]
            acc_blocks[sc][cb] = acc
    for sc, blocks in enumerate(acc_blocks):
        conv = _layer_norm(jnp.concatenate(blocks, axis=1) + conv_b_ref[...], lcg_ref[...], lcb_ref[...])
        mix_scr[sc * PROMPT_SCOPE:(sc + 1) * PROMPT_SCOPE, ATTN_DIM:] = (conv * jax.nn.sigmoid(conv)).astype(BF16)

    nc_ref[0] = u_scr[pre + tq - (CONV_WIDTH - 1):pre + tq, :]
    tail = u_scr[tq:tq + pre, :]
    u_scr[0:pre, :] = tail

    q, k, v = _in_proj_qkv(xb_scr[...], w_in_ref, b_in_ref, w_sw_ref, b_sw_ref, rope_ref[...])
    qb = q.astype(BF16)
    kb = jnp.concatenate([kprev[...], k], axis=0).astype(BF16)
    vb = jnp.concatenate([vprev[...], v], axis=0).astype(BF16)
    row = lax.broadcasted_iota(jnp.int32, (WINDOW, 2 * WINDOW), 0)
    col = lax.broadcasted_iota(jnp.int32, (WINDOW, 2 * WINDOW), 1)
    dist = row + WINDOW - col
    band = (dist >= 0) & (dist < WINDOW)
    first_col = jnp.where(i > 0, 0, WINDOW)
    kprev[...] = k[tq - WINDOW:, :]
    vprev[...] = v[tq - WINDOW:, :]
    nk_ref[0] = k[tq - WINDOW:, :].T
    nv_ref[0] = v[tq - WINDOW:, :].T
    blocks_per_scope = PROMPT_SCOPE // WINDOW

    def scope_attention(sc):
        for j in range(sc * blocks_per_scope, (sc + 1) * blocks_per_scope):
            msk = (band & (col >= first_col)) if j == 0 else band
            heads = [None] * N_HEADS
            for c in range(N_KV_HEADS):
                kc = kb[j * WINDOW:(j + 2) * WINDOW, c * HEAD_DIM:(c + 1) * HEAD_DIM]
                vc = vb[j * WINDOW:(j + 2) * WINDOW, c * HEAD_DIM:(c + 1) * HEAD_DIM]
                qc = jnp.concatenate(
                    [qb[j * WINDOW:(j + 1) * WINDOW, (c * Q_GROUP + g) * HEAD_DIM:(c * Q_GROUP + g + 1) * HEAD_DIM]
                     for g in range(Q_GROUP)], axis=0)
                s = lax.dot_general(qc, kc, (((1,), (1,)), ((), ())), preferred_element_type=F32)
                yield
                probs = []
                for g in range(Q_GROUP):
                    sg = jnp.where(msk, s[g * WINDOW:(g + 1) * WINDOW], NEG_INF)
                    p, _ = _softmax_rows_with_sink(sg, sinks_ref[c * Q_GROUP + g] * LOG2E)
                    probs.append(p.astype(BF16))
                    if g % 2 == 1:
                        yield
                o = jnp.dot(jnp.concatenate(probs, axis=0), vc, preferred_element_type=F32)
                for g in range(Q_GROUP):
                    heads[c * Q_GROUP + g] = o[g * WINDOW:(g + 1) * WINDOW]
                yield
            mix_scr[j * WINDOW:(j + 1) * WINDOW, 0:ATTN_DIM] = jnp.concatenate(heads, axis=1).astype(BF16)

    def scope_tail(sc):
        rows = slice(sc * PROMPT_SCOPE, (sc + 1) * PROMPT_SCOPE)
        mix = jnp.dot(mix_scr[rows, :], w_out_ref[...], preferred_element_type=F32) + b_out_ref[...]
        yield
        h = _layer_norm(ALPHA * x_ref[0, rows, :] + mix, l1g_ref[...], l1b_ref[...])
        yield
        route, area, cnt = yield from _route_and_sort_stages(h, w_r_ref[...])
        h_ref[0, rows, :] = h
        route_ref[rows, :] = route
        xs_ref[sc * AREA_ROWS:(sc + 1) * AREA_ROWS, :] = area
        cnt_ref[sc] = jnp.broadcast_to(cnt, (SUBLANES, LANES))

    _round_robin(*[scope_attention(sc) for sc in range(n_sc)])
    _round_robin(*[scope_tail(sc) for sc in range(n_sc)])


def _mix_sample_kernel(sinks_ref, x_ref, rope_ref, ck_ref, cv_ref, st_ref, w_in_ref, b_in_ref, w_sw_ref, b_sw_ref,
                       conv_w_ref, conv_b_ref, lcg_ref, lcb_ref, w_out_ref, b_out_ref, l1g_ref, l1b_ref,
                       w_r_ref,
                       h_ref, nk_ref, nv_ref, nc_ref, xs_ref, route_ref, cnt_ref,
                       h_scr):
    tb = x_ref.shape[0]
    n_tok = h_scr.shape[0]
    step = pl.program_id(0)
    x = x_ref[...]
    xb = x.astype(BF16)
    u = _in_proj_glu(xb, w_in_ref, b_in_ref)
    q, k_new, v_new = _in_proj_qkv(xb, w_in_ref, b_in_ref, w_sw_ref, b_sw_ref, rope_ref[...])

    qb = q.astype(BF16)
    knb = k_new.astype(BF16).astype(F32)
    vnb = v_new.astype(BF16).astype(F32)
    gq = SAMPLE_GRP * Q_GROUP
    nkeys = SAMPLE_GRP * WINDOW
    r_i = lax.broadcasted_iota(jnp.int32, (gq, nkeys), 0)
    c_i = lax.broadcasted_iota(jnp.int32, (gq, nkeys), 1)
    window_shift, grp_shift = WINDOW.bit_length() - 1, SAMPLE_GRP.bit_length() - 1
    msk = ((c_i >> window_shift) == (r_i & (SAMPLE_GRP - 1))) & ((c_i & (WINDOW - 1)) >= 1)
    r_col = lax.broadcasted_iota(jnp.int32, (gq, 1), 0) >> grp_shift
    attn_groups = [None] * (tb // SAMPLE_GRP)

    def group_attention(grp):
        r0 = grp * SAMPLE_GRP
        heads = [None] * N_HEADS
        for c in range(N_KV_HEADS):
            lanes = slice(c * HEAD_DIM, (c + 1) * HEAD_DIM)
            kt_all = jnp.concatenate([ck_ref[r0 + b, lanes, :] for b in range(SAMPLE_GRP)], axis=1).astype(BF16)
            vt_all = jnp.concatenate([cv_ref[r0 + b, lanes, :] for b in range(SAMPLE_GRP)], axis=1).astype(BF16)
            yield
            qc = jnp.concatenate(
                [qb[r0:r0 + SAMPLE_GRP, (c * Q_GROUP + g) * HEAD_DIM:(c * Q_GROUP + g + 1) * HEAD_DIM]
                 for g in range(Q_GROUP)], axis=0)
            s = jnp.dot(qc, kt_all, preferred_element_type=F32)
            s = jnp.where(msk, s, NEG_INF)
            kn = jnp.concatenate([knb[r0:r0 + SAMPLE_GRP, lanes]] * Q_GROUP, axis=0)
            vn = jnp.concatenate([vnb[r0:r0 + SAMPLE_GRP, lanes]] * Q_GROUP, axis=0)
            s_new = jnp.sum(qc.astype(F32) * kn, axis=-1, keepdims=True)
            sink = jnp.zeros((gq, 1), F32)
            for g in range(Q_GROUP):
                sink = jnp.where(r_col == g, sinks_ref[c * Q_GROUP + g] * LOG2E, sink)
            yield
            p, p_new = _softmax_rows_with_sink(s, sink, s_new)
            yield
            o = (lax.dot_general(p.astype(BF16), vt_all, (((1,), (1,)), ((), ())), preferred_element_type=F32)
                 + p_new.astype(BF16).astype(F32) * vn)
            for g in range(Q_GROUP):
                heads[c * Q_GROUP + g] = o[g * SAMPLE_GRP:(g + 1) * SAMPLE_GRP]
            yield
        attn_groups[grp] = jnp.concatenate(heads, axis=1)

    _round_robin(*[group_attention(grp) for grp in range(len(attn_groups))])
    attn = jnp.concatenate(attn_groups, axis=0)

    acc = u * conv_w_ref[CONV_WIDTH - 1:CONV_WIDTH, :] + conv_b_ref[...]
    for j in range(CONV_WIDTH - 1):
        acc = acc + st_ref[j] * conv_w_ref[j:j + 1, :]
    conv_pre = acc

    nc_ref[0:CONV_WIDTH - 2] = st_ref[1:CONV_WIDTH - 1]
    nc_ref[CONV_WIDTH - 2] = u
    k_cols = k_new.T
    v_cols = v_new.T
    last = lax.broadcasted_iota(jnp.int32, (KV_DIM, WINDOW), 1) == WINDOW - 1
    for b in range(tb):
        nk_ref[b] = jnp.where(last, k_cols[:, b:b + 1], pltpu.roll(ck_ref[b], WINDOW - 1, 1))
        nv_ref[b] = jnp.where(last, v_cols[:, b:b + 1], pltpu.roll(cv_ref[b], WINDOW - 1, 1))

    h = _drain(_mix_out_stages(x, attn, conv_pre, lcg_ref[...], lcb_ref[...], w_out_ref, b_out_ref[...],
                               l1g_ref[...], l1b_ref[...]))
    h_ref[...] = h
    h_scr[pl.ds(pl.multiple_of(step * tb, tb), tb), :] = h

    @pl.when(step == pl.num_programs(0) - 1)
    def _():
        route, area, cnt = _drain(_route_and_sort_stages(h_scr[...], w_r_ref[...]))
        route_ref[...] = route
        xs_ref[...] = area
        cnt_ref[0] = jnp.broadcast_to(cnt, (SUBLANES, LANES))


GATHER_BUFS = 8
SCATTER_BUFS = 8
SCATTER_LAG = 2
ZERO_BITS = (64, 32, 16, 8, 4, 2, 1)
DUMP_CHUNKS = SCATTER_BUFS * BLOCK_CHUNKS
ZERO_ROWS = max(ZERO_BITS[0], DUMP_CHUNKS) * SUBLANES


def _moe_ffn_kernel(src_ref, dst_ref, bgen_ref, elist_ref, meta_ref, used_ref,
                    xs_hbm, w1_hbm, w3_hbm, w2_hbm,
                    ys_hbm,
                    xbuf, ybuf, hbuf, zbuf, wst1, wst3, wst2, w1b, w3b, w2b,
                    sem_in, sem_out, sem_zero, sem_dump, sem_w):
    n_areas = used_ref.shape[0]
    total = meta_ref[0]
    n_gen = meta_ref[1]
    dump_row0 = n_areas * AREA_ROWS
    lookahead = GATHER_BUFS - 1
    any_rows = pl.ds(0, SUBLANES)

    def chunk_at(ref, idx):
        return pl.ds(pl.multiple_of(ref[idx] * SUBLANES, SUBLANES), SUBLANES)

    def in_copy(rows, sl, c):
        return pltpu.make_async_copy(xs_hbm.at[rows], xbuf.at[sl, pl.ds(c * SUBLANES, SUBLANES)], sem_in.at[sl])

    def out_copy(rows, sl, c):
        return pltpu.make_async_copy(ybuf.at[sl, pl.ds(c * SUBLANES, SUBLANES)], ys_hbm.at[rows], sem_out.at[sl])

    def start_gather(blk):
        for c in range(BLOCK_CHUNKS):
            in_copy(chunk_at(src_ref, blk * BLOCK_CHUNKS + c), blk % GATHER_BUFS, c).start()

    def wait_gather(blk):
        for c in range(BLOCK_CHUNKS):
            in_copy(any_rows, blk % GATHER_BUFS, c).wait()

    def start_scatter(blk):
        for c in range(BLOCK_CHUNKS):
            out_copy(chunk_at(dst_ref, (blk + SCATTER_LAG) * BLOCK_CHUNKS + c), blk % SCATTER_BUFS, c).start()

    def wait_scatter(sl):
        for c in range(BLOCK_CHUNKS):
            out_copy(any_rows, sl, c).wait()

    def load_rows(blk):
        return _unpack_rows(xbuf[blk % GATHER_BUFS])

    def issue_dmas(blk):
        start_scatter(blk - SCATTER_LAG)
        start_gather(blk + lookahead)

    def up_gate(blk, rows, ws):
        x_lo, x_hi = rows
        a1 = (jnp.dot(x_lo, w1b[ws, 0:HALF, :], preferred_element_type=F32)
              + jnp.dot(x_hi, w1b[ws, HALF:, :], preferred_element_type=F32))
        a3 = (jnp.dot(x_lo, w3b[ws, 0:HALF, :], preferred_element_type=F32)
              + jnp.dot(x_hi, w3b[ws, HALF:, :], preferred_element_type=F32))
        hbuf[blk % SCATTER_BUFS] = (a1 * jax.nn.sigmoid(a1) * a3).astype(BF16)

    def down(blk, ws):
        y = jnp.dot(hbuf[blk % SCATTER_BUFS], w2b[ws], preferred_element_type=F32)
        ybuf[blk % SCATTER_BUFS] = _pack_rows(y.astype(BF16).astype(F32))

    def for_each_weight_copy(gen, fn):
        ws = gen % 2
        expert = elist_ref[gen]
        for hbm, staging in ((w1_hbm, wst1), (w3_hbm, wst3), (w2_hbm, wst2)):
            fn(pltpu.make_async_copy(hbm.at[expert], staging.at[ws], sem_w.at[ws]))

    def switch_weights(gen):
        ws = gen % 2
        for_each_weight_copy(gen, lambda cp: cp.wait())
        w1b[ws] = wst1[ws].astype(BF16)
        w3b[ws] = wst3[ws].astype(BF16)
        w2b[ws] = wst2[ws].astype(BF16)

        @pl.when(gen + 1 < n_gen)
        def _():
            for_each_weight_copy(gen + 1, lambda cp: cp.start())

    def for_each_tail_piece(fn):
        def area_body(s, carry):
            used = used_ref[s]
            tail = AREA_CHUNKS - used
            row = (s * AREA_CHUNKS + used) * SUBLANES
            for bit in ZERO_BITS:
                take = (tail & bit) != 0

                @pl.when(take)
                def _(row=row, bit=bit):
                    fn(pltpu.make_async_copy(zbuf.at[pl.ds(0, bit * SUBLANES)],
                                             ys_hbm.at[pl.ds(pl.multiple_of(row, SUBLANES), bit * SUBLANES)],
                                             sem_zero))
                row = row + jnp.where(take, bit * SUBLANES, 0)
            return carry
        lax.fori_loop(0, n_areas, area_body, 0)

    @pl.when(n_gen > 0)
    def _():
        for_each_weight_copy(0, lambda cp: cp.start())

    zbuf[...] = jnp.zeros(zbuf.shape, U32)
    ybuf[...] = jnp.zeros(ybuf.shape, U32)
    dump = pltpu.make_async_copy(zbuf.at[pl.ds(0, DUMP_CHUNKS * SUBLANES)],
                                 ys_hbm.at[pl.ds(dump_row0, DUMP_CHUNKS * SUBLANES)], sem_dump)
    dump.start()
    dump.wait()
    for_each_tail_piece(lambda cp: cp.start())
    for sl in range(SCATTER_BUFS - SCATTER_LAG):
        for c in range(BLOCK_CHUNKS):
            out_copy(pl.ds(dump_row0 + (sl * BLOCK_CHUNKS + c) * SUBLANES, SUBLANES), sl, c).start()
    for k in range(lookahead):
        start_gather(k)

    @pl.when(total > 0)
    def _():
        switch_weights(0)
        wait_gather(0)
        rows = load_rows(0)
        issue_dmas(0)
        up_gate(0, rows, 0)

        def block_body(blk, carry):
            gen = bgen_ref[blk]
            gen_prev = bgen_ref[blk - 1]

            @pl.when(gen != gen_prev)
            def _():
                switch_weights(gen)

            wait_gather(blk)
            wait_scatter((blk - 1) % SCATTER_BUFS)
            rows = load_rows(blk)
            issue_dmas(blk)
            down(blk - 1, gen_prev % 2)
            up_gate(blk, rows, gen % 2)
            return carry

        lax.fori_loop(1, total, block_body, 0)
        wait_scatter((total - 1) % SCATTER_BUFS)
        down(total - 1, bgen_ref[total - 1] % 2)

    for k in range(SCATTER_LAG):
        start_scatter(total - SCATTER_LAG + k)
    for sl in range(SCATTER_BUFS):
        wait_scatter(sl)
    for k in range(lookahead):
        wait_gather(total + k)
    for_each_tail_piece(lambda cp: cp.wait())


def _moe_combine_kernel(h_ref, route_ref, ys_ref, l2g_ref, l2b_ref, y_ref):
    n = h_ref.shape[0]
    scope = n // (ys_ref.shape[0] // AREA_ROWS)
    s_iota = lax.broadcasted_iota(jnp.int32, (scope, AREA_ROWS), 1).astype(F32)
    for sc in range(n // scope):
        rows = slice(sc * scope, (sc + 1) * scope)
        route = route_ref[rows, :]
        slot1 = route[:, 0:1]
        slot2 = route[:, 1:2]
        gate1 = route[:, 2:3]
        gate2 = route[:, 3:4]
        gmat = (jnp.where(s_iota == slot1, gate1, 0.0) + jnp.where(s_iota == slot2, gate2, 0.0)).astype(BF16)
        y_lo, y_hi = _unpack_rows(ys_ref[sc * AREA_ROWS:(sc + 1) * AREA_ROWS, :])
        f = jnp.concatenate([jnp.dot(gmat, y_lo, preferred_element_type=F32),
                             jnp.dot(gmat, y_hi, preferred_element_type=F32)], axis=1)
        y_ref[rows, :] = _layer_norm(ALPHA * h_ref[rows, :] + f, l2g_ref[...], l2b_ref[...])


def _excl_cumsum(a, axis):
    return jnp.cumsum(a, axis=axis) - a


def _plan_blocks(cnt, q_max):
    n_areas = cnt.shape[0]
    c8 = (cnt + (SUBLANES - 1)) // SUBLANES
    used8 = jnp.sum(c8, axis=1)
    base8 = jnp.arange(n_areas, dtype=jnp.int32) * AREA_CHUNKS
    src_runs = (base8[:, None] + _excl_cumsum(c8, 1)).T
    len_runs = c8.T
    tot = jnp.sum(len_runs, axis=1)
    ptot = (tot + BLOCK_CHUNKS - 1) // BLOCK_CHUNKS * BLOCK_CHUNKS
    pstart = _excl_cumsum(ptot, 0)
    dst_runs = pstart[:, None] + _excl_cumsum(len_runs, 1)
    off = (src_runs - dst_runs).reshape(-1)
    diff = off - jnp.concatenate([jnp.zeros((1,), jnp.int32), off[:-1]])
    q = jnp.arange(q_max, dtype=jnp.int32)
    src = q + jnp.sum(jnp.where(dst_runs.reshape(1, -1) <= q[:, None], diff[None, :], 0), axis=1)
    valid = jnp.any((pstart[None, :] <= q[:, None]) & (q[:, None] < (pstart + tot)[None, :]), axis=1)
    zero_chunk = AREA_CHUNKS - 1
    ahead = jnp.full(((GATHER_BUFS - 1) * BLOCK_CHUNKS,), zero_chunk, jnp.int32)
    src_tab = jnp.concatenate([jnp.where(valid, src, zero_chunk).astype(jnp.int32), ahead])
    qv = jnp.arange(-SCATTER_LAG * BLOCK_CHUNKS, q_max, dtype=jnp.int32)
    dump = n_areas * AREA_CHUNKS + ((qv // BLOCK_CHUNKS) % SCATTER_BUFS) * BLOCK_CHUNKS + qv % BLOCK_CHUNKS
    lead = jnp.zeros((SCATTER_LAG * BLOCK_CHUNKS,), jnp.int32)
    dst_tab = jnp.where(jnp.concatenate([lead > 0, valid]), jnp.concatenate([lead, src]), dump).astype(jnp.int32)
    nblk = ptot // BLOCK_CHUNKS
    bstart = pstart // BLOCK_CHUNKS
    has = nblk > 0
    gen_of = jnp.cumsum(has.astype(jnp.int32)) - 1
    experts = jnp.arange(N_EXPERTS, dtype=jnp.int32)
    elist = jnp.sum(jnp.where(has[None, :] & (gen_of[None, :] == experts[:, None]), experts[None, :], 0), axis=1)
    blk = jnp.arange(q_max // BLOCK_CHUNKS, dtype=jnp.int32)
    owner = (bstart[None, :] <= blk[:, None]) & (blk[:, None] < (bstart + nblk)[None, :])
    bgen = jnp.sum(jnp.where(owner, gen_of[None, :], 0), axis=1)
    meta = jnp.stack([jnp.sum(nblk), jnp.sum(has.astype(jnp.int32))])
    return (src_tab, dst_tab, bgen.astype(jnp.int32), elist.astype(jnp.int32), meta.astype(jnp.int32),
            used8.astype(jnp.int32))


PREP_STEPS = 4


def _swap_head_halves(a):
    half = HEAD_DIM // 2
    pieces = []
    for g in range(a.shape[-1] // LANES):
        p = a[:, g * LANES:(g + 1) * LANES]
        lane = lax.broadcasted_iota(jnp.int32, p.shape, 1)
        pieces.append(jnp.where((lane & half) == 0, pltpu.roll(p, LANES - half, 1), pltpu.roll(p, half, 1)))
    return jnp.concatenate(pieces, axis=-1)


def _glu_blocks(a):
    lo = ATTN_DIM + 2 * KV_DIM
    pieces = []
    for cb in range(CONV_CH // LANES):
        pieces.append(a[:, lo + cb * LANES:lo + (cb + 1) * LANES])
        pieces.append(a[:, lo + CONV_CH + cb * LANES:lo + CONV_CH + (cb + 1) * LANES])
    return jnp.concatenate(pieces, axis=-1)


def _prep_weights_kernel(w_in_ref, b_in_ref, w_out_ref, w_in_b_ref, w_sw_ref, w_glu_ref, w_out_b_ref,
                         b_sw_ref, b_glu_ref):
    qk = ATTN_DIM + KV_DIM
    w = w_in_ref[...]
    w_in_b_ref[...] = w.astype(BF16)
    w_sw_ref[...] = _swap_head_halves(w[:, :qk]).astype(BF16)
    w_glu_ref[...] = _glu_blocks(w).astype(BF16)
    w_out_b_ref[...] = w_out_ref[...].astype(BF16)
    b = jnp.broadcast_to(b_in_ref[...], (SUBLANES, IN_DIM))
    b_sw_ref[...] = _swap_head_halves(b[:, :qk])[0:1]
    b_glu_ref[...] = _glu_blocks(b)[0:1]


def _rope_tables(pos):
    half = HEAD_DIM // 2
    inv = (ROPE_THETA ** (-np.arange(half, dtype=np.float64) / half)).astype(np.float32)
    ang = (np.asarray(pos, np.float32)[:, None] * inv).astype(np.float64)
    cos = np.cos(ang)
    sin = np.sin(ang)
    reps = LANES // HEAD_DIM
    cos_t = np.concatenate([cos, cos] * reps, axis=1)
    sin_t = np.concatenate([-sin, sin] * reps, axis=1)
    q_scale = ATTN_SCALE * LOG2E
    return jnp.asarray(np.concatenate([cos_t, sin_t, cos_t * q_scale, sin_t * q_scale], axis=1), F32)


def _full(shape):
    nd = len(shape)
    return pl.BlockSpec(shape, lambda *_: (0,) * nd)


def kernel(x_prompt, x_sample, cache_k, cache_v, state_conv, w_in, b_in, sinks, conv_w, conv_b, ln_conv_g,
           ln_conv_b, w_out, b_out, ln1_g, ln1_b, w_router_group, w_router_expert, w1, w3, w2, ln2_g, ln2_b):
    assert w_in.shape[0] == DEPTH
    bsz, seq, _ = x_prompt.shape
    dec_b, dec_t, _ = x_sample.shape
    assert dec_t == 1 and seq % TILE_Q == 0 and dec_b % SAMPLE_TB == 0 and 2 * dec_b + 7 * N_EXPERTS <= AREA_ROWS
    assert (bsz * seq) % COMBINE_TILE == 0
    n_prompt = bsz * seq
    tiles = seq // TILE_Q
    spt = TILE_Q // PROMPT_SCOPE
    p_areas = n_prompt // PROMPT_SCOPE
    n_areas = p_areas + spt
    q_max = (n_areas * AREA_CHUNKS + N_EXPERTS * (BLOCK_CHUNKS - 1) + BLOCK_CHUNKS - 1) // BLOCK_CHUNKS * BLOCK_CHUNKS

    row = lambda a: a.reshape(1, -1)
    cparams = lambda sem: pltpu.CompilerParams(dimension_semantics=sem, vmem_limit_bytes=VMEM_LIMIT)

    qk = ATTN_DIM + KV_DIM
    wrows = D_MODEL // PREP_STEPS
    wspec = lambda cols: pl.BlockSpec((wrows, cols), lambda t: (t, 0))
    w_in_b, w_sw_b, w_glu_b, w_out_b, b_sw, b_glu = pl.pallas_call(
        _prep_weights_kernel,
        grid=(PREP_STEPS,),
        in_specs=[wspec(IN_DIM), _full((1, IN_DIM)), wspec(D_MODEL)],
        out_specs=[wspec(IN_DIM), wspec(qk), wspec(2 * CONV_CH), wspec(D_MODEL),
                   _full((1, qk)), _full((1, 2 * CONV_CH))],
        out_shape=[jax.ShapeDtypeStruct((D_MODEL, IN_DIM), BF16), jax.ShapeDtypeStruct((D_MODEL, qk), BF16),
                   jax.ShapeDtypeStruct((D_MODEL, 2 * CONV_CH), BF16), jax.ShapeDtypeStruct((D_MODEL, D_MODEL), BF16),
                   jax.ShapeDtypeStruct((1, qk), F32), jax.ShapeDtypeStruct((1, 2 * CONV_CH), F32)],
        compiler_params=cparams(("arbitrary",)),
        name="prep_weights",
    )(w_in[0], row(b_in[0]), w_out[0])
    w_r = jnp.concatenate([w_router_group[0], w_router_expert[0].reshape(D_MODEL, N_EXPERTS),
                           jnp.zeros((D_MODEL, LANES - ELANE0 - N_EXPERTS), F32)], axis=1).astype(BF16)
    shared = (w_in_b, row(b_in[0]), w_sw_b, b_sw, conv_w[0], row(conv_b[0]),
              row(ln_conv_g[0]), row(ln_conv_b[0]), w_out_b, row(b_out[0]), row(ln1_g[0]), row(ln1_b[0]), w_r)
    shared_specs = [_full(a.shape) for a in shared]
    glu = (w_glu_b, b_glu)
    glu_specs = [_full(a.shape) for a in glu]
    assert len(glu) + len(shared) == N_PROMPT_WEIGHTS
    smem = pl.BlockSpec(memory_space=pltpu.SMEM)
    rope_p = _rope_tables(np.arange(seq))
    rope_s = _rope_tables(PAST_LEN + np.arange(dec_t))

    to_window_minor = lambda a: jnp.transpose(a[0], (0, 2, 3, 1)).reshape(dec_b, KV_DIM, WINDOW)
    from_window_minor = lambda a: jnp.transpose(a.reshape(-1, N_KV_HEADS, HEAD_DIM, WINDOW), (0, 3, 1, 2))[None]
    from_tap_major = lambda a: jnp.transpose(a, (1, 0, 2))[None]
    ck = to_window_minor(cache_k)
    cv = to_window_minor(cache_v)
    st = jnp.transpose(state_conv[0], (1, 0, 2))
    tb = SAMPLE_TB
    cache_spec = pl.BlockSpec((tb, KV_DIM, WINDOW), lambda t: (t, 0, 0))
    conv_state_spec = pl.BlockSpec((CONV_WIDTH - 1, tb, CONV_CH), lambda t: (0, t, 0))
    h_s, nk_s, nv_s, nc_s, xs_s, route_s, cnt_s = pl.pallas_call(
        _mix_sample_kernel,
        grid=(dec_b // tb,),
        in_specs=[smem,
                  pl.BlockSpec((tb, None, D_MODEL), lambda t: (t, 0, 0)),
                  _full(rope_s.shape),
                  cache_spec, cache_spec, conv_state_spec] + shared_specs,
        out_specs=[pl.BlockSpec((tb, D_MODEL), lambda t: (t, 0)),
                   cache_spec, cache_spec, conv_state_spec,
                   _full((AREA_ROWS, HALF)),
                   _full((dec_b, LANES)),
                   _full((1, SUBLANES, LANES))],
        out_shape=[jax.ShapeDtypeStruct((dec_b, D_MODEL), F32),
                   jax.ShapeDtypeStruct((dec_b, KV_DIM, WINDOW), F32),
                   jax.ShapeDtypeStruct((dec_b, KV_DIM, WINDOW), F32),
                   jax.ShapeDtypeStruct((CONV_WIDTH - 1, dec_b, CONV_CH), F32),
                   jax.ShapeDtypeStruct((AREA_ROWS, HALF), U32),
                   jax.ShapeDtypeStruct((dec_b, LANES), F32),
                   jax.ShapeDtypeStruct((1, SUBLANES, LANES), F32)],
        scratch_shapes=[pltpu.VMEM((dec_b, D_MODEL), F32)],
        compiler_params=cparams(("arbitrary",)),
        name="mix_sample",
    )(sinks[0], x_sample, rope_s, ck, cv, st, *shared)

    n_real = bsz * tiles
    real = lambda t: jnp.minimum(t, n_real - 1)
    h_p, xs, route_p, cnt_p, nk_p, nv_p, nc_p = pl.pallas_call(
        functools.partial(_mix_prompt_kernel, tiles=tiles),
        grid=(n_real + 1,),
        in_specs=[smem,
                  pl.BlockSpec((1, TILE_Q, D_MODEL), lambda t: (real(t) // tiles, real(t) % tiles, 0)),
                  pl.BlockSpec((TILE_Q, 4 * LANES), lambda t: (real(t) % tiles, 0)),
                  _full((AREA_ROWS, HALF))] + glu_specs + shared_specs,
        out_specs=[pl.BlockSpec((1, TILE_Q, D_MODEL), lambda t: (real(t) // tiles, real(t) % tiles, 0)),
                   pl.BlockSpec((spt * AREA_ROWS, HALF), lambda t: (t, 0)),
                   pl.BlockSpec((TILE_Q, LANES), lambda t: (real(t), 0)),
                   pl.BlockSpec((spt, SUBLANES, LANES), lambda t: (t, 0, 0)),
                   pl.BlockSpec((1, KV_DIM, WINDOW), lambda t: (real(t) // tiles, 0, 0)),
                   pl.BlockSpec((1, KV_DIM, WINDOW), lambda t: (real(t) // tiles, 0, 0)),
                   pl.BlockSpec((1, CONV_WIDTH - 1, CONV_CH), lambda t: (real(t) // tiles, 0, 0))],
        out_shape=[jax.ShapeDtypeStruct((bsz, seq, D_MODEL), F32),
                   jax.ShapeDtypeStruct((n_areas * AREA_ROWS, HALF), U32),
                   jax.ShapeDtypeStruct((n_prompt, LANES), F32),
                   jax.ShapeDtypeStruct((n_areas, SUBLANES, LANES), F32),
                   jax.ShapeDtypeStruct((bsz, KV_DIM, WINDOW), F32),
                   jax.ShapeDtypeStruct((bsz, KV_DIM, WINDOW), F32),
                   jax.ShapeDtypeStruct((bsz, CONV_WIDTH - 1, CONV_CH), F32)],
        scratch_shapes=[pltpu.VMEM((WINDOW, KV_DIM), F32), pltpu.VMEM((WINDOW, KV_DIM), F32),
                        pltpu.VMEM((TILE_Q + CONV_PRE, CONV_CH), F32),
                        pltpu.VMEM((SUBLANES - 1, TILE_Q + CONV_SHIFT_EXTRA, CONV_CH), F32),
                        pltpu.VMEM((TILE_Q, D_MODEL), BF16), pltpu.VMEM((TILE_Q, D_MODEL), BF16)],
        compiler_params=cparams(("arbitrary",)),
        name="mix_prompt",
    )(sinks[0], x_prompt, rope_p, xs_s, *glu, *shared)

    cnt = jnp.concatenate([cnt_p[:p_areas, 0], cnt_s[:, 0], cnt_p[p_areas + 1:, 0]], axis=0)
    cnt = cnt[:, ELANE0:ELANE0 + N_EXPERTS].astype(jnp.int32)
    plan = _plan_blocks(cnt, q_max)
    hbm = pl.BlockSpec(memory_space=pl.ANY)
    up_shape, down_shape = (2, D_MODEL, D_FF_EXPERT), (2, D_FF_EXPERT, D_MODEL)
    ys = pl.pallas_call(
        _moe_ffn_kernel,
        grid_spec=pltpu.PrefetchScalarGridSpec(
            num_scalar_prefetch=len(plan),
            grid=(1,),
            in_specs=[hbm, hbm, hbm, hbm],
            out_specs=hbm,
            scratch_shapes=[pltpu.VMEM((GATHER_BUFS, BLOCK_ROWS, HALF), U32),
                            pltpu.VMEM((SCATTER_BUFS, BLOCK_ROWS, HALF), U32),
                            pltpu.VMEM((SCATTER_BUFS, BLOCK_ROWS, D_FF_EXPERT), BF16),
                            pltpu.VMEM((ZERO_ROWS, HALF), U32),
                            pltpu.VMEM(up_shape, F32), pltpu.VMEM(up_shape, F32), pltpu.VMEM(down_shape, F32),
                            pltpu.VMEM(up_shape, BF16), pltpu.VMEM(up_shape, BF16), pltpu.VMEM(down_shape, BF16),
                            pltpu.SemaphoreType.DMA((GATHER_BUFS,)), pltpu.SemaphoreType.DMA((SCATTER_BUFS,)),
                            pltpu.SemaphoreType.DMA(()), pltpu.SemaphoreType.DMA(()),
                            pltpu.SemaphoreType.DMA((2,))]),
        out_shape=jax.ShapeDtypeStruct((n_areas * AREA_ROWS + DUMP_CHUNKS * SUBLANES, HALF), U32),
        compiler_params=cparams(("arbitrary",)),
        name="moe_ffn",
    )(*plan, xs, w1[0], w3[0], w2[0])

    def combine(h2d, route, first_area, scope, tile):
        n = h2d.shape[0]
        apt = tile // scope
        return pl.pallas_call(
            _moe_combine_kernel,
            grid=(n // tile,),
            in_specs=[pl.BlockSpec((tile, D_MODEL), lambda t: (t, 0)),
                      pl.BlockSpec((tile, LANES), lambda t: (t, 0)),
                      pl.BlockSpec((apt * AREA_ROWS, HALF), lambda t: (first_area // apt + t, 0)),
                      _full((1, D_MODEL)), _full((1, D_MODEL))],
            out_specs=pl.BlockSpec((tile, D_MODEL), lambda t: (t, 0)),
            out_shape=jax.ShapeDtypeStruct((n, D_MODEL), F32),
            compiler_params=cparams(("arbitrary",)),
            name="moe_combine",
        )(h2d, route, ys, row(ln2_g[0]), row(ln2_b[0]))

    y_p = combine(h_p.reshape(n_prompt, D_MODEL), route_p, 0, PROMPT_SCOPE, COMBINE_TILE)
    y_s = combine(h_s, route_s, p_areas, dec_b, dec_b)

    return (y_p.reshape(bsz, seq, D_MODEL), y_s.reshape(dec_b, dec_t, D_MODEL),
            from_window_minor(nk_p), from_window_minor(nv_p), nc_p[None],
            from_window_minor(nk_s), from_window_minor(nv_s), from_tap_major(nc_s))
```
